```python
import jax
import jax.numpy as jnp
from jax import lax
import numpy as np

D_MODEL = 1024
BATCH = 16
SEQ = 256
DEPTH = 4
DEC_BATCH = 4
DEC_SEQ = 1024
PAST_LEN = 256

GRID_W = 64
N_EVEN = (DEPTH + 1) // 2
N_ODD = DEPTH // 2
EPS = 1e-6
GLA_HEADS = 4
GLA_DK = D_MODEL // 16
GLA_DV = D_MODEL // 8
GLA_RANK = 16
GLA_CHUNK = 64
GLA_GATE_TAU = 16.0
GMLP_GROUPS = 4
GMLP_DIM = D_MODEL // 8
GMLP_CHUNK = 128
ATT_HEADS = 8
ATT_KV_HEADS = 2
HEAD_DIM = D_MODEL // ATT_HEADS
Q_BLOCK = 128
ROPE_THETA = 10000.0
MOE_GROUPS = 4
MOE_PER_GROUP = 8
N_EXPERTS = MOE_GROUPS * MOE_PER_GROUP
MOE_TOPK = 2
D_EXPERT = D_MODEL // 4

GLA_QK_W = GLA_HEADS * GLA_DK
GLA_V_W = GLA_HEADS * GLA_DV
GMLP_W = GMLP_GROUPS * GMLP_DIM
EVEN_SPLITS = (GLA_QK_W, 2 * GLA_QK_W, 2 * GLA_QK_W + GLA_V_W, 2 * GLA_QK_W + 2 * GLA_V_W,
               2 * GLA_QK_W + 2 * GLA_V_W + GLA_RANK, 2 * GLA_QK_W + 2 * GLA_V_W + 2 * GLA_RANK,
               2 * GLA_QK_W + 2 * GLA_V_W + 2 * GLA_RANK + GMLP_W)
EVEN_IN = 2 * GLA_QK_W + 2 * GLA_V_W + 2 * GLA_RANK + 2 * GMLP_W
EVEN_MIX = GLA_V_W + GMLP_W
ATT_Q_W = ATT_HEADS * HEAD_DIM
ATT_KV_W = ATT_KV_HEADS * HEAD_DIM
ODD_IN = ATT_Q_W + 2 * ATT_KV_W

kernel_name = 'hybrid_gla_gmlp_gqa_hmoe_flow_step'

F32 = jnp.float32


def rms_norm(x, g):
    x32 = x.astype(F32)
    y = x32 * lax.rsqrt(jnp.mean(x32 * x32, axis=-1, keepdims=True) + EPS)
    return (y * g.astype(F32)).astype(x.dtype)


def layer_norm(x):
    x32 = x.astype(F32)
    xc = x32 - jnp.mean(x32, axis=-1, keepdims=True)
    return (xc * lax.rsqrt(jnp.mean(xc * xc, axis=-1, keepdims=True) + EPS)).astype(x.dtype)


def adaln(cond, w, b):
    mod = jax.nn.silu(cond) @ w + b
    return [a[:, None, :] for a in jnp.split(mod, 6, axis=-1)]


def modulate(h, shift, scale):
    return h * (1 + scale) + shift


def gla_chunk_scan(q, k, v, log_a, s0):
    B, T, H, DK = q.shape
    DV = v.shape[-1]
    n = T // GLA_CHUNK

    def chunks(a):
        return a.astype(F32).reshape(B, n, GLA_CHUNK, H, a.shape[-1]).transpose(1, 0, 3, 2, 4)

    lower = jnp.tril(jnp.ones((GLA_CHUNK, GLA_CHUNK), bool))[:, :, None]

    def step(S, inp):
        qi, ki, vi, ai = inp
        b = jnp.cumsum(ai, axis=2)
        diff = jnp.where(lower, b[:, :, :, None, :] - b[:, :, None, :, :], -jnp.inf)
        att = jnp.einsum('bhtd,bhsd,bhtsd->bhts', qi, ki, jnp.exp(diff))
        o = jnp.einsum('bhts,bhsv->bhtv', att, vi) + jnp.einsum('bhtd,bhdv->bhtv', qi * jnp.exp(b), S)
        b_end = b[:, :, -1, :]
        S = jnp.exp(b_end)[..., None] * S + jnp.einsum(
            'bhsd,bhsv->bhdv', ki * jnp.exp(b_end[:, :, None, :] - b), vi)
        return S, o

    S, o = lax.scan(step, s0.astype(F32), (chunks(q), chunks(k), chunks(v), chunks(log_a)))
    return o.transpose(1, 0, 3, 2, 4).reshape(B, T, H, DV), S


def even_mixer(h, s_f, s_b, w_in, w_gate_up, b_gate_up, gla_g, w_s, b_s, w_out):
    B, T, _ = h.shape
    q, k, v, g, a_f, a_b, u, vg = jnp.split(h @ w_in, EVEN_SPLITS, axis=-1)
    q = q.reshape(B, T, GLA_HEADS, GLA_DK) * GLA_DK ** -0.5
    k = k.reshape(B, T, GLA_HEADS, GLA_DK)
    v = v.reshape(B, T, GLA_HEADS, GLA_DV)

    def log_decay(a, d):
        z = (a @ w_gate_up[d] + b_gate_up[d]).astype(F32)
        return (jax.nn.log_sigmoid(z) / GLA_GATE_TAU).reshape(B, T, GLA_HEADS, GLA_DK)

    def rev(t):
        return t[:, ::-1]

    o_f, st_f = gla_chunk_scan(q, k, v, log_decay(a_f, 0), s_f)
    o_b, st_b = gla_chunk_scan(rev(q), rev(k), rev(v), rev(log_decay(a_b, 1)), s_b)
    o = rms_norm(o_f + rev(o_b), gla_g) * jax.nn.silu(g.reshape(B, T, GLA_HEADS, GLA_DV).astype(F32))
    o = o.reshape(B, T, GLA_V_W).astype(h.dtype)
    n = T // GMLP_CHUNK
    vg = layer_norm(jax.nn.gelu(vg).reshape(B, n, GMLP_CHUNK, GMLP_GROUPS, GMLP_DIM))
    sg = jnp.einsum('gts,bnsgc->bntgc', w_s, vg) + b_s.T[None, None, :, :, None]
    ob = (jax.nn.gelu(u) * sg.reshape(B, T, GMLP_W)).astype(h.dtype)
    y = jnp.concatenate([o, ob], axis=-1) @ w_out
    return y, st_f, st_b


def attn_qkv(h, w_in, gq, gk):
    B, T, _ = h.shape
    q, k, v = jnp.split(h @ w_in, (ATT_Q_W, ATT_Q_W + ATT_KV_W), axis=-1)
    q = rms_norm(q.reshape(B, T, ATT_HEADS, HEAD_DIM), gq)
    k = rms_norm(k.reshape(B, T, ATT_KV_HEADS, HEAD_DIM), gk)
    return q, k, v.reshape(B, T, ATT_KV_HEADS, HEAD_DIM)


def axial_rope(T):
    rows = T // GRID_W
    row = jnp.repeat(jnp.arange(rows), GRID_W).astype(F32)
    col = jnp.tile(jnp.arange(GRID_W), rows).astype(F32)
    n_freq = HEAD_DIM // 4
    inv = ROPE_THETA ** (-jnp.arange(n_freq, dtype=F32) / n_freq)
    ang = jnp.concatenate([row[:, None] * inv, col[:, None] * inv], axis=-1)
    return jnp.cos(ang)[None, :, None, :], jnp.sin(ang)[None, :, None, :]


def apply_rope(x, cos, sin):
    x32 = x.astype(F32)
    x1, x2 = x32[..., 0::2], x32[..., 1::2]
    out = jnp.stack([x1 * cos - x2 * sin, x1 * sin + x2 * cos], axis=-1).reshape(x.shape)
    return out.astype(x.dtype)


def attend(q, k, v):
    B, Tq, H, HD = q.shape
    G = H // ATT_KV_HEADS
    nb = Tq // Q_BLOCK
    qb = q.reshape(B, nb, Q_BLOCK, ATT_KV_HEADS, G, HD).swapaxes(0, 1)
    k32, v32 = k.astype(F32), v.astype(F32)

    def one_block(qi):
        s = jnp.einsum('bqhgd,bkhd->bhgqk', qi.astype(F32), k32) * HD ** -0.5
        p = jax.nn.softmax(s, axis=-1)
        return jnp.einsum('bhgqk,bkhd->bqhgd', p, v32)

    o = lax.map(one_block, qb)
    return o.swapaxes(0, 1).reshape(B, Tq, H * HD).astype(q.dtype)


def hier_moe(h, w_rg, b_rg, w_re, b_re, w_g, w_u, w_d):
    B, T, D = h.shape
    x = h.reshape(B * T, D)
    g_prob = jax.nn.softmax((x @ w_rg + b_rg).astype(F32), axis=-1)
    g_p, g_i = lax.top_k(g_prob, 1)
    e_logits = (x @ w_re + b_re).astype(F32).reshape(-1, MOE_GROUPS, MOE_PER_GROUP)
    e_sel = jnp.take_along_axis(e_logits, g_i[:, :, None], axis=1)[:, 0]
    e_v, e_i = lax.top_k(e_sel, MOE_TOPK)
    gate = jax.nn.softmax(e_v, axis=-1) * g_p
    expert = g_i * MOE_PER_GROUP + e_i
    combine = jnp.sum(jax.nn.one_hot(expert, N_EXPERTS, dtype=F32) * gate[..., None], axis=1)
    hid = jax.nn.silu(jnp.einsum('nd,edf->nef', x, w_g)) * jnp.einsum('nd,edf->nef', x, w_u)
    hid = hid * combine[:, :, None].astype(hid.dtype)
    y = hid.reshape(B * T, N_EXPERTS * D_EXPERT) @ w_d.reshape(N_EXPERTS * D_EXPERT, D)
    return y.reshape(B, T, D).astype(h.dtype)


def setup_inputs(seed: int = 0) -> dict:
    key = jax.random.key(seed)
    keys = iter(jax.random.split(key, 40))

    def nrm(shape, scale):
        return jax.random.normal(next(keys), shape, F32) * scale

    D = D_MODEL
    return {
        'x_prompt': nrm((BATCH, SEQ, D), 1.0),
        'x_sample': nrm((DEC_BATCH, DEC_SEQ, D), 1.0),
        'state_gla': nrm((DEC_BATCH, N_EVEN, 2, GLA_HEADS, GLA_DK, GLA_DV), 1.0),
        'cache_k': nrm((DEC_BATCH, N_ODD, PAST_LEN, ATT_KV_HEADS, HEAD_DIM), 1.0),
        'cache_v': nrm((DEC_BATCH, N_ODD, PAST_LEN, ATT_KV_HEADS, HEAD_DIM), 1.0),
        'c': nrm((DEC_BATCH, D), 1.0),
        'c_ctx': nrm((D,), 1.0),
        'w_mod': nrm((DEPTH, D, 6 * D), 0.5 * D ** -0.5),
        'b_mod': nrm((DEPTH, 6 * D), 0.02),
        'norm1_g': 1.0 + nrm((DEPTH, D), 0.02),
        'norm2_g': 1.0 + nrm((DEPTH, D), 0.02),
        'w_in_even': nrm((N_EVEN, D, EVEN_IN), D ** -0.5),
        'w_gate_up': nrm((N_EVEN, 2, GLA_RANK, GLA_QK_W), GLA_RANK ** -0.5),
        'b_gate_up': nrm((N_EVEN, 2, GLA_QK_W), 0.02),
        'gla_norm_g': 1.0 + nrm((N_EVEN, GLA_DV), 0.02),
        'w_spatial': nrm((N_EVEN, GMLP_GROUPS, GMLP_CHUNK, GMLP_CHUNK), GMLP_CHUNK ** -0.5),
        'b_spatial': nrm((N_EVEN, GMLP_GROUPS, GMLP_CHUNK), 0.02),
        'w_out_even': nrm((N_EVEN, EVEN_MIX, D), EVEN_MIX ** -0.5),
        'w_in_odd': nrm((N_ODD, D, ODD_IN), D ** -0.5),
        'q_norm_g': 1.0 + nrm((N_ODD, HEAD_DIM), 0.02),
        'k_norm_g': 1.0 + nrm((N_ODD, HEAD_DIM), 0.02),
        'w_out_odd': nrm((N_ODD, ATT_Q_W, D), ATT_Q_W ** -0.5),
        'w_router_group': nrm((DEPTH, D, MOE_GROUPS), D ** -0.5),
        'b_router_group': nrm((DEPTH, MOE_GROUPS), 0.01),
        'w_router_expert': nrm((DEPTH, D, N_EXPERTS), D ** -0.5),
        'b_router_expert': nrm((DEPTH, N_EXPERTS), 0.01),
        'w_exp_gate': nrm((DEPTH, N_EXPERTS, D, D_EXPERT), D ** -0.5),
        'w_exp_up': nrm((DEPTH, N_EXPERTS, D, D_EXPERT), D ** -0.5),
        'w_exp_down': nrm((DEPTH, N_EXPERTS, D_EXPERT, D), D_EXPERT ** -0.5),
        'final_norm_g': 1.0 + nrm((D,), 0.02),
    }


def reference(x_prompt, x_sample, state_gla, cache_k, cache_v, c, c_ctx, w_mod, b_mod, norm1_g, norm2_g,
              w_in_even, w_gate_up, b_gate_up, gla_norm_g, w_spatial, b_spatial, w_out_even,
              w_in_odd, q_norm_g, k_norm_g, w_out_odd, w_router_group, b_router_group,
              w_router_expert, b_router_expert, w_exp_gate, w_exp_up, w_exp_down, final_norm_g):
    def channel_sublayer(x, m, l):
        h = modulate(rms_norm(x, norm2_g[l]), m[3], m[4])
        y = hier_moe(h, w_router_group[l], b_router_group[l], w_router_expert[l], b_router_expert[l],
                     w_exp_gate[l], w_exp_up[l], w_exp_down[l])
        return x + m[5] * y

    def even_params(i):
        return (w_in_even[i], w_gate_up[i], b_gate_up[i], gla_norm_g[i], w_spatial[i], b_spatial[i],
                w_out_even[i])

    xc = x_prompt
    Bc = xc.shape[0]
    gla_states, ctx_k, ctx_v = [], [], []
    for l in range(DEPTH):
        i = l // 2
        m = adaln(c_ctx[None], w_mod[l], b_mod[l])
        h = modulate(rms_norm(xc, norm1_g[l]), m[0], m[1])
        if l % 2 == 0:
            zero = jnp.zeros((Bc, GLA_HEADS, GLA_DK, GLA_DV), F32)
            y, st_f, st_b = even_mixer(h, zero, zero, *even_params(i))
            gla_states.append(jnp.stack([st_f, st_b], axis=1))
        else:
            q, k, v = attn_qkv(h, w_in_odd[i], q_norm_g[i], k_norm_g[i])
            y = attend(q, k, v) @ w_out_odd[i]
            ctx_k.append(k)
            ctx_v.append(v)
        xc = xc + m[2] * y
        xc = channel_sublayer(xc, m, l)
    y_prompt = rms_norm(xc, final_norm_g)
    new_state_gla = jnp.stack(gla_states, axis=1).astype(x_prompt.dtype)
    new_cache_k = jnp.stack(ctx_k, axis=1)
    new_cache_v = jnp.stack(ctx_v, axis=1)

    xs = x_sample
    cos, sin = axial_rope(xs.shape[1])
    for l in range(DEPTH):
        i = l // 2
        m = adaln(c, w_mod[l], b_mod[l])
        h = modulate(rms_norm(xs, norm1_g[l]), m[0], m[1])
        if l % 2 == 0:
            y, _, _ = even_mixer(h, state_gla[:, i, 0], state_gla[:, i, 1], *even_params(i))
        else:
            q, k, v = attn_qkv(h, w_in_odd[i], q_norm_g[i], k_norm_g[i])
            q = apply_rope(q, cos, sin)
            k = apply_rope(k, cos, sin)
            keys = jnp.concatenate([cache_k[:, i].astype(k.dtype), k], axis=1)
            vals = jnp.concatenate([cache_v[:, i].astype(v.dtype), v], axis=1)
            y = attend(q, keys, vals) @ w_out_odd[i]
        xs = xs + m[2] * y
        xs = channel_sublayer(xs, m, l)
    y_sample = rms_norm(xs, final_norm_g)

    return (y_prompt, y_sample, new_state_gla, new_cache_k, new_cache_v)
```

```python
import functools

import jax
import jax.numpy as jnp
import numpy as np
from jax import lax
from jax.experimental import pallas as pl
from jax.experimental.pallas import tpu as pltpu

F32 = jnp.float32
BF16 = jnp.bfloat16

D = 1024
BATCH, SEQ = 16, 256
DEC_BATCH, DEC_SEQ = 4, 1024
N_CTX = BATCH * SEQ
N_LAT = DEC_BATCH * DEC_SEQ
N_TOK = N_CTX + N_LAT
DEPTH = 4
EPS = 1e-6
GRID_W = 64
ROPE_THETA = 10000.0

GLA_HEADS, GLA_DK, GLA_DV, GLA_RANK, GLA_CHUNK, GLA_TAU = 4, 64, 128, 16, 64, 16.0
QK_W = GLA_HEADS * GLA_DK
V_W = GLA_HEADS * GLA_DV
GMLP_GROUPS, GMLP_DIM, GMLP_CHUNK = 4, 128, 128
GMLP_W = GMLP_GROUPS * GMLP_DIM
C_Q, C_K, C_V, C_G, C_U, C_VG, C_A = 0, 256, 512, 1024, 1536, 2048, 2560
EVEN_PACK = 2688

ATT_HEADS, ATT_KV, HD = 8, 2, 128
ATT_G = ATT_HEADS // ATT_KV
Q_W = ATT_HEADS * HD
KV_W = ATT_KV * HD

MOE_GROUPS, MOE_PER_GROUP = 4, 8
N_EXP = MOE_GROUPS * MOE_PER_GROUP
D_EXP = D // 4
NEG = -1e30

VMEM_LIMIT = 56 * 1024 * 1024


def _cp(*sem):
    return pltpu.CompilerParams(dimension_semantics=sem, vmem_limit_bytes=VMEM_LIMIT)


def _dot(a, b):
    return jnp.dot(a.astype(BF16), b.astype(BF16), preferred_element_type=F32)


def _dot_nt(a, b):
    return lax.dot_general(a.astype(BF16), b.astype(BF16), (((1,), (1,)), ((), ())),
                           preferred_element_type=F32)


def _dot_tn(a, b):
    return lax.dot_general(a.astype(BF16), b.astype(BF16), (((0,), (0,)), ((), ())),
                           preferred_element_type=F32)


def _rms(x, g):
    return x * lax.rsqrt(jnp.mean(x * x, axis=-1, keepdims=True) + EPS) * g


def _silu(x):
    return x * jax.nn.sigmoid(x)


def _gelu(x):
    return 0.5 * x * (1.0 + jnp.tanh(np.sqrt(2.0 / np.pi).astype(np.float32) * (x + 0.044715 * (x * x * x))))


def _log_sigmoid(z):
    return jnp.minimum(z, 0.0) - jnp.log(1.0 + jnp.exp(-jnp.abs(z)))


def _resident(shape, index_map):
    return pl.BlockSpec(shape, index_map, pipeline_mode=pl.Buffered(1))


def _mod_kernel(cond_ref, w_ref, b_ref, o_ref):
    c = cond_ref[...]
    o_ref[...] = jnp.dot(_silu(c), w_ref[...], precision=lax.Precision.HIGHEST,
                         preferred_element_type=F32) + b_ref[...]


def _modulation(cond8, w_mod, b_mod):
    tn = 1024
    out = pl.pallas_call(
        _mod_kernel,
        grid=(DEPTH, 6 * D // tn),
        in_specs=[
            pl.BlockSpec((8, D), lambda l, j: (0, 0)),
            pl.BlockSpec((None, D, tn), lambda l, j: (l, 0, j)),
            pl.BlockSpec((None, 1, tn), lambda l, j: (l, 0, j)),
        ],
        out_specs=pl.BlockSpec((None, 8, tn), lambda l, j: (l, 0, j)),
        out_shape=jax.ShapeDtypeStruct((DEPTH, 8, 6 * D), F32),
        compiler_params=_cp("arbitrary", "arbitrary"),
        name="adaln_mod",
    )(cond8, w_mod, b_mod.reshape(DEPTH, 1, 6 * D))
    return out.reshape(DEPTH, 8, 6, D)


def _even_kernel(x_ref, mod_ref, n1g_ref, win_ref, wgu_ref, bgu_ref, glag_ref, ws_ref, bs_ref,
                 wout_ref, s0_ref, xo_ref, st_ref, proj, la, o_f, o_b, st_scr, *, T):
    n_chunks = T // GLA_CHUNK
    shift, scale, gate = mod_ref[0:1, :], mod_ref[1:2, :], mod_ref[2:3, :]
    RB = 128

    def proj_body(r, carry):
        r0 = pl.multiple_of(r * RB, RB)
        h = _rms(x_ref[pl.ds(r0, RB), :], n1g_ref[...]) * (1.0 + scale) + shift
        p = _dot(h, win_ref[...])
        proj[pl.ds(r0, RB), :] = p
        z = _dot(p[:, C_A:C_A + 128], wgu_ref[...]) + bgu_ref[...]
        la[pl.ds(r0, RB), :] = _log_sigmoid(z) * (1.0 / GLA_TAU)
        return carry

    lax.fori_loop(0, T // RB, proj_body, 0)

    st_scr[0] = s0_ref[0].T
    st_scr[1] = s0_ref[1].T

    ci = lax.broadcasted_iota(jnp.int32, (GLA_CHUNK, GLA_CHUNK), 0)
    cj = lax.broadcasted_iota(jnp.int32, (GLA_CHUNK, GLA_CHUNK), 1)
    tri = (jnp.where(ci >= cj, 1.0, 0.0).astype(BF16), jnp.where(ci <= cj, 1.0, 0.0).astype(BF16))
    ai = lax.broadcasted_iota(jnp.int32, (GLA_HEADS * GLA_CHUNK, GLA_CHUNK), 0) % GLA_CHUNK
    aj = lax.broadcasted_iota(jnp.int32, (GLA_HEADS * GLA_CHUNK, GLA_CHUNK), 1)
    amask = (ai >= aj, ai <= aj)
    lane_head = lax.broadcasted_iota(jnp.int32, (1, QK_W), 1) // GLA_DK
    hmask = [jnp.where(lane_head == h, 1.0, 0.0) for h in range(GLA_HEADS)]

    def chunk_body(i, carry):
        for d in range(2):
            c = i if d == 0 else n_chunks - 1 - i
            r0 = pl.multiple_of(c * GLA_CHUNK, GLA_CHUNK)
            q = proj[pl.ds(r0, GLA_CHUNK), C_Q:C_Q + QK_W] * (GLA_DK ** -0.5)
            k = proj[pl.ds(r0, GLA_CHUNK), C_K:C_K + QK_W]
            v = proj[pl.ds(r0, GLA_CHUNK), C_V:C_V + V_W]
            lac = la[pl.ds(r0, GLA_CHUNK), d * QK_W:(d + 1) * QK_W]
            hi = lac.astype(BF16)
            lo = (lac - hi.astype(F32)).astype(BF16)
            b = (jnp.dot(tri[d], hi, preferred_element_type=F32)
                 + jnp.dot(tri[d], lo, preferred_element_type=F32))
            bend = b[GLA_CHUNK - 1:GLA_CHUNK, :] if d == 0 else b[0:1, :]
            qe = q * jnp.exp(b)
            ke = k * jnp.exp(-b)
            kd = k * jnp.exp(bend - b)
            st = st_scr[d]
            qstack = jnp.concatenate([qe * hmask[h] for h in range(GLA_HEADS)], axis=0).astype(BF16)
            att = jnp.where(amask[d], _dot_nt(qstack, ke), 0.0)
            inter = _dot_nt(qstack, st)
            outs = []
            for h in range(GLA_HEADS):
                rows = slice(h * GLA_CHUNK, (h + 1) * GLA_CHUNK)
                outs.append(_dot(att[rows], v[:, h * GLA_DV:(h + 1) * GLA_DV]) + inter[rows])
            o = jnp.concatenate(outs, axis=1)
            if d == 0:
                o_f[pl.ds(r0, GLA_CHUNK), :] = o
            else:
                o_b[pl.ds(r0, GLA_CHUNK), :] = o
            vstack = jnp.concatenate([v[:, h * GLA_DV:(h + 1) * GLA_DV] for h in range(GLA_HEADS)], axis=0)
            kstack = jnp.concatenate([kd * hmask[h] for h in range(GLA_HEADS)], axis=0)
            st_scr[d] = st * jnp.exp(bend) + _dot_tn(vstack, kstack)
        return carry

    lax.fori_loop(0, n_chunks, chunk_body, 0)
    st_ref[0] = st_scr[0].T
    st_ref[1] = st_scr[1].T

    def out_body(r, carry):
        r0 = pl.multiple_of(r * RB, RB)
        osum = o_f[pl.ds(r0, RB), :] + o_b[pl.ds(r0, RB), :]
        g = proj[pl.ds(r0, RB), C_G:C_G + V_W]
        u = proj[pl.ds(r0, RB), C_U:C_U + GMLP_W]
        vg = _gelu(proj[pl.ds(r0, RB), C_VG:C_VG + GMLP_W])
        parts = []
        for h in range(GLA_HEADS):
            oh = osum[:, h * GLA_DV:(h + 1) * GLA_DV]
            parts.append(_rms(oh, glag_ref[...]) * _silu(g[:, h * GLA_DV:(h + 1) * GLA_DV]))
        for gi in range(GMLP_GROUPS):
            vc = vg[:, gi * GMLP_DIM:(gi + 1) * GMLP_DIM]
            vc = vc - jnp.mean(vc, axis=-1, keepdims=True)
            vn = vc * lax.rsqrt(jnp.mean(vc * vc, axis=-1, keepdims=True) + EPS)
            sg = _dot(ws_ref[gi], vn) + bs_ref[:, gi:gi + 1]
            parts.append(_gelu(u[:, gi * GMLP_DIM:(gi + 1) * GMLP_DIM]) * sg)
        mix = jnp.concatenate(parts, axis=1)
        y = _dot(mix, wout_ref[...])
        xo_ref[pl.ds(r0, RB), :] = x_ref[pl.ds(r0, RB), :] + gate * y
        return carry

    lax.fori_loop(0, T // RB, out_body, 0)


def _even_mixer(x_all, mod_l, n1g, win, wgu, bgu, glag, ws, bs, wout, s0, *, latent):
    if latent:
        T, nseq, blk0 = DEC_SEQ, DEC_BATCH, N_CTX // DEC_SEQ
        cond = lambda i: 1 + i
        s0_spec = pl.BlockSpec((None, 2, QK_W, GLA_DV), lambda i: (i, 0, 0, 0))
    else:
        T, nseq, blk0 = SEQ, BATCH, 0
        cond = lambda i: 0
        s0_spec = pl.BlockSpec((None, 2, QK_W, GLA_DV), lambda i: (0, 0, 0, 0))
    const2 = lambda i: (0, 0)
    x_new, states = pl.pallas_call(
        functools.partial(_even_kernel, T=T),
        grid=(nseq,),
        in_specs=[
            pl.BlockSpec((T, D), lambda i: (blk0 + i, 0)),
            pl.BlockSpec((None, 6, D), lambda i: (cond(i), 0, 0)),
            _resident((1, D), const2),
            _resident((D, EVEN_PACK), const2),
            _resident((128, 2 * QK_W), const2),
            _resident((1, 2 * QK_W), const2),
            _resident((1, GLA_DV), const2),
            _resident((GMLP_GROUPS, GMLP_CHUNK, GMLP_CHUNK), lambda i: (0, 0, 0)),
            _resident((GMLP_CHUNK, GMLP_GROUPS), const2),
            _resident((D, D), const2),
            s0_spec,
        ],
        out_specs=[
            pl.BlockSpec((T, D), lambda i: (blk0 + i, 0)),
            pl.BlockSpec((None, 2, QK_W, GLA_DV), lambda i: (i, 0, 0, 0)),
        ],
        out_shape=[
            jax.ShapeDtypeStruct((N_TOK, D), F32),
            jax.ShapeDtypeStruct((nseq, 2, QK_W, GLA_DV), F32),
        ],
        scratch_shapes=[
            pltpu.VMEM((T, EVEN_PACK), F32),
            pltpu.VMEM((T, 2 * QK_W), F32),
            pltpu.VMEM((T, V_W), F32),
            pltpu.VMEM((T, V_W), F32),
            pltpu.VMEM((2, GLA_DV, QK_W), F32),
        ],
        input_output_aliases={0: 0},
        compiler_params=_cp("arbitrary"),
        name="even_mixer_latent" if latent else "even_mixer_context",
    )(x_all, mod_l, n1g, win, wgu, bgu, glag, ws, bs, wout, s0)
    return x_new, states


QKV_TB = 512


def _qkv_kernel(x_ref, mod_ref, n1g_ref, win_ref, gq_ref, gk_ref, cos_ref, sin_ref, q_ref, k_ref, v_ref):
    shift, scale = mod_ref[0:1, :], mod_ref[1:2, :]
    h = _rms(x_ref[...], n1g_ref[...]) * (1.0 + scale) + shift
    p = _dot(h, win_ref[...])
    cos, sin = cos_ref[...], sin_ref[...]
    even_lane = lax.broadcasted_iota(jnp.int32, (1, HD), 1) % 2 == 0

    def norm_rope(xh, g):
        xn = _rms(xh, g)
        swapped = jnp.where(even_lane, pltpu.roll(xn, HD - 1, axis=1), pltpu.roll(xn, 1, axis=1))
        return xn * cos + swapped * sin

    for hh in range(ATT_HEADS):
        q_ref[:, hh * HD:(hh + 1) * HD] = norm_rope(p[:, hh * HD:(hh + 1) * HD], gq_ref[...]).astype(BF16)
    for hh in range(ATT_KV):
        k_ref[:, hh * HD:(hh + 1) * HD] = norm_rope(p[:, Q_W + hh * HD:Q_W + (hh + 1) * HD], gk_ref[...])
    v_ref[...] = p[:, Q_W + KV_W:]


def _qkv(x_all, mod_l, n1g, win, gq, gk, cos_tab, sin_tab):
    nb_ctx = N_CTX // QKV_TB
    per_seq = DEC_SEQ // QKV_TB
    cond = lambda i: jnp.where(i < nb_ctx, 0, 1 + (i - nb_ctx) // per_seq)
    tab = lambda i: jnp.where(i < nb_ctx, 0, 1 + (i - nb_ctx) % per_seq)
    const2 = lambda i: (0, 0)
    return pl.pallas_call(
        _qkv_kernel,
        grid=(N_TOK // QKV_TB,),
        in_specs=[
            pl.BlockSpec((QKV_TB, D), lambda i: (i, 0)),
            pl.BlockSpec((None, 6, D), lambda i: (cond(i), 0, 0)),
            _resident((1, D), const2),
            _resident((D, Q_W + 2 * KV_W), const2),
            _resident((1, HD), const2),
            _resident((1, HD), const2),
            pl.BlockSpec((None, QKV_TB, HD), lambda i: (tab(i), 0, 0)),
            pl.BlockSpec((None, QKV_TB, HD), lambda i: (tab(i), 0, 0)),
        ],
        out_specs=[
            pl.BlockSpec((QKV_TB, Q_W), lambda i: (i, 0)),
            pl.BlockSpec((QKV_TB, KV_W), lambda i: (i, 0)),
            pl.BlockSpec((QKV_TB, KV_W), lambda i: (i, 0)),
        ],
        out_shape=[
            jax.ShapeDtypeStruct((N_TOK, Q_W), BF16),
            jax.ShapeDtypeStruct((N_TOK, KV_W), F32),
            jax.ShapeDtypeStruct((N_TOK, KV_W), F32),
        ],
        compiler_params=_cp("arbitrary"),
        name="odd_qkv",
    )(x_all, mod_l, n1g, win, gq, gk, cos_tab, sin_tab)


ATT_TQ = 256


def _attn_kernel(*refs, n_kv):
    q_ref = refs[0]
    kv_refs = refs[1:1 + 2 * n_kv]
    x_ref, mod_ref, wout_ref, xo_ref, att_scr = refs[1 + 2 * n_kv:]
    gate = mod_ref[2:3, :]
    for kh in range(ATT_KV):
        ks = [kv_refs[2 * s][:, kh * HD:(kh + 1) * HD].astype(BF16) for s in range(n_kv)]
        vs = [kv_refs[2 * s + 1][:, kh * HD:(kh + 1) * HD].astype(BF16) for s in range(n_kv)]
        for g in range(ATT_G):
            hh = kh * ATT_G + g
            qh = q_ref[:, hh * HD:(hh + 1) * HD]
            ss = [_dot_nt(qh, kk) * (HD ** -0.5) for kk in ks]
            m = ss[0].max(axis=-1, keepdims=True)
            for s in ss[1:]:
                m = jnp.maximum(m, s.max(axis=-1, keepdims=True))
            ps = [jnp.exp(s - m) for s in ss]
            den = ps[0].sum(axis=-1, keepdims=True)
            for p in ps[1:]:
                den = den + p.sum(axis=-1, keepdims=True)
            o = _dot(ps[0], vs[0])
            for p, vv in zip(ps[1:], vs[1:]):
                o = o + _dot(p, vv)
            att_scr[:, hh * HD:(hh + 1) * HD] = o / den
    y = _dot(att_scr[...], wout_ref[...])
    xo_ref[...] = x_ref[...] + gate * y


def _attention(x_all, mod_l, q, k, v, wout, cache_k=None, cache_v=None, layer_i=0):
    latent = cache_k is not None
    const2 = lambda *a: (0, 0)
    if latent:
        nq = DEC_SEQ // ATT_TQ
        row_blk = lambda b, j: (N_CTX // ATT_TQ + b * nq + j, 0)
        grid = (DEC_BATCH, nq)
        kv_specs = [
            pl.BlockSpec((None, None, SEQ, KV_W), lambda b, j: (b, layer_i, 0, 0)),
            pl.BlockSpec((None, None, SEQ, KV_W), lambda b, j: (b, layer_i, 0, 0)),
            pl.BlockSpec((DEC_SEQ, KV_W), lambda b, j: (N_CTX // DEC_SEQ + b, 0)),
            pl.BlockSpec((DEC_SEQ, KV_W), lambda b, j: (N_CTX // DEC_SEQ + b, 0)),
        ]
        kv_args = (cache_k, cache_v, k, v)
        mod_spec = pl.BlockSpec((None, 6, D), lambda b, j: (1 + b, 0, 0))
        sem = ("arbitrary", "arbitrary")
        n_kv = 2
    else:
        row_blk = lambda i: (i, 0)
        grid = (BATCH,)
        kv_specs = [pl.BlockSpec((SEQ, KV_W), row_blk), pl.BlockSpec((SEQ, KV_W), row_blk)]
        kv_args = (k, v)
        mod_spec = pl.BlockSpec((None, 6, D), lambda i: (0, 0, 0))
        sem = ("arbitrary",)
        n_kv = 1
    n_in = 1 + len(kv_args)
    return pl.pallas_call(
        functools.partial(_attn_kernel, n_kv=n_kv),
        grid=grid,
        in_specs=[pl.BlockSpec((ATT_TQ, Q_W), row_blk)] + kv_specs + [
            pl.BlockSpec((ATT_TQ, D), row_blk),
            mod_spec,
            _resident((D, D), const2),
        ],
        out_specs=pl.BlockSpec((ATT_TQ, D), row_blk),
        out_shape=jax.ShapeDtypeStruct((N_TOK, D), F32),
        scratch_shapes=[pltpu.VMEM((ATT_TQ, Q_W), F32)],
        input_output_aliases={n_in: 0},
        compiler_params=_cp(*sem),
        name="attention_latent" if latent else "attention_context",
    )(q, *kv_args, x_all, mod_l, wout)


ROUTE_TB = 512


def _router_kernel(x_ref, mod_ref, n2g_ref, wr_ref, br_ref, h_ref, comb_ref):
    shift, scale = mod_ref[3:4, :], mod_ref[4:5, :]
    h = _rms(x_ref[...], n2g_ref[...]) * (1.0 + scale) + shift
    h_ref[...] = h.astype(BF16)
    logits = jnp.dot(h, wr_ref[...], precision=lax.Precision.HIGHEST, preferred_element_type=F32) + br_ref[...]
    lane = lax.broadcasted_iota(jnp.int32, logits.shape, 1).astype(F32)
    big = 1e4

    def first_argmax(vals):
        m = vals.max(axis=-1, keepdims=True)
        return m, jnp.where(vals == m, lane, big).min(axis=-1, keepdims=True)

    gl = jnp.where((lane >= N_EXP) & (lane < N_EXP + MOE_GROUPS), logits, NEG)
    gmax, glane = first_argmax(gl)
    g_p = 1.0 / jnp.exp(gl - gmax).sum(axis=-1, keepdims=True)
    lo = (glane - N_EXP) * MOE_PER_GROUP
    el = jnp.where((lane >= lo) & (lane < lo + MOE_PER_GROUP), logits, NEG)
    m1, i1 = first_argmax(el)
    m2, i2 = first_argmax(jnp.where(lane == i1, NEG, el))
    t = jnp.exp(m2 - m1)
    w1 = 1.0 / (1.0 + t)
    comb_ref[...] = jnp.where(lane == i1, w1 * g_p, 0.0) + jnp.where(lane == i2, (t * w1) * g_p, 0.0)


def _router(x_all, mod_l, n2g, wr, br):
    nb_ctx = N_CTX // ROUTE_TB
    per_seq = DEC_SEQ // ROUTE_TB
    cond = lambda i: jnp.where(i < nb_ctx, 0, 1 + (i - nb_ctx) // per_seq)
    const2 = lambda i: (0, 0)
    return pl.pallas_call(
        _router_kernel,
        grid=(N_TOK // ROUTE_TB,),
        in_specs=[
            pl.BlockSpec((ROUTE_TB, D), lambda i: (i, 0)),
            pl.BlockSpec((None, 6, D), lambda i: (cond(i), 0, 0)),
            _resident((1, D), const2),
            _resident((D, 128), const2),
            _resident((1, 128), const2),
        ],
        out_specs=[
            pl.BlockSpec((ROUTE_TB, D), lambda i: (i, 0)),
            pl.BlockSpec((ROUTE_TB, 128), lambda i: (i, 0)),
        ],
        out_shape=[
            jax.ShapeDtypeStruct((N_TOK, D), BF16),
            jax.ShapeDtypeStruct((N_TOK, 128), F32),
        ],
        compiler_params=_cp("arbitrary"),
        name="moe_router",
    )(x_all, mod_l, n2g, wr, br)


EXP_TB = 1024


def _experts_kernel(h_ref, comb_ref, wg_ref, wu_ref, wd_ref, x_ref, mod_ref, xo_ref, acc):
    e = pl.program_id(1)

    @pl.when(e == 0)
    def _():
        acc[...] = jnp.zeros_like(acc)

    h = h_ref[...]
    lane = lax.broadcasted_iota(jnp.int32, comb_ref.shape, 1)
    cw = jnp.where(lane == e, comb_ref[...], 0.0).sum(axis=-1, keepdims=True)
    hid = _silu(_dot(h, wg_ref[...])) * _dot(h, wu_ref[...]) * cw
    acc[...] += _dot(hid, wd_ref[...])

    @pl.when(e == N_EXP - 1)
    def _():
        xo_ref[...] = x_ref[...] + mod_ref[5:6, :] * acc[...]


def _experts(x_all, mod_l, h, comb, wg, wu, wd, layer):
    nb_ctx = N_CTX // EXP_TB
    cond = lambda i: jnp.where(i < nb_ctx, 0, 1 + (i - nb_ctx))
    return pl.pallas_call(
        _experts_kernel,
        grid=(N_TOK // EXP_TB, N_EXP),
        in_specs=[
            pl.BlockSpec((EXP_TB, D), lambda i, e: (i, 0)),
            pl.BlockSpec((EXP_TB, 128), lambda i, e: (i, 0)),
            pl.BlockSpec((None, None, D, D_EXP), lambda i, e: (layer, e, 0, 0)),
            pl.BlockSpec((None, None, D, D_EXP), lambda i, e: (layer, e, 0, 0)),
            pl.BlockSpec((None, None, D_EXP, D), lambda i, e: (layer, e, 0, 0)),
            pl.BlockSpec((EXP_TB, D), lambda i, e: (i, 0)),
            pl.BlockSpec((None, 6, D), lambda i, e: (cond(i), 0, 0)),
        ],
        out_specs=pl.BlockSpec((EXP_TB, D), lambda i, e: (i, 0)),
        out_shape=jax.ShapeDtypeStruct((N_TOK, D), F32),
        scratch_shapes=[pltpu.VMEM((EXP_TB, D), F32)],
        compiler_params=_cp("arbitrary", "arbitrary"),
        name="moe_experts",
    )(h, comb, wg, wu, wd, x_all, mod_l)


def _final_kernel(x_ref, g_ref, o_ref):
    o_ref[...] = _rms(x_ref[...], g_ref[...])


def _final_norm(x_all, g, blk0, n_rows):
    tb = 512
    return pl.pallas_call(
        _final_kernel,
        grid=(n_rows // tb,),
        in_specs=[pl.BlockSpec((tb, D), lambda i: (blk0 + i, 0)), _resident((1, D), lambda i: (0, 0))],
        out_specs=pl.BlockSpec((tb, D), lambda i: (i, 0)),
        out_shape=jax.ShapeDtypeStruct((n_rows, D), F32),
        compiler_params=_cp("arbitrary"),
        name="final_norm",
    )(x_all, g)


def _rope_tables():
    pos = jnp.arange(DEC_SEQ)
    row = (pos // GRID_W).astype(F32)
    col = (pos % GRID_W).astype(F32)
    n_freq = HD // 4
    inv = ROPE_THETA ** (-jnp.arange(n_freq, dtype=F32) / n_freq)
    ang = jnp.concatenate([row[:, None] * inv, col[:, None] * inv], axis=-1)
    cos = jnp.repeat(jnp.cos(ang), 2, axis=-1)
    sin = jnp.repeat(jnp.sin(ang), 2, axis=-1) * jnp.tile(jnp.array([-1.0, 1.0], F32), HD // 2)
    nblk = DEC_SEQ // QKV_TB
    cos_tab = jnp.concatenate([jnp.ones((1, QKV_TB, HD), F32), cos.reshape(nblk, QKV_TB, HD)], axis=0)
    sin_tab = jnp.concatenate([jnp.zeros((1, QKV_TB, HD), F32), sin.reshape(nblk, QKV_TB, HD)], axis=0)
    return cos_tab, sin_tab


def kernel(x_prompt, x_sample, state_gla, cache_k, cache_v, c, c_ctx, w_mod, b_mod, norm1_g, norm2_g,
           w_in_even, w_gate_up, b_gate_up, gla_norm_g, w_spatial, b_spatial, w_out_even,
           w_in_odd, q_norm_g, k_norm_g, w_out_odd, w_router_group, b_router_group,
           w_router_expert, b_router_expert, w_exp_gate, w_exp_up, w_exp_down, final_norm_g):
    x_all = jnp.concatenate([x_prompt.reshape(N_CTX, D), x_sample.reshape(N_LAT, D)], axis=0)
    cond8 = jnp.concatenate([c_ctx[None], c, jnp.zeros((3, D), F32)], axis=0)
    mod = _modulation(cond8, w_mod, b_mod)
    cos_tab, sin_tab = _rope_tables()
    zero_state = jnp.zeros((1, 2, QK_W, GLA_DV), F32)
    state_in = state_gla.reshape(DEC_BATCH, -1, 2, QK_W, GLA_DV)
    cache_k2 = cache_k.reshape(DEC_BATCH, -1, SEQ, KV_W)
    cache_v2 = cache_v.reshape(DEC_BATCH, -1, SEQ, KV_W)

    gla_states, ctx_k, ctx_v = [], [], []
    for l in range(DEPTH):
        i = l // 2
        n1g = norm1_g[l][None]
        if l % 2 == 0:
            w = w_in_even[i]
            win = jnp.concatenate([w[:, :1536], w[:, 1568:], w[:, 1536:1568], jnp.zeros((D, 96), F32)],
                                  axis=1).astype(BF16)
            wgu = jnp.zeros((128, 2 * QK_W), F32)
            wgu = wgu.at[0:GLA_RANK, 0:QK_W].set(w_gate_up[i, 0])
            wgu = wgu.at[GLA_RANK:2 * GLA_RANK, QK_W:].set(w_gate_up[i, 1]).astype(BF16)
            bgu = b_gate_up[i].reshape(1, 2 * QK_W)
            args = (mod[l], n1g, win, wgu, bgu, gla_norm_g[i][None], w_spatial[i].astype(BF16),
                    b_spatial[i].T, w_out_even[i].astype(BF16))
            x_all, st = _even_mixer(x_all, *args, zero_state, latent=False)
            gla_states.append(st)
            x_all, _ = _even_mixer(x_all, *args, state_in[:, i], latent=True)
        else:
            q, k, v = _qkv(x_all, mod[l], n1g, w_in_odd[i].astype(BF16), q_norm_g[i][None],
                           k_norm_g[i][None], cos_tab, sin_tab)
            ctx_k.append(k[:N_CTX].reshape(BATCH, SEQ, ATT_KV, HD))
            ctx_v.append(v[:N_CTX].reshape(BATCH, SEQ, ATT_KV, HD))
            wout = w_out_odd[i].astype(BF16)
            x_all = _attention(x_all, mod[l], q, k, v, wout)
            x_all = _attention(x_all, mod[l], q, k, v, wout, cache_k2, cache_v2, layer_i=i)
        wr = jnp.concatenate([w_router_expert[l], w_router_group[l],
                              jnp.zeros((D, 128 - N_EXP - MOE_GROUPS), F32)], axis=1)
        br = jnp.concatenate([b_router_expert[l], b_router_group[l],
                              jnp.zeros((128 - N_EXP - MOE_GROUPS,), F32)])[None]
        h, comb = _router(x_all, mod[l], norm2_g[l][None], wr, br)
        x_all = _experts(x_all, mod[l], h, comb, w_exp_gate, w_exp_up, w_exp_down, l)

    fg = final_norm_g[None]
    y_prompt = _final_norm(x_all, fg, 0, N_CTX).reshape(BATCH, SEQ, D)
    y_sample = _final_norm(x_all, fg, N_CTX // 512, N_LAT).reshape(DEC_BATCH, DEC_SEQ, D)
    new_state = jnp.stack(gla_states, axis=1).reshape(BATCH, -1, 2, GLA_HEADS, GLA_DK, GLA_DV)
    return (y_prompt, y_sample, new_state, jnp.stack(ctx_k, axis=1), jnp.stack(ctx_v, axis=1))
```

```python
import functools

import jax
import jax.numpy as jnp
import numpy as np
from jax import lax
from jax.experimental import pallas as pl
from jax.experimental.pallas import tpu as pltpu

F32 = jnp.float32
BF16 = jnp.bfloat16

D = 1024
BATCH, SEQ = 16, 256
DEC_BATCH, DEC_SEQ = 4, 1024
N_CTX = BATCH * SEQ
N_LAT = DEC_BATCH * DEC_SEQ
N_TOK = N_CTX + N_LAT
DEPTH = 4
EPS = 1e-6
GRID_W = 64
ROPE_THETA = 10000.0

GLA_HEADS, GLA_DK, GLA_DV, GLA_RANK, GLA_CHUNK, GLA_TAU = 4, 64, 128, 16, 64, 16.0
QK_W = GLA_HEADS * GLA_DK
V_W = GLA_HEADS * GLA_DV
GMLP_GROUPS, GMLP_DIM, GMLP_CHUNK = 4, 128, 128
GMLP_W = GMLP_GROUPS * GMLP_DIM
C_Q, C_K, C_V, C_G, C_U, C_VG, C_A = 0, 256, 512, 1024, 1536, 2048, 2560
EVEN_PACK = 2688

ATT_HEADS, ATT_KV, HD = 8, 2, 128
ATT_G = ATT_HEADS // ATT_KV
Q_W = ATT_HEADS * HD
KV_W = ATT_KV * HD

MOE_GROUPS, MOE_PER_GROUP = 4, 8
N_EXP = MOE_GROUPS * MOE_PER_GROUP
D_EXP = D // 4
NEG = -1e30

VMEM_LIMIT = 56 * 1024 * 1024


def _cp(*sem):
    return pltpu.CompilerParams(dimension_semantics=sem, vmem_limit_bytes=VMEM_LIMIT)


def _dot(a, b):
    return jnp.dot(a.astype(BF16), b.astype(BF16), preferred_element_type=F32)


def _dot_nt(a, b):
    return lax.dot_general(a.astype(BF16), b.astype(BF16), (((1,), (1,)), ((), ())),
                           preferred_element_type=F32)


def _dot_tn(a, b):
    return lax.dot_general(a.astype(BF16), b.astype(BF16), (((0,), (0,)), ((), ())),
                           preferred_element_type=F32)


def _rms(x, g):
    return x * lax.rsqrt(jnp.mean(x * x, axis=-1, keepdims=True) + EPS) * g


def _silu(x):
    return x * jax.nn.sigmoid(x)


def _gelu(x):
    return 0.5 * x * (1.0 + jnp.tanh(np.sqrt(2.0 / np.pi).astype(np.float32) * (x + 0.044715 * (x * x * x))))


def _log_sigmoid(z):
    return jnp.minimum(z, 0.0) - jnp.log(1.0 + jnp.exp(-jnp.abs(z)))


def _rows_to_tiles(ref, x):
    rows = x.shape[0]
    for j in range(D // 128):
        ref[pl.ds(j, rows, stride=8), :] = x[:, j * 128:(j + 1) * 128]


def _tiles_to_rows(ref, rows):
    return jnp.concatenate([ref[pl.ds(j, rows, stride=8), :] for j in range(D // 128)], axis=1)


def _tile_of(ref, row8):
    return ref.at[pl.ds(pl.multiple_of(row8, 8), 8), :]


def _resident(shape, index_map):
    return pl.BlockSpec(shape, index_map, pipeline_mode=pl.Buffered(1))


def _mod_kernel(cond_ref, w_ref, b_ref, o_ref):
    c = cond_ref[...]
    o_ref[...] = jnp.dot(_silu(c), w_ref[...], precision=lax.Precision.HIGHEST,
                         preferred_element_type=F32) + b_ref[...]


def _modulation(cond8, w_mod, b_mod):
    tn = 1024
    out = pl.pallas_call(
        _mod_kernel,
        grid=(DEPTH, 6 * D // tn),
        in_specs=[
            pl.BlockSpec((8, D), lambda l, j: (0, 0)),
            pl.BlockSpec((None, D, tn), lambda l, j: (l, 0, j)),
            pl.BlockSpec((None, 1, tn), lambda l, j: (l, 0, j)),
        ],
        out_specs=pl.BlockSpec((None, 8, tn), lambda l, j: (l, 0, j)),
        out_shape=jax.ShapeDtypeStruct((DEPTH, 8, 6 * D), F32),
        compiler_params=_cp("arbitrary", "arbitrary"),
        name="adaln_mod",
    )(cond8, w_mod, b_mod.reshape(DEPTH, 1, 6 * D))
    return out.reshape(DEPTH, 8, 6, D)


def _even_kernel(x_ref, mod_ref, n1g_ref, win_ref, wgu_ref, bgu_ref, glag_ref, ws_ref, bs_ref,
                 wout_ref, s0_ref, xo_ref, st_ref, proj, la, o_f, o_b, st_scr, *, T):
    n_chunks = T // GLA_CHUNK
    shift, scale, gate = mod_ref[0:1, :], mod_ref[1:2, :], mod_ref[2:3, :]
    RB = 128

    def proj_body(r, carry):
        r0 = pl.multiple_of(r * RB, RB)
        h = _rms(x_ref[pl.ds(r0, RB), :], n1g_ref[...]) * (1.0 + scale) + shift
        p = _dot(h, win_ref[...])
        proj[pl.ds(r0, RB), :] = p
        z = _dot(p[:, C_A:C_A + 128], wgu_ref[...]) + bgu_ref[...]
        la[pl.ds(r0, RB), :] = _log_sigmoid(z) * (1.0 / GLA_TAU)
        return carry

    lax.fori_loop(0, T // RB, proj_body, 0)

    st_scr[0] = s0_ref[0].T
    st_scr[1] = s0_ref[1].T

    ci = lax.broadcasted_iota(jnp.int32, (GLA_CHUNK, GLA_CHUNK), 0)
    cj = lax.broadcasted_iota(jnp.int32, (GLA_CHUNK, GLA_CHUNK), 1)
    tri = (jnp.where(ci >= cj, 1.0, 0.0).astype(BF16), jnp.where(ci <= cj, 1.0, 0.0).astype(BF16))
    ai = lax.broadcasted_iota(jnp.int32, (GLA_HEADS * GLA_CHUNK, GLA_CHUNK), 0) % GLA_CHUNK
    aj = lax.broadcasted_iota(jnp.int32, (GLA_HEADS * GLA_CHUNK, GLA_CHUNK), 1)
    amask = (ai >= aj, ai <= aj)
    lane_head = lax.broadcasted_iota(jnp.int32, (1, QK_W), 1) // GLA_DK
    hmask = [jnp.where(lane_head == h, 1.0, 0.0) for h in range(GLA_HEADS)]

    def chunk_body(i, carry):
        for d in range(2):
            c = i if d == 0 else n_chunks - 1 - i
            r0 = pl.multiple_of(c * GLA_CHUNK, GLA_CHUNK)
            q = proj[pl.ds(r0, GLA_CHUNK), C_Q:C_Q + QK_W] * (GLA_DK ** -0.5)
            k = proj[pl.ds(r0, GLA_CHUNK), C_K:C_K + QK_W]
            v = proj[pl.ds(r0, GLA_CHUNK), C_V:C_V + V_W]
            lac = la[pl.ds(r0, GLA_CHUNK), d * QK_W:(d + 1) * QK_W]
            hi = lac.astype(BF16)
            lo = (lac - hi.astype(F32)).astype(BF16)
            b = (jnp.dot(tri[d], hi, preferred_element_type=F32)
                 + jnp.dot(tri[d], lo, preferred_element_type=F32))
            bend = b[GLA_CHUNK - 1:GLA_CHUNK, :] if d == 0 else b[0:1, :]
            qe = q * jnp.exp(b)
            ke = k * jnp.exp(-b)
            kd = k * jnp.exp(bend - b)
            st = st_scr[d]
            qstack = jnp.concatenate([qe * hmask[h] for h in range(GLA_HEADS)], axis=0).astype(BF16)
            att = jnp.where(amask[d], _dot_nt(qstack, ke), 0.0)
            inter = _dot_nt(qstack, st)
            outs = []
            for h in range(GLA_HEADS):
                rows = slice(h * GLA_CHUNK, (h + 1) * GLA_CHUNK)
                outs.append(_dot(att[rows], v[:, h * GLA_DV:(h + 1) * GLA_DV]) + inter[rows])
            o = jnp.concatenate(outs, axis=1)
            if d == 0:
                o_f[pl.ds(r0, GLA_CHUNK), :] = o
            else:
                o_b[pl.ds(r0, GLA_CHUNK), :] = o
            vstack = jnp.concatenate([v[:, h * GLA_DV:(h + 1) * GLA_DV] for h in range(GLA_HEADS)], axis=0)
            kstack = jnp.concatenate([kd * hmask[h] for h in range(GLA_HEADS)], axis=0)
            st_scr[d] = st * jnp.exp(bend) + _dot_tn(vstack, kstack)
        return carry

    lax.fori_loop(0, n_chunks, chunk_body, 0)
    st_ref[0] = st_scr[0].T
    st_ref[1] = st_scr[1].T

    def out_body(r, carry):
        r0 = pl.multiple_of(r * RB, RB)
        osum = o_f[pl.ds(r0, RB), :] + o_b[pl.ds(r0, RB), :]
        g = proj[pl.ds(r0, RB), C_G:C_G + V_W]
        u = proj[pl.ds(r0, RB), C_U:C_U + GMLP_W]
        vg = _gelu(proj[pl.ds(r0, RB), C_VG:C_VG + GMLP_W])
        parts = []
        for h in range(GLA_HEADS):
            oh = osum[:, h * GLA_DV:(h + 1) * GLA_DV]
            parts.append(_rms(oh, glag_ref[...]) * _silu(g[:, h * GLA_DV:(h + 1) * GLA_DV]))
        for gi in range(GMLP_GROUPS):
            vc = vg[:, gi * GMLP_DIM:(gi + 1) * GMLP_DIM]
            vc = vc - jnp.mean(vc, axis=-1, keepdims=True)
            vn = vc * lax.rsqrt(jnp.mean(vc * vc, axis=-1, keepdims=True) + EPS)
            sg = _dot(ws_ref[gi], vn) + bs_ref[:, gi:gi + 1]
            parts.append(_gelu(u[:, gi * GMLP_DIM:(gi + 1) * GMLP_DIM]) * sg)
        mix = jnp.concatenate(parts, axis=1)
        y = _dot(mix, wout_ref[...])
        xo_ref[pl.ds(r0, RB), :] = x_ref[pl.ds(r0, RB), :] + gate * y
        return carry

    lax.fori_loop(0, T // RB, out_body, 0)


def _even_mixer(x_all, mod_l, n1g, win, wgu, bgu, glag, ws, bs, wout, s0, *, latent):
    if latent:
        T, nseq, blk0 = DEC_SEQ, DEC_BATCH, N_CTX // DEC_SEQ
        cond = lambda i: 1 + i
        s0_spec = pl.BlockSpec((None, 2, QK_W, GLA_DV), lambda i: (i, 0, 0, 0))
    else:
        T, nseq, blk0 = SEQ, BATCH, 0
        cond = lambda i: 0
        s0_spec = pl.BlockSpec((None, 2, QK_W, GLA_DV), lambda i: (0, 0, 0, 0))
    const2 = lambda i: (0, 0)
    x_new, states = pl.pallas_call(
        functools.partial(_even_kernel, T=T),
        grid=(nseq,),
        in_specs=[
            pl.BlockSpec((T, D), lambda i: (blk0 + i, 0)),
            pl.BlockSpec((None, 6, D), lambda i: (cond(i), 0, 0)),
            _resident((1, D), const2),
            _resident((D, EVEN_PACK), const2),
            _resident((128, 2 * QK_W), const2),
            _resident((1, 2 * QK_W), const2),
            _resident((1, GLA_DV), const2),
            _resident((GMLP_GROUPS, GMLP_CHUNK, GMLP_CHUNK), lambda i: (0, 0, 0)),
            _resident((GMLP_CHUNK, GMLP_GROUPS), const2),
            _resident((D, D), const2),
            s0_spec,
        ],
        out_specs=[
            pl.BlockSpec((T, D), lambda i: (blk0 + i, 0)),
            pl.BlockSpec((None, 2, QK_W, GLA_DV), lambda i: (i, 0, 0, 0)),
        ],
        out_shape=[
            jax.ShapeDtypeStruct((N_TOK, D), F32),
            jax.ShapeDtypeStruct((nseq, 2, QK_W, GLA_DV), F32),
        ],
        scratch_shapes=[
            pltpu.VMEM((T, EVEN_PACK), F32),
            pltpu.VMEM((T, 2 * QK_W), F32),
            pltpu.VMEM((T, V_W), F32),
            pltpu.VMEM((T, V_W), F32),
            pltpu.VMEM((2, GLA_DV, QK_W), F32),
        ],
        input_output_aliases={0: 0},
        compiler_params=_cp("arbitrary"),
        name="even_mixer_latent" if latent else "even_mixer_context",
    )(x_all, mod_l, n1g, win, wgu, bgu, glag, ws, bs, wout, s0)
    return x_new, states


QKV_TB = 512


def _qkv_kernel(x_ref, mod_ref, n1g_ref, win_ref, gq_ref, gk_ref, cos_ref, sin_ref, q_ref, k_ref, v_ref):
    shift, scale = mod_ref[0:1, :], mod_ref[1:2, :]
    h = _rms(x_ref[...], n1g_ref[...]) * (1.0 + scale) + shift
    p = _dot(h, win_ref[...])
    cos, sin = cos_ref[...], sin_ref[...]
    even_lane = lax.broadcasted_iota(jnp.int32, (1, HD), 1) % 2 == 0

    def norm_rope(xh, g):
        xn = _rms(xh, g)
        swapped = jnp.where(even_lane, pltpu.roll(xn, HD - 1, axis=1), pltpu.roll(xn, 1, axis=1))
        return xn * cos + swapped * sin

    for hh in range(ATT_HEADS):
        q_ref[:, hh * HD:(hh + 1) * HD] = norm_rope(p[:, hh * HD:(hh + 1) * HD], gq_ref[...]).astype(BF16)
    for hh in range(ATT_KV):
        k_ref[:, hh * HD:(hh + 1) * HD] = norm_rope(p[:, Q_W + hh * HD:Q_W + (hh + 1) * HD], gk_ref[...])
    v_ref[...] = p[:, Q_W + KV_W:]


def _qkv(x_all, mod_l, n1g, win, gq, gk, cos_tab, sin_tab):
    nb_ctx = N_CTX // QKV_TB
    per_seq = DEC_SEQ // QKV_TB
    cond = lambda i: jnp.where(i < nb_ctx, 0, 1 + (i - nb_ctx) // per_seq)
    tab = lambda i: jnp.where(i < nb_ctx, 0, 1 + (i - nb_ctx) % per_seq)
    const2 = lambda i: (0, 0)
    return pl.pallas_call(
        _qkv_kernel,
        grid=(N_TOK // QKV_TB,),
        in_specs=[
            pl.BlockSpec((QKV_TB, D), lambda i: (i, 0)),
            pl.BlockSpec((None, 6, D), lambda i: (cond(i), 0, 0)),
            _resident((1, D), const2),
            _resident((D, Q_W + 2 * KV_W), const2),
            _resident((1, HD), const2),
            _resident((1, HD), const2),
            pl.BlockSpec((None, QKV_TB, HD), lambda i: (tab(i), 0, 0)),
            pl.BlockSpec((None, QKV_TB, HD), lambda i: (tab(i), 0, 0)),
        ],
        out_specs=[
            pl.BlockSpec((QKV_TB, Q_W), lambda i: (i, 0)),
            pl.BlockSpec((QKV_TB, KV_W), lambda i: (i, 0)),
            pl.BlockSpec((QKV_TB, KV_W), lambda i: (i, 0)),
        ],
        out_shape=[
            jax.ShapeDtypeStruct((N_TOK, Q_W), BF16),
            jax.ShapeDtypeStruct((N_TOK, KV_W), F32),
            jax.ShapeDtypeStruct((N_TOK, KV_W), F32),
        ],
        compiler_params=_cp("arbitrary"),
        name="odd_qkv",
    )(x_all, mod_l, n1g, win, gq, gk, cos_tab, sin_tab)


ATT_TQ = 256


def _attn_kernel(*refs, n_kv):
    q_ref = refs[0]
    kv_refs = refs[1:1 + 2 * n_kv]
    x_ref, mod_ref, wout_ref, xo_ref, att_scr = refs[1 + 2 * n_kv:]
    gate = mod_ref[2:3, :]
    for kh in range(ATT_KV):
        ks = [kv_refs[2 * s][:, kh * HD:(kh + 1) * HD].astype(BF16) for s in range(n_kv)]
        vs = [kv_refs[2 * s + 1][:, kh * HD:(kh + 1) * HD].astype(BF16) for s in range(n_kv)]
        for g in range(ATT_G):
            hh = kh * ATT_G + g
            qh = q_ref[:, hh * HD:(hh + 1) * HD]
            ss = [_dot_nt(qh, kk) * (HD ** -0.5) for kk in ks]
            m = ss[0].max(axis=-1, keepdims=True)
            for s in ss[1:]:
                m = jnp.maximum(m, s.max(axis=-1, keepdims=True))
            ps = [jnp.exp(s - m) for s in ss]
            den = ps[0].sum(axis=-1, keepdims=True)
            for p in ps[1:]:
                den = den + p.sum(axis=-1, keepdims=True)
            o = _dot(ps[0], vs[0])
            for p, vv in zip(ps[1:], vs[1:]):
                o = o + _dot(p, vv)
            att_scr[:, hh * HD:(hh + 1) * HD] = o / den
    y = _dot(att_scr[...], wout_ref[...])
    xo_ref[...] = x_ref[...] + gate * y


def _attention(x_all, mod_l, q, k, v, wout, cache_k=None, cache_v=None, layer_i=0):
    latent = cache_k is not None
    const2 = lambda *a: (0, 0)
    if latent:
        nq = DEC_SEQ // ATT_TQ
        row_blk = lambda b, j: (N_CTX // ATT_TQ + b * nq + j, 0)
        grid = (DEC_BATCH, nq)
        kv_specs = [
            pl.BlockSpec((None, None, SEQ, KV_W), lambda b, j: (b, layer_i, 0, 0)),
            pl.BlockSpec((None, None, SEQ, KV_W), lambda b, j: (b, layer_i, 0, 0)),
            pl.BlockSpec((DEC_SEQ, KV_W), lambda b, j: (N_CTX // DEC_SEQ + b, 0)),
            pl.BlockSpec((DEC_SEQ, KV_W), lambda b, j: (N_CTX // DEC_SEQ + b, 0)),
        ]
        kv_args = (cache_k, cache_v, k, v)
        mod_spec = pl.BlockSpec((None, 6, D), lambda b, j: (1 + b, 0, 0))
        sem = ("arbitrary", "arbitrary")
        n_kv = 2
    else:
        row_blk = lambda i: (i, 0)
        grid = (BATCH,)
        kv_specs = [pl.BlockSpec((SEQ, KV_W), row_blk), pl.BlockSpec((SEQ, KV_W), row_blk)]
        kv_args = (k, v)
        mod_spec = pl.BlockSpec((None, 6, D), lambda i: (0, 0, 0))
        sem = ("arbitrary",)
        n_kv = 1
    n_in = 1 + len(kv_args)
    return pl.pallas_call(
        functools.partial(_attn_kernel, n_kv=n_kv),
        grid=grid,
        in_specs=[pl.BlockSpec((ATT_TQ, Q_W), row_blk)] + kv_specs + [
            pl.BlockSpec((ATT_TQ, D), row_blk),
            mod_spec,
            _resident((D, D), const2),
        ],
        out_specs=pl.BlockSpec((ATT_TQ, D), row_blk),
        out_shape=jax.ShapeDtypeStruct((N_TOK, D), F32),
        scratch_shapes=[pltpu.VMEM((ATT_TQ, Q_W), F32)],
        input_output_aliases={n_in: 0},
        compiler_params=_cp(*sem),
        name="attention_latent" if latent else "attention_context",
    )(q, *kv_args, x_all, mod_l, wout)


ROUTE_TB = 512
M_E1, M_E2, M_G1, M_G2, M_R1, M_R2 = 0, 1, 2, 3, 4, 5


def _router_kernel(x_ref, mod_ref, n2g_ref, wr_ref, br_ref, h_ref, meta_ref, cnt_ref, run):
    @pl.when(pl.program_id(0) == 0)
    def _():
        run[...] = jnp.zeros_like(run)

    shift, scale = mod_ref[3:4, :], mod_ref[4:5, :]
    h = _rms(x_ref[...], n2g_ref[...]) * (1.0 + scale) + shift
    _rows_to_tiles(h_ref, h)
    logits = jnp.dot(h, wr_ref[...], precision=lax.Precision.HIGHEST, preferred_element_type=F32) + br_ref[...]
    lane = lax.broadcasted_iota(jnp.int32, logits.shape, 1).astype(F32)
    big = 1e4

    def first_argmax(vals):
        m = vals.max(axis=-1, keepdims=True)
        return m, jnp.where(vals == m, lane, big).min(axis=-1, keepdims=True)

    gl = jnp.where((lane >= N_EXP) & (lane < N_EXP + MOE_GROUPS), logits, NEG)
    gmax, glane = first_argmax(gl)
    g_p = 1.0 / jnp.exp(gl - gmax).sum(axis=-1, keepdims=True)
    lo = (glane - N_EXP) * MOE_PER_GROUP
    el = jnp.where((lane >= lo) & (lane < lo + MOE_PER_GROUP), logits, NEG)
    m1, i1 = first_argmax(el)
    m2, i2 = first_argmax(jnp.where(lane == i1, NEG, el))
    t = jnp.exp(m2 - m1)
    w1 = 1.0 / (1.0 + t)
    sel1, sel2 = lane == i1, lane == i2
    onehot = jnp.where(sel1 | sel2, 1.0, 0.0)
    ri = lax.broadcasted_iota(jnp.int32, (ROUTE_TB, ROUTE_TB), 0)
    rj = lax.broadcasted_iota(jnp.int32, (ROUTE_TB, ROUTE_TB), 1)
    before = _dot(jnp.where(ri > rj, 1.0, 0.0), onehot) + run[...]
    r1 = jnp.where(sel1, before, 0.0).sum(axis=-1, keepdims=True)
    r2 = jnp.where(sel2, before, 0.0).sum(axis=-1, keepdims=True)
    run[...] += onehot.sum(axis=0, keepdims=True)
    cnt_ref[...] = run[...]
    meta = jnp.zeros_like(logits)
    for j, val in enumerate([i1, i2, w1 * g_p, (t * w1) * g_p, r1, r2]):
        meta = jnp.where(lane == j, val, meta)
    meta_ref[...] = meta


def _router(x_all, mod_l, n2g, wr, br):
    nb_ctx = N_CTX // ROUTE_TB
    per_seq = DEC_SEQ // ROUTE_TB
    cond = lambda i: jnp.where(i < nb_ctx, 0, 1 + (i - nb_ctx) // per_seq)
    const2 = lambda i: (0, 0)
    return pl.pallas_call(
        _router_kernel,
        grid=(N_TOK // ROUTE_TB,),
        in_specs=[
            pl.BlockSpec((ROUTE_TB, D), lambda i: (i, 0)),
            pl.BlockSpec((None, 6, D), lambda i: (cond(i), 0, 0)),
            _resident((1, D), const2),
            _resident((D, 128), const2),
            _resident((1, 128), const2),
        ],
        out_specs=[
            pl.BlockSpec((ROUTE_TB * 8, 128), lambda i: (i, 0)),
            pl.BlockSpec((ROUTE_TB, 128), lambda i: (i, 0)),
            pl.BlockSpec((1, 128), const2),
        ],
        out_shape=[
            jax.ShapeDtypeStruct((N_TOK * 8, 128), F32),
            jax.ShapeDtypeStruct((N_TOK, 128), F32),
            jax.ShapeDtypeStruct((1, 128), F32),
        ],
        scratch_shapes=[pltpu.VMEM((1, 128), F32)],
        compiler_params=_cp("arbitrary"),
        name="moe_router",
    )(x_all, mod_l, n2g, wr, br)


EXP_TM = 256
N_ASSIGN = 2 * N_TOK
MAX_TILES = N_ASSIGN // EXP_TM + N_EXP
DISPATCH_TB = 256


def _dispatch_kernel(pos_ref, ztile_ref, h_ref, xs_hbm, zeros, sem_z, sem):
    i = pl.program_id(0)

    def zero_copy(e):
        return pltpu.make_async_copy(zeros, xs_hbm.at[pl.ds(pl.multiple_of(ztile_ref[e], 8), EXP_TM * 8), :], sem_z)

    @pl.when(i == 0)
    def _():
        zeros[...] = jnp.zeros_like(zeros)
        for e in range(N_EXP):
            @pl.when(ztile_ref[e] >= 0)
            def _():
                zero_copy(e).start()
        for e in range(N_EXP):
            @pl.when(ztile_ref[e] >= 0)
            def _():
                zero_copy(e).wait()

    base = i * DISPATCH_TB

    def row_copy(r, k):
        return pltpu.make_async_copy(_tile_of(h_ref, r * 8), _tile_of(xs_hbm, pos_ref[k, base + r]), sem)

    def issue(r, carry):
        row_copy(r, 0).start()
        row_copy(r, 1).start()
        return carry

    def drain(r, carry):
        row_copy(r, 0).wait()
        row_copy(r, 1).wait()
        return carry

    lax.fori_loop(0, DISPATCH_TB, issue, 0, unroll=8)
    lax.fori_loop(0, DISPATCH_TB, drain, 0, unroll=8)


def _dispatch(pos, ztile, h):
    return pl.pallas_call(
        _dispatch_kernel,
        grid_spec=pltpu.PrefetchScalarGridSpec(
            num_scalar_prefetch=2,
            grid=(N_TOK // DISPATCH_TB,),
            in_specs=[pl.BlockSpec((DISPATCH_TB * 8, 128), lambda i, pos, zt: (i, 0))],
            out_specs=pl.BlockSpec(memory_space=pl.ANY),
            scratch_shapes=[pltpu.VMEM((EXP_TM * 8, 128), F32), pltpu.SemaphoreType.DMA, pltpu.SemaphoreType.DMA],
        ),
        out_shape=jax.ShapeDtypeStruct((MAX_TILES * EXP_TM * 8, 128), F32),
        compiler_params=_cp("arbitrary"),
        name="moe_dispatch",
    )(pos, ztile, h)


def _grouped_kernel(te_ref, nv_ref, xs_ref, wg_ref, wu_ref, wd_ref, ys_ref):
    @pl.when(pl.program_id(0) < nv_ref[0])
    def _():
        x = _tiles_to_rows(xs_ref, EXP_TM).astype(BF16)
        hid = _silu(_dot(x, wg_ref[...])) * _dot(x, wu_ref[...])
        _rows_to_tiles(ys_ref, _dot(hid, wd_ref[...]))


def _grouped(tile_expert, n_valid, xs, wg, wu, wd, layer):
    tile = lambda t, te, nv: jnp.minimum(t, nv[0] - 1)
    wmap = lambda t, te, nv: (layer, te[tile(t, te, nv)], 0, 0)
    return pl.pallas_call(
        _grouped_kernel,
        grid_spec=pltpu.PrefetchScalarGridSpec(
            num_scalar_prefetch=2,
            grid=(MAX_TILES,),
            in_specs=[
                pl.BlockSpec((EXP_TM * 8, 128), lambda t, te, nv: (tile(t, te, nv), 0)),
                pl.BlockSpec((None, None, D, D_EXP), wmap),
                pl.BlockSpec((None, None, D, D_EXP), wmap),
                pl.BlockSpec((None, None, D_EXP, D), wmap),
            ],
            out_specs=pl.BlockSpec((EXP_TM * 8, 128), lambda t, te, nv: (tile(t, te, nv), 0)),
        ),
        out_shape=jax.ShapeDtypeStruct((MAX_TILES * EXP_TM * 8, 128), F32),
        compiler_params=_cp("arbitrary"),
        name="moe_grouped",
    )(tile_expert, n_valid, xs, wg, wu, wd)


COMBINE_TB = 256


def _combine_kernel(pos_ref, ys_hbm, meta_ref, x_ref, mod_ref, xo_ref, buf1, buf2, sem):
    base = pl.program_id(0) * COMBINE_TB
    bufs = (buf1, buf2)

    def row_copy(r, k):
        return pltpu.make_async_copy(_tile_of(ys_hbm, pos_ref[k, base + r]), _tile_of(bufs[k], r * 8), sem)

    def issue(r, carry):
        row_copy(r, 0).start()
        row_copy(r, 1).start()
        return carry

    def drain(r, carry):
        row_copy(r, 0).wait()
        row_copy(r, 1).wait()
        return carry

    lax.fori_loop(0, COMBINE_TB, issue, 0, unroll=8)
    lax.fori_loop(0, COMBINE_TB, drain, 0, unroll=8)
    g1 = meta_ref[:, M_G1:M_G1 + 1]
    g2 = meta_ref[:, M_G2:M_G2 + 1]
    y = g1 * _tiles_to_rows(buf1, COMBINE_TB) + g2 * _tiles_to_rows(buf2, COMBINE_TB)
    xo_ref[...] = x_ref[...] + mod_ref[5:6, :] * y


def _combine(pos, ys, meta, x_all, mod_l):
    nb_ctx = N_CTX // COMBINE_TB
    per_seq = DEC_SEQ // COMBINE_TB
    cond = lambda i: jnp.where(i < nb_ctx, 0, 1 + (i - nb_ctx) // per_seq)
    return pl.pallas_call(
        _combine_kernel,
        grid_spec=pltpu.PrefetchScalarGridSpec(
            num_scalar_prefetch=1,
            grid=(N_TOK // COMBINE_TB,),
            in_specs=[
                pl.BlockSpec(memory_space=pl.ANY),
                pl.BlockSpec((COMBINE_TB, 128), lambda i, pos: (i, 0)),
                pl.BlockSpec((COMBINE_TB, D), lambda i, pos: (i, 0)),
                pl.BlockSpec((None, 6, D), lambda i, pos: (cond(i), 0, 0)),
            ],
            out_specs=pl.BlockSpec((COMBINE_TB, D), lambda i, pos: (i, 0)),
            scratch_shapes=[pltpu.VMEM((COMBINE_TB * 8, 128), F32), pltpu.VMEM((COMBINE_TB * 8, 128), F32),
                            pltpu.SemaphoreType.DMA],
        ),
        out_shape=jax.ShapeDtypeStruct((N_TOK, D), F32),
        compiler_params=_cp("arbitrary"),
        name="moe_combine",
    )(pos, ys, meta, x_all, mod_l)


def _moe(x_all, mod_l, n2g, wr, br, wg, wu, wd, layer):
    h, meta, cnt = _router(x_all, mod_l, n2g, wr, br)
    counts = cnt[0, :N_EXP].astype(jnp.int32)
    padded = (counts + EXP_TM - 1) // EXP_TM * EXP_TM
    ends = jnp.cumsum(padded)
    offs = ends - padded
    e1 = meta[:, M_E1].astype(jnp.int32)
    e2 = meta[:, M_E2].astype(jnp.int32)
    pos = 8 * jnp.stack([offs[e1] + meta[:, M_R1].astype(jnp.int32), offs[e2] + meta[:, M_R2].astype(jnp.int32)])
    n_valid = (ends[-1:] // EXP_TM).astype(jnp.int32)
    tile_expert = jnp.minimum(jnp.searchsorted(ends, jnp.arange(MAX_TILES, dtype=jnp.int32) * EXP_TM, side="right"),
                              N_EXP - 1).astype(jnp.int32)
    ztile = jnp.where(padded > 0, 8 * (ends - EXP_TM), -1).astype(jnp.int32)
    xs = _dispatch(pos, ztile, h)
    ys = _grouped(tile_expert, n_valid, xs, wg, wu, wd, layer)
    return _combine(pos, ys, meta, x_all, mod_l)


def _final_kernel(x_ref, g_ref, o_ref):
    o_ref[...] = _rms(x_ref[...], g_ref[...])


def _final_norm(x_all, g, blk0, n_rows):
    tb = 512
    return pl.pallas_call(
        _final_kernel,
        grid=(n_rows // tb,),
        in_specs=[pl.BlockSpec((tb, D), lambda i: (blk0 + i, 0)), _resident((1, D), lambda i: (0, 0))],
        out_specs=pl.BlockSpec((tb, D), lambda i: (i, 0)),
        out_shape=jax.ShapeDtypeStruct((n_rows, D), F32),
        compiler_params=_cp("arbitrary"),
        name="final_norm",
    )(x_all, g)


def _rope_tables():
    pos = jnp.arange(DEC_SEQ)
    row = (pos // GRID_W).astype(F32)
    col = (pos % GRID_W).astype(F32)
    n_freq = HD // 4
    inv = ROPE_THETA ** (-jnp.arange(n_freq, dtype=F32) / n_freq)
    ang = jnp.concatenate([row[:, None] * inv, col[:, None] * inv], axis=-1)
    cos = jnp.repeat(jnp.cos(ang), 2, axis=-1)
    sin = jnp.repeat(jnp.sin(ang), 2, axis=-1) * jnp.tile(jnp.array([-1.0, 1.0], F32), HD // 2)
    nblk = DEC_SEQ // QKV_TB
    cos_tab = jnp.concatenate([jnp.ones((1, QKV_TB, HD), F32), cos.reshape(nblk, QKV_TB, HD)], axis=0)
    sin_tab = jnp.concatenate([jnp.zeros((1, QKV_TB, HD), F32), sin.reshape(nblk, QKV_TB, HD)], axis=0)
    return cos_tab, sin_tab


def kernel(x_prompt, x_sample, state_gla, cache_k, cache_v, c, c_ctx, w_mod, b_mod, norm1_g, norm2_g,
           w_in_even, w_gate_up, b_gate_up, gla_norm_g, w_spatial, b_spatial, w_out_even,
           w_in_odd, q_norm_g, k_norm_g, w_out_odd, w_router_group, b_router_group,
           w_router_expert, b_router_expert, w_exp_gate, w_exp_up, w_exp_down, final_norm_g):
    x_all = jnp.concatenate([x_prompt.reshape(N_CTX, D), x_sample.reshape(N_LAT, D)], axis=0)
    cond8 = jnp.concatenate([c_ctx[None], c, jnp.zeros((3, D), F32)], axis=0)
    mod = _modulation(cond8, w_mod, b_mod)
    cos_tab, sin_tab = _rope_tables()
    zero_state = jnp.zeros((1, 2, QK_W, GLA_DV), F32)
    state_in = state_gla.reshape(DEC_BATCH, -1, 2, QK_W, GLA_DV)
    cache_k2 = cache_k.reshape(DEC_BATCH, -1, SEQ, KV_W)
    cache_v2 = cache_v.reshape(DEC_BATCH, -1, SEQ, KV_W)

    gla_states, ctx_k, ctx_v = [], [], []
    for l in range(DEPTH):
        i = l // 2
        n1g = norm1_g[l][None]
        if l % 2 == 0:
            w = w_in_even[i]
            win = jnp.concatenate([w[:, :1536], w[:, 1568:], w[:, 1536:1568], jnp.zeros((D, 96), F32)],
                                  axis=1).astype(BF16)
            wgu = jnp.zeros((128, 2 * QK_W), F32)
            wgu = wgu.at[0:GLA_RANK, 0:QK_W].set(w_gate_up[i, 0])
            wgu = wgu.at[GLA_RANK:2 * GLA_RANK, QK_W:].set(w_gate_up[i, 1]).astype(BF16)
            bgu = b_gate_up[i].reshape(1, 2 * QK_W)
            args = (mod[l], n1g, win, wgu, bgu, gla_norm_g[i][None], w_spatial[i].astype(BF16),
                    b_spatial[i].T, w_out_even[i].astype(BF16))
            x_all, st = _even_mixer(x_all, *args, zero_state, latent=False)
            gla_states.append(st)
            x_all, _ = _even_mixer(x_all, *args, state_in[:, i], latent=True)
        else:
            q, k, v = _qkv(x_all, mod[l], n1g, w_in_odd[i].astype(BF16), q_norm_g[i][None],
                           k_norm_g[i][None], cos_tab, sin_tab)
            ctx_k.append(k[:N_CTX].reshape(BATCH, SEQ, ATT_KV, HD))
            ctx_v.append(v[:N_CTX].reshape(BATCH, SEQ, ATT_KV, HD))
            wout = w_out_odd[i].astype(BF16)
            x_all = _attention(x_all, mod[l], q, k, v, wout)
            x_all = _attention(x_all, mod[l], q, k, v, wout, cache_k2, cache_v2, layer_i=i)
        wr = jnp.concatenate([w_router_expert[l], w_router_group[l],
                              jnp.zeros((D, 128 - N_EXP - MOE_GROUPS), F32)], axis=1)
        br = jnp.concatenate([b_router_expert[l], b_router_group[l],
                              jnp.zeros((128 - N_EXP - MOE_GROUPS,), F32)])[None]
        x_all = _moe(x_all, mod[l], norm2_g[l][None], wr, br, w_exp_gate, w_exp_up, w_exp_down, l)

    fg = final_norm_g[None]
    y_prompt = _final_norm(x_all, fg, 0, N_CTX).reshape(BATCH, SEQ, D)
    y_sample = _final_norm(x_all, fg, N_CTX // 512, N_LAT).reshape(DEC_BATCH, DEC_SEQ, D)
    new_state = jnp.stack(gla_states, axis=1).reshape(BATCH, -1, 2, GLA_HEADS, GLA_DK, GLA_DV)
    return (y_prompt, y_sample, new_state, jnp.stack(ctx_k, axis=1), jnp.stack(ctx_v, axis=1))
```

```python
import functools

import jax
import jax.numpy as jnp
import numpy as np
from jax import lax
from jax.experimental import pallas as pl
from jax.experimental.pallas import tpu as pltpu

F32 = jnp.float32
BF16 = jnp.bfloat16

D = 1024
BATCH, SEQ = 16, 256
DEC_BATCH, DEC_SEQ = 4, 1024
N_CTX = BATCH * SEQ
N_LAT = DEC_BATCH * DEC_SEQ
N_TOK = N_CTX + N_LAT
DEPTH = 4
EPS = 1e-6
GRID_W = 64
ROPE_THETA = 10000.0

GLA_HEADS, GLA_DK, GLA_DV, GLA_RANK, GLA_CHUNK, GLA_TAU = 4, 64, 128, 16, 64, 16.0
QK_W = GLA_HEADS * GLA_DK
V_W = GLA_HEADS * GLA_DV
GMLP_GROUPS, GMLP_DIM, GMLP_CHUNK = 4, 128, 128
GMLP_W = GMLP_GROUPS * GMLP_DIM
C_Q, C_K, C_V, C_G, C_U, C_VG, C_A = 0, 256, 512, 1024, 1536, 2048, 2560
EVEN_PACK = 2688

ATT_HEADS, ATT_KV, HD = 8, 2, 128
ATT_G = ATT_HEADS // ATT_KV
Q_W = ATT_HEADS * HD
KV_W = ATT_KV * HD

MOE_GROUPS, MOE_PER_GROUP = 4, 8
N_EXP = MOE_GROUPS * MOE_PER_GROUP
D_EXP = D // 4
NEG = -1e30

VMEM_LIMIT = 56 * 1024 * 1024


def _cp(*sem):
    return pltpu.CompilerParams(dimension_semantics=sem, vmem_limit_bytes=VMEM_LIMIT)


def _dot(a, b):
    return jnp.dot(a.astype(BF16), b.astype(BF16), preferred_element_type=F32)


def _dot_nt(a, b):
    return lax.dot_general(a.astype(BF16), b.astype(BF16), (((1,), (1,)), ((), ())),
                           preferred_element_type=F32)


def _dot_tn(a, b):
    return lax.dot_general(a.astype(BF16), b.astype(BF16), (((0,), (0,)), ((), ())),
                           preferred_element_type=F32)


def _rms(x, g):
    return x * lax.rsqrt(jnp.mean(x * x, axis=-1, keepdims=True) + EPS) * g


def _silu(x):
    return x * jax.nn.sigmoid(x)


def _gelu(x):
    return 0.5 * x * (1.0 + jnp.tanh(np.sqrt(2.0 / np.pi).astype(np.float32) * (x + 0.044715 * (x * x * x))))


def _log_sigmoid(z):
    return jnp.minimum(z, 0.0) - jnp.log(1.0 + jnp.exp(-jnp.abs(z)))


def _rows_to_tiles(ref, x):
    rows = x.shape[0]
    for j in range(D // 128):
        ref[pl.ds(j, rows, stride=8), :] = x[:, j * 128:(j + 1) * 128]


def _tiles_to_rows(ref, rows):
    return jnp.concatenate([ref[pl.ds(j, rows, stride=8), :] for j in range(D // 128)], axis=1)


def _tile_of(ref, row8):
    return ref.at[pl.ds(pl.multiple_of(row8, 8), 8), :]


def _resident(shape, index_map):
    return pl.BlockSpec(shape, index_map, pipeline_mode=pl.Buffered(1))


def _mod_kernel(cond_ref, w_ref, b_ref, o_ref):
    c = cond_ref[...]
    o_ref[...] = jnp.dot(_silu(c), w_ref[...], precision=lax.Precision.HIGHEST,
                         preferred_element_type=F32) + b_ref[...]


def _modulation(cond8, w_mod, b_mod):
    tn = 1024
    out = pl.pallas_call(
        _mod_kernel,
        grid=(DEPTH, 6 * D // tn),
        in_specs=[
            pl.BlockSpec((8, D), lambda l, j: (0, 0)),
            pl.BlockSpec((None, D, tn), lambda l, j: (l, 0, j)),
            pl.BlockSpec((None, 1, tn), lambda l, j: (l, 0, j)),
        ],
        out_specs=pl.BlockSpec((None, 8, tn), lambda l, j: (l, 0, j)),
        out_shape=jax.ShapeDtypeStruct((DEPTH, 8, 6 * D), F32),
        compiler_params=_cp("arbitrary", "arbitrary"),
        name="adaln_mod",
    )(cond8, w_mod, b_mod.reshape(DEPTH, 1, 6 * D))
    return out.reshape(DEPTH, 8, 6, D)


def _even_kernel(x_ref, mod_ref, n1g_ref, win_ref, wgu_ref, bgu_ref, glag_ref, ws_ref, bs_ref,
                 wout_ref, s0_ref, xo_ref, st_ref, proj, la, o_f, o_b, st_scr, *, T):
    n_chunks = T // GLA_CHUNK
    shift, scale, gate = mod_ref[0:1, :], mod_ref[1:2, :], mod_ref[2:3, :]
    RB = 128

    def proj_body(r, carry):
        r0 = pl.multiple_of(r * RB, RB)
        h = _rms(x_ref[pl.ds(r0, RB), :], n1g_ref[...]) * (1.0 + scale) + shift
        p = _dot(h, win_ref[...])
        proj[pl.ds(r0, RB), :] = p
        z = _dot(p[:, C_A:C_A + 128], wgu_ref[...]) + bgu_ref[...]
        la[pl.ds(r0, RB), :] = _log_sigmoid(z) * (1.0 / GLA_TAU)
        return carry

    lax.fori_loop(0, T // RB, proj_body, 0)

    st_scr[0] = s0_ref[0].T
    st_scr[1] = s0_ref[1].T

    ci = lax.broadcasted_iota(jnp.int32, (GLA_CHUNK, GLA_CHUNK), 0)
    cj = lax.broadcasted_iota(jnp.int32, (GLA_CHUNK, GLA_CHUNK), 1)
    tri = (jnp.where(ci >= cj, 1.0, 0.0).astype(BF16), jnp.where(ci <= cj, 1.0, 0.0).astype(BF16))
    ai = lax.broadcasted_iota(jnp.int32, (GLA_HEADS * GLA_CHUNK, GLA_CHUNK), 0) % GLA_CHUNK
    aj = lax.broadcasted_iota(jnp.int32, (GLA_HEADS * GLA_CHUNK, GLA_CHUNK), 1)
    amask = (ai >= aj, ai <= aj)
    lane_head = lax.broadcasted_iota(jnp.int32, (1, QK_W), 1) // GLA_DK
    hmask = [jnp.where(lane_head == h, 1.0, 0.0) for h in range(GLA_HEADS)]

    def chunk_body(i, carry):
        for d in range(2):
            c = i if d == 0 else n_chunks - 1 - i
            r0 = pl.multiple_of(c * GLA_CHUNK, GLA_CHUNK)
            q = proj[pl.ds(r0, GLA_CHUNK), C_Q:C_Q + QK_W] * (GLA_DK ** -0.5)
            k = proj[pl.ds(r0, GLA_CHUNK), C_K:C_K + QK_W]
            v = proj[pl.ds(r0, GLA_CHUNK), C_V:C_V + V_W]
            lac = la[pl.ds(r0, GLA_CHUNK), d * QK_W:(d + 1) * QK_W]
            hi = lac.astype(BF16)
            lo = (lac - hi.astype(F32)).astype(BF16)
            b = (jnp.dot(tri[d], hi, preferred_element_type=F32)
                 + jnp.dot(tri[d], lo, preferred_element_type=F32))
            bend = b[GLA_CHUNK - 1:GLA_CHUNK, :] if d == 0 else b[0:1, :]
            qe = q * jnp.exp(b)
            ke = k * jnp.exp(-b)
            kd = k * jnp.exp(bend - b)
            st = st_scr[d]
            qstack = jnp.concatenate([qe * hmask[h] for h in range(GLA_HEADS)], axis=0).astype(BF16)
            att = jnp.where(amask[d], _dot_nt(qstack, ke), 0.0)
            inter = _dot_nt(qstack, st)
            outs = []
            for h in range(GLA_HEADS):
                rows = slice(h * GLA_CHUNK, (h + 1) * GLA_CHUNK)
                outs.append(_dot(att[rows], v[:, h * GLA_DV:(h + 1) * GLA_DV]) + inter[rows])
            o = jnp.concatenate(outs, axis=1)
            if d == 0:
                o_f[pl.ds(r0, GLA_CHUNK), :] = o
            else:
                o_b[pl.ds(r0, GLA_CHUNK), :] = o
            vstack = jnp.concatenate([v[:, h * GLA_DV:(h + 1) * GLA_DV] for h in range(GLA_HEADS)], axis=0)
            kstack = jnp.concatenate([kd * hmask[h] for h in range(GLA_HEADS)], axis=0)
            st_scr[d] = st * jnp.exp(bend) + _dot_tn(vstack, kstack)
        return carry

    lax.fori_loop(0, n_chunks, chunk_body, 0)
    st_ref[0] = st_scr[0].T
    st_ref[1] = st_scr[1].T

    def out_body(r, carry):
        r0 = pl.multiple_of(r * RB, RB)
        osum = o_f[pl.ds(r0, RB), :] + o_b[pl.ds(r0, RB), :]
        g = proj[pl.ds(r0, RB), C_G:C_G + V_W]
        u = proj[pl.ds(r0, RB), C_U:C_U + GMLP_W]
        vg = _gelu(proj[pl.ds(r0, RB), C_VG:C_VG + GMLP_W])
        parts = []
        for h in range(GLA_HEADS):
            oh = osum[:, h * GLA_DV:(h + 1) * GLA_DV]
            parts.append(_rms(oh, glag_ref[...]) * _silu(g[:, h * GLA_DV:(h + 1) * GLA_DV]))
        for gi in range(GMLP_GROUPS):
            vc = vg[:, gi * GMLP_DIM:(gi + 1) * GMLP_DIM]
            vc = vc - jnp.mean(vc, axis=-1, keepdims=True)
            vn = vc * lax.rsqrt(jnp.mean(vc * vc, axis=-1, keepdims=True) + EPS)
            sg = _dot(ws_ref[gi], vn) + bs_ref[:, gi:gi + 1]
            parts.append(_gelu(u[:, gi * GMLP_DIM:(gi + 1) * GMLP_DIM]) * sg)
        mix = jnp.concatenate(parts, axis=1)
        y = _dot(mix, wout_ref[...])
        xo_ref[pl.ds(r0, RB), :] = x_ref[pl.ds(r0, RB), :] + gate * y
        return carry

    lax.fori_loop(0, T // RB, out_body, 0)


def _even_mixer(x_all, mod_l, n1g, win, wgu, bgu, glag, ws, bs, wout, s0, *, latent):
    if latent:
        T, nseq, blk0 = DEC_SEQ, DEC_BATCH, N_CTX // DEC_SEQ
        cond = lambda i: 1 + i
        s0_spec = pl.BlockSpec((None, 2, QK_W, GLA_DV), lambda i: (i, 0, 0, 0))
    else:
        T, nseq, blk0 = SEQ, BATCH, 0
        cond = lambda i: 0
        s0_spec = pl.BlockSpec((None, 2, QK_W, GLA_DV), lambda i: (0, 0, 0, 0))
    const2 = lambda i: (0, 0)
    x_new, states = pl.pallas_call(
        functools.partial(_even_kernel, T=T),
        grid=(nseq,),
        in_specs=[
            pl.BlockSpec((T, D), lambda i: (blk0 + i, 0)),
            pl.BlockSpec((None, 6, D), lambda i: (cond(i), 0, 0)),
            _resident((1, D), const2),
            _resident((D, EVEN_PACK), const2),
            _resident((128, 2 * QK_W), const2),
            _resident((1, 2 * QK_W), const2),
            _resident((1, GLA_DV), const2),
            _resident((GMLP_GROUPS, GMLP_CHUNK, GMLP_CHUNK), lambda i: (0, 0, 0)),
            _resident((GMLP_CHUNK, GMLP_GROUPS), const2),
            _resident((D, D), const2),
            s0_spec,
        ],
        out_specs=[
            pl.BlockSpec((T, D), lambda i: (blk0 + i, 0)),
            pl.BlockSpec((None, 2, QK_W, GLA_DV), lambda i: (i, 0, 0, 0)),
        ],
        out_shape=[
            jax.ShapeDtypeStruct((N_TOK, D), F32),
            jax.ShapeDtypeStruct((nseq, 2, QK_W, GLA_DV), F32),
        ],
        scratch_shapes=[
            pltpu.VMEM((T, EVEN_PACK), F32),
            pltpu.VMEM((T, 2 * QK_W), F32),
            pltpu.VMEM((T, V_W), F32),
            pltpu.VMEM((T, V_W), F32),
            pltpu.VMEM((2, GLA_DV, QK_W), F32),
        ],
        input_output_aliases={0: 0},
        compiler_params=_cp("arbitrary"),
        name="even_mixer_latent" if latent else "even_mixer_context",
    )(x_all, mod_l, n1g, win, wgu, bgu, glag, ws, bs, wout, s0)
    return x_new, states


QKV_TB = 512


def _qkv_kernel(x_ref, mod_ref, n1g_ref, win_ref, gq_ref, gk_ref, cos_ref, sin_ref, q_ref, k_ref, v_ref):
    shift, scale = mod_ref[0:1, :], mod_ref[1:2, :]
    h = _rms(x_ref[...], n1g_ref[...]) * (1.0 + scale) + shift
    p = _dot(h, win_ref[...])
    cos, sin = cos_ref[...], sin_ref[...]
    even_lane = lax.broadcasted_iota(jnp.int32, (1, HD), 1) % 2 == 0

    def norm_rope(xh, g):
        xn = _rms(xh, g)
        swapped = jnp.where(even_lane, pltpu.roll(xn, HD - 1, axis=1), pltpu.roll(xn, 1, axis=1))
        return xn * cos + swapped * sin

    for hh in range(ATT_HEADS):
        q_ref[:, hh * HD:(hh + 1) * HD] = norm_rope(p[:, hh * HD:(hh + 1) * HD], gq_ref[...]).astype(BF16)
    for hh in range(ATT_KV):
        k_ref[:, hh * HD:(hh + 1) * HD] = norm_rope(p[:, Q_W + hh * HD:Q_W + (hh + 1) * HD], gk_ref[...])
    v_ref[...] = p[:, Q_W + KV_W:]


def _qkv(x_all, mod_l, n1g, win, gq, gk, cos_tab, sin_tab):
    nb_ctx = N_CTX // QKV_TB
    per_seq = DEC_SEQ // QKV_TB
    cond = lambda i: jnp.where(i < nb_ctx, 0, 1 + (i - nb_ctx) // per_seq)
    tab = lambda i: jnp.where(i < nb_ctx, 0, 1 + (i - nb_ctx) % per_seq)
    const2 = lambda i: (0, 0)
    return pl.pallas_call(
        _qkv_kernel,
        grid=(N_TOK // QKV_TB,),
        in_specs=[
            pl.BlockSpec((QKV_TB, D), lambda i: (i, 0)),
            pl.BlockSpec((None, 6, D), lambda i: (cond(i), 0, 0)),
            _resident((1, D), const2),
            _resident((D, Q_W + 2 * KV_W), const2),
            _resident((1, HD), const2),
            _resident((1, HD), const2),
            pl.BlockSpec((None, QKV_TB, HD), lambda i: (tab(i), 0, 0)),
            pl.BlockSpec((None, QKV_TB, HD), lambda i: (tab(i), 0, 0)),
        ],
        out_specs=[
            pl.BlockSpec((QKV_TB, Q_W), lambda i: (i, 0)),
            pl.BlockSpec((QKV_TB, KV_W), lambda i: (i, 0)),
            pl.BlockSpec((QKV_TB, KV_W), lambda i: (i, 0)),
        ],
        out_shape=[
            jax.ShapeDtypeStruct((N_TOK, Q_W), BF16),
            jax.ShapeDtypeStruct((N_TOK, KV_W), F32),
            jax.ShapeDtypeStruct((N_TOK, KV_W), F32),
        ],
        compiler_params=_cp("arbitrary"),
        name="odd_qkv",
    )(x_all, mod_l, n1g, win, gq, gk, cos_tab, sin_tab)


ATT_TQ = 256


def _attn_kernel(*refs, n_kv):
    q_ref = refs[0]
    kv_refs = refs[1:1 + 2 * n_kv]
    x_ref, mod_ref, wout_ref, xo_ref, att_scr = refs[1 + 2 * n_kv:]
    gate = mod_ref[2:3, :]
    for kh in range(ATT_KV):
        ks = [kv_refs[2 * s][:, kh * HD:(kh + 1) * HD].astype(BF16) for s in range(n_kv)]
        vs = [kv_refs[2 * s + 1][:, kh * HD:(kh + 1) * HD].astype(BF16) for s in range(n_kv)]
        for g in range(ATT_G):
            hh = kh * ATT_G + g
            qh = q_ref[:, hh * HD:(hh + 1) * HD]
            ss = [_dot_nt(qh, kk) * (HD ** -0.5) for kk in ks]
            m = ss[0].max(axis=-1, keepdims=True)
            for s in ss[1:]:
                m = jnp.maximum(m, s.max(axis=-1, keepdims=True))
            ps = [jnp.exp(s - m) for s in ss]
            den = ps[0].sum(axis=-1, keepdims=True)
            for p in ps[1:]:
                den = den + p.sum(axis=-1, keepdims=True)
            o = _dot(ps[0], vs[0])
            for p, vv in zip(ps[1:], vs[1:]):
                o = o + _dot(p, vv)
            att_scr[:, hh * HD:(hh + 1) * HD] = o / den
    y = _dot(att_scr[...], wout_ref[...])
    xo_ref[...] = x_ref[...] + gate * y


def _attention(x_all, mod_l, q, k, v, wout, cache_k=None, cache_v=None, layer_i=0):
    latent = cache_k is not None
    const2 = lambda *a: (0, 0)
    if latent:
        nq = DEC_SEQ // ATT_TQ
        row_blk = lambda b, j: (N_CTX // ATT_TQ + b * nq + j, 0)
        grid = (DEC_BATCH, nq)
        kv_specs = [
            pl.BlockSpec((None, None, SEQ, KV_W), lambda b, j: (b, layer_i, 0, 0)),
            pl.BlockSpec((None, None, SEQ, KV_W), lambda b, j: (b, layer_i, 0, 0)),
            pl.BlockSpec((DEC_SEQ, KV_W), lambda b, j: (N_CTX // DEC_SEQ + b, 0)),
            pl.BlockSpec((DEC_SEQ, KV_W), lambda b, j: (N_CTX // DEC_SEQ + b, 0)),
        ]
        kv_args = (cache_k, cache_v, k, v)
        mod_spec = pl.BlockSpec((None, 6, D), lambda b, j: (1 + b, 0, 0))
        sem = ("arbitrary", "arbitrary")
        n_kv = 2
    else:
        row_blk = lambda i: (i, 0)
        grid = (BATCH,)
        kv_specs = [pl.BlockSpec((SEQ, KV_W), row_blk), pl.BlockSpec((SEQ, KV_W), row_blk)]
        kv_args = (k, v)
        mod_spec = pl.BlockSpec((None, 6, D), lambda i: (0, 0, 0))
        sem = ("arbitrary",)
        n_kv = 1
    n_in = 1 + len(kv_args)
    return pl.pallas_call(
        functools.partial(_attn_kernel, n_kv=n_kv),
        grid=grid,
        in_specs=[pl.BlockSpec((ATT_TQ, Q_W), row_blk)] + kv_specs + [
            pl.BlockSpec((ATT_TQ, D), row_blk),
            mod_spec,
            _resident((D, D), const2),
        ],
        out_specs=pl.BlockSpec((ATT_TQ, D), row_blk),
        out_shape=jax.ShapeDtypeStruct((N_TOK, D), F32),
        scratch_shapes=[pltpu.VMEM((ATT_TQ, Q_W), F32)],
        input_output_aliases={n_in: 0},
        compiler_params=_cp(*sem),
        name="attention_latent" if latent else "attention_context",
    )(q, *kv_args, x_all, mod_l, wout)


ROUTE_TB = 512
M_E1, M_E2, M_G1, M_G2, M_R1, M_R2 = 0, 1, 2, 3, 4, 5


def _router_kernel(x_ref, mod_ref, n2g_ref, wr_ref, br_ref, h_ref, meta_ref, metat_ref, cnt_ref, run):
    @pl.when(pl.program_id(0) == 0)
    def _():
        run[...] = jnp.zeros_like(run)

    shift, scale = mod_ref[3:4, :], mod_ref[4:5, :]
    h = _rms(x_ref[...], n2g_ref[...]) * (1.0 + scale) + shift
    _rows_to_tiles(h_ref, h)
    logits = jnp.dot(h, wr_ref[...], precision=lax.Precision.HIGHEST, preferred_element_type=F32) + br_ref[...]
    lane = lax.broadcasted_iota(jnp.int32, logits.shape, 1).astype(F32)
    big = 1e4

    def first_argmax(vals):
        m = vals.max(axis=-1, keepdims=True)
        return m, jnp.where(vals == m, lane, big).min(axis=-1, keepdims=True)

    gl = jnp.where((lane >= N_EXP) & (lane < N_EXP + MOE_GROUPS), logits, NEG)
    gmax, glane = first_argmax(gl)
    g_p = 1.0 / jnp.exp(gl - gmax).sum(axis=-1, keepdims=True)
    lo = (glane - N_EXP) * MOE_PER_GROUP
    el = jnp.where((lane >= lo) & (lane < lo + MOE_PER_GROUP), logits, NEG)
    m1, i1 = first_argmax(el)
    m2, i2 = first_argmax(jnp.where(lane == i1, NEG, el))
    t = jnp.exp(m2 - m1)
    w1 = 1.0 / (1.0 + t)
    sel1, sel2 = lane == i1, lane == i2
    onehot = jnp.where(sel1 | sel2, 1.0, 0.0)
    ri = lax.broadcasted_iota(jnp.int32, (ROUTE_TB, ROUTE_TB), 0)
    rj = lax.broadcasted_iota(jnp.int32, (ROUTE_TB, ROUTE_TB), 1)
    before = _dot(jnp.where(ri > rj, 1.0, 0.0), onehot) + run[...]
    r1 = jnp.where(sel1, before, 0.0).sum(axis=-1, keepdims=True)
    r2 = jnp.where(sel2, before, 0.0).sum(axis=-1, keepdims=True)
    run[...] += onehot.sum(axis=0, keepdims=True)
    cnt_ref[...] = run[...]
    meta = jnp.zeros_like(logits)
    for j, val in enumerate([i1, i2, w1 * g_p, (t * w1) * g_p, r1, r2]):
        meta = jnp.where(lane == j, val, meta)
    meta_ref[...] = meta
    metat_ref[...] = meta.T[0:8, :]


def _router(x_all, mod_l, n2g, wr, br):
    nb_ctx = N_CTX // ROUTE_TB
    per_seq = DEC_SEQ // ROUTE_TB
    cond = lambda i: jnp.where(i < nb_ctx, 0, 1 + (i - nb_ctx) // per_seq)
    const2 = lambda i: (0, 0)
    return pl.pallas_call(
        _router_kernel,
        grid=(N_TOK // ROUTE_TB,),
        in_specs=[
            pl.BlockSpec((ROUTE_TB, D), lambda i: (i, 0)),
            pl.BlockSpec((None, 6, D), lambda i: (cond(i), 0, 0)),
            _resident((1, D), const2),
            _resident((D, 128), const2),
            _resident((1, 128), const2),
        ],
        out_specs=[
            pl.BlockSpec((ROUTE_TB * 8, 128), lambda i: (i, 0)),
            pl.BlockSpec((ROUTE_TB, 128), lambda i: (i, 0)),
            pl.BlockSpec((8, ROUTE_TB), lambda i: (0, i)),
            pl.BlockSpec((1, 128), const2),
        ],
        out_shape=[
            jax.ShapeDtypeStruct((N_TOK * 8, 128), F32),
            jax.ShapeDtypeStruct((N_TOK, 128), F32),
            jax.ShapeDtypeStruct((8, N_TOK), F32),
            jax.ShapeDtypeStruct((1, 128), F32),
        ],
        scratch_shapes=[pltpu.VMEM((1, 128), F32)],
        compiler_params=_cp("arbitrary"),
        name="moe_router",
    )(x_all, mod_l, n2g, wr, br)


EXP_TM = 256
N_ASSIGN = 2 * N_TOK
MAX_TILES = N_ASSIGN // EXP_TM + N_EXP
N_SORTED = MAX_TILES * EXP_TM
PLANE = N_TOK + EXP_TM
ORDER_BLK = 4096


def _order_kernel(packed_ref, offs_ref, src_ref):
    base = pl.program_id(0) * ORDER_BLK

    def body(a, carry):
        v = packed_ref[a]
        g = base + a
        src_ref[offs_ref[v >> 16] + (v & 0xFFFF)] = (g & 1) * PLANE + (g >> 1)
        return carry

    lax.fori_loop(0, ORDER_BLK, body, 0, unroll=8)


def _order(packed, offs):
    return pl.pallas_call(
        _order_kernel,
        grid=(N_ASSIGN // ORDER_BLK,),
        in_specs=[
            pl.BlockSpec((ORDER_BLK,), lambda i: (i,), memory_space=pltpu.SMEM),
            pl.BlockSpec(memory_space=pltpu.SMEM),
        ],
        out_specs=pl.BlockSpec(memory_space=pltpu.SMEM),
        out_shape=jax.ShapeDtypeStruct((N_SORTED,), jnp.int32),
        compiler_params=_cp("arbitrary"),
        name="moe_order",
    )(packed, offs)


def _experts_kernel(te_ref, rows_ref, nv_ref, src_ref, h_hbm, wg_ref, wu_ref, wd_ref, out_hbm,
                    xbuf0, xbuf1, ybuf0, ybuf1, gsem, ssem):
    t = pl.program_id(0)
    nv = nv_ref[0]
    xbufs, ybufs = (xbuf0, xbuf1), (ybuf0, ybuf1)

    def code_of(tile, r):
        n_rows = rows_ref[tile]
        return src_ref[tile * EXP_TM + jnp.minimum(r, n_rows - 1)], r < n_rows

    def gather(tile, slot, r):
        code, valid = code_of(tile, r)
        tok = jnp.where(code >= PLANE, code - PLANE, code)
        row = jnp.where(valid, tok, 0)
        return pltpu.make_async_copy(_tile_of(h_hbm, row * 8), _tile_of(xbufs[slot], r * 8), gsem.at[slot])

    def scatter(tile, slot, r):
        code, valid = code_of(tile, r)
        row = jnp.where(valid, code, slot * PLANE + N_TOK + r)
        return pltpu.make_async_copy(_tile_of(ybufs[slot], r * 8), _tile_of(out_hbm, row * 8), ssem.at[slot])

    def for_rows(fn):
        def body(r, carry):
            fn(r)
            return carry
        lax.fori_loop(0, EXP_TM, body, 0, unroll=8)

    def wait_gathers(slot):
        for_rows(lambda r: pltpu.make_async_copy(_tile_of(h_hbm, 0), _tile_of(xbufs[slot], 0), gsem.at[slot]).wait())

    def wait_scatters(slot):
        for_rows(lambda r: pltpu.make_async_copy(_tile_of(ybufs[slot], 0), _tile_of(out_hbm, 0), ssem.at[slot]).wait())

    @pl.when(t == 0)
    def _():
        for_rows(lambda r: gather(0, 0, r).start())

    def process(slot):
        @pl.when(t + 1 < nv)
        def _():
            for_rows(lambda r: gather(t + 1, 1 - slot, r).start())

        wait_gathers(slot)

        @pl.when(t >= 2)
        def _():
            wait_scatters(slot)

        x = _tiles_to_rows(xbufs[slot], EXP_TM).astype(BF16)
        hid = _silu(_dot(x, wg_ref[...])) * _dot(x, wu_ref[...])
        _rows_to_tiles(ybufs[slot], _dot(hid, wd_ref[...]))
        for_rows(lambda r: scatter(t, slot, r).start())

        @pl.when(t == nv - 1)
        def _():
            wait_scatters(slot)

            @pl.when(t >= 1)
            def _():
                wait_scatters(1 - slot)

    for slot in range(2):
        @pl.when((t < nv) & (t % 2 == slot))
        def _():
            process(slot)


def _experts(tile_expert, tile_rows, n_valid, src, h, wg, wu, wd, layer):
    wmap = lambda t, te, rows, nv, src: (layer, te[jnp.minimum(t, nv[0] - 1)], 0, 0)
    tile_buf = pltpu.VMEM((EXP_TM * 8, 128), F32)
    return pl.pallas_call(
        _experts_kernel,
        grid_spec=pltpu.PrefetchScalarGridSpec(
            num_scalar_prefetch=4,
            grid=(MAX_TILES,),
            in_specs=[
                pl.BlockSpec(memory_space=pl.ANY),
                pl.BlockSpec((None, None, D, D_EXP), wmap),
                pl.BlockSpec((None, None, D, D_EXP), wmap),
                pl.BlockSpec((None, None, D_EXP, D), wmap),
            ],
            out_specs=pl.BlockSpec(memory_space=pl.ANY),
            scratch_shapes=[tile_buf, tile_buf, tile_buf, tile_buf,
                            pltpu.SemaphoreType.DMA((2,)), pltpu.SemaphoreType.DMA((2,))],
        ),
        out_shape=jax.ShapeDtypeStruct((2 * PLANE * 8, 128), F32),
        compiler_params=_cp("arbitrary"),
        name="moe_experts",
    )(tile_expert, tile_rows, n_valid, src, h, wg, wu, wd)


COMBINE_TB = 256


def _combine_kernel(y1_ref, y2_ref, meta_ref, x_ref, mod_ref, xo_ref):
    g1 = meta_ref[:, M_G1:M_G1 + 1]
    g2 = meta_ref[:, M_G2:M_G2 + 1]
    y = g1 * _tiles_to_rows(y1_ref, COMBINE_TB) + g2 * _tiles_to_rows(y2_ref, COMBINE_TB)
    xo_ref[...] = x_ref[...] + mod_ref[5:6, :] * y


def _combine(ys, meta, x_all, mod_l):
    nb_ctx = N_CTX // COMBINE_TB
    per_seq = DEC_SEQ // COMBINE_TB
    cond = lambda i: jnp.where(i < nb_ctx, 0, 1 + (i - nb_ctx) // per_seq)
    return pl.pallas_call(
        _combine_kernel,
        grid=(N_TOK // COMBINE_TB,),
        in_specs=[
            pl.BlockSpec((COMBINE_TB * 8, 128), lambda i: (i, 0)),
            pl.BlockSpec((COMBINE_TB * 8, 128), lambda i: (PLANE // COMBINE_TB + i, 0)),
            pl.BlockSpec((COMBINE_TB, 128), lambda i: (i, 0)),
            pl.BlockSpec((COMBINE_TB, D), lambda i: (i, 0)),
            pl.BlockSpec((None, 6, D), lambda i: (cond(i), 0, 0)),
        ],
        out_specs=pl.BlockSpec((COMBINE_TB, D), lambda i: (i, 0)),
        out_shape=jax.ShapeDtypeStruct((N_TOK, D), F32),
        compiler_params=_cp("arbitrary"),
        name="moe_combine",
    )(ys, ys, meta, x_all, mod_l)


def _moe(x_all, mod_l, n2g, wr, br, wg, wu, wd, layer):
    h, meta, metat, cnt = _router(x_all, mod_l, n2g, wr, br)
    counts = cnt[0, :N_EXP].astype(jnp.int32)
    padded = (counts + EXP_TM - 1) // EXP_TM * EXP_TM
    ends = jnp.cumsum(padded)
    offs = ends - padded
    n_valid = jnp.maximum(ends[-1:] // EXP_TM, 1)
    starts = jnp.arange(MAX_TILES, dtype=jnp.int32) * EXP_TM
    tile_expert = jnp.minimum(jnp.sum((ends[None, :] <= starts[:, None]).astype(jnp.int32), axis=1), N_EXP - 1)
    tile_rows = jnp.clip(counts[tile_expert] - (starts - offs[tile_expert]), 0, EXP_TM)
    rec = metat.astype(jnp.int32)
    packed = ((rec[M_E1:M_E2 + 1] << 16) | rec[M_R1:M_R2 + 1]).T.reshape(N_ASSIGN)
    src = _order(packed, offs)
    ys = _experts(tile_expert, tile_rows, n_valid, src, h, wg, wu, wd, layer)
    return _combine(ys, meta, x_all, mod_l)


def _final_kernel(x_ref, g_ref, o_ref):
    o_ref[...] = _rms(x_ref[...], g_ref[...])


def _final_norm(x_all, g, blk0, n_rows):
    tb = 512
    return pl.pallas_call(
        _final_kernel,
        grid=(n_rows // tb,),
        in_specs=[pl.BlockSpec((tb, D), lambda i: (blk0 + i, 0)), _resident((1, D), lambda i: (0, 0))],
        out_specs=pl.BlockSpec((tb, D), lambda i: (i, 0)),
        out_shape=jax.ShapeDtypeStruct((n_rows, D), F32),
        compiler_params=_cp("arbitrary"),
        name="final_norm",
    )(x_all, g)


def _rope_tables():
    pos = jnp.arange(DEC_SEQ)
    row = (pos // GRID_W).astype(F32)
    col = (pos % GRID_W).astype(F32)
    n_freq = HD // 4
    inv = ROPE_THETA ** (-jnp.arange(n_freq, dtype=F32) / n_freq)
    ang = jnp.concatenate([row[:, None] * inv, col[:, None] * inv], axis=-1)
    cos = jnp.repeat(jnp.cos(ang), 2, axis=-1)
    sin = jnp.repeat(jnp.sin(ang), 2, axis=-1) * jnp.tile(jnp.array([-1.0, 1.0], F32), HD // 2)
    nblk = DEC_SEQ // QKV_TB
    cos_tab = jnp.concatenate([jnp.ones((1, QKV_TB, HD), F32), cos.reshape(nblk, QKV_TB, HD)], axis=0)
    sin_tab = jnp.concatenate([jnp.zeros((1, QKV_TB, HD), F32), sin.reshape(nblk, QKV_TB, HD)], axis=0)
    return cos_tab, sin_tab


def kernel(x_prompt, x_sample, state_gla, cache_k, cache_v, c, c_ctx, w_mod, b_mod, norm1_g, norm2_g,
           w_in_even, w_gate_up, b_gate_up, gla_norm_g, w_spatial, b_spatial, w_out_even,
           w_in_odd, q_norm_g, k_norm_g, w_out_odd, w_router_group, b_router_group,
           w_router_expert, b_router_expert, w_exp_gate, w_exp_up, w_exp_down, final_norm_g):
    x_all = jnp.concatenate([x_prompt.reshape(N_CTX, D), x_sample.reshape(N_LAT, D)], axis=0)
    cond8 = jnp.concatenate([c_ctx[None], c, jnp.zeros((3, D), F32)], axis=0)
    mod = _modulation(cond8, w_mod, b_mod)
    cos_tab, sin_tab = _rope_tables()
    zero_state = jnp.zeros((1, 2, QK_W, GLA_DV), F32)
    state_in = state_gla.reshape(DEC_BATCH, -1, 2, QK_W, GLA_DV)
    cache_k2 = cache_k.reshape(DEC_BATCH, -1, SEQ, KV_W)
    cache_v2 = cache_v.reshape(DEC_BATCH, -1, SEQ, KV_W)

    gla_states, ctx_k, ctx_v = [], [], []
    for l in range(DEPTH):
        i = l // 2
        n1g = norm1_g[l][None]
        if l % 2 == 0:
            w = w_in_even[i]
            win = jnp.concatenate([w[:, :1536], w[:, 1568:], w[:, 1536:1568], jnp.zeros((D, 96), F32)],
                                  axis=1).astype(BF16)
            wgu = jnp.zeros((128, 2 * QK_W), F32)
            wgu = wgu.at[0:GLA_RANK, 0:QK_W].set(w_gate_up[i, 0])
            wgu = wgu.at[GLA_RANK:2 * GLA_RANK, QK_W:].set(w_gate_up[i, 1]).astype(BF16)
            bgu = b_gate_up[i].reshape(1, 2 * QK_W)
            args = (mod[l], n1g, win, wgu, bgu, gla_norm_g[i][None], w_spatial[i].astype(BF16),
                    b_spatial[i].T, w_out_even[i].astype(BF16))
            x_all, st = _even_mixer(x_all, *args, zero_state, latent=False)
            gla_states.append(st)
            x_all, _ = _even_mixer(x_all, *args, state_in[:, i], latent=True)
        else:
            q, k, v = _qkv(x_all, mod[l], n1g, w_in_odd[i].astype(BF16), q_norm_g[i][None],
                           k_norm_g[i][None], cos_tab, sin_tab)
            ctx_k.append(k[:N_CTX].reshape(BATCH, SEQ, ATT_KV, HD))
            ctx_v.append(v[:N_CTX].reshape(BATCH, SEQ, ATT_KV, HD))
            wout = w_out_odd[i].astype(BF16)
            x_all = _attention(x_all, mod[l], q, k, v, wout)
            x_all = _attention(x_all, mod[l], q, k, v, wout, cache_k2, cache_v2, layer_i=i)
        wr = jnp.concatenate([w_router_expert[l], w_router_group[l],
                              jnp.zeros((D, 128 - N_EXP - MOE_GROUPS), F32)], axis=1)
        br = jnp.concatenate([b_router_expert[l], b_router_group[l],
                              jnp.zeros((128 - N_EXP - MOE_GROUPS,), F32)])[None]
        x_all = _moe(x_all, mod[l], norm2_g[l][None], wr, br, w_exp_gate, w_exp_up, w_exp_down, l)

    fg = final_norm_g[None]
    y_prompt = _final_norm(x_all, fg, 0, N_CTX).reshape(BATCH, SEQ, D)
    y_sample = _final_norm(x_all, fg, N_CTX // 512, N_LAT).reshape(DEC_BATCH, DEC_SEQ, D)
    new_state = jnp.stack(gla_states, axis=1).reshape(BATCH, -1, 2, GLA_HEADS, GLA_DK, GLA_DV)
    return (y_prompt, y_sample, new_state, jnp.stack(ctx_k, axis=1), jnp.stack(ctx_v, axis=1))
```

```python
import functools

import jax
import jax.numpy as jnp
import numpy as np
from jax import lax
from jax.experimental import pallas as pl
from jax.experimental.pallas import tpu as pltpu

F32 = jnp.float32
BF16 = jnp.bfloat16

D = 1024
BATCH, SEQ = 16, 256
DEC_BATCH, DEC_SEQ = 4, 1024
N_CTX = BATCH * SEQ
N_LAT = DEC_BATCH * DEC_SEQ
N_TOK = N_CTX + N_LAT
DEPTH = 4
EPS = 1e-6
GRID_W = 64
ROPE_THETA = 10000.0

GLA_HEADS, GLA_DK, GLA_DV, GLA_RANK, GLA_CHUNK, GLA_TAU = 4, 64, 128, 16, 64, 16.0
QK_W = GLA_HEADS * GLA_DK
V_W = GLA_HEADS * GLA_DV
GMLP_GROUPS, GMLP_DIM, GMLP_CHUNK = 4, 128, 128
GMLP_W = GMLP_GROUPS * GMLP_DIM
C_Q, C_K, C_V, C_G, C_U, C_VG, C_A = 0, 256, 512, 1024, 1536, 2048, 2560
EVEN_PACK = 2688

ATT_HEADS, ATT_KV, HD = 8, 2, 128
ATT_G = ATT_HEADS // ATT_KV
Q_W = ATT_HEADS * HD
KV_W = ATT_KV * HD

MOE_GROUPS, MOE_PER_GROUP = 4, 8
N_EXP = MOE_GROUPS * MOE_PER_GROUP
D_EXP = D // 4
NEG = -1e30

VMEM_LIMIT = 56 * 1024 * 1024


def _cp(*sem):
    return pltpu.CompilerParams(dimension_semantics=sem, vmem_limit_bytes=VMEM_LIMIT)


def _dot(a, b):
    return jnp.dot(a.astype(BF16), b.astype(BF16), preferred_element_type=F32)


def _dot_nt(a, b):
    return lax.dot_general(a.astype(BF16), b.astype(BF16), (((1,), (1,)), ((), ())),
                           preferred_element_type=F32)


def _dot_tn(a, b):
    return lax.dot_general(a.astype(BF16), b.astype(BF16), (((0,), (0,)), ((), ())),
                           preferred_element_type=F32)


def _rms(x, g):
    return x * lax.rsqrt(jnp.mean(x * x, axis=-1, keepdims=True) + EPS) * g


def _silu(x):
    return x * jax.nn.sigmoid(x)


def _gelu(x):
    return 0.5 * x * (1.0 + jnp.tanh(np.sqrt(2.0 / np.pi).astype(np.float32) * (x + 0.044715 * (x * x * x))))


def _log_sigmoid(z):
    return jnp.minimum(z, 0.0) - jnp.log(1.0 + jnp.exp(-jnp.abs(z)))


def _rows_to_tiles(ref, x):
    rows = x.shape[0]
    for j in range(D // 128):
        ref[pl.ds(j, rows, stride=8), :] = x[:, j * 128:(j + 1) * 128]


def _tiles_to_rows(ref, rows):
    return jnp.concatenate([ref[pl.ds(j, rows, stride=8), :] for j in range(D // 128)], axis=1)


def _tile_of(ref, row8):
    return ref.at[pl.ds(pl.multiple_of(row8, 8), 8), :]


def _resident(shape, index_map):
    return pl.BlockSpec(shape, index_map, pipeline_mode=pl.Buffered(1))


def _mod_kernel(cond_ref, w_ref, b_ref, o_ref):
    c = cond_ref[...]
    o_ref[...] = jnp.dot(_silu(c), w_ref[...], precision=lax.Precision.HIGHEST,
                         preferred_element_type=F32) + b_ref[...]


def _modulation(cond8, w_mod, b_mod):
    tn = 1024
    out = pl.pallas_call(
        _mod_kernel,
        grid=(DEPTH, 6 * D // tn),
        in_specs=[
            pl.BlockSpec((8, D), lambda l, j: (0, 0)),
            pl.BlockSpec((None, D, tn), lambda l, j: (l, 0, j)),
            pl.BlockSpec((None, 1, tn), lambda l, j: (l, 0, j)),
        ],
        out_specs=pl.BlockSpec((None, 8, tn), lambda l, j: (l, 0, j)),
        out_shape=jax.ShapeDtypeStruct((DEPTH, 8, 6 * D), F32),
        compiler_params=_cp("arbitrary", "arbitrary"),
        name="adaln_mod",
    )(cond8, w_mod, b_mod.reshape(DEPTH, 1, 6 * D))
    return out.reshape(DEPTH, 8, 6, D)


def _even_kernel(x_ref, mod_ref, n1g_ref, win_ref, wgu_ref, bgu_ref, glag_ref, ws_ref, bs_ref,
                 wout_ref, s0_ref, xo_ref, st_ref, proj, la, o_f, o_b, st_scr, *, T):
    n_chunks = T // GLA_CHUNK
    shift, scale, gate = mod_ref[0:1, :], mod_ref[1:2, :], mod_ref[2:3, :]
    RB = 128

    def proj_body(r, carry):
        r0 = pl.multiple_of(r * RB, RB)
        h = _rms(x_ref[pl.ds(r0, RB), :], n1g_ref[...]) * (1.0 + scale) + shift
        p = _dot(h, win_ref[...])
        proj[pl.ds(r0, RB), :] = p
        z = _dot(p[:, C_A:C_A + 128], wgu_ref[...]) + bgu_ref[...]
        la[pl.ds(r0, RB), :] = _log_sigmoid(z) * (1.0 / GLA_TAU)
        return carry

    lax.fori_loop(0, T // RB, proj_body, 0)

    st_scr[0] = s0_ref[0].T
    st_scr[1] = s0_ref[1].T

    ci = lax.broadcasted_iota(jnp.int32, (GLA_CHUNK, GLA_CHUNK), 0)
    cj = lax.broadcasted_iota(jnp.int32, (GLA_CHUNK, GLA_CHUNK), 1)
    tri = (jnp.where(ci >= cj, 1.0, 0.0).astype(BF16), jnp.where(ci <= cj, 1.0, 0.0).astype(BF16))
    ai = lax.broadcasted_iota(jnp.int32, (GLA_HEADS * GLA_CHUNK, GLA_CHUNK), 0) % GLA_CHUNK
    aj = lax.broadcasted_iota(jnp.int32, (GLA_HEADS * GLA_CHUNK, GLA_CHUNK), 1)
    amask = (ai >= aj, ai <= aj)
    lane_head = lax.broadcasted_iota(jnp.int32, (1, QK_W), 1) // GLA_DK
    hmask = [jnp.where(lane_head == h, 1.0, 0.0) for h in range(GLA_HEADS)]

    def chunk_body(i, carry):
        for d in range(2):
            c = i if d == 0 else n_chunks - 1 - i
            r0 = pl.multiple_of(c * GLA_CHUNK, GLA_CHUNK)
            q = proj[pl.ds(r0, GLA_CHUNK), C_Q:C_Q + QK_W] * (GLA_DK ** -0.5)
            k = proj[pl.ds(r0, GLA_CHUNK), C_K:C_K + QK_W]
            v = proj[pl.ds(r0, GLA_CHUNK), C_V:C_V + V_W]
            lac = la[pl.ds(r0, GLA_CHUNK), d * QK_W:(d + 1) * QK_W]
            hi = lac.astype(BF16)
            lo = (lac - hi.astype(F32)).astype(BF16)
            b = (jnp.dot(tri[d], hi, preferred_element_type=F32)
                 + jnp.dot(tri[d], lo, preferred_element_type=F32))
            bend = b[GLA_CHUNK - 1:GLA_CHUNK, :] if d == 0 else b[0:1, :]
            qe = q * jnp.exp(b)
            ke = k * jnp.exp(-b)
            kd = k * jnp.exp(bend - b)
            st = st_scr[d]
            qstack = jnp.concatenate([qe * hmask[h] for h in range(GLA_HEADS)], axis=0).astype(BF16)
            att = jnp.where(amask[d], _dot_nt(qstack, ke), 0.0)
            inter = _dot_nt(qstack, st)
            outs = []
            for h in range(GLA_HEADS):
                rows = slice(h * GLA_CHUNK, (h + 1) * GLA_CHUNK)
                outs.append(_dot(att[rows], v[:, h * GLA_DV:(h + 1) * GLA_DV]) + inter[rows])
            o = jnp.concatenate(outs, axis=1)
            if d == 0:
                o_f[pl.ds(r0, GLA_CHUNK), :] = o
            else:
                o_b[pl.ds(r0, GLA_CHUNK), :] = o
            vstack = jnp.concatenate([v[:, h * GLA_DV:(h + 1) * GLA_DV] for h in range(GLA_HEADS)], axis=0)
            kstack = jnp.concatenate([kd * hmask[h] for h in range(GLA_HEADS)], axis=0)
            st_scr[d] = st * jnp.exp(bend) + _dot_tn(vstack, kstack)
        return carry

    lax.fori_loop(0, n_chunks, chunk_body, 0)
    st_ref[0] = st_scr[0].T
    st_ref[1] = st_scr[1].T

    def out_body(r, carry):
        r0 = pl.multiple_of(r * RB, RB)
        osum = o_f[pl.ds(r0, RB), :] + o_b[pl.ds(r0, RB), :]
        g = proj[pl.ds(r0, RB), C_G:C_G + V_W]
        u = proj[pl.ds(r0, RB), C_U:C_U + GMLP_W]
        vg = _gelu(proj[pl.ds(r0, RB), C_VG:C_VG + GMLP_W])
        parts = []
        for h in range(GLA_HEADS):
            oh = osum[:, h * GLA_DV:(h + 1) * GLA_DV]
            parts.append(_rms(oh, glag_ref[...]) * _silu(g[:, h * GLA_DV:(h + 1) * GLA_DV]))
        for gi in range(GMLP_GROUPS):
            vc = vg[:, gi * GMLP_DIM:(gi + 1) * GMLP_DIM]
            vc = vc - jnp.mean(vc, axis=-1, keepdims=True)
            vn = vc * lax.rsqrt(jnp.mean(vc * vc, axis=-1, keepdims=True) + EPS)
            sg = _dot(ws_ref[gi], vn) + bs_ref[:, gi:gi + 1]
            parts.append(_gelu(u[:, gi * GMLP_DIM:(gi + 1) * GMLP_DIM]) * sg)
        mix = jnp.concatenate(parts, axis=1)
        y = _dot(mix, wout_ref[...])
        xo_ref[pl.ds(r0, RB), :] = x_ref[pl.ds(r0, RB), :] + gate * y
        return carry

    lax.fori_loop(0, T // RB, out_body, 0)


def _even_mixer(x_all, mod_l, n1g, win, wgu, bgu, glag, ws, bs, wout, s0, *, latent):
    if latent:
        T, nseq, blk0 = DEC_SEQ, DEC_BATCH, N_CTX // DEC_SEQ
        cond = lambda i: 1 + i
        s0_spec = pl.BlockSpec((None, 2, QK_W, GLA_DV), lambda i: (i, 0, 0, 0))
    else:
        T, nseq, blk0 = SEQ, BATCH, 0
        cond = lambda i: 0
        s0_spec = pl.BlockSpec((None, 2, QK_W, GLA_DV), lambda i: (0, 0, 0, 0))
    const2 = lambda i: (0, 0)
    x_new, states = pl.pallas_call(
        functools.partial(_even_kernel, T=T),
        grid=(nseq,),
        in_specs=[
            pl.BlockSpec((T, D), lambda i: (blk0 + i, 0)),
            pl.BlockSpec((None, 6, D), lambda i: (cond(i), 0, 0)),
            _resident((1, D), const2),
            _resident((D, EVEN_PACK), const2),
            _resident((128, 2 * QK_W), const2),
            _resident((1, 2 * QK_W), const2),
            _resident((1, GLA_DV), const2),
            _resident((GMLP_GROUPS, GMLP_CHUNK, GMLP_CHUNK), lambda i: (0, 0, 0)),
            _resident((GMLP_CHUNK, GMLP_GROUPS), const2),
            _resident((D, D), const2),
            s0_spec,
        ],
        out_specs=[
            pl.BlockSpec((T, D), lambda i: (blk0 + i, 0)),
            pl.BlockSpec((None, 2, QK_W, GLA_DV), lambda i: (i, 0, 0, 0)),
        ],
        out_shape=[
            jax.ShapeDtypeStruct((N_TOK, D), F32),
            jax.ShapeDtypeStruct((nseq, 2, QK_W, GLA_DV), F32),
        ],
        scratch_shapes=[
            pltpu.VMEM((T, EVEN_PACK), F32),
            pltpu.VMEM((T, 2 * QK_W), F32),
            pltpu.VMEM((T, V_W), F32),
            pltpu.VMEM((T, V_W), F32),
            pltpu.VMEM((2, GLA_DV, QK_W), F32),
        ],
        input_output_aliases={0: 0},
        compiler_params=_cp("arbitrary"),
        name="even_mixer_latent" if latent else "even_mixer_context",
    )(x_all, mod_l, n1g, win, wgu, bgu, glag, ws, bs, wout, s0)
    return x_new, states


QKV_TB = 512


def _qkv_kernel(x_ref, mod_ref, n1g_ref, win_ref, gq_ref, gk_ref, cos_ref, sin_ref, q_ref, k_ref, v_ref):
    shift, scale = mod_ref[0:1, :], mod_ref[1:2, :]
    h = _rms(x_ref[...], n1g_ref[...]) * (1.0 + scale) + shift
    p = _dot(h, win_ref[...])
    cos, sin = cos_ref[...], sin_ref[...]
    even_lane = lax.broadcasted_iota(jnp.int32, (1, HD), 1) % 2 == 0

    def norm_rope(xh, g):
        xn = _rms(xh, g)
        swapped = jnp.where(even_lane, pltpu.roll(xn, HD - 1, axis=1), pltpu.roll(xn, 1, axis=1))
        return xn * cos + swapped * sin

    for hh in range(ATT_HEADS):
        q_ref[:, hh * HD:(hh + 1) * HD] = norm_rope(p[:, hh * HD:(hh + 1) * HD], gq_ref[...]).astype(BF16)
    for hh in range(ATT_KV):
        k_ref[:, hh * HD:(hh + 1) * HD] = norm_rope(p[:, Q_W + hh * HD:Q_W + (hh + 1) * HD], gk_ref[...])
    v_ref[...] = p[:, Q_W + KV_W:]


def _qkv(x_all, mod_l, n1g, win, gq, gk, cos_tab, sin_tab):
    nb_ctx = N_CTX // QKV_TB
    per_seq = DEC_SEQ // QKV_TB
    cond = lambda i: jnp.where(i < nb_ctx, 0, 1 + (i - nb_ctx) // per_seq)
    tab = lambda i: jnp.where(i < nb_ctx, 0, 1 + (i - nb_ctx) % per_seq)
    const2 = lambda i: (0, 0)
    return pl.pallas_call(
        _qkv_kernel,
        grid=(N_TOK // QKV_TB,),
        in_specs=[
            pl.BlockSpec((QKV_TB, D), lambda i: (i, 0)),
            pl.BlockSpec((None, 6, D), lambda i: (cond(i), 0, 0)),
            _resident((1, D), const2),
            _resident((D, Q_W + 2 * KV_W), const2),
            _resident((1, HD), const2),
            _resident((1, HD), const2),
            pl.BlockSpec((None, QKV_TB, HD), lambda i: (tab(i), 0, 0)),
            pl.BlockSpec((None, QKV_TB, HD), lambda i: (tab(i), 0, 0)),
        ],
        out_specs=[
            pl.BlockSpec((QKV_TB, Q_W), lambda i: (i, 0)),
            pl.BlockSpec((QKV_TB, KV_W), lambda i: (i, 0)),
            pl.BlockSpec((QKV_TB, KV_W), lambda i: (i, 0)),
        ],
        out_shape=[
            jax.ShapeDtypeStruct((N_TOK, Q_W), BF16),
            jax.ShapeDtypeStruct((N_TOK, KV_W), F32),
            jax.ShapeDtypeStruct((N_TOK, KV_W), F32),
        ],
        compiler_params=_cp("arbitrary"),
        name="odd_qkv",
    )(x_all, mod_l, n1g, win, gq, gk, cos_tab, sin_tab)


ATT_TQ = 256


def _attn_kernel(*refs, n_kv):
    q_ref = refs[0]
    kv_refs = refs[1:1 + 2 * n_kv]
    x_ref, mod_ref, wout_ref, xo_ref, att_scr = refs[1 + 2 * n_kv:]
    gate = mod_ref[2:3, :]
    for kh in range(ATT_KV):
        ks = [kv_refs[2 * s][:, kh * HD:(kh + 1) * HD].astype(BF16) for s in range(n_kv)]
        vs = [kv_refs[2 * s + 1][:, kh * HD:(kh + 1) * HD].astype(BF16) for s in range(n_kv)]
        for g in range(ATT_G):
            hh = kh * ATT_G + g
            qh = q_ref[:, hh * HD:(hh + 1) * HD]
            ss = [_dot_nt(qh, kk) * (HD ** -0.5) for kk in ks]
            m = ss[0].max(axis=-1, keepdims=True)
            for s in ss[1:]:
                m = jnp.maximum(m, s.max(axis=-1, keepdims=True))
            ps = [jnp.exp(s - m) for s in ss]
            den = ps[0].sum(axis=-1, keepdims=True)
            for p in ps[1:]:
                den = den + p.sum(axis=-1, keepdims=True)
            o = _dot(ps[0], vs[0])
            for p, vv in zip(ps[1:], vs[1:]):
                o = o + _dot(p, vv)
            att_scr[:, hh * HD:(hh + 1) * HD] = o / den
    y = _dot(att_scr[...], wout_ref[...])
    xo_ref[...] = x_ref[...] + gate * y


def _attention(x_all, mod_l, q, k, v, wout, cache_k=None, cache_v=None, layer_i=0):
    latent = cache_k is not None
    const2 = lambda *a: (0, 0)
    if latent:
        nq = DEC_SEQ // ATT_TQ
        row_blk = lambda b, j: (N_CTX // ATT_TQ + b * nq + j, 0)
        grid = (DEC_BATCH, nq)
        kv_specs = [
            pl.BlockSpec((None, None, SEQ, KV_W), lambda b, j: (b, layer_i, 0, 0)),
            pl.BlockSpec((None, None, SEQ, KV_W), lambda b, j: (b, layer_i, 0, 0)),
            pl.BlockSpec((DEC_SEQ, KV_W), lambda b, j: (N_CTX // DEC_SEQ + b, 0)),
            pl.BlockSpec((DEC_SEQ, KV_W), lambda b, j: (N_CTX // DEC_SEQ + b, 0)),
        ]
        kv_args = (cache_k, cache_v, k, v)
        mod_spec = pl.BlockSpec((None, 6, D), lambda b, j: (1 + b, 0, 0))
        sem = ("arbitrary", "arbitrary")
        n_kv = 2
    else:
        row_blk = lambda i: (i, 0)
        grid = (BATCH,)
        kv_specs = [pl.BlockSpec((SEQ, KV_W), row_blk), pl.BlockSpec((SEQ, KV_W), row_blk)]
        kv_args = (k, v)
        mod_spec = pl.BlockSpec((None, 6, D), lambda i: (0, 0, 0))
        sem = ("arbitrary",)
        n_kv = 1
    n_in = 1 + len(kv_args)
    return pl.pallas_call(
        functools.partial(_attn_kernel, n_kv=n_kv),
        grid=grid,
        in_specs=[pl.BlockSpec((ATT_TQ, Q_W), row_blk)] + kv_specs + [
            pl.BlockSpec((ATT_TQ, D), row_blk),
            mod_spec,
            _resident((D, D), const2),
        ],
        out_specs=pl.BlockSpec((ATT_TQ, D), row_blk),
        out_shape=jax.ShapeDtypeStruct((N_TOK, D), F32),
        scratch_shapes=[pltpu.VMEM((ATT_TQ, Q_W), F32)],
        input_output_aliases={n_in: 0},
        compiler_params=_cp(*sem),
        name="attention_latent" if latent else "attention_context",
    )(q, *kv_args, x_all, mod_l, wout)


ROUTE_TB = 512
HALF_TOK = N_TOK // 2
M_E1, M_E2, M_G1, M_G2, M_R1, M_R2 = 0, 1, 2, 3, 4, 5


def _router_kernel(x_ref, mod_ref, n2g_ref, wr_ref, br_ref, h_ref, meta_ref, metat_ref, cnt_ref, run):
    @pl.when(pl.program_id(0) % (HALF_TOK // ROUTE_TB) == 0)
    def _():
        run[...] = jnp.zeros_like(run)

    shift, scale = mod_ref[3:4, :], mod_ref[4:5, :]
    h = _rms(x_ref[...], n2g_ref[...]) * (1.0 + scale) + shift
    _rows_to_tiles(h_ref, h)
    logits = jnp.dot(h, wr_ref[...], precision=lax.Precision.HIGHEST, preferred_element_type=F32) + br_ref[...]
    lane = lax.broadcasted_iota(jnp.int32, logits.shape, 1).astype(F32)
    big = 1e4

    def first_argmax(vals):
        m = vals.max(axis=-1, keepdims=True)
        return m, jnp.where(vals == m, lane, big).min(axis=-1, keepdims=True)

    gl = jnp.where((lane >= N_EXP) & (lane < N_EXP + MOE_GROUPS), logits, NEG)
    gmax, glane = first_argmax(gl)
    g_p = 1.0 / jnp.exp(gl - gmax).sum(axis=-1, keepdims=True)
    lo = (glane - N_EXP) * MOE_PER_GROUP
    el = jnp.where((lane >= lo) & (lane < lo + MOE_PER_GROUP), logits, NEG)
    m1, i1 = first_argmax(el)
    m2, i2 = first_argmax(jnp.where(lane == i1, NEG, el))
    t = jnp.exp(m2 - m1)
    w1 = 1.0 / (1.0 + t)
    sel1, sel2 = lane == i1, lane == i2
    onehot = jnp.where(sel1 | sel2, 1.0, 0.0)
    ri = lax.broadcasted_iota(jnp.int32, (ROUTE_TB, ROUTE_TB), 0)
    rj = lax.broadcasted_iota(jnp.int32, (ROUTE_TB, ROUTE_TB), 1)
    before = _dot(jnp.where(ri > rj, 1.0, 0.0), onehot) + run[...]
    r1 = jnp.where(sel1, before, 0.0).sum(axis=-1, keepdims=True)
    r2 = jnp.where(sel2, before, 0.0).sum(axis=-1, keepdims=True)
    run[...] += onehot.sum(axis=0, keepdims=True)
    cnt_ref[...] = run[...]
    meta = jnp.zeros_like(logits)
    for j, val in enumerate([i1, i2, w1 * g_p, (t * w1) * g_p, r1, r2]):
        meta = jnp.where(lane == j, val, meta)
    meta_ref[...] = meta
    metat_ref[...] = meta.T[0:8, :]


def _router(x_all, mod_l, n2g, wr, br):
    nb_ctx = N_CTX // ROUTE_TB
    per_seq = DEC_SEQ // ROUTE_TB
    cond = lambda i: jnp.where(i < nb_ctx, 0, 1 + (i - nb_ctx) // per_seq)
    const2 = lambda i: (0, 0)
    return pl.pallas_call(
        _router_kernel,
        grid=(N_TOK // ROUTE_TB,),
        in_specs=[
            pl.BlockSpec((ROUTE_TB, D), lambda i: (i, 0)),
            pl.BlockSpec((None, 6, D), lambda i: (cond(i), 0, 0)),
            _resident((1, D), const2),
            _resident((D, 128), const2),
            _resident((1, 128), const2),
        ],
        out_specs=[
            pl.BlockSpec((ROUTE_TB * 8, 128), lambda i: (i, 0)),
            pl.BlockSpec((ROUTE_TB, 128), lambda i: (i, 0)),
            pl.BlockSpec((8, ROUTE_TB), lambda i: (0, i)),
            pl.BlockSpec((None, 1, 128), lambda i: (i // (HALF_TOK // ROUTE_TB), 0, 0)),
        ],
        out_shape=[
            jax.ShapeDtypeStruct((N_TOK * 8, 128), F32),
            jax.ShapeDtypeStruct((N_TOK, 128), F32),
            jax.ShapeDtypeStruct((8, N_TOK), F32),
            jax.ShapeDtypeStruct((2, 1, 128), F32),
        ],
        scratch_shapes=[pltpu.VMEM((1, 128), F32)],
        compiler_params=_cp("arbitrary"),
        name="moe_router",
    )(x_all, mod_l, n2g, wr, br)


EXP_TM = 128
N_ASSIGN = 2 * N_TOK
N_GROUPS = 2 * N_EXP
MAX_TILES = N_ASSIGN // EXP_TM + N_GROUPS
N_SORTED = MAX_TILES * EXP_TM
ORDER_BLK = 4096


def _order_kernel(packed_ref, gate_ref, offs_ref, src_ref, gs_ref):
    i = pl.program_id(0)
    half = i // (N_ASSIGN // 2 // ORDER_BLK)
    local = i * (ORDER_BLK // 2) - half * HALF_TOK

    def body(a, carry):
        v = packed_ref[a]
        p = offs_ref[half * N_EXP + (v >> 16)] + (v & 0xFFFF)
        src_ref[p] = (local + (a >> 1)) * 8
        gs_ref[p] = gate_ref[a]
        return carry

    lax.fori_loop(0, ORDER_BLK, body, 0, unroll=8)


def _order(packed, gates, offs):
    return pl.pallas_call(
        _order_kernel,
        grid=(N_ASSIGN // ORDER_BLK,),
        in_specs=[
            pl.BlockSpec((ORDER_BLK,), lambda i: (i,), memory_space=pltpu.SMEM),
            pl.BlockSpec((ORDER_BLK,), lambda i: (i,), memory_space=pltpu.SMEM),
            pl.BlockSpec(memory_space=pltpu.SMEM),
        ],
        out_specs=[pl.BlockSpec(memory_space=pltpu.SMEM), pl.BlockSpec(memory_space=pltpu.SMEM)],
        out_shape=[jax.ShapeDtypeStruct((N_SORTED,), jnp.int32), jax.ShapeDtypeStruct((N_SORTED,), F32)],
        compiler_params=_cp("arbitrary"),
        name="moe_order",
    )(packed, gates, offs)


F_FIRST, F_LAST, F_NEW, F_HALF = 1, 2, 4, 8
ACC_TOK = HALF_TOK + 64
DUMMY8 = HALF_TOK * 8
ROW_GROUP = 8


def _experts_kernel(te_ref, rows_ref, flag_ref, nv_ref, src_ref, gs_ref, h_hbm, wg_ref, wu_ref, wd_ref, out_hbm,
                    h_res, acc, xbuf, ybuf, wgb, wub, wdb, sem):
    t = pl.program_id(0)

    @pl.when(t < nv_ref[0])
    def _():
        flags = flag_ref[t]
        n_rows = rows_ref[t]
        rows0 = pl.multiple_of(((flags // F_HALF) % 2) * (HALF_TOK * 8), 8)

        @pl.when((flags & F_FIRST) != 0)
        def _():
            cp = pltpu.make_async_copy(h_hbm.at[pl.ds(rows0, HALF_TOK * 8), :], h_res, sem)
            cp.start()

            def zero(i, carry):
                acc[pl.ds(pl.multiple_of(i * 512, 512), 512), :] = jnp.zeros((512, 128), F32)
                return carry

            lax.fori_loop(0, ACC_TOK * 8 // 512, zero, 0)
            cp.wait()

        @pl.when((flags & F_NEW) != 0)
        def _():
            wgb[...] = wg_ref[...].astype(BF16)
            wub[...] = wu_ref[...].astype(BF16)
            wdb[...] = wd_ref[...].astype(BF16)

        def gather(g, carry):
            for j in range(ROW_GROUP):
                r = g * ROW_GROUP + j
                xbuf[pl.ds(pl.multiple_of(r * 8, 8), 8), :] = _tile_of(h_res, src_ref[jnp.minimum(r, n_rows - 1)])[...]
            return carry

        lax.fori_loop(0, EXP_TM // ROW_GROUP, gather, 0)
        x = _tiles_to_rows(xbuf, EXP_TM).astype(BF16)
        hid = _silu(_dot(x, wgb[...])) * _dot(x, wub[...])
        _rows_to_tiles(ybuf, _dot(hid, wdb[...]))

        def accumulate(g, carry):
            targets, values = [], []
            for j in range(ROW_GROUP):
                r = g * ROW_GROUP + j
                rc = jnp.minimum(r, n_rows - 1)
                target = _tile_of(acc, jnp.where(r < n_rows, src_ref[rc], DUMMY8))
                targets.append(target)
                values.append(target[...] + gs_ref[rc] * ybuf[pl.ds(pl.multiple_of(r * 8, 8), 8), :])
            for target, value in zip(targets, values):
                target[...] = value
            return carry

        lax.fori_loop(0, EXP_TM // ROW_GROUP, accumulate, 0)

        @pl.when((flags & F_LAST) != 0)
        def _():
            cp = pltpu.make_async_copy(acc.at[pl.ds(0, HALF_TOK * 8), :],
                                       out_hbm.at[pl.ds(rows0, HALF_TOK * 8), :], sem)
            cp.start()
            cp.wait()


def _experts(tile_expert, tile_rows, tile_flags, n_valid, src, gs, h, wg, wu, wd, layer):
    tile = lambda t, te, rows, fl, nv: jnp.minimum(t, nv[0] - 1)
    wmap = lambda t, te, rows, fl, nv: (layer, te[tile(t, te, rows, fl, nv)], 0, 0)
    smap = lambda t, te, rows, fl, nv: (tile(t, te, rows, fl, nv),)
    return pl.pallas_call(
        _experts_kernel,
        grid_spec=pltpu.PrefetchScalarGridSpec(
            num_scalar_prefetch=4,
            grid=(MAX_TILES,),
            in_specs=[
                pl.BlockSpec((EXP_TM,), smap, memory_space=pltpu.SMEM),
                pl.BlockSpec((EXP_TM,), smap, memory_space=pltpu.SMEM),
                pl.BlockSpec(memory_space=pl.ANY),
                pl.BlockSpec((None, None, D, D_EXP), wmap),
                pl.BlockSpec((None, None, D, D_EXP), wmap),
                pl.BlockSpec((None, None, D_EXP, D), wmap),
            ],
            out_specs=pl.BlockSpec(memory_space=pl.ANY),
            scratch_shapes=[
                pltpu.VMEM((HALF_TOK * 8, 128), F32),
                pltpu.VMEM((ACC_TOK * 8, 128), F32),
                pltpu.VMEM((EXP_TM * 8, 128), F32),
                pltpu.VMEM((EXP_TM * 8, 128), F32),
                pltpu.VMEM((D, D_EXP), BF16),
                pltpu.VMEM((D, D_EXP), BF16),
                pltpu.VMEM((D_EXP, D), BF16),
                pltpu.SemaphoreType.DMA,
            ],
        ),
        out_shape=jax.ShapeDtypeStruct((N_TOK * 8, 128), F32),
        compiler_params=_cp("arbitrary"),
        name="moe_experts",
    )(tile_expert, tile_rows, tile_flags, n_valid, src, gs, h, wg, wu, wd)


COMBINE_TB = 256


def _combine_kernel(y_ref, x_ref, mod_ref, xo_ref):
    xo_ref[...] = x_ref[...] + mod_ref[5:6, :] * _tiles_to_rows(y_ref, COMBINE_TB)


def _combine(ys, x_all, mod_l):
    nb_ctx = N_CTX // COMBINE_TB
    per_seq = DEC_SEQ // COMBINE_TB
    cond = lambda i: jnp.where(i < nb_ctx, 0, 1 + (i - nb_ctx) // per_seq)
    return pl.pallas_call(
        _combine_kernel,
        grid=(N_TOK // COMBINE_TB,),
        in_specs=[
            pl.BlockSpec((COMBINE_TB * 8, 128), lambda i: (i, 0)),
            pl.BlockSpec((COMBINE_TB, D), lambda i: (i, 0)),
            pl.BlockSpec((None, 6, D), lambda i: (cond(i), 0, 0)),
        ],
        out_specs=pl.BlockSpec((COMBINE_TB, D), lambda i: (i, 0)),
        out_shape=jax.ShapeDtypeStruct((N_TOK, D), F32),
        compiler_params=_cp("arbitrary"),
        name="moe_combine",
    )(ys, x_all, mod_l)


def _moe(x_all, mod_l, n2g, wr, br, wg, wu, wd, layer):
    h, meta, metat, cnt = _router(x_all, mod_l, n2g, wr, br)
    counts = cnt[:, 0, :N_EXP].astype(jnp.int32).reshape(N_GROUPS)
    padded = (counts + EXP_TM - 1) // EXP_TM * EXP_TM
    ends = jnp.cumsum(padded)
    offs = ends - padded
    n_valid = jnp.maximum(ends[-1:] // EXP_TM, 1)
    tiles = jnp.arange(MAX_TILES, dtype=jnp.int32)
    starts = tiles * EXP_TM
    group = jnp.minimum(jnp.sum((ends[None, :] <= starts[:, None]).astype(jnp.int32), axis=1), N_GROUPS - 1)
    tile_rows = jnp.clip(counts[group] - (starts - offs[group]), 0, EXP_TM)
    half = group // N_EXP
    prev_group = jnp.concatenate([jnp.full((1,), -1, jnp.int32), group[:-1]])
    next_half = jnp.concatenate([half[1:], jnp.full((1,), 2, jnp.int32)])
    first = (prev_group // N_EXP != half) | (tiles == 0)
    last = (next_half != half) | (tiles == n_valid - 1)
    tile_flags = (F_FIRST * first + F_LAST * last + F_NEW * (first | (prev_group != group)) + F_HALF * half)
    rec = metat.astype(jnp.int32)
    packed = ((rec[M_E1:M_E2 + 1] << 16) | rec[M_R1:M_R2 + 1]).T.reshape(N_ASSIGN)
    gates = metat[M_G1:M_G2 + 1].T.reshape(N_ASSIGN)
    src, gs = _order(packed, gates, offs)
    ys = _experts(group % N_EXP, tile_rows, tile_flags.astype(jnp.int32), n_valid, src, gs, h, wg, wu, wd, layer)
    return _combine(ys, x_all, mod_l)


def _final_kernel(x_ref, g_ref, o_ref):
    o_ref[...] = _rms(x_ref[...], g_ref[...])


def _final_norm(x_all, g, blk0, n_rows):
    tb = 512
    return pl.pallas_call(
        _final_kernel,
        grid=(n_rows // tb,),
        in_specs=[pl.BlockSpec((tb, D), lambda i: (blk0 + i, 0)), _resident((1, D), lambda i: (0, 0))],
        out_specs=pl.BlockSpec((tb, D), lambda i: (i, 0)),
        out_shape=jax.ShapeDtypeStruct((n_rows, D), F32),
        compiler_params=_cp("arbitrary"),
        name="final_norm",
    )(x_all, g)


def _rope_tables():
    pos = jnp.arange(DEC_SEQ)
    row = (pos // GRID_W).astype(F32)
    col = (pos % GRID_W).astype(F32)
    n_freq = HD // 4
    inv = ROPE_THETA ** (-jnp.arange(n_freq, dtype=F32) / n_freq)
    ang = jnp.concatenate([row[:, None] * inv, col[:, None] * inv], axis=-1)
    cos = jnp.repeat(jnp.cos(ang), 2, axis=-1)
    sin = jnp.repeat(jnp.sin(ang), 2, axis=-1) * jnp.tile(jnp.array([-1.0, 1.0], F32), HD // 2)
    nblk = DEC_SEQ // QKV_TB
    cos_tab = jnp.concatenate([jnp.ones((1, QKV_TB, HD), F32), cos.reshape(nblk, QKV_TB, HD)], axis=0)
    sin_tab = jnp.concatenate([jnp.zeros((1, QKV_TB, HD), F32), sin.reshape(nblk, QKV_TB, HD)], axis=0)
    return cos_tab, sin_tab


def kernel(x_prompt, x_sample, state_gla, cache_k, cache_v, c, c_ctx, w_mod, b_mod, norm1_g, norm2_g,
           w_in_even, w_gate_up, b_gate_up, gla_norm_g, w_spatial, b_spatial, w_out_even,
           w_in_odd, q_norm_g, k_norm_g, w_out_odd, w_router_group, b_router_group,
           w_router_expert, b_router_expert, w_exp_gate, w_exp_up, w_exp_down, final_norm_g):
    x_all = jnp.concatenate([x_prompt.reshape(N_CTX, D), x_sample.reshape(N_LAT, D)], axis=0)
    cond8 = jnp.concatenate([c_ctx[None], c, jnp.zeros((3, D), F32)], axis=0)
    mod = _modulation(cond8, w_mod, b_mod)
    cos_tab, sin_tab = _rope_tables()
    zero_state = jnp.zeros((1, 2, QK_W, GLA_DV), F32)
    state_in = state_gla.reshape(DEC_BATCH, -1, 2, QK_W, GLA_DV)
    cache_k2 = cache_k.reshape(DEC_BATCH, -1, SEQ, KV_W)
    cache_v2 = cache_v.reshape(DEC_BATCH, -1, SEQ, KV_W)

    gla_states, ctx_k, ctx_v = [], [], []
    for l in range(DEPTH):
        i = l // 2
        n1g = norm1_g[l][None]
        if l % 2 == 0:
            w = w_in_even[i]
            win = jnp.concatenate([w[:, :1536], w[:, 1568:], w[:, 1536:1568], jnp.zeros((D, 96), F32)],
                                  axis=1).astype(BF16)
            wgu = jnp.zeros((128, 2 * QK_W), F32)
            wgu = wgu.at[0:GLA_RANK, 0:QK_W].set(w_gate_up[i, 0])
            wgu = wgu.at[GLA_RANK:2 * GLA_RANK, QK_W:].set(w_gate_up[i, 1]).astype(BF16)
            bgu = b_gate_up[i].reshape(1, 2 * QK_W)
            args = (mod[l], n1g, win, wgu, bgu, gla_norm_g[i][None], w_spatial[i].astype(BF16),
                    b_spatial[i].T, w_out_even[i].astype(BF16))
            x_all, st = _even_mixer(x_all, *args, zero_state, latent=False)
            gla_states.append(st)
            x_all, _ = _even_mixer(x_all, *args, state_in[:, i], latent=True)
        else:
            q, k, v = _qkv(x_all, mod[l], n1g, w_in_odd[i].astype(BF16), q_norm_g[i][None],
                           k_norm_g[i][None], cos_tab, sin_tab)
            ctx_k.append(k[:N_CTX].reshape(BATCH, SEQ, ATT_KV, HD))
            ctx_v.append(v[:N_CTX].reshape(BATCH, SEQ, ATT_KV, HD))
            wout = w_out_odd[i].astype(BF16)
            x_all = _attention(x_all, mod[l], q, k, v, wout)
            x_all = _attention(x_all, mod[l], q, k, v, wout, cache_k2, cache_v2, layer_i=i)
        wr = jnp.concatenate([w_router_expert[l], w_router_group[l],
                              jnp.zeros((D, 128 - N_EXP - MOE_GROUPS), F32)], axis=1)
        br = jnp.concatenate([b_router_expert[l], b_router_group[l],
                              jnp.zeros((128 - N_EXP - MOE_GROUPS,), F32)])[None]
        x_all = _moe(x_all, mod[l], norm2_g[l][None], wr, br, w_exp_gate, w_exp_up, w_exp_down, l)

    fg = final_norm_g[None]
    y_prompt = _final_norm(x_all, fg, 0, N_CTX).reshape(BATCH, SEQ, D)
    y_sample = _final_norm(x_all, fg, N_CTX // 512, N_LAT).reshape(DEC_BATCH, DEC_SEQ, D)
    new_state = jnp.stack(gla_states, axis=1).reshape(BATCH, -1, 2, GLA_HEADS, GLA_DK, GLA_DV)
    return (y_prompt, y_sample, new_state, jnp.stack(ctx_k, axis=1), jnp.stack(ctx_v, axis=1))
```

```python
import functools

import jax
import jax.numpy as jnp
import numpy as np
from jax import lax
from jax.experimental import pallas as pl
from jax.experimental.pallas import tpu as pltpu

F32 = jnp.float32
BF16 = jnp.bfloat16

D = 1024
BATCH, SEQ = 16, 256
DEC_BATCH, DEC_SEQ = 4, 1024
N_CTX = BATCH * SEQ
N_LAT = DEC_BATCH * DEC_SEQ
N_TOK = N_CTX + N_LAT
DEPTH = 4
EPS = 1e-6
GRID_W = 64
ROPE_THETA = 10000.0

GLA_HEADS, GLA_DK, GLA_DV, GLA_RANK, GLA_CHUNK, GLA_TAU = 4, 64, 128, 16, 64, 16.0
QK_W = GLA_HEADS * GLA_DK
V_W = GLA_HEADS * GLA_DV
GMLP_GROUPS, GMLP_DIM, GMLP_CHUNK = 4, 128, 128
GMLP_W = GMLP_GROUPS * GMLP_DIM
C_Q, C_K, C_V, C_G, C_U, C_VG, C_A = 0, 256, 512, 1024, 1536, 2048, 2560
EVEN_PACK = 2688

ATT_HEADS, ATT_KV, HD = 8, 2, 128
ATT_G = ATT_HEADS // ATT_KV
Q_W = ATT_HEADS * HD
KV_W = ATT_KV * HD

MOE_GROUPS, MOE_PER_GROUP = 4, 8
N_EXP = MOE_GROUPS * MOE_PER_GROUP
D_EXP = D // 4
NEG = -1e30

VMEM_LIMIT = 56 * 1024 * 1024


def _cp(*sem):
    return pltpu.CompilerParams(dimension_semantics=sem, vmem_limit_bytes=VMEM_LIMIT)


def _dot(a, b):
    return jnp.dot(a.astype(BF16), b.astype(BF16), preferred_element_type=F32)


def _dot_nt(a, b):
    return lax.dot_general(a.astype(BF16), b.astype(BF16), (((1,), (1,)), ((), ())),
                           preferred_element_type=F32)


def _dot_tn(a, b):
    return lax.dot_general(a.astype(BF16), b.astype(BF16), (((0,), (0,)), ((), ())),
                           preferred_element_type=F32)


def _rms(x, g):
    return x * lax.rsqrt(jnp.mean(x * x, axis=-1, keepdims=True) + EPS) * g


def _silu(x):
    return x * jax.nn.sigmoid(x)


def _gelu(x):
    return 0.5 * x * (1.0 + jnp.tanh(np.sqrt(2.0 / np.pi).astype(np.float32) * (x + 0.044715 * (x * x * x))))


def _log_sigmoid(z):
    return jnp.minimum(z, 0.0) - jnp.log(1.0 + jnp.exp(-jnp.abs(z)))


def _rows_to_tiles(ref, x):
    rows = x.shape[0]
    for j in range(D // 128):
        ref[pl.ds(j, rows, stride=8), :] = x[:, j * 128:(j + 1) * 128]


def _tiles_to_rows(ref, rows):
    return jnp.concatenate([ref[pl.ds(j, rows, stride=8), :] for j in range(D // 128)], axis=1)


def _tile_of(ref, row8):
    return ref.at[pl.ds(pl.multiple_of(row8, 8), 8), :]


def _resident(shape, index_map):
    return pl.BlockSpec(shape, index_map, pipeline_mode=pl.Buffered(1))


def _mod_kernel(cond_ref, w_ref, b_ref, o_ref):
    c = cond_ref[...]
    o_ref[...] = jnp.dot(_silu(c), w_ref[...], precision=lax.Precision.HIGHEST,
                         preferred_element_type=F32) + b_ref[...]


def _modulation(cond8, w_mod, b_mod):
    tn = 1024
    out = pl.pallas_call(
        _mod_kernel,
        grid=(DEPTH, 6 * D // tn),
        in_specs=[
            pl.BlockSpec((8, D), lambda l, j: (0, 0)),
            pl.BlockSpec((None, D, tn), lambda l, j: (l, 0, j)),
            pl.BlockSpec((None, 1, tn), lambda l, j: (l, 0, j)),
        ],
        out_specs=pl.BlockSpec((None, 8, tn), lambda l, j: (l, 0, j)),
        out_shape=jax.ShapeDtypeStruct((DEPTH, 8, 6 * D), F32),
        compiler_params=_cp("arbitrary", "arbitrary"),
        name="adaln_mod",
    )(cond8, w_mod, b_mod.reshape(DEPTH, 1, 6 * D))
    return out.reshape(DEPTH, 8, 6, D)


def _even_kernel(x_ref, mod_ref, n1g_ref, win_ref, wgu_ref, bgu_ref, glag_ref, ws_ref, bs_ref,
                 wout_ref, s0_ref, xo_ref, st_ref, proj, la, o_f, o_b, st_scr, *, T):
    n_chunks = T // GLA_CHUNK
    shift, scale, gate = mod_ref[0:1, :], mod_ref[1:2, :], mod_ref[2:3, :]
    RB = 128

    def proj_body(r, carry):
        r0 = pl.multiple_of(r * RB, RB)
        h = _rms(x_ref[pl.ds(r0, RB), :], n1g_ref[...]) * (1.0 + scale) + shift
        p = _dot(h, win_ref[...])
        proj[pl.ds(r0, RB), :] = p
        z = _dot(p[:, C_A:C_A + 128], wgu_ref[...]) + bgu_ref[...]
        la[pl.ds(r0, RB), :] = _log_sigmoid(z) * (1.0 / GLA_TAU)
        return carry

    lax.fori_loop(0, T // RB, proj_body, 0)

    st_scr[0] = s0_ref[0].T
    st_scr[1] = s0_ref[1].T

    ci = lax.broadcasted_iota(jnp.int32, (GLA_CHUNK, GLA_CHUNK), 0)
    cj = lax.broadcasted_iota(jnp.int32, (GLA_CHUNK, GLA_CHUNK), 1)
    tri = (jnp.where(ci >= cj, 1.0, 0.0).astype(BF16), jnp.where(ci <= cj, 1.0, 0.0).astype(BF16))
    ai = lax.broadcasted_iota(jnp.int32, (GLA_HEADS * GLA_CHUNK, GLA_CHUNK), 0) % GLA_CHUNK
    aj = lax.broadcasted_iota(jnp.int32, (GLA_HEADS * GLA_CHUNK, GLA_CHUNK), 1)
    amask = (ai >= aj, ai <= aj)
    lane_head = lax.broadcasted_iota(jnp.int32, (1, QK_W), 1) // GLA_DK
    hmask = [jnp.where(lane_head == h, 1.0, 0.0) for h in range(GLA_HEADS)]

    def chunk_body(i, carry):
        for d in range(2):
            c = i if d == 0 else n_chunks - 1 - i
            r0 = pl.multiple_of(c * GLA_CHUNK, GLA_CHUNK)
            q = proj[pl.ds(r0, GLA_CHUNK), C_Q:C_Q + QK_W] * (GLA_DK ** -0.5)
            k = proj[pl.ds(r0, GLA_CHUNK), C_K:C_K + QK_W]
            v = proj[pl.ds(r0, GLA_CHUNK), C_V:C_V + V_W]
            lac = la[pl.ds(r0, GLA_CHUNK), d * QK_W:(d + 1) * QK_W]
            hi = lac.astype(BF16)
            lo = (lac - hi.astype(F32)).astype(BF16)
            b = (jnp.dot(tri[d], hi, preferred_element_type=F32)
                 + jnp.dot(tri[d], lo, preferred_element_type=F32))
            bend = b[GLA_CHUNK - 1:GLA_CHUNK, :] if d == 0 else b[0:1, :]
            qe = q * jnp.exp(b)
            ke = k * jnp.exp(-b)
            kd = k * jnp.exp(bend - b)
            st = st_scr[d]
            qstack = jnp.concatenate([qe * hmask[h] for h in range(GLA_HEADS)], axis=0).astype(BF16)
            att = jnp.where(amask[d], _dot_nt(qstack, ke), 0.0)
            inter = _dot_nt(qstack, st)
            outs = []
            for h in range(GLA_HEADS):
                rows = slice(h * GLA_CHUNK, (h + 1) * GLA_CHUNK)
                outs.append(_dot(att[rows], v[:, h * GLA_DV:(h + 1) * GLA_DV]) + inter[rows])
            o = jnp.concatenate(outs, axis=1)
            if d == 0:
                o_f[pl.ds(r0, GLA_CHUNK), :] = o
            else:
                o_b[pl.ds(r0, GLA_CHUNK), :] = o
            vstack = jnp.concatenate([v[:, h * GLA_DV:(h + 1) * GLA_DV] for h in range(GLA_HEADS)], axis=0)
            kstack = jnp.concatenate([kd * hmask[h] for h in range(GLA_HEADS)], axis=0)
            st_scr[d] = st * jnp.exp(bend) + _dot_tn(vstack, kstack)
        return carry

    lax.fori_loop(0, n_chunks, chunk_body, 0, unroll=4)
    st_ref[0] = st_scr[0].T
    st_ref[1] = st_scr[1].T

    def out_body(r, carry):
        r0 = pl.multiple_of(r * RB, RB)
        osum = o_f[pl.ds(r0, RB), :] + o_b[pl.ds(r0, RB), :]
        g = proj[pl.ds(r0, RB), C_G:C_G + V_W]
        u = proj[pl.ds(r0, RB), C_U:C_U + GMLP_W]
        vg = _gelu(proj[pl.ds(r0, RB), C_VG:C_VG + GMLP_W])
        parts = []
        for h in range(GLA_HEADS):
            oh = osum[:, h * GLA_DV:(h + 1) * GLA_DV]
            parts.append(_rms(oh, glag_ref[...]) * _silu(g[:, h * GLA_DV:(h + 1) * GLA_DV]))
        for gi in range(GMLP_GROUPS):
            vc = vg[:, gi * GMLP_DIM:(gi + 1) * GMLP_DIM]
            vc = vc - jnp.mean(vc, axis=-1, keepdims=True)
            vn = vc * lax.rsqrt(jnp.mean(vc * vc, axis=-1, keepdims=True) + EPS)
            sg = _dot(ws_ref[gi], vn) + bs_ref[:, gi:gi + 1]
            parts.append(_gelu(u[:, gi * GMLP_DIM:(gi + 1) * GMLP_DIM]) * sg)
        mix = jnp.concatenate(parts, axis=1)
        y = _dot(mix, wout_ref[...])
        xo_ref[pl.ds(r0, RB), :] = x_ref[pl.ds(r0, RB), :] + gate * y
        return carry

    lax.fori_loop(0, T // RB, out_body, 0)


def _even_mixer(x_all, mod_l, n1g, win, wgu, bgu, glag, ws, bs, wout, s0, *, latent):
    if latent:
        T, nseq, blk0 = DEC_SEQ, DEC_BATCH, N_CTX // DEC_SEQ
        cond = lambda i: 1 + i
        s0_spec = pl.BlockSpec((None, 2, QK_W, GLA_DV), lambda i: (i, 0, 0, 0))
    else:
        T, nseq, blk0 = SEQ, BATCH, 0
        cond = lambda i: 0
        s0_spec = pl.BlockSpec((None, 2, QK_W, GLA_DV), lambda i: (0, 0, 0, 0))
    const2 = lambda i: (0, 0)
    x_new, states = pl.pallas_call(
        functools.partial(_even_kernel, T=T),
        grid=(nseq,),
        in_specs=[
            pl.BlockSpec((T, D), lambda i: (blk0 + i, 0)),
            pl.BlockSpec((None, 6, D), lambda i: (cond(i), 0, 0)),
            _resident((1, D), const2),
            _resident((D, EVEN_PACK), const2),
            _resident((128, 2 * QK_W), const2),
            _resident((1, 2 * QK_W), const2),
            _resident((1, GLA_DV), const2),
            _resident((GMLP_GROUPS, GMLP_CHUNK, GMLP_CHUNK), lambda i: (0, 0, 0)),
            _resident((GMLP_CHUNK, GMLP_GROUPS), const2),
            _resident((D, D), const2),
            s0_spec,
        ],
        out_specs=[
            pl.BlockSpec((T, D), lambda i: (blk0 + i, 0)),
            pl.BlockSpec((None, 2, QK_W, GLA_DV), lambda i: (i, 0, 0, 0)),
        ],
        out_shape=[
            jax.ShapeDtypeStruct((N_TOK, D), F32),
            jax.ShapeDtypeStruct((nseq, 2, QK_W, GLA_DV), F32),
        ],
        scratch_shapes=[
            pltpu.VMEM((T, EVEN_PACK), F32),
            pltpu.VMEM((T, 2 * QK_W), F32),
            pltpu.VMEM((T, V_W), F32),
            pltpu.VMEM((T, V_W), F32),
            pltpu.VMEM((2, GLA_DV, QK_W), F32),
        ],
        input_output_aliases={0: 0},
        compiler_params=_cp("arbitrary"),
        name="even_mixer_latent" if latent else "even_mixer_context",
    )(x_all, mod_l, n1g, win, wgu, bgu, glag, ws, bs, wout, s0)
    return x_new, states


QKV_TB = 512


def _qkv_kernel(x_ref, mod_ref, n1g_ref, win_ref, gq_ref, gk_ref, cos_ref, sin_ref, q_ref, k_ref, v_ref):
    shift, scale = mod_ref[0:1, :], mod_ref[1:2, :]
    h = _rms(x_ref[...], n1g_ref[...]) * (1.0 + scale) + shift
    p = _dot(h, win_ref[...])
    cos, sin = cos_ref[...], sin_ref[...]
    even_lane = lax.broadcasted_iota(jnp.int32, (1, HD), 1) % 2 == 0

    def norm_rope(xh, g):
        xn = _rms(xh, g)
        swapped = jnp.where(even_lane, pltpu.roll(xn, HD - 1, axis=1), pltpu.roll(xn, 1, axis=1))
        return xn * cos + swapped * sin

    for hh in range(ATT_HEADS):
        q_ref[:, hh * HD:(hh + 1) * HD] = norm_rope(p[:, hh * HD:(hh + 1) * HD], gq_ref[...]).astype(BF16)
    for hh in range(ATT_KV):
        k_ref[:, hh * HD:(hh + 1) * HD] = norm_rope(p[:, Q_W + hh * HD:Q_W + (hh + 1) * HD], gk_ref[...])
    v_ref[...] = p[:, Q_W + KV_W:]


def _qkv(x_all, mod_l, n1g, win, gq, gk, cos_tab, sin_tab):
    nb_ctx = N_CTX // QKV_TB
    per_seq = DEC_SEQ // QKV_TB
    cond = lambda i: jnp.where(i < nb_ctx, 0, 1 + (i - nb_ctx) // per_seq)
    tab = lambda i: jnp.where(i < nb_ctx, 0, 1 + (i - nb_ctx) % per_seq)
    const2 = lambda i: (0, 0)
    return pl.pallas_call(
        _qkv_kernel,
        grid=(N_TOK // QKV_TB,),
        in_specs=[
            pl.BlockSpec((QKV_TB, D), lambda i: (i, 0)),
            pl.BlockSpec((None, 6, D), lambda i: (cond(i), 0, 0)),
            _resident((1, D), const2),
            _resident((D, Q_W + 2 * KV_W), const2),
            _resident((1, HD), const2),
            _resident((1, HD), const2),
            pl.BlockSpec((None, QKV_TB, HD), lambda i: (tab(i), 0, 0)),
            pl.BlockSpec((None, QKV_TB, HD), lambda i: (tab(i), 0, 0)),
        ],
        out_specs=[
            pl.BlockSpec((QKV_TB, Q_W), lambda i: (i, 0)),
            pl.BlockSpec((QKV_TB, KV_W), lambda i: (i, 0)),
            pl.BlockSpec((QKV_TB, KV_W), lambda i: (i, 0)),
        ],
        out_shape=[
            jax.ShapeDtypeStruct((N_TOK, Q_W), BF16),
            jax.ShapeDtypeStruct((N_TOK, KV_W), F32),
            jax.ShapeDtypeStruct((N_TOK, KV_W), F32),
        ],
        compiler_params=_cp("arbitrary"),
        name="odd_qkv",
    )(x_all, mod_l, n1g, win, gq, gk, cos_tab, sin_tab)


ATT_TQ = 256


def _attn_kernel(*refs, n_kv):
    q_ref = refs[0]
    kv_refs = refs[1:1 + 2 * n_kv]
    x_ref, mod_ref, wout_ref, xo_ref, att_scr = refs[1 + 2 * n_kv:]
    gate = mod_ref[2:3, :]
    for kh in range(ATT_KV):
        ks = [kv_refs[2 * s][:, kh * HD:(kh + 1) * HD].astype(BF16) for s in range(n_kv)]
        vs = [kv_refs[2 * s + 1][:, kh * HD:(kh + 1) * HD].astype(BF16) for s in range(n_kv)]
        for g in range(ATT_G):
            hh = kh * ATT_G + g
            qh = q_ref[:, hh * HD:(hh + 1) * HD]
            ss = [_dot_nt(qh, kk) * (HD ** -0.5) for kk in ks]
            m = ss[0].max(axis=-1, keepdims=True)
            for s in ss[1:]:
                m = jnp.maximum(m, s.max(axis=-1, keepdims=True))
            ps = [jnp.exp(s - m) for s in ss]
            den = ps[0].sum(axis=-1, keepdims=True)
            for p in ps[1:]:
                den = den + p.sum(axis=-1, keepdims=True)
            o = _dot(ps[0], vs[0])
            for p, vv in zip(ps[1:], vs[1:]):
                o = o + _dot(p, vv)
            att_scr[:, hh * HD:(hh + 1) * HD] = o / den
    y = _dot(att_scr[...], wout_ref[...])
    xo_ref[...] = x_ref[...] + gate * y


def _attention(x_all, mod_l, q, k, v, wout, cache_k=None, cache_v=None, layer_i=0):
    latent = cache_k is not None
    const2 = lambda *a: (0, 0)
    if latent:
        nq = DEC_SEQ // ATT_TQ
        row_blk = lambda b, j: (N_CTX // ATT_TQ + b * nq + j, 0)
        grid = (DEC_BATCH, nq)
        kv_specs = [
            pl.BlockSpec((None, None, SEQ, KV_W), lambda b, j: (b, layer_i, 0, 0)),
            pl.BlockSpec((None, None, SEQ, KV_W), lambda b, j: (b, layer_i, 0, 0)),
            pl.BlockSpec((DEC_SEQ, KV_W), lambda b, j: (N_CTX // DEC_SEQ + b, 0)),
            pl.BlockSpec((DEC_SEQ, KV_W), lambda b, j: (N_CTX // DEC_SEQ + b, 0)),
        ]
        kv_args = (cache_k, cache_v, k, v)
        mod_spec = pl.BlockSpec((None, 6, D), lambda b, j: (1 + b, 0, 0))
        sem = ("arbitrary", "arbitrary")
        n_kv = 2
    else:
        row_blk = lambda i: (i, 0)
        grid = (BATCH,)
        kv_specs = [pl.BlockSpec((SEQ, KV_W), row_blk), pl.BlockSpec((SEQ, KV_W), row_blk)]
        kv_args = (k, v)
        mod_spec = pl.BlockSpec((None, 6, D), lambda i: (0, 0, 0))
        sem = ("arbitrary",)
        n_kv = 1
    n_in = 1 + len(kv_args)
    return pl.pallas_call(
        functools.partial(_attn_kernel, n_kv=n_kv),
        grid=grid,
        in_specs=[pl.BlockSpec((ATT_TQ, Q_W), row_blk)] + kv_specs + [
            pl.BlockSpec((ATT_TQ, D), row_blk),
            mod_spec,
            _resident((D, D), const2),
        ],
        out_specs=pl.BlockSpec((ATT_TQ, D), row_blk),
        out_shape=jax.ShapeDtypeStruct((N_TOK, D), F32),
        scratch_shapes=[pltpu.VMEM((ATT_TQ, Q_W), F32)],
        input_output_aliases={n_in: 0},
        compiler_params=_cp(*sem),
        name="attention_latent" if latent else "attention_context",
    )(q, *kv_args, x_all, mod_l, wout)


ROUTE_TB = 512
HALF_TOK = N_TOK // 2
M_E1, M_E2, M_G1, M_G2, M_R1, M_R2 = 0, 1, 2, 3, 4, 5


def _router_kernel(x_ref, mod_ref, n2g_ref, wr_ref, br_ref, h_ref, meta_ref, metat_ref, cnt_ref, run):
    @pl.when(pl.program_id(0) % (HALF_TOK // ROUTE_TB) == 0)
    def _():
        run[...] = jnp.zeros_like(run)

    shift, scale = mod_ref[3:4, :], mod_ref[4:5, :]
    h = _rms(x_ref[...], n2g_ref[...]) * (1.0 + scale) + shift
    _rows_to_tiles(h_ref, h)
    logits = jnp.dot(h, wr_ref[...], precision=lax.Precision.HIGHEST, preferred_element_type=F32) + br_ref[...]
    lane = lax.broadcasted_iota(jnp.int32, logits.shape, 1).astype(F32)
    big = 1e4

    def first_argmax(vals):
        m = vals.max(axis=-1, keepdims=True)
        return m, jnp.where(vals == m, lane, big).min(axis=-1, keepdims=True)

    gl = jnp.where((lane >= N_EXP) & (lane < N_EXP + MOE_GROUPS), logits, NEG)
    gmax, glane = first_argmax(gl)
    g_p = 1.0 / jnp.exp(gl - gmax).sum(axis=-1, keepdims=True)
    lo = (glane - N_EXP) * MOE_PER_GROUP
    el = jnp.where((lane >= lo) & (lane < lo + MOE_PER_GROUP), logits, NEG)
    m1, i1 = first_argmax(el)
    m2, i2 = first_argmax(jnp.where(lane == i1, NEG, el))
    t = jnp.exp(m2 - m1)
    w1 = 1.0 / (1.0 + t)
    sel1, sel2 = lane == i1, lane == i2
    onehot = jnp.where(sel1 | sel2, 1.0, 0.0)
    ri = lax.broadcasted_iota(jnp.int32, (ROUTE_TB, ROUTE_TB), 0)
    rj = lax.broadcasted_iota(jnp.int32, (ROUTE_TB, ROUTE_TB), 1)
    before = _dot(jnp.where(ri > rj, 1.0, 0.0), onehot) + run[...]
    r1 = jnp.where(sel1, before, 0.0).sum(axis=-1, keepdims=True)
    r2 = jnp.where(sel2, before, 0.0).sum(axis=-1, keepdims=True)
    run[...] += onehot.sum(axis=0, keepdims=True)
    cnt_ref[...] = run[...]
    meta = jnp.zeros_like(logits)
    for j, val in enumerate([i1, i2, w1 * g_p, (t * w1) * g_p, r1, r2]):
        meta = jnp.where(lane == j, val, meta)
    meta_ref[...] = meta
    metat_ref[...] = meta.T[0:8, :]


def _router(x_all, mod_l, n2g, wr, br):
    nb_ctx = N_CTX // ROUTE_TB
    per_seq = DEC_SEQ // ROUTE_TB
    cond = lambda i: jnp.where(i < nb_ctx, 0, 1 + (i - nb_ctx) // per_seq)
    const2 = lambda i: (0, 0)
    return pl.pallas_call(
        _router_kernel,
        grid=(N_TOK // ROUTE_TB,),
        in_specs=[
            pl.BlockSpec((ROUTE_TB, D), lambda i: (i, 0)),
            pl.BlockSpec((None, 6, D), lambda i: (cond(i), 0, 0)),
            _resident((1, D), const2),
            _resident((D, 128), const2),
            _resident((1, 128), const2),
        ],
        out_specs=[
            pl.BlockSpec((ROUTE_TB * 8, 128), lambda i: (i, 0)),
            pl.BlockSpec((ROUTE_TB, 128), lambda i: (i, 0)),
            pl.BlockSpec((8, ROUTE_TB), lambda i: (0, i)),
            pl.BlockSpec((None, 1, 128), lambda i: (i // (HALF_TOK // ROUTE_TB), 0, 0)),
        ],
        out_shape=[
            jax.ShapeDtypeStruct((N_TOK * 8, 128), F32),
            jax.ShapeDtypeStruct((N_TOK, 128), F32),
            jax.ShapeDtypeStruct((8, N_TOK), F32),
            jax.ShapeDtypeStruct((2, 1, 128), F32),
        ],
        scratch_shapes=[pltpu.VMEM((1, 128), F32)],
        compiler_params=_cp("arbitrary"),
        name="moe_router",
    )(x_all, mod_l, n2g, wr, br)


EXP_TM = 128
N_ASSIGN = 2 * N_TOK
N_GROUPS = 2 * N_EXP
MAX_TILES = N_ASSIGN // EXP_TM + N_GROUPS
N_SORTED = MAX_TILES * EXP_TM
ORDER_BLK = 4096


DUMMY8 = HALF_TOK * 8


def _order_kernel(packed_ref, gate_ref, offs_ref, pad_lo_ref, pad_hi_ref, src_ref, gs_ref):
    i = pl.program_id(0)
    half = i // (N_ASSIGN // 2 // ORDER_BLK)
    local = i * (ORDER_BLK // 2) - half * HALF_TOK

    def body(a, carry):
        v = packed_ref[a]
        p = offs_ref[half * N_EXP + (v >> 16)] + (v & 0xFFFF)
        src_ref[p] = (local + (a >> 1)) * 8
        gs_ref[p] = gate_ref[a]
        return carry

    lax.fori_loop(0, ORDER_BLK, body, 0, unroll=8)

    @pl.when(i == 0)
    def _():
        def group(g, carry):
            def pad(p, c):
                src_ref[p] = DUMMY8
                gs_ref[p] = 0.0
                return c
            return lax.fori_loop(pad_lo_ref[g], pad_hi_ref[g], pad, carry)

        lax.fori_loop(0, N_GROUPS, group, 0)


def _order(packed, gates, offs, pad_lo, pad_hi):
    return pl.pallas_call(
        _order_kernel,
        grid=(N_ASSIGN // ORDER_BLK,),
        in_specs=[
            pl.BlockSpec((ORDER_BLK,), lambda i: (i,), memory_space=pltpu.SMEM),
            pl.BlockSpec((ORDER_BLK,), lambda i: (i,), memory_space=pltpu.SMEM),
            pl.BlockSpec(memory_space=pltpu.SMEM),
            pl.BlockSpec(memory_space=pltpu.SMEM),
            pl.BlockSpec(memory_space=pltpu.SMEM),
        ],
        out_specs=[pl.BlockSpec(memory_space=pltpu.SMEM), pl.BlockSpec(memory_space=pltpu.SMEM)],
        out_shape=[jax.ShapeDtypeStruct((N_SORTED,), jnp.int32), jax.ShapeDtypeStruct((N_SORTED,), F32)],
        compiler_params=_cp("arbitrary"),
        name="moe_order",
    )(packed, gates, offs, pad_lo, pad_hi)


F_FIRST, F_LAST, F_NEW, F_HALF = 1, 2, 4, 8
ACC_TOK = HALF_TOK + 64
ROW_GROUP = 8


def _experts_kernel(te_ref, flag_ref, nv_ref, src_ref, gs_ref, h_hbm, wg_ref, wu_ref, wd_ref, out_hbm,
                    h_res, acc, xbuf, ybuf, wgb, wub, wdb, sem):
    t = pl.program_id(0)

    @pl.when(t < nv_ref[0])
    def _():
        flags = flag_ref[t]
        rows0 = pl.multiple_of(((flags // F_HALF) % 2) * (HALF_TOK * 8), 8)

        @pl.when((flags & F_FIRST) != 0)
        def _():
            cp = pltpu.make_async_copy(h_hbm.at[pl.ds(rows0, HALF_TOK * 8), :], h_res, sem)
            cp.start()

            def zero(i, carry):
                acc[pl.ds(pl.multiple_of(i * 512, 512), 512), :] = jnp.zeros((512, 128), F32)
                return carry

            lax.fori_loop(0, ACC_TOK * 8 // 512, zero, 0)
            cp.wait()

        @pl.when((flags & F_NEW) != 0)
        def _():
            wgb[...] = wg_ref[...].astype(BF16)
            wub[...] = wu_ref[...].astype(BF16)
            wdb[...] = wd_ref[...].astype(BF16)

        def gather(g, carry):
            for j in range(ROW_GROUP):
                r = g * ROW_GROUP + j
                row8 = jnp.minimum(src_ref[r], DUMMY8 - 8)
                xbuf[pl.ds(pl.multiple_of(r * 8, 8), 8), :] = _tile_of(h_res, row8)[...]
            return carry

        lax.fori_loop(0, EXP_TM // ROW_GROUP, gather, 0)
        x = _tiles_to_rows(xbuf, EXP_TM).astype(BF16)
        hid = _silu(_dot(x, wgb[...])) * _dot(x, wub[...])
        _rows_to_tiles(ybuf, _dot(hid, wdb[...]))

        def accumulate(g, carry):
            targets, values = [], []
            for j in range(ROW_GROUP):
                r = g * ROW_GROUP + j
                target = _tile_of(acc, src_ref[r])
                targets.append(target)
                values.append(target[...] + gs_ref[r] * ybuf[pl.ds(pl.multiple_of(r * 8, 8), 8), :])
            for target, value in zip(targets, values):
                target[...] = value
            return carry

        lax.fori_loop(0, EXP_TM // ROW_GROUP, accumulate, 0)

        @pl.when((flags & F_LAST) != 0)
        def _():
            cp = pltpu.make_async_copy(acc.at[pl.ds(0, HALF_TOK * 8), :],
                                       out_hbm.at[pl.ds(rows0, HALF_TOK * 8), :], sem)
            cp.start()
            cp.wait()


def _experts(tile_expert, tile_flags, n_valid, src, gs, h, wg, wu, wd, layer):
    tile = lambda t, te, fl, nv: jnp.minimum(t, nv[0] - 1)
    wmap = lambda t, te, fl, nv: (layer, te[tile(t, te, fl, nv)], 0, 0)
    smap = lambda t, te, fl, nv: (tile(t, te, fl, nv),)
    return pl.pallas_call(
        _experts_kernel,
        grid_spec=pltpu.PrefetchScalarGridSpec(
            num_scalar_prefetch=3,
            grid=(MAX_TILES,),
            in_specs=[
                pl.BlockSpec((EXP_TM,), smap, memory_space=pltpu.SMEM),
                pl.BlockSpec((EXP_TM,), smap, memory_space=pltpu.SMEM),
                pl.BlockSpec(memory_space=pl.ANY),
                pl.BlockSpec((None, None, D, D_EXP), wmap),
                pl.BlockSpec((None, None, D, D_EXP), wmap),
                pl.BlockSpec((None, None, D_EXP, D), wmap),
            ],
            out_specs=pl.BlockSpec(memory_space=pl.ANY),
            scratch_shapes=[
                pltpu.VMEM((HALF_TOK * 8, 128), F32),
                pltpu.VMEM((ACC_TOK * 8, 128), F32),
                pltpu.VMEM((EXP_TM * 8, 128), F32),
                pltpu.VMEM((EXP_TM * 8, 128), F32),
                pltpu.VMEM((D, D_EXP), BF16),
                pltpu.VMEM((D, D_EXP), BF16),
                pltpu.VMEM((D_EXP, D), BF16),
                pltpu.SemaphoreType.DMA,
            ],
        ),
        out_shape=jax.ShapeDtypeStruct((N_TOK * 8, 128), F32),
        compiler_params=_cp("arbitrary"),
        name="moe_experts",
    )(tile_expert, tile_flags, n_valid, src, gs, h, wg, wu, wd)


COMBINE_TB = 256


def _combine_kernel(y_ref, x_ref, mod_ref, xo_ref):
    xo_ref[...] = x_ref[...] + mod_ref[5:6, :] * _tiles_to_rows(y_ref, COMBINE_TB)


def _combine(ys, x_all, mod_l):
    nb_ctx = N_CTX // COMBINE_TB
    per_seq = DEC_SEQ // COMBINE_TB
    cond = lambda i: jnp.where(i < nb_ctx, 0, 1 + (i - nb_ctx) // per_seq)
    return pl.pallas_call(
        _combine_kernel,
        grid=(N_TOK // COMBINE_TB,),
        in_specs=[
            pl.BlockSpec((COMBINE_TB * 8, 128), lambda i: (i, 0)),
            pl.BlockSpec((COMBINE_TB, D), lambda i: (i, 0)),
            pl.BlockSpec((None, 6, D), lambda i: (cond(i), 0, 0)),
        ],
        out_specs=pl.BlockSpec((COMBINE_TB, D), lambda i: (i, 0)),
        out_shape=jax.ShapeDtypeStruct((N_TOK, D), F32),
        compiler_params=_cp("arbitrary"),
        name="moe_combine",
    )(ys, x_all, mod_l)


def _moe(x_all, mod_l, n2g, wr, br, wg, wu, wd, layer):
    h, meta, metat, cnt = _router(x_all, mod_l, n2g, wr, br)
    counts = cnt[:, 0, :N_EXP].astype(jnp.int32).reshape(N_GROUPS)
    padded = (counts + EXP_TM - 1) // EXP_TM * EXP_TM
    ends = jnp.cumsum(padded)
    offs = ends - padded
    n_valid = jnp.maximum(ends[-1:] // EXP_TM, 1)
    tiles = jnp.arange(MAX_TILES, dtype=jnp.int32)
    starts = tiles * EXP_TM
    group = jnp.minimum(jnp.sum((ends[None, :] <= starts[:, None]).astype(jnp.int32), axis=1), N_GROUPS - 1)
    half = group // N_EXP
    prev_group = jnp.concatenate([jnp.full((1,), -1, jnp.int32), group[:-1]])
    next_half = jnp.concatenate([half[1:], jnp.full((1,), 2, jnp.int32)])
    first = (prev_group // N_EXP != half) | (tiles == 0)
    last = (next_half != half) | (tiles == n_valid - 1)
    tile_flags = (F_FIRST * first + F_LAST * last + F_NEW * (first | (prev_group != group)) + F_HALF * half)
    rec = metat.astype(jnp.int32)
    packed = ((rec[M_E1:M_E2 + 1] << 16) | rec[M_R1:M_R2 + 1]).T.reshape(N_ASSIGN)
    gates = metat[M_G1:M_G2 + 1].T.reshape(N_ASSIGN)
    src, gs = _order(packed, gates, offs, offs + counts, ends)
    ys = _experts(group % N_EXP, tile_flags.astype(jnp.int32), n_valid, src, gs, h, wg, wu, wd, layer)
    return _combine(ys, x_all, mod_l)


def _final_kernel(x_ref, g_ref, o_ref):
    o_ref[...] = _rms(x_ref[...], g_ref[...])


def _final_norm(x_all, g, blk0, n_rows):
    tb = 512
    return pl.pallas_call(
        _final_kernel,
        grid=(n_rows // tb,),
        in_specs=[pl.BlockSpec((tb, D), lambda i: (blk0 + i, 0)), _resident((1, D), lambda i: (0, 0))],
        out_specs=pl.BlockSpec((tb, D), lambda i: (i, 0)),
        out_shape=jax.ShapeDtypeStruct((n_rows, D), F32),
        compiler_params=_cp("arbitrary"),
        name="final_norm",
    )(x_all, g)


def _rope_tables():
    pos = jnp.arange(DEC_SEQ)
    row = (pos // GRID_W).astype(F32)
    col = (pos % GRID_W).astype(F32)
    n_freq = HD // 4
    inv = ROPE_THETA ** (-jnp.arange(n_freq, dtype=F32) / n_freq)
    ang = jnp.concatenate([row[:, None] * inv, col[:, None] * inv], axis=-1)
    cos = jnp.repeat(jnp.cos(ang), 2, axis=-1)
    sin = jnp.repeat(jnp.sin(ang), 2, axis=-1) * jnp.tile(jnp.array([-1.0, 1.0], F32), HD // 2)
    nblk = DEC_SEQ // QKV_TB
    cos_tab = jnp.concatenate([jnp.ones((1, QKV_TB, HD), F32), cos.reshape(nblk, QKV_TB, HD)], axis=0)
    sin_tab = jnp.concatenate([jnp.zeros((1, QKV_TB, HD), F32), sin.reshape(nblk, QKV_TB, HD)], axis=0)
    return cos_tab, sin_tab


def kernel(x_prompt, x_sample, state_gla, cache_k, cache_v, c, c_ctx, w_mod, b_mod, norm1_g, norm2_g,
           w_in_even, w_gate_up, b_gate_up, gla_norm_g, w_spatial, b_spatial, w_out_even,
           w_in_odd, q_norm_g, k_norm_g, w_out_odd, w_router_group, b_router_group,
           w_router_expert, b_router_expert, w_exp_gate, w_exp_up, w_exp_down, final_norm_g):
    x_all = jnp.concatenate([x_prompt.reshape(N_CTX, D), x_sample.reshape(N_LAT, D)], axis=0)
    cond8 = jnp.concatenate([c_ctx[None], c, jnp.zeros((3, D), F32)], axis=0)
    mod = _modulation(cond8, w_mod, b_mod)
    cos_tab, sin_tab = _rope_tables()
    zero_state = jnp.zeros((1, 2, QK_W, GLA_DV), F32)
    state_in = state_gla.reshape(DEC_BATCH, -1, 2, QK_W, GLA_DV)
    cache_k2 = cache_k.reshape(DEC_BATCH, -1, SEQ, KV_W)
    cache_v2 = cache_v.reshape(DEC_BATCH, -1, SEQ, KV_W)

    gla_states, ctx_k, ctx_v = [], [], []
    for l in range(DEPTH):
        i = l // 2
        n1g = norm1_g[l][None]
        if l % 2 == 0:
            w = w_in_even[i]
            win = jnp.concatenate([w[:, :1536], w[:, 1568:], w[:, 1536:1568], jnp.zeros((D, 96), F32)],
                                  axis=1).astype(BF16)
            wgu = jnp.zeros((128, 2 * QK_W), F32)
            wgu = wgu.at[0:GLA_RANK, 0:QK_W].set(w_gate_up[i, 0])
            wgu = wgu.at[GLA_RANK:2 * GLA_RANK, QK_W:].set(w_gate_up[i, 1]).astype(BF16)
            bgu = b_gate_up[i].reshape(1, 2 * QK_W)
            args = (mod[l], n1g, win, wgu, bgu, gla_norm_g[i][None], w_spatial[i].astype(BF16),
                    b_spatial[i].T, w_out_even[i].astype(BF16))
            x_all, st = _even_mixer(x_all, *args, zero_state, latent=False)
            gla_states.append(st)
            x_all, _ = _even_mixer(x_all, *args, state_in[:, i], latent=True)
        else:
            q, k, v = _qkv(x_all, mod[l], n1g, w_in_odd[i].astype(BF16), q_norm_g[i][None],
                           k_norm_g[i][None], cos_tab, sin_tab)
            ctx_k.append(k[:N_CTX].reshape(BATCH, SEQ, ATT_KV, HD))
            ctx_v.append(v[:N_CTX].reshape(BATCH, SEQ, ATT_KV, HD))
            wout = w_out_odd[i].astype(BF16)
            x_all = _attention(x_all, mod[l], q, k, v, wout)
            x_all = _attention(x_all, mod[l], q, k, v, wout, cache_k2, cache_v2, layer_i=i)
        wr = jnp.concatenate([w_router_expert[l], w_router_group[l],
                              jnp.zeros((D, 128 - N_EXP - MOE_GROUPS), F32)], axis=1)
        br = jnp.concatenate([b_router_expert[l], b_router_group[l],
                              jnp.zeros((128 - N_EXP - MOE_GROUPS,), F32)])[None]
        x_all = _moe(x_all, mod[l], norm2_g[l][None], wr, br, w_exp_gate, w_exp_up, w_exp_down, l)

    fg = final_norm_g[None]
    y_prompt = _final_norm(x_all, fg, 0, N_CTX).reshape(BATCH, SEQ, D)
    y_sample = _final_norm(x_all, fg, N_CTX // 512, N_LAT).reshape(DEC_BATCH, DEC_SEQ, D)
    new_state = jnp.stack(gla_states, axis=1).reshape(BATCH, -1, 2, GLA_HEADS, GLA_DK, GLA_DV)
    return (y_prompt, y_sample, new_state, jnp.stack(ctx_k, axis=1), jnp.stack(ctx_v, axis=1))
```

```python
import functools

import jax
import jax.numpy as jnp
import numpy as np
from jax import lax
from jax.experimental import pallas as pl
from jax.experimental.pallas import tpu as pltpu

F32 = jnp.float32
BF16 = jnp.bfloat16

D = 1024
BATCH, SEQ = 16, 256
DEC_BATCH, DEC_SEQ = 4, 1024
N_CTX = BATCH * SEQ
N_LAT = DEC_BATCH * DEC_SEQ
N_TOK = N_CTX + N_LAT
DEPTH = 4
EPS = 1e-6
GRID_W = 64
ROPE_THETA = 10000.0

GLA_HEADS, GLA_DK, GLA_DV, GLA_RANK, GLA_CHUNK, GLA_TAU = 4, 64, 128, 16, 64, 16.0
QK_W = GLA_HEADS * GLA_DK
V_W = GLA_HEADS * GLA_DV
GMLP_GROUPS, GMLP_DIM, GMLP_CHUNK = 4, 128, 128
GMLP_W = GMLP_GROUPS * GMLP_DIM
C_Q, C_K, C_V, C_G, C_U, C_VG, C_A = 0, 256, 512, 1024, 1536, 2048, 2560
EVEN_PACK = 2688

ATT_HEADS, ATT_KV, HD = 8, 2, 128
ATT_G = ATT_HEADS // ATT_KV
Q_W = ATT_HEADS * HD
KV_W = ATT_KV * HD

MOE_GROUPS, MOE_PER_GROUP = 4, 8
N_EXP = MOE_GROUPS * MOE_PER_GROUP
D_EXP = D // 4
NEG = -1e30

VMEM_LIMIT = 56 * 1024 * 1024


def _cp(*sem):
    return pltpu.CompilerParams(dimension_semantics=sem, vmem_limit_bytes=VMEM_LIMIT)


def _dot(a, b):
    return jnp.dot(a.astype(BF16), b.astype(BF16), preferred_element_type=F32)


def _dot_nt(a, b):
    return lax.dot_general(a.astype(BF16), b.astype(BF16), (((1,), (1,)), ((), ())),
                           preferred_element_type=F32)


def _dot_tn(a, b):
    return lax.dot_general(a.astype(BF16), b.astype(BF16), (((0,), (0,)), ((), ())),
                           preferred_element_type=F32)


def _rms(x, g):
    return x * lax.rsqrt(jnp.mean(x * x, axis=-1, keepdims=True) + EPS) * g


def _silu(x):
    return x * jax.nn.sigmoid(x)


def _gelu(x):
    return 0.5 * x * (1.0 + jnp.tanh(np.sqrt(2.0 / np.pi).astype(np.float32) * (x + 0.044715 * (x * x * x))))


def _log_sigmoid(z):
    return jnp.minimum(z, 0.0) - jnp.log(1.0 + jnp.exp(-jnp.abs(z)))


def _rows_to_tiles(ref, x):
    rows = x.shape[0]
    for j in range(D // 128):
        ref[pl.ds(j, rows, stride=8), :] = x[:, j * 128:(j + 1) * 128]


def _tiles_to_rows(ref, rows):
    return jnp.concatenate([ref[pl.ds(j, rows, stride=8), :] for j in range(D // 128)], axis=1)


def _tile_of(ref, row8):
    return ref.at[pl.ds(pl.multiple_of(row8, 8), 8), :]


def _resident(shape, index_map):
    return pl.BlockSpec(shape, index_map, pipeline_mode=pl.Buffered(1))


def _mod_kernel(cond_ref, w_ref, b_ref, o_ref):
    c = cond_ref[...]
    o_ref[...] = jnp.dot(_silu(c), w_ref[...], precision=lax.Precision.HIGHEST,
                         preferred_element_type=F32) + b_ref[...]


def _modulation(cond8, w_mod, b_mod):
    tn = 1024
    out = pl.pallas_call(
        _mod_kernel,
        grid=(DEPTH, 6 * D // tn),
        in_specs=[
            pl.BlockSpec((8, D), lambda l, j: (0, 0)),
            pl.BlockSpec((None, D, tn), lambda l, j: (l, 0, j)),
            pl.BlockSpec((None, 1, tn), lambda l, j: (l, 0, j)),
        ],
        out_specs=pl.BlockSpec((None, 8, tn), lambda l, j: (l, 0, j)),
        out_shape=jax.ShapeDtypeStruct((DEPTH, 8, 6 * D), F32),
        compiler_params=_cp("arbitrary", "arbitrary"),
        name="adaln_mod",
    )(cond8, w_mod, b_mod.reshape(DEPTH, 1, 6 * D))
    return out.reshape(DEPTH, 8, 6, D)


def _even_kernel(x_ref, mod_ref, n1g_ref, win_ref, wgu_ref, bgu_ref, glag_ref, ws_ref, bs_ref,
                 wout_ref, s0_ref, xo_ref, st_ref, proj, la, o_f, o_b, st_scr, *, T):
    n_chunks = T // GLA_CHUNK
    shift, scale, gate = mod_ref[0:1, :], mod_ref[1:2, :], mod_ref[2:3, :]
    RB = 128

    def proj_body(r, carry):
        r0 = pl.multiple_of(r * RB, RB)
        h = _rms(x_ref[pl.ds(r0, RB), :], n1g_ref[...]) * (1.0 + scale) + shift
        p = _dot(h, win_ref[...])
        proj[pl.ds(r0, RB), :] = p
        z = _dot(p[:, C_A:C_A + 128], wgu_ref[...]) + bgu_ref[...]
        la[pl.ds(r0, RB), :] = _log_sigmoid(z) * (1.0 / GLA_TAU)
        return carry

    lax.fori_loop(0, T // RB, proj_body, 0)

    st_scr[0] = s0_ref[0].T
    st_scr[1] = s0_ref[1].T

    ci = lax.broadcasted_iota(jnp.int32, (GLA_CHUNK, GLA_CHUNK), 0)
    cj = lax.broadcasted_iota(jnp.int32, (GLA_CHUNK, GLA_CHUNK), 1)
    tri = (jnp.where(ci >= cj, 1.0, 0.0).astype(BF16), jnp.where(ci <= cj, 1.0, 0.0).astype(BF16))
    ai = lax.broadcasted_iota(jnp.int32, (GLA_HEADS * GLA_CHUNK, GLA_CHUNK), 0) % GLA_CHUNK
    aj = lax.broadcasted_iota(jnp.int32, (GLA_HEADS * GLA_CHUNK, GLA_CHUNK), 1)
    amask = (ai >= aj, ai <= aj)
    lane_head = lax.broadcasted_iota(jnp.int32, (1, QK_W), 1) // GLA_DK
    hmask = [jnp.where(lane_head == h, 1.0, 0.0) for h in range(GLA_HEADS)]

    def chunk_body(i, carry):
        for d in range(2):
            c = i if d == 0 else n_chunks - 1 - i
            r0 = pl.multiple_of(c * GLA_CHUNK, GLA_CHUNK)
            q = proj[pl.ds(r0, GLA_CHUNK), C_Q:C_Q + QK_W] * (GLA_DK ** -0.5)
            k = proj[pl.ds(r0, GLA_CHUNK), C_K:C_K + QK_W]
            v = proj[pl.ds(r0, GLA_CHUNK), C_V:C_V + V_W]
            lac = la[pl.ds(r0, GLA_CHUNK), d * QK_W:(d + 1) * QK_W]
            hi = lac.astype(BF16)
            lo = (lac - hi.astype(F32)).astype(BF16)
            b = (jnp.dot(tri[d], hi, preferred_element_type=F32)
                 + jnp.dot(tri[d], lo, preferred_element_type=F32))
            bend = b[GLA_CHUNK - 1:GLA_CHUNK, :] if d == 0 else b[0:1, :]
            qe = q * jnp.exp(b)
            ke = k * jnp.exp(-b)
            kd = k * jnp.exp(bend - b)
            st = st_scr[d]
            qstack = jnp.concatenate([qe * hmask[h] for h in range(GLA_HEADS)], axis=0).astype(BF16)
            att = jnp.where(amask[d], _dot_nt(qstack, ke), 0.0)
            inter = _dot_nt(qstack, st)
            outs = []
            for h in range(GLA_HEADS):
                rows = slice(h * GLA_CHUNK, (h + 1) * GLA_CHUNK)
                outs.append(_dot(att[rows], v[:, h * GLA_DV:(h + 1) * GLA_DV]) + inter[rows])
            o = jnp.concatenate(outs, axis=1)
            if d == 0:
                o_f[pl.ds(r0, GLA_CHUNK), :] = o
            else:
                o_b[pl.ds(r0, GLA_CHUNK), :] = o
            vstack = jnp.concatenate([v[:, h * GLA_DV:(h + 1) * GLA_DV] for h in range(GLA_HEADS)], axis=0)
            kstack = jnp.concatenate([kd * hmask[h] for h in range(GLA_HEADS)], axis=0)
            st_scr[d] = st * jnp.exp(bend) + _dot_tn(vstack, kstack)
        return carry

    lax.fori_loop(0, n_chunks, chunk_body, 0, unroll=4)
    st_ref[0] = st_scr[0].T
    st_ref[1] = st_scr[1].T

    def out_body(r, carry):
        r0 = pl.multiple_of(r * RB, RB)
        osum = o_f[pl.ds(r0, RB), :] + o_b[pl.ds(r0, RB), :]
        g = proj[pl.ds(r0, RB), C_G:C_G + V_W]
        u = proj[pl.ds(r0, RB), C_U:C_U + GMLP_W]
        vg = _gelu(proj[pl.ds(r0, RB), C_VG:C_VG + GMLP_W])
        parts = []
        for h in range(GLA_HEADS):
            oh = osum[:, h * GLA_DV:(h + 1) * GLA_DV]
            parts.append(_rms(oh, glag_ref[...]) * _silu(g[:, h * GLA_DV:(h + 1) * GLA_DV]))
        for gi in range(GMLP_GROUPS):
            vc = vg[:, gi * GMLP_DIM:(gi + 1) * GMLP_DIM]
            vc = vc - jnp.mean(vc, axis=-1, keepdims=True)
            vn = vc * lax.rsqrt(jnp.mean(vc * vc, axis=-1, keepdims=True) + EPS)
            sg = _dot(ws_ref[gi], vn) + bs_ref[:, gi:gi + 1]
            parts.append(_gelu(u[:, gi * GMLP_DIM:(gi + 1) * GMLP_DIM]) * sg)
        mix = jnp.concatenate(parts, axis=1)
        y = _dot(mix, wout_ref[...])
        xo_ref[pl.ds(r0, RB), :] = x_ref[pl.ds(r0, RB), :] + gate * y
        return carry

    lax.fori_loop(0, T // RB, out_body, 0)


def _even_mixer(x_all, mod_l, n1g, win, wgu, bgu, glag, ws, bs, wout, s0, *, latent):
    if latent:
        T, nseq, blk0 = DEC_SEQ, DEC_BATCH, N_CTX // DEC_SEQ
        cond = lambda i: 1 + i
        s0_spec = pl.BlockSpec((None, 2, QK_W, GLA_DV), lambda i: (i, 0, 0, 0))
    else:
        T, nseq, blk0 = SEQ, BATCH, 0
        cond = lambda i: 0
        s0_spec = pl.BlockSpec((None, 2, QK_W, GLA_DV), lambda i: (0, 0, 0, 0))
    const2 = lambda i: (0, 0)
    x_new, states = pl.pallas_call(
        functools.partial(_even_kernel, T=T),
        grid=(nseq,),
        in_specs=[
            pl.BlockSpec((T, D), lambda i: (blk0 + i, 0)),
            pl.BlockSpec((None, 6, D), lambda i: (cond(i), 0, 0)),
            _resident((1, D), const2),
            _resident((D, EVEN_PACK), const2),
            _resident((128, 2 * QK_W), const2),
            _resident((1, 2 * QK_W), const2),
            _resident((1, GLA_DV), const2),
            _resident((GMLP_GROUPS, GMLP_CHUNK, GMLP_CHUNK), lambda i: (0, 0, 0)),
            _resident((GMLP_CHUNK, GMLP_GROUPS), const2),
            _resident((D, D), const2),
            s0_spec,
        ],
        out_specs=[
            pl.BlockSpec((T, D), lambda i: (blk0 + i, 0)),
            pl.BlockSpec((None, 2, QK_W, GLA_DV), lambda i: (i, 0, 0, 0)),
        ],
        out_shape=[
            jax.ShapeDtypeStruct((N_TOK, D), F32),
            jax.ShapeDtypeStruct((nseq, 2, QK_W, GLA_DV), F32),
        ],
        scratch_shapes=[
            pltpu.VMEM((T, EVEN_PACK), F32),
            pltpu.VMEM((T, 2 * QK_W), F32),
            pltpu.VMEM((T, V_W), F32),
            pltpu.VMEM((T, V_W), F32),
            pltpu.VMEM((2, GLA_DV, QK_W), F32),
        ],
        input_output_aliases={0: 0},
        compiler_params=_cp("arbitrary"),
        name="even_mixer_latent" if latent else "even_mixer_context",
    )(x_all, mod_l, n1g, win, wgu, bgu, glag, ws, bs, wout, s0)
    return x_new, states


QKV_TB = 512


def _qkv_kernel(x_ref, mod_ref, n1g_ref, win_ref, gq_ref, gk_ref, cos_ref, sin_ref, q_ref, k_ref, v_ref):
    shift, scale = mod_ref[0:1, :], mod_ref[1:2, :]
    h = _rms(x_ref[...], n1g_ref[...]) * (1.0 + scale) + shift
    p = _dot(h, win_ref[...])
    cos, sin = cos_ref[...], sin_ref[...]
    even_lane = lax.broadcasted_iota(jnp.int32, (1, HD), 1) % 2 == 0

    def norm_rope(xh, g):
        xn = _rms(xh, g)
        swapped = jnp.where(even_lane, pltpu.roll(xn, HD - 1, axis=1), pltpu.roll(xn, 1, axis=1))
        return xn * cos + swapped * sin

    for hh in range(ATT_HEADS):
        q_ref[:, hh * HD:(hh + 1) * HD] = norm_rope(p[:, hh * HD:(hh + 1) * HD], gq_ref[...]).astype(BF16)
    for hh in range(ATT_KV):
        k_ref[:, hh * HD:(hh + 1) * HD] = norm_rope(p[:, Q_W + hh * HD:Q_W + (hh + 1) * HD], gk_ref[...])
    v_ref[...] = p[:, Q_W + KV_W:]


def _qkv(x_all, mod_l, n1g, win, gq, gk, cos_tab, sin_tab):
    nb_ctx = N_CTX // QKV_TB
    per_seq = DEC_SEQ // QKV_TB
    cond = lambda i: jnp.where(i < nb_ctx, 0, 1 + (i - nb_ctx) // per_seq)
    tab = lambda i: jnp.where(i < nb_ctx, 0, 1 + (i - nb_ctx) % per_seq)
    const2 = lambda i: (0, 0)
    return pl.pallas_call(
        _qkv_kernel,
        grid=(N_TOK // QKV_TB,),
        in_specs=[
            pl.BlockSpec((QKV_TB, D), lambda i: (i, 0)),
            pl.BlockSpec((None, 6, D), lambda i: (cond(i), 0, 0)),
            _resident((1, D), const2),
            _resident((D, Q_W + 2 * KV_W), const2),
            _resident((1, HD), const2),
            _resident((1, HD), const2),
            pl.BlockSpec((None, QKV_TB, HD), lambda i: (tab(i), 0, 0)),
            pl.BlockSpec((None, QKV_TB, HD), lambda i: (tab(i), 0, 0)),
        ],
        out_specs=[
            pl.BlockSpec((QKV_TB, Q_W), lambda i: (i, 0)),
            pl.BlockSpec((QKV_TB, KV_W), lambda i: (i, 0)),
            pl.BlockSpec((QKV_TB, KV_W), lambda i: (i, 0)),
        ],
        out_shape=[
            jax.ShapeDtypeStruct((N_TOK, Q_W), BF16),
            jax.ShapeDtypeStruct((N_TOK, KV_W), F32),
            jax.ShapeDtypeStruct((N_TOK, KV_W), F32),
        ],
        compiler_params=_cp("arbitrary"),
        name="odd_qkv",
    )(x_all, mod_l, n1g, win, gq, gk, cos_tab, sin_tab)


ATT_TQ = 256


def _attn_kernel(*refs, n_kv):
    q_ref = refs[0]
    kv_refs = refs[1:1 + 2 * n_kv]
    x_ref, mod_ref, wout_ref, xo_ref, att_scr = refs[1 + 2 * n_kv:]
    gate = mod_ref[2:3, :]
    for kh in range(ATT_KV):
        ks = [kv_refs[2 * s][:, kh * HD:(kh + 1) * HD].astype(BF16) for s in range(n_kv)]
        vs = [kv_refs[2 * s + 1][:, kh * HD:(kh + 1) * HD].astype(BF16) for s in range(n_kv)]
        for g in range(ATT_G):
            hh = kh * ATT_G + g
            qh = q_ref[:, hh * HD:(hh + 1) * HD]
            ss = [_dot_nt(qh, kk) * (HD ** -0.5) for kk in ks]
            m = ss[0].max(axis=-1, keepdims=True)
            for s in ss[1:]:
                m = jnp.maximum(m, s.max(axis=-1, keepdims=True))
            ps = [jnp.exp(s - m) for s in ss]
            den = ps[0].sum(axis=-1, keepdims=True)
            for p in ps[1:]:
                den = den + p.sum(axis=-1, keepdims=True)
            o = _dot(ps[0], vs[0])
            for p, vv in zip(ps[1:], vs[1:]):
                o = o + _dot(p, vv)
            att_scr[:, hh * HD:(hh + 1) * HD] = o / den
    y = _dot(att_scr[...], wout_ref[...])
    xo_ref[...] = x_ref[...] + gate * y


def _attention(x_all, mod_l, q, k, v, wout, cache_k=None, cache_v=None, layer_i=0):
    latent = cache_k is not None
    const2 = lambda *a: (0, 0)
    if latent:
        nq = DEC_SEQ // ATT_TQ
        row_blk = lambda b, j: (N_CTX // ATT_TQ + b * nq + j, 0)
        grid = (DEC_BATCH, nq)
        kv_specs = [
            pl.BlockSpec((None, None, SEQ, KV_W), lambda b, j: (b, layer_i, 0, 0)),
            pl.BlockSpec((None, None, SEQ, KV_W), lambda b, j: (b, layer_i, 0, 0)),
            pl.BlockSpec((DEC_SEQ, KV_W), lambda b, j: (N_CTX // DEC_SEQ + b, 0)),
            pl.BlockSpec((DEC_SEQ, KV_W), lambda b, j: (N_CTX // DEC_SEQ + b, 0)),
        ]
        kv_args = (cache_k, cache_v, k, v)
        mod_spec = pl.BlockSpec((None, 6, D), lambda b, j: (1 + b, 0, 0))
        sem = ("arbitrary", "arbitrary")
        n_kv = 2
    else:
        row_blk = lambda i: (i, 0)
        grid = (BATCH,)
        kv_specs = [pl.BlockSpec((SEQ, KV_W), row_blk), pl.BlockSpec((SEQ, KV_W), row_blk)]
        kv_args = (k, v)
        mod_spec = pl.BlockSpec((None, 6, D), lambda i: (0, 0, 0))
        sem = ("arbitrary",)
        n_kv = 1
    n_in = 1 + len(kv_args)
    return pl.pallas_call(
        functools.partial(_attn_kernel, n_kv=n_kv),
        grid=grid,
        in_specs=[pl.BlockSpec((ATT_TQ, Q_W), row_blk)] + kv_specs + [
            pl.BlockSpec((ATT_TQ, D), row_blk),
            mod_spec,
            _resident((D, D), const2),
        ],
        out_specs=pl.BlockSpec((ATT_TQ, D), row_blk),
        out_shape=jax.ShapeDtypeStruct((N_TOK, D), F32),
        scratch_shapes=[pltpu.VMEM((ATT_TQ, Q_W), F32)],
        input_output_aliases={n_in: 0},
        compiler_params=_cp(*sem),
        name="attention_latent" if latent else "attention_context",
    )(q, *kv_args, x_all, mod_l, wout)


ROUTE_TB = 512
HALF_TOK = N_TOK // 2
M_E1, M_E2, M_G1, M_G2, M_R1, M_R2 = 0, 1, 2, 3, 4, 5


def _router_kernel(x_ref, mod_ref, n2g_ref, whi_ref, wlo_ref, br_ref, h_ref, meta_ref, metat_ref, cnt_ref, run):
    @pl.when(pl.program_id(0) % (HALF_TOK // ROUTE_TB) == 0)
    def _():
        run[...] = jnp.zeros_like(run)

    shift, scale = mod_ref[3:4, :], mod_ref[4:5, :]
    h = _rms(x_ref[...], n2g_ref[...]) * (1.0 + scale) + shift
    _rows_to_tiles(h_ref, h)
    h_hi = h.astype(BF16)
    h_lo = (h - h_hi.astype(F32)).astype(BF16)
    dot = functools.partial(jnp.dot, preferred_element_type=F32)
    logits = dot(h_hi, whi_ref[...]) + dot(h_lo, whi_ref[...]) + dot(h_hi, wlo_ref[...]) + br_ref[...]
    lane = lax.broadcasted_iota(jnp.int32, logits.shape, 1).astype(F32)
    big = 1e4

    def first_argmax(vals):
        m = vals.max(axis=-1, keepdims=True)
        return m, jnp.where(vals == m, lane, big).min(axis=-1, keepdims=True)

    gl = jnp.where((lane >= N_EXP) & (lane < N_EXP + MOE_GROUPS), logits, NEG)
    gmax, glane = first_argmax(gl)
    g_p = 1.0 / jnp.exp(gl - gmax).sum(axis=-1, keepdims=True)
    lo = (glane - N_EXP) * MOE_PER_GROUP
    el = jnp.where((lane >= lo) & (lane < lo + MOE_PER_GROUP), logits, NEG)
    m1, i1 = first_argmax(el)
    m2, i2 = first_argmax(jnp.where(lane == i1, NEG, el))
    t = jnp.exp(m2 - m1)
    w1 = 1.0 / (1.0 + t)
    sel1, sel2 = lane == i1, lane == i2
    onehot = jnp.where(sel1 | sel2, 1.0, 0.0)
    ri = lax.broadcasted_iota(jnp.int32, (ROUTE_TB, ROUTE_TB), 0)
    rj = lax.broadcasted_iota(jnp.int32, (ROUTE_TB, ROUTE_TB), 1)
    before = _dot(jnp.where(ri > rj, 1.0, 0.0), onehot) + run[...]
    r1 = jnp.where(sel1, before, 0.0).sum(axis=-1, keepdims=True)
    r2 = jnp.where(sel2, before, 0.0).sum(axis=-1, keepdims=True)
    run[...] += onehot.sum(axis=0, keepdims=True)
    cnt_ref[...] = run[...]
    meta = jnp.zeros_like(logits)
    for j, val in enumerate([i1, i2, w1 * g_p, (t * w1) * g_p, r1, r2]):
        meta = jnp.where(lane == j, val, meta)
    meta_ref[...] = meta
    metat_ref[...] = meta.T[0:8, :]


def _router(x_all, mod_l, n2g, wr, br):
    w_hi = wr.astype(BF16)
    wr_lo = (wr - w_hi.astype(F32)).astype(BF16)
    nb_ctx = N_CTX // ROUTE_TB
    per_seq = DEC_SEQ // ROUTE_TB
    cond = lambda i: jnp.where(i < nb_ctx, 0, 1 + (i - nb_ctx) // per_seq)
    const2 = lambda i: (0, 0)
    return pl.pallas_call(
        _router_kernel,
        grid=(N_TOK // ROUTE_TB,),
        in_specs=[
            pl.BlockSpec((ROUTE_TB, D), lambda i: (i, 0)),
            pl.BlockSpec((None, 6, D), lambda i: (cond(i), 0, 0)),
            _resident((1, D), const2),
            _resident((D, 128), const2),
            _resident((D, 128), const2),
            _resident((1, 128), const2),
        ],
        out_specs=[
            pl.BlockSpec((ROUTE_TB * 8, 128), lambda i: (i, 0)),
            pl.BlockSpec((ROUTE_TB, 128), lambda i: (i, 0)),
            pl.BlockSpec((8, ROUTE_TB), lambda i: (0, i)),
            pl.BlockSpec((None, 1, 128), lambda i: (i // (HALF_TOK // ROUTE_TB), 0, 0)),
        ],
        out_shape=[
            jax.ShapeDtypeStruct((N_TOK * 8, 128), F32),
            jax.ShapeDtypeStruct((N_TOK, 128), F32),
            jax.ShapeDtypeStruct((8, N_TOK), F32),
            jax.ShapeDtypeStruct((2, 1, 128), F32),
        ],
        scratch_shapes=[pltpu.VMEM((1, 128), F32)],
        compiler_params=_cp("arbitrary"),
        name="moe_router",
    )(x_all, mod_l, n2g, w_hi, wr_lo, br)


EXP_TM = 128
N_ASSIGN = 2 * N_TOK
N_GROUPS = 2 * N_EXP
MAX_TILES = N_ASSIGN // EXP_TM + N_GROUPS
N_SORTED = MAX_TILES * EXP_TM
ORDER_BLK = 4096


DUMMY8 = HALF_TOK * 8


def _order_kernel(pos_ref, gate_ref, pad_lo_ref, pad_hi_ref, src_ref, gs_ref):
    i = pl.program_id(0)
    half = i // (N_ASSIGN // 2 // ORDER_BLK)
    local8 = (i * (ORDER_BLK // 2) - half * HALF_TOK) * 8

    def body(t, carry):
        for k in range(2):
            p = pos_ref[2 * t + k]
            src_ref[p] = local8 + t * 8
            gs_ref[p] = gate_ref[2 * t + k]
        return carry

    lax.fori_loop(0, ORDER_BLK // 2, body, 0, unroll=8)

    @pl.when(i == 0)
    def _():
        def group(g, carry):
            def pad(p, c):
                src_ref[p] = DUMMY8
                gs_ref[p] = 0.0
                return c
            return lax.fori_loop(pad_lo_ref[g], pad_hi_ref[g], pad, carry)

        lax.fori_loop(0, N_GROUPS, group, 0)


def _order(pos, gates, pad_lo, pad_hi):
    return pl.pallas_call(
        _order_kernel,
        grid=(N_ASSIGN // ORDER_BLK,),
        in_specs=[
            pl.BlockSpec((ORDER_BLK,), lambda i: (i,), memory_space=pltpu.SMEM),
            pl.BlockSpec((ORDER_BLK,), lambda i: (i,), memory_space=pltpu.SMEM),
            pl.BlockSpec(memory_space=pltpu.SMEM),
            pl.BlockSpec(memory_space=pltpu.SMEM),
        ],
        out_specs=[pl.BlockSpec(memory_space=pltpu.SMEM), pl.BlockSpec(memory_space=pltpu.SMEM)],
        out_shape=[jax.ShapeDtypeStruct((N_SORTED,), jnp.int32), jax.ShapeDtypeStruct((N_SORTED,), F32)],
        compiler_params=_cp("arbitrary"),
        name="moe_order",
    )(pos, gates, pad_lo, pad_hi)


ACC_TOK = HALF_TOK + 64
ROW_GROUP = 8


def _experts_kernel(tile0_ref, ntile_ref, src_ref, gs_ref, h_hbm, wg_ref, wu_ref, wd_ref, out_hbm,
                    h_res, acc, xbuf, ybuf, wgb, wub, wdb, sem):
    group = pl.program_id(0)
    expert = group % N_EXP
    rows0 = pl.multiple_of((group // N_EXP) * (HALF_TOK * 8), 8)

    @pl.when(expert == 0)
    def _():
        cp = pltpu.make_async_copy(h_hbm.at[pl.ds(rows0, HALF_TOK * 8), :], h_res, sem)
        cp.start()

        def zero(i, carry):
            acc[pl.ds(pl.multiple_of(i * 512, 512), 512), :] = jnp.zeros((512, 128), F32)
            return carry

        lax.fori_loop(0, ACC_TOK * 8 // 512, zero, 0)
        cp.wait()

    n_tiles = ntile_ref[group]

    @pl.when(n_tiles > 0)
    def _():
        wgb[...] = wg_ref[...].astype(BF16)
        wub[...] = wu_ref[...].astype(BF16)
        wdb[...] = wd_ref[...].astype(BF16)

    def tile_body(j, carry):
        base = (tile0_ref[group] + j) * EXP_TM

        def gather(g, c):
            for i in range(ROW_GROUP):
                r = g * ROW_GROUP + i
                row8 = jnp.minimum(src_ref[base + r], DUMMY8 - 8)
                xbuf[pl.ds(pl.multiple_of(r * 8, 8), 8), :] = _tile_of(h_res, row8)[...]
            return c

        lax.fori_loop(0, EXP_TM // ROW_GROUP, gather, 0)
        x = _tiles_to_rows(xbuf, EXP_TM).astype(BF16)
        hid = _silu(_dot(x, wgb[...])) * _dot(x, wub[...])
        _rows_to_tiles(ybuf, _dot(hid, wdb[...]))

        def accumulate(g, c):
            targets, values = [], []
            for i in range(ROW_GROUP):
                r = g * ROW_GROUP + i
                target = _tile_of(acc, src_ref[base + r])
                targets.append(target)
                values.append(target[...] + gs_ref[base + r] * ybuf[pl.ds(pl.multiple_of(r * 8, 8), 8), :])
            for target, value in zip(targets, values):
                target[...] = value
            return c

        lax.fori_loop(0, EXP_TM // ROW_GROUP, accumulate, 0)
        return carry

    lax.fori_loop(0, n_tiles, tile_body, 0)

    @pl.when(expert == N_EXP - 1)
    def _():
        cp = pltpu.make_async_copy(acc.at[pl.ds(0, HALF_TOK * 8), :], out_hbm.at[pl.ds(rows0, HALF_TOK * 8), :], sem)
        cp.start()
        cp.wait()


def _experts(tile0, n_tiles, src, gs, h, wg, wu, wd, layer):
    wmap = lambda g, t0, nt, src: (layer, g % N_EXP, 0, 0)
    return pl.pallas_call(
        _experts_kernel,
        grid_spec=pltpu.PrefetchScalarGridSpec(
            num_scalar_prefetch=3,
            grid=(N_GROUPS,),
            in_specs=[
                pl.BlockSpec(memory_space=pltpu.SMEM),
                pl.BlockSpec(memory_space=pl.ANY),
                pl.BlockSpec((None, None, D, D_EXP), wmap),
                pl.BlockSpec((None, None, D, D_EXP), wmap),
                pl.BlockSpec((None, None, D_EXP, D), wmap),
            ],
            out_specs=pl.BlockSpec(memory_space=pl.ANY),
            scratch_shapes=[
                pltpu.VMEM((HALF_TOK * 8, 128), F32),
                pltpu.VMEM((ACC_TOK * 8, 128), F32),
                pltpu.VMEM((EXP_TM * 8, 128), F32),
                pltpu.VMEM((EXP_TM * 8, 128), F32),
                pltpu.VMEM((D, D_EXP), BF16),
                pltpu.VMEM((D, D_EXP), BF16),
                pltpu.VMEM((D_EXP, D), BF16),
                pltpu.SemaphoreType.DMA,
            ],
        ),
        out_shape=jax.ShapeDtypeStruct((N_TOK * 8, 128), F32),
        compiler_params=_cp("arbitrary"),
        name="moe_experts",
    )(tile0, n_tiles, src, gs, h, wg, wu, wd)


COMBINE_TB = 256


def _combine_kernel(y_ref, x_ref, mod_ref, xo_ref):
    xo_ref[...] = x_ref[...] + mod_ref[5:6, :] * _tiles_to_rows(y_ref, COMBINE_TB)


def _combine(ys, x_all, mod_l):
    nb_ctx = N_CTX // COMBINE_TB
    per_seq = DEC_SEQ // COMBINE_TB
    cond = lambda i: jnp.where(i < nb_ctx, 0, 1 + (i - nb_ctx) // per_seq)
    return pl.pallas_call(
        _combine_kernel,
        grid=(N_TOK // COMBINE_TB,),
        in_specs=[
            pl.BlockSpec((COMBINE_TB * 8, 128), lambda i: (i, 0)),
            pl.BlockSpec((COMBINE_TB, D), lambda i: (i, 0)),
            pl.BlockSpec((None, 6, D), lambda i: (cond(i), 0, 0)),
        ],
        out_specs=pl.BlockSpec((COMBINE_TB, D), lambda i: (i, 0)),
        out_shape=jax.ShapeDtypeStruct((N_TOK, D), F32),
        compiler_params=_cp("arbitrary"),
        name="moe_combine",
    )(ys, x_all, mod_l)


def _moe(x_all, mod_l, n2g, wr, br, wg, wu, wd, layer):
    h, meta, metat, cnt = _router(x_all, mod_l, n2g, wr, br)
    counts = cnt[:, 0, :N_EXP].astype(jnp.int32).reshape(N_GROUPS)
    padded = (counts + EXP_TM - 1) // EXP_TM * EXP_TM
    ends = jnp.cumsum(padded)
    offs = ends - padded
    rec = metat.astype(jnp.int32)
    half = (jnp.arange(N_TOK, dtype=jnp.int32) // HALF_TOK)[None, :]
    pos = offs[rec[M_E1:M_E2 + 1] + N_EXP * half] + rec[M_R1:M_R2 + 1]
    pos = pos.T.reshape(N_ASSIGN)
    gates = metat[M_G1:M_G2 + 1].T.reshape(N_ASSIGN)
    src, gs = _order(pos, gates, offs + counts, ends)
    ys = _experts(offs // EXP_TM, padded // EXP_TM, src, gs, h, wg, wu, wd, layer)
    return _combine(ys, x_all, mod_l)


def _final_kernel(x_ref, g_ref, o_ref):
    o_ref[...] = _rms(x_ref[...], g_ref[...])


def _final_norm(x_all, g, blk0, n_rows):
    tb = 512
    return pl.pallas_call(
        _final_kernel,
        grid=(n_rows // tb,),
        in_specs=[pl.BlockSpec((tb, D), lambda i: (blk0 + i, 0)), _resident((1, D), lambda i: (0, 0))],
        out_specs=pl.BlockSpec((tb, D), lambda i: (i, 0)),
        out_shape=jax.ShapeDtypeStruct((n_rows, D), F32),
        compiler_params=_cp("arbitrary"),
        name="final_norm",
    )(x_all, g)


def _rope_tables():
    pos = jnp.arange(DEC_SEQ)
    row = (pos // GRID_W).astype(F32)
    col = (pos % GRID_W).astype(F32)
    n_freq = HD // 4
    inv = ROPE_THETA ** (-jnp.arange(n_freq, dtype=F32) / n_freq)
    ang = jnp.concatenate([row[:, None] * inv, col[:, None] * inv], axis=-1)
    cos = jnp.repeat(jnp.cos(ang), 2, axis=-1)
    sin = jnp.repeat(jnp.sin(ang), 2, axis=-1) * jnp.tile(jnp.array([-1.0, 1.0], F32), HD // 2)
    nblk = DEC_SEQ // QKV_TB
    cos_tab = jnp.concatenate([jnp.ones((1, QKV_TB, HD), F32), cos.reshape(nblk, QKV_TB, HD)], axis=0)
    sin_tab = jnp.concatenate([jnp.zeros((1, QKV_TB, HD), F32), sin.reshape(nblk, QKV_TB, HD)], axis=0)
    return cos_tab, sin_tab


def kernel(x_prompt, x_sample, state_gla, cache_k, cache_v, c, c_ctx, w_mod, b_mod, norm1_g, norm2_g,
           w_in_even, w_gate_up, b_gate_up, gla_norm_g, w_spatial, b_spatial, w_out_even,
           w_in_odd, q_norm_g, k_norm_g, w_out_odd, w_router_group, b_router_group,
           w_router_expert, b_router_expert, w_exp_gate, w_exp_up, w_exp_down, final_norm_g):
    x_all = jnp.concatenate([x_prompt.reshape(N_CTX, D), x_sample.reshape(N_LAT, D)], axis=0)
    cond8 = jnp.concatenate([c_ctx[None], c, jnp.zeros((3, D), F32)], axis=0)
    mod = _modulation(cond8, w_mod, b_mod)
    cos_tab, sin_tab = _rope_tables()
    zero_state = jnp.zeros((1, 2, QK_W, GLA_DV), F32)
    state_in = state_gla.reshape(DEC_BATCH, -1, 2, QK_W, GLA_DV)
    cache_k2 = cache_k.reshape(DEC_BATCH, -1, SEQ, KV_W)
    cache_v2 = cache_v.reshape(DEC_BATCH, -1, SEQ, KV_W)

    gla_states, ctx_k, ctx_v = [], [], []
    for l in range(DEPTH):
        i = l // 2
        n1g = norm1_g[l][None]
        if l % 2 == 0:
            w = w_in_even[i]
            win = jnp.concatenate([w[:, :1536], w[:, 1568:], w[:, 1536:1568], jnp.zeros((D, 96), F32)],
                                  axis=1).astype(BF16)
            wgu = jnp.zeros((128, 2 * QK_W), F32)
            wgu = wgu.at[0:GLA_RANK, 0:QK_W].set(w_gate_up[i, 0])
            wgu = wgu.at[GLA_RANK:2 * GLA_RANK, QK_W:].set(w_gate_up[i, 1]).astype(BF16)
            bgu = b_gate_up[i].reshape(1, 2 * QK_W)
            args = (mod[l], n1g, win, wgu, bgu, gla_norm_g[i][None], w_spatial[i].astype(BF16),
                    b_spatial[i].T, w_out_even[i].astype(BF16))
            x_all, st = _even_mixer(x_all, *args, zero_state, latent=False)
            gla_states.append(st)
            x_all, _ = _even_mixer(x_all, *args, state_in[:, i], latent=True)
        else:
            q, k, v = _qkv(x_all, mod[l], n1g, w_in_odd[i].astype(BF16), q_norm_g[i][None],
                           k_norm_g[i][None], cos_tab, sin_tab)
            ctx_k.append(k[:N_CTX].reshape(BATCH, SEQ, ATT_KV, HD))
            ctx_v.append(v[:N_CTX].reshape(BATCH, SEQ, ATT_KV, HD))
            wout = w_out_odd[i].astype(BF16)
            x_all = _attention(x_all, mod[l], q, k, v, wout)
            x_all = _attention(x_all, mod[l], q, k, v, wout, cache_k2, cache_v2, layer_i=i)
        wr = jnp.concatenate([w_router_expert[l], w_router_group[l],
                              jnp.zeros((D, 128 - N_EXP - MOE_GROUPS), F32)], axis=1)
        br = jnp.concatenate([b_router_expert[l], b_router_group[l],
                              jnp.zeros((128 - N_EXP - MOE_GROUPS,), F32)])[None]
        x_all = _moe(x_all, mod[l], norm2_g[l][None], wr, br, w_exp_gate, w_exp_up, w_exp_down, l)

    fg = final_norm_g[None]
    y_prompt = _final_norm(x_all, fg, 0, N_CTX).reshape(BATCH, SEQ, D)
    y_sample = _final_norm(x_all, fg, N_CTX // 512, N_LAT).reshape(DEC_BATCH, DEC_SEQ, D)
    new_state = jnp.stack(gla_states, axis=1).reshape(BATCH, -1, 2, GLA_HEADS, GLA_DK, GLA_DV)
    return (y_prompt, y_sample, new_state, jnp.stack(ctx_k, axis=1), jnp.stack(ctx_v, axis=1))
```

```python
import functools

import jax
import jax.numpy as jnp
import numpy as np
from jax import lax
from jax.experimental import pallas as pl
from jax.experimental.pallas import tpu as pltpu

F32 = jnp.float32
BF16 = jnp.bfloat16

D = 1024
BATCH, SEQ = 16, 256
DEC_BATCH, DEC_SEQ = 4, 1024
N_CTX = BATCH * SEQ
N_LAT = DEC_BATCH * DEC_SEQ
N_TOK = N_CTX + N_LAT
DEPTH = 4
EPS = 1e-6
GRID_W = 64
ROPE_THETA = 10000.0

GLA_HEADS, GLA_DK, GLA_DV, GLA_RANK, GLA_CHUNK, GLA_TAU = 4, 64, 128, 16, 64, 16.0
QK_W = GLA_HEADS * GLA_DK
V_W = GLA_HEADS * GLA_DV
GMLP_GROUPS, GMLP_DIM, GMLP_CHUNK = 4, 128, 128
GMLP_W = GMLP_GROUPS * GMLP_DIM
C_Q, C_K, C_V, C_G, C_U, C_VG, C_A = 0, 256, 512, 1024, 1536, 2048, 2560
EVEN_PACK = 2688

ATT_HEADS, ATT_KV, HD = 8, 2, 128
ATT_G = ATT_HEADS // ATT_KV
Q_W = ATT_HEADS * HD
KV_W = ATT_KV * HD

MOE_GROUPS, MOE_PER_GROUP = 4, 8
N_EXP = MOE_GROUPS * MOE_PER_GROUP
D_EXP = D // 4
NEG = -1e30

VMEM_LIMIT = 56 * 1024 * 1024


def _cp(*sem):
    return pltpu.CompilerParams(dimension_semantics=sem, vmem_limit_bytes=VMEM_LIMIT)


def _dot(a, b):
    return jnp.dot(a.astype(BF16), b.astype(BF16), preferred_element_type=F32)


def _dot_nt(a, b):
    return lax.dot_general(a.astype(BF16), b.astype(BF16), (((1,), (1,)), ((), ())),
                           preferred_element_type=F32)


def _dot_tn(a, b):
    return lax.dot_general(a.astype(BF16), b.astype(BF16), (((0,), (0,)), ((), ())),
                           preferred_element_type=F32)


def _rms(x, g):
    return x * lax.rsqrt(jnp.mean(x * x, axis=-1, keepdims=True) + EPS) * g


def _silu(x):
    return x * jax.nn.sigmoid(x)


def _gelu(x):
    return 0.5 * x * (1.0 + jnp.tanh(np.sqrt(2.0 / np.pi).astype(np.float32) * (x + 0.044715 * (x * x * x))))


def _log_sigmoid(z):
    return jnp.minimum(z, 0.0) - jnp.log(1.0 + jnp.exp(-jnp.abs(z)))


def _rows_to_tiles(ref, x):
    rows = x.shape[0]
    for j in range(D // 128):
        ref[pl.ds(j, rows, stride=8), :] = x[:, j * 128:(j + 1) * 128]


def _tiles_to_rows(ref, rows):
    return jnp.concatenate([ref[pl.ds(j, rows, stride=8), :] for j in range(D // 128)], axis=1)


def _tile_of(ref, row8):
    return ref.at[pl.ds(pl.multiple_of(row8, 8), 8), :]


def _resident(shape, index_map):
    return pl.BlockSpec(shape, index_map, pipeline_mode=pl.Buffered(1))


def _mod_kernel(cond_ref, w_ref, b_ref, o_ref):
    c = cond_ref[...]
    o_ref[...] = jnp.dot(_silu(c), w_ref[...], precision=lax.Precision.HIGHEST,
                         preferred_element_type=F32) + b_ref[...]


def _modulation(cond8, w_mod, b_mod):
    tn = 1024
    out = pl.pallas_call(
        _mod_kernel,
        grid=(DEPTH, 6 * D // tn),
        in_specs=[
            pl.BlockSpec((8, D), lambda l, j: (0, 0)),
            pl.BlockSpec((None, D, tn), lambda l, j: (l, 0, j)),
            pl.BlockSpec((None, 1, tn), lambda l, j: (l, 0, j)),
        ],
        out_specs=pl.BlockSpec((None, 8, tn), lambda l, j: (l, 0, j)),
        out_shape=jax.ShapeDtypeStruct((DEPTH, 8, 6 * D), F32),
        compiler_params=_cp("arbitrary", "arbitrary"),
        name="adaln_mod",
    )(cond8, w_mod, b_mod.reshape(DEPTH, 1, 6 * D))
    return out.reshape(DEPTH, 8, 6, D)


def _even_kernel(x_ref, mod_ref, n1g_ref, win_ref, wgu_ref, bgu_ref, glag_ref, ws_ref, bs_ref,
                 wout_ref, s0_ref, xo_ref, st_ref, proj, la, o_f, o_b, st_scr, *, T):
    n_chunks = T // GLA_CHUNK
    shift, scale, gate = mod_ref[0:1, :], mod_ref[1:2, :], mod_ref[2:3, :]
    RB = 128

    def proj_body(r, carry):
        r0 = pl.multiple_of(r * RB, RB)
        h = _rms(x_ref[pl.ds(r0, RB), :], n1g_ref[...]) * (1.0 + scale) + shift
        p = _dot(h, win_ref[...])
        proj[pl.ds(r0, RB), :] = p
        z = _dot(p[:, C_A:C_A + 128], wgu_ref[...]) + bgu_ref[...]
        la[pl.ds(r0, RB), :] = _log_sigmoid(z) * (1.0 / GLA_TAU)
        return carry

    lax.fori_loop(0, T // RB, proj_body, 0)

    st_scr[0] = s0_ref[0].T
    st_scr[1] = s0_ref[1].T

    ci = lax.broadcasted_iota(jnp.int32, (GLA_CHUNK, GLA_CHUNK), 0)
    cj = lax.broadcasted_iota(jnp.int32, (GLA_CHUNK, GLA_CHUNK), 1)
    tri = (jnp.where(ci >= cj, 1.0, 0.0).astype(BF16), jnp.where(ci <= cj, 1.0, 0.0).astype(BF16))
    ai = lax.broadcasted_iota(jnp.int32, (GLA_HEADS * GLA_CHUNK, GLA_CHUNK), 0) % GLA_CHUNK
    aj = lax.broadcasted_iota(jnp.int32, (GLA_HEADS * GLA_CHUNK, GLA_CHUNK), 1)
    amask = (ai >= aj, ai <= aj)
    lane_head = lax.broadcasted_iota(jnp.int32, (1, QK_W), 1) // GLA_DK
    hmask = [jnp.where(lane_head == h, 1.0, 0.0) for h in range(GLA_HEADS)]

    def chunk_body(i, carry):
        for d in range(2):
            c = i if d == 0 else n_chunks - 1 - i
            r0 = pl.multiple_of(c * GLA_CHUNK, GLA_CHUNK)
            q = proj[pl.ds(r0, GLA_CHUNK), C_Q:C_Q + QK_W] * (GLA_DK ** -0.5)
            k = proj[pl.ds(r0, GLA_CHUNK), C_K:C_K + QK_W]
            v = proj[pl.ds(r0, GLA_CHUNK), C_V:C_V + V_W]
            lac = la[pl.ds(r0, GLA_CHUNK), d * QK_W:(d + 1) * QK_W]
            hi = lac.astype(BF16)
            lo = (lac - hi.astype(F32)).astype(BF16)
            b = (jnp.dot(tri[d], hi, preferred_element_type=F32)
                 + jnp.dot(tri[d], lo, preferred_element_type=F32))
            bend = b[GLA_CHUNK - 1:GLA_CHUNK, :] if d == 0 else b[0:1, :]
            qe = q * jnp.exp(b)
            ke = k * jnp.exp(-b)
            kd = k * jnp.exp(bend - b)
            st = st_scr[d]
            qstack = jnp.concatenate([qe * hmask[h] for h in range(GLA_HEADS)], axis=0).astype(BF16)
            att = jnp.where(amask[d], _dot_nt(qstack, ke), 0.0)
            inter = _dot_nt(qstack, st)
            outs = []
            for h in range(GLA_HEADS):
                rows = slice(h * GLA_CHUNK, (h + 1) * GLA_CHUNK)
                outs.append(_dot(att[rows], v[:, h * GLA_DV:(h + 1) * GLA_DV]) + inter[rows])
            o = jnp.concatenate(outs, axis=1)
            if d == 0:
                o_f[pl.ds(r0, GLA_CHUNK), :] = o
            else:
                o_b[pl.ds(r0, GLA_CHUNK), :] = o
            vstack = jnp.concatenate([v[:, h * GLA_DV:(h + 1) * GLA_DV] for h in range(GLA_HEADS)], axis=0)
            kstack = jnp.concatenate([kd * hmask[h] for h in range(GLA_HEADS)], axis=0)
            st_scr[d] = st * jnp.exp(bend) + _dot_tn(vstack, kstack)
        return carry

    lax.fori_loop(0, n_chunks, chunk_body, 0, unroll=4)
    st_ref[0] = st_scr[0].T
    st_ref[1] = st_scr[1].T

    def out_body(r, carry):
        r0 = pl.multiple_of(r * RB, RB)
        osum = o_f[pl.ds(r0, RB), :] + o_b[pl.ds(r0, RB), :]
        g = proj[pl.ds(r0, RB), C_G:C_G + V_W]
        u = proj[pl.ds(r0, RB), C_U:C_U + GMLP_W]
        vg = _gelu(proj[pl.ds(r0, RB), C_VG:C_VG + GMLP_W])
        parts = []
        for h in range(GLA_HEADS):
            oh = osum[:, h * GLA_DV:(h + 1) * GLA_DV]
            parts.append(_rms(oh, glag_ref[...]) * _silu(g[:, h * GLA_DV:(h + 1) * GLA_DV]))
        for gi in range(GMLP_GROUPS):
            vc = vg[:, gi * GMLP_DIM:(gi + 1) * GMLP_DIM]
            vc = vc - jnp.mean(vc, axis=-1, keepdims=True)
            vn = vc * lax.rsqrt(jnp.mean(vc * vc, axis=-1, keepdims=True) + EPS)
            sg = _dot(ws_ref[gi], vn) + bs_ref[:, gi:gi + 1]
            parts.append(_gelu(u[:, gi * GMLP_DIM:(gi + 1) * GMLP_DIM]) * sg)
        mix = jnp.concatenate(parts, axis=1)
        y = _dot(mix, wout_ref[...])
        xo_ref[pl.ds(r0, RB), :] = x_ref[pl.ds(r0, RB), :] + gate * y
        return carry

    lax.fori_loop(0, T // RB, out_body, 0)


def _even_mixer(x_all, mod_l, n1g, win, wgu, bgu, glag, ws, bs, wout, s0, *, latent):
    if latent:
        T, nseq, blk0 = DEC_SEQ, DEC_BATCH, N_CTX // DEC_SEQ
        cond = lambda i: 1 + i
        s0_spec = pl.BlockSpec((None, 2, QK_W, GLA_DV), lambda i: (i, 0, 0, 0))
    else:
        T, nseq, blk0 = SEQ, BATCH, 0
        cond = lambda i: 0
        s0_spec = pl.BlockSpec((None, 2, QK_W, GLA_DV), lambda i: (0, 0, 0, 0))
    const2 = lambda i: (0, 0)
    x_new, states = pl.pallas_call(
        functools.partial(_even_kernel, T=T),
        grid=(nseq,),
        in_specs=[
            pl.BlockSpec((T, D), lambda i: (blk0 + i, 0)),
            pl.BlockSpec((None, 6, D), lambda i: (cond(i), 0, 0)),
            _resident((1, D), const2),
            _resident((D, EVEN_PACK), const2),
            _resident((128, 2 * QK_W), const2),
            _resident((1, 2 * QK_W), const2),
            _resident((1, GLA_DV), const2),
            _resident((GMLP_GROUPS, GMLP_CHUNK, GMLP_CHUNK), lambda i: (0, 0, 0)),
            _resident((GMLP_CHUNK, GMLP_GROUPS), const2),
            _resident((D, D), const2),
            s0_spec,
        ],
        out_specs=[
            pl.BlockSpec((T, D), lambda i: (blk0 + i, 0)),
            pl.BlockSpec((None, 2, QK_W, GLA_DV), lambda i: (i, 0, 0, 0)),
        ],
        out_shape=[
            jax.ShapeDtypeStruct((N_TOK, D), F32),
            jax.ShapeDtypeStruct((nseq, 2, QK_W, GLA_DV), F32),
        ],
        scratch_shapes=[
            pltpu.VMEM((T, EVEN_PACK), F32),
            pltpu.VMEM((T, 2 * QK_W), F32),
            pltpu.VMEM((T, V_W), F32),
            pltpu.VMEM((T, V_W), F32),
            pltpu.VMEM((2, GLA_DV, QK_W), F32),
        ],
        input_output_aliases={0: 0},
        compiler_params=_cp("arbitrary"),
        name="even_mixer_latent" if latent else "even_mixer_context",
    )(x_all, mod_l, n1g, win, wgu, bgu, glag, ws, bs, wout, s0)
    return x_new, states


QKV_TB = 512


def _qkv_kernel(x_ref, mod_ref, n1g_ref, win_ref, gq_ref, gk_ref, cos_ref, sin_ref, q_ref, k_ref, v_ref):
    shift, scale = mod_ref[0:1, :], mod_ref[1:2, :]
    h = _rms(x_ref[...], n1g_ref[...]) * (1.0 + scale) + shift
    p = _dot(h, win_ref[...])
    cos, sin = cos_ref[...], sin_ref[...]
    even_lane = lax.broadcasted_iota(jnp.int32, (1, HD), 1) % 2 == 0

    def norm_rope(xh, g):
        xn = _rms(xh, g)
        swapped = jnp.where(even_lane, pltpu.roll(xn, HD - 1, axis=1), pltpu.roll(xn, 1, axis=1))
        return xn * cos + swapped * sin

    for hh in range(ATT_HEADS):
        q_ref[:, hh * HD:(hh + 1) * HD] = norm_rope(p[:, hh * HD:(hh + 1) * HD], gq_ref[...]).astype(BF16)
    for hh in range(ATT_KV):
        k_ref[:, hh * HD:(hh + 1) * HD] = norm_rope(p[:, Q_W + hh * HD:Q_W + (hh + 1) * HD], gk_ref[...])
    v_ref[...] = p[:, Q_W + KV_W:]


def _qkv(x_all, mod_l, n1g, win, gq, gk, cos_tab, sin_tab):
    nb_ctx = N_CTX // QKV_TB
    per_seq = DEC_SEQ // QKV_TB
    cond = lambda i: jnp.where(i < nb_ctx, 0, 1 + (i - nb_ctx) // per_seq)
    tab = lambda i: jnp.where(i < nb_ctx, 0, 1 + (i - nb_ctx) % per_seq)
    const2 = lambda i: (0, 0)
    return pl.pallas_call(
        _qkv_kernel,
        grid=(N_TOK // QKV_TB,),
        in_specs=[
            pl.BlockSpec((QKV_TB, D), lambda i: (i, 0)),
            pl.BlockSpec((None, 6, D), lambda i: (cond(i), 0, 0)),
            _resident((1, D), const2),
            _resident((D, Q_W + 2 * KV_W), const2),
            _resident((1, HD), const2),
            _resident((1, HD), const2),
            pl.BlockSpec((None, QKV_TB, HD), lambda i: (tab(i), 0, 0)),
            pl.BlockSpec((None, QKV_TB, HD), lambda i: (tab(i), 0, 0)),
        ],
        out_specs=[
            pl.BlockSpec((QKV_TB, Q_W), lambda i: (i, 0)),
            pl.BlockSpec((QKV_TB, KV_W), lambda i: (i, 0)),
            pl.BlockSpec((QKV_TB, KV_W), lambda i: (i, 0)),
        ],
        out_shape=[
            jax.ShapeDtypeStruct((N_TOK, Q_W), BF16),
            jax.ShapeDtypeStruct((N_TOK, KV_W), F32),
            jax.ShapeDtypeStruct((N_TOK, KV_W), F32),
        ],
        compiler_params=_cp("arbitrary"),
        name="odd_qkv",
    )(x_all, mod_l, n1g, win, gq, gk, cos_tab, sin_tab)


ATT_TQ = 256


def _attn_kernel(*refs, n_kv):
    q_ref = refs[0]
    kv_refs = refs[1:1 + 2 * n_kv]
    x_ref, mod_ref, wout_ref, xo_ref, att_scr = refs[1 + 2 * n_kv:]
    gate = mod_ref[2:3, :]
    for kh in range(ATT_KV):
        ks = [kv_refs[2 * s][:, kh * HD:(kh + 1) * HD].astype(BF16) for s in range(n_kv)]
        vs = [kv_refs[2 * s + 1][:, kh * HD:(kh + 1) * HD].astype(BF16) for s in range(n_kv)]
        for g in range(ATT_G):
            hh = kh * ATT_G + g
            qh = q_ref[:, hh * HD:(hh + 1) * HD]
            ss = [_dot_nt(qh, kk) * (HD ** -0.5) for kk in ks]
            m = ss[0].max(axis=-1, keepdims=True)
            for s in ss[1:]:
                m = jnp.maximum(m, s.max(axis=-1, keepdims=True))
            ps = [jnp.exp(s - m) for s in ss]
            den = ps[0].sum(axis=-1, keepdims=True)
            for p in ps[1:]:
                den = den + p.sum(axis=-1, keepdims=True)
            o = _dot(ps[0], vs[0])
            for p, vv in zip(ps[1:], vs[1:]):
                o = o + _dot(p, vv)
            att_scr[:, hh * HD:(hh + 1) * HD] = o / den
    y = _dot(att_scr[...], wout_ref[...])
    xo_ref[...] = x_ref[...] + gate * y


def _attention(x_all, mod_l, q, k, v, wout, cache_k=None, cache_v=None, layer_i=0):
    latent = cache_k is not None
    const2 = lambda *a: (0, 0)
    if latent:
        nq = DEC_SEQ // ATT_TQ
        row_blk = lambda b, j: (N_CTX // ATT_TQ + b * nq + j, 0)
        grid = (DEC_BATCH, nq)
        kv_specs = [
            pl.BlockSpec((None, None, SEQ, KV_W), lambda b, j: (b, layer_i, 0, 0)),
            pl.BlockSpec((None, None, SEQ, KV_W), lambda b, j: (b, layer_i, 0, 0)),
            pl.BlockSpec((DEC_SEQ, KV_W), lambda b, j: (N_CTX // DEC_SEQ + b, 0)),
            pl.BlockSpec((DEC_SEQ, KV_W), lambda b, j: (N_CTX // DEC_SEQ + b, 0)),
        ]
        kv_args = (cache_k, cache_v, k, v)
        mod_spec = pl.BlockSpec((None, 6, D), lambda b, j: (1 + b, 0, 0))
        sem = ("arbitrary", "arbitrary")
        n_kv = 2
    else:
        row_blk = lambda i: (i, 0)
        grid = (BATCH,)
        kv_specs = [pl.BlockSpec((SEQ, KV_W), row_blk), pl.BlockSpec((SEQ, KV_W), row_blk)]
        kv_args = (k, v)
        mod_spec = pl.BlockSpec((None, 6, D), lambda i: (0, 0, 0))
        sem = ("arbitrary",)
        n_kv = 1
    n_in = 1 + len(kv_args)
    return pl.pallas_call(
        functools.partial(_attn_kernel, n_kv=n_kv),
        grid=grid,
        in_specs=[pl.BlockSpec((ATT_TQ, Q_W), row_blk)] + kv_specs + [
            pl.BlockSpec((ATT_TQ, D), row_blk),
            mod_spec,
            _resident((D, D), const2),
        ],
        out_specs=pl.BlockSpec((ATT_TQ, D), row_blk),
        out_shape=jax.ShapeDtypeStruct((N_TOK, D), F32),
        scratch_shapes=[pltpu.VMEM((ATT_TQ, Q_W), F32)],
        input_output_aliases={n_in: 0},
        compiler_params=_cp(*sem),
        name="attention_latent" if latent else "attention_context",
    )(q, *kv_args, x_all, mod_l, wout)


ROUTE_TB = 512
HALF_TOK = N_TOK // 2
M_E1, M_E2, M_G1, M_G2, M_R1, M_R2 = 0, 1, 2, 3, 4, 5


def _router_kernel(x_ref, mod_ref, n2g_ref, whi_ref, wlo_ref, br_ref, h_ref, metat_ref, cnt_ref, run):
    @pl.when(pl.program_id(0) % (HALF_TOK // ROUTE_TB) == 0)
    def _():
        run[...] = jnp.zeros_like(run)

    shift, scale = mod_ref[3:4, :], mod_ref[4:5, :]
    h = _rms(x_ref[...], n2g_ref[...]) * (1.0 + scale) + shift
    _rows_to_tiles(h_ref, h)
    h_hi = h.astype(BF16)
    h_lo = (h - h_hi.astype(F32)).astype(BF16)
    dot = functools.partial(jnp.dot, preferred_element_type=F32)
    logits = dot(h_hi, whi_ref[...]) + dot(h_lo, whi_ref[...]) + dot(h_hi, wlo_ref[...]) + br_ref[...]
    lane = lax.broadcasted_iota(jnp.int32, logits.shape, 1).astype(F32)
    big = 1e4

    def first_argmax(vals):
        m = vals.max(axis=-1, keepdims=True)
        return m, jnp.where(vals == m, lane, big).min(axis=-1, keepdims=True)

    gl = jnp.where((lane >= N_EXP) & (lane < N_EXP + MOE_GROUPS), logits, NEG)
    gmax, glane = first_argmax(gl)
    g_p = 1.0 / jnp.exp(gl - gmax).sum(axis=-1, keepdims=True)
    lo = (glane - N_EXP) * MOE_PER_GROUP
    el = jnp.where((lane >= lo) & (lane < lo + MOE_PER_GROUP), logits, NEG)
    m1, i1 = first_argmax(el)
    m2, i2 = first_argmax(jnp.where(lane == i1, NEG, el))
    t = jnp.exp(m2 - m1)
    w1 = 1.0 / (1.0 + t)
    sel1, sel2 = lane == i1, lane == i2
    onehot = jnp.where(sel1 | sel2, 1.0, 0.0)
    ri = lax.broadcasted_iota(jnp.int32, (ROUTE_TB, ROUTE_TB), 0)
    rj = lax.broadcasted_iota(jnp.int32, (ROUTE_TB, ROUTE_TB), 1)
    before = _dot(jnp.where(ri > rj, 1.0, 0.0), onehot) + run[...]
    r1 = jnp.where(sel1, before, 0.0).sum(axis=-1, keepdims=True)
    r2 = jnp.where(sel2, before, 0.0).sum(axis=-1, keepdims=True)
    run[...] += onehot.sum(axis=0, keepdims=True)
    cnt_ref[...] = run[...]
    meta = jnp.zeros_like(logits)
    for j, val in enumerate([i1, i2, w1 * g_p, (t * w1) * g_p, r1, r2]):
        meta = jnp.where(lane == j, val, meta)
    metat_ref[...] = meta.T[0:8, :]


def _router(x_all, mod_l, n2g, wr, br):
    w_hi = wr.astype(BF16)
    wr_lo = (wr - w_hi.astype(F32)).astype(BF16)
    nb_ctx = N_CTX // ROUTE_TB
    per_seq = DEC_SEQ // ROUTE_TB
    cond = lambda i: jnp.where(i < nb_ctx, 0, 1 + (i - nb_ctx) // per_seq)
    const2 = lambda i: (0, 0)
    return pl.pallas_call(
        _router_kernel,
        grid=(N_TOK // ROUTE_TB,),
        in_specs=[
            pl.BlockSpec((ROUTE_TB, D), lambda i: (i, 0)),
            pl.BlockSpec((None, 6, D), lambda i: (cond(i), 0, 0)),
            _resident((1, D), const2),
            _resident((D, 128), const2),
            _resident((D, 128), const2),
            _resident((1, 128), const2),
        ],
        out_specs=[
            pl.BlockSpec((ROUTE_TB * 8, 128), lambda i: (i, 0)),
            pl.BlockSpec((8, ROUTE_TB), lambda i: (0, i)),
            pl.BlockSpec((None, 1, 128), lambda i: (i // (HALF_TOK // ROUTE_TB), 0, 0)),
        ],
        out_shape=[
            jax.ShapeDtypeStruct((N_TOK * 8, 128), F32),
            jax.ShapeDtypeStruct((8, N_TOK), F32),
            jax.ShapeDtypeStruct((2, 1, 128), F32),
        ],
        scratch_shapes=[pltpu.VMEM((1, 128), F32)],
        compiler_params=_cp("arbitrary"),
        name="moe_router",
    )(x_all, mod_l, n2g, w_hi, wr_lo, br)


EXP_TM = 128
N_ASSIGN = 2 * N_TOK
N_GROUPS = 2 * N_EXP
MAX_TILES = N_ASSIGN // EXP_TM + N_GROUPS
N_SORTED = MAX_TILES * EXP_TM
ORDER_BLK = 4096


DUMMY8 = HALF_TOK * 8


def _order_kernel(pos_ref, gate_ref, pad_lo_ref, pad_hi_ref, src_ref, gs_ref):
    i = pl.program_id(0)
    half = i // (N_ASSIGN // 2 // ORDER_BLK)
    local8 = (i * (ORDER_BLK // 2) - half * HALF_TOK) * 8

    def body(t, carry):
        for k in range(2):
            p = pos_ref[2 * t + k]
            src_ref[p] = local8 + t * 8
            gs_ref[p] = gate_ref[2 * t + k]
        return carry

    lax.fori_loop(0, ORDER_BLK // 2, body, 0, unroll=8)

    @pl.when(i == 0)
    def _():
        def group(g, carry):
            def pad(p, c):
                src_ref[p] = DUMMY8
                gs_ref[p] = 0.0
                return c
            return lax.fori_loop(pad_lo_ref[g], pad_hi_ref[g], pad, carry)

        lax.fori_loop(0, N_GROUPS, group, 0)


def _order(pos, gates, pad_lo, pad_hi):
    return pl.pallas_call(
        _order_kernel,
        grid=(N_ASSIGN // ORDER_BLK,),
        in_specs=[
            pl.BlockSpec((ORDER_BLK,), lambda i: (i,), memory_space=pltpu.SMEM),
            pl.BlockSpec((ORDER_BLK,), lambda i: (i,), memory_space=pltpu.SMEM),
            pl.BlockSpec(memory_space=pltpu.SMEM),
            pl.BlockSpec(memory_space=pltpu.SMEM),
        ],
        out_specs=[pl.BlockSpec(memory_space=pltpu.SMEM), pl.BlockSpec(memory_space=pltpu.SMEM)],
        out_shape=[jax.ShapeDtypeStruct((N_SORTED,), jnp.int32), jax.ShapeDtypeStruct((N_SORTED,), F32)],
        compiler_params=_cp("arbitrary"),
        name="moe_order",
    )(pos, gates, pad_lo, pad_hi)


ACC_TOK = HALF_TOK + 64
ROW_GROUP = 8


def _experts_kernel(tile0_ref, ntile_ref, src_ref, gs_ref, h_hbm, wg_ref, wu_ref, wd_ref, out_hbm,
                    h_res, acc, xbuf, ybuf, wgb, wub, wdb, sem):
    group = pl.program_id(0)
    expert = group % N_EXP
    rows0 = pl.multiple_of((group // N_EXP) * (HALF_TOK * 8), 8)

    @pl.when(expert == 0)
    def _():
        cp = pltpu.make_async_copy(h_hbm.at[pl.ds(rows0, HALF_TOK * 8), :], h_res, sem)
        cp.start()

        def zero(i, carry):
            acc[pl.ds(pl.multiple_of(i * 512, 512), 512), :] = jnp.zeros((512, 128), F32)
            return carry

        lax.fori_loop(0, ACC_TOK * 8 // 512, zero, 0)
        cp.wait()

    n_tiles = ntile_ref[group]

    @pl.when(n_tiles > 0)
    def _():
        wgb[...] = wg_ref[...].astype(BF16)
        wub[...] = wu_ref[...].astype(BF16)
        wdb[...] = wd_ref[...].astype(BF16)

    def tile_body(j, carry):
        base = (tile0_ref[group] + j) * EXP_TM

        def gather(g, c):
            for i in range(ROW_GROUP):
                r = g * ROW_GROUP + i
                row8 = jnp.minimum(src_ref[base + r], DUMMY8 - 8)
                xbuf[pl.ds(pl.multiple_of(r * 8, 8), 8), :] = _tile_of(h_res, row8)[...]
            return c

        lax.fori_loop(0, EXP_TM // ROW_GROUP, gather, 0)
        x = _tiles_to_rows(xbuf, EXP_TM).astype(BF16)
        hid = _silu(_dot(x, wgb[...])) * _dot(x, wub[...])
        _rows_to_tiles(ybuf, _dot(hid, wdb[...]))

        def accumulate(g, c):
            targets, values = [], []
            for i in range(ROW_GROUP):
                r = g * ROW_GROUP + i
                target = _tile_of(acc, src_ref[base + r])
                targets.append(target)
                values.append(target[...] + gs_ref[base + r] * ybuf[pl.ds(pl.multiple_of(r * 8, 8), 8), :])
            for target, value in zip(targets, values):
                target[...] = value
            return c

        lax.fori_loop(0, EXP_TM // ROW_GROUP, accumulate, 0)
        return carry

    lax.fori_loop(0, n_tiles, tile_body, 0)

    @pl.when(expert == N_EXP - 1)
    def _():
        cp = pltpu.make_async_copy(acc.at[pl.ds(0, HALF_TOK * 8), :], out_hbm.at[pl.ds(rows0, HALF_TOK * 8), :], sem)
        cp.start()
        cp.wait()


def _experts(tile0, n_tiles, src, gs, h, wg, wu, wd, layer):
    wmap = lambda g, t0, nt, src: (layer, g % N_EXP, 0, 0)
    return pl.pallas_call(
        _experts_kernel,
        grid_spec=pltpu.PrefetchScalarGridSpec(
            num_scalar_prefetch=3,
            grid=(N_GROUPS,),
            in_specs=[
                pl.BlockSpec(memory_space=pltpu.SMEM),
                pl.BlockSpec(memory_space=pl.ANY),
                pl.BlockSpec((None, None, D, D_EXP), wmap),
                pl.BlockSpec((None, None, D, D_EXP), wmap),
                pl.BlockSpec((None, None, D_EXP, D), wmap),
            ],
            out_specs=pl.BlockSpec(memory_space=pl.ANY),
            scratch_shapes=[
                pltpu.VMEM((HALF_TOK * 8, 128), F32),
                pltpu.VMEM((ACC_TOK * 8, 128), F32),
                pltpu.VMEM((EXP_TM * 8, 128), F32),
                pltpu.VMEM((EXP_TM * 8, 128), F32),
                pltpu.VMEM((D, D_EXP), BF16),
                pltpu.VMEM((D, D_EXP), BF16),
                pltpu.VMEM((D_EXP, D), BF16),
                pltpu.SemaphoreType.DMA,
            ],
        ),
        out_shape=jax.ShapeDtypeStruct((N_TOK * 8, 128), F32),
        compiler_params=_cp("arbitrary"),
        name="moe_experts",
    )(tile0, n_tiles, src, gs, h, wg, wu, wd)


COMBINE_TB = 256


def _combine_kernel(y_ref, x_ref, mod_ref, xo_ref):
    xo_ref[...] = x_ref[...] + mod_ref[5:6, :] * _tiles_to_rows(y_ref, COMBINE_TB)


def _combine(ys, x_all, mod_l):
    nb_ctx = N_CTX // COMBINE_TB
    per_seq = DEC_SEQ // COMBINE_TB
    cond = lambda i: jnp.where(i < nb_ctx, 0, 1 + (i - nb_ctx) // per_seq)
    return pl.pallas_call(
        _combine_kernel,
        grid=(N_TOK // COMBINE_TB,),
        in_specs=[
            pl.BlockSpec((COMBINE_TB * 8, 128), lambda i: (i, 0)),
            pl.BlockSpec((COMBINE_TB, D), lambda i: (i, 0)),
            pl.BlockSpec((None, 6, D), lambda i: (cond(i), 0, 0)),
        ],
        out_specs=pl.BlockSpec((COMBINE_TB, D), lambda i: (i, 0)),
        out_shape=jax.ShapeDtypeStruct((N_TOK, D), F32),
        compiler_params=_cp("arbitrary"),
        name="moe_combine",
    )(ys, x_all, mod_l)


def _moe(x_all, mod_l, n2g, wr, br, wg, wu, wd, layer):
    h, metat, cnt = _router(x_all, mod_l, n2g, wr, br)
    counts = cnt[:, 0, :N_EXP].astype(jnp.int32).reshape(N_GROUPS)
    padded = (counts + EXP_TM - 1) // EXP_TM * EXP_TM
    ends = jnp.cumsum(padded)
    offs = ends - padded
    rec = metat.astype(jnp.int32)
    half = (jnp.arange(N_TOK, dtype=jnp.int32) // HALF_TOK)[None, :]
    group = rec[M_E1:M_E2 + 1] + N_EXP * half
    is_group = group[None] == jnp.arange(N_GROUPS, dtype=jnp.int32)[:, None, None]
    pos = jnp.sum(jnp.where(is_group, offs[:, None, None], 0), axis=0) + rec[M_R1:M_R2 + 1]
    pos = pos.T.reshape(N_ASSIGN)
    gates = metat[M_G1:M_G2 + 1].T.reshape(N_ASSIGN)
    src, gs = _order(pos, gates, offs + counts, ends)
    ys = _experts(offs // EXP_TM, padded // EXP_TM, src, gs, h, wg, wu, wd, layer)
    return _combine(ys, x_all, mod_l)


def _final_kernel(x_ref, g_ref, o_ref):
    o_ref[...] = _rms(x_ref[...], g_ref[...])


def _final_norm(x_all, g, blk0, n_rows):
    tb = 512
    return pl.pallas_call(
        _final_kernel,
        grid=(n_rows // tb,),
        in_specs=[pl.BlockSpec((tb, D), lambda i: (blk0 + i, 0)), _resident((1, D), lambda i: (0, 0))],
        out_specs=pl.BlockSpec((tb, D), lambda i: (i, 0)),
        out_shape=jax.ShapeDtypeStruct((n_rows, D), F32),
        compiler_params=_cp("arbitrary"),
        name="final_norm",
    )(x_all, g)


def _rope_tables():
    pos = jnp.arange(DEC_SEQ)
    row = (pos // GRID_W).astype(F32)
    col = (pos % GRID_W).astype(F32)
    n_freq = HD // 4
    inv = ROPE_THETA ** (-jnp.arange(n_freq, dtype=F32) / n_freq)
    ang = jnp.concatenate([row[:, None] * inv, col[:, None] * inv], axis=-1)
    cos = jnp.repeat(jnp.cos(ang), 2, axis=-1)
    sin = jnp.repeat(jnp.sin(ang), 2, axis=-1) * jnp.tile(jnp.array([-1.0, 1.0], F32), HD // 2)
    nblk = DEC_SEQ // QKV_TB
    cos_tab = jnp.concatenate([jnp.ones((1, QKV_TB, HD), F32), cos.reshape(nblk, QKV_TB, HD)], axis=0)
    sin_tab = jnp.concatenate([jnp.zeros((1, QKV_TB, HD), F32), sin.reshape(nblk, QKV_TB, HD)], axis=0)
    return cos_tab, sin_tab


def kernel(x_prompt, x_sample, state_gla, cache_k, cache_v, c, c_ctx, w_mod, b_mod, norm1_g, norm2_g,
           w_in_even, w_gate_up, b_gate_up, gla_norm_g, w_spatial, b_spatial, w_out_even,
           w_in_odd, q_norm_g, k_norm_g, w_out_odd, w_router_group, b_router_group,
           w_router_expert, b_router_expert, w_exp_gate, w_exp_up, w_exp_down, final_norm_g):
    x_all = jnp.concatenate([x_prompt.reshape(N_CTX, D), x_sample.reshape(N_LAT, D)], axis=0)
    cond8 = jnp.concatenate([c_ctx[None], c, jnp.zeros((3, D), F32)], axis=0)
    mod = _modulation(cond8, w_mod, b_mod)
    cos_tab, sin_tab = _rope_tables()
    zero_state = jnp.zeros((1, 2, QK_W, GLA_DV), F32)
    state_in = state_gla.reshape(DEC_BATCH, -1, 2, QK_W, GLA_DV)
    cache_k2 = cache_k.reshape(DEC_BATCH, -1, SEQ, KV_W)
    cache_v2 = cache_v.reshape(DEC_BATCH, -1, SEQ, KV_W)

    gla_states, ctx_k, ctx_v = [], [], []
    for l in range(DEPTH):
        i = l // 2
        n1g = norm1_g[l][None]
        if l % 2 == 0:
            w = w_in_even[i]
            win = jnp.concatenate([w[:, :1536], w[:, 1568:], w[:, 1536:1568], jnp.zeros((D, 96), F32)],
                                  axis=1).astype(BF16)
            wgu = jnp.zeros((128, 2 * QK_W), F32)
            wgu = wgu.at[0:GLA_RANK, 0:QK_W].set(w_gate_up[i, 0])
            wgu = wgu.at[GLA_RANK:2 * GLA_RANK, QK_W:].set(w_gate_up[i, 1]).astype(BF16)
            bgu = b_gate_up[i].reshape(1, 2 * QK_W)
            args = (mod[l], n1g, win, wgu, bgu, gla_norm_g[i][None], w_spatial[i].astype(BF16),
                    b_spatial[i].T, w_out_even[i].astype(BF16))
            x_all, st = _even_mixer(x_all, *args, zero_state, latent=False)
            gla_states.append(st)
            x_all, _ = _even_mixer(x_all, *args, state_in[:, i], latent=True)
        else:
            q, k, v = _qkv(x_all, mod[l], n1g, w_in_odd[i].astype(BF16), q_norm_g[i][None],
                           k_norm_g[i][None], cos_tab, sin_tab)
            ctx_k.append(k[:N_CTX].reshape(BATCH, SEQ, ATT_KV, HD))
            ctx_v.append(v[:N_CTX].reshape(BATCH, SEQ, ATT_KV, HD))
            wout = w_out_odd[i].astype(BF16)
            x_all = _attention(x_all, mod[l], q, k, v, wout)
            x_all = _attention(x_all, mod[l], q, k, v, wout, cache_k2, cache_v2, layer_i=i)
        wr = jnp.concatenate([w_router_expert[l], w_router_group[l],
                              jnp.zeros((D, 128 - N_EXP - MOE_GROUPS), F32)], axis=1)
        br = jnp.concatenate([b_router_expert[l], b_router_group[l],
                              jnp.zeros((128 - N_EXP - MOE_GROUPS,), F32)])[None]
        x_all = _moe(x_all, mod[l], norm2_g[l][None], wr, br, w_exp_gate, w_exp_up, w_exp_down, l)

    fg = final_norm_g[None]
    y_prompt = _final_norm(x_all, fg, 0, N_CTX).reshape(BATCH, SEQ, D)
    y_sample = _final_norm(x_all, fg, N_CTX // 512, N_LAT).reshape(DEC_BATCH, DEC_SEQ, D)
    new_state = jnp.stack(gla_states, axis=1).reshape(BATCH, -1, 2, GLA_HEADS, GLA_DK, GLA_DV)
    return (y_prompt, y_sample, new_state, jnp.stack(ctx_k, axis=1), jnp.stack(ctx_v, axis=1))
```

```python
import functools

import jax
import jax.numpy as jnp
import numpy as np
from jax import lax
from jax.experimental import pallas as pl
from jax.experimental.pallas import tpu as pltpu

F32 = jnp.float32
BF16 = jnp.bfloat16

D = 1024
BATCH, SEQ = 16, 256
DEC_BATCH, DEC_SEQ = 4, 1024
N_CTX = BATCH * SEQ
N_LAT = DEC_BATCH * DEC_SEQ
N_TOK = N_CTX + N_LAT
DEPTH = 4
EPS = 1e-6
GRID_W = 64
ROPE_THETA = 10000.0

GLA_HEADS, GLA_DK, GLA_DV, GLA_RANK, GLA_CHUNK, GLA_TAU = 4, 64, 128, 16, 64, 16.0
QK_W = GLA_HEADS * GLA_DK
V_W = GLA_HEADS * GLA_DV
GMLP_GROUPS, GMLP_DIM, GMLP_CHUNK = 4, 128, 128
GMLP_W = GMLP_GROUPS * GMLP_DIM
C_Q, C_K, C_V, C_G, C_U, C_VG, C_A = 0, 256, 512, 1024, 1536, 2048, 2560
EVEN_PACK = 2688

ATT_HEADS, ATT_KV, HD = 8, 2, 128
ATT_G = ATT_HEADS // ATT_KV
Q_W = ATT_HEADS * HD
KV_W = ATT_KV * HD

MOE_GROUPS, MOE_PER_GROUP = 4, 8
N_EXP = MOE_GROUPS * MOE_PER_GROUP
D_EXP = D // 4
NEG = -1e30

VMEM_LIMIT = 56 * 1024 * 1024


def _cp(*sem):
    return pltpu.CompilerParams(dimension_semantics=sem, vmem_limit_bytes=VMEM_LIMIT)


def _dot(a, b):
    return jnp.dot(a.astype(BF16), b.astype(BF16), preferred_element_type=F32)


def _dot_nt(a, b):
    return lax.dot_general(a.astype(BF16), b.astype(BF16), (((1,), (1,)), ((), ())),
                           preferred_element_type=F32)


def _dot_tn(a, b):
    return lax.dot_general(a.astype(BF16), b.astype(BF16), (((0,), (0,)), ((), ())),
                           preferred_element_type=F32)


def _rms(x, g):
    return x * lax.rsqrt(jnp.mean(x * x, axis=-1, keepdims=True) + EPS) * g


def _silu(x):
    return x * jax.nn.sigmoid(x)


def _gelu(x):
    return 0.5 * x * (1.0 + jnp.tanh(np.sqrt(2.0 / np.pi).astype(np.float32) * (x + 0.044715 * (x * x * x))))


def _log_sigmoid(z):
    return jnp.minimum(z, 0.0) - jnp.log(1.0 + jnp.exp(-jnp.abs(z)))


def _rows_to_tiles(ref, x):
    rows = x.shape[0]
    for j in range(D // 128):
        ref[pl.ds(j, rows, stride=8), :] = x[:, j * 128:(j + 1) * 128]


def _tiles_to_rows(ref, rows):
    return jnp.concatenate([ref[pl.ds(j, rows, stride=8), :] for j in range(D // 128)], axis=1)


def _tile_of(ref, row8):
    return ref.at[pl.ds(pl.multiple_of(row8, 8), 8), :]


def _resident(shape, index_map):
    return pl.BlockSpec(shape, index_map, pipeline_mode=pl.Buffered(1))


def _mod_kernel(cond_ref, w_ref, b_ref, o_ref):
    c = cond_ref[...]
    o_ref[...] = jnp.dot(_silu(c), w_ref[...], precision=lax.Precision.HIGHEST,
                         preferred_element_type=F32) + b_ref[...]


def _modulation(cond8, w_mod, b_mod):
    tn = 1024
    out = pl.pallas_call(
        _mod_kernel,
        grid=(DEPTH, 6 * D // tn),
        in_specs=[
            pl.BlockSpec((8, D), lambda l, j: (0, 0)),
            pl.BlockSpec((None, D, tn), lambda l, j: (l, 0, j)),
            pl.BlockSpec((None, 1, tn), lambda l, j: (l, 0, j)),
        ],
        out_specs=pl.BlockSpec((None, 8, tn), lambda l, j: (l, 0, j)),
        out_shape=jax.ShapeDtypeStruct((DEPTH, 8, 6 * D), F32),
        compiler_params=_cp("arbitrary", "arbitrary"),
        name="adaln_mod",
    )(cond8, w_mod, b_mod.reshape(DEPTH, 1, 6 * D))
    return out.reshape(DEPTH, 8, 6, D)


def _even_kernel(x_ref, mod_ref, n1g_ref, win_ref, wgu_ref, bgu_ref, glag_ref, ws_ref, bs_ref,
                 wout_ref, s0_ref, xo_ref, st_ref, proj, la, o_f, o_b, st_scr, *, T):
    n_chunks = T // GLA_CHUNK
    shift, scale, gate = mod_ref[0:1, :], mod_ref[1:2, :], mod_ref[2:3, :]
    RB = 128

    def proj_body(r, carry):
        r0 = pl.multiple_of(r * RB, RB)
        h = _rms(x_ref[pl.ds(r0, RB), :], n1g_ref[...]) * (1.0 + scale) + shift
        p = _dot(h, win_ref[...])
        proj[pl.ds(r0, RB), :] = p
        z = _dot(p[:, C_A:C_A + 128], wgu_ref[...]) + bgu_ref[...]
        la[pl.ds(r0, RB), :] = _log_sigmoid(z) * (1.0 / GLA_TAU)
        return carry

    lax.fori_loop(0, T // RB, proj_body, 0)

    st_scr[0] = s0_ref[0].T
    st_scr[1] = s0_ref[1].T

    ci = lax.broadcasted_iota(jnp.int32, (GLA_CHUNK, GLA_CHUNK), 0)
    cj = lax.broadcasted_iota(jnp.int32, (GLA_CHUNK, GLA_CHUNK), 1)
    tri = (jnp.where(ci >= cj, 1.0, 0.0).astype(BF16), jnp.where(ci <= cj, 1.0, 0.0).astype(BF16))
    ai = lax.broadcasted_iota(jnp.int32, (GLA_HEADS * GLA_CHUNK, GLA_CHUNK), 0) % GLA_CHUNK
    aj = lax.broadcasted_iota(jnp.int32, (GLA_HEADS * GLA_CHUNK, GLA_CHUNK), 1)
    amask = (ai >= aj, ai <= aj)
    lane_head = lax.broadcasted_iota(jnp.int32, (1, QK_W), 1) // GLA_DK
    hmask = [jnp.where(lane_head == h, 1.0, 0.0) for h in range(GLA_HEADS)]

    def chunk_body(i, carry):
        for d in range(2):
            c = i if d == 0 else n_chunks - 1 - i
            r0 = pl.multiple_of(c * GLA_CHUNK, GLA_CHUNK)
            q = proj[pl.ds(r0, GLA_CHUNK), C_Q:C_Q + QK_W] * (GLA_DK ** -0.5)
            k = proj[pl.ds(r0, GLA_CHUNK), C_K:C_K + QK_W]
            v = proj[pl.ds(r0, GLA_CHUNK), C_V:C_V + V_W]
            lac = la[pl.ds(r0, GLA_CHUNK), d * QK_W:(d + 1) * QK_W]
            hi = lac.astype(BF16)
            lo = (lac - hi.astype(F32)).astype(BF16)
            b = (jnp.dot(tri[d], hi, preferred_element_type=F32)
                 + jnp.dot(tri[d], lo, preferred_element_type=F32))
            bend = b[GLA_CHUNK - 1:GLA_CHUNK, :] if d == 0 else b[0:1, :]
            qe = q * jnp.exp(b)
            ke = k * jnp.exp(-b)
            kd = k * jnp.exp(bend - b)
            st = st_scr[d]
            qstack = jnp.concatenate([qe * hmask[h] for h in range(GLA_HEADS)], axis=0).astype(BF16)
            att = jnp.where(amask[d], _dot_nt(qstack, ke), 0.0)
            inter = _dot_nt(qstack, st)
            outs = []
            for h in range(GLA_HEADS):
                rows = slice(h * GLA_CHUNK, (h + 1) * GLA_CHUNK)
                outs.append(_dot(att[rows], v[:, h * GLA_DV:(h + 1) * GLA_DV]) + inter[rows])
            o = jnp.concatenate(outs, axis=1)
            if d == 0:
                o_f[pl.ds(r0, GLA_CHUNK), :] = o
            else:
                o_b[pl.ds(r0, GLA_CHUNK), :] = o
            vstack = jnp.concatenate([v[:, h * GLA_DV:(h + 1) * GLA_DV] for h in range(GLA_HEADS)], axis=0)
            kstack = jnp.concatenate([kd * hmask[h] for h in range(GLA_HEADS)], axis=0)
            st_scr[d] = st * jnp.exp(bend) + _dot_tn(vstack, kstack)
        return carry

    lax.fori_loop(0, n_chunks, chunk_body, 0, unroll=4)
    st_ref[0] = st_scr[0].T
    st_ref[1] = st_scr[1].T

    def out_body(r, carry):
        r0 = pl.multiple_of(r * RB, RB)
        osum = o_f[pl.ds(r0, RB), :] + o_b[pl.ds(r0, RB), :]
        g = proj[pl.ds(r0, RB), C_G:C_G + V_W]
        u = proj[pl.ds(r0, RB), C_U:C_U + GMLP_W]
        vg = _gelu(proj[pl.ds(r0, RB), C_VG:C_VG + GMLP_W])
        parts = []
        for h in range(GLA_HEADS):
            oh = osum[:, h * GLA_DV:(h + 1) * GLA_DV]
            parts.append(_rms(oh, glag_ref[...]) * _silu(g[:, h * GLA_DV:(h + 1) * GLA_DV]))
        for gi in range(GMLP_GROUPS):
            vc = vg[:, gi * GMLP_DIM:(gi + 1) * GMLP_DIM]
            vc = vc - jnp.mean(vc, axis=-1, keepdims=True)
            vn = vc * lax.rsqrt(jnp.mean(vc * vc, axis=-1, keepdims=True) + EPS)
            sg = _dot(ws_ref[gi], vn) + bs_ref[:, gi:gi + 1]
            parts.append(_gelu(u[:, gi * GMLP_DIM:(gi + 1) * GMLP_DIM]) * sg)
        mix = jnp.concatenate(parts, axis=1)
        y = _dot(mix, wout_ref[...])
        xo_ref[pl.ds(r0, RB), :] = x_ref[pl.ds(r0, RB), :] + gate * y
        return carry

    lax.fori_loop(0, T // RB, out_body, 0)


def _even_mixer(x_all, mod_l, n1g, win, wgu, bgu, glag, ws, bs, wout, s0, *, latent, x_first=None):
    if latent:
        T, nseq, blk0 = DEC_SEQ, DEC_BATCH, N_CTX // DEC_SEQ
        cond = lambda i: 1 + i
        s0_spec = pl.BlockSpec((None, 2, QK_W, GLA_DV), lambda i: (i, 0, 0, 0))
    else:
        T, nseq, blk0 = SEQ, BATCH, 0
        cond = lambda i: 0
        s0_spec = pl.BlockSpec((None, 2, QK_W, GLA_DV), lambda i: (0, 0, 0, 0))
    const2 = lambda i: (0, 0)
    body = functools.partial(_even_kernel, T=T)
    x_spec = pl.BlockSpec((T, D), lambda i: (blk0 + i, 0))
    if x_first is None:
        lead_specs, lead_args, aliases = [x_spec], (x_all,), {0: 0}
    elif x_all is None:
        lead_specs, lead_args, aliases = [pl.BlockSpec((T, D), lambda i: (i, 0))], (x_first,), {}
    else:
        lead_specs = [pl.BlockSpec(memory_space=pl.ANY), pl.BlockSpec((T, D), lambda i: (i, 0))]
        lead_args, aliases = (x_all, x_first), {0: 0}
        body = lambda dst_ref, *refs: _even_kernel(*refs, T=T)
    x_new, states = pl.pallas_call(
        body,
        grid=(nseq,),
        in_specs=lead_specs + [
            pl.BlockSpec((None, 6, D), lambda i: (cond(i), 0, 0)),
            _resident((1, D), const2),
            _resident((D, EVEN_PACK), const2),
            _resident((128, 2 * QK_W), const2),
            _resident((1, 2 * QK_W), const2),
            _resident((1, GLA_DV), const2),
            _resident((GMLP_GROUPS, GMLP_CHUNK, GMLP_CHUNK), lambda i: (0, 0, 0)),
            _resident((GMLP_CHUNK, GMLP_GROUPS), const2),
            _resident((D, D), const2),
            s0_spec,
        ],
        out_specs=[
            x_spec,
            pl.BlockSpec((None, 2, QK_W, GLA_DV), lambda i: (i, 0, 0, 0)),
        ],
        out_shape=[
            jax.ShapeDtypeStruct((N_TOK, D), F32),
            jax.ShapeDtypeStruct((nseq, 2, QK_W, GLA_DV), F32),
        ],
        scratch_shapes=[
            pltpu.VMEM((T, EVEN_PACK), F32),
            pltpu.VMEM((T, 2 * QK_W), F32),
            pltpu.VMEM((T, V_W), F32),
            pltpu.VMEM((T, V_W), F32),
            pltpu.VMEM((2, GLA_DV, QK_W), F32),
        ],
        input_output_aliases=aliases,
        compiler_params=_cp("arbitrary"),
        name="even_mixer_latent" if latent else "even_mixer_context",
    )(*lead_args, mod_l, n1g, win, wgu, bgu, glag, ws, bs, wout, s0)
    return x_new, states


QKV_TB = 512


def _qkv_kernel(x_ref, mod_ref, n1g_ref, win_ref, gq_ref, gk_ref, cos_ref, sin_ref, q_ref, k_ref, v_ref):
    shift, scale = mod_ref[0:1, :], mod_ref[1:2, :]
    h = _rms(x_ref[...], n1g_ref[...]) * (1.0 + scale) + shift
    p = _dot(h, win_ref[...])
    cos, sin = cos_ref[...], sin_ref[...]
    even_lane = lax.broadcasted_iota(jnp.int32, (1, HD), 1) % 2 == 0

    def norm_rope(xh, g):
        xn = _rms(xh, g)
        swapped = jnp.where(even_lane, pltpu.roll(xn, HD - 1, axis=1), pltpu.roll(xn, 1, axis=1))
        return xn * cos + swapped * sin

    for hh in range(ATT_HEADS):
        q_ref[:, hh * HD:(hh + 1) * HD] = norm_rope(p[:, hh * HD:(hh + 1) * HD], gq_ref[...]).astype(BF16)
    for hh in range(ATT_KV):
        k_ref[:, hh * HD:(hh + 1) * HD] = norm_rope(p[:, Q_W + hh * HD:Q_W + (hh + 1) * HD], gk_ref[...])
    v_ref[...] = p[:, Q_W + KV_W:]


def _qkv(x_all, mod_l, n1g, win, gq, gk, cos_tab, sin_tab):
    nb_ctx = N_CTX // QKV_TB
    per_seq = DEC_SEQ // QKV_TB
    cond = lambda i: jnp.where(i < nb_ctx, 0, 1 + (i - nb_ctx) // per_seq)
    tab = lambda i: jnp.where(i < nb_ctx, 0, 1 + (i - nb_ctx) % per_seq)
    const2 = lambda i: (0, 0)
    return pl.pallas_call(
        _qkv_kernel,
        grid=(N_TOK // QKV_TB,),
        in_specs=[
            pl.BlockSpec((QKV_TB, D), lambda i: (i, 0)),
            pl.BlockSpec((None, 6, D), lambda i: (cond(i), 0, 0)),
            _resident((1, D), const2),
            _resident((D, Q_W + 2 * KV_W), const2),
            _resident((1, HD), const2),
            _resident((1, HD), const2),
            pl.BlockSpec((None, QKV_TB, HD), lambda i: (tab(i), 0, 0)),
            pl.BlockSpec((None, QKV_TB, HD), lambda i: (tab(i), 0, 0)),
        ],
        out_specs=[
            pl.BlockSpec((QKV_TB, Q_W), lambda i: (i, 0)),
            pl.BlockSpec((QKV_TB, KV_W), lambda i: (i, 0)),
            pl.BlockSpec((QKV_TB, KV_W), lambda i: (i, 0)),
        ],
        out_shape=[
            jax.ShapeDtypeStruct((N_TOK, Q_W), BF16),
            jax.ShapeDtypeStruct((N_TOK, KV_W), F32),
            jax.ShapeDtypeStruct((N_TOK, KV_W), F32),
        ],
        compiler_params=_cp("arbitrary"),
        name="odd_qkv",
    )(x_all, mod_l, n1g, win, gq, gk, cos_tab, sin_tab)


ATT_TQ = 256


def _attn_kernel(*refs, n_kv):
    q_ref = refs[0]
    kv_refs = refs[1:1 + 2 * n_kv]
    x_ref, mod_ref, wout_ref, xo_ref, att_scr = refs[1 + 2 * n_kv:]
    gate = mod_ref[2:3, :]
    for kh in range(ATT_KV):
        ks = [kv_refs[2 * s][:, kh * HD:(kh + 1) * HD].astype(BF16) for s in range(n_kv)]
        vs = [kv_refs[2 * s + 1][:, kh * HD:(kh + 1) * HD].astype(BF16) for s in range(n_kv)]
        for g in range(ATT_G):
            hh = kh * ATT_G + g
            qh = q_ref[:, hh * HD:(hh + 1) * HD]
            ss = [_dot_nt(qh, kk) * (HD ** -0.5) for kk in ks]
            m = ss[0].max(axis=-1, keepdims=True)
            for s in ss[1:]:
                m = jnp.maximum(m, s.max(axis=-1, keepdims=True))
            ps = [jnp.exp(s - m) for s in ss]
            den = ps[0].sum(axis=-1, keepdims=True)
            for p in ps[1:]:
                den = den + p.sum(axis=-1, keepdims=True)
            o = _dot(ps[0], vs[0])
            for p, vv in zip(ps[1:], vs[1:]):
                o = o + _dot(p, vv)
            att_scr[:, hh * HD:(hh + 1) * HD] = o / den
    y = _dot(att_scr[...], wout_ref[...])
    xo_ref[...] = x_ref[...] + gate * y


def _attention(x_all, mod_l, q, k, v, wout, cache_k=None, cache_v=None, layer_i=0):
    latent = cache_k is not None
    const2 = lambda *a: (0, 0)
    if latent:
        nq = DEC_SEQ // ATT_TQ
        row_blk = lambda b, j: (N_CTX // ATT_TQ + b * nq + j, 0)
        grid = (DEC_BATCH, nq)
        kv_specs = [
            pl.BlockSpec((None, None, SEQ, KV_W), lambda b, j: (b, layer_i, 0, 0)),
            pl.BlockSpec((None, None, SEQ, KV_W), lambda b, j: (b, layer_i, 0, 0)),
            pl.BlockSpec((DEC_SEQ, KV_W), lambda b, j: (N_CTX // DEC_SEQ + b, 0)),
            pl.BlockSpec((DEC_SEQ, KV_W), lambda b, j: (N_CTX // DEC_SEQ + b, 0)),
        ]
        kv_args = (cache_k, cache_v, k, v)
        mod_spec = pl.BlockSpec((None, 6, D), lambda b, j: (1 + b, 0, 0))
        sem = ("arbitrary", "arbitrary")
        n_kv = 2
    else:
        row_blk = lambda i: (i, 0)
        grid = (BATCH,)
        kv_specs = [pl.BlockSpec((SEQ, KV_W), row_blk), pl.BlockSpec((SEQ, KV_W), row_blk)]
        kv_args = (k, v)
        mod_spec = pl.BlockSpec((None, 6, D), lambda i: (0, 0, 0))
        sem = ("arbitrary",)
        n_kv = 1
    n_in = 1 + len(kv_args)
    return pl.pallas_call(
        functools.partial(_attn_kernel, n_kv=n_kv),
        grid=grid,
        in_specs=[pl.BlockSpec((ATT_TQ, Q_W), row_blk)] + kv_specs + [
            pl.BlockSpec((ATT_TQ, D), row_blk),
            mod_spec,
            _resident((D, D), const2),
        ],
        out_specs=pl.BlockSpec((ATT_TQ, D), row_blk),
        out_shape=jax.ShapeDtypeStruct((N_TOK, D), F32),
        scratch_shapes=[pltpu.VMEM((ATT_TQ, Q_W), F32)],
        input_output_aliases={n_in: 0},
        compiler_params=_cp(*sem),
        name="attention_latent" if latent else "attention_context",
    )(q, *kv_args, x_all, mod_l, wout)


ROUTE_TB = 512
HALF_TOK = N_TOK // 2
M_E1, M_E2, M_G1, M_G2, M_R1, M_R2 = 0, 1, 2, 3, 4, 5


def _router_kernel(x_ref, mod_ref, n2g_ref, whi_ref, wlo_ref, br_ref, h_ref, metat_ref, cnt_ref, run):
    @pl.when(pl.program_id(0) % (HALF_TOK // ROUTE_TB) == 0)
    def _():
        run[...] = jnp.zeros_like(run)

    shift, scale = mod_ref[3:4, :], mod_ref[4:5, :]
    h = _rms(x_ref[...], n2g_ref[...]) * (1.0 + scale) + shift
    _rows_to_tiles(h_ref, h)
    h_hi = h.astype(BF16)
    h_lo = (h - h_hi.astype(F32)).astype(BF16)
    dot = functools.partial(jnp.dot, preferred_element_type=F32)
    logits = dot(h_hi, whi_ref[...]) + dot(h_lo, whi_ref[...]) + dot(h_hi, wlo_ref[...]) + br_ref[...]
    lane = lax.broadcasted_iota(jnp.int32, logits.shape, 1).astype(F32)
    big = 1e4

    def first_argmax(vals):
        m = vals.max(axis=-1, keepdims=True)
        return m, jnp.where(vals == m, lane, big).min(axis=-1, keepdims=True)

    gl = jnp.where((lane >= N_EXP) & (lane < N_EXP + MOE_GROUPS), logits, NEG)
    gmax, glane = first_argmax(gl)
    g_p = 1.0 / jnp.exp(gl - gmax).sum(axis=-1, keepdims=True)
    lo = (glane - N_EXP) * MOE_PER_GROUP
    el = jnp.where((lane >= lo) & (lane < lo + MOE_PER_GROUP), logits, NEG)
    m1, i1 = first_argmax(el)
    m2, i2 = first_argmax(jnp.where(lane == i1, NEG, el))
    t = jnp.exp(m2 - m1)
    w1 = 1.0 / (1.0 + t)
    sel1, sel2 = lane == i1, lane == i2
    onehot = jnp.where(sel1 | sel2, 1.0, 0.0)
    ri = lax.broadcasted_iota(jnp.int32, (ROUTE_TB, ROUTE_TB), 0)
    rj = lax.broadcasted_iota(jnp.int32, (ROUTE_TB, ROUTE_TB), 1)
    before = _dot(jnp.where(ri > rj, 1.0, 0.0), onehot) + run[...]
    r1 = jnp.where(sel1, before, 0.0).sum(axis=-1, keepdims=True)
    r2 = jnp.where(sel2, before, 0.0).sum(axis=-1, keepdims=True)
    run[...] += onehot.sum(axis=0, keepdims=True)
    cnt_ref[...] = run[...]
    meta = jnp.zeros_like(logits)
    for j, val in enumerate([i1, i2, w1 * g_p, (t * w1) * g_p, r1, r2]):
        meta = jnp.where(lane == j, val, meta)
    metat_ref[...] = meta.T[0:8, :]


def _router(x_all, mod_l, n2g, wr, br):
    w_hi = wr.astype(BF16)
    wr_lo = (wr - w_hi.astype(F32)).astype(BF16)
    nb_ctx = N_CTX // ROUTE_TB
    per_seq = DEC_SEQ // ROUTE_TB
    cond = lambda i: jnp.where(i < nb_ctx, 0, 1 + (i - nb_ctx) // per_seq)
    const2 = lambda i: (0, 0)
    return pl.pallas_call(
        _router_kernel,
        grid=(N_TOK // ROUTE_TB,),
        in_specs=[
            pl.BlockSpec((ROUTE_TB, D), lambda i: (i, 0)),
            pl.BlockSpec((None, 6, D), lambda i: (cond(i), 0, 0)),
            _resident((1, D), const2),
            _resident((D, 128), const2),
            _resident((D, 128), const2),
            _resident((1, 128), const2),
        ],
        out_specs=[
            pl.BlockSpec((ROUTE_TB * 8, 128), lambda i: (i, 0)),
            pl.BlockSpec((8, ROUTE_TB), lambda i: (0, i)),
            pl.BlockSpec((None, 1, 128), lambda i: (i // (HALF_TOK // ROUTE_TB), 0, 0)),
        ],
        out_shape=[
            jax.ShapeDtypeStruct((N_TOK * 8, 128), F32),
            jax.ShapeDtypeStruct((8, N_TOK), F32),
            jax.ShapeDtypeStruct((2, 1, 128), F32),
        ],
        scratch_shapes=[pltpu.VMEM((1, 128), F32)],
        compiler_params=_cp("arbitrary"),
        name="moe_router",
    )(x_all, mod_l, n2g, w_hi, wr_lo, br)


EXP_TM = 128
N_ASSIGN = 2 * N_TOK
N_GROUPS = 2 * N_EXP
MAX_TILES = N_ASSIGN // EXP_TM + N_GROUPS
N_SORTED = MAX_TILES * EXP_TM
ORDER_BLK = 2048
CODE_PLANE = 2 * HALF_TOK
CODE_MASK = 8 * CODE_PLANE - 1
DUMMY8 = HALF_TOK * 8


def _order_kernel(pos1_ref, pos2_ref, pad_lo_ref, pad_hi_ref, src_ref):
    i = pl.program_id(0)
    local = (i % (HALF_TOK // ORDER_BLK)) * ORDER_BLK

    def body(t, carry):
        src_ref[pos1_ref[t]] = (local + t) * 8
        src_ref[pos2_ref[t]] = (local + t + CODE_PLANE) * 8
        return carry

    lax.fori_loop(0, ORDER_BLK, body, 0, unroll=16)

    @pl.when(i == 0)
    def _():
        def group(g, carry):
            def pad(p, c):
                src_ref[p] = DUMMY8
                return c
            return lax.fori_loop(pad_lo_ref[g], pad_hi_ref[g], pad, carry)

        lax.fori_loop(0, N_GROUPS, group, 0)


def _order(pos, pad_lo, pad_hi):
    return pl.pallas_call(
        _order_kernel,
        grid=(N_TOK // ORDER_BLK,),
        in_specs=[
            pl.BlockSpec((ORDER_BLK,), lambda i: (i,), memory_space=pltpu.SMEM),
            pl.BlockSpec((ORDER_BLK,), lambda i: (N_TOK // ORDER_BLK + i,), memory_space=pltpu.SMEM),
            pl.BlockSpec(memory_space=pltpu.SMEM),
            pl.BlockSpec(memory_space=pltpu.SMEM),
        ],
        out_specs=pl.BlockSpec(memory_space=pltpu.SMEM),
        out_shape=jax.ShapeDtypeStruct((N_SORTED,), jnp.int32),
        compiler_params=_cp("arbitrary"),
        name="moe_order",
    )(pos, pos, pad_lo, pad_hi)


GATE_BLK = CODE_PLANE + HALF_TOK
ACC_TOK = HALF_TOK + 64
ROW_GROUP = 8


def _experts_kernel(tile0_ref, ntile_ref, src_ref, gs_ref, h_hbm, wg_ref, wu_ref, wd_ref, out_hbm,
                    h_res, acc, xbuf, ybuf, wgb, wub, wdb, sem):
    group = pl.program_id(0)
    expert = group % N_EXP
    rows0 = pl.multiple_of((group // N_EXP) * (HALF_TOK * 8), 8)

    @pl.when(expert == 0)
    def _():
        cp = pltpu.make_async_copy(h_hbm.at[pl.ds(rows0, HALF_TOK * 8), :], h_res, sem)
        cp.start()

        def zero(i, carry):
            acc[pl.ds(pl.multiple_of(i * 512, 512), 512), :] = jnp.zeros((512, 128), F32)
            return carry

        lax.fori_loop(0, ACC_TOK * 8 // 512, zero, 0)
        cp.wait()

    n_tiles = ntile_ref[group]

    @pl.when(n_tiles > 0)
    def _():
        wgb[...] = wg_ref[...].astype(BF16)
        wub[...] = wu_ref[...].astype(BF16)
        wdb[...] = wd_ref[...].astype(BF16)

    def tile_body(j, carry):
        base = (tile0_ref[group] + j) * EXP_TM

        def gather(g, c):
            for i in range(ROW_GROUP):
                r = g * ROW_GROUP + i
                row8 = jnp.minimum(src_ref[base + r] & CODE_MASK, DUMMY8 - 8)
                xbuf[pl.ds(pl.multiple_of(r * 8, 8), 8), :] = _tile_of(h_res, row8)[...]
            return c

        lax.fori_loop(0, EXP_TM // ROW_GROUP, gather, 0)
        x = _tiles_to_rows(xbuf, EXP_TM).astype(BF16)
        hid = _silu(_dot(x, wgb[...])) * _dot(x, wub[...])
        _rows_to_tiles(ybuf, _dot(hid, wdb[...]))

        def accumulate(g, c):
            targets, values = [], []
            for i in range(ROW_GROUP):
                r = g * ROW_GROUP + i
                code = src_ref[base + r]
                target = _tile_of(acc, code & CODE_MASK)
                targets.append(target)
                values.append(target[...] + gs_ref[code >> 3] * ybuf[pl.ds(pl.multiple_of(r * 8, 8), 8), :])
            for target, value in zip(targets, values):
                target[...] = value
            return c

        lax.fori_loop(0, EXP_TM // ROW_GROUP, accumulate, 0)
        return carry

    lax.fori_loop(0, n_tiles, tile_body, 0)

    @pl.when(expert == N_EXP - 1)
    def _():
        cp = pltpu.make_async_copy(acc.at[pl.ds(0, HALF_TOK * 8), :], out_hbm.at[pl.ds(rows0, HALF_TOK * 8), :], sem)
        cp.start()
        cp.wait()


def _experts(tile0, n_tiles, src, gs, h, wg, wu, wd, layer):
    wmap = lambda g, t0, nt, src: (layer, g % N_EXP, 0, 0)
    return pl.pallas_call(
        _experts_kernel,
        grid_spec=pltpu.PrefetchScalarGridSpec(
            num_scalar_prefetch=3,
            grid=(N_GROUPS,),
            in_specs=[
                pl.BlockSpec((GATE_BLK,), lambda g, t0, nt, src: (g // N_EXP,), memory_space=pltpu.SMEM),
                pl.BlockSpec(memory_space=pl.ANY),
                pl.BlockSpec((None, None, D, D_EXP), wmap),
                pl.BlockSpec((None, None, D, D_EXP), wmap),
                pl.BlockSpec((None, None, D_EXP, D), wmap),
            ],
            out_specs=pl.BlockSpec(memory_space=pl.ANY),
            scratch_shapes=[
                pltpu.VMEM((HALF_TOK * 8, 128), F32),
                pltpu.VMEM((ACC_TOK * 8, 128), F32),
                pltpu.VMEM((EXP_TM * 8, 128), F32),
                pltpu.VMEM((EXP_TM * 8, 128), F32),
                pltpu.VMEM((D, D_EXP), BF16),
                pltpu.VMEM((D, D_EXP), BF16),
                pltpu.VMEM((D_EXP, D), BF16),
                pltpu.SemaphoreType.DMA,
            ],
        ),
        out_shape=jax.ShapeDtypeStruct((N_TOK * 8, 128), F32),
        compiler_params=_cp("arbitrary"),
        name="moe_experts",
    )(tile0, n_tiles, src, gs, h, wg, wu, wd)


COMBINE_TB = 256


def _combine_kernel(y_ref, x_ref, mod_ref, xo_ref):
    xo_ref[...] = x_ref[...] + mod_ref[5:6, :] * _tiles_to_rows(y_ref, COMBINE_TB)


def _combine(ys, x_all, mod_l):
    nb_ctx = N_CTX // COMBINE_TB
    per_seq = DEC_SEQ // COMBINE_TB
    cond = lambda i: jnp.where(i < nb_ctx, 0, 1 + (i - nb_ctx) // per_seq)
    return pl.pallas_call(
        _combine_kernel,
        grid=(N_TOK // COMBINE_TB,),
        in_specs=[
            pl.BlockSpec((COMBINE_TB * 8, 128), lambda i: (i, 0)),
            pl.BlockSpec((COMBINE_TB, D), lambda i: (i, 0)),
            pl.BlockSpec((None, 6, D), lambda i: (cond(i), 0, 0)),
        ],
        out_specs=pl.BlockSpec((COMBINE_TB, D), lambda i: (i, 0)),
        out_shape=jax.ShapeDtypeStruct((N_TOK, D), F32),
        compiler_params=_cp("arbitrary"),
        name="moe_combine",
    )(ys, x_all, mod_l)


def _moe(x_all, mod_l, n2g, wr, br, wg, wu, wd, layer):
    h, metat, cnt = _router(x_all, mod_l, n2g, wr, br)
    counts = cnt[:, 0, :N_EXP].astype(jnp.int32).reshape(N_GROUPS)
    padded = (counts + EXP_TM - 1) // EXP_TM * EXP_TM
    ends = jnp.cumsum(padded)
    offs = ends - padded
    rec = metat.astype(jnp.int32)
    half = (jnp.arange(N_TOK, dtype=jnp.int32) // HALF_TOK)[None, :]
    group = rec[M_E1:M_E2 + 1] + N_EXP * half
    is_group = group[None] == jnp.arange(N_GROUPS, dtype=jnp.int32)[:, None, None]
    pos = jnp.sum(jnp.where(is_group, offs[:, None, None], 0), axis=0) + rec[M_R1:M_R2 + 1]
    src = _order(pos.reshape(N_ASSIGN), offs + counts, ends)
    g12 = metat[M_G1:M_G2 + 1].reshape(2, 2, HALF_TOK)
    gates = jnp.concatenate([g12[0], jnp.zeros((2, CODE_PLANE - HALF_TOK), F32), g12[1]], axis=1)
    ys = _experts(offs // EXP_TM, padded // EXP_TM, src, gates.reshape(2 * GATE_BLK), h, wg, wu, wd, layer)
    return _combine(ys, x_all, mod_l)


def _final_kernel(x_ref, g_ref, o_ref):
    o_ref[...] = _rms(x_ref[...], g_ref[...])


def _final_norm(x_all, g, blk0, n_rows):
    tb = 512
    return pl.pallas_call(
        _final_kernel,
        grid=(n_rows // tb,),
        in_specs=[pl.BlockSpec((tb, D), lambda i: (blk0 + i, 0)), _resident((1, D), lambda i: (0, 0))],
        out_specs=pl.BlockSpec((tb, D), lambda i: (i, 0)),
        out_shape=jax.ShapeDtypeStruct((n_rows, D), F32),
        compiler_params=_cp("arbitrary"),
        name="final_norm",
    )(x_all, g)


def _rope_tables():
    pos = jnp.arange(DEC_SEQ)
    row = (pos // GRID_W).astype(F32)
    col = (pos % GRID_W).astype(F32)
    n_freq = HD // 4
    inv = ROPE_THETA ** (-jnp.arange(n_freq, dtype=F32) / n_freq)
    ang = jnp.concatenate([row[:, None] * inv, col[:, None] * inv], axis=-1)
    cos = jnp.repeat(jnp.cos(ang), 2, axis=-1)
    sin = jnp.repeat(jnp.sin(ang), 2, axis=-1) * jnp.tile(jnp.array([-1.0, 1.0], F32), HD // 2)
    nblk = DEC_SEQ // QKV_TB
    cos_tab = jnp.concatenate([jnp.ones((1, QKV_TB, HD), F32), cos.reshape(nblk, QKV_TB, HD)], axis=0)
    sin_tab = jnp.concatenate([jnp.zeros((1, QKV_TB, HD), F32), sin.reshape(nblk, QKV_TB, HD)], axis=0)
    return cos_tab, sin_tab


def kernel(x_prompt, x_sample, state_gla, cache_k, cache_v, c, c_ctx, w_mod, b_mod, norm1_g, norm2_g,
           w_in_even, w_gate_up, b_gate_up, gla_norm_g, w_spatial, b_spatial, w_out_even,
           w_in_odd, q_norm_g, k_norm_g, w_out_odd, w_router_group, b_router_group,
           w_router_expert, b_router_expert, w_exp_gate, w_exp_up, w_exp_down, final_norm_g):
    x_all = None
    cond8 = jnp.concatenate([c_ctx[None], c, jnp.zeros((3, D), F32)], axis=0)
    mod = _modulation(cond8, w_mod, b_mod)
    cos_tab, sin_tab = _rope_tables()
    zero_state = jnp.zeros((1, 2, QK_W, GLA_DV), F32)
    state_in = state_gla.reshape(DEC_BATCH, -1, 2, QK_W, GLA_DV)
    cache_k2 = cache_k.reshape(DEC_BATCH, -1, SEQ, KV_W)
    cache_v2 = cache_v.reshape(DEC_BATCH, -1, SEQ, KV_W)

    gla_states, ctx_k, ctx_v = [], [], []
    for l in range(DEPTH):
        i = l // 2
        n1g = norm1_g[l][None]
        if l % 2 == 0:
            w = w_in_even[i]
            win = jnp.concatenate([w[:, :1536], w[:, 1568:], w[:, 1536:1568], jnp.zeros((D, 96), F32)],
                                  axis=1).astype(BF16)
            wgu = jnp.zeros((128, 2 * QK_W), F32)
            wgu = wgu.at[0:GLA_RANK, 0:QK_W].set(w_gate_up[i, 0])
            wgu = wgu.at[GLA_RANK:2 * GLA_RANK, QK_W:].set(w_gate_up[i, 1]).astype(BF16)
            bgu = b_gate_up[i].reshape(1, 2 * QK_W)
            args = (mod[l], n1g, win, wgu, bgu, gla_norm_g[i][None], w_spatial[i].astype(BF16),
                    b_spatial[i].T, w_out_even[i].astype(BF16))
            first = l == 0
            x_all, st = _even_mixer(x_all, *args, zero_state, latent=False,
                                    x_first=x_prompt.reshape(N_CTX, D) if first else None)
            gla_states.append(st)
            x_all, _ = _even_mixer(x_all, *args, state_in[:, i], latent=True,
                                   x_first=x_sample.reshape(N_LAT, D) if first else None)
        else:
            q, k, v = _qkv(x_all, mod[l], n1g, w_in_odd[i].astype(BF16), q_norm_g[i][None],
                           k_norm_g[i][None], cos_tab, sin_tab)
            ctx_k.append(k[:N_CTX].reshape(BATCH, SEQ, ATT_KV, HD))
            ctx_v.append(v[:N_CTX].reshape(BATCH, SEQ, ATT_KV, HD))
            wout = w_out_odd[i].astype(BF16)
            x_all = _attention(x_all, mod[l], q, k, v, wout)
            x_all = _attention(x_all, mod[l], q, k, v, wout, cache_k2, cache_v2, layer_i=i)
        wr = jnp.concatenate([w_router_expert[l], w_router_group[l],
                              jnp.zeros((D, 128 - N_EXP - MOE_GROUPS), F32)], axis=1)
        br = jnp.concatenate([b_router_expert[l], b_router_group[l],
                              jnp.zeros((128 - N_EXP - MOE_GROUPS,), F32)])[None]
        x_all = _moe(x_all, mod[l], norm2_g[l][None], wr, br, w_exp_gate, w_exp_up, w_exp_down, l)

    fg = final_norm_g[None]
    y_prompt = _final_norm(x_all, fg, 0, N_CTX).reshape(BATCH, SEQ, D)
    y_sample = _final_norm(x_all, fg, N_CTX // 512, N_LAT).reshape(DEC_BATCH, DEC_SEQ, D)
    new_state = jnp.stack(gla_states, axis=1).reshape(BATCH, -1, 2, GLA_HEADS, GLA_DK, GLA_DV)
    return (y_prompt, y_sample, new_state, jnp.stack(ctx_k, axis=1), jnp.stack(ctx_v, axis=1))
```

```python
import functools

import jax
import jax.numpy as jnp
import numpy as np
from jax import lax
from jax.experimental import pallas as pl
from jax.experimental.pallas import tpu as pltpu

F32 = jnp.float32
BF16 = jnp.bfloat16

D = 1024
BATCH, SEQ = 16, 256
DEC_BATCH, DEC_SEQ = 4, 1024
N_CTX = BATCH * SEQ
N_LAT = DEC_BATCH * DEC_SEQ
N_TOK = N_CTX + N_LAT
DEPTH = 4
EPS = 1e-6
GRID_W = 64
ROPE_THETA = 10000.0

GLA_HEADS, GLA_DK, GLA_DV, GLA_RANK, GLA_CHUNK, GLA_TAU = 4, 64, 128, 16, 128, 16.0
QK_W = GLA_HEADS * GLA_DK
V_W = GLA_HEADS * GLA_DV
GMLP_GROUPS, GMLP_DIM, GMLP_CHUNK = 4, 128, 128
GMLP_W = GMLP_GROUPS * GMLP_DIM
C_Q, C_K, C_V, C_G, C_U, C_VG, C_A = 0, 256, 512, 1024, 1536, 2048, 2560
EVEN_PACK = 2688

ATT_HEADS, ATT_KV, HD = 8, 2, 128
ATT_G = ATT_HEADS // ATT_KV
Q_W = ATT_HEADS * HD
KV_W = ATT_KV * HD

MOE_GROUPS, MOE_PER_GROUP = 4, 8
N_EXP = MOE_GROUPS * MOE_PER_GROUP
D_EXP = D // 4
NEG = -1e30

VMEM_LIMIT = 56 * 1024 * 1024


def _cp(*sem):
    return pltpu.CompilerParams(dimension_semantics=sem, vmem_limit_bytes=VMEM_LIMIT)


def _dot(a, b):
    return jnp.dot(a.astype(BF16), b.astype(BF16), preferred_element_type=F32)


def _dot_nt(a, b):
    return lax.dot_general(a.astype(BF16), b.astype(BF16), (((1,), (1,)), ((), ())),
                           preferred_element_type=F32)


def _dot_tn(a, b):
    return lax.dot_general(a.astype(BF16), b.astype(BF16), (((0,), (0,)), ((), ())),
                           preferred_element_type=F32)


def _rms(x, g):
    return x * lax.rsqrt(jnp.mean(x * x, axis=-1, keepdims=True) + EPS) * g


def _silu(x):
    return x * jax.nn.sigmoid(x)


def _gelu(x):
    return 0.5 * x * (1.0 + jnp.tanh(np.sqrt(2.0 / np.pi).astype(np.float32) * (x + 0.044715 * (x * x * x))))


def _log_sigmoid(z):
    return jnp.minimum(z, 0.0) - jnp.log(1.0 + jnp.exp(-jnp.abs(z)))


def _rows_to_tiles(ref, x):
    rows = x.shape[0]
    for j in range(D // 128):
        ref[pl.ds(j, rows, stride=8), :] = x[:, j * 128:(j + 1) * 128]


def _tiles_to_rows(ref, rows):
    return jnp.concatenate([ref[pl.ds(j, rows, stride=8), :] for j in range(D // 128)], axis=1)


def _tile_of(ref, row8):
    return ref.at[pl.ds(pl.multiple_of(row8, 8), 8), :]


def _resident(shape, index_map):
    return pl.BlockSpec(shape, index_map, pipeline_mode=pl.Buffered(1))


def _mod_kernel(cond_ref, w_ref, b_ref, o_ref):
    c = cond_ref[...]
    o_ref[...] = jnp.dot(_silu(c), w_ref[...], precision=lax.Precision.HIGHEST,
                         preferred_element_type=F32) + b_ref[...]


def _modulation(cond8, w_mod, b_mod):
    tn = 1024
    out = pl.pallas_call(
        _mod_kernel,
        grid=(DEPTH, 6 * D // tn),
        in_specs=[
            pl.BlockSpec((8, D), lambda l, j: (0, 0)),
            pl.BlockSpec((None, D, tn), lambda l, j: (l, 0, j)),
            pl.BlockSpec((None, 1, tn), lambda l, j: (l, 0, j)),
        ],
        out_specs=pl.BlockSpec((None, 8, tn), lambda l, j: (l, 0, j)),
        out_shape=jax.ShapeDtypeStruct((DEPTH, 8, 6 * D), F32),
        compiler_params=_cp("arbitrary", "arbitrary"),
        name="adaln_mod",
    )(cond8, w_mod, b_mod.reshape(DEPTH, 1, 6 * D))
    return out.reshape(DEPTH, 8, 6, D)


def _even_kernel(x_ref, mod_ref, n1g_ref, win_ref, wgu_ref, bgu_ref, glag_ref, ws_ref, bs_ref,
                 wout_ref, s0_ref, xo_ref, st_ref, proj, la, o_f, o_b, st_scr, *, T):
    n_chunks = T // GLA_CHUNK
    shift, scale, gate = mod_ref[0:1, :], mod_ref[1:2, :], mod_ref[2:3, :]
    RB = 128

    def proj_body(r, carry):
        r0 = pl.multiple_of(r * RB, RB)
        h = _rms(x_ref[pl.ds(r0, RB), :], n1g_ref[...]) * (1.0 + scale) + shift
        p = _dot(h, win_ref[...])
        proj[pl.ds(r0, RB), :] = p
        z = _dot(p[:, C_A:C_A + 128], wgu_ref[...]) + bgu_ref[...]
        la[pl.ds(r0, RB), :] = _log_sigmoid(z) * (1.0 / GLA_TAU)
        return carry

    lax.fori_loop(0, T // RB, proj_body, 0)

    st_scr[0] = s0_ref[0].T
    st_scr[1] = s0_ref[1].T

    ci = lax.broadcasted_iota(jnp.int32, (GLA_CHUNK, GLA_CHUNK), 0)
    cj = lax.broadcasted_iota(jnp.int32, (GLA_CHUNK, GLA_CHUNK), 1)
    tri = (jnp.where(ci >= cj, 1.0, 0.0).astype(BF16), jnp.where(ci <= cj, 1.0, 0.0).astype(BF16))
    ai = lax.broadcasted_iota(jnp.int32, (GLA_HEADS * GLA_CHUNK, GLA_CHUNK), 0) % GLA_CHUNK
    aj = lax.broadcasted_iota(jnp.int32, (GLA_HEADS * GLA_CHUNK, GLA_CHUNK), 1)
    amask = (ai >= aj, ai <= aj)
    lane_head = lax.broadcasted_iota(jnp.int32, (1, QK_W), 1) // GLA_DK
    hmask = [jnp.where(lane_head == h, 1.0, 0.0) for h in range(GLA_HEADS)]

    def chunk_body(i, carry):
        for d in range(2):
            c = i if d == 0 else n_chunks - 1 - i
            r0 = pl.multiple_of(c * GLA_CHUNK, GLA_CHUNK)
            q = proj[pl.ds(r0, GLA_CHUNK), C_Q:C_Q + QK_W] * (GLA_DK ** -0.5)
            k = proj[pl.ds(r0, GLA_CHUNK), C_K:C_K + QK_W]
            v = proj[pl.ds(r0, GLA_CHUNK), C_V:C_V + V_W]
            lac = la[pl.ds(r0, GLA_CHUNK), d * QK_W:(d + 1) * QK_W]
            hi = lac.astype(BF16)
            lo = (lac - hi.astype(F32)).astype(BF16)
            b = (jnp.dot(tri[d], hi, preferred_element_type=F32)
                 + jnp.dot(tri[d], lo, preferred_element_type=F32))
            bend = b[GLA_CHUNK - 1:GLA_CHUNK, :] if d == 0 else b[0:1, :]
            qe = q * jnp.exp(b)
            ke = k * jnp.exp(-b)
            kd = k * jnp.exp(bend - b)
            st = st_scr[d]
            qstack = jnp.concatenate([qe * hmask[h] for h in range(GLA_HEADS)], axis=0).astype(BF16)
            att = jnp.where(amask[d], _dot_nt(qstack, ke), 0.0)
            inter = _dot_nt(qstack, st)
            outs = []
            for h in range(GLA_HEADS):
                rows = slice(h * GLA_CHUNK, (h + 1) * GLA_CHUNK)
                outs.append(_dot(att[rows], v[:, h * GLA_DV:(h + 1) * GLA_DV]) + inter[rows])
            o = jnp.concatenate(outs, axis=1)
            if d == 0:
                o_f[pl.ds(r0, GLA_CHUNK), :] = o
            else:
                o_b[pl.ds(r0, GLA_CHUNK), :] = o
            vstack = jnp.concatenate([v[:, h * GLA_DV:(h + 1) * GLA_DV] for h in range(GLA_HEADS)], axis=0)
            kstack = jnp.concatenate([kd * hmask[h] for h in range(GLA_HEADS)], axis=0)
            st_scr[d] = st * jnp.exp(bend) + _dot_tn(vstack, kstack)
        return carry

    lax.fori_loop(0, n_chunks, chunk_body, 0, unroll=2)
    st_ref[0] = st_scr[0].T
    st_ref[1] = st_scr[1].T

    def out_body(r, carry):
        r0 = pl.multiple_of(r * RB, RB)
        osum = o_f[pl.ds(r0, RB), :] + o_b[pl.ds(r0, RB), :]
        g = proj[pl.ds(r0, RB), C_G:C_G + V_W]
        u = proj[pl.ds(r0, RB), C_U:C_U + GMLP_W]
        vg = _gelu(proj[pl.ds(r0, RB), C_VG:C_VG + GMLP_W])
        parts = []
        for h in range(GLA_HEADS):
            oh = osum[:, h * GLA_DV:(h + 1) * GLA_DV]
            parts.append(_rms(oh, glag_ref[...]) * _silu(g[:, h * GLA_DV:(h + 1) * GLA_DV]))
        for gi in range(GMLP_GROUPS):
            vc = vg[:, gi * GMLP_DIM:(gi + 1) * GMLP_DIM]
            vc = vc - jnp.mean(vc, axis=-1, keepdims=True)
            vn = vc * lax.rsqrt(jnp.mean(vc * vc, axis=-1, keepdims=True) + EPS)
            sg = _dot(ws_ref[gi], vn) + bs_ref[:, gi:gi + 1]
            parts.append(_gelu(u[:, gi * GMLP_DIM:(gi + 1) * GMLP_DIM]) * sg)
        mix = jnp.concatenate(parts, axis=1)
        y = _dot(mix, wout_ref[...])
        xo_ref[pl.ds(r0, RB), :] = x_ref[pl.ds(r0, RB), :] + gate * y
        return carry

    lax.fori_loop(0, T // RB, out_body, 0)


def _even_mixer(x_all, mod_l, n1g, win, wgu, bgu, glag, ws, bs, wout, s0, *, latent, x_first=None):
    if latent:
        T, nseq, blk0 = DEC_SEQ, DEC_BATCH, N_CTX // DEC_SEQ
        cond = lambda i: 1 + i
        s0_spec = pl.BlockSpec((None, 2, QK_W, GLA_DV), lambda i: (i, 0, 0, 0))
    else:
        T, nseq, blk0 = SEQ, BATCH, 0
        cond = lambda i: 0
        s0_spec = pl.BlockSpec((None, 2, QK_W, GLA_DV), lambda i: (0, 0, 0, 0))
    const2 = lambda i: (0, 0)
    body = functools.partial(_even_kernel, T=T)
    x_spec = pl.BlockSpec((T, D), lambda i: (blk0 + i, 0))
    if x_first is None:
        lead_specs, lead_args, aliases = [x_spec], (x_all,), {0: 0}
    elif x_all is None:
        lead_specs, lead_args, aliases = [pl.BlockSpec((T, D), lambda i: (i, 0))], (x_first,), {}
    else:
        lead_specs = [pl.BlockSpec(memory_space=pl.ANY), pl.BlockSpec((T, D), lambda i: (i, 0))]
        lead_args, aliases = (x_all, x_first), {0: 0}
        body = lambda dst_ref, *refs: _even_kernel(*refs, T=T)
    x_new, states = pl.pallas_call(
        body,
        grid=(nseq,),
        in_specs=lead_specs + [
            pl.BlockSpec((None, 6, D), lambda i: (cond(i), 0, 0)),
            _resident((1, D), const2),
            _resident((D, EVEN_PACK), const2),
            _resident((128, 2 * QK_W), const2),
            _resident((1, 2 * QK_W), const2),
            _resident((1, GLA_DV), const2),
            _resident((GMLP_GROUPS, GMLP_CHUNK, GMLP_CHUNK), lambda i: (0, 0, 0)),
            _resident((GMLP_CHUNK, GMLP_GROUPS), const2),
            _resident((D, D), const2),
            s0_spec,
        ],
        out_specs=[
            x_spec,
            pl.BlockSpec((None, 2, QK_W, GLA_DV), lambda i: (i, 0, 0, 0)),
        ],
        out_shape=[
            jax.ShapeDtypeStruct((N_TOK, D), F32),
            jax.ShapeDtypeStruct((nseq, 2, QK_W, GLA_DV), F32),
        ],
        scratch_shapes=[
            pltpu.VMEM((T, EVEN_PACK), F32),
            pltpu.VMEM((T, 2 * QK_W), F32),
            pltpu.VMEM((T, V_W), F32),
            pltpu.VMEM((T, V_W), F32),
            pltpu.VMEM((2, GLA_DV, QK_W), F32),
        ],
        input_output_aliases=aliases,
        compiler_params=_cp("arbitrary"),
        name="even_mixer_latent" if latent else "even_mixer_context",
    )(*lead_args, mod_l, n1g, win, wgu, bgu, glag, ws, bs, wout, s0)
    return x_new, states


QKV_TB = 512


def _qkv_kernel(x_ref, mod_ref, n1g_ref, win_ref, gq_ref, gk_ref, cos_ref, sin_ref, q_ref, k_ref, v_ref):
    shift, scale = mod_ref[0:1, :], mod_ref[1:2, :]
    h = _rms(x_ref[...], n1g_ref[...]) * (1.0 + scale) + shift
    p = _dot(h, win_ref[...])
    cos, sin = cos_ref[...], sin_ref[...]
    even_lane = lax.broadcasted_iota(jnp.int32, (1, HD), 1) % 2 == 0

    def norm_rope(xh, g):
        xn = _rms(xh, g)
        swapped = jnp.where(even_lane, pltpu.roll(xn, HD - 1, axis=1), pltpu.roll(xn, 1, axis=1))
        return xn * cos + swapped * sin

    for hh in range(ATT_HEADS):
        q_ref[:, hh * HD:(hh + 1) * HD] = norm_rope(p[:, hh * HD:(hh + 1) * HD], gq_ref[...]).astype(BF16)
    for hh in range(ATT_KV):
        k_ref[:, hh * HD:(hh + 1) * HD] = norm_rope(p[:, Q_W + hh * HD:Q_W + (hh + 1) * HD], gk_ref[...])
    v_ref[...] = p[:, Q_W + KV_W:]


def _qkv(x_all, mod_l, n1g, win, gq, gk, cos_tab, sin_tab):
    nb_ctx = N_CTX // QKV_TB
    per_seq = DEC_SEQ // QKV_TB
    cond = lambda i: jnp.where(i < nb_ctx, 0, 1 + (i - nb_ctx) // per_seq)
    tab = lambda i: jnp.where(i < nb_ctx, 0, 1 + (i - nb_ctx) % per_seq)
    const2 = lambda i: (0, 0)
    return pl.pallas_call(
        _qkv_kernel,
        grid=(N_TOK // QKV_TB,),
        in_specs=[
            pl.BlockSpec((QKV_TB, D), lambda i: (i, 0)),
            pl.BlockSpec((None, 6, D), lambda i: (cond(i), 0, 0)),
            _resident((1, D), const2),
            _resident((D, Q_W + 2 * KV_W), const2),
            _resident((1, HD), const2),
            _resident((1, HD), const2),
            pl.BlockSpec((None, QKV_TB, HD), lambda i: (tab(i), 0, 0)),
            pl.BlockSpec((None, QKV_TB, HD), lambda i: (tab(i), 0, 0)),
        ],
        out_specs=[
            pl.BlockSpec((QKV_TB, Q_W), lambda i: (i, 0)),
            pl.BlockSpec((QKV_TB, KV_W), lambda i: (i, 0)),
            pl.BlockSpec((QKV_TB, KV_W), lambda i: (i, 0)),
        ],
        out_shape=[
            jax.ShapeDtypeStruct((N_TOK, Q_W), BF16),
            jax.ShapeDtypeStruct((N_TOK, KV_W), F32),
            jax.ShapeDtypeStruct((N_TOK, KV_W), F32),
        ],
        compiler_params=_cp("arbitrary"),
        name="odd_qkv",
    )(x_all, mod_l, n1g, win, gq, gk, cos_tab, sin_tab)


ATT_TQ = 256


def _attn_kernel(*refs, n_kv):
    q_ref = refs[0]
    kv_refs = refs[1:1 + 2 * n_kv]
    x_ref, mod_ref, wout_ref, xo_ref, att_scr = refs[1 + 2 * n_kv:]
    gate = mod_ref[2:3, :]
    for kh in range(ATT_KV):
        ks = [kv_refs[2 * s][:, kh * HD:(kh + 1) * HD].astype(BF16) for s in range(n_kv)]
        vs = [kv_refs[2 * s + 1][:, kh * HD:(kh + 1) * HD].astype(BF16) for s in range(n_kv)]
        for g in range(ATT_G):
            hh = kh * ATT_G + g
            qh = q_ref[:, hh * HD:(hh + 1) * HD]
            ss = [_dot_nt(qh, kk) * (HD ** -0.5) for kk in ks]
            m = ss[0].max(axis=-1, keepdims=True)
            for s in ss[1:]:
                m = jnp.maximum(m, s.max(axis=-1, keepdims=True))
            ps = [jnp.exp(s - m) for s in ss]
            den = ps[0].sum(axis=-1, keepdims=True)
            for p in ps[1:]:
                den = den + p.sum(axis=-1, keepdims=True)
            o = _dot(ps[0], vs[0])
            for p, vv in zip(ps[1:], vs[1:]):
                o = o + _dot(p, vv)
            att_scr[:, hh * HD:(hh + 1) * HD] = o / den
    y = _dot(att_scr[...], wout_ref[...])
    xo_ref[...] = x_ref[...] + gate * y


def _attention(x_all, mod_l, q, k, v, wout, cache_k=None, cache_v=None, layer_i=0):
    latent = cache_k is not None
    const2 = lambda *a: (0, 0)
    if latent:
        nq = DEC_SEQ // ATT_TQ
        row_blk = lambda b, j: (N_CTX // ATT_TQ + b * nq + j, 0)
        grid = (DEC_BATCH, nq)
        kv_specs = [
            pl.BlockSpec((None, None, SEQ, KV_W), lambda b, j: (b, layer_i, 0, 0)),
            pl.BlockSpec((None, None, SEQ, KV_W), lambda b, j: (b, layer_i, 0, 0)),
            pl.BlockSpec((DEC_SEQ, KV_W), lambda b, j: (N_CTX // DEC_SEQ + b, 0)),
            pl.BlockSpec((DEC_SEQ, KV_W), lambda b, j: (N_CTX // DEC_SEQ + b, 0)),
        ]
        kv_args = (cache_k, cache_v, k, v)
        mod_spec = pl.BlockSpec((None, 6, D), lambda b, j: (1 + b, 0, 0))
        sem = ("arbitrary", "arbitrary")
        n_kv = 2
    else:
        row_blk = lambda i: (i, 0)
        grid = (BATCH,)
        kv_specs = [pl.BlockSpec((SEQ, KV_W), row_blk), pl.BlockSpec((SEQ, KV_W), row_blk)]
        kv_args = (k, v)
        mod_spec = pl.BlockSpec((None, 6, D), lambda i: (0, 0, 0))
        sem = ("arbitrary",)
        n_kv = 1
    n_in = 1 + len(kv_args)
    return pl.pallas_call(
        functools.partial(_attn_kernel, n_kv=n_kv),
        grid=grid,
        in_specs=[pl.BlockSpec((ATT_TQ, Q_W), row_blk)] + kv_specs + [
            pl.BlockSpec((ATT_TQ, D), row_blk),
            mod_spec,
            _resident((D, D), const2),
        ],
        out_specs=pl.BlockSpec((ATT_TQ, D), row_blk),
        out_shape=jax.ShapeDtypeStruct((N_TOK, D), F32),
        scratch_shapes=[pltpu.VMEM((ATT_TQ, Q_W), F32)],
        input_output_aliases={n_in: 0},
        compiler_params=_cp(*sem),
        name="attention_latent" if latent else "attention_context",
    )(q, *kv_args, x_all, mod_l, wout)


ROUTE_TB = 512
HALF_TOK = N_TOK // 2
M_E1, M_E2, M_G1, M_G2, M_R1, M_R2 = 0, 1, 2, 3, 4, 5


def _router_kernel(x_ref, mod_ref, n2g_ref, whi_ref, wlo_ref, br_ref, h_ref, metat_ref, cnt_ref, run):
    @pl.when(pl.program_id(0) % (HALF_TOK // ROUTE_TB) == 0)
    def _():
        run[...] = jnp.zeros_like(run)

    shift, scale = mod_ref[3:4, :], mod_ref[4:5, :]
    h = _rms(x_ref[...], n2g_ref[...]) * (1.0 + scale) + shift
    _rows_to_tiles(h_ref, h)
    h_hi = h.astype(BF16)
    h_lo = (h - h_hi.astype(F32)).astype(BF16)
    dot = functools.partial(jnp.dot, preferred_element_type=F32)
    logits = dot(h_hi, whi_ref[...]) + dot(h_lo, whi_ref[...]) + dot(h_hi, wlo_ref[...]) + br_ref[...]
    lane = lax.broadcasted_iota(jnp.int32, logits.shape, 1).astype(F32)
    big = 1e4

    def first_argmax(vals):
        m = vals.max(axis=-1, keepdims=True)
        return m, jnp.where(vals == m, lane, big).min(axis=-1, keepdims=True)

    gl = jnp.where((lane >= N_EXP) & (lane < N_EXP + MOE_GROUPS), logits, NEG)
    gmax, glane = first_argmax(gl)
    g_p = 1.0 / jnp.exp(gl - gmax).sum(axis=-1, keepdims=True)
    lo = (glane - N_EXP) * MOE_PER_GROUP
    el = jnp.where((lane >= lo) & (lane < lo + MOE_PER_GROUP), logits, NEG)
    m1, i1 = first_argmax(el)
    m2, i2 = first_argmax(jnp.where(lane == i1, NEG, el))
    t = jnp.exp(m2 - m1)
    w1 = 1.0 / (1.0 + t)
    sel1, sel2 = lane == i1, lane == i2
    onehot = jnp.where(sel1 | sel2, 1.0, 0.0)
    ri = lax.broadcasted_iota(jnp.int32, (ROUTE_TB, ROUTE_TB), 0)
    rj = lax.broadcasted_iota(jnp.int32, (ROUTE_TB, ROUTE_TB), 1)
    before = _dot(jnp.where(ri > rj, 1.0, 0.0), onehot) + run[...]
    r1 = jnp.where(sel1, before, 0.0).sum(axis=-1, keepdims=True)
    r2 = jnp.where(sel2, before, 0.0).sum(axis=-1, keepdims=True)
    run[...] += onehot.sum(axis=0, keepdims=True)
    cnt_ref[...] = run[...]
    meta = jnp.zeros_like(logits)
    for j, val in enumerate([i1, i2, w1 * g_p, (t * w1) * g_p, r1, r2]):
        meta = jnp.where(lane == j, val, meta)
    metat_ref[...] = meta.T[0:8, :]


def _router(x_all, mod_l, n2g, wr, br):
    w_hi = wr.astype(BF16)
    wr_lo = (wr - w_hi.astype(F32)).astype(BF16)
    nb_ctx = N_CTX // ROUTE_TB
    per_seq = DEC_SEQ // ROUTE_TB
    cond = lambda i: jnp.where(i < nb_ctx, 0, 1 + (i - nb_ctx) // per_seq)
    const2 = lambda i: (0, 0)
    return pl.pallas_call(
        _router_kernel,
        grid=(N_TOK // ROUTE_TB,),
        in_specs=[
            pl.BlockSpec((ROUTE_TB, D), lambda i: (i, 0)),
            pl.BlockSpec((None, 6, D), lambda i: (cond(i), 0, 0)),
            _resident((1, D), const2),
            _resident((D, 128), const2),
            _resident((D, 128), const2),
            _resident((1, 128), const2),
        ],
        out_specs=[
            pl.BlockSpec((ROUTE_TB * 8, 128), lambda i: (i, 0)),
            pl.BlockSpec((8, ROUTE_TB), lambda i: (0, i)),
            pl.BlockSpec((None, 1, 128), lambda i: (i // (HALF_TOK // ROUTE_TB), 0, 0)),
        ],
        out_shape=[
            jax.ShapeDtypeStruct((N_TOK * 8, 128), F32),
            jax.ShapeDtypeStruct((8, N_TOK), F32),
            jax.ShapeDtypeStruct((2, 1, 128), F32),
        ],
        scratch_shapes=[pltpu.VMEM((1, 128), F32)],
        compiler_params=_cp("arbitrary"),
        name="moe_router",
    )(x_all, mod_l, n2g, w_hi, wr_lo, br)


EXP_TM = 128
N_ASSIGN = 2 * N_TOK
N_GROUPS = 2 * N_EXP
MAX_TILES = N_ASSIGN // EXP_TM + N_GROUPS
N_SORTED = MAX_TILES * EXP_TM
ORDER_BLK = 2048
CODE_PLANE = 2 * HALF_TOK
CODE_MASK = 8 * CODE_PLANE - 1
DUMMY8 = HALF_TOK * 8


def _order_kernel(pos1_ref, pos2_ref, pad_lo_ref, pad_hi_ref, src_ref):
    i = pl.program_id(0)
    local = (i % (HALF_TOK // ORDER_BLK)) * ORDER_BLK

    def body(t, carry):
        src_ref[pos1_ref[t]] = (local + t) * 8
        src_ref[pos2_ref[t]] = (local + t + CODE_PLANE) * 8
        return carry

    lax.fori_loop(0, ORDER_BLK, body, 0, unroll=16)

    @pl.when(i == 0)
    def _():
        def group(g, carry):
            def pad(p, c):
                src_ref[p] = DUMMY8
                return c
            return lax.fori_loop(pad_lo_ref[g], pad_hi_ref[g], pad, carry)

        lax.fori_loop(0, N_GROUPS, group, 0)


def _order(pos, pad_lo, pad_hi):
    return pl.pallas_call(
        _order_kernel,
        grid=(N_TOK // ORDER_BLK,),
        in_specs=[
            pl.BlockSpec((ORDER_BLK,), lambda i: (i,), memory_space=pltpu.SMEM),
            pl.BlockSpec((ORDER_BLK,), lambda i: (N_TOK // ORDER_BLK + i,), memory_space=pltpu.SMEM),
            pl.BlockSpec(memory_space=pltpu.SMEM),
            pl.BlockSpec(memory_space=pltpu.SMEM),
        ],
        out_specs=pl.BlockSpec(memory_space=pltpu.SMEM),
        out_shape=jax.ShapeDtypeStruct((N_SORTED,), jnp.int32),
        compiler_params=_cp("arbitrary"),
        name="moe_order",
    )(pos, pos, pad_lo, pad_hi)


GATE_BLK = CODE_PLANE + HALF_TOK
ACC_TOK = HALF_TOK + 64
ROW_GROUP = 8


def _experts_kernel(tile0_ref, ntile_ref, src_ref, gs_ref, h_hbm, wg_ref, wu_ref, wd_ref, out_hbm,
                    h_res, acc, xbuf, ybuf, wgb, wub, wdb, sem):
    group = pl.program_id(0)
    expert = group % N_EXP
    rows0 = pl.multiple_of((group // N_EXP) * (HALF_TOK * 8), 8)

    @pl.when(expert == 0)
    def _():
        cp = pltpu.make_async_copy(h_hbm.at[pl.ds(rows0, HALF_TOK * 8), :], h_res.at[pl.ds(0, HALF_TOK * 8), :], sem)
        cp.start()
        h_res[pl.ds(DUMMY8, 8), :] = jnp.zeros((8, 128), F32)

        def zero(i, carry):
            acc[pl.ds(pl.multiple_of(i * 512, 512), 512), :] = jnp.zeros((512, 128), F32)
            return carry

        lax.fori_loop(0, ACC_TOK * 8 // 512, zero, 0)
        cp.wait()

    n_tiles = ntile_ref[group]

    @pl.when(n_tiles > 0)
    def _():
        wgb[...] = wg_ref[...].astype(BF16)
        wub[...] = wu_ref[...].astype(BF16)
        wdb[...] = wd_ref[...].astype(BF16)

    def tile_body(j, carry):
        base = (tile0_ref[group] + j) * EXP_TM

        def gather(g, c):
            for i in range(ROW_GROUP):
                r = g * ROW_GROUP + i
                xbuf[pl.ds(pl.multiple_of(r * 8, 8), 8), :] = _tile_of(h_res, src_ref[base + r] & CODE_MASK)[...]
            return c

        lax.fori_loop(0, EXP_TM // ROW_GROUP, gather, 0)
        x = _tiles_to_rows(xbuf, EXP_TM).astype(BF16)
        hid = _silu(_dot(x, wgb[...])) * _dot(x, wub[...])
        _rows_to_tiles(ybuf, _dot(hid, wdb[...]))

        def accumulate(g, c):
            targets, values = [], []
            for i in range(ROW_GROUP):
                r = g * ROW_GROUP + i
                code = src_ref[base + r]
                target = _tile_of(acc, code & CODE_MASK)
                targets.append(target)
                values.append(target[...] + gs_ref[code >> 3] * ybuf[pl.ds(pl.multiple_of(r * 8, 8), 8), :])
            for target, value in zip(targets, values):
                target[...] = value
            return c

        lax.fori_loop(0, EXP_TM // ROW_GROUP, accumulate, 0)
        return carry

    lax.fori_loop(0, n_tiles, tile_body, 0)

    @pl.when(expert == N_EXP - 1)
    def _():
        cp = pltpu.make_async_copy(acc.at[pl.ds(0, HALF_TOK * 8), :], out_hbm.at[pl.ds(rows0, HALF_TOK * 8), :], sem)
        cp.start()
        cp.wait()


def _experts(tile0, n_tiles, src, gs, h, wg, wu, wd, layer):
    wmap = lambda g, t0, nt, src: (layer, g % N_EXP, 0, 0)
    return pl.pallas_call(
        _experts_kernel,
        grid_spec=pltpu.PrefetchScalarGridSpec(
            num_scalar_prefetch=3,
            grid=(N_GROUPS,),
            in_specs=[
                pl.BlockSpec((GATE_BLK,), lambda g, t0, nt, src: (g // N_EXP,), memory_space=pltpu.SMEM),
                pl.BlockSpec(memory_space=pl.ANY),
                pl.BlockSpec((None, None, D, D_EXP), wmap),
                pl.BlockSpec((None, None, D, D_EXP), wmap),
                pl.BlockSpec((None, None, D_EXP, D), wmap),
            ],
            out_specs=pl.BlockSpec(memory_space=pl.ANY),
            scratch_shapes=[
                pltpu.VMEM((ACC_TOK * 8, 128), F32),
                pltpu.VMEM((ACC_TOK * 8, 128), F32),
                pltpu.VMEM((EXP_TM * 8, 128), F32),
                pltpu.VMEM((EXP_TM * 8, 128), F32),
                pltpu.VMEM((D, D_EXP), BF16),
                pltpu.VMEM((D, D_EXP), BF16),
                pltpu.VMEM((D_EXP, D), BF16),
                pltpu.SemaphoreType.DMA,
            ],
        ),
        out_shape=jax.ShapeDtypeStruct((N_TOK * 8, 128), F32),
        compiler_params=_cp("arbitrary"),
        name="moe_experts",
    )(tile0, n_tiles, src, gs, h, wg, wu, wd)


COMBINE_TB = 256


def _combine_kernel(y_ref, x_ref, mod_ref, xo_ref):
    xo_ref[...] = x_ref[...] + mod_ref[5:6, :] * _tiles_to_rows(y_ref, COMBINE_TB)


def _combine(ys, x_all, mod_l):
    nb_ctx = N_CTX // COMBINE_TB
    per_seq = DEC_SEQ // COMBINE_TB
    cond = lambda i: jnp.where(i < nb_ctx, 0, 1 + (i - nb_ctx) // per_seq)
    return pl.pallas_call(
        _combine_kernel,
        grid=(N_TOK // COMBINE_TB,),
        in_specs=[
            pl.BlockSpec((COMBINE_TB * 8, 128), lambda i: (i, 0)),
            pl.BlockSpec((COMBINE_TB, D), lambda i: (i, 0)),
            pl.BlockSpec((None, 6, D), lambda i: (cond(i), 0, 0)),
        ],
        out_specs=pl.BlockSpec((COMBINE_TB, D), lambda i: (i, 0)),
        out_shape=jax.ShapeDtypeStruct((N_TOK, D), F32),
        compiler_params=_cp("arbitrary"),
        name="moe_combine",
    )(ys, x_all, mod_l)


def _moe(x_all, mod_l, n2g, wr, br, wg, wu, wd, layer):
    h, metat, cnt = _router(x_all, mod_l, n2g, wr, br)
    counts = cnt[:, 0, :N_EXP].astype(jnp.int32).reshape(N_GROUPS)
    padded = (counts + EXP_TM - 1) // EXP_TM * EXP_TM
    ends = jnp.cumsum(padded)
    offs = ends - padded
    rec = metat.astype(jnp.int32)
    half = (jnp.arange(N_TOK, dtype=jnp.int32) // HALF_TOK)[None, :]
    group = rec[M_E1:M_E2 + 1] + N_EXP * half
    is_group = group[None] == jnp.arange(N_GROUPS, dtype=jnp.int32)[:, None, None]
    pos = jnp.sum(jnp.where(is_group, offs[:, None, None], 0), axis=0) + rec[M_R1:M_R2 + 1]
    src = _order(pos.reshape(N_ASSIGN), offs + counts, ends)
    g12 = metat[M_G1:M_G2 + 1].reshape(2, 2, HALF_TOK)
    gates = jnp.concatenate([g12[0], jnp.zeros((2, CODE_PLANE - HALF_TOK), F32), g12[1]], axis=1)
    ys = _experts(offs // EXP_TM, padded // EXP_TM, src, gates.reshape(2 * GATE_BLK), h, wg, wu, wd, layer)
    return _combine(ys, x_all, mod_l)


def _final_kernel(x_ref, g_ref, o_ref):
    o_ref[...] = _rms(x_ref[...], g_ref[...])


def _final_norm(x_all, g, blk0, n_rows):
    tb = 512
    return pl.pallas_call(
        _final_kernel,
        grid=(n_rows // tb,),
        in_specs=[pl.BlockSpec((tb, D), lambda i: (blk0 + i, 0)), _resident((1, D), lambda i: (0, 0))],
        out_specs=pl.BlockSpec((tb, D), lambda i: (i, 0)),
        out_shape=jax.ShapeDtypeStruct((n_rows, D), F32),
        compiler_params=_cp("arbitrary"),
        name="final_norm",
    )(x_all, g)


def _rope_tables():
    pos = jnp.arange(DEC_SEQ)
    row = (pos // GRID_W).astype(F32)
    col = (pos % GRID_W).astype(F32)
    n_freq = HD // 4
    inv = ROPE_THETA ** (-jnp.arange(n_freq, dtype=F32) / n_freq)
    ang = jnp.concatenate([row[:, None] * inv, col[:, None] * inv], axis=-1)
    cos = jnp.repeat(jnp.cos(ang), 2, axis=-1)
    sin = jnp.repeat(jnp.sin(ang), 2, axis=-1) * jnp.tile(jnp.array([-1.0, 1.0], F32), HD // 2)
    nblk = DEC_SEQ // QKV_TB
    cos_tab = jnp.concatenate([jnp.ones((1, QKV_TB, HD), F32), cos.reshape(nblk, QKV_TB, HD)], axis=0)
    sin_tab = jnp.concatenate([jnp.zeros((1, QKV_TB, HD), F32), sin.reshape(nblk, QKV_TB, HD)], axis=0)
    return cos_tab, sin_tab


def kernel(x_prompt, x_sample, state_gla, cache_k, cache_v, c, c_ctx, w_mod, b_mod, norm1_g, norm2_g,
           w_in_even, w_gate_up, b_gate_up, gla_norm_g, w_spatial, b_spatial, w_out_even,
           w_in_odd, q_norm_g, k_norm_g, w_out_odd, w_router_group, b_router_group,
           w_router_expert, b_router_expert, w_exp_gate, w_exp_up, w_exp_down, final_norm_g):
    x_all = None
    cond8 = jnp.concatenate([c_ctx[None], c, jnp.zeros((3, D), F32)], axis=0)
    mod = _modulation(cond8, w_mod, b_mod)
    cos_tab, sin_tab = _rope_tables()
    zero_state = jnp.zeros((1, 2, QK_W, GLA_DV), F32)
    state_in = state_gla.reshape(DEC_BATCH, -1, 2, QK_W, GLA_DV)
    cache_k2 = cache_k.reshape(DEC_BATCH, -1, SEQ, KV_W)
    cache_v2 = cache_v.reshape(DEC_BATCH, -1, SEQ, KV_W)

    gla_states, ctx_k, ctx_v = [], [], []
    for l in range(DEPTH):
        i = l // 2
        n1g = norm1_g[l][None]
        if l % 2 == 0:
            w = w_in_even[i]
            win = jnp.concatenate([w[:, :1536], w[:, 1568:], w[:, 1536:1568], jnp.zeros((D, 96), F32)],
                                  axis=1).astype(BF16)
            wgu = jnp.zeros((128, 2 * QK_W), F32)
            wgu = wgu.at[0:GLA_RANK, 0:QK_W].set(w_gate_up[i, 0])
            wgu = wgu.at[GLA_RANK:2 * GLA_RANK, QK_W:].set(w_gate_up[i, 1]).astype(BF16)
            bgu = b_gate_up[i].reshape(1, 2 * QK_W)
            args = (mod[l], n1g, win, wgu, bgu, gla_norm_g[i][None], w_spatial[i].astype(BF16),
                    b_spatial[i].T, w_out_even[i].astype(BF16))
            first = l == 0
            x_all, st = _even_mixer(x_all, *args, zero_state, latent=False,
                                    x_first=x_prompt.reshape(N_CTX, D) if first else None)
            gla_states.append(st)
            x_all, _ = _even_mixer(x_all, *args, state_in[:, i], latent=True,
                                   x_first=x_sample.reshape(N_LAT, D) if first else None)
        else:
            q, k, v = _qkv(x_all, mod[l], n1g, w_in_odd[i].astype(BF16), q_norm_g[i][None],
                           k_norm_g[i][None], cos_tab, sin_tab)
            ctx_k.append(k[:N_CTX].reshape(BATCH, SEQ, ATT_KV, HD))
            ctx_v.append(v[:N_CTX].reshape(BATCH, SEQ, ATT_KV, HD))
            wout = w_out_odd[i].astype(BF16)
            x_all = _attention(x_all, mod[l], q, k, v, wout)
            x_all = _attention(x_all, mod[l], q, k, v, wout, cache_k2, cache_v2, layer_i=i)
        wr = jnp.concatenate([w_router_expert[l], w_router_group[l],
                              jnp.zeros((D, 128 - N_EXP - MOE_GROUPS), F32)], axis=1)
        br = jnp.concatenate([b_router_expert[l], b_router_group[l],
                              jnp.zeros((128 - N_EXP - MOE_GROUPS,), F32)])[None]
        x_all = _moe(x_all, mod[l], norm2_g[l][None], wr, br, w_exp_gate, w_exp_up, w_exp_down, l)

    fg = final_norm_g[None]
    y_prompt = _final_norm(x_all, fg, 0, N_CTX).reshape(BATCH, SEQ, D)
    y_sample = _final_norm(x_all, fg, N_CTX // 512, N_LAT).reshape(DEC_BATCH, DEC_SEQ, D)
    new_state = jnp.stack(gla_states, axis=1).reshape(BATCH, -1, 2, GLA_HEADS, GLA_DK, GLA_DV)
    return (y_prompt, y_sample, new_state, jnp.stack(ctx_k, axis=1), jnp.stack(ctx_v, axis=1))
```

```python
import functools

import jax
import jax.numpy as jnp
import numpy as np
from jax import lax
from jax.experimental import pallas as pl
from jax.experimental.pallas import tpu as pltpu

F32 = jnp.float32
BF16 = jnp.bfloat16

D = 1024
BATCH, SEQ = 16, 256
DEC_BATCH, DEC_SEQ = 4, 1024
N_CTX = BATCH * SEQ
N_LAT = DEC_BATCH * DEC_SEQ
N_TOK = N_CTX + N_LAT
DEPTH = 4
EPS = 1e-6
GRID_W = 64
ROPE_THETA = 10000.0

GLA_HEADS, GLA_DK, GLA_DV, GLA_RANK, GLA_CHUNK, GLA_TAU = 4, 64, 128, 16, 128, 16.0
QK_W = GLA_HEADS * GLA_DK
V_W = GLA_HEADS * GLA_DV
GMLP_GROUPS, GMLP_DIM, GMLP_CHUNK = 4, 128, 128
GMLP_W = GMLP_GROUPS * GMLP_DIM
C_Q, C_K, C_V, C_G, C_U, C_VG, C_A = 0, 256, 512, 1024, 1536, 2048, 2560
EVEN_PACK = 2688

ATT_HEADS, ATT_KV, HD = 8, 2, 128
ATT_G = ATT_HEADS // ATT_KV
Q_W = ATT_HEADS * HD
KV_W = ATT_KV * HD

MOE_GROUPS, MOE_PER_GROUP = 4, 8
N_EXP = MOE_GROUPS * MOE_PER_GROUP
D_EXP = D // 4
NEG = -1e30

VMEM_LIMIT = 56 * 1024 * 1024


def _cp(*sem):
    return pltpu.CompilerParams(dimension_semantics=sem, vmem_limit_bytes=VMEM_LIMIT)


def _dot(a, b):
    return jnp.dot(a.astype(BF16), b.astype(BF16), preferred_element_type=F32)


def _dot_nt(a, b):
    return lax.dot_general(a.astype(BF16), b.astype(BF16), (((1,), (1,)), ((), ())),
                           preferred_element_type=F32)


def _dot_tn(a, b):
    return lax.dot_general(a.astype(BF16), b.astype(BF16), (((0,), (0,)), ((), ())),
                           preferred_element_type=F32)


def _rms(x, g):
    return x * lax.rsqrt(jnp.mean(x * x, axis=-1, keepdims=True) + EPS) * g


def _silu(x):
    return x * jax.nn.sigmoid(x)


def _gelu(x):
    return 0.5 * x * (1.0 + jnp.tanh(np.sqrt(2.0 / np.pi).astype(np.float32) * (x + 0.044715 * (x * x * x))))


def _log_sigmoid(z):
    return jnp.minimum(z, 0.0) - jnp.log(1.0 + jnp.exp(-jnp.abs(z)))


def _rows_to_tiles(ref, x):
    rows = x.shape[0]
    for j in range(D // 128):
        ref[pl.ds(j, rows, stride=8), :] = x[:, j * 128:(j + 1) * 128]


def _tiles_to_rows(ref, rows):
    return jnp.concatenate([ref[pl.ds(j, rows, stride=8), :] for j in range(D // 128)], axis=1)


def _tile_of(ref, row8):
    return ref.at[pl.ds(pl.multiple_of(row8, 8), 8), :]


def _resident(shape, index_map):
    return pl.BlockSpec(shape, index_map, pipeline_mode=pl.Buffered(1))


def _mod_kernel(cond_ref, w_ref, b_ref, o_ref):
    c = cond_ref[...]
    o_ref[...] = jnp.dot(_silu(c), w_ref[...], precision=lax.Precision.HIGHEST,
                         preferred_element_type=F32) + b_ref[...]


def _modulation(cond8, w_mod, b_mod):
    tn = 1024
    out = pl.pallas_call(
        _mod_kernel,
        grid=(DEPTH, 6 * D // tn),
        in_specs=[
            pl.BlockSpec((8, D), lambda l, j: (0, 0)),
            pl.BlockSpec((None, D, tn), lambda l, j: (l, 0, j)),
            pl.BlockSpec((None, 1, tn), lambda l, j: (l, 0, j)),
        ],
        out_specs=pl.BlockSpec((None, 8, tn), lambda l, j: (l, 0, j)),
        out_shape=jax.ShapeDtypeStruct((DEPTH, 8, 6 * D), F32),
        compiler_params=_cp("arbitrary", "arbitrary"),
        name="adaln_mod",
    )(cond8, w_mod, b_mod.reshape(DEPTH, 1, 6 * D))
    return out.reshape(DEPTH, 8, 6, D)


def _even_kernel(x_ref, mod_ref, n1g_ref, win_ref, wgu_ref, bgu_ref, glag_ref, ws_ref, bs_ref,
                 wout_ref, s0_ref, xo_ref, st_ref, proj, la, o_f, o_b, st_scr, *, T):
    n_chunks = T // GLA_CHUNK
    shift, scale, gate = mod_ref[0:1, :], mod_ref[1:2, :], mod_ref[2:3, :]
    RB = 128

    def proj_body(r, carry):
        r0 = pl.multiple_of(r * RB, RB)
        h = _rms(x_ref[pl.ds(r0, RB), :], n1g_ref[...]) * (1.0 + scale) + shift
        p = _dot(h, win_ref[...])
        proj[pl.ds(r0, RB), :] = p
        z = _dot(p[:, C_A:C_A + 128], wgu_ref[...]) + bgu_ref[...]
        la[pl.ds(r0, RB), :] = _log_sigmoid(z) * (1.0 / GLA_TAU)
        return carry

    lax.fori_loop(0, T // RB, proj_body, 0)

    st_scr[0] = s0_ref[0].T
    st_scr[1] = s0_ref[1].T

    ci = lax.broadcasted_iota(jnp.int32, (GLA_CHUNK, GLA_CHUNK), 0)
    cj = lax.broadcasted_iota(jnp.int32, (GLA_CHUNK, GLA_CHUNK), 1)
    tri = (jnp.where(ci >= cj, 1.0, 0.0).astype(BF16), jnp.where(ci <= cj, 1.0, 0.0).astype(BF16))
    ai = lax.broadcasted_iota(jnp.int32, (GLA_HEADS * GLA_CHUNK, GLA_CHUNK), 0) % GLA_CHUNK
    aj = lax.broadcasted_iota(jnp.int32, (GLA_HEADS * GLA_CHUNK, GLA_CHUNK), 1)
    amask = (ai >= aj, ai <= aj)
    lane_head = lax.broadcasted_iota(jnp.int32, (1, QK_W), 1) // GLA_DK
    hmask = [jnp.where(lane_head == h, 1.0, 0.0) for h in range(GLA_HEADS)]

    def chunk_body(i, carry):
        for d in range(2):
            c = i if d == 0 else n_chunks - 1 - i
            r0 = pl.multiple_of(c * GLA_CHUNK, GLA_CHUNK)
            q = proj[pl.ds(r0, GLA_CHUNK), C_Q:C_Q + QK_W] * (GLA_DK ** -0.5)
            k = proj[pl.ds(r0, GLA_CHUNK), C_K:C_K + QK_W]
            v = proj[pl.ds(r0, GLA_CHUNK), C_V:C_V + V_W]
            lac = la[pl.ds(r0, GLA_CHUNK), d * QK_W:(d + 1) * QK_W]
            hi = lac.astype(BF16)
            lo = (lac - hi.astype(F32)).astype(BF16)
            b = (jnp.dot(tri[d], hi, preferred_element_type=F32)
                 + jnp.dot(tri[d], lo, preferred_element_type=F32))
            bend = b[GLA_CHUNK - 1:GLA_CHUNK, :] if d == 0 else b[0:1, :]
            qe = q * jnp.exp(b)
            ke = k * jnp.exp(-b)
            kd = k * jnp.exp(bend - b)
            st = st_scr[d]
            qstack = jnp.concatenate([qe * hmask[h] for h in range(GLA_HEADS)], axis=0).astype(BF16)
            att = jnp.where(amask[d], _dot_nt(qstack, ke), 0.0)
            inter = _dot_nt(qstack, st)
            outs = []
            for h in range(GLA_HEADS):
                rows = slice(h * GLA_CHUNK, (h + 1) * GLA_CHUNK)
                outs.append(_dot(att[rows], v[:, h * GLA_DV:(h + 1) * GLA_DV]) + inter[rows])
            o = jnp.concatenate(outs, axis=1)
            if d == 0:
                o_f[pl.ds(r0, GLA_CHUNK), :] = o
            else:
                o_b[pl.ds(r0, GLA_CHUNK), :] = o
            vstack = jnp.concatenate([v[:, h * GLA_DV:(h + 1) * GLA_DV] for h in range(GLA_HEADS)], axis=0)
            kstack = jnp.concatenate([kd * hmask[h] for h in range(GLA_HEADS)], axis=0)
            st_scr[d] = st * jnp.exp(bend) + _dot_tn(vstack, kstack)
        return carry

    lax.fori_loop(0, n_chunks, chunk_body, 0, unroll=2)
    st_ref[0] = st_scr[0].T
    st_ref[1] = st_scr[1].T

    def out_body(r, carry):
        r0 = pl.multiple_of(r * RB, RB)
        osum = o_f[pl.ds(r0, RB), :] + o_b[pl.ds(r0, RB), :]
        g = proj[pl.ds(r0, RB), C_G:C_G + V_W]
        u = proj[pl.ds(r0, RB), C_U:C_U + GMLP_W]
        vg = _gelu(proj[pl.ds(r0, RB), C_VG:C_VG + GMLP_W])
        parts = []
        for h in range(GLA_HEADS):
            oh = osum[:, h * GLA_DV:(h + 1) * GLA_DV]
            parts.append(_rms(oh, glag_ref[...]) * _silu(g[:, h * GLA_DV:(h + 1) * GLA_DV]))
        for gi in range(GMLP_GROUPS):
            vc = vg[:, gi * GMLP_DIM:(gi + 1) * GMLP_DIM]
            vc = vc - jnp.mean(vc, axis=-1, keepdims=True)
            vn = vc * lax.rsqrt(jnp.mean(vc * vc, axis=-1, keepdims=True) + EPS)
            sg = _dot(ws_ref[gi], vn) + bs_ref[:, gi:gi + 1]
            parts.append(_gelu(u[:, gi * GMLP_DIM:(gi + 1) * GMLP_DIM]) * sg)
        mix = jnp.concatenate(parts, axis=1)
        y = _dot(mix, wout_ref[...])
        xo_ref[pl.ds(r0, RB), :] = x_ref[pl.ds(r0, RB), :] + gate * y
        return carry

    lax.fori_loop(0, T // RB, out_body, 0)


def _even_mixer(x_all, mod_l, n1g, win, wgu, bgu, glag, ws, bs, wout, s0, *, latent, x_first=None):
    if latent:
        T, nseq, blk0 = DEC_SEQ, DEC_BATCH, N_CTX // DEC_SEQ
        cond = lambda i: 1 + i
        s0_spec = pl.BlockSpec((None, 2, QK_W, GLA_DV), lambda i: (i, 0, 0, 0))
    else:
        T, nseq, blk0 = SEQ, BATCH, 0
        cond = lambda i: 0
        s0_spec = pl.BlockSpec((None, 2, QK_W, GLA_DV), lambda i: (0, 0, 0, 0))
    const2 = lambda i: (0, 0)
    body = functools.partial(_even_kernel, T=T)
    x_spec = pl.BlockSpec((T, D), lambda i: (blk0 + i, 0))
    if x_first is None:
        lead_specs, lead_args, aliases = [x_spec], (x_all,), {0: 0}
    elif x_all is None:
        lead_specs, lead_args, aliases = [pl.BlockSpec((T, D), lambda i: (i, 0))], (x_first,), {}
    else:
        lead_specs = [pl.BlockSpec(memory_space=pl.ANY), pl.BlockSpec((T, D), lambda i: (i, 0))]
        lead_args, aliases = (x_all, x_first), {0: 0}
        body = lambda dst_ref, *refs: _even_kernel(*refs, T=T)
    x_new, states = pl.pallas_call(
        body,
        grid=(nseq,),
        in_specs=lead_specs + [
            pl.BlockSpec((None, 6, D), lambda i: (cond(i), 0, 0)),
            _resident((1, D), const2),
            _resident((D, EVEN_PACK), const2),
            _resident((128, 2 * QK_W), const2),
            _resident((1, 2 * QK_W), const2),
            _resident((1, GLA_DV), const2),
            _resident((GMLP_GROUPS, GMLP_CHUNK, GMLP_CHUNK), lambda i: (0, 0, 0)),
            _resident((GMLP_CHUNK, GMLP_GROUPS), const2),
            _resident((D, D), const2),
            s0_spec,
        ],
        out_specs=[
            x_spec,
            pl.BlockSpec((None, 2, QK_W, GLA_DV), lambda i: (i, 0, 0, 0)),
        ],
        out_shape=[
            jax.ShapeDtypeStruct((N_TOK, D), F32),
            jax.ShapeDtypeStruct((nseq, 2, QK_W, GLA_DV), F32),
        ],
        scratch_shapes=[
            pltpu.VMEM((T, EVEN_PACK), F32),
            pltpu.VMEM((T, 2 * QK_W), F32),
            pltpu.VMEM((T, V_W), F32),
            pltpu.VMEM((T, V_W), F32),
            pltpu.VMEM((2, GLA_DV, QK_W), F32),
        ],
        input_output_aliases=aliases,
        compiler_params=_cp("arbitrary"),
        name="even_mixer_latent" if latent else "even_mixer_context",
    )(*lead_args, mod_l, n1g, win, wgu, bgu, glag, ws, bs, wout, s0)
    return x_new, states


QKV_TB = 512


def _qkv_kernel(x_ref, mod_ref, n1g_ref, win_ref, gq_ref, gk_ref, cos_ref, sin_ref, q_ref, k_ref, v_ref):
    shift, scale = mod_ref[0:1, :], mod_ref[1:2, :]
    h = _rms(x_ref[...], n1g_ref[...]) * (1.0 + scale) + shift
    p = _dot(h, win_ref[...])
    cos, sin = cos_ref[...], sin_ref[...]
    even_lane = lax.broadcasted_iota(jnp.int32, (1, HD), 1) % 2 == 0

    def norm_rope(xh, g):
        xn = _rms(xh, g)
        swapped = jnp.where(even_lane, pltpu.roll(xn, HD - 1, axis=1), pltpu.roll(xn, 1, axis=1))
        return xn * cos + swapped * sin

    for hh in range(ATT_HEADS):
        q_ref[:, hh * HD:(hh + 1) * HD] = norm_rope(p[:, hh * HD:(hh + 1) * HD], gq_ref[...]).astype(BF16)
    for hh in range(ATT_KV):
        k_ref[:, hh * HD:(hh + 1) * HD] = norm_rope(p[:, Q_W + hh * HD:Q_W + (hh + 1) * HD], gk_ref[...])
    v_ref[...] = p[:, Q_W + KV_W:]


def _qkv(x_all, mod_l, n1g, win, gq, gk, cos_tab, sin_tab):
    nb_ctx = N_CTX // QKV_TB
    per_seq = DEC_SEQ // QKV_TB
    cond = lambda i: jnp.where(i < nb_ctx, 0, 1 + (i - nb_ctx) // per_seq)
    tab = lambda i: jnp.where(i < nb_ctx, 0, 1 + (i - nb_ctx) % per_seq)
    const2 = lambda i: (0, 0)
    return pl.pallas_call(
        _qkv_kernel,
        grid=(N_TOK // QKV_TB,),
        in_specs=[
            pl.BlockSpec((QKV_TB, D), lambda i: (i, 0)),
            pl.BlockSpec((None, 6, D), lambda i: (cond(i), 0, 0)),
            _resident((1, D), const2),
            _resident((D, Q_W + 2 * KV_W), const2),
            _resident((1, HD), const2),
            _resident((1, HD), const2),
            pl.BlockSpec((None, QKV_TB, HD), lambda i: (tab(i), 0, 0)),
            pl.BlockSpec((None, QKV_TB, HD), lambda i: (tab(i), 0, 0)),
        ],
        out_specs=[
            pl.BlockSpec((QKV_TB, Q_W), lambda i: (i, 0)),
            pl.BlockSpec((QKV_TB, KV_W), lambda i: (i, 0)),
            pl.BlockSpec((QKV_TB, KV_W), lambda i: (i, 0)),
        ],
        out_shape=[
            jax.ShapeDtypeStruct((N_TOK, Q_W), BF16),
            jax.ShapeDtypeStruct((N_TOK, KV_W), F32),
            jax.ShapeDtypeStruct((N_TOK, KV_W), F32),
        ],
        compiler_params=_cp("arbitrary"),
        name="odd_qkv",
    )(x_all, mod_l, n1g, win, gq, gk, cos_tab, sin_tab)


ATT_TQ = 256


def _attn_kernel(*refs, n_kv):
    q_ref = refs[0]
    kv_refs = refs[1:1 + 2 * n_kv]
    x_ref, mod_ref, wout_ref, xo_ref, att_scr = refs[1 + 2 * n_kv:]
    gate = mod_ref[2:3, :]
    for kh in range(ATT_KV):
        ks = [kv_refs[2 * s][:, kh * HD:(kh + 1) * HD].astype(BF16) for s in range(n_kv)]
        vs = [kv_refs[2 * s + 1][:, kh * HD:(kh + 1) * HD].astype(BF16) for s in range(n_kv)]
        for g in range(ATT_G):
            hh = kh * ATT_G + g
            qh = q_ref[:, hh * HD:(hh + 1) * HD]
            ss = [_dot_nt(qh, kk) * (HD ** -0.5) for kk in ks]
            m = ss[0].max(axis=-1, keepdims=True)
            for s in ss[1:]:
                m = jnp.maximum(m, s.max(axis=-1, keepdims=True))
            ps = [jnp.exp(s - m) for s in ss]
            den = ps[0].sum(axis=-1, keepdims=True)
            for p in ps[1:]:
                den = den + p.sum(axis=-1, keepdims=True)
            o = _dot(ps[0], vs[0])
            for p, vv in zip(ps[1:], vs[1:]):
                o = o + _dot(p, vv)
            att_scr[:, hh * HD:(hh + 1) * HD] = o / den
    y = _dot(att_scr[...], wout_ref[...])
    xo_ref[...] = x_ref[...] + gate * y


def _attention(x_all, mod_l, q, k, v, wout, cache_k=None, cache_v=None, layer_i=0):
    latent = cache_k is not None
    const2 = lambda *a: (0, 0)
    if latent:
        nq = DEC_SEQ // ATT_TQ
        row_blk = lambda b, j: (N_CTX // ATT_TQ + b * nq + j, 0)
        grid = (DEC_BATCH, nq)
        kv_specs = [
            pl.BlockSpec((None, None, SEQ, KV_W), lambda b, j: (b, layer_i, 0, 0)),
            pl.BlockSpec((None, None, SEQ, KV_W), lambda b, j: (b, layer_i, 0, 0)),
            pl.BlockSpec((DEC_SEQ, KV_W), lambda b, j: (N_CTX // DEC_SEQ + b, 0)),
            pl.BlockSpec((DEC_SEQ, KV_W), lambda b, j: (N_CTX // DEC_SEQ + b, 0)),
        ]
        kv_args = (cache_k, cache_v, k, v)
        mod_spec = pl.BlockSpec((None, 6, D), lambda b, j: (1 + b, 0, 0))
        sem = ("arbitrary", "arbitrary")
        n_kv = 2
    else:
        row_blk = lambda i: (i, 0)
        grid = (BATCH,)
        kv_specs = [pl.BlockSpec((SEQ, KV_W), row_blk), pl.BlockSpec((SEQ, KV_W), row_blk)]
        kv_args = (k, v)
        mod_spec = pl.BlockSpec((None, 6, D), lambda i: (0, 0, 0))
        sem = ("arbitrary",)
        n_kv = 1
    n_in = 1 + len(kv_args)
    return pl.pallas_call(
        functools.partial(_attn_kernel, n_kv=n_kv),
        grid=grid,
        in_specs=[pl.BlockSpec((ATT_TQ, Q_W), row_blk)] + kv_specs + [
            pl.BlockSpec((ATT_TQ, D), row_blk),
            mod_spec,
            _resident((D, D), const2),
        ],
        out_specs=pl.BlockSpec((ATT_TQ, D), row_blk),
        out_shape=jax.ShapeDtypeStruct((N_TOK, D), F32),
        scratch_shapes=[pltpu.VMEM((ATT_TQ, Q_W), F32)],
        input_output_aliases={n_in: 0},
        compiler_params=_cp(*sem),
        name="attention_latent" if latent else "attention_context",
    )(q, *kv_args, x_all, mod_l, wout)


ROUTE_TB = 512
HALF_TOK = N_TOK // 2
M_E1, M_E2, M_G1, M_G2, M_R1, M_R2 = 0, 1, 2, 3, 4, 5


def _router_kernel(x_ref, mod_ref, n2g_ref, whi_ref, wlo_ref, br_ref, h_ref, metat_ref, cnt_ref, run):
    @pl.when(pl.program_id(0) % (HALF_TOK // ROUTE_TB) == 0)
    def _():
        run[...] = jnp.zeros_like(run)

    shift, scale = mod_ref[3:4, :], mod_ref[4:5, :]
    h = _rms(x_ref[...], n2g_ref[...]) * (1.0 + scale) + shift
    _rows_to_tiles(h_ref, h)
    h_hi = h.astype(BF16)
    h_lo = (h - h_hi.astype(F32)).astype(BF16)
    dot = functools.partial(jnp.dot, preferred_element_type=F32)
    logits = dot(h_hi, whi_ref[...]) + dot(h_lo, whi_ref[...]) + dot(h_hi, wlo_ref[...]) + br_ref[...]
    lane = lax.broadcasted_iota(jnp.int32, logits.shape, 1).astype(F32)
    big = 1e4

    def first_argmax(vals):
        m = vals.max(axis=-1, keepdims=True)
        return m, jnp.where(vals == m, lane, big).min(axis=-1, keepdims=True)

    gl = jnp.where((lane >= N_EXP) & (lane < N_EXP + MOE_GROUPS), logits, NEG)
    gmax, glane = first_argmax(gl)
    g_p = 1.0 / jnp.exp(gl - gmax).sum(axis=-1, keepdims=True)
    lo = (glane - N_EXP) * MOE_PER_GROUP
    el = jnp.where((lane >= lo) & (lane < lo + MOE_PER_GROUP), logits, NEG)
    m1, i1 = first_argmax(el)
    m2, i2 = first_argmax(jnp.where(lane == i1, NEG, el))
    t = jnp.exp(m2 - m1)
    w1 = 1.0 / (1.0 + t)
    sel1, sel2 = lane == i1, lane == i2
    onehot = jnp.where(sel1 | sel2, 1.0, 0.0)
    ri = lax.broadcasted_iota(jnp.int32, (ROUTE_TB, ROUTE_TB), 0)
    rj = lax.broadcasted_iota(jnp.int32, (ROUTE_TB, ROUTE_TB), 1)
    before = _dot(jnp.where(ri > rj, 1.0, 0.0), onehot) + run[...]
    r1 = jnp.where(sel1, before, 0.0).sum(axis=-1, keepdims=True)
    r2 = jnp.where(sel2, before, 0.0).sum(axis=-1, keepdims=True)
    run[...] += onehot.sum(axis=0, keepdims=True)
    cnt_ref[...] = run[...]
    meta = jnp.zeros_like(logits)
    for j, val in enumerate([i1, i2, w1 * g_p, (t * w1) * g_p, r1, r2]):
        meta = jnp.where(lane == j, val, meta)
    metat_ref[...] = meta.T[0:8, :]


def _router(x_all, mod_l, n2g, wr, br):
    w_hi = wr.astype(BF16)
    wr_lo = (wr - w_hi.astype(F32)).astype(BF16)
    nb_ctx = N_CTX // ROUTE_TB
    per_seq = DEC_SEQ // ROUTE_TB
    cond = lambda i: jnp.where(i < nb_ctx, 0, 1 + (i - nb_ctx) // per_seq)
    const2 = lambda i: (0, 0)
    return pl.pallas_call(
        _router_kernel,
        grid=(N_TOK // ROUTE_TB,),
        in_specs=[
            pl.BlockSpec((ROUTE_TB, D), lambda i: (i, 0)),
            pl.BlockSpec((None, 6, D), lambda i: (cond(i), 0, 0)),
            _resident((1, D), const2),
            _resident((D, 128), const2),
            _resident((D, 128), const2),
            _resident((1, 128), const2),
        ],
        out_specs=[
            pl.BlockSpec((ROUTE_TB * 8, 128), lambda i: (i, 0)),
            pl.BlockSpec((8, ROUTE_TB), lambda i: (0, i)),
            pl.BlockSpec((None, 1, 128), lambda i: (i // (HALF_TOK // ROUTE_TB), 0, 0)),
        ],
        out_shape=[
            jax.ShapeDtypeStruct((N_TOK * 8, 128), F32),
            jax.ShapeDtypeStruct((8, N_TOK), F32),
            jax.ShapeDtypeStruct((2, 1, 128), F32),
        ],
        scratch_shapes=[pltpu.VMEM((1, 128), F32)],
        compiler_params=_cp("arbitrary"),
        name="moe_router",
    )(x_all, mod_l, n2g, w_hi, wr_lo, br)


EXP_TM = 128
N_ASSIGN = 2 * N_TOK
N_GROUPS = 2 * N_EXP
MAX_TILES = N_ASSIGN // EXP_TM + N_GROUPS
N_SORTED = MAX_TILES * EXP_TM
ORDER_BLK = 2048
CODE_PLANE = 2 * HALF_TOK
CODE_MASK = 8 * CODE_PLANE - 1
DUMMY8 = HALF_TOK * 8


def _order_kernel(pos1_ref, pos2_ref, pad_lo_ref, pad_hi_ref, src_ref):
    i = pl.program_id(0)
    local = (i % (HALF_TOK // ORDER_BLK)) * ORDER_BLK

    def body(t, carry):
        src_ref[pos1_ref[t]] = (local + t) * 8
        src_ref[pos2_ref[t]] = (local + t + CODE_PLANE) * 8
        return carry

    lax.fori_loop(0, ORDER_BLK, body, 0, unroll=16)

    @pl.when(i == 0)
    def _():
        def group(g, carry):
            def pad(p, c):
                src_ref[p] = DUMMY8
                return c
            return lax.fori_loop(pad_lo_ref[g], pad_hi_ref[g], pad, carry)

        lax.fori_loop(0, N_GROUPS, group, 0)


def _order(pos, pad_lo, pad_hi):
    return pl.pallas_call(
        _order_kernel,
        grid=(N_TOK // ORDER_BLK,),
        in_specs=[
            pl.BlockSpec((ORDER_BLK,), lambda i: (i,), memory_space=pltpu.SMEM),
            pl.BlockSpec((ORDER_BLK,), lambda i: (N_TOK // ORDER_BLK + i,), memory_space=pltpu.SMEM),
            pl.BlockSpec(memory_space=pltpu.SMEM),
            pl.BlockSpec(memory_space=pltpu.SMEM),
        ],
        out_specs=pl.BlockSpec(memory_space=pltpu.SMEM),
        out_shape=jax.ShapeDtypeStruct((N_SORTED,), jnp.int32),
        compiler_params=_cp("arbitrary"),
        name="moe_order",
    )(pos, pos, pad_lo, pad_hi)


GATE_BLK = CODE_PLANE + HALF_TOK
ACC_TOK = HALF_TOK + 64
ROW_GROUP = 8


def _experts_kernel(tile0_ref, ntile_ref, src_ref, gs_ref, h_hbm, wg_ref, wu_ref, wd_ref, out_hbm,
                    h_res, acc, xbuf, ybuf, wgb, wub, wdb, sem):
    group = pl.program_id(0)
    expert = group % N_EXP
    rows0 = pl.multiple_of((group // N_EXP) * (HALF_TOK * 8), 8)

    @pl.when(expert == 0)
    def _():
        cp = pltpu.make_async_copy(h_hbm.at[pl.ds(rows0, HALF_TOK * 8), :], h_res.at[pl.ds(0, HALF_TOK * 8), :], sem)
        cp.start()
        h_res[pl.ds(DUMMY8, 8), :] = jnp.zeros((8, 128), F32)

        def zero(i, carry):
            acc[pl.ds(pl.multiple_of(i * 512, 512), 512), :] = jnp.zeros((512, 128), F32)
            return carry

        lax.fori_loop(0, ACC_TOK * 8 // 512, zero, 0)
        cp.wait()

    n_tiles = ntile_ref[group]

    @pl.when(n_tiles > 0)
    def _():
        wgb[...] = wg_ref[...].astype(BF16)
        wub[...] = wu_ref[...].astype(BF16)
        wdb[...] = wd_ref[...].astype(BF16)

    def process(base, rows):
        def gather(g, c):
            for i in range(ROW_GROUP):
                r = g * ROW_GROUP + i
                xbuf[pl.ds(pl.multiple_of(r * 8, 8), 8), :] = _tile_of(h_res, src_ref[base + r] & CODE_MASK)[...]
            return c

        lax.fori_loop(0, rows // ROW_GROUP, gather, 0)
        x = _tiles_to_rows(xbuf, rows).astype(BF16)
        hid = _silu(_dot(x, wgb[...])) * _dot(x, wub[...])
        _rows_to_tiles(ybuf, _dot(hid, wdb[...]))

        def accumulate(g, c):
            targets, values = [], []
            for i in range(ROW_GROUP):
                r = g * ROW_GROUP + i
                code = src_ref[base + r]
                target = _tile_of(acc, code & CODE_MASK)
                targets.append(target)
                values.append(target[...] + gs_ref[code >> 3] * ybuf[pl.ds(pl.multiple_of(r * 8, 8), 8), :])
            for target, value in zip(targets, values):
                target[...] = value
            return c

        lax.fori_loop(0, rows // ROW_GROUP, accumulate, 0)

    row0 = tile0_ref[group] * EXP_TM

    def pair_body(j, carry):
        process(row0 + j * (2 * EXP_TM), 2 * EXP_TM)
        return carry

    lax.fori_loop(0, n_tiles // 2, pair_body, 0)

    @pl.when(n_tiles % 2 == 1)
    def _():
        process(row0 + (n_tiles - 1) * EXP_TM, EXP_TM)

    @pl.when(expert == N_EXP - 1)
    def _():
        cp = pltpu.make_async_copy(acc.at[pl.ds(0, HALF_TOK * 8), :], out_hbm.at[pl.ds(rows0, HALF_TOK * 8), :], sem)
        cp.start()
        cp.wait()


def _experts(tile0, n_tiles, src, gs, h, wg, wu, wd, layer):
    wmap = lambda g, t0, nt, src: (layer, g % N_EXP, 0, 0)
    return pl.pallas_call(
        _experts_kernel,
        grid_spec=pltpu.PrefetchScalarGridSpec(
            num_scalar_prefetch=3,
            grid=(N_GROUPS,),
            in_specs=[
                pl.BlockSpec((GATE_BLK,), lambda g, t0, nt, src: (g // N_EXP,), memory_space=pltpu.SMEM),
                pl.BlockSpec(memory_space=pl.ANY),
                pl.BlockSpec((None, None, D, D_EXP), wmap),
                pl.BlockSpec((None, None, D, D_EXP), wmap),
                pl.BlockSpec((None, None, D_EXP, D), wmap),
            ],
            out_specs=pl.BlockSpec(memory_space=pl.ANY),
            scratch_shapes=[
                pltpu.VMEM((ACC_TOK * 8, 128), F32),
                pltpu.VMEM((ACC_TOK * 8, 128), F32),
                pltpu.VMEM((2 * EXP_TM * 8, 128), F32),
                pltpu.VMEM((2 * EXP_TM * 8, 128), F32),
                pltpu.VMEM((D, D_EXP), BF16),
                pltpu.VMEM((D, D_EXP), BF16),
                pltpu.VMEM((D_EXP, D), BF16),
                pltpu.SemaphoreType.DMA,
            ],
        ),
        out_shape=jax.ShapeDtypeStruct((N_TOK * 8, 128), F32),
        compiler_params=_cp("arbitrary"),
        name="moe_experts",
    )(tile0, n_tiles, src, gs, h, wg, wu, wd)


COMBINE_TB = 256


def _combine_kernel(y_ref, x_ref, mod_ref, xo_ref):
    xo_ref[...] = x_ref[...] + mod_ref[5:6, :] * _tiles_to_rows(y_ref, COMBINE_TB)


def _combine(ys, x_all, mod_l):
    nb_ctx = N_CTX // COMBINE_TB
    per_seq = DEC_SEQ // COMBINE_TB
    cond = lambda i: jnp.where(i < nb_ctx, 0, 1 + (i - nb_ctx) // per_seq)
    return pl.pallas_call(
        _combine_kernel,
        grid=(N_TOK // COMBINE_TB,),
        in_specs=[
            pl.BlockSpec((COMBINE_TB * 8, 128), lambda i: (i, 0)),
            pl.BlockSpec((COMBINE_TB, D), lambda i: (i, 0)),
            pl.BlockSpec((None, 6, D), lambda i: (cond(i), 0, 0)),
        ],
        out_specs=pl.BlockSpec((COMBINE_TB, D), lambda i: (i, 0)),
        out_shape=jax.ShapeDtypeStruct((N_TOK, D), F32),
        compiler_params=_cp("arbitrary"),
        name="moe_combine",
    )(ys, x_all, mod_l)


def _moe(x_all, mod_l, n2g, wr, br, wg, wu, wd, layer):
    h, metat, cnt = _router(x_all, mod_l, n2g, wr, br)
    counts = cnt[:, 0, :N_EXP].astype(jnp.int32).reshape(N_GROUPS)
    padded = (counts + EXP_TM - 1) // EXP_TM * EXP_TM
    ends = jnp.cumsum(padded)
    offs = ends - padded
    rec = metat.astype(jnp.int32)
    half = (jnp.arange(N_TOK, dtype=jnp.int32) // HALF_TOK)[None, :]
    group = rec[M_E1:M_E2 + 1] + N_EXP * half
    is_group = group[None] == jnp.arange(N_GROUPS, dtype=jnp.int32)[:, None, None]
    pos = jnp.sum(jnp.where(is_group, offs[:, None, None], 0), axis=0) + rec[M_R1:M_R2 + 1]
    src = _order(pos.reshape(N_ASSIGN), offs + counts, ends)
    g12 = metat[M_G1:M_G2 + 1].reshape(2, 2, HALF_TOK)
    gates = jnp.concatenate([g12[0], jnp.zeros((2, CODE_PLANE - HALF_TOK), F32), g12[1]], axis=1)
    ys = _experts(offs // EXP_TM, padded // EXP_TM, src, gates.reshape(2 * GATE_BLK), h, wg, wu, wd, layer)
    return _combine(ys, x_all, mod_l)


def _final_kernel(x_ref, g_ref, o_ref):
    o_ref[...] = _rms(x_ref[...], g_ref[...])


def _final_norm(x_all, g, blk0, n_rows):
    tb = 512
    return pl.pallas_call(
        _final_kernel,
        grid=(n_rows // tb,),
        in_specs=[pl.BlockSpec((tb, D), lambda i: (blk0 + i, 0)), _resident((1, D), lambda i: (0, 0))],
        out_specs=pl.BlockSpec((tb, D), lambda i: (i, 0)),
        out_shape=jax.ShapeDtypeStruct((n_rows, D), F32),
        compiler_params=_cp("arbitrary"),
        name="final_norm",
    )(x_all, g)


def _rope_tables():
    pos = jnp.arange(DEC_SEQ)
    row = (pos // GRID_W).astype(F32)
    col = (pos % GRID_W).astype(F32)
    n_freq = HD // 4
    inv = ROPE_THETA ** (-jnp.arange(n_freq, dtype=F32) / n_freq)
    ang = jnp.concatenate([row[:, None] * inv, col[:, None] * inv], axis=-1)
    cos = jnp.repeat(jnp.cos(ang), 2, axis=-1)
    sin = jnp.repeat(jnp.sin(ang), 2, axis=-1) * jnp.tile(jnp.array([-1.0, 1.0], F32), HD // 2)
    nblk = DEC_SEQ // QKV_TB
    cos_tab = jnp.concatenate([jnp.ones((1, QKV_TB, HD), F32), cos.reshape(nblk, QKV_TB, HD)], axis=0)
    sin_tab = jnp.concatenate([jnp.zeros((1, QKV_TB, HD), F32), sin.reshape(nblk, QKV_TB, HD)], axis=0)
    return cos_tab, sin_tab


def kernel(x_prompt, x_sample, state_gla, cache_k, cache_v, c, c_ctx, w_mod, b_mod, norm1_g, norm2_g,
           w_in_even, w_gate_up, b_gate_up, gla_norm_g, w_spatial, b_spatial, w_out_even,
           w_in_odd, q_norm_g, k_norm_g, w_out_odd, w_router_group, b_router_group,
           w_router_expert, b_router_expert, w_exp_gate, w_exp_up, w_exp_down, final_norm_g):
    x_all = None
    cond8 = jnp.concatenate([c_ctx[None], c, jnp.zeros((3, D), F32)], axis=0)
    mod = _modulation(cond8, w_mod, b_mod)
    cos_tab, sin_tab = _rope_tables()
    zero_state = jnp.zeros((1, 2, QK_W, GLA_DV), F32)
    state_in = state_gla.reshape(DEC_BATCH, -1, 2, QK_W, GLA_DV)
    cache_k2 = cache_k.reshape(DEC_BATCH, -1, SEQ, KV_W)
    cache_v2 = cache_v.reshape(DEC_BATCH, -1, SEQ, KV_W)

    gla_states, ctx_k, ctx_v = [], [], []
    for l in range(DEPTH):
        i = l // 2
        n1g = norm1_g[l][None]
        if l % 2 == 0:
            w = w_in_even[i]
            win = jnp.concatenate([w[:, :1536], w[:, 1568:], w[:, 1536:1568], jnp.zeros((D, 96), F32)],
                                  axis=1).astype(BF16)
            wgu = jnp.zeros((128, 2 * QK_W), F32)
            wgu = wgu.at[0:GLA_RANK, 0:QK_W].set(w_gate_up[i, 0])
            wgu = wgu.at[GLA_RANK:2 * GLA_RANK, QK_W:].set(w_gate_up[i, 1]).astype(BF16)
            bgu = b_gate_up[i].reshape(1, 2 * QK_W)
            args = (mod[l], n1g, win, wgu, bgu, gla_norm_g[i][None], w_spatial[i].astype(BF16),
                    b_spatial[i].T, w_out_even[i].astype(BF16))
            first = l == 0
            x_all, st = _even_mixer(x_all, *args, zero_state, latent=False,
                                    x_first=x_prompt.reshape(N_CTX, D) if first else None)
            gla_states.append(st)
            x_all, _ = _even_mixer(x_all, *args, state_in[:, i], latent=True,
                                   x_first=x_sample.reshape(N_LAT, D) if first else None)
        else:
            q, k, v = _qkv(x_all, mod[l], n1g, w_in_odd[i].astype(BF16), q_norm_g[i][None],
                           k_norm_g[i][None], cos_tab, sin_tab)
            ctx_k.append(k[:N_CTX].reshape(BATCH, SEQ, ATT_KV, HD))
            ctx_v.append(v[:N_CTX].reshape(BATCH, SEQ, ATT_KV, HD))
            wout = w_out_odd[i].astype(BF16)
            x_all = _attention(x_all, mod[l], q, k, v, wout)
            x_all = _attention(x_all, mod[l], q, k, v, wout, cache_k2, cache_v2, layer_i=i)
        wr = jnp.concatenate([w_router_expert[l], w_router_group[l],
                              jnp.zeros((D, 128 - N_EXP - MOE_GROUPS), F32)], axis=1)
        br = jnp.concatenate([b_router_expert[l], b_router_group[l],
                              jnp.zeros((128 - N_EXP - MOE_GROUPS,), F32)])[None]
        x_all = _moe(x_all, mod[l], norm2_g[l][None], wr, br, w_exp_gate, w_exp_up, w_exp_down, l)

    fg = final_norm_g[None]
    y_prompt = _final_norm(x_all, fg, 0, N_CTX).reshape(BATCH, SEQ, D)
    y_sample = _final_norm(x_all, fg, N_CTX // 512, N_LAT).reshape(DEC_BATCH, DEC_SEQ, D)
    new_state = jnp.stack(gla_states, axis=1).reshape(BATCH, -1, 2, GLA_HEADS, GLA_DK, GLA_DV)
    return (y_prompt, y_sample, new_state, jnp.stack(ctx_k, axis=1), jnp.stack(ctx_v, axis=1))
```

```python
import functools

import jax
import jax.numpy as jnp
import numpy as np
from jax import lax
from jax.experimental import pallas as pl
from jax.experimental.pallas import tpu as pltpu

F32 = jnp.float32
BF16 = jnp.bfloat16

D = 1024
BATCH, SEQ = 16, 256
DEC_BATCH, DEC_SEQ = 4, 1024
N_CTX = BATCH * SEQ
N_LAT = DEC_BATCH * DEC_SEQ
N_TOK = N_CTX + N_LAT
DEPTH = 4
EPS = 1e-6
GRID_W = 64
ROPE_THETA = 10000.0

GLA_HEADS, GLA_DK, GLA_DV, GLA_RANK, GLA_CHUNK, GLA_TAU = 4, 64, 128, 16, 128, 16.0
QK_W = GLA_HEADS * GLA_DK
V_W = GLA_HEADS * GLA_DV
GMLP_GROUPS, GMLP_DIM, GMLP_CHUNK = 4, 128, 128
GMLP_W = GMLP_GROUPS * GMLP_DIM
C_Q, C_K, C_V, C_G, C_U, C_VG, C_A = 0, 256, 512, 1024, 1536, 2048, 2560
EVEN_PACK = 2688

ATT_HEADS, ATT_KV, HD = 8, 2, 128
ATT_G = ATT_HEADS // ATT_KV
Q_W = ATT_HEADS * HD
KV_W = ATT_KV * HD

MOE_GROUPS, MOE_PER_GROUP = 4, 8
N_EXP = MOE_GROUPS * MOE_PER_GROUP
D_EXP = D // 4
NEG = -1e30

VMEM_LIMIT = 56 * 1024 * 1024


def _cp(*sem):
    return pltpu.CompilerParams(dimension_semantics=sem, vmem_limit_bytes=VMEM_LIMIT)


def _dot(a, b):
    return jnp.dot(a.astype(BF16), b.astype(BF16), preferred_element_type=F32)


def _dot_nt(a, b):
    return lax.dot_general(a.astype(BF16), b.astype(BF16), (((1,), (1,)), ((), ())),
                           preferred_element_type=F32)


def _dot_tn(a, b):
    return lax.dot_general(a.astype(BF16), b.astype(BF16), (((0,), (0,)), ((), ())),
                           preferred_element_type=F32)


def _rms(x, g):
    return x * lax.rsqrt(jnp.mean(x * x, axis=-1, keepdims=True) + EPS) * g


def _silu(x):
    return x * jax.nn.sigmoid(x)


def _gelu(x):
    return 0.5 * x * (1.0 + jnp.tanh(np.sqrt(2.0 / np.pi).astype(np.float32) * (x + 0.044715 * (x * x * x))))


def _log_sigmoid(z):
    return jnp.minimum(z, 0.0) - jnp.log(1.0 + jnp.exp(-jnp.abs(z)))


def _rows_to_tiles(ref, x):
    rows = x.shape[0]
    for j in range(D // 128):
        ref[pl.ds(j, rows, stride=8), :] = x[:, j * 128:(j + 1) * 128]


def _tiles_to_rows(ref, rows):
    return jnp.concatenate([ref[pl.ds(j, rows, stride=8), :] for j in range(D // 128)], axis=1)


def _tile_of(ref, row8):
    return ref.at[pl.ds(pl.multiple_of(row8, 8), 8), :]


def _resident(shape, index_map):
    return pl.BlockSpec(shape, index_map, pipeline_mode=pl.Buffered(1))


def _mod_kernel(cond_ref, w_ref, b_ref, o_ref):
    c = cond_ref[...]
    o_ref[...] = jnp.dot(_silu(c), w_ref[...], precision=lax.Precision.HIGHEST,
                         preferred_element_type=F32) + b_ref[...]


def _modulation(cond8, w_mod, b_mod):
    tn = 1024
    out = pl.pallas_call(
        _mod_kernel,
        grid=(DEPTH, 6 * D // tn),
        in_specs=[
            pl.BlockSpec((8, D), lambda l, j: (0, 0)),
            pl.BlockSpec((None, D, tn), lambda l, j: (l, 0, j)),
            pl.BlockSpec((None, 1, tn), lambda l, j: (l, 0, j)),
        ],
        out_specs=pl.BlockSpec((None, 8, tn), lambda l, j: (l, 0, j)),
        out_shape=jax.ShapeDtypeStruct((DEPTH, 8, 6 * D), F32),
        compiler_params=_cp("arbitrary", "arbitrary"),
        name="adaln_mod",
    )(cond8, w_mod, b_mod.reshape(DEPTH, 1, 6 * D))
    return out.reshape(DEPTH, 8, 6, D)


def _even_kernel(x_ref, mod_ref, n1g_ref, win_ref, wgu_ref, bgu_ref, glag_ref, ws_ref, bs_ref,
                 wout_ref, s0_ref, xo_ref, st_ref, proj, la, o_f, o_b, st_scr, *, T):
    n_chunks = T // GLA_CHUNK
    shift, scale, gate = mod_ref[0:1, :], mod_ref[1:2, :], mod_ref[2:3, :]
    RB = 128

    def proj_body(r, carry):
        r0 = pl.multiple_of(r * RB, RB)
        h = _rms(x_ref[pl.ds(r0, RB), :], n1g_ref[...]) * (1.0 + scale) + shift
        p = _dot(h, win_ref[...])
        proj[pl.ds(r0, RB), :] = p
        z = _dot(p[:, C_A:C_A + 128], wgu_ref[...]) + bgu_ref[...]
        la[pl.ds(r0, RB), :] = _log_sigmoid(z) * (1.0 / GLA_TAU)
        return carry

    lax.fori_loop(0, T // RB, proj_body, 0)

    st_scr[0] = s0_ref[0].T
    st_scr[1] = s0_ref[1].T

    ci = lax.broadcasted_iota(jnp.int32, (GLA_CHUNK, GLA_CHUNK), 0)
    cj = lax.broadcasted_iota(jnp.int32, (GLA_CHUNK, GLA_CHUNK), 1)
    tri = (jnp.where(ci >= cj, 1.0, 0.0).astype(BF16), jnp.where(ci <= cj, 1.0, 0.0).astype(BF16))
    ai = lax.broadcasted_iota(jnp.int32, (GLA_HEADS * GLA_CHUNK, GLA_CHUNK), 0) % GLA_CHUNK
    aj = lax.broadcasted_iota(jnp.int32, (GLA_HEADS * GLA_CHUNK, GLA_CHUNK), 1)
    amask = (ai >= aj, ai <= aj)
    lane_head = lax.broadcasted_iota(jnp.int32, (1, QK_W), 1) // GLA_DK
    hmask = [jnp.where(lane_head == h, 1.0, 0.0) for h in range(GLA_HEADS)]

    def chunk_body(i, carry):
        for d in range(2):
            c = i if d == 0 else n_chunks - 1 - i
            r0 = pl.multiple_of(c * GLA_CHUNK, GLA_CHUNK)
            q = proj[pl.ds(r0, GLA_CHUNK), C_Q:C_Q + QK_W] * (GLA_DK ** -0.5)
            k = proj[pl.ds(r0, GLA_CHUNK), C_K:C_K + QK_W]
            v = proj[pl.ds(r0, GLA_CHUNK), C_V:C_V + V_W]
            lac = la[pl.ds(r0, GLA_CHUNK), d * QK_W:(d + 1) * QK_W]
            hi = lac.astype(BF16)
            lo = (lac - hi.astype(F32)).astype(BF16)
            b = (jnp.dot(tri[d], hi, preferred_element_type=F32)
                 + jnp.dot(tri[d], lo, preferred_element_type=F32))
            bend = b[GLA_CHUNK - 1:GLA_CHUNK, :] if d == 0 else b[0:1, :]
            qe = q * jnp.exp(b)
            ke = k * jnp.exp(-b)
            kd = k * jnp.exp(bend - b)
            st = st_scr[d]
            qstack = jnp.concatenate([qe * hmask[h] for h in range(GLA_HEADS)], axis=0).astype(BF16)
            att = jnp.where(amask[d], _dot_nt(qstack, ke), 0.0)
            inter = _dot_nt(qstack, st)
            outs = []
            for h in range(GLA_HEADS):
                rows = slice(h * GLA_CHUNK, (h + 1) * GLA_CHUNK)
                outs.append(_dot(att[rows], v[:, h * GLA_DV:(h + 1) * GLA_DV]) + inter[rows])
            o = jnp.concatenate(outs, axis=1)
            if d == 0:
                o_f[pl.ds(r0, GLA_CHUNK), :] = o
            else:
                o_b[pl.ds(r0, GLA_CHUNK), :] = o
            vstack = jnp.concatenate([v[:, h * GLA_DV:(h + 1) * GLA_DV] for h in range(GLA_HEADS)], axis=0)
            kstack = jnp.concatenate([kd * hmask[h] for h in range(GLA_HEADS)], axis=0)
            st_scr[d] = st * jnp.exp(bend) + _dot_tn(vstack, kstack)
        return carry

    lax.fori_loop(0, n_chunks, chunk_body, 0, unroll=2)
    st_ref[0] = st_scr[0].T
    st_ref[1] = st_scr[1].T

    def out_body(r, carry):
        r0 = pl.multiple_of(r * RB, RB)
        osum = o_f[pl.ds(r0, RB), :] + o_b[pl.ds(r0, RB), :]
        g = proj[pl.ds(r0, RB), C_G:C_G + V_W]
        u = proj[pl.ds(r0, RB), C_U:C_U + GMLP_W]
        vg = _gelu(proj[pl.ds(r0, RB), C_VG:C_VG + GMLP_W])
        parts = []
        for h in range(GLA_HEADS):
            oh = osum[:, h * GLA_DV:(h + 1) * GLA_DV]
            parts.append(_rms(oh, glag_ref[...]) * _silu(g[:, h * GLA_DV:(h + 1) * GLA_DV]))
        for gi in range(GMLP_GROUPS):
            vc = vg[:, gi * GMLP_DIM:(gi + 1) * GMLP_DIM]
            vc = vc - jnp.mean(vc, axis=-1, keepdims=True)
            vn = vc * lax.rsqrt(jnp.mean(vc * vc, axis=-1, keepdims=True) + EPS)
            sg = _dot(ws_ref[gi], vn) + bs_ref[:, gi:gi + 1]
            parts.append(_gelu(u[:, gi * GMLP_DIM:(gi + 1) * GMLP_DIM]) * sg)
        mix = jnp.concatenate(parts, axis=1)
        y = _dot(mix, wout_ref[...])
        xo_ref[pl.ds(r0, RB), :] = x_ref[pl.ds(r0, RB), :] + gate * y
        return carry

    lax.fori_loop(0, T // RB, out_body, 0)


def _even_mixer(x_all, mod_l, n1g, win, wgu, bgu, glag, ws, bs, wout, s0, *, latent, x_first=None):
    if latent:
        T, nseq, blk0 = DEC_SEQ, DEC_BATCH, N_CTX // DEC_SEQ
        cond = lambda i: 1 + i
        s0_spec = pl.BlockSpec((None, 2, QK_W, GLA_DV), lambda i: (i, 0, 0, 0))
    else:
        T, nseq, blk0 = SEQ, BATCH, 0
        cond = lambda i: 0
        s0_spec = pl.BlockSpec((None, 2, QK_W, GLA_DV), lambda i: (0, 0, 0, 0))
    const2 = lambda i: (0, 0)
    body = functools.partial(_even_kernel, T=T)
    x_spec = pl.BlockSpec((T, D), lambda i: (blk0 + i, 0))
    if x_first is None:
        lead_specs, lead_args, aliases = [x_spec], (x_all,), {0: 0}
    elif x_all is None:
        lead_specs, lead_args, aliases = [pl.BlockSpec((T, D), lambda i: (i, 0))], (x_first,), {}
    else:
        lead_specs = [pl.BlockSpec(memory_space=pl.ANY), pl.BlockSpec((T, D), lambda i: (i, 0))]
        lead_args, aliases = (x_all, x_first), {0: 0}
        body = lambda dst_ref, *refs: _even_kernel(*refs, T=T)
    x_new, states = pl.pallas_call(
        body,
        grid=(nseq,),
        in_specs=lead_specs + [
            pl.BlockSpec((None, 6, D), lambda i: (cond(i), 0, 0)),
            _resident((1, D), const2),
            _resident((D, EVEN_PACK), const2),
            _resident((128, 2 * QK_W), const2),
            _resident((1, 2 * QK_W), const2),
            _resident((1, GLA_DV), const2),
            _resident((GMLP_GROUPS, GMLP_CHUNK, GMLP_CHUNK), lambda i: (0, 0, 0)),
            _resident((GMLP_CHUNK, GMLP_GROUPS), const2),
            _resident((D, D), const2),
            s0_spec,
        ],
        out_specs=[
            x_spec,
            pl.BlockSpec((None, 2, QK_W, GLA_DV), lambda i: (i, 0, 0, 0)),
        ],
        out_shape=[
            jax.ShapeDtypeStruct((N_TOK, D), F32),
            jax.ShapeDtypeStruct((nseq, 2, QK_W, GLA_DV), F32),
        ],
        scratch_shapes=[
            pltpu.VMEM((T, EVEN_PACK), F32),
            pltpu.VMEM((T, 2 * QK_W), F32),
            pltpu.VMEM((T, V_W), F32),
            pltpu.VMEM((T, V_W), F32),
            pltpu.VMEM((2, GLA_DV, QK_W), F32),
        ],
        input_output_aliases=aliases,
        compiler_params=_cp("arbitrary"),
        name="even_mixer_latent" if latent else "even_mixer_context",
    )(*lead_args, mod_l, n1g, win, wgu, bgu, glag, ws, bs, wout, s0)
    return x_new, states


QKV_TB = 512


def _qkv_kernel(x_ref, mod_ref, n1g_ref, win_ref, gq_ref, gk_ref, cos_ref, sin_ref, q_ref, k_ref, v_ref):
    shift, scale = mod_ref[0:1, :], mod_ref[1:2, :]
    h = _rms(x_ref[...], n1g_ref[...]) * (1.0 + scale) + shift
    p = _dot(h, win_ref[...])
    cos, sin = cos_ref[...], sin_ref[...]
    even_lane = lax.broadcasted_iota(jnp.int32, (1, HD), 1) % 2 == 0

    def norm_rope(xh, g):
        xn = _rms(xh, g)
        swapped = jnp.where(even_lane, pltpu.roll(xn, HD - 1, axis=1), pltpu.roll(xn, 1, axis=1))
        return xn * cos + swapped * sin

    for hh in range(ATT_HEADS):
        q_ref[:, hh * HD:(hh + 1) * HD] = norm_rope(p[:, hh * HD:(hh + 1) * HD], gq_ref[...]).astype(BF16)
    for hh in range(ATT_KV):
        k_ref[:, hh * HD:(hh + 1) * HD] = norm_rope(p[:, Q_W + hh * HD:Q_W + (hh + 1) * HD], gk_ref[...])
    v_ref[...] = p[:, Q_W + KV_W:]


def _qkv(x_all, mod_l, n1g, win, gq, gk, cos_tab, sin_tab):
    nb_ctx = N_CTX // QKV_TB
    per_seq = DEC_SEQ // QKV_TB
    cond = lambda i: jnp.where(i < nb_ctx, 0, 1 + (i - nb_ctx) // per_seq)
    tab = lambda i: jnp.where(i < nb_ctx, 0, 1 + (i - nb_ctx) % per_seq)
    const2 = lambda i: (0, 0)
    return pl.pallas_call(
        _qkv_kernel,
        grid=(N_TOK // QKV_TB,),
        in_specs=[
            pl.BlockSpec((QKV_TB, D), lambda i: (i, 0)),
            pl.BlockSpec((None, 6, D), lambda i: (cond(i), 0, 0)),
            _resident((1, D), const2),
            _resident((D, Q_W + 2 * KV_W), const2),
            _resident((1, HD), const2),
            _resident((1, HD), const2),
            pl.BlockSpec((None, QKV_TB, HD), lambda i: (tab(i), 0, 0)),
            pl.BlockSpec((None, QKV_TB, HD), lambda i: (tab(i), 0, 0)),
        ],
        out_specs=[
            pl.BlockSpec((QKV_TB, Q_W), lambda i: (i, 0)),
            pl.BlockSpec((QKV_TB, KV_W), lambda i: (i, 0)),
            pl.BlockSpec((QKV_TB, KV_W), lambda i: (i, 0)),
        ],
        out_shape=[
            jax.ShapeDtypeStruct((N_TOK, Q_W), BF16),
            jax.ShapeDtypeStruct((N_TOK, KV_W), F32),
            jax.ShapeDtypeStruct((N_TOK, KV_W), F32),
        ],
        compiler_params=_cp("arbitrary"),
        name="odd_qkv",
    )(x_all, mod_l, n1g, win, gq, gk, cos_tab, sin_tab)


ATT_TQ = 256


def _attn_kernel(*refs, n_kv):
    q_ref = refs[0]
    kv_refs = refs[1:1 + 2 * n_kv]
    x_ref, mod_ref, wout_ref, xo_ref, att_scr = refs[1 + 2 * n_kv:]
    gate = mod_ref[2:3, :]
    for kh in range(ATT_KV):
        ks = [kv_refs[2 * s][:, kh * HD:(kh + 1) * HD].astype(BF16) for s in range(n_kv)]
        vs = [kv_refs[2 * s + 1][:, kh * HD:(kh + 1) * HD].astype(BF16) for s in range(n_kv)]
        for g in range(ATT_G):
            hh = kh * ATT_G + g
            qh = q_ref[:, hh * HD:(hh + 1) * HD]
            ss = [_dot_nt(qh, kk) * (HD ** -0.5) for kk in ks]
            m = ss[0].max(axis=-1, keepdims=True)
            for s in ss[1:]:
                m = jnp.maximum(m, s.max(axis=-1, keepdims=True))
            ps = [jnp.exp(s - m) for s in ss]
            den = ps[0].sum(axis=-1, keepdims=True)
            for p in ps[1:]:
                den = den + p.sum(axis=-1, keepdims=True)
            o = _dot(ps[0], vs[0])
            for p, vv in zip(ps[1:], vs[1:]):
                o = o + _dot(p, vv)
            att_scr[:, hh * HD:(hh + 1) * HD] = o / den
    y = _dot(att_scr[...], wout_ref[...])
    xo_ref[...] = x_ref[...] + gate * y


def _attention(x_all, mod_l, q, k, v, wout, cache_k=None, cache_v=None, layer_i=0):
    latent = cache_k is not None
    const2 = lambda *a: (0, 0)
    if latent:
        nq = DEC_SEQ // ATT_TQ
        row_blk = lambda b, j: (N_CTX // ATT_TQ + b * nq + j, 0)
        grid = (DEC_BATCH, nq)
        kv_specs = [
            pl.BlockSpec((None, None, SEQ, KV_W), lambda b, j: (b, layer_i, 0, 0)),
            pl.BlockSpec((None, None, SEQ, KV_W), lambda b, j: (b, layer_i, 0, 0)),
            pl.BlockSpec((DEC_SEQ, KV_W), lambda b, j: (N_CTX // DEC_SEQ + b, 0)),
            pl.BlockSpec((DEC_SEQ, KV_W), lambda b, j: (N_CTX // DEC_SEQ + b, 0)),
        ]
        kv_args = (cache_k, cache_v, k, v)
        mod_spec = pl.BlockSpec((None, 6, D), lambda b, j: (1 + b, 0, 0))
        sem = ("arbitrary", "arbitrary")
        n_kv = 2
    else:
        row_blk = lambda i: (i, 0)
        grid = (BATCH,)
        kv_specs = [pl.BlockSpec((SEQ, KV_W), row_blk), pl.BlockSpec((SEQ, KV_W), row_blk)]
        kv_args = (k, v)
        mod_spec = pl.BlockSpec((None, 6, D), lambda i: (0, 0, 0))
        sem = ("arbitrary",)
        n_kv = 1
    n_in = 1 + len(kv_args)
    return pl.pallas_call(
        functools.partial(_attn_kernel, n_kv=n_kv),
        grid=grid,
        in_specs=[pl.BlockSpec((ATT_TQ, Q_W), row_blk)] + kv_specs + [
            pl.BlockSpec((ATT_TQ, D), row_blk),
            mod_spec,
            _resident((D, D), const2),
        ],
        out_specs=pl.BlockSpec((ATT_TQ, D), row_blk),
        out_shape=jax.ShapeDtypeStruct((N_TOK, D), F32),
        scratch_shapes=[pltpu.VMEM((ATT_TQ, Q_W), F32)],
        input_output_aliases={n_in: 0},
        compiler_params=_cp(*sem),
        name="attention_latent" if latent else "attention_context",
    )(q, *kv_args, x_all, mod_l, wout)


ROUTE_TB = 512
HALF_TOK = N_TOK // 2
M_E1, M_E2, M_G1, M_G2, M_R1, M_R2 = 0, 1, 2, 3, 4, 5


def _router_kernel(x_ref, mod_ref, n2g_ref, whi_ref, wlo_ref, br_ref, h_ref, metat_ref, cnt_ref, run):
    @pl.when(pl.program_id(0) % (HALF_TOK // ROUTE_TB) == 0)
    def _():
        run[...] = jnp.zeros_like(run)

    shift, scale = mod_ref[3:4, :], mod_ref[4:5, :]
    h = _rms(x_ref[...], n2g_ref[...]) * (1.0 + scale) + shift
    _rows_to_tiles(h_ref, h)
    h_hi = h.astype(BF16)
    h_lo = (h - h_hi.astype(F32)).astype(BF16)
    dot = functools.partial(jnp.dot, preferred_element_type=F32)
    logits = dot(h_hi, whi_ref[...]) + dot(h_lo, whi_ref[...]) + dot(h_hi, wlo_ref[...]) + br_ref[...]
    lane = lax.broadcasted_iota(jnp.int32, logits.shape, 1).astype(F32)
    big = 1e4

    def first_argmax(vals):
        m = vals.max(axis=-1, keepdims=True)
        return m, jnp.where(vals == m, lane, big).min(axis=-1, keepdims=True)

    gl = jnp.where((lane >= N_EXP) & (lane < N_EXP + MOE_GROUPS), logits, NEG)
    gmax, glane = first_argmax(gl)
    g_p = 1.0 / jnp.exp(gl - gmax).sum(axis=-1, keepdims=True)
    lo = (glane - N_EXP) * MOE_PER_GROUP
    el = jnp.where((lane >= lo) & (lane < lo + MOE_PER_GROUP), logits, NEG)
    m1, i1 = first_argmax(el)
    m2, i2 = first_argmax(jnp.where(lane == i1, NEG, el))
    t = jnp.exp(m2 - m1)
    w1 = 1.0 / (1.0 + t)
    sel1, sel2 = lane == i1, lane == i2
    onehot = jnp.where(sel1 | sel2, 1.0, 0.0)
    ri = lax.broadcasted_iota(jnp.int32, (ROUTE_TB, ROUTE_TB), 0)
    rj = lax.broadcasted_iota(jnp.int32, (ROUTE_TB, ROUTE_TB), 1)
    before = _dot(jnp.where(ri > rj, 1.0, 0.0), onehot) + run[...]
    r1 = jnp.where(sel1, before, 0.0).sum(axis=-1, keepdims=True)
    r2 = jnp.where(sel2, before, 0.0).sum(axis=-1, keepdims=True)
    run[...] += onehot.sum(axis=0, keepdims=True)
    cnt_ref[...] = run[...]
    meta = jnp.zeros_like(logits)
    for j, val in enumerate([i1, i2, w1 * g_p, (t * w1) * g_p, r1, r2]):
        meta = jnp.where(lane == j, val, meta)
    metat_ref[...] = meta.T[0:8, :]


def _router(x_all, mod_l, n2g, wr, br):
    w_hi = wr.astype(BF16)
    wr_lo = (wr - w_hi.astype(F32)).astype(BF16)
    nb_ctx = N_CTX // ROUTE_TB
    per_seq = DEC_SEQ // ROUTE_TB
    cond = lambda i: jnp.where(i < nb_ctx, 0, 1 + (i - nb_ctx) // per_seq)
    const2 = lambda i: (0, 0)
    return pl.pallas_call(
        _router_kernel,
        grid=(N_TOK // ROUTE_TB,),
        in_specs=[
            pl.BlockSpec((ROUTE_TB, D), lambda i: (i, 0)),
            pl.BlockSpec((None, 6, D), lambda i: (cond(i), 0, 0)),
            _resident((1, D), const2),
            _resident((D, 128), const2),
            _resident((D, 128), const2),
            _resident((1, 128), const2),
        ],
        out_specs=[
            pl.BlockSpec((ROUTE_TB * 8, 128), lambda i: (i, 0)),
            pl.BlockSpec((8, ROUTE_TB), lambda i: (0, i)),
            pl.BlockSpec((None, 1, 128), lambda i: (i // (HALF_TOK // ROUTE_TB), 0, 0)),
        ],
        out_shape=[
            jax.ShapeDtypeStruct((N_TOK * 8, 128), F32),
            jax.ShapeDtypeStruct((8, N_TOK), F32),
            jax.ShapeDtypeStruct((2, 1, 128), F32),
        ],
        scratch_shapes=[pltpu.VMEM((1, 128), F32)],
        compiler_params=_cp("arbitrary"),
        name="moe_router",
    )(x_all, mod_l, n2g, w_hi, wr_lo, br)


EXP_TM = 128
N_ASSIGN = 2 * N_TOK
N_GROUPS = 2 * N_EXP
MAX_TILES = N_ASSIGN // EXP_TM + N_GROUPS
N_SORTED = MAX_TILES * EXP_TM
ORDER_BLK = 2048
CODE_PLANE = 2 * HALF_TOK
CODE_MASK = 8 * CODE_PLANE - 1
DUMMY8 = HALF_TOK * 8


def _order_kernel(pos1_ref, pos2_ref, pad_lo_ref, pad_hi_ref, src_ref):
    i = pl.program_id(0)
    local = (i % (HALF_TOK // ORDER_BLK)) * ORDER_BLK

    def body(t, carry):
        src_ref[pos1_ref[t]] = (local + t) * 8
        src_ref[pos2_ref[t]] = (local + t + CODE_PLANE) * 8
        return carry

    lax.fori_loop(0, ORDER_BLK, body, 0, unroll=16)

    @pl.when(i == 0)
    def _():
        def group(g, carry):
            def pad(p, c):
                src_ref[p] = DUMMY8
                return c
            return lax.fori_loop(pad_lo_ref[g], pad_hi_ref[g], pad, carry)

        lax.fori_loop(0, N_GROUPS, group, 0)


def _order(pos, pad_lo, pad_hi):
    return pl.pallas_call(
        _order_kernel,
        grid=(N_TOK // ORDER_BLK,),
        in_specs=[
            pl.BlockSpec((ORDER_BLK,), lambda i: (i,), memory_space=pltpu.SMEM),
            pl.BlockSpec((ORDER_BLK,), lambda i: (N_TOK // ORDER_BLK + i,), memory_space=pltpu.SMEM),
            pl.BlockSpec(memory_space=pltpu.SMEM),
            pl.BlockSpec(memory_space=pltpu.SMEM),
        ],
        out_specs=pl.BlockSpec(memory_space=pltpu.SMEM),
        out_shape=jax.ShapeDtypeStruct((N_SORTED,), jnp.int32),
        compiler_params=_cp("arbitrary"),
        name="moe_order",
    )(pos, pos, pad_lo, pad_hi)


GATE_BLK = CODE_PLANE + HALF_TOK
ACC_TOK = HALF_TOK + 64
ROW_GROUP = 8


RES_TB = 256


def _experts_kernel(tile0_ref, ntile_ref, src_ref, gs_ref, h_hbm, x_hbm, mod_ref, wg_ref, wu_ref, wd_ref, xo_hbm,
                    h_res, acc, xbuf, ybuf, wgb, wub, wdb, xin, xout, sem, in_sem, out_sem):
    group = pl.program_id(0)
    expert = group % N_EXP
    half = group // N_EXP
    rows0 = pl.multiple_of(half * (HALF_TOK * 8), 8)

    @pl.when(expert == 0)
    def _():
        cp = pltpu.make_async_copy(h_hbm.at[pl.ds(rows0, HALF_TOK * 8), :], h_res.at[pl.ds(0, HALF_TOK * 8), :], sem)
        cp.start()
        h_res[pl.ds(DUMMY8, 8), :] = jnp.zeros((8, 128), F32)

        def zero(i, carry):
            acc[pl.ds(pl.multiple_of(i * 512, 512), 512), :] = jnp.zeros((512, 128), F32)
            return carry

        lax.fori_loop(0, ACC_TOK * 8 // 512, zero, 0)
        cp.wait()

    n_tiles = ntile_ref[group]

    @pl.when(n_tiles > 0)
    def _():
        wgb[...] = wg_ref[...].astype(BF16)
        wub[...] = wu_ref[...].astype(BF16)
        wdb[...] = wd_ref[...].astype(BF16)

    def process(base, rows):
        def gather(g, c):
            for i in range(ROW_GROUP):
                r = g * ROW_GROUP + i
                xbuf[pl.ds(pl.multiple_of(r * 8, 8), 8), :] = _tile_of(h_res, src_ref[base + r] & CODE_MASK)[...]
            return c

        lax.fori_loop(0, rows // ROW_GROUP, gather, 0)
        x = _tiles_to_rows(xbuf, rows).astype(BF16)
        hid = _silu(_dot(x, wgb[...])) * _dot(x, wub[...])
        _rows_to_tiles(ybuf, _dot(hid, wdb[...]))

        def accumulate(g, c):
            targets, values = [], []
            for i in range(ROW_GROUP):
                r = g * ROW_GROUP + i
                code = src_ref[base + r]
                target = _tile_of(acc, code & CODE_MASK)
                targets.append(target)
                values.append(target[...] + gs_ref[code >> 3] * ybuf[pl.ds(pl.multiple_of(r * 8, 8), 8), :])
            for target, value in zip(targets, values):
                target[...] = value
            return c

        lax.fori_loop(0, rows // ROW_GROUP, accumulate, 0)

    row0 = tile0_ref[group] * EXP_TM

    def pair_body(j, carry):
        process(row0 + j * (2 * EXP_TM), 2 * EXP_TM)
        return carry

    lax.fori_loop(0, n_tiles // 2, pair_body, 0)

    @pl.when(n_tiles % 2 == 1)
    def _():
        process(row0 + (n_tiles - 1) * EXP_TM, EXP_TM)

    def x_rows(blk):
        return pl.ds(pl.multiple_of(half * HALF_TOK + blk * RES_TB, RES_TB), RES_TB)

    def load_x(blk, slot):
        return pltpu.make_async_copy(x_hbm.at[x_rows(blk), :], xin.at[slot], in_sem.at[slot])

    def store_x(blk, slot):
        return pltpu.make_async_copy(xout.at[slot], xo_hbm.at[x_rows(blk), :], out_sem.at[slot])

    def residual(blk, slot):
        cond = jnp.where(half == 0, 0, 1 + blk // (DEC_SEQ // RES_TB))
        gate = mod_ref[cond, 5:6, :]
        y = _tiles_to_rows(acc.at[pl.ds(pl.multiple_of(blk * (RES_TB * 8), RES_TB * 8), RES_TB * 8), :], RES_TB)
        xout[slot] = xin[slot] + gate * y

    @pl.when(expert == N_EXP - 1)
    def _():
        n_pairs = HALF_TOK // RES_TB // 2
        load_x(0, 0).start()

        def pair(p, carry):
            for slot in range(2):
                blk = 2 * p + slot
                if slot == 0:
                    load_x(blk + 1, 1).start()
                else:
                    @pl.when(p + 1 < n_pairs)
                    def _():
                        load_x(blk + 1, 0).start()
                load_x(blk, slot).wait()

                @pl.when(p > 0)
                def _():
                    store_x(blk - 2, slot).wait()

                residual(blk, slot)
                store_x(blk, slot).start()
            return carry

        lax.fori_loop(0, n_pairs, pair, 0)
        store_x(2 * n_pairs - 2, 0).wait()
        store_x(2 * n_pairs - 1, 1).wait()


def _experts(tile0, n_tiles, src, gs, h, x_all, mod_l, wg, wu, wd, layer):
    wmap = lambda g, t0, nt, src: (layer, g % N_EXP, 0, 0)
    return pl.pallas_call(
        _experts_kernel,
        grid_spec=pltpu.PrefetchScalarGridSpec(
            num_scalar_prefetch=3,
            grid=(N_GROUPS,),
            in_specs=[
                pl.BlockSpec((GATE_BLK,), lambda g, t0, nt, src: (g // N_EXP,), memory_space=pltpu.SMEM),
                pl.BlockSpec(memory_space=pl.ANY),
                pl.BlockSpec(memory_space=pl.ANY),
                pl.BlockSpec((8, 6, D), lambda g, t0, nt, src: (0, 0, 0)),
                pl.BlockSpec((None, None, D, D_EXP), wmap),
                pl.BlockSpec((None, None, D, D_EXP), wmap),
                pl.BlockSpec((None, None, D_EXP, D), wmap),
            ],
            out_specs=pl.BlockSpec(memory_space=pl.ANY),
            scratch_shapes=[
                pltpu.VMEM((ACC_TOK * 8, 128), F32),
                pltpu.VMEM((ACC_TOK * 8, 128), F32),
                pltpu.VMEM((2 * EXP_TM * 8, 128), F32),
                pltpu.VMEM((2 * EXP_TM * 8, 128), F32),
                pltpu.VMEM((D, D_EXP), BF16),
                pltpu.VMEM((D, D_EXP), BF16),
                pltpu.VMEM((D_EXP, D), BF16),
                pltpu.VMEM((2, RES_TB, D), F32),
                pltpu.VMEM((2, RES_TB, D), F32),
                pltpu.SemaphoreType.DMA,
                pltpu.SemaphoreType.DMA((2,)),
                pltpu.SemaphoreType.DMA((2,)),
            ],
        ),
        out_shape=jax.ShapeDtypeStruct((N_TOK, D), F32),
        input_output_aliases={5: 0},
        compiler_params=_cp("arbitrary"),
        name="moe_experts",
    )(tile0, n_tiles, src, gs, h, x_all, mod_l, wg, wu, wd)


def _moe(x_all, mod_l, n2g, wr, br, wg, wu, wd, layer):
    h, metat, cnt = _router(x_all, mod_l, n2g, wr, br)
    counts = cnt[:, 0, :N_EXP].astype(jnp.int32).reshape(N_GROUPS)
    padded = (counts + EXP_TM - 1) // EXP_TM * EXP_TM
    ends = jnp.cumsum(padded)
    offs = ends - padded
    rec = metat.astype(jnp.int32)
    half = (jnp.arange(N_TOK, dtype=jnp.int32) // HALF_TOK)[None, :]
    group = rec[M_E1:M_E2 + 1] + N_EXP * half
    is_group = group[None] == jnp.arange(N_GROUPS, dtype=jnp.int32)[:, None, None]
    pos = jnp.sum(jnp.where(is_group, offs[:, None, None], 0), axis=0) + rec[M_R1:M_R2 + 1]
    src = _order(pos.reshape(N_ASSIGN), offs + counts, ends)
    g12 = metat[M_G1:M_G2 + 1].reshape(2, 2, HALF_TOK)
    gates = jnp.concatenate([g12[0], jnp.zeros((2, CODE_PLANE - HALF_TOK), F32), g12[1]], axis=1)
    return _experts(offs // EXP_TM, padded // EXP_TM, src, gates.reshape(2 * GATE_BLK), h, x_all, mod_l,
                    wg, wu, wd, layer)


def _final_kernel(x_ref, g_ref, o_ref):
    o_ref[...] = _rms(x_ref[...], g_ref[...])


def _final_norm(x_all, g, blk0, n_rows):
    tb = 512
    return pl.pallas_call(
        _final_kernel,
        grid=(n_rows // tb,),
        in_specs=[pl.BlockSpec((tb, D), lambda i: (blk0 + i, 0)), _resident((1, D), lambda i: (0, 0))],
        out_specs=pl.BlockSpec((tb, D), lambda i: (i, 0)),
        out_shape=jax.ShapeDtypeStruct((n_rows, D), F32),
        compiler_params=_cp("arbitrary"),
        name="final_norm",
    )(x_all, g)


def _rope_tables():
    pos = jnp.arange(DEC_SEQ)
    row = (pos // GRID_W).astype(F32)
    col = (pos % GRID_W).astype(F32)
    n_freq = HD // 4
    inv = ROPE_THETA ** (-jnp.arange(n_freq, dtype=F32) / n_freq)
    ang = jnp.concatenate([row[:, None] * inv, col[:, None] * inv], axis=-1)
    cos = jnp.repeat(jnp.cos(ang), 2, axis=-1)
    sin = jnp.repeat(jnp.sin(ang), 2, axis=-1) * jnp.tile(jnp.array([-1.0, 1.0], F32), HD // 2)
    nblk = DEC_SEQ // QKV_TB
    cos_tab = jnp.concatenate([jnp.ones((1, QKV_TB, HD), F32), cos.reshape(nblk, QKV_TB, HD)], axis=0)
    sin_tab = jnp.concatenate([jnp.zeros((1, QKV_TB, HD), F32), sin.reshape(nblk, QKV_TB, HD)], axis=0)
    return cos_tab, sin_tab


def kernel(x_prompt, x_sample, state_gla, cache_k, cache_v, c, c_ctx, w_mod, b_mod, norm1_g, norm2_g,
           w_in_even, w_gate_up, b_gate_up, gla_norm_g, w_spatial, b_spatial, w_out_even,
           w_in_odd, q_norm_g, k_norm_g, w_out_odd, w_router_group, b_router_group,
           w_router_expert, b_router_expert, w_exp_gate, w_exp_up, w_exp_down, final_norm_g):
    x_all = None
    cond8 = jnp.concatenate([c_ctx[None], c, jnp.zeros((3, D), F32)], axis=0)
    mod = _modulation(cond8, w_mod, b_mod)
    cos_tab, sin_tab = _rope_tables()
    zero_state = jnp.zeros((1, 2, QK_W, GLA_DV), F32)
    state_in = state_gla.reshape(DEC_BATCH, -1, 2, QK_W, GLA_DV)
    cache_k2 = cache_k.reshape(DEC_BATCH, -1, SEQ, KV_W)
    cache_v2 = cache_v.reshape(DEC_BATCH, -1, SEQ, KV_W)

    gla_states, ctx_k, ctx_v = [], [], []
    for l in range(DEPTH):
        i = l // 2
        n1g = norm1_g[l][None]
        if l % 2 == 0:
            w = w_in_even[i]
            win = jnp.concatenate([w[:, :1536], w[:, 1568:], w[:, 1536:1568], jnp.zeros((D, 96), F32)],
                                  axis=1).astype(BF16)
            wgu = jnp.zeros((128, 2 * QK_W), F32)
            wgu = wgu.at[0:GLA_RANK, 0:QK_W].set(w_gate_up[i, 0])
            wgu = wgu.at[GLA_RANK:2 * GLA_RANK, QK_W:].set(w_gate_up[i, 1]).astype(BF16)
            bgu = b_gate_up[i].reshape(1, 2 * QK_W)
            args = (mod[l], n1g, win, wgu, bgu, gla_norm_g[i][None], w_spatial[i].astype(BF16),
                    b_spatial[i].T, w_out_even[i].astype(BF16))
            first = l == 0
            x_all, st = _even_mixer(x_all, *args, zero_state, latent=False,
                                    x_first=x_prompt.reshape(N_CTX, D) if first else None)
            gla_states.append(st)
            x_all, _ = _even_mixer(x_all, *args, state_in[:, i], latent=True,
                                   x_first=x_sample.reshape(N_LAT, D) if first else None)
        else:
            q, k, v = _qkv(x_all, mod[l], n1g, w_in_odd[i].astype(BF16), q_norm_g[i][None],
                           k_norm_g[i][None], cos_tab, sin_tab)
            ctx_k.append(k[:N_CTX].reshape(BATCH, SEQ, ATT_KV, HD))
            ctx_v.append(v[:N_CTX].reshape(BATCH, SEQ, ATT_KV, HD))
            wout = w_out_odd[i].astype(BF16)
            x_all = _attention(x_all, mod[l], q, k, v, wout)
            x_all = _attention(x_all, mod[l], q, k, v, wout, cache_k2, cache_v2, layer_i=i)
        wr = jnp.concatenate([w_router_expert[l], w_router_group[l],
                              jnp.zeros((D, 128 - N_EXP - MOE_GROUPS), F32)], axis=1)
        br = jnp.concatenate([b_router_expert[l], b_router_group[l],
                              jnp.zeros((128 - N_EXP - MOE_GROUPS,), F32)])[None]
        x_all = _moe(x_all, mod[l], norm2_g[l][None], wr, br, w_exp_gate, w_exp_up, w_exp_down, l)

    fg = final_norm_g[None]
    y_prompt = _final_norm(x_all, fg, 0, N_CTX).reshape(BATCH, SEQ, D)
    y_sample = _final_norm(x_all, fg, N_CTX // 512, N_LAT).reshape(DEC_BATCH, DEC_SEQ, D)
    new_state = jnp.stack(gla_states, axis=1).reshape(BATCH, -1, 2, GLA_HEADS, GLA_DK, GLA_DV)
    return (y_prompt, y_sample, new_state, jnp.stack(ctx_k, axis=1), jnp.stack(ctx_v, axis=1))
```

```python
import functools

import jax
import jax.numpy as jnp
import numpy as np
from jax import lax
from jax.experimental import pallas as pl
from jax.experimental.pallas import tpu as pltpu

F32 = jnp.float32
BF16 = jnp.bfloat16

D = 1024
BATCH, SEQ = 16, 256
DEC_BATCH, DEC_SEQ = 4, 1024
N_CTX = BATCH * SEQ
N_LAT = DEC_BATCH * DEC_SEQ
N_TOK = N_CTX + N_LAT
DEPTH = 4
EPS = 1e-6
GRID_W = 64
ROPE_THETA = 10000.0

GLA_HEADS, GLA_DK, GLA_DV, GLA_RANK, GLA_CHUNK, GLA_TAU = 4, 64, 128, 16, 128, 16.0
QK_W = GLA_HEADS * GLA_DK
V_W = GLA_HEADS * GLA_DV
GMLP_GROUPS, GMLP_DIM, GMLP_CHUNK = 4, 128, 128
GMLP_W = GMLP_GROUPS * GMLP_DIM
C_Q, C_K, C_V, C_G, C_U, C_VG, C_A = 0, 256, 512, 1024, 1536, 2048, 2560
EVEN_PACK = 2688

ATT_HEADS, ATT_KV, HD = 8, 2, 128
ATT_G = ATT_HEADS // ATT_KV
Q_W = ATT_HEADS * HD
KV_W = ATT_KV * HD

MOE_GROUPS, MOE_PER_GROUP = 4, 8
N_EXP = MOE_GROUPS * MOE_PER_GROUP
D_EXP = D // 4
NEG = -1e30

VMEM_LIMIT = 56 * 1024 * 1024


def _cp(*sem):
    return pltpu.CompilerParams(dimension_semantics=sem, vmem_limit_bytes=VMEM_LIMIT)


def _dot(a, b):
    return jnp.dot(a.astype(BF16), b.astype(BF16), preferred_element_type=F32)


def _dot_nt(a, b):
    return lax.dot_general(a.astype(BF16), b.astype(BF16), (((1,), (1,)), ((), ())),
                           preferred_element_type=F32)


def _dot_tn(a, b):
    return lax.dot_general(a.astype(BF16), b.astype(BF16), (((0,), (0,)), ((), ())),
                           preferred_element_type=F32)


def _rms(x, g):
    return x * lax.rsqrt(jnp.mean(x * x, axis=-1, keepdims=True) + EPS) * g


def _silu(x):
    return x * jax.nn.sigmoid(x)


def _gelu(x):
    return 0.5 * x * (1.0 + jnp.tanh(np.sqrt(2.0 / np.pi).astype(np.float32) * (x + 0.044715 * (x * x * x))))


def _log_sigmoid(z):
    return jnp.minimum(z, 0.0) - jnp.log(1.0 + jnp.exp(-jnp.abs(z)))


def _rows_to_tiles(ref, x):
    rows = x.shape[0]
    for j in range(D // 128):
        ref[pl.ds(j, rows, stride=8), :] = x[:, j * 128:(j + 1) * 128]


def _tiles_to_rows(ref, rows):
    return jnp.concatenate([ref[pl.ds(j, rows, stride=8), :] for j in range(D // 128)], axis=1)


def _tile_of(ref, row8):
    return ref.at[pl.ds(pl.multiple_of(row8, 8), 8), :]


def _resident(shape, index_map):
    return pl.BlockSpec(shape, index_map, pipeline_mode=pl.Buffered(1))


def _mod_kernel(cond_ref, w_ref, b_ref, o_ref):
    c = cond_ref[...]
    o_ref[...] = jnp.dot(_silu(c), w_ref[...], precision=lax.Precision.HIGHEST,
                         preferred_element_type=F32) + b_ref[...]


def _modulation(cond8, w_mod, b_mod):
    tn = 1024
    out = pl.pallas_call(
        _mod_kernel,
        grid=(DEPTH, 6 * D // tn),
        in_specs=[
            pl.BlockSpec((8, D), lambda l, j: (0, 0)),
            pl.BlockSpec((None, D, tn), lambda l, j: (l, 0, j)),
            pl.BlockSpec((None, 1, tn), lambda l, j: (l, 0, j)),
        ],
        out_specs=pl.BlockSpec((None, 8, tn), lambda l, j: (l, 0, j)),
        out_shape=jax.ShapeDtypeStruct((DEPTH, 8, 6 * D), F32),
        compiler_params=_cp("arbitrary", "arbitrary"),
        name="adaln_mod",
    )(cond8, w_mod, b_mod.reshape(DEPTH, 1, 6 * D))
    return out.reshape(DEPTH, 8, 6, D)


def _even_kernel(x_ref, mod_ref, n1g_ref, win_ref, wgu_ref, bgu_ref, glag_ref, ws_ref, bs_ref,
                 wout_ref, s0_ref, xo_ref, st_ref, proj, la, o_f, o_b, st_scr, *, T):
    n_chunks = T // GLA_CHUNK
    shift, scale, gate = mod_ref[0:1, :], mod_ref[1:2, :], mod_ref[2:3, :]
    RB = 128
    PROJ_RB = 256

    def proj_body(r, carry):
        r0 = pl.multiple_of(r * PROJ_RB, PROJ_RB)
        h = _rms(x_ref[pl.ds(r0, PROJ_RB), :], n1g_ref[...]) * (1.0 + scale) + shift
        p = _dot(h, win_ref[...])
        proj[pl.ds(r0, PROJ_RB), :] = p
        z = _dot(p[:, C_A:C_A + 128], wgu_ref[...]) + bgu_ref[...]
        la[pl.ds(r0, PROJ_RB), :] = _log_sigmoid(z) * (1.0 / GLA_TAU)
        return carry

    lax.fori_loop(0, T // PROJ_RB, proj_body, 0)

    st_scr[0] = s0_ref[0].T
    st_scr[1] = s0_ref[1].T

    ci = lax.broadcasted_iota(jnp.int32, (GLA_CHUNK, GLA_CHUNK), 0)
    cj = lax.broadcasted_iota(jnp.int32, (GLA_CHUNK, GLA_CHUNK), 1)
    tri = (jnp.where(ci >= cj, 1.0, 0.0).astype(BF16), jnp.where(ci <= cj, 1.0, 0.0).astype(BF16))
    ai = lax.broadcasted_iota(jnp.int32, (GLA_HEADS * GLA_CHUNK, GLA_CHUNK), 0) % GLA_CHUNK
    aj = lax.broadcasted_iota(jnp.int32, (GLA_HEADS * GLA_CHUNK, GLA_CHUNK), 1)
    amask = (ai >= aj, ai <= aj)
    lane_head = lax.broadcasted_iota(jnp.int32, (1, QK_W), 1) // GLA_DK
    hmask = [jnp.where(lane_head == h, 1.0, 0.0) for h in range(GLA_HEADS)]

    def chunk_body(i, carry):
        for d in range(2):
            c = i if d == 0 else n_chunks - 1 - i
            r0 = pl.multiple_of(c * GLA_CHUNK, GLA_CHUNK)
            q = proj[pl.ds(r0, GLA_CHUNK), C_Q:C_Q + QK_W] * (GLA_DK ** -0.5)
            k = proj[pl.ds(r0, GLA_CHUNK), C_K:C_K + QK_W]
            v = proj[pl.ds(r0, GLA_CHUNK), C_V:C_V + V_W]
            lac = la[pl.ds(r0, GLA_CHUNK), d * QK_W:(d + 1) * QK_W]
            hi = lac.astype(BF16)
            lo = (lac - hi.astype(F32)).astype(BF16)
            b = (jnp.dot(tri[d], hi, preferred_element_type=F32)
                 + jnp.dot(tri[d], lo, preferred_element_type=F32))
            bend = b[GLA_CHUNK - 1:GLA_CHUNK, :] if d == 0 else b[0:1, :]
            qe = q * jnp.exp(b)
            ke = k * jnp.exp(-b)
            kd = k * jnp.exp(bend - b)
            st = st_scr[d]
            qstack = jnp.concatenate([qe * hmask[h] for h in range(GLA_HEADS)], axis=0).astype(BF16)
            att = jnp.where(amask[d], _dot_nt(qstack, ke), 0.0)
            inter = _dot_nt(qstack, st)
            outs = []
            for h in range(GLA_HEADS):
                rows = slice(h * GLA_CHUNK, (h + 1) * GLA_CHUNK)
                outs.append(_dot(att[rows], v[:, h * GLA_DV:(h + 1) * GLA_DV]) + inter[rows])
            o = jnp.concatenate(outs, axis=1)
            if d == 0:
                o_f[pl.ds(r0, GLA_CHUNK), :] = o
            else:
                o_b[pl.ds(r0, GLA_CHUNK), :] = o
            vstack = jnp.concatenate([v[:, h * GLA_DV:(h + 1) * GLA_DV] for h in range(GLA_HEADS)], axis=0)
            kstack = jnp.concatenate([kd * hmask[h] for h in range(GLA_HEADS)], axis=0)
            st_scr[d] = st * jnp.exp(bend) + _dot_tn(vstack, kstack)
        return carry

    lax.fori_loop(0, n_chunks, chunk_body, 0, unroll=2)
    st_ref[0] = st_scr[0].T
    st_ref[1] = st_scr[1].T

    def out_body(r, carry):
        r0 = pl.multiple_of(r * RB, RB)
        osum = o_f[pl.ds(r0, RB), :] + o_b[pl.ds(r0, RB), :]
        g = proj[pl.ds(r0, RB), C_G:C_G + V_W]
        u = proj[pl.ds(r0, RB), C_U:C_U + GMLP_W]
        vg = _gelu(proj[pl.ds(r0, RB), C_VG:C_VG + GMLP_W])
        parts = []
        for h in range(GLA_HEADS):
            oh = osum[:, h * GLA_DV:(h + 1) * GLA_DV]
            parts.append(_rms(oh, glag_ref[...]) * _silu(g[:, h * GLA_DV:(h + 1) * GLA_DV]))
        for gi in range(GMLP_GROUPS):
            vc = vg[:, gi * GMLP_DIM:(gi + 1) * GMLP_DIM]
            vc = vc - jnp.mean(vc, axis=-1, keepdims=True)
            vn = vc * lax.rsqrt(jnp.mean(vc * vc, axis=-1, keepdims=True) + EPS)
            sg = _dot(ws_ref[gi], vn) + bs_ref[:, gi:gi + 1]
            parts.append(_gelu(u[:, gi * GMLP_DIM:(gi + 1) * GMLP_DIM]) * sg)
        mix = jnp.concatenate(parts, axis=1)
        y = _dot(mix, wout_ref[...])
        xo_ref[pl.ds(r0, RB), :] = x_ref[pl.ds(r0, RB), :] + gate * y
        return carry

    lax.fori_loop(0, T // RB, out_body, 0)


def _even_mixer(x_all, mod_l, n1g, win, wgu, bgu, glag, ws, bs, wout, s0, *, latent, x_first=None):
    if latent:
        T, nseq, blk0 = DEC_SEQ, DEC_BATCH, N_CTX // DEC_SEQ
        cond = lambda i: 1 + i
        s0_spec = pl.BlockSpec((None, 2, QK_W, GLA_DV), lambda i: (i, 0, 0, 0))
    else:
        T, nseq, blk0 = SEQ, BATCH, 0
        cond = lambda i: 0
        s0_spec = pl.BlockSpec((None, 2, QK_W, GLA_DV), lambda i: (0, 0, 0, 0))
    const2 = lambda i: (0, 0)
    body = functools.partial(_even_kernel, T=T)
    x_spec = pl.BlockSpec((T, D), lambda i: (blk0 + i, 0))
    if x_first is None:
        lead_specs, lead_args, aliases = [x_spec], (x_all,), {0: 0}
    elif x_all is None:
        lead_specs, lead_args, aliases = [pl.BlockSpec((T, D), lambda i: (i, 0))], (x_first,), {}
    else:
        lead_specs = [pl.BlockSpec(memory_space=pl.ANY), pl.BlockSpec((T, D), lambda i: (i, 0))]
        lead_args, aliases = (x_all, x_first), {0: 0}
        body = lambda dst_ref, *refs: _even_kernel(*refs, T=T)
    x_new, states = pl.pallas_call(
        body,
        grid=(nseq,),
        in_specs=lead_specs + [
            pl.BlockSpec((None, 6, D), lambda i: (cond(i), 0, 0)),
            _resident((1, D), const2),
            _resident((D, EVEN_PACK), const2),
            _resident((128, 2 * QK_W), const2),
            _resident((1, 2 * QK_W), const2),
            _resident((1, GLA_DV), const2),
            _resident((GMLP_GROUPS, GMLP_CHUNK, GMLP_CHUNK), lambda i: (0, 0, 0)),
            _resident((GMLP_CHUNK, GMLP_GROUPS), const2),
            _resident((D, D), const2),
            s0_spec,
        ],
        out_specs=[
            x_spec,
            pl.BlockSpec((None, 2, QK_W, GLA_DV), lambda i: (i, 0, 0, 0)),
        ],
        out_shape=[
            jax.ShapeDtypeStruct((N_TOK, D), F32),
            jax.ShapeDtypeStruct((nseq, 2, QK_W, GLA_DV), F32),
        ],
        scratch_shapes=[
            pltpu.VMEM((T, EVEN_PACK), F32),
            pltpu.VMEM((T, 2 * QK_W), F32),
            pltpu.VMEM((T, V_W), F32),
            pltpu.VMEM((T, V_W), F32),
            pltpu.VMEM((2, GLA_DV, QK_W), F32),
        ],
        input_output_aliases=aliases,
        compiler_params=_cp("arbitrary"),
        name="even_mixer_latent" if latent else "even_mixer_context",
    )(*lead_args, mod_l, n1g, win, wgu, bgu, glag, ws, bs, wout, s0)
    return x_new, states


QKV_TB = 512


def _qkv_kernel(x_ref, mod_ref, n1g_ref, win_ref, gq_ref, gk_ref, cos_ref, sin_ref, q_ref, k_ref, v_ref):
    shift, scale = mod_ref[0:1, :], mod_ref[1:2, :]
    h = _rms(x_ref[...], n1g_ref[...]) * (1.0 + scale) + shift
    p = _dot(h, win_ref[...])
    cos, sin = cos_ref[...], sin_ref[...]
    even_lane = lax.broadcasted_iota(jnp.int32, (1, HD), 1) % 2 == 0

    def rope(xn):
        swapped = jnp.where(even_lane, pltpu.roll(xn, HD - 1, axis=1), pltpu.roll(xn, 1, axis=1))
        return xn * cos + swapped * sin

    def emit(rotate):
        for hh in range(ATT_HEADS):
            qn = _rms(p[:, hh * HD:(hh + 1) * HD], gq_ref[...])
            q_ref[:, hh * HD:(hh + 1) * HD] = (rotate(qn) * (HD ** -0.5)).astype(BF16)
        for hh in range(ATT_KV):
            k_ref[:, hh * HD:(hh + 1) * HD] = rotate(_rms(p[:, Q_W + hh * HD:Q_W + (hh + 1) * HD], gk_ref[...]))

    is_latent = pl.program_id(0) >= N_CTX // QKV_TB

    @pl.when(is_latent)
    def _():
        emit(rope)

    @pl.when(jnp.logical_not(is_latent))
    def _():
        emit(lambda xn: xn)

    v_ref[...] = p[:, Q_W + KV_W:]


def _qkv(x_all, mod_l, n1g, win, gq, gk, cos_tab, sin_tab):
    nb_ctx = N_CTX // QKV_TB
    per_seq = DEC_SEQ // QKV_TB
    cond = lambda i: jnp.where(i < nb_ctx, 0, 1 + (i - nb_ctx) // per_seq)
    tab = lambda i: jnp.where(i < nb_ctx, 0, 1 + (i - nb_ctx) % per_seq)
    const2 = lambda i: (0, 0)
    return pl.pallas_call(
        _qkv_kernel,
        grid=(N_TOK // QKV_TB,),
        in_specs=[
            pl.BlockSpec((QKV_TB, D), lambda i: (i, 0)),
            pl.BlockSpec((None, 6, D), lambda i: (cond(i), 0, 0)),
            _resident((1, D), const2),
            _resident((D, Q_W + 2 * KV_W), const2),
            _resident((1, HD), const2),
            _resident((1, HD), const2),
            pl.BlockSpec((None, QKV_TB, HD), lambda i: (tab(i), 0, 0)),
            pl.BlockSpec((None, QKV_TB, HD), lambda i: (tab(i), 0, 0)),
        ],
        out_specs=[
            pl.BlockSpec((QKV_TB, Q_W), lambda i: (i, 0)),
            pl.BlockSpec((QKV_TB, KV_W), lambda i: (i, 0)),
            pl.BlockSpec((QKV_TB, KV_W), lambda i: (i, 0)),
        ],
        out_shape=[
            jax.ShapeDtypeStruct((N_TOK, Q_W), BF16),
            jax.ShapeDtypeStruct((N_TOK, KV_W), F32),
            jax.ShapeDtypeStruct((N_TOK, KV_W), F32),
        ],
        compiler_params=_cp("arbitrary"),
        name="odd_qkv",
    )(x_all, mod_l, n1g, win, gq, gk, cos_tab, sin_tab)


ATT_TQ = 256


def _attn_kernel(*refs, n_kv):
    q_ref = refs[0]
    kv_refs = refs[1:1 + 2 * n_kv]
    x_ref, mod_ref, wout_ref, xo_ref, att_scr = refs[1 + 2 * n_kv:]
    gate = mod_ref[2:3, :]
    for kh in range(ATT_KV):
        ks = [kv_refs[2 * s][:, kh * HD:(kh + 1) * HD].astype(BF16) for s in range(n_kv)]
        vs = [jnp.concatenate([kv_refs[2 * s + 1][:, kh * HD:(kh + 1) * HD].astype(BF16),
                               jnp.ones((kv_refs[2 * s + 1].shape[0], HD), BF16)], axis=1) for s in range(n_kv)]
        for g in range(ATT_G):
            hh = kh * ATT_G + g
            qh = q_ref[:, hh * HD:(hh + 1) * HD]
            ss = [_dot_nt(qh, kk) for kk in ks]
            m = ss[0].max(axis=-1, keepdims=True)
            for s in ss[1:]:
                m = jnp.maximum(m, s.max(axis=-1, keepdims=True))
            o = _dot(jnp.exp(ss[0] - m), vs[0])
            for s, vv in zip(ss[1:], vs[1:]):
                o = o + _dot(jnp.exp(s - m), vv)
            att_scr[:, hh * HD:(hh + 1) * HD] = o[:, :HD] / o[:, HD:HD + 1]
    y = _dot(att_scr[...], wout_ref[...])
    xo_ref[...] = x_ref[...] + gate * y


def _attention(x_all, mod_l, q, k, v, wout, cache_k=None, cache_v=None, layer_i=0):
    latent = cache_k is not None
    const2 = lambda *a: (0, 0)
    if latent:
        nq = DEC_SEQ // ATT_TQ
        row_blk = lambda b, j: (N_CTX // ATT_TQ + b * nq + j, 0)
        grid = (DEC_BATCH, nq)
        kv_specs = [
            pl.BlockSpec((None, None, SEQ, KV_W), lambda b, j: (b, layer_i, 0, 0)),
            pl.BlockSpec((None, None, SEQ, KV_W), lambda b, j: (b, layer_i, 0, 0)),
            pl.BlockSpec((DEC_SEQ, KV_W), lambda b, j: (N_CTX // DEC_SEQ + b, 0)),
            pl.BlockSpec((DEC_SEQ, KV_W), lambda b, j: (N_CTX // DEC_SEQ + b, 0)),
        ]
        kv_args = (cache_k, cache_v, k, v)
        mod_spec = pl.BlockSpec((None, 6, D), lambda b, j: (1 + b, 0, 0))
        sem = ("arbitrary", "arbitrary")
        n_kv = 2
    else:
        row_blk = lambda i: (i, 0)
        grid = (BATCH,)
        kv_specs = [pl.BlockSpec((SEQ, KV_W), row_blk), pl.BlockSpec((SEQ, KV_W), row_blk)]
        kv_args = (k, v)
        mod_spec = pl.BlockSpec((None, 6, D), lambda i: (0, 0, 0))
        sem = ("arbitrary",)
        n_kv = 1
    n_in = 1 + len(kv_args)
    return pl.pallas_call(
        functools.partial(_attn_kernel, n_kv=n_kv),
        grid=grid,
        in_specs=[pl.BlockSpec((ATT_TQ, Q_W), row_blk)] + kv_specs + [
            pl.BlockSpec((ATT_TQ, D), row_blk),
            mod_spec,
            _resident((D, D), const2),
        ],
        out_specs=pl.BlockSpec((ATT_TQ, D), row_blk),
        out_shape=jax.ShapeDtypeStruct((N_TOK, D), F32),
        scratch_shapes=[pltpu.VMEM((ATT_TQ, Q_W), F32)],
        input_output_aliases={n_in: 0},
        compiler_params=_cp(*sem),
        name="attention_latent" if latent else "attention_context",
    )(q, *kv_args, x_all, mod_l, wout)


ROUTE_TB = 512
HALF_TOK = N_TOK // 2
M_E1, M_E2, M_G1, M_G2, M_R1, M_R2 = 0, 1, 2, 3, 4, 5


def _router_kernel(x_ref, mod_ref, n2g_ref, whi_ref, wlo_ref, br_ref, h_ref, metat_ref, cnt_ref, run):
    @pl.when(pl.program_id(0) % (HALF_TOK // ROUTE_TB) == 0)
    def _():
        run[...] = jnp.zeros_like(run)

    shift, scale = mod_ref[3:4, :], mod_ref[4:5, :]
    h = _rms(x_ref[...], n2g_ref[...]) * (1.0 + scale) + shift
    _rows_to_tiles(h_ref, h)
    h_hi = h.astype(BF16)
    h_lo = (h - h_hi.astype(F32)).astype(BF16)
    dot = functools.partial(jnp.dot, preferred_element_type=F32)
    logits = dot(h_hi, whi_ref[...]) + dot(h_lo, whi_ref[...]) + dot(h_hi, wlo_ref[...]) + br_ref[...]
    lane = lax.broadcasted_iota(jnp.int32, logits.shape, 1).astype(F32)
    big = 1e4

    def first_argmax(vals):
        m = vals.max(axis=-1, keepdims=True)
        return m, jnp.where(vals == m, lane, big).min(axis=-1, keepdims=True)

    gl = jnp.where((lane >= N_EXP) & (lane < N_EXP + MOE_GROUPS), logits, NEG)
    gmax, glane = first_argmax(gl)
    g_p = 1.0 / jnp.exp(gl - gmax).sum(axis=-1, keepdims=True)
    lo = (glane - N_EXP) * MOE_PER_GROUP
    el = jnp.where((lane >= lo) & (lane < lo + MOE_PER_GROUP), logits, NEG)
    m1, i1 = first_argmax(el)
    m2, i2 = first_argmax(jnp.where(lane == i1, NEG, el))
    t = jnp.exp(m2 - m1)
    w1 = 1.0 / (1.0 + t)
    sel1, sel2 = lane == i1, lane == i2
    onehot = jnp.where(sel1 | sel2, 1.0, 0.0)
    ri = lax.broadcasted_iota(jnp.int32, (ROUTE_TB, ROUTE_TB), 0)
    rj = lax.broadcasted_iota(jnp.int32, (ROUTE_TB, ROUTE_TB), 1)
    before = _dot(jnp.where(ri > rj, 1.0, 0.0), onehot) + run[...]
    r1 = jnp.where(sel1, before, 0.0).sum(axis=-1, keepdims=True)
    r2 = jnp.where(sel2, before, 0.0).sum(axis=-1, keepdims=True)
    run[...] += onehot.sum(axis=0, keepdims=True)
    cnt_ref[...] = run[...]
    meta = jnp.zeros_like(logits)
    for j, val in enumerate([i1, i2, w1 * g_p, (t * w1) * g_p, r1, r2]):
        meta = jnp.where(lane == j, val, meta)
    metat_ref[...] = meta.T[0:8, :]


def _router(x_all, mod_l, n2g, wr, br):
    w_hi = wr.astype(BF16)
    wr_lo = (wr - w_hi.astype(F32)).astype(BF16)
    nb_ctx = N_CTX // ROUTE_TB
    per_seq = DEC_SEQ // ROUTE_TB
    cond = lambda i: jnp.where(i < nb_ctx, 0, 1 + (i - nb_ctx) // per_seq)
    const2 = lambda i: (0, 0)
    return pl.pallas_call(
        _router_kernel,
        grid=(N_TOK // ROUTE_TB,),
        in_specs=[
            pl.BlockSpec((ROUTE_TB, D), lambda i: (i, 0)),
            pl.BlockSpec((None, 6, D), lambda i: (cond(i), 0, 0)),
            _resident((1, D), const2),
            _resident((D, 128), const2),
            _resident((D, 128), const2),
            _resident((1, 128), const2),
        ],
        out_specs=[
            pl.BlockSpec((ROUTE_TB * 8, 128), lambda i: (i, 0)),
            pl.BlockSpec((8, ROUTE_TB), lambda i: (0, i)),
            pl.BlockSpec((None, 1, 128), lambda i: (i // (HALF_TOK // ROUTE_TB), 0, 0)),
        ],
        out_shape=[
            jax.ShapeDtypeStruct((N_TOK * 8, 128), F32),
            jax.ShapeDtypeStruct((8, N_TOK), F32),
            jax.ShapeDtypeStruct((2, 1, 128), F32),
        ],
        scratch_shapes=[pltpu.VMEM((1, 128), F32)],
        compiler_params=_cp("arbitrary"),
        name="moe_router",
    )(x_all, mod_l, n2g, w_hi, wr_lo, br)


EXP_TM = 128
N_ASSIGN = 2 * N_TOK
N_GROUPS = 2 * N_EXP
MAX_TILES = N_ASSIGN // EXP_TM + N_GROUPS
N_SORTED = MAX_TILES * EXP_TM
ORDER_BLK = 2048
CODE_PLANE = 2 * HALF_TOK
CODE_MASK = 8 * CODE_PLANE - 1
DUMMY8 = HALF_TOK * 8


def _order_kernel(pos1_ref, pos2_ref, pad_lo_ref, pad_hi_ref, src_ref):
    i = pl.program_id(0)
    local = (i % (HALF_TOK // ORDER_BLK)) * ORDER_BLK

    def body(t, carry):
        src_ref[pos1_ref[t]] = (local + t) * 8
        src_ref[pos2_ref[t]] = (local + t + CODE_PLANE) * 8
        return carry

    lax.fori_loop(0, ORDER_BLK, body, 0, unroll=16)

    @pl.when(i == 0)
    def _():
        def group(g, carry):
            def pad(p, c):
                src_ref[p] = DUMMY8
                return c
            return lax.fori_loop(pad_lo_ref[g], pad_hi_ref[g], pad, carry)

        lax.fori_loop(0, N_GROUPS, group, 0)


def _order(pos, pad_lo, pad_hi):
    return pl.pallas_call(
        _order_kernel,
        grid=(N_TOK // ORDER_BLK,),
        in_specs=[
            pl.BlockSpec((ORDER_BLK,), lambda i: (i,), memory_space=pltpu.SMEM),
            pl.BlockSpec((ORDER_BLK,), lambda i: (N_TOK // ORDER_BLK + i,), memory_space=pltpu.SMEM),
            pl.BlockSpec(memory_space=pltpu.SMEM),
            pl.BlockSpec(memory_space=pltpu.SMEM),
        ],
        out_specs=pl.BlockSpec(memory_space=pltpu.SMEM),
        out_shape=jax.ShapeDtypeStruct((N_SORTED,), jnp.int32),
        compiler_params=_cp("arbitrary"),
        name="moe_order",
    )(pos, pos, pad_lo, pad_hi)


GATE_BLK = CODE_PLANE + HALF_TOK
ACC_TOK = HALF_TOK + 64
GATHER_GROUP, ACC_GROUP = 16, 8


RES_TB = 256


def _experts_kernel(tile0_ref, ntile_ref, src_ref, gs_ref, h_hbm, x_hbm, mod_ref, wg_ref, wu_ref, wd_ref, xo_hbm,
                    h_res, acc, xbuf, ybuf, wgb, wub, wdb, xin, xout, sem, in_sem, out_sem):
    group = pl.program_id(0)
    expert = group % N_EXP
    half = group // N_EXP
    rows0 = pl.multiple_of(half * (HALF_TOK * 8), 8)

    @pl.when(expert == 0)
    def _():
        cp = pltpu.make_async_copy(h_hbm.at[pl.ds(rows0, HALF_TOK * 8), :], h_res.at[pl.ds(0, HALF_TOK * 8), :], sem)
        cp.start()
        h_res[pl.ds(DUMMY8, 8), :] = jnp.zeros((8, 128), F32)

        def zero(i, carry):
            acc[pl.ds(pl.multiple_of(i * 512, 512), 512), :] = jnp.zeros((512, 128), F32)
            return carry

        lax.fori_loop(0, ACC_TOK * 8 // 512, zero, 0)
        cp.wait()

    n_tiles = ntile_ref[group]

    @pl.when(n_tiles > 0)
    def _():
        wgb[...] = wg_ref[...].astype(BF16)
        wub[...] = wu_ref[...].astype(BF16)
        wdb[...] = wd_ref[...].astype(BF16)

    def process(base, rows):
        def gather(g, c):
            for i in range(GATHER_GROUP):
                r = g * GATHER_GROUP + i
                xbuf[pl.ds(pl.multiple_of(r * 8, 8), 8), :] = _tile_of(h_res, src_ref[base + r] & CODE_MASK)[...]
            return c

        lax.fori_loop(0, rows // GATHER_GROUP, gather, 0)
        x = _tiles_to_rows(xbuf, rows).astype(BF16)
        hid = _silu(_dot(x, wgb[...])) * _dot(x, wub[...])
        _rows_to_tiles(ybuf, _dot(hid, wdb[...]))

        def accumulate(g, c):
            targets, values = [], []
            for i in range(ACC_GROUP):
                r = g * ACC_GROUP + i
                code = src_ref[base + r]
                target = _tile_of(acc, code & CODE_MASK)
                targets.append(target)
                values.append(target[...] + gs_ref[code >> 3] * ybuf[pl.ds(pl.multiple_of(r * 8, 8), 8), :])
            for target, value in zip(targets, values):
                target[...] = value
            return c

        lax.fori_loop(0, rows // ACC_GROUP, accumulate, 0)

    row0 = tile0_ref[group] * EXP_TM

    def pair_body(j, carry):
        process(row0 + j * (2 * EXP_TM), 2 * EXP_TM)
        return carry

    lax.fori_loop(0, n_tiles // 2, pair_body, 0)

    @pl.when(n_tiles % 2 == 1)
    def _():
        process(row0 + (n_tiles - 1) * EXP_TM, EXP_TM)

    def x_rows(blk):
        return pl.ds(pl.multiple_of(half * HALF_TOK + blk * RES_TB, RES_TB), RES_TB)

    def load_x(blk, slot):
        return pltpu.make_async_copy(x_hbm.at[x_rows(blk), :], xin.at[slot], in_sem.at[slot])

    def store_x(blk, slot):
        return pltpu.make_async_copy(xout.at[slot], xo_hbm.at[x_rows(blk), :], out_sem.at[slot])

    def residual(blk, slot):
        cond = jnp.where(half == 0, 0, 1 + blk // (DEC_SEQ // RES_TB))
        gate = mod_ref[cond, 5:6, :]
        y = _tiles_to_rows(acc.at[pl.ds(pl.multiple_of(blk * (RES_TB * 8), RES_TB * 8), RES_TB * 8), :], RES_TB)
        xout[slot] = xin[slot] + gate * y

    @pl.when(expert == N_EXP - 1)
    def _():
        n_pairs = HALF_TOK // RES_TB // 2
        load_x(0, 0).start()

        def pair(p, carry):
            for slot in range(2):
                blk = 2 * p + slot
                if slot == 0:
                    load_x(blk + 1, 1).start()
                else:
                    @pl.when(p + 1 < n_pairs)
                    def _():
                        load_x(blk + 1, 0).start()
                load_x(blk, slot).wait()

                @pl.when(p > 0)
                def _():
                    store_x(blk - 2, slot).wait()

                residual(blk, slot)
                store_x(blk, slot).start()
            return carry

        lax.fori_loop(0, n_pairs, pair, 0)
        store_x(2 * n_pairs - 2, 0).wait()
        store_x(2 * n_pairs - 1, 1).wait()


def _experts(tile0, n_tiles, src, gs, h, x_all, mod_l, wg, wu, wd, layer):
    wmap = lambda g, t0, nt, src: (layer, g % N_EXP, 0, 0)
    return pl.pallas_call(
        _experts_kernel,
        grid_spec=pltpu.PrefetchScalarGridSpec(
            num_scalar_prefetch=3,
            grid=(N_GROUPS,),
            in_specs=[
                pl.BlockSpec((GATE_BLK,), lambda g, t0, nt, src: (g // N_EXP,), memory_space=pltpu.SMEM),
                pl.BlockSpec(memory_space=pl.ANY),
                pl.BlockSpec(memory_space=pl.ANY),
                pl.BlockSpec((8, 6, D), lambda g, t0, nt, src: (0, 0, 0)),
                pl.BlockSpec((None, None, D, D_EXP), wmap),
                pl.BlockSpec((None, None, D, D_EXP), wmap),
                pl.BlockSpec((None, None, D_EXP, D), wmap),
            ],
            out_specs=pl.BlockSpec(memory_space=pl.ANY),
            scratch_shapes=[
                pltpu.VMEM((ACC_TOK * 8, 128), F32),
                pltpu.VMEM((ACC_TOK * 8, 128), F32),
                pltpu.VMEM((2 * EXP_TM * 8, 128), F32),
                pltpu.VMEM((2 * EXP_TM * 8, 128), F32),
                pltpu.VMEM((D, D_EXP), BF16),
                pltpu.VMEM((D, D_EXP), BF16),
                pltpu.VMEM((D_EXP, D), BF16),
                pltpu.VMEM((2, RES_TB, D), F32),
                pltpu.VMEM((2, RES_TB, D), F32),
                pltpu.SemaphoreType.DMA,
                pltpu.SemaphoreType.DMA((2,)),
                pltpu.SemaphoreType.DMA((2,)),
            ],
        ),
        out_shape=jax.ShapeDtypeStruct((N_TOK, D), F32),
        input_output_aliases={5: 0},
        compiler_params=_cp("arbitrary"),
        name="moe_experts",
    )(tile0, n_tiles, src, gs, h, x_all, mod_l, wg, wu, wd)


def _moe(x_all, mod_l, n2g, wr, br, wg, wu, wd, layer):
    h, metat, cnt = _router(x_all, mod_l, n2g, wr, br)
    counts = cnt[:, 0, :N_EXP].astype(jnp.int32).reshape(N_GROUPS)
    padded = (counts + EXP_TM - 1) // EXP_TM * EXP_TM
    ends = jnp.cumsum(padded)
    offs = ends - padded
    rec = metat.astype(jnp.int32)
    half = (jnp.arange(N_TOK, dtype=jnp.int32) // HALF_TOK)[None, :]
    group = rec[M_E1:M_E2 + 1] + N_EXP * half
    is_group = group[None] == jnp.arange(N_GROUPS, dtype=jnp.int32)[:, None, None]
    pos = jnp.sum(jnp.where(is_group, offs[:, None, None], 0), axis=0) + rec[M_R1:M_R2 + 1]
    src = _order(pos.reshape(N_ASSIGN), offs + counts, ends)
    g12 = metat[M_G1:M_G2 + 1].reshape(2, 2, HALF_TOK)
    gates = jnp.concatenate([g12[0], jnp.zeros((2, CODE_PLANE - HALF_TOK), F32), g12[1]], axis=1)
    return _experts(offs // EXP_TM, padded // EXP_TM, src, gates.reshape(2 * GATE_BLK), h, x_all, mod_l,
                    wg, wu, wd, layer)


def _final_kernel(x_ref, g_ref, o_ref):
    o_ref[...] = _rms(x_ref[...], g_ref[...])


def _final_norm(x_all, g, blk0, n_rows):
    tb = 512
    return pl.pallas_call(
        _final_kernel,
        grid=(n_rows // tb,),
        in_specs=[pl.BlockSpec((tb, D), lambda i: (blk0 + i, 0)), _resident((1, D), lambda i: (0, 0))],
        out_specs=pl.BlockSpec((tb, D), lambda i: (i, 0)),
        out_shape=jax.ShapeDtypeStruct((n_rows, D), F32),
        compiler_params=_cp("arbitrary"),
        name="final_norm",
    )(x_all, g)


def _rope_tables():
    pos = jnp.arange(DEC_SEQ)
    row = (pos // GRID_W).astype(F32)
    col = (pos % GRID_W).astype(F32)
    n_freq = HD // 4
    inv = ROPE_THETA ** (-jnp.arange(n_freq, dtype=F32) / n_freq)
    ang = jnp.concatenate([row[:, None] * inv, col[:, None] * inv], axis=-1)
    cos = jnp.repeat(jnp.cos(ang), 2, axis=-1)
    sin = jnp.repeat(jnp.sin(ang), 2, axis=-1) * jnp.tile(jnp.array([-1.0, 1.0], F32), HD // 2)
    nblk = DEC_SEQ // QKV_TB
    cos_tab = jnp.concatenate([jnp.ones((1, QKV_TB, HD), F32), cos.reshape(nblk, QKV_TB, HD)], axis=0)
    sin_tab = jnp.concatenate([jnp.zeros((1, QKV_TB, HD), F32), sin.reshape(nblk, QKV_TB, HD)], axis=0)
    return cos_tab, sin_tab


def kernel(x_prompt, x_sample, state_gla, cache_k, cache_v, c, c_ctx, w_mod, b_mod, norm1_g, norm2_g,
           w_in_even, w_gate_up, b_gate_up, gla_norm_g, w_spatial, b_spatial, w_out_even,
           w_in_odd, q_norm_g, k_norm_g, w_out_odd, w_router_group, b_router_group,
           w_router_expert, b_router_expert, w_exp_gate, w_exp_up, w_exp_down, final_norm_g):
    x_all = None
    cond8 = jnp.concatenate([c_ctx[None], c, jnp.zeros((3, D), F32)], axis=0)
    mod = _modulation(cond8, w_mod, b_mod)
    cos_tab, sin_tab = _rope_tables()
    zero_state = jnp.zeros((1, 2, QK_W, GLA_DV), F32)
    state_in = state_gla.reshape(DEC_BATCH, -1, 2, QK_W, GLA_DV)
    cache_k2 = cache_k.reshape(DEC_BATCH, -1, SEQ, KV_W)
    cache_v2 = cache_v.reshape(DEC_BATCH, -1, SEQ, KV_W)

    gla_states, ctx_k, ctx_v = [], [], []
    for l in range(DEPTH):
        i = l // 2
        n1g = norm1_g[l][None]
        if l % 2 == 0:
            w = w_in_even[i]
            win = jnp.concatenate([w[:, :1536], w[:, 1568:], w[:, 1536:1568], jnp.zeros((D, 96), F32)],
                                  axis=1).astype(BF16)
            wgu = jnp.zeros((128, 2 * QK_W), F32)
            wgu = wgu.at[0:GLA_RANK, 0:QK_W].set(w_gate_up[i, 0])
            wgu = wgu.at[GLA_RANK:2 * GLA_RANK, QK_W:].set(w_gate_up[i, 1]).astype(BF16)
            bgu = b_gate_up[i].reshape(1, 2 * QK_W)
            args = (mod[l], n1g, win, wgu, bgu, gla_norm_g[i][None], w_spatial[i].astype(BF16),
                    b_spatial[i].T, w_out_even[i].astype(BF16))
            first = l == 0
            x_all, st = _even_mixer(x_all, *args, zero_state, latent=False,
                                    x_first=x_prompt.reshape(N_CTX, D) if first else None)
            gla_states.append(st)
            x_all, _ = _even_mixer(x_all, *args, state_in[:, i], latent=True,
                                   x_first=x_sample.reshape(N_LAT, D) if first else None)
        else:
            q, k, v = _qkv(x_all, mod[l], n1g, w_in_odd[i].astype(BF16), q_norm_g[i][None],
                           k_norm_g[i][None], cos_tab, sin_tab)
            ctx_k.append(k[:N_CTX].reshape(BATCH, SEQ, ATT_KV, HD))
            ctx_v.append(v[:N_CTX].reshape(BATCH, SEQ, ATT_KV, HD))
            wout = w_out_odd[i].astype(BF16)
            x_all = _attention(x_all, mod[l], q, k, v, wout)
            x_all = _attention(x_all, mod[l], q, k, v, wout, cache_k2, cache_v2, layer_i=i)
        wr = jnp.concatenate([w_router_expert[l], w_router_group[l],
                              jnp.zeros((D, 128 - N_EXP - MOE_GROUPS), F32)], axis=1)
        br = jnp.concatenate([b_router_expert[l], b_router_group[l],
                              jnp.zeros((128 - N_EXP - MOE_GROUPS,), F32)])[None]
        x_all = _moe(x_all, mod[l], norm2_g[l][None], wr, br, w_exp_gate, w_exp_up, w_exp_down, l)

    fg = final_norm_g[None]
    y_prompt = _final_norm(x_all, fg, 0, N_CTX).reshape(BATCH, SEQ, D)
    y_sample = _final_norm(x_all, fg, N_CTX // 512, N_LAT).reshape(DEC_BATCH, DEC_SEQ, D)
    new_state = jnp.stack(gla_states, axis=1).reshape(BATCH, -1, 2, GLA_HEADS, GLA_DK, GLA_DV)
    return (y_prompt, y_sample, new_state, jnp.stack(ctx_k, axis=1), jnp.stack(ctx_v, axis=1))
```

```python
import functools

import jax
import jax.numpy as jnp
import numpy as np
from jax import lax
from jax.experimental import pallas as pl
from jax.experimental.pallas import tpu as pltpu

F32 = jnp.float32
BF16 = jnp.bfloat16

D = 1024
BATCH, SEQ = 16, 256
DEC_BATCH, DEC_SEQ = 4, 1024
N_CTX = BATCH * SEQ
N_LAT = DEC_BATCH * DEC_SEQ
N_TOK = N_CTX + N_LAT
DEPTH = 4
EPS = 1e-6
GRID_W = 64
ROPE_THETA = 10000.0

GLA_HEADS, GLA_DK, GLA_DV, GLA_RANK, GLA_CHUNK, GLA_TAU = 4, 64, 128, 16, 128, 16.0
QK_W = GLA_HEADS * GLA_DK
V_W = GLA_HEADS * GLA_DV
GMLP_GROUPS, GMLP_DIM, GMLP_CHUNK = 4, 128, 128
GMLP_W = GMLP_GROUPS * GMLP_DIM
C_Q, C_K, C_V, C_G, C_U, C_VG, C_A = 0, 256, 512, 1024, 1536, 2048, 2560
EVEN_PACK = 2688

ATT_HEADS, ATT_KV, HD = 8, 2, 128
ATT_G = ATT_HEADS // ATT_KV
Q_W = ATT_HEADS * HD
KV_W = ATT_KV * HD

MOE_GROUPS, MOE_PER_GROUP = 4, 8
N_EXP = MOE_GROUPS * MOE_PER_GROUP
D_EXP = D // 4
NEG = -1e30

VMEM_LIMIT = 56 * 1024 * 1024


def _cp(*sem):
    return pltpu.CompilerParams(dimension_semantics=sem, vmem_limit_bytes=VMEM_LIMIT)


def _dot(a, b):
    return jnp.dot(a.astype(BF16), b.astype(BF16), preferred_element_type=F32)


def _dot_nt(a, b):
    return lax.dot_general(a.astype(BF16), b.astype(BF16), (((1,), (1,)), ((), ())),
                           preferred_element_type=F32)


def _dot_tn(a, b):
    return lax.dot_general(a.astype(BF16), b.astype(BF16), (((0,), (0,)), ((), ())),
                           preferred_element_type=F32)


def _rms(x, g):
    return x * lax.rsqrt(jnp.mean(x * x, axis=-1, keepdims=True) + EPS) * g


def _silu(x):
    return x * jax.nn.sigmoid(x)


def _gelu(x):
    return 0.5 * x * (1.0 + jnp.tanh(np.sqrt(2.0 / np.pi).astype(np.float32) * (x + 0.044715 * (x * x * x))))


def _log_sigmoid(z):
    return jnp.minimum(z, 0.0) - jnp.log(1.0 + jnp.exp(-jnp.abs(z)))


def _rows_to_tiles(ref, x):
    rows = x.shape[0]
    for j in range(D // 128):
        ref[pl.ds(j, rows, stride=8), :] = x[:, j * 128:(j + 1) * 128]


def _tiles_to_rows(ref, rows):
    return jnp.concatenate([ref[pl.ds(j, rows, stride=8), :] for j in range(D // 128)], axis=1)


def _tile_of(ref, row8):
    return ref.at[pl.ds(pl.multiple_of(row8, 8), 8), :]


def _resident(shape, index_map):
    return pl.BlockSpec(shape, index_map, pipeline_mode=pl.Buffered(1))


def _mod_kernel(cond_ref, w_ref, b_ref, o_ref):
    c = cond_ref[...]
    o_ref[...] = jnp.dot(_silu(c), w_ref[...], precision=lax.Precision.HIGHEST,
                         preferred_element_type=F32) + b_ref[...]


def _modulation(cond8, w_mod, b_mod):
    tn = 1024
    out = pl.pallas_call(
        _mod_kernel,
        grid=(DEPTH, 6 * D // tn),
        in_specs=[
            pl.BlockSpec((8, D), lambda l, j: (0, 0)),
            pl.BlockSpec((None, D, tn), lambda l, j: (l, 0, j)),
            pl.BlockSpec((None, 1, tn), lambda l, j: (l, 0, j)),
        ],
        out_specs=pl.BlockSpec((None, 8, tn), lambda l, j: (l, 0, j)),
        out_shape=jax.ShapeDtypeStruct((DEPTH, 8, 6 * D), F32),
        compiler_params=_cp("arbitrary", "arbitrary"),
        name="adaln_mod",
    )(cond8, w_mod, b_mod.reshape(DEPTH, 1, 6 * D))
    return out.reshape(DEPTH, 8, 6, D)


def _even_kernel(x_ref, mod_ref, n1g_ref, win_ref, wgu_ref, bgu_ref, glag_ref, ws_ref, bs_ref,
                 wout_ref, s0_ref, xo_ref, st_ref, proj, la, o_f, o_b, st_scr, *, T):
    n_chunks = T // GLA_CHUNK
    shift, scale, gate = mod_ref[0:1, :], mod_ref[1:2, :], mod_ref[2:3, :]
    RB = 128
    PROJ_RB = 256

    def proj_body(r, carry):
        r0 = pl.multiple_of(r * PROJ_RB, PROJ_RB)
        h = _rms(x_ref[pl.ds(r0, PROJ_RB), :], n1g_ref[...]) * (1.0 + scale) + shift
        p = _dot(h, win_ref[...])
        proj[pl.ds(r0, PROJ_RB), :] = p
        z = _dot(p[:, C_A:C_A + 128], wgu_ref[...]) + bgu_ref[...]
        la[pl.ds(r0, PROJ_RB), :] = _log_sigmoid(z) * (1.0 / GLA_TAU)
        return carry

    lax.fori_loop(0, T // PROJ_RB, proj_body, 0)

    st_scr[0] = s0_ref[0].T
    st_scr[1] = s0_ref[1].T

    ci = lax.broadcasted_iota(jnp.int32, (GLA_CHUNK, GLA_CHUNK), 0)
    cj = lax.broadcasted_iota(jnp.int32, (GLA_CHUNK, GLA_CHUNK), 1)
    tri = (jnp.where(ci >= cj, 1.0, 0.0).astype(BF16), jnp.where(ci <= cj, 1.0, 0.0).astype(BF16))
    ai = lax.broadcasted_iota(jnp.int32, (GLA_HEADS * GLA_CHUNK, GLA_CHUNK), 0) % GLA_CHUNK
    aj = lax.broadcasted_iota(jnp.int32, (GLA_HEADS * GLA_CHUNK, GLA_CHUNK), 1)
    amask = (ai >= aj, ai <= aj)
    lane_head = lax.broadcasted_iota(jnp.int32, (1, QK_W), 1) // GLA_DK
    hmask = [jnp.where(lane_head == h, 1.0, 0.0) for h in range(GLA_HEADS)]

    def chunk_body(i, carry):
        for d in range(2):
            c = i if d == 0 else n_chunks - 1 - i
            r0 = pl.multiple_of(c * GLA_CHUNK, GLA_CHUNK)
            q = proj[pl.ds(r0, GLA_CHUNK), C_Q:C_Q + QK_W] * (GLA_DK ** -0.5)
            k = proj[pl.ds(r0, GLA_CHUNK), C_K:C_K + QK_W]
            v = proj[pl.ds(r0, GLA_CHUNK), C_V:C_V + V_W]
            lac = la[pl.ds(r0, GLA_CHUNK), d * QK_W:(d + 1) * QK_W]
            hi = lac.astype(BF16)
            lo = (lac - hi.astype(F32)).astype(BF16)
            b = (jnp.dot(tri[d], hi, preferred_element_type=F32)
                 + jnp.dot(tri[d], lo, preferred_element_type=F32))
            bend = b[GLA_CHUNK - 1:GLA_CHUNK, :] if d == 0 else b[0:1, :]
            qe = q * jnp.exp(b)
            ke = k * jnp.exp(-b)
            kd = k * jnp.exp(bend - b)
            st = st_scr[d]
            qstack = jnp.concatenate([qe * hmask[h] for h in range(GLA_HEADS)], axis=0).astype(BF16)
            att = jnp.where(amask[d], _dot_nt(qstack, ke), 0.0)
            inter = _dot_nt(qstack, st)
            outs = []
            for h in range(GLA_HEADS):
                rows = slice(h * GLA_CHUNK, (h + 1) * GLA_CHUNK)
                outs.append(_dot(att[rows], v[:, h * GLA_DV:(h + 1) * GLA_DV]) + inter[rows])
            o = jnp.concatenate(outs, axis=1)
            if d == 0:
                o_f[pl.ds(r0, GLA_CHUNK), :] = o
            else:
                o_b[pl.ds(r0, GLA_CHUNK), :] = o
            vstack = jnp.concatenate([v[:, h * GLA_DV:(h + 1) * GLA_DV] for h in range(GLA_HEADS)], axis=0)
            kstack = jnp.concatenate([kd * hmask[h] for h in range(GLA_HEADS)], axis=0)
            st_scr[d] = st * jnp.exp(bend) + _dot_tn(vstack, kstack)
        return carry

    lax.fori_loop(0, n_chunks, chunk_body, 0, unroll=2)
    st_ref[0] = st_scr[0].T
    st_ref[1] = st_scr[1].T

    def out_body(r, carry):
        r0 = pl.multiple_of(r * RB, RB)
        osum = o_f[pl.ds(r0, RB), :] + o_b[pl.ds(r0, RB), :]
        g = proj[pl.ds(r0, RB), C_G:C_G + V_W]
        u = proj[pl.ds(r0, RB), C_U:C_U + GMLP_W]
        vg = _gelu(proj[pl.ds(r0, RB), C_VG:C_VG + GMLP_W])
        parts = []
        for h in range(GLA_HEADS):
            oh = osum[:, h * GLA_DV:(h + 1) * GLA_DV]
            parts.append(_rms(oh, glag_ref[...]) * _silu(g[:, h * GLA_DV:(h + 1) * GLA_DV]))
        for gi in range(GMLP_GROUPS):
            vc = vg[:, gi * GMLP_DIM:(gi + 1) * GMLP_DIM]
            vc = vc - jnp.mean(vc, axis=-1, keepdims=True)
            vn = vc * lax.rsqrt(jnp.mean(vc * vc, axis=-1, keepdims=True) + EPS)
            sg = _dot(ws_ref[gi], vn) + bs_ref[:, gi:gi + 1]
            parts.append(_gelu(u[:, gi * GMLP_DIM:(gi + 1) * GMLP_DIM]) * sg)
        mix = jnp.concatenate(parts, axis=1)
        y = _dot(mix, wout_ref[...])
        xo_ref[pl.ds(r0, RB), :] = x_ref[pl.ds(r0, RB), :] + gate * y
        return carry

    lax.fori_loop(0, T // RB, out_body, 0)


def _even_mixer(x_all, mod_l, n1g, win, wgu, bgu, glag, ws, bs, wout, s0, *, latent, x_first=None):
    if latent:
        T, nseq, blk0 = DEC_SEQ, DEC_BATCH, N_CTX // DEC_SEQ
        cond = lambda i: 1 + i
        s0_spec = pl.BlockSpec((None, 2, QK_W, GLA_DV), lambda i: (i, 0, 0, 0))
    else:
        T, nseq, blk0 = SEQ, BATCH, 0
        cond = lambda i: 0
        s0_spec = pl.BlockSpec((None, 2, QK_W, GLA_DV), lambda i: (0, 0, 0, 0))
    const2 = lambda i: (0, 0)
    body = functools.partial(_even_kernel, T=T)
    x_spec = pl.BlockSpec((T, D), lambda i: (blk0 + i, 0))
    if x_first is None:
        lead_specs, lead_args, aliases = [x_spec], (x_all,), {0: 0}
    elif x_all is None:
        lead_specs, lead_args, aliases = [pl.BlockSpec((T, D), lambda i: (i, 0))], (x_first,), {}
    else:
        lead_specs = [pl.BlockSpec(memory_space=pl.ANY), pl.BlockSpec((T, D), lambda i: (i, 0))]
        lead_args, aliases = (x_all, x_first), {0: 0}
        body = lambda dst_ref, *refs: _even_kernel(*refs, T=T)
    x_new, states = pl.pallas_call(
        body,
        grid=(nseq,),
        in_specs=lead_specs + [
            pl.BlockSpec((None, 6, D), lambda i: (cond(i), 0, 0)),
            _resident((1, D), const2),
            _resident((D, EVEN_PACK), const2),
            _resident((128, 2 * QK_W), const2),
            _resident((1, 2 * QK_W), const2),
            _resident((1, GLA_DV), const2),
            _resident((GMLP_GROUPS, GMLP_CHUNK, GMLP_CHUNK), lambda i: (0, 0, 0)),
            _resident((GMLP_CHUNK, GMLP_GROUPS), const2),
            _resident((D, D), const2),
            s0_spec,
        ],
        out_specs=[
            x_spec,
            pl.BlockSpec((None, 2, QK_W, GLA_DV), lambda i: (i, 0, 0, 0)),
        ],
        out_shape=[
            jax.ShapeDtypeStruct((N_TOK, D), F32),
            jax.ShapeDtypeStruct((nseq, 2, QK_W, GLA_DV), F32),
        ],
        scratch_shapes=[
            pltpu.VMEM((T, EVEN_PACK), F32),
            pltpu.VMEM((T, 2 * QK_W), F32),
            pltpu.VMEM((T, V_W), F32),
            pltpu.VMEM((T, V_W), F32),
            pltpu.VMEM((2, GLA_DV, QK_W), F32),
        ],
        input_output_aliases=aliases,
        compiler_params=_cp("arbitrary"),
        name="even_mixer_latent" if latent else "even_mixer_context",
    )(*lead_args, mod_l, n1g, win, wgu, bgu, glag, ws, bs, wout, s0)
    return x_new, states


QKV_TB = 512


def _qkv_kernel(x_ref, mod_ref, n1g_ref, win_ref, gq_ref, gk_ref, cos_ref, sin_ref, q_ref, k_ref, v_ref):
    shift, scale = mod_ref[0:1, :], mod_ref[1:2, :]
    h = _rms(x_ref[...], n1g_ref[...]) * (1.0 + scale) + shift
    p = _dot(h, win_ref[...])
    cos, sin = cos_ref[...], sin_ref[...]
    even_lane = lax.broadcasted_iota(jnp.int32, (1, HD), 1) % 2 == 0

    def rope(xn):
        swapped = jnp.where(even_lane, pltpu.roll(xn, HD - 1, axis=1), pltpu.roll(xn, 1, axis=1))
        return xn * cos + swapped * sin

    def emit(rotate):
        for hh in range(ATT_HEADS):
            qn = _rms(p[:, hh * HD:(hh + 1) * HD], gq_ref[...])
            q_ref[:, hh * HD:(hh + 1) * HD] = (rotate(qn) * (HD ** -0.5)).astype(BF16)
        for hh in range(ATT_KV):
            k_ref[:, hh * HD:(hh + 1) * HD] = rotate(_rms(p[:, Q_W + hh * HD:Q_W + (hh + 1) * HD], gk_ref[...]))

    is_latent = pl.program_id(0) >= N_CTX // QKV_TB

    @pl.when(is_latent)
    def _():
        emit(rope)

    @pl.when(jnp.logical_not(is_latent))
    def _():
        emit(lambda xn: xn)

    v_ref[...] = p[:, Q_W + KV_W:]


def _qkv(x_all, mod_l, n1g, win, gq, gk, cos_tab, sin_tab):
    nb_ctx = N_CTX // QKV_TB
    per_seq = DEC_SEQ // QKV_TB
    cond = lambda i: jnp.where(i < nb_ctx, 0, 1 + (i - nb_ctx) // per_seq)
    tab = lambda i: jnp.where(i < nb_ctx, 0, 1 + (i - nb_ctx) % per_seq)
    const2 = lambda i: (0, 0)
    return pl.pallas_call(
        _qkv_kernel,
        grid=(N_TOK // QKV_TB,),
        in_specs=[
            pl.BlockSpec((QKV_TB, D), lambda i: (i, 0)),
            pl.BlockSpec((None, 6, D), lambda i: (cond(i), 0, 0)),
            _resident((1, D), const2),
            _resident((D, Q_W + 2 * KV_W), const2),
            _resident((1, HD), const2),
            _resident((1, HD), const2),
            pl.BlockSpec((None, QKV_TB, HD), lambda i: (tab(i), 0, 0)),
            pl.BlockSpec((None, QKV_TB, HD), lambda i: (tab(i), 0, 0)),
        ],
        out_specs=[
            pl.BlockSpec((QKV_TB, Q_W), lambda i: (i, 0)),
            pl.BlockSpec((QKV_TB, KV_W), lambda i: (i, 0)),
            pl.BlockSpec((QKV_TB, KV_W), lambda i: (i, 0)),
        ],
        out_shape=[
            jax.ShapeDtypeStruct((N_TOK, Q_W), BF16),
            jax.ShapeDtypeStruct((N_TOK, KV_W), F32),
            jax.ShapeDtypeStruct((N_TOK, KV_W), F32),
        ],
        compiler_params=_cp("arbitrary"),
        name="odd_qkv",
    )(x_all, mod_l, n1g, win, gq, gk, cos_tab, sin_tab)


ATT_TQ = 256


def _attn_kernel(*refs, n_kv):
    q_ref = refs[0]
    kv_refs = refs[1:1 + 2 * n_kv]
    x_ref, mod_ref, wout_ref, xo_ref, att_scr = refs[1 + 2 * n_kv:]
    gate = mod_ref[2:3, :]
    for kh in range(ATT_KV):
        ks = [kv_refs[2 * s][:, kh * HD:(kh + 1) * HD].astype(BF16) for s in range(n_kv)]
        vs = [jnp.concatenate([kv_refs[2 * s + 1][:, kh * HD:(kh + 1) * HD].astype(BF16),
                               jnp.ones((kv_refs[2 * s + 1].shape[0], HD), BF16)], axis=1) for s in range(n_kv)]
        for g in range(ATT_G):
            hh = kh * ATT_G + g
            qh = q_ref[:, hh * HD:(hh + 1) * HD]
            ss = [_dot_nt(qh, kk) for kk in ks]
            m = ss[0].max(axis=-1, keepdims=True)
            for s in ss[1:]:
                m = jnp.maximum(m, s.max(axis=-1, keepdims=True))
            o = _dot(jnp.exp(ss[0] - m), vs[0])
            for s, vv in zip(ss[1:], vs[1:]):
                o = o + _dot(jnp.exp(s - m), vv)
            att_scr[:, hh * HD:(hh + 1) * HD] = o[:, :HD] / o[:, HD:HD + 1]
    y = _dot(att_scr[...], wout_ref[...])
    xo_ref[...] = x_ref[...] + gate * y


def _attention(x_all, mod_l, q, k, v, wout, cache_k=None, cache_v=None, layer_i=0):
    latent = cache_k is not None
    const2 = lambda *a: (0, 0)
    if latent:
        nq = DEC_SEQ // ATT_TQ
        row_blk = lambda b, j: (N_CTX // ATT_TQ + b * nq + j, 0)
        grid = (DEC_BATCH, nq)
        kv_specs = [
            pl.BlockSpec((None, None, SEQ, KV_W), lambda b, j: (b, layer_i, 0, 0)),
            pl.BlockSpec((None, None, SEQ, KV_W), lambda b, j: (b, layer_i, 0, 0)),
            pl.BlockSpec((DEC_SEQ, KV_W), lambda b, j: (N_CTX // DEC_SEQ + b, 0)),
            pl.BlockSpec((DEC_SEQ, KV_W), lambda b, j: (N_CTX // DEC_SEQ + b, 0)),
        ]
        kv_args = (cache_k, cache_v, k, v)
        mod_spec = pl.BlockSpec((None, 6, D), lambda b, j: (1 + b, 0, 0))
        sem = ("arbitrary", "arbitrary")
        n_kv = 2
    else:
        row_blk = lambda i: (i, 0)
        grid = (BATCH,)
        kv_specs = [pl.BlockSpec((SEQ, KV_W), row_blk), pl.BlockSpec((SEQ, KV_W), row_blk)]
        kv_args = (k, v)
        mod_spec = pl.BlockSpec((None, 6, D), lambda i: (0, 0, 0))
        sem = ("arbitrary",)
        n_kv = 1
    n_in = 1 + len(kv_args)
    return pl.pallas_call(
        functools.partial(_attn_kernel, n_kv=n_kv),
        grid=grid,
        in_specs=[pl.BlockSpec((ATT_TQ, Q_W), row_blk)] + kv_specs + [
            pl.BlockSpec((ATT_TQ, D), row_blk),
            mod_spec,
            _resident((D, D), const2),
        ],
        out_specs=pl.BlockSpec((ATT_TQ, D), row_blk),
        out_shape=jax.ShapeDtypeStruct((N_TOK, D), F32),
        scratch_shapes=[pltpu.VMEM((ATT_TQ, Q_W), F32)],
        input_output_aliases={n_in: 0},
        compiler_params=_cp(*sem),
        name="attention_latent" if latent else "attention_context",
    )(q, *kv_args, x_all, mod_l, wout)


ROUTE_TB = 512
HALF_TOK = N_TOK // 2
M_E1, M_E2, M_G1, M_G2, M_R1, M_R2 = 0, 1, 2, 3, 4, 5


def _router_kernel(x_ref, mod_ref, n2g_ref, whi_ref, wlo_ref, br_ref, h_ref, metat_ref, cnt_ref, run):
    @pl.when(pl.program_id(0) % (HALF_TOK // ROUTE_TB) == 0)
    def _():
        run[...] = jnp.zeros_like(run)

    shift, scale = mod_ref[3:4, :], mod_ref[4:5, :]
    h = _rms(x_ref[...], n2g_ref[...]) * (1.0 + scale) + shift
    _rows_to_tiles(h_ref, h)
    h_hi = h.astype(BF16)
    h_lo = (h - h_hi.astype(F32)).astype(BF16)
    dot = functools.partial(jnp.dot, preferred_element_type=F32)
    logits = dot(h_hi, whi_ref[...]) + dot(h_lo, whi_ref[...]) + dot(h_hi, wlo_ref[...]) + br_ref[...]
    lane = lax.broadcasted_iota(jnp.int32, logits.shape, 1).astype(F32)
    big = 1e4

    def first_argmax(vals):
        m = vals.max(axis=-1, keepdims=True)
        return m, jnp.where(vals == m, lane, big).min(axis=-1, keepdims=True)

    gl = jnp.where((lane >= N_EXP) & (lane < N_EXP + MOE_GROUPS), logits, NEG)
    gmax, glane = first_argmax(gl)
    g_p = 1.0 / jnp.exp(gl - gmax).sum(axis=-1, keepdims=True)
    lo = (glane - N_EXP) * MOE_PER_GROUP
    el = jnp.where((lane >= lo) & (lane < lo + MOE_PER_GROUP), logits, NEG)
    m1, i1 = first_argmax(el)
    m2, i2 = first_argmax(jnp.where(lane == i1, NEG, el))
    t = jnp.exp(m2 - m1)
    w1 = 1.0 / (1.0 + t)
    sel1, sel2 = lane == i1, lane == i2
    onehot = jnp.where(sel1 | sel2, 1.0, 0.0)
    ri = lax.broadcasted_iota(jnp.int32, (ROUTE_TB, ROUTE_TB), 0)
    rj = lax.broadcasted_iota(jnp.int32, (ROUTE_TB, ROUTE_TB), 1)
    before = _dot(jnp.where(ri > rj, 1.0, 0.0), onehot) + run[...]
    r1 = jnp.where(sel1, before, 0.0).sum(axis=-1, keepdims=True)
    r2 = jnp.where(sel2, before, 0.0).sum(axis=-1, keepdims=True)
    run[...] += onehot.sum(axis=0, keepdims=True)
    cnt_ref[...] = run[...]
    meta = jnp.zeros_like(logits)
    for j, val in enumerate([i1, i2, w1 * g_p, (t * w1) * g_p, r1, r2]):
        meta = jnp.where(lane == j, val, meta)
    metat_ref[...] = meta.T[0:8, :]


def _router(x_all, mod_l, n2g, wr, br):
    w_hi = wr.astype(BF16)
    wr_lo = (wr - w_hi.astype(F32)).astype(BF16)
    nb_ctx = N_CTX // ROUTE_TB
    per_seq = DEC_SEQ // ROUTE_TB
    cond = lambda i: jnp.where(i < nb_ctx, 0, 1 + (i - nb_ctx) // per_seq)
    const2 = lambda i: (0, 0)
    return pl.pallas_call(
        _router_kernel,
        grid=(N_TOK // ROUTE_TB,),
        in_specs=[
            pl.BlockSpec((ROUTE_TB, D), lambda i: (i, 0)),
            pl.BlockSpec((None, 6, D), lambda i: (cond(i), 0, 0)),
            _resident((1, D), const2),
            _resident((D, 128), const2),
            _resident((D, 128), const2),
            _resident((1, 128), const2),
        ],
        out_specs=[
            pl.BlockSpec((ROUTE_TB * 8, 128), lambda i: (i, 0)),
            pl.BlockSpec((8, ROUTE_TB), lambda i: (0, i)),
            pl.BlockSpec((None, 1, 128), lambda i: (i // (HALF_TOK // ROUTE_TB), 0, 0)),
        ],
        out_shape=[
            jax.ShapeDtypeStruct((N_TOK * 8, 128), F32),
            jax.ShapeDtypeStruct((8, N_TOK), F32),
            jax.ShapeDtypeStruct((2, 1, 128), F32),
        ],
        scratch_shapes=[pltpu.VMEM((1, 128), F32)],
        compiler_params=_cp("arbitrary"),
        name="moe_router",
    )(x_all, mod_l, n2g, w_hi, wr_lo, br)


EXP_TM = 128
N_ASSIGN = 2 * N_TOK
N_GROUPS = 2 * N_EXP
MAX_TILES = N_ASSIGN // EXP_TM + N_GROUPS
N_SORTED = MAX_TILES * EXP_TM
ORDER_BLK = 2048
CODE_PLANE = 2 * HALF_TOK
CODE_MASK = 8 * CODE_PLANE - 1
DUMMY8 = HALF_TOK * 8


def _order_kernel(pos1_ref, pos2_ref, pad_lo_ref, pad_hi_ref, src_ref):
    i = pl.program_id(0)
    local = (i % (HALF_TOK // ORDER_BLK)) * ORDER_BLK

    def body(t, carry):
        src_ref[pos1_ref[t]] = (local + t) * 8
        src_ref[pos2_ref[t]] = (local + t + CODE_PLANE) * 8
        return carry

    lax.fori_loop(0, ORDER_BLK, body, 0, unroll=16)

    @pl.when(i == 0)
    def _():
        def group(g, carry):
            def pad(p, c):
                src_ref[p] = DUMMY8
                return c
            return lax.fori_loop(pad_lo_ref[g], pad_hi_ref[g], pad, carry)

        lax.fori_loop(0, N_GROUPS, group, 0)


def _order(pos, pad_lo, pad_hi):
    return pl.pallas_call(
        _order_kernel,
        grid=(N_TOK // ORDER_BLK,),
        in_specs=[
            pl.BlockSpec((ORDER_BLK,), lambda i: (i,), memory_space=pltpu.SMEM),
            pl.BlockSpec((ORDER_BLK,), lambda i: (N_TOK // ORDER_BLK + i,), memory_space=pltpu.SMEM),
            pl.BlockSpec(memory_space=pltpu.SMEM),
            pl.BlockSpec(memory_space=pltpu.SMEM),
        ],
        out_specs=pl.BlockSpec(memory_space=pltpu.SMEM),
        out_shape=jax.ShapeDtypeStruct((N_SORTED,), jnp.int32),
        compiler_params=_cp("arbitrary"),
        name="moe_order",
    )(pos, pos, pad_lo, pad_hi)


GATE_BLK = CODE_PLANE + HALF_TOK
ACC_TOK = HALF_TOK + 64
GATHER_GROUP, ACC_GROUP = 16, 8


RES_TB = 256


def _experts_kernel(tile0_ref, ntile_ref, count_ref, src_ref, gs_ref, h_hbm, x_hbm, mod_ref, wg_ref, wu_ref, wd_ref, xo_hbm,
                    h_res, acc, xbuf, ybuf, wgb, wub, wdb, xin, xout, sem, in_sem, out_sem):
    group = pl.program_id(0)
    expert = group % N_EXP
    half = group // N_EXP
    rows0 = pl.multiple_of(half * (HALF_TOK * 8), 8)

    @pl.when(expert == 0)
    def _():
        cp = pltpu.make_async_copy(h_hbm.at[pl.ds(rows0, HALF_TOK * 8), :], h_res.at[pl.ds(0, HALF_TOK * 8), :], sem)
        cp.start()
        h_res[pl.ds(DUMMY8, 8), :] = jnp.zeros((8, 128), F32)
        xbuf[...] = jnp.zeros_like(xbuf)

        def zero(i, carry):
            acc[pl.ds(pl.multiple_of(i * 512, 512), 512), :] = jnp.zeros((512, 128), F32)
            return carry

        lax.fori_loop(0, ACC_TOK * 8 // 512, zero, 0)
        cp.wait()

    n_tiles = ntile_ref[group]

    @pl.when(n_tiles > 0)
    def _():
        wgb[...] = wg_ref[...].astype(BF16)
        wub[...] = wu_ref[...].astype(BF16)
        wdb[...] = wd_ref[...].astype(BF16)

    row0 = tile0_ref[group] * EXP_TM
    row_end = row0 + count_ref[group]

    def process(base, rows):
        live = (jnp.clip(row_end - base, 0, rows) + GATHER_GROUP - 1) // GATHER_GROUP

        def gather(g, c):
            for i in range(GATHER_GROUP):
                r = g * GATHER_GROUP + i
                xbuf[pl.ds(pl.multiple_of(r * 8, 8), 8), :] = _tile_of(h_res, src_ref[base + r] & CODE_MASK)[...]
            return c

        lax.fori_loop(0, live, gather, 0)
        x = _tiles_to_rows(xbuf, rows).astype(BF16)
        hid = _silu(_dot(x, wgb[...])) * _dot(x, wub[...])
        _rows_to_tiles(ybuf, _dot(hid, wdb[...]))

        def accumulate(g, c):
            targets, values = [], []
            for i in range(ACC_GROUP):
                r = g * ACC_GROUP + i
                code = src_ref[base + r]
                target = _tile_of(acc, code & CODE_MASK)
                targets.append(target)
                values.append(target[...] + gs_ref[code >> 3] * ybuf[pl.ds(pl.multiple_of(r * 8, 8), 8), :])
            for target, value in zip(targets, values):
                target[...] = value
            return c

        lax.fori_loop(0, live * (GATHER_GROUP // ACC_GROUP), accumulate, 0)


    def pair_body(j, carry):
        process(row0 + j * (2 * EXP_TM), 2 * EXP_TM)
        return carry

    lax.fori_loop(0, n_tiles // 2, pair_body, 0)

    @pl.when(n_tiles % 2 == 1)
    def _():
        process(row0 + (n_tiles - 1) * EXP_TM, EXP_TM)

    def x_rows(blk):
        return pl.ds(pl.multiple_of(half * HALF_TOK + blk * RES_TB, RES_TB), RES_TB)

    def load_x(blk, slot):
        return pltpu.make_async_copy(x_hbm.at[x_rows(blk), :], xin.at[slot], in_sem.at[slot])

    def store_x(blk, slot):
        return pltpu.make_async_copy(xout.at[slot], xo_hbm.at[x_rows(blk), :], out_sem.at[slot])

    def residual(blk, slot):
        cond = jnp.where(half == 0, 0, 1 + blk // (DEC_SEQ // RES_TB))
        gate = mod_ref[cond, 5:6, :]
        y = _tiles_to_rows(acc.at[pl.ds(pl.multiple_of(blk * (RES_TB * 8), RES_TB * 8), RES_TB * 8), :], RES_TB)
        xout[slot] = xin[slot] + gate * y

    @pl.when(expert == N_EXP - 1)
    def _():
        n_pairs = HALF_TOK // RES_TB // 2
        load_x(0, 0).start()

        def pair(p, carry):
            for slot in range(2):
                blk = 2 * p + slot
                if slot == 0:
                    load_x(blk + 1, 1).start()
                else:
                    @pl.when(p + 1 < n_pairs)
                    def _():
                        load_x(blk + 1, 0).start()
                load_x(blk, slot).wait()

                @pl.when(p > 0)
                def _():
                    store_x(blk - 2, slot).wait()

                residual(blk, slot)
                store_x(blk, slot).start()
            return carry

        lax.fori_loop(0, n_pairs, pair, 0)
        store_x(2 * n_pairs - 2, 0).wait()
        store_x(2 * n_pairs - 1, 1).wait()


def _experts(tile0, n_tiles, counts, src, gs, h, x_all, mod_l, wg, wu, wd, layer):
    wmap = lambda g, t0, nt, cnt, src: (layer, g % N_EXP, 0, 0)
    return pl.pallas_call(
        _experts_kernel,
        grid_spec=pltpu.PrefetchScalarGridSpec(
            num_scalar_prefetch=4,
            grid=(N_GROUPS,),
            in_specs=[
                pl.BlockSpec((GATE_BLK,), lambda g, t0, nt, cnt, src: (g // N_EXP,), memory_space=pltpu.SMEM),
                pl.BlockSpec(memory_space=pl.ANY),
                pl.BlockSpec(memory_space=pl.ANY),
                pl.BlockSpec((8, 6, D), lambda g, t0, nt, cnt, src: (0, 0, 0)),
                pl.BlockSpec((None, None, D, D_EXP), wmap),
                pl.BlockSpec((None, None, D, D_EXP), wmap),
                pl.BlockSpec((None, None, D_EXP, D), wmap),
            ],
            out_specs=pl.BlockSpec(memory_space=pl.ANY),
            scratch_shapes=[
                pltpu.VMEM((ACC_TOK * 8, 128), F32),
                pltpu.VMEM((ACC_TOK * 8, 128), F32),
                pltpu.VMEM((2 * EXP_TM * 8, 128), F32),
                pltpu.VMEM((2 * EXP_TM * 8, 128), F32),
                pltpu.VMEM((D, D_EXP), BF16),
                pltpu.VMEM((D, D_EXP), BF16),
                pltpu.VMEM((D_EXP, D), BF16),
                pltpu.VMEM((2, RES_TB, D), F32),
                pltpu.VMEM((2, RES_TB, D), F32),
                pltpu.SemaphoreType.DMA,
                pltpu.SemaphoreType.DMA((2,)),
                pltpu.SemaphoreType.DMA((2,)),
            ],
        ),
        out_shape=jax.ShapeDtypeStruct((N_TOK, D), F32),
        input_output_aliases={6: 0},
        compiler_params=_cp("arbitrary"),
        name="moe_experts",
    )(tile0, n_tiles, counts, src, gs, h, x_all, mod_l, wg, wu, wd)


def _moe(x_all, mod_l, n2g, wr, br, wg, wu, wd, layer):
    h, metat, cnt = _router(x_all, mod_l, n2g, wr, br)
    counts = cnt[:, 0, :N_EXP].astype(jnp.int32).reshape(N_GROUPS)
    padded = (counts + EXP_TM - 1) // EXP_TM * EXP_TM
    ends = jnp.cumsum(padded)
    offs = ends - padded
    rec = metat.astype(jnp.int32)
    half = (jnp.arange(N_TOK, dtype=jnp.int32) // HALF_TOK)[None, :]
    group = rec[M_E1:M_E2 + 1] + N_EXP * half
    is_group = group[None] == jnp.arange(N_GROUPS, dtype=jnp.int32)[:, None, None]
    pos = jnp.sum(jnp.where(is_group, offs[:, None, None], 0), axis=0) + rec[M_R1:M_R2 + 1]
    live_end = offs + (counts + GATHER_GROUP - 1) // GATHER_GROUP * GATHER_GROUP
    src = _order(pos.reshape(N_ASSIGN), offs + counts, live_end)
    g12 = metat[M_G1:M_G2 + 1].reshape(2, 2, HALF_TOK)
    gates = jnp.concatenate([g12[0], jnp.zeros((2, CODE_PLANE - HALF_TOK), F32), g12[1]], axis=1)
    return _experts(offs // EXP_TM, padded // EXP_TM, counts, src, gates.reshape(2 * GATE_BLK), h, x_all, mod_l,
                    wg, wu, wd, layer)


def _final_kernel(x_ref, g_ref, o_ref):
    o_ref[...] = _rms(x_ref[...], g_ref[...])


def _final_norm(x_all, g, blk0, n_rows):
    tb = 512
    return pl.pallas_call(
        _final_kernel,
        grid=(n_rows // tb,),
        in_specs=[pl.BlockSpec((tb, D), lambda i: (blk0 + i, 0)), _resident((1, D), lambda i: (0, 0))],
        out_specs=pl.BlockSpec((tb, D), lambda i: (i, 0)),
        out_shape=jax.ShapeDtypeStruct((n_rows, D), F32),
        compiler_params=_cp("arbitrary"),
        name="final_norm",
    )(x_all, g)


def _rope_tables():
    pos = jnp.arange(DEC_SEQ)
    row = (pos // GRID_W).astype(F32)
    col = (pos % GRID_W).astype(F32)
    n_freq = HD // 4
    inv = ROPE_THETA ** (-jnp.arange(n_freq, dtype=F32) / n_freq)
    ang = jnp.concatenate([row[:, None] * inv, col[:, None] * inv], axis=-1)
    cos = jnp.repeat(jnp.cos(ang), 2, axis=-1)
    sin = jnp.repeat(jnp.sin(ang), 2, axis=-1) * jnp.tile(jnp.array([-1.0, 1.0], F32), HD // 2)
    nblk = DEC_SEQ // QKV_TB
    cos_tab = jnp.concatenate([jnp.ones((1, QKV_TB, HD), F32), cos.reshape(nblk, QKV_TB, HD)], axis=0)
    sin_tab = jnp.concatenate([jnp.zeros((1, QKV_TB, HD), F32), sin.reshape(nblk, QKV_TB, HD)], axis=0)
    return cos_tab, sin_tab


def kernel(x_prompt, x_sample, state_gla, cache_k, cache_v, c, c_ctx, w_mod, b_mod, norm1_g, norm2_g,
           w_in_even, w_gate_up, b_gate_up, gla_norm_g, w_spatial, b_spatial, w_out_even,
           w_in_odd, q_norm_g, k_norm_g, w_out_odd, w_router_group, b_router_group,
           w_router_expert, b_router_expert, w_exp_gate, w_exp_up, w_exp_down, final_norm_g):
    x_all = None
    cond8 = jnp.concatenate([c_ctx[None], c, jnp.zeros((3, D), F32)], axis=0)
    mod = _modulation(cond8, w_mod, b_mod)
    cos_tab, sin_tab = _rope_tables()
    zero_state = jnp.zeros((1, 2, QK_W, GLA_DV), F32)
    state_in = state_gla.reshape(DEC_BATCH, -1, 2, QK_W, GLA_DV)
    cache_k2 = cache_k.reshape(DEC_BATCH, -1, SEQ, KV_W)
    cache_v2 = cache_v.reshape(DEC_BATCH, -1, SEQ, KV_W)

    gla_states, ctx_k, ctx_v = [], [], []
    for l in range(DEPTH):
        i = l // 2
        n1g = norm1_g[l][None]
        if l % 2 == 0:
            w = w_in_even[i]
            win = jnp.concatenate([w[:, :1536], w[:, 1568:], w[:, 1536:1568], jnp.zeros((D, 96), F32)],
                                  axis=1).astype(BF16)
            wgu = jnp.zeros((128, 2 * QK_W), F32)
            wgu = wgu.at[0:GLA_RANK, 0:QK_W].set(w_gate_up[i, 0])
            wgu = wgu.at[GLA_RANK:2 * GLA_RANK, QK_W:].set(w_gate_up[i, 1]).astype(BF16)
            bgu = b_gate_up[i].reshape(1, 2 * QK_W)
            args = (mod[l], n1g, win, wgu, bgu, gla_norm_g[i][None], w_spatial[i].astype(BF16),
                    b_spatial[i].T, w_out_even[i].astype(BF16))
            first = l == 0
            x_all, st = _even_mixer(x_all, *args, zero_state, latent=False,
                                    x_first=x_prompt.reshape(N_CTX, D) if first else None)
            gla_states.append(st)
            x_all, _ = _even_mixer(x_all, *args, state_in[:, i], latent=True,
                                   x_first=x_sample.reshape(N_LAT, D) if first else None)
        else:
            q, k, v = _qkv(x_all, mod[l], n1g, w_in_odd[i].astype(BF16), q_norm_g[i][None],
                           k_norm_g[i][None], cos_tab, sin_tab)
            ctx_k.append(k[:N_CTX].reshape(BATCH, SEQ, ATT_KV, HD))
            ctx_v.append(v[:N_CTX].reshape(BATCH, SEQ, ATT_KV, HD))
            wout = w_out_odd[i].astype(BF16)
            x_all = _attention(x_all, mod[l], q, k, v, wout)
            x_all = _attention(x_all, mod[l], q, k, v, wout, cache_k2, cache_v2, layer_i=i)
        wr = jnp.concatenate([w_router_expert[l], w_router_group[l],
                              jnp.zeros((D, 128 - N_EXP - MOE_GROUPS), F32)], axis=1)
        br = jnp.concatenate([b_router_expert[l], b_router_group[l],
                              jnp.zeros((128 - N_EXP - MOE_GROUPS,), F32)])[None]
        x_all = _moe(x_all, mod[l], norm2_g[l][None], wr, br, w_exp_gate, w_exp_up, w_exp_down, l)

    fg = final_norm_g[None]
    y_prompt = _final_norm(x_all, fg, 0, N_CTX).reshape(BATCH, SEQ, D)
    y_sample = _final_norm(x_all, fg, N_CTX // 512, N_LAT).reshape(DEC_BATCH, DEC_SEQ, D)
    new_state = jnp.stack(gla_states, axis=1).reshape(BATCH, -1, 2, GLA_HEADS, GLA_DK, GLA_DV)
    return (y_prompt, y_sample, new_state, jnp.stack(ctx_k, axis=1), jnp.stack(ctx_v, axis=1))
```

```python
import functools

import jax
import jax.numpy as jnp
import numpy as np
from jax import lax
from jax.experimental import pallas as pl
from jax.experimental.pallas import tpu as pltpu

F32 = jnp.float32
BF16 = jnp.bfloat16

D = 1024
BATCH, SEQ = 16, 256
DEC_BATCH, DEC_SEQ = 4, 1024
N_CTX = BATCH * SEQ
N_LAT = DEC_BATCH * DEC_SEQ
N_TOK = N_CTX + N_LAT
DEPTH = 4
EPS = 1e-6
GRID_W = 64
ROPE_THETA = 10000.0

GLA_HEADS, GLA_DK, GLA_DV, GLA_RANK, GLA_CHUNK, GLA_TAU = 4, 64, 128, 16, 128, 16.0
QK_W = GLA_HEADS * GLA_DK
V_W = GLA_HEADS * GLA_DV
GMLP_GROUPS, GMLP_DIM, GMLP_CHUNK = 4, 128, 128
GMLP_W = GMLP_GROUPS * GMLP_DIM
C_Q, C_K, C_V, C_G, C_U, C_VG, C_A = 0, 256, 512, 1024, 1536, 2048, 2560
EVEN_PACK = 2688

ATT_HEADS, ATT_KV, HD = 8, 2, 128
ATT_G = ATT_HEADS // ATT_KV
Q_W = ATT_HEADS * HD
KV_W = ATT_KV * HD

MOE_GROUPS, MOE_PER_GROUP = 4, 8
N_EXP = MOE_GROUPS * MOE_PER_GROUP
D_EXP = D // 4
NEG = -1e30

VMEM_LIMIT = 56 * 1024 * 1024


def _cp(*sem):
    return pltpu.CompilerParams(dimension_semantics=sem, vmem_limit_bytes=VMEM_LIMIT)


def _dot(a, b):
    return jnp.dot(a.astype(BF16), b.astype(BF16), preferred_element_type=F32)


def _dot_nt(a, b):
    return lax.dot_general(a.astype(BF16), b.astype(BF16), (((1,), (1,)), ((), ())),
                           preferred_element_type=F32)


def _dot_tn(a, b):
    return lax.dot_general(a.astype(BF16), b.astype(BF16), (((0,), (0,)), ((), ())),
                           preferred_element_type=F32)


def _rms(x, g):
    return x * lax.rsqrt(jnp.mean(x * x, axis=-1, keepdims=True) + EPS) * g


def _silu(x):
    return x * jax.nn.sigmoid(x)


def _gelu(x):
    return 0.5 * x * (1.0 + jnp.tanh(np.sqrt(2.0 / np.pi).astype(np.float32) * (x + 0.044715 * (x * x * x))))


def _log_sigmoid(z):
    return jnp.minimum(z, 0.0) - jnp.log(1.0 + jnp.exp(-jnp.abs(z)))


def _rows_to_tiles(ref, x):
    rows = x.shape[0]
    for j in range(D // 128):
        ref[pl.ds(j, rows, stride=8), :] = x[:, j * 128:(j + 1) * 128]


def _tiles_to_rows(ref, rows):
    return jnp.concatenate([ref[pl.ds(j, rows, stride=8), :] for j in range(D // 128)], axis=1)


def _tile_of(ref, row8):
    return ref.at[pl.ds(pl.multiple_of(row8, 8), 8), :]


def _resident(shape, index_map):
    return pl.BlockSpec(shape, index_map, pipeline_mode=pl.Buffered(1))


def _mod_kernel(cond_ref, w_ref, b_ref, o_ref):
    c = cond_ref[...]
    o_ref[...] = jnp.dot(_silu(c), w_ref[...], precision=lax.Precision.HIGHEST,
                         preferred_element_type=F32) + b_ref[...]


def _modulation(cond8, w_mod, b_mod):
    tn = 1024
    out = pl.pallas_call(
        _mod_kernel,
        grid=(DEPTH, 6 * D // tn),
        in_specs=[
            pl.BlockSpec((8, D), lambda l, j: (0, 0)),
            pl.BlockSpec((None, D, tn), lambda l, j: (l, 0, j)),
            pl.BlockSpec((None, 1, tn), lambda l, j: (l, 0, j)),
        ],
        out_specs=pl.BlockSpec((None, 8, tn), lambda l, j: (l, 0, j)),
        out_shape=jax.ShapeDtypeStruct((DEPTH, 8, 6 * D), F32),
        compiler_params=_cp("arbitrary", "arbitrary"),
        name="adaln_mod",
    )(cond8, w_mod, b_mod.reshape(DEPTH, 1, 6 * D))
    return out.reshape(DEPTH, 8, 6, D)


def _even_kernel(x_ref, mod_ref, n1g_ref, win_ref, wgu_ref, bgu_ref, glag_ref, ws_ref, bs_ref,
                 wout_ref, s0_ref, xo_ref, st_ref, proj, la, o_f, o_b, st_scr, *, T):
    n_chunks = T // GLA_CHUNK
    shift, scale, gate = mod_ref[0:1, :], mod_ref[1:2, :], mod_ref[2:3, :]
    RB = 128
    PROJ_RB = 256

    def proj_body(r, carry):
        r0 = pl.multiple_of(r * PROJ_RB, PROJ_RB)
        h = _rms(x_ref[pl.ds(r0, PROJ_RB), :], n1g_ref[...]) * (1.0 + scale) + shift
        p = _dot(h, win_ref[...])
        proj[pl.ds(r0, PROJ_RB), :] = p
        z = _dot(p[:, C_A:C_A + 128], wgu_ref[...]) + bgu_ref[...]
        la[pl.ds(r0, PROJ_RB), :] = _log_sigmoid(z) * (1.0 / GLA_TAU)
        return carry

    lax.fori_loop(0, T // PROJ_RB, proj_body, 0)

    st_scr[0] = s0_ref[0].T
    st_scr[1] = s0_ref[1].T

    ci = lax.broadcasted_iota(jnp.int32, (GLA_CHUNK, GLA_CHUNK), 0)
    cj = lax.broadcasted_iota(jnp.int32, (GLA_CHUNK, GLA_CHUNK), 1)
    tri = (jnp.where(ci >= cj, 1.0, 0.0).astype(BF16), jnp.where(ci <= cj, 1.0, 0.0).astype(BF16))
    ai = lax.broadcasted_iota(jnp.int32, (GLA_HEADS * GLA_CHUNK, GLA_CHUNK), 0) % GLA_CHUNK
    aj = lax.broadcasted_iota(jnp.int32, (GLA_HEADS * GLA_CHUNK, GLA_CHUNK), 1)
    amask = (ai >= aj, ai <= aj)
    lane_head = lax.broadcasted_iota(jnp.int32, (1, QK_W), 1) // GLA_DK
    hmask = [jnp.where(lane_head == h, 1.0, 0.0) for h in range(GLA_HEADS)]

    def chunk_body(i, carry):
        for d in range(2):
            c = i if d == 0 else n_chunks - 1 - i
            r0 = pl.multiple_of(c * GLA_CHUNK, GLA_CHUNK)
            q = proj[pl.ds(r0, GLA_CHUNK), C_Q:C_Q + QK_W] * (GLA_DK ** -0.5)
            k = proj[pl.ds(r0, GLA_CHUNK), C_K:C_K + QK_W]
            v = proj[pl.ds(r0, GLA_CHUNK), C_V:C_V + V_W]
            lac = la[pl.ds(r0, GLA_CHUNK), d * QK_W:(d + 1) * QK_W]
            hi = lac.astype(BF16)
            lo = (lac - hi.astype(F32)).astype(BF16)
            b = (jnp.dot(tri[d], hi, preferred_element_type=F32)
                 + jnp.dot(tri[d], lo, preferred_element_type=F32))
            bend = b[GLA_CHUNK - 1:GLA_CHUNK, :] if d == 0 else b[0:1, :]
            qe = q * jnp.exp(b)
            ke = k * jnp.exp(-b)
            kd = k * jnp.exp(bend - b)
            st = st_scr[d]
            qstack = jnp.concatenate([qe * hmask[h] for h in range(GLA_HEADS)], axis=0).astype(BF16)
            att = jnp.where(amask[d], _dot_nt(qstack, ke), 0.0)
            inter = _dot_nt(qstack, st)
            outs = []
            for h in range(GLA_HEADS):
                rows = slice(h * GLA_CHUNK, (h + 1) * GLA_CHUNK)
                outs.append(_dot(att[rows], v[:, h * GLA_DV:(h + 1) * GLA_DV]) + inter[rows])
            o = jnp.concatenate(outs, axis=1)
            if d == 0:
                o_f[pl.ds(r0, GLA_CHUNK), :] = o
            else:
                o_b[pl.ds(r0, GLA_CHUNK), :] = o
            vstack = jnp.concatenate([v[:, h * GLA_DV:(h + 1) * GLA_DV] for h in range(GLA_HEADS)], axis=0)
            kstack = jnp.concatenate([kd * hmask[h] for h in range(GLA_HEADS)], axis=0)
            st_scr[d] = st * jnp.exp(bend) + _dot_tn(vstack, kstack)
        return carry

    lax.fori_loop(0, n_chunks, chunk_body, 0, unroll=2)
    st_ref[0] = st_scr[0].T
    st_ref[1] = st_scr[1].T

    def out_body(r, carry):
        r0 = pl.multiple_of(r * RB, RB)
        osum = o_f[pl.ds(r0, RB), :] + o_b[pl.ds(r0, RB), :]
        g = proj[pl.ds(r0, RB), C_G:C_G + V_W]
        u = proj[pl.ds(r0, RB), C_U:C_U + GMLP_W]
        vg = _gelu(proj[pl.ds(r0, RB), C_VG:C_VG + GMLP_W])
        parts = []
        for h in range(GLA_HEADS):
            oh = osum[:, h * GLA_DV:(h + 1) * GLA_DV]
            parts.append(_rms(oh, glag_ref[...]) * _silu(g[:, h * GLA_DV:(h + 1) * GLA_DV]))
        for gi in range(GMLP_GROUPS):
            vc = vg[:, gi * GMLP_DIM:(gi + 1) * GMLP_DIM]
            vc = vc - jnp.mean(vc, axis=-1, keepdims=True)
            vn = vc * lax.rsqrt(jnp.mean(vc * vc, axis=-1, keepdims=True) + EPS)
            sg = _dot(ws_ref[gi], vn) + bs_ref[:, gi:gi + 1]
            parts.append(_gelu(u[:, gi * GMLP_DIM:(gi + 1) * GMLP_DIM]) * sg)
        mix = jnp.concatenate(parts, axis=1)
        y = _dot(mix, wout_ref[...])
        xo_ref[pl.ds(r0, RB), :] = x_ref[pl.ds(r0, RB), :] + gate * y
        return carry

    lax.fori_loop(0, T // RB, out_body, 0)


def _even_mixer(x_all, mod_l, n1g, win, wgu, bgu, glag, ws, bs, wout, s0, *, latent, x_first=None):
    if latent:
        T, nseq, blk0 = DEC_SEQ, DEC_BATCH, N_CTX // DEC_SEQ
        cond = lambda i: 1 + i
        s0_spec = pl.BlockSpec((None, 2, QK_W, GLA_DV), lambda i: (i, 0, 0, 0))
    else:
        T, nseq, blk0 = SEQ, BATCH, 0
        cond = lambda i: 0
        s0_spec = pl.BlockSpec((None, 2, QK_W, GLA_DV), lambda i: (0, 0, 0, 0))
    const2 = lambda i: (0, 0)
    body = functools.partial(_even_kernel, T=T)
    x_spec = pl.BlockSpec((T, D), lambda i: (blk0 + i, 0))
    if x_first is None:
        lead_specs, lead_args, aliases = [x_spec], (x_all,), {0: 0}
    elif x_all is None:
        lead_specs, lead_args, aliases = [pl.BlockSpec((T, D), lambda i: (i, 0))], (x_first,), {}
    else:
        lead_specs = [pl.BlockSpec(memory_space=pl.ANY), pl.BlockSpec((T, D), lambda i: (i, 0))]
        lead_args, aliases = (x_all, x_first), {0: 0}
        body = lambda dst_ref, *refs: _even_kernel(*refs, T=T)
    x_new, states = pl.pallas_call(
        body,
        grid=(nseq,),
        in_specs=lead_specs + [
            pl.BlockSpec((None, 6, D), lambda i: (cond(i), 0, 0)),
            _resident((1, D), const2),
            _resident((D, EVEN_PACK), const2),
            _resident((128, 2 * QK_W), const2),
            _resident((1, 2 * QK_W), const2),
            _resident((1, GLA_DV), const2),
            _resident((GMLP_GROUPS, GMLP_CHUNK, GMLP_CHUNK), lambda i: (0, 0, 0)),
            _resident((GMLP_CHUNK, GMLP_GROUPS), const2),
            _resident((D, D), const2),
            s0_spec,
        ],
        out_specs=[
            x_spec,
            pl.BlockSpec((None, 2, QK_W, GLA_DV), lambda i: (i, 0, 0, 0)),
        ],
        out_shape=[
            jax.ShapeDtypeStruct((N_TOK, D), F32),
            jax.ShapeDtypeStruct((nseq, 2, QK_W, GLA_DV), F32),
        ],
        scratch_shapes=[
            pltpu.VMEM((T, EVEN_PACK), F32),
            pltpu.VMEM((T, 2 * QK_W), F32),
            pltpu.VMEM((T, V_W), F32),
            pltpu.VMEM((T, V_W), F32),
            pltpu.VMEM((2, GLA_DV, QK_W), F32),
        ],
        input_output_aliases=aliases,
        compiler_params=_cp("arbitrary"),
        name="even_mixer_latent" if latent else "even_mixer_context",
    )(*lead_args, mod_l, n1g, win, wgu, bgu, glag, ws, bs, wout, s0)
    return x_new, states


QKV_TB = 512


def _qkv_kernel(x_ref, mod_ref, n1g_ref, win_ref, gq_ref, gk_ref, cos_ref, sin_ref, q_ref, k_ref, v_ref):
    shift, scale = mod_ref[0:1, :], mod_ref[1:2, :]
    h = _rms(x_ref[...], n1g_ref[...]) * (1.0 + scale) + shift
    p = _dot(h, win_ref[...])
    cos, sin = cos_ref[...], sin_ref[...]
    even_lane = lax.broadcasted_iota(jnp.int32, (1, HD), 1) % 2 == 0

    def rope(xn):
        swapped = jnp.where(even_lane, pltpu.roll(xn, HD - 1, axis=1), pltpu.roll(xn, 1, axis=1))
        return xn * cos + swapped * sin

    def emit(rotate):
        for hh in range(ATT_HEADS):
            qn = _rms(p[:, hh * HD:(hh + 1) * HD], gq_ref[...])
            q_ref[:, hh * HD:(hh + 1) * HD] = (rotate(qn) * (HD ** -0.5)).astype(BF16)
        for hh in range(ATT_KV):
            k_ref[:, hh * HD:(hh + 1) * HD] = rotate(_rms(p[:, Q_W + hh * HD:Q_W + (hh + 1) * HD], gk_ref[...]))

    is_latent = pl.program_id(0) >= N_CTX // QKV_TB

    @pl.when(is_latent)
    def _():
        emit(rope)

    @pl.when(jnp.logical_not(is_latent))
    def _():
        emit(lambda xn: xn)

    v_ref[...] = p[:, Q_W + KV_W:]


def _qkv(x_all, mod_l, n1g, win, gq, gk, cos_tab, sin_tab):
    nb_ctx = N_CTX // QKV_TB
    per_seq = DEC_SEQ // QKV_TB
    cond = lambda i: jnp.where(i < nb_ctx, 0, 1 + (i - nb_ctx) // per_seq)
    tab = lambda i: jnp.where(i < nb_ctx, 0, 1 + (i - nb_ctx) % per_seq)
    const2 = lambda i: (0, 0)
    return pl.pallas_call(
        _qkv_kernel,
        grid=(N_TOK // QKV_TB,),
        in_specs=[
            pl.BlockSpec((QKV_TB, D), lambda i: (i, 0)),
            pl.BlockSpec((None, 6, D), lambda i: (cond(i), 0, 0)),
            _resident((1, D), const2),
            _resident((D, Q_W + 2 * KV_W), const2),
            _resident((1, HD), const2),
            _resident((1, HD), const2),
            pl.BlockSpec((None, QKV_TB, HD), lambda i: (tab(i), 0, 0)),
            pl.BlockSpec((None, QKV_TB, HD), lambda i: (tab(i), 0, 0)),
        ],
        out_specs=[
            pl.BlockSpec((QKV_TB, Q_W), lambda i: (i, 0)),
            pl.BlockSpec((QKV_TB, KV_W), lambda i: (i, 0)),
            pl.BlockSpec((QKV_TB, KV_W), lambda i: (i, 0)),
        ],
        out_shape=[
            jax.ShapeDtypeStruct((N_TOK, Q_W), BF16),
            jax.ShapeDtypeStruct((N_TOK, KV_W), F32),
            jax.ShapeDtypeStruct((N_TOK, KV_W), F32),
        ],
        compiler_params=_cp("arbitrary"),
        name="odd_qkv",
    )(x_all, mod_l, n1g, win, gq, gk, cos_tab, sin_tab)


ATT_TQ = 256


def _attn_kernel(*refs, n_kv):
    q_ref = refs[0]
    kv_refs = refs[1:1 + 2 * n_kv]
    x_ref, mod_ref, wout_ref, xo_ref, att_scr = refs[1 + 2 * n_kv:]
    gate = mod_ref[2:3, :]
    for kh in range(ATT_KV):
        ks = [kv_refs[2 * s][:, kh * HD:(kh + 1) * HD].astype(BF16) for s in range(n_kv)]
        vs = [jnp.concatenate([kv_refs[2 * s + 1][:, kh * HD:(kh + 1) * HD].astype(BF16),
                               jnp.ones((kv_refs[2 * s + 1].shape[0], HD), BF16)], axis=1) for s in range(n_kv)]
        for g in range(ATT_G):
            hh = kh * ATT_G + g
            qh = q_ref[:, hh * HD:(hh + 1) * HD]
            ss = [_dot_nt(qh, kk) for kk in ks]
            m = ss[0].max(axis=-1, keepdims=True)
            for s in ss[1:]:
                m = jnp.maximum(m, s.max(axis=-1, keepdims=True))
            o = _dot(jnp.exp(ss[0] - m), vs[0])
            for s, vv in zip(ss[1:], vs[1:]):
                o = o + _dot(jnp.exp(s - m), vv)
            att_scr[:, hh * HD:(hh + 1) * HD] = o[:, :HD] / o[:, HD:HD + 1]
    y = _dot(att_scr[...], wout_ref[...])
    xo_ref[...] = x_ref[...] + gate * y


def _attention(x_all, mod_l, q, k, v, wout, cache_k=None, cache_v=None, layer_i=0):
    latent = cache_k is not None
    const2 = lambda *a: (0, 0)
    if latent:
        nq = DEC_SEQ // ATT_TQ
        row_blk = lambda b, j: (N_CTX // ATT_TQ + b * nq + j, 0)
        grid = (DEC_BATCH, nq)
        kv_specs = [
            pl.BlockSpec((None, None, SEQ, KV_W), lambda b, j: (b, layer_i, 0, 0)),
            pl.BlockSpec((None, None, SEQ, KV_W), lambda b, j: (b, layer_i, 0, 0)),
            pl.BlockSpec((DEC_SEQ, KV_W), lambda b, j: (N_CTX // DEC_SEQ + b, 0)),
            pl.BlockSpec((DEC_SEQ, KV_W), lambda b, j: (N_CTX // DEC_SEQ + b, 0)),
        ]
        kv_args = (cache_k, cache_v, k, v)
        mod_spec = pl.BlockSpec((None, 6, D), lambda b, j: (1 + b, 0, 0))
        sem = ("arbitrary", "arbitrary")
        n_kv = 2
    else:
        row_blk = lambda i: (i, 0)
        grid = (BATCH,)
        kv_specs = [pl.BlockSpec((SEQ, KV_W), row_blk), pl.BlockSpec((SEQ, KV_W), row_blk)]
        kv_args = (k, v)
        mod_spec = pl.BlockSpec((None, 6, D), lambda i: (0, 0, 0))
        sem = ("arbitrary",)
        n_kv = 1
    n_in = 1 + len(kv_args)
    return pl.pallas_call(
        functools.partial(_attn_kernel, n_kv=n_kv),
        grid=grid,
        in_specs=[pl.BlockSpec((ATT_TQ, Q_W), row_blk)] + kv_specs + [
            pl.BlockSpec((ATT_TQ, D), row_blk),
            mod_spec,
            _resident((D, D), const2),
        ],
        out_specs=pl.BlockSpec((ATT_TQ, D), row_blk),
        out_shape=jax.ShapeDtypeStruct((N_TOK, D), F32),
        scratch_shapes=[pltpu.VMEM((ATT_TQ, Q_W), F32)],
        input_output_aliases={n_in: 0},
        compiler_params=_cp(*sem),
        name="attention_latent" if latent else "attention_context",
    )(q, *kv_args, x_all, mod_l, wout)


ROUTE_TB = 512
HALF_TOK = N_TOK // 2
M_E1, M_E2, M_G1, M_G2, M_R1, M_R2 = 0, 1, 2, 3, 4, 5


def _router_kernel(x_ref, mod_ref, n2g_ref, whi_ref, wlo_ref, br_ref, h_ref, metat_ref, cnt_ref, run):
    @pl.when(pl.program_id(0) % (HALF_TOK // ROUTE_TB) == 0)
    def _():
        run[...] = jnp.zeros_like(run)

    shift, scale = mod_ref[3:4, :], mod_ref[4:5, :]
    h = _rms(x_ref[...], n2g_ref[...]) * (1.0 + scale) + shift
    _rows_to_tiles(h_ref, h)
    h_hi = h.astype(BF16)
    h_lo = (h - h_hi.astype(F32)).astype(BF16)
    dot = functools.partial(jnp.dot, preferred_element_type=F32)
    logits = dot(h_hi, whi_ref[...]) + dot(h_lo, whi_ref[...]) + dot(h_hi, wlo_ref[...]) + br_ref[...]
    lane = lax.broadcasted_iota(jnp.int32, logits.shape, 1).astype(F32)
    big = 1e4

    def first_argmax(vals):
        m = vals.max(axis=-1, keepdims=True)
        return m, jnp.where(vals == m, lane, big).min(axis=-1, keepdims=True)

    gl = jnp.where((lane >= N_EXP) & (lane < N_EXP + MOE_GROUPS), logits, NEG)
    gmax, glane = first_argmax(gl)
    g_p = 1.0 / jnp.exp(gl - gmax).sum(axis=-1, keepdims=True)
    lo = (glane - N_EXP) * MOE_PER_GROUP
    el = jnp.where((lane >= lo) & (lane < lo + MOE_PER_GROUP), logits, NEG)
    m1, i1 = first_argmax(el)
    m2, i2 = first_argmax(jnp.where(lane == i1, NEG, el))
    t = jnp.exp(m2 - m1)
    w1 = 1.0 / (1.0 + t)
    sel1, sel2 = lane == i1, lane == i2
    onehot = jnp.where(sel1 | sel2, 1.0, 0.0)
    ri = lax.broadcasted_iota(jnp.int32, (ROUTE_TB, ROUTE_TB), 0)
    rj = lax.broadcasted_iota(jnp.int32, (ROUTE_TB, ROUTE_TB), 1)
    before = _dot(jnp.where(ri > rj, 1.0, 0.0), onehot) + run[...]
    r1 = jnp.where(sel1, before, 0.0).sum(axis=-1, keepdims=True)
    r2 = jnp.where(sel2, before, 0.0).sum(axis=-1, keepdims=True)
    run[...] += onehot.sum(axis=0, keepdims=True)
    cnt_ref[...] = run[...]
    meta = jnp.zeros_like(logits)
    for j, val in enumerate([i1, i2, w1 * g_p, (t * w1) * g_p, r1, r2]):
        meta = jnp.where(lane == j, val, meta)
    metat_ref[...] = meta.T[0:8, :]


def _router(x_all, mod_l, n2g, wr, br):
    w_hi = wr.astype(BF16)
    wr_lo = (wr - w_hi.astype(F32)).astype(BF16)
    nb_ctx = N_CTX // ROUTE_TB
    per_seq = DEC_SEQ // ROUTE_TB
    cond = lambda i: jnp.where(i < nb_ctx, 0, 1 + (i - nb_ctx) // per_seq)
    const2 = lambda i: (0, 0)
    return pl.pallas_call(
        _router_kernel,
        grid=(N_TOK // ROUTE_TB,),
        in_specs=[
            pl.BlockSpec((ROUTE_TB, D), lambda i: (i, 0)),
            pl.BlockSpec((None, 6, D), lambda i: (cond(i), 0, 0)),
            _resident((1, D), const2),
            _resident((D, 128), const2),
            _resident((D, 128), const2),
            _resident((1, 128), const2),
        ],
        out_specs=[
            pl.BlockSpec((ROUTE_TB * 8, 128), lambda i: (i, 0)),
            pl.BlockSpec((8, ROUTE_TB), lambda i: (0, i)),
            pl.BlockSpec((None, 1, 128), lambda i: (i // (HALF_TOK // ROUTE_TB), 0, 0)),
        ],
        out_shape=[
            jax.ShapeDtypeStruct((N_TOK * 8, 128), F32),
            jax.ShapeDtypeStruct((8, N_TOK), F32),
            jax.ShapeDtypeStruct((2, 1, 128), F32),
        ],
        scratch_shapes=[pltpu.VMEM((1, 128), F32)],
        compiler_params=_cp("arbitrary"),
        name="moe_router",
    )(x_all, mod_l, n2g, w_hi, wr_lo, br)


EXP_TM = 128
N_ASSIGN = 2 * N_TOK
N_GROUPS = 2 * N_EXP
MAX_TILES = N_ASSIGN // EXP_TM + N_GROUPS
N_SORTED = MAX_TILES * EXP_TM
ORDER_BLK = 2048
CODE_PLANE = 2 * HALF_TOK
CODE_MASK = 8 * CODE_PLANE - 1
DUMMY8 = HALF_TOK * 8


def _order_kernel(pos1_ref, pos2_ref, pad_lo_ref, pad_hi_ref, src_ref):
    i = pl.program_id(0)
    local = (i % (HALF_TOK // ORDER_BLK)) * ORDER_BLK

    def body(t, carry):
        src_ref[pos1_ref[t]] = (local + t) * 8
        src_ref[pos2_ref[t]] = (local + t + CODE_PLANE) * 8
        return carry

    lax.fori_loop(0, ORDER_BLK, body, 0, unroll=16)

    @pl.when(i == 0)
    def _():
        def group(g, carry):
            def pad(p, c):
                src_ref[p] = DUMMY8
                return c
            return lax.fori_loop(pad_lo_ref[g], pad_hi_ref[g], pad, carry)

        lax.fori_loop(0, N_GROUPS, group, 0)


def _order(pos, pad_lo, pad_hi):
    return pl.pallas_call(
        _order_kernel,
        grid=(N_TOK // ORDER_BLK,),
        in_specs=[
            pl.BlockSpec((ORDER_BLK,), lambda i: (i,), memory_space=pltpu.SMEM),
            pl.BlockSpec((ORDER_BLK,), lambda i: (N_TOK // ORDER_BLK + i,), memory_space=pltpu.SMEM),
            pl.BlockSpec(memory_space=pltpu.SMEM),
            pl.BlockSpec(memory_space=pltpu.SMEM),
        ],
        out_specs=pl.BlockSpec(memory_space=pltpu.SMEM),
        out_shape=jax.ShapeDtypeStruct((N_SORTED,), jnp.int32),
        compiler_params=_cp("arbitrary"),
        name="moe_order",
    )(pos, pos, pad_lo, pad_hi)


GATE_BLK = CODE_PLANE + HALF_TOK
ACC_TOK = HALF_TOK + 64
GATHER_GROUP, ACC_GROUP = 16, 8


RES_TB = 256


def _experts_kernel(tile0_ref, ntile_ref, count_ref, src_ref, gs_ref, h_hbm, x_hbm, mod_ref, wg_ref, wu_ref, wd_ref,
                    *rest, final):
    if final:
        fg_ref, *dst_hbm = rest[:3]
        rest = rest[3:]
    else:
        dst_hbm, rest = rest[:1], rest[1:]
    h_res, acc, xbuf, ybuf, wgb, wub, wdb, xin, xout, sem, in_sem, out_sem = rest
    group = pl.program_id(0)
    expert = group % N_EXP
    half = group // N_EXP
    rows0 = pl.multiple_of(half * (HALF_TOK * 8), 8)

    @pl.when(expert == 0)
    def _():
        cp = pltpu.make_async_copy(h_hbm.at[pl.ds(rows0, HALF_TOK * 8), :], h_res.at[pl.ds(0, HALF_TOK * 8), :], sem)
        cp.start()
        h_res[pl.ds(DUMMY8, 8), :] = jnp.zeros((8, 128), F32)
        xbuf[...] = jnp.zeros_like(xbuf)

        def zero(i, carry):
            acc[pl.ds(pl.multiple_of(i * 512, 512), 512), :] = jnp.zeros((512, 128), F32)
            return carry

        lax.fori_loop(0, ACC_TOK * 8 // 512, zero, 0)
        cp.wait()

    n_tiles = ntile_ref[group]

    @pl.when(n_tiles > 0)
    def _():
        wgb[...] = wg_ref[...].astype(BF16)
        wub[...] = wu_ref[...].astype(BF16)
        wdb[...] = wd_ref[...].astype(BF16)

    row0 = tile0_ref[group] * EXP_TM
    row_end = row0 + count_ref[group]

    def process(base, rows):
        live = (jnp.clip(row_end - base, 0, rows) + GATHER_GROUP - 1) // GATHER_GROUP

        def gather(g, c):
            for i in range(GATHER_GROUP):
                r = g * GATHER_GROUP + i
                xbuf[pl.ds(pl.multiple_of(r * 8, 8), 8), :] = _tile_of(h_res, src_ref[base + r] & CODE_MASK)[...]
            return c

        lax.fori_loop(0, live, gather, 0)
        x = _tiles_to_rows(xbuf, rows).astype(BF16)
        hid = _silu(_dot(x, wgb[...])) * _dot(x, wub[...])
        _rows_to_tiles(ybuf, _dot(hid, wdb[...]))

        def accumulate(g, c):
            targets, values = [], []
            for i in range(ACC_GROUP):
                r = g * ACC_GROUP + i
                code = src_ref[base + r]
                target = _tile_of(acc, code & CODE_MASK)
                targets.append(target)
                values.append(target[...] + gs_ref[code >> 3] * ybuf[pl.ds(pl.multiple_of(r * 8, 8), 8), :])
            for target, value in zip(targets, values):
                target[...] = value
            return c

        lax.fori_loop(0, live * (GATHER_GROUP // ACC_GROUP), accumulate, 0)


    def pair_body(j, carry):
        process(row0 + j * (2 * EXP_TM), 2 * EXP_TM)
        return carry

    lax.fori_loop(0, n_tiles // 2, pair_body, 0)

    @pl.when(n_tiles % 2 == 1)
    def _():
        process(row0 + (n_tiles - 1) * EXP_TM, EXP_TM)

    def rows_of(first, blk):
        return pl.ds(pl.multiple_of(first + blk * RES_TB, RES_TB), RES_TB)

    def load_x(blk, slot):
        return pltpu.make_async_copy(x_hbm.at[rows_of(half * HALF_TOK, blk), :], xin.at[slot], in_sem.at[slot])

    def residual(blk, slot):
        cond = jnp.where(half == 0, 0, 1 + blk // (DEC_SEQ // RES_TB))
        gate = mod_ref[cond, 5:6, :]
        y = _tiles_to_rows(acc.at[pl.ds(pl.multiple_of(blk * (RES_TB * 8), RES_TB * 8), RES_TB * 8), :], RES_TB)
        x_new = xin[slot] + gate * y
        xout[slot] = _rms(x_new, fg_ref[...]) if final else x_new

    def epilogue(dst, first_row):
        def store_x(blk, slot):
            return pltpu.make_async_copy(xout.at[slot], dst.at[rows_of(first_row, blk), :], out_sem.at[slot])

        n_pairs = HALF_TOK // RES_TB // 2
        load_x(0, 0).start()

        def pair(p, carry):
            for slot in range(2):
                blk = 2 * p + slot
                if slot == 0:
                    load_x(blk + 1, 1).start()
                else:
                    @pl.when(p + 1 < n_pairs)
                    def _():
                        load_x(blk + 1, 0).start()
                load_x(blk, slot).wait()

                @pl.when(p > 0)
                def _():
                    store_x(blk - 2, slot).wait()

                residual(blk, slot)
                store_x(blk, slot).start()
            return carry

        lax.fori_loop(0, n_pairs, pair, 0)
        store_x(2 * n_pairs - 2, 0).wait()
        store_x(2 * n_pairs - 1, 1).wait()

    if final:
        for which in range(2):
            @pl.when((expert == N_EXP - 1) & (half == which))
            def _():
                epilogue(dst_hbm[which], 0)
    else:
        @pl.when(expert == N_EXP - 1)
        def _():
            epilogue(dst_hbm[0], half * HALF_TOK)


def _experts(tile0, n_tiles, counts, src, gs, h, x_all, mod_l, wg, wu, wd, layer, final_g=None):
    final = final_g is not None
    wmap = lambda g, t0, nt, cnt, src: (layer, g % N_EXP, 0, 0)
    any_spec = pl.BlockSpec(memory_space=pl.ANY)
    extra_specs = [pl.BlockSpec((1, D), lambda g, t0, nt, cnt, src: (0, 0))] if final else []
    extra_args = (final_g,) if final else ()
    return pl.pallas_call(
        functools.partial(_experts_kernel, final=final),
        grid_spec=pltpu.PrefetchScalarGridSpec(
            num_scalar_prefetch=4,
            grid=(N_GROUPS,),
            in_specs=[
                pl.BlockSpec((GATE_BLK,), lambda g, t0, nt, cnt, src: (g // N_EXP,), memory_space=pltpu.SMEM),
                pl.BlockSpec(memory_space=pl.ANY),
                pl.BlockSpec(memory_space=pl.ANY),
                pl.BlockSpec((8, 6, D), lambda g, t0, nt, cnt, src: (0, 0, 0)),
                pl.BlockSpec((None, None, D, D_EXP), wmap),
                pl.BlockSpec((None, None, D, D_EXP), wmap),
                pl.BlockSpec((None, None, D_EXP, D), wmap),
            ] + extra_specs,
            out_specs=[any_spec, any_spec] if final else any_spec,
            scratch_shapes=[
                pltpu.VMEM((ACC_TOK * 8, 128), F32),
                pltpu.VMEM((ACC_TOK * 8, 128), F32),
                pltpu.VMEM((2 * EXP_TM * 8, 128), F32),
                pltpu.VMEM((2 * EXP_TM * 8, 128), F32),
                pltpu.VMEM((D, D_EXP), BF16),
                pltpu.VMEM((D, D_EXP), BF16),
                pltpu.VMEM((D_EXP, D), BF16),
                pltpu.VMEM((2, RES_TB, D), F32),
                pltpu.VMEM((2, RES_TB, D), F32),
                pltpu.SemaphoreType.DMA,
                pltpu.SemaphoreType.DMA((2,)),
                pltpu.SemaphoreType.DMA((2,)),
            ],
        ),
        out_shape=([jax.ShapeDtypeStruct((HALF_TOK, D), F32)] * 2 if final
                   else jax.ShapeDtypeStruct((N_TOK, D), F32)),
        input_output_aliases={} if final else {6: 0},
        compiler_params=_cp("arbitrary"),
        name="moe_experts_final" if final else "moe_experts",
    )(tile0, n_tiles, counts, src, gs, h, x_all, mod_l, wg, wu, wd, *extra_args)


def _moe(x_all, mod_l, n2g, wr, br, wg, wu, wd, layer, final_g=None):
    h, metat, cnt = _router(x_all, mod_l, n2g, wr, br)
    counts = cnt[:, 0, :N_EXP].astype(jnp.int32).reshape(N_GROUPS)
    padded = (counts + EXP_TM - 1) // EXP_TM * EXP_TM
    ends = jnp.cumsum(padded)
    offs = ends - padded
    rec = metat.astype(jnp.int32)
    half = (jnp.arange(N_TOK, dtype=jnp.int32) // HALF_TOK)[None, :]
    group = rec[M_E1:M_E2 + 1] + N_EXP * half
    is_group = group[None] == jnp.arange(N_GROUPS, dtype=jnp.int32)[:, None, None]
    pos = jnp.sum(jnp.where(is_group, offs[:, None, None], 0), axis=0) + rec[M_R1:M_R2 + 1]
    live_end = offs + (counts + GATHER_GROUP - 1) // GATHER_GROUP * GATHER_GROUP
    src = _order(pos.reshape(N_ASSIGN), offs + counts, live_end)
    g12 = metat[M_G1:M_G2 + 1].reshape(2, 2, HALF_TOK)
    gates = jnp.concatenate([g12[0], jnp.zeros((2, CODE_PLANE - HALF_TOK), F32), g12[1]], axis=1)
    return _experts(offs // EXP_TM, padded // EXP_TM, counts, src, gates.reshape(2 * GATE_BLK), h, x_all, mod_l,
                    wg, wu, wd, layer, final_g)


def _rope_tables():
    pos = jnp.arange(DEC_SEQ)
    row = (pos // GRID_W).astype(F32)
    col = (pos % GRID_W).astype(F32)
    n_freq = HD // 4
    inv = ROPE_THETA ** (-jnp.arange(n_freq, dtype=F32) / n_freq)
    ang = jnp.concatenate([row[:, None] * inv, col[:, None] * inv], axis=-1)
    cos = jnp.repeat(jnp.cos(ang), 2, axis=-1)
    sin = jnp.repeat(jnp.sin(ang), 2, axis=-1) * jnp.tile(jnp.array([-1.0, 1.0], F32), HD // 2)
    nblk = DEC_SEQ // QKV_TB
    cos_tab = jnp.concatenate([jnp.ones((1, QKV_TB, HD), F32), cos.reshape(nblk, QKV_TB, HD)], axis=0)
    sin_tab = jnp.concatenate([jnp.zeros((1, QKV_TB, HD), F32), sin.reshape(nblk, QKV_TB, HD)], axis=0)
    return cos_tab, sin_tab


def kernel(x_prompt, x_sample, state_gla, cache_k, cache_v, c, c_ctx, w_mod, b_mod, norm1_g, norm2_g,
           w_in_even, w_gate_up, b_gate_up, gla_norm_g, w_spatial, b_spatial, w_out_even,
           w_in_odd, q_norm_g, k_norm_g, w_out_odd, w_router_group, b_router_group,
           w_router_expert, b_router_expert, w_exp_gate, w_exp_up, w_exp_down, final_norm_g):
    x_all = None
    cond8 = jnp.concatenate([c_ctx[None], c, jnp.zeros((3, D), F32)], axis=0)
    mod = _modulation(cond8, w_mod, b_mod)
    cos_tab, sin_tab = _rope_tables()
    zero_state = jnp.zeros((1, 2, QK_W, GLA_DV), F32)
    state_in = state_gla.reshape(DEC_BATCH, -1, 2, QK_W, GLA_DV)
    cache_k2 = cache_k.reshape(DEC_BATCH, -1, SEQ, KV_W)
    cache_v2 = cache_v.reshape(DEC_BATCH, -1, SEQ, KV_W)

    gla_states, ctx_k, ctx_v = [], [], []
    for l in range(DEPTH):
        i = l // 2
        n1g = norm1_g[l][None]
        if l % 2 == 0:
            w = w_in_even[i]
            win = jnp.concatenate([w[:, :1536], w[:, 1568:], w[:, 1536:1568], jnp.zeros((D, 96), F32)],
                                  axis=1).astype(BF16)
            wgu = jnp.zeros((128, 2 * QK_W), F32)
            wgu = wgu.at[0:GLA_RANK, 0:QK_W].set(w_gate_up[i, 0])
            wgu = wgu.at[GLA_RANK:2 * GLA_RANK, QK_W:].set(w_gate_up[i, 1]).astype(BF16)
            bgu = b_gate_up[i].reshape(1, 2 * QK_W)
            args = (mod[l], n1g, win, wgu, bgu, gla_norm_g[i][None], w_spatial[i].astype(BF16),
                    b_spatial[i].T, w_out_even[i].astype(BF16))
            first = l == 0
            x_all, st = _even_mixer(x_all, *args, zero_state, latent=False,
                                    x_first=x_prompt.reshape(N_CTX, D) if first else None)
            gla_states.append(st)
            x_all, _ = _even_mixer(x_all, *args, state_in[:, i], latent=True,
                                   x_first=x_sample.reshape(N_LAT, D) if first else None)
        else:
            q, k, v = _qkv(x_all, mod[l], n1g, w_in_odd[i].astype(BF16), q_norm_g[i][None],
                           k_norm_g[i][None], cos_tab, sin_tab)
            ctx_k.append(k[:N_CTX].reshape(BATCH, SEQ, ATT_KV, HD))
            ctx_v.append(v[:N_CTX].reshape(BATCH, SEQ, ATT_KV, HD))
            wout = w_out_odd[i].astype(BF16)
            x_all = _attention(x_all, mod[l], q, k, v, wout)
            x_all = _attention(x_all, mod[l], q, k, v, wout, cache_k2, cache_v2, layer_i=i)
        wr = jnp.concatenate([w_router_expert[l], w_router_group[l],
                              jnp.zeros((D, 128 - N_EXP - MOE_GROUPS), F32)], axis=1)
        br = jnp.concatenate([b_router_expert[l], b_router_group[l],
                              jnp.zeros((128 - N_EXP - MOE_GROUPS,), F32)])[None]
        x_all = _moe(x_all, mod[l], norm2_g[l][None], wr, br, w_exp_gate, w_exp_up, w_exp_down, l,
                     final_norm_g[None] if l == DEPTH - 1 else None)

    y_prompt = x_all[0].reshape(BATCH, SEQ, D)
    y_sample = x_all[1].reshape(DEC_BATCH, DEC_SEQ, D)
    new_state = jnp.stack(gla_states, axis=1).reshape(BATCH, -1, 2, GLA_HEADS, GLA_DK, GLA_DV)
    return (y_prompt, y_sample, new_state, jnp.stack(ctx_k, axis=1), jnp.stack(ctx_v, axis=1))
```

```python
import functools

import jax
import jax.numpy as jnp
import numpy as np
from jax import lax
from jax.experimental import pallas as pl
from jax.experimental.pallas import tpu as pltpu

F32 = jnp.float32
BF16 = jnp.bfloat16

D = 1024
BATCH, SEQ = 16, 256
DEC_BATCH, DEC_SEQ = 4, 1024
N_CTX = BATCH * SEQ
N_LAT = DEC_BATCH * DEC_SEQ
N_TOK = N_CTX + N_LAT
DEPTH = 4
EPS = 1e-6
GRID_W = 64
ROPE_THETA = 10000.0

GLA_HEADS, GLA_DK, GLA_DV, GLA_RANK, GLA_CHUNK, GLA_TAU = 4, 64, 128, 16, 128, 16.0
QK_W = GLA_HEADS * GLA_DK
V_W = GLA_HEADS * GLA_DV
GMLP_GROUPS, GMLP_DIM, GMLP_CHUNK = 4, 128, 128
GMLP_W = GMLP_GROUPS * GMLP_DIM
C_Q, C_K, C_V, C_G, C_U, C_VG, C_A = 0, 256, 512, 1024, 1536, 2048, 2560
EVEN_PACK = 2688

ATT_HEADS, ATT_KV, HD = 8, 2, 128
ATT_G = ATT_HEADS // ATT_KV
Q_W = ATT_HEADS * HD
KV_W = ATT_KV * HD

MOE_GROUPS, MOE_PER_GROUP = 4, 8
N_EXP = MOE_GROUPS * MOE_PER_GROUP
D_EXP = D // 4
NEG = -1e30

VMEM_LIMIT = 56 * 1024 * 1024


def _cp(*sem):
    return pltpu.CompilerParams(dimension_semantics=sem, vmem_limit_bytes=VMEM_LIMIT)


def _dot(a, b):
    return jnp.dot(a.astype(BF16), b.astype(BF16), preferred_element_type=F32)


def _dot_nt(a, b):
    return lax.dot_general(a.astype(BF16), b.astype(BF16), (((1,), (1,)), ((), ())),
                           preferred_element_type=F32)


def _dot_tn(a, b):
    return lax.dot_general(a.astype(BF16), b.astype(BF16), (((0,), (0,)), ((), ())),
                           preferred_element_type=F32)


def _rms(x, g):
    return x * lax.rsqrt(jnp.mean(x * x, axis=-1, keepdims=True) + EPS) * g


def _silu(x):
    return x * jax.nn.sigmoid(x)


def _gelu(x):
    return 0.5 * x * (1.0 + jnp.tanh(np.sqrt(2.0 / np.pi).astype(np.float32) * (x + 0.044715 * (x * x * x))))


def _log_sigmoid(z):
    return jnp.minimum(z, 0.0) - jnp.log(1.0 + jnp.exp(-jnp.abs(z)))


def _rows_to_tiles(ref, x):
    rows = x.shape[0]
    for j in range(D // 128):
        ref[pl.ds(j, rows, stride=8), :] = x[:, j * 128:(j + 1) * 128]


def _tiles_to_rows(ref, rows):
    return jnp.concatenate([ref[pl.ds(j, rows, stride=8), :] for j in range(D // 128)], axis=1)


def _tile_of(ref, row8):
    return ref.at[pl.ds(pl.multiple_of(row8, 8), 8), :]


def _resident(shape, index_map):
    return pl.BlockSpec(shape, index_map, pipeline_mode=pl.Buffered(1))


def _mod_kernel(cond_ref, w_ref, b_ref, o_ref):
    c = cond_ref[...]
    o_ref[...] = jnp.dot(_silu(c), w_ref[...], precision=lax.Precision.HIGHEST,
                         preferred_element_type=F32) + b_ref[...]


def _modulation(cond8, w_mod, b_mod):
    tn = 1024
    out = pl.pallas_call(
        _mod_kernel,
        grid=(DEPTH, 6 * D // tn),
        in_specs=[
            pl.BlockSpec((8, D), lambda l, j: (0, 0)),
            pl.BlockSpec((None, D, tn), lambda l, j: (l, 0, j)),
            pl.BlockSpec((None, 1, tn), lambda l, j: (l, 0, j)),
        ],
        out_specs=pl.BlockSpec((None, 8, tn), lambda l, j: (l, 0, j)),
        out_shape=jax.ShapeDtypeStruct((DEPTH, 8, 6 * D), F32),
        compiler_params=_cp("arbitrary", "arbitrary"),
        name="adaln_mod",
    )(cond8, w_mod, b_mod.reshape(DEPTH, 1, 6 * D))
    return out.reshape(DEPTH, 8, 6, D)


def _even_kernel(x_ref, mod_ref, n1g_ref, win_ref, wgu_ref, bgu_ref, glag_ref, ws_ref, bs_ref,
                 wout_ref, s0_ref, xo_ref, st_ref, proj, la, o_f, o_b, st_scr, *, T):
    n_chunks = T // GLA_CHUNK
    shift, scale, gate = mod_ref[0:1, :], mod_ref[1:2, :], mod_ref[2:3, :]
    RB = 128
    PROJ_RB = 256

    def proj_body(r, carry):
        r0 = pl.multiple_of(r * PROJ_RB, PROJ_RB)
        h = _rms(x_ref[pl.ds(r0, PROJ_RB), :], n1g_ref[...]) * (1.0 + scale) + shift
        p = _dot(h, win_ref[...])
        proj[pl.ds(r0, PROJ_RB), :] = p
        z = _dot(p[:, C_A:C_A + 128], wgu_ref[...]) + bgu_ref[...]
        la[pl.ds(r0, PROJ_RB), :] = _log_sigmoid(z) * (1.0 / GLA_TAU)
        return carry

    lax.fori_loop(0, T // PROJ_RB, proj_body, 0)

    st_scr[0] = s0_ref[0].T
    st_scr[1] = s0_ref[1].T

    ci = lax.broadcasted_iota(jnp.int32, (GLA_CHUNK, GLA_CHUNK), 0)
    cj = lax.broadcasted_iota(jnp.int32, (GLA_CHUNK, GLA_CHUNK), 1)
    tri = (jnp.where(ci >= cj, 1.0, 0.0).astype(BF16), jnp.where(ci <= cj, 1.0, 0.0).astype(BF16))
    ai = lax.broadcasted_iota(jnp.int32, (GLA_HEADS * GLA_CHUNK, GLA_CHUNK), 0) % GLA_CHUNK
    aj = lax.broadcasted_iota(jnp.int32, (GLA_HEADS * GLA_CHUNK, GLA_CHUNK), 1)
    amask = (ai >= aj, ai <= aj)
    lane_head = lax.broadcasted_iota(jnp.int32, (1, QK_W), 1) // GLA_DK
    hmask = [jnp.where(lane_head == h, 1.0, 0.0) for h in range(GLA_HEADS)]

    def chunk_body(i, carry):
        for d in range(2):
            c = i if d == 0 else n_chunks - 1 - i
            r0 = pl.multiple_of(c * GLA_CHUNK, GLA_CHUNK)
            q = proj[pl.ds(r0, GLA_CHUNK), C_Q:C_Q + QK_W] * (GLA_DK ** -0.5)
            k = proj[pl.ds(r0, GLA_CHUNK), C_K:C_K + QK_W]
            v = proj[pl.ds(r0, GLA_CHUNK), C_V:C_V + V_W]
            lac = la[pl.ds(r0, GLA_CHUNK), d * QK_W:(d + 1) * QK_W]
            hi = lac.astype(BF16)
            lo = (lac - hi.astype(F32)).astype(BF16)
            b = (jnp.dot(tri[d], hi, preferred_element_type=F32)
                 + jnp.dot(tri[d], lo, preferred_element_type=F32))
            bend = b[GLA_CHUNK - 1:GLA_CHUNK, :] if d == 0 else b[0:1, :]
            qe = q * jnp.exp(b)
            ke = k * jnp.exp(-b)
            kd = k * jnp.exp(bend - b)
            st = st_scr[d]
            qstack = jnp.concatenate([qe * hmask[h] for h in range(GLA_HEADS)], axis=0).astype(BF16)
            att = jnp.where(amask[d], _dot_nt(qstack, ke), 0.0)
            inter = _dot_nt(qstack, st)
            outs = []
            for h in range(GLA_HEADS):
                rows = slice(h * GLA_CHUNK, (h + 1) * GLA_CHUNK)
                outs.append(_dot(att[rows], v[:, h * GLA_DV:(h + 1) * GLA_DV]) + inter[rows])
            o = jnp.concatenate(outs, axis=1)
            if d == 0:
                o_f[pl.ds(r0, GLA_CHUNK), :] = o
            else:
                o_b[pl.ds(r0, GLA_CHUNK), :] = o
            vstack = jnp.concatenate([v[:, h * GLA_DV:(h + 1) * GLA_DV] for h in range(GLA_HEADS)], axis=0)
            kstack = jnp.concatenate([kd * hmask[h] for h in range(GLA_HEADS)], axis=0)
            st_scr[d] = st * jnp.exp(bend) + _dot_tn(vstack, kstack)
        return carry

    lax.fori_loop(0, n_chunks, chunk_body, 0, unroll=2)
    st_ref[0] = st_scr[0].T
    st_ref[1] = st_scr[1].T

    def out_body(r, carry):
        r0 = pl.multiple_of(r * RB, RB)
        osum = o_f[pl.ds(r0, RB), :] + o_b[pl.ds(r0, RB), :]
        g = proj[pl.ds(r0, RB), C_G:C_G + V_W]
        u = proj[pl.ds(r0, RB), C_U:C_U + GMLP_W]
        vg = _gelu(proj[pl.ds(r0, RB), C_VG:C_VG + GMLP_W])
        parts = []
        for h in range(GLA_HEADS):
            oh = osum[:, h * GLA_DV:(h + 1) * GLA_DV]
            parts.append(_rms(oh, glag_ref[...]) * _silu(g[:, h * GLA_DV:(h + 1) * GLA_DV]))
        for gi in range(GMLP_GROUPS):
            vc = vg[:, gi * GMLP_DIM:(gi + 1) * GMLP_DIM]
            vc = vc - jnp.mean(vc, axis=-1, keepdims=True)
            vn = vc * lax.rsqrt(jnp.mean(vc * vc, axis=-1, keepdims=True) + EPS)
            sg = _dot(ws_ref[gi], vn) + bs_ref[:, gi:gi + 1]
            parts.append(_gelu(u[:, gi * GMLP_DIM:(gi + 1) * GMLP_DIM]) * sg)
        mix = jnp.concatenate(parts, axis=1)
        y = _dot(mix, wout_ref[...])
        xo_ref[pl.ds(r0, RB), :] = x_ref[pl.ds(r0, RB), :] + gate * y
        return carry

    lax.fori_loop(0, T // RB, out_body, 0)


def _even_mixer(x_all, mod_l, n1g, win, wgu, bgu, glag, ws, bs, wout, s0, *, latent, x_first=None):
    if latent:
        T, nseq, blk0 = DEC_SEQ, DEC_BATCH, N_CTX // DEC_SEQ
        cond = lambda i: 1 + i
        s0_spec = pl.BlockSpec((None, 2, QK_W, GLA_DV), lambda i: (i, 0, 0, 0))
    else:
        T, nseq, blk0 = SEQ, BATCH, 0
        cond = lambda i: 0
        s0_spec = pl.BlockSpec((None, 2, QK_W, GLA_DV), lambda i: (0, 0, 0, 0))
    const2 = lambda i: (0, 0)
    body = functools.partial(_even_kernel, T=T)
    x_spec = pl.BlockSpec((T, D), lambda i: (blk0 + i, 0))
    if x_first is None:
        lead_specs, lead_args, aliases = [x_spec], (x_all,), {0: 0}
    elif x_all is None:
        lead_specs, lead_args, aliases = [pl.BlockSpec((T, D), lambda i: (i, 0))], (x_first,), {}
    else:
        lead_specs = [pl.BlockSpec(memory_space=pl.ANY), pl.BlockSpec((T, D), lambda i: (i, 0))]
        lead_args, aliases = (x_all, x_first), {0: 0}
        body = lambda dst_ref, *refs: _even_kernel(*refs, T=T)
    x_new, states = pl.pallas_call(
        body,
        grid=(nseq,),
        in_specs=lead_specs + [
            pl.BlockSpec((None, 6, D), lambda i: (cond(i), 0, 0)),
            _resident((1, D), const2),
            _resident((D, EVEN_PACK), const2),
            _resident((128, 2 * QK_W), const2),
            _resident((1, 2 * QK_W), const2),
            _resident((1, GLA_DV), const2),
            _resident((GMLP_GROUPS, GMLP_CHUNK, GMLP_CHUNK), lambda i: (0, 0, 0)),
            _resident((GMLP_CHUNK, GMLP_GROUPS), const2),
            _resident((D, D), const2),
            s0_spec,
        ],
        out_specs=[
            x_spec,
            pl.BlockSpec((None, 2, QK_W, GLA_DV), lambda i: (i, 0, 0, 0)),
        ],
        out_shape=[
            jax.ShapeDtypeStruct((N_TOK, D), F32),
            jax.ShapeDtypeStruct((nseq, 2, QK_W, GLA_DV), F32),
        ],
        scratch_shapes=[
            pltpu.VMEM((T, EVEN_PACK), F32),
            pltpu.VMEM((T, 2 * QK_W), F32),
            pltpu.VMEM((T, V_W), F32),
            pltpu.VMEM((T, V_W), F32),
            pltpu.VMEM((2, GLA_DV, QK_W), F32),
        ],
        input_output_aliases=aliases,
        compiler_params=_cp("arbitrary"),
        name="even_mixer_latent" if latent else "even_mixer_context",
    )(*lead_args, mod_l, n1g, win, wgu, bgu, glag, ws, bs, wout, s0)
    return x_new, states


QKV_TB = 512


def _qkv_kernel(x_ref, mod_ref, n1g_ref, win_ref, gq_ref, gk_ref, cos_ref, sin_ref, q_ref, k_ref, v_ref,
                ck_ref, cv_ref):
    shift, scale = mod_ref[0:1, :], mod_ref[1:2, :]
    h = _rms(x_ref[...], n1g_ref[...]) * (1.0 + scale) + shift
    p = _dot(h, win_ref[...])
    cos, sin = cos_ref[...], sin_ref[...]
    even_lane = lax.broadcasted_iota(jnp.int32, (1, HD), 1) % 2 == 0

    def rope(xn):
        swapped = jnp.where(even_lane, pltpu.roll(xn, HD - 1, axis=1), pltpu.roll(xn, 1, axis=1))
        return xn * cos + swapped * sin

    def emit(rotate, to_cache):
        for hh in range(ATT_HEADS):
            qn = _rms(p[:, hh * HD:(hh + 1) * HD], gq_ref[...])
            q_ref[:, hh * HD:(hh + 1) * HD] = (rotate(qn) * (HD ** -0.5)).astype(BF16)
        for hh in range(ATT_KV):
            kn = rotate(_rms(p[:, Q_W + hh * HD:Q_W + (hh + 1) * HD], gk_ref[...]))
            k_ref[:, hh * HD:(hh + 1) * HD] = kn
            if to_cache:
                for s in range(QKV_TB // SEQ):
                    ck_ref[s, :, hh * HD:(hh + 1) * HD] = kn[s * SEQ:(s + 1) * SEQ]
        if to_cache:
            for s in range(QKV_TB // SEQ):
                cv_ref[s] = p[s * SEQ:(s + 1) * SEQ, Q_W + KV_W:]

    is_latent = pl.program_id(0) >= N_CTX // QKV_TB

    @pl.when(is_latent)
    def _():
        emit(rope, False)

    @pl.when(jnp.logical_not(is_latent))
    def _():
        emit(lambda xn: xn, True)

    v_ref[...] = p[:, Q_W + KV_W:]


def _qkv(x_all, mod_l, n1g, win, gq, gk, cos_tab, sin_tab, layer_i, caches=None):
    nb_ctx = N_CTX // QKV_TB
    per_seq = DEC_SEQ // QKV_TB
    cond = lambda i: jnp.where(i < nb_ctx, 0, 1 + (i - nb_ctx) // per_seq)
    tab = lambda i: jnp.where(i < nb_ctx, 0, 1 + (i - nb_ctx) % per_seq)
    const2 = lambda i: (0, 0)
    cache_spec = pl.BlockSpec((QKV_TB // SEQ, None, SEQ, KV_W), lambda i: (jnp.minimum(i, nb_ctx - 1), layer_i, 0, 0))
    cache_shape = jax.ShapeDtypeStruct((BATCH, DEPTH // 2, SEQ, KV_W), F32)
    if caches is None:
        body, lead_specs, lead_args, aliases = _qkv_kernel, [], (), {}
    else:
        body = lambda ck_in, cv_in, *refs: _qkv_kernel(*refs)
        lead_specs = [pl.BlockSpec(memory_space=pl.ANY)] * 2
        lead_args, aliases = tuple(caches), {0: 3, 1: 4}
    return pl.pallas_call(
        body,
        grid=(N_TOK // QKV_TB,),
        in_specs=lead_specs + [
            pl.BlockSpec((QKV_TB, D), lambda i: (i, 0)),
            pl.BlockSpec((None, 6, D), lambda i: (cond(i), 0, 0)),
            _resident((1, D), const2),
            _resident((D, Q_W + 2 * KV_W), const2),
            _resident((1, HD), const2),
            _resident((1, HD), const2),
            pl.BlockSpec((None, QKV_TB, HD), lambda i: (tab(i), 0, 0)),
            pl.BlockSpec((None, QKV_TB, HD), lambda i: (tab(i), 0, 0)),
        ],
        out_specs=[
            pl.BlockSpec((QKV_TB, Q_W), lambda i: (i, 0)),
            pl.BlockSpec((QKV_TB, KV_W), lambda i: (i, 0)),
            pl.BlockSpec((QKV_TB, KV_W), lambda i: (i, 0)),
            cache_spec,
            cache_spec,
        ],
        out_shape=[
            jax.ShapeDtypeStruct((N_TOK, Q_W), BF16),
            jax.ShapeDtypeStruct((N_TOK, KV_W), F32),
            jax.ShapeDtypeStruct((N_TOK, KV_W), F32),
            cache_shape,
            cache_shape,
        ],
        input_output_aliases=aliases,
        compiler_params=_cp("arbitrary"),
        name="odd_qkv",
    )(*lead_args, x_all, mod_l, n1g, win, gq, gk, cos_tab, sin_tab)


ATT_TQ_LATENT = 512


def _attn_kernel(*refs, n_kv):
    q_ref = refs[0]
    kv_refs = refs[1:1 + 2 * n_kv]
    x_ref, mod_ref, wout_ref, xo_ref, att_scr = refs[1 + 2 * n_kv:]
    gate = mod_ref[2:3, :]
    for kh in range(ATT_KV):
        ks = [kv_refs[2 * s][:, kh * HD:(kh + 1) * HD].astype(BF16) for s in range(n_kv)]
        vs = [jnp.concatenate([kv_refs[2 * s + 1][:, kh * HD:(kh + 1) * HD].astype(BF16),
                               jnp.ones((kv_refs[2 * s + 1].shape[0], HD), BF16)], axis=1) for s in range(n_kv)]
        for g in range(ATT_G):
            hh = kh * ATT_G + g
            qh = q_ref[:, hh * HD:(hh + 1) * HD]
            ss = [_dot_nt(qh, kk) for kk in ks]
            m = ss[0].max(axis=-1, keepdims=True)
            for s in ss[1:]:
                m = jnp.maximum(m, s.max(axis=-1, keepdims=True))
            o = _dot(jnp.exp(ss[0] - m), vs[0])
            for s, vv in zip(ss[1:], vs[1:]):
                o = o + _dot(jnp.exp(s - m), vv)
            att_scr[:, hh * HD:(hh + 1) * HD] = o[:, :HD] / o[:, HD:HD + 1]
    y = _dot(att_scr[...], wout_ref[...])
    xo_ref[...] = x_ref[...] + gate * y


def _attention(x_all, mod_l, q, k, v, wout, cache_k=None, cache_v=None, layer_i=0):
    latent = cache_k is not None
    const2 = lambda *a: (0, 0)
    tq = ATT_TQ_LATENT if latent else SEQ
    if latent:
        nq = DEC_SEQ // tq
        row_blk = lambda b, j: (N_CTX // tq + b * nq + j, 0)
        grid = (DEC_BATCH, nq)
        kv_specs = [
            pl.BlockSpec((None, None, SEQ, KV_W), lambda b, j: (b, layer_i, 0, 0)),
            pl.BlockSpec((None, None, SEQ, KV_W), lambda b, j: (b, layer_i, 0, 0)),
            pl.BlockSpec((DEC_SEQ, KV_W), lambda b, j: (N_CTX // DEC_SEQ + b, 0)),
            pl.BlockSpec((DEC_SEQ, KV_W), lambda b, j: (N_CTX // DEC_SEQ + b, 0)),
        ]
        kv_args = (cache_k, cache_v, k, v)
        mod_spec = pl.BlockSpec((None, 6, D), lambda b, j: (1 + b, 0, 0))
        sem = ("arbitrary", "arbitrary")
        n_kv = 2
    else:
        row_blk = lambda i: (i, 0)
        grid = (BATCH,)
        kv_specs = [pl.BlockSpec((SEQ, KV_W), row_blk), pl.BlockSpec((SEQ, KV_W), row_blk)]
        kv_args = (k, v)
        mod_spec = pl.BlockSpec((None, 6, D), lambda i: (0, 0, 0))
        sem = ("arbitrary",)
        n_kv = 1
    n_in = 1 + len(kv_args)
    return pl.pallas_call(
        functools.partial(_attn_kernel, n_kv=n_kv),
        grid=grid,
        in_specs=[pl.BlockSpec((tq, Q_W), row_blk)] + kv_specs + [
            pl.BlockSpec((tq, D), row_blk),
            mod_spec,
            _resident((D, D), const2),
        ],
        out_specs=pl.BlockSpec((tq, D), row_blk),
        out_shape=jax.ShapeDtypeStruct((N_TOK, D), F32),
        scratch_shapes=[pltpu.VMEM((tq, Q_W), F32)],
        input_output_aliases={n_in: 0},
        compiler_params=_cp(*sem),
        name="attention_latent" if latent else "attention_context",
    )(q, *kv_args, x_all, mod_l, wout)


ROUTE_TB = 512
HALF_TOK = N_TOK // 2
M_E1, M_E2, M_G1, M_G2, M_R1, M_R2 = 0, 1, 2, 3, 4, 5


def _router_kernel(x_ref, mod_ref, n2g_ref, whi_ref, wlo_ref, br_ref, h_ref, metat_ref, cnt_ref, run):
    @pl.when(pl.program_id(0) % (HALF_TOK // ROUTE_TB) == 0)
    def _():
        run[...] = jnp.zeros_like(run)

    shift, scale = mod_ref[3:4, :], mod_ref[4:5, :]
    h = _rms(x_ref[...], n2g_ref[...]) * (1.0 + scale) + shift
    _rows_to_tiles(h_ref, h)
    h_hi = h.astype(BF16)
    h_lo = (h - h_hi.astype(F32)).astype(BF16)
    dot = functools.partial(jnp.dot, preferred_element_type=F32)
    logits = dot(h_hi, whi_ref[...]) + dot(h_lo, whi_ref[...]) + dot(h_hi, wlo_ref[...]) + br_ref[...]
    lane = lax.broadcasted_iota(jnp.int32, logits.shape, 1).astype(F32)
    big = 1e4

    def first_argmax(vals):
        m = vals.max(axis=-1, keepdims=True)
        return m, jnp.where(vals == m, lane, big).min(axis=-1, keepdims=True)

    gl = jnp.where((lane >= N_EXP) & (lane < N_EXP + MOE_GROUPS), logits, NEG)
    gmax, glane = first_argmax(gl)
    g_p = 1.0 / jnp.exp(gl - gmax).sum(axis=-1, keepdims=True)
    lo = (glane - N_EXP) * MOE_PER_GROUP
    el = jnp.where((lane >= lo) & (lane < lo + MOE_PER_GROUP), logits, NEG)
    m1, i1 = first_argmax(el)
    m2, i2 = first_argmax(jnp.where(lane == i1, NEG, el))
    t = jnp.exp(m2 - m1)
    w1 = 1.0 / (1.0 + t)
    sel1, sel2 = lane == i1, lane == i2
    onehot = jnp.where(sel1 | sel2, 1.0, 0.0)
    ri = lax.broadcasted_iota(jnp.int32, (ROUTE_TB, ROUTE_TB), 0)
    rj = lax.broadcasted_iota(jnp.int32, (ROUTE_TB, ROUTE_TB), 1)
    before = _dot(jnp.where(ri > rj, 1.0, 0.0), onehot) + run[...]
    r1 = jnp.where(sel1, before, 0.0).sum(axis=-1, keepdims=True)
    r2 = jnp.where(sel2, before, 0.0).sum(axis=-1, keepdims=True)
    run[...] += onehot.sum(axis=0, keepdims=True)
    cnt_ref[...] = run[...]
    meta = jnp.zeros_like(logits)
    for j, val in enumerate([i1, i2, w1 * g_p, (t * w1) * g_p, r1, r2]):
        meta = jnp.where(lane == j, val, meta)
    metat_ref[...] = meta.T[0:8, :]


def _router(x_all, mod_l, n2g, wr, br):
    w_hi = wr.astype(BF16)
    wr_lo = (wr - w_hi.astype(F32)).astype(BF16)
    nb_ctx = N_CTX // ROUTE_TB
    per_seq = DEC_SEQ // ROUTE_TB
    cond = lambda i: jnp.where(i < nb_ctx, 0, 1 + (i - nb_ctx) // per_seq)
    const2 = lambda i: (0, 0)
    return pl.pallas_call(
        _router_kernel,
        grid=(N_TOK // ROUTE_TB,),
        in_specs=[
            pl.BlockSpec((ROUTE_TB, D), lambda i: (i, 0)),
            pl.BlockSpec((None, 6, D), lambda i: (cond(i), 0, 0)),
            _resident((1, D), const2),
            _resident((D, 128), const2),
            _resident((D, 128), const2),
            _resident((1, 128), const2),
        ],
        out_specs=[
            pl.BlockSpec((ROUTE_TB * 8, 128), lambda i: (i, 0)),
            pl.BlockSpec((8, ROUTE_TB), lambda i: (0, i)),
            pl.BlockSpec((None, 1, 128), lambda i: (i // (HALF_TOK // ROUTE_TB), 0, 0)),
        ],
        out_shape=[
            jax.ShapeDtypeStruct((N_TOK * 8, 128), F32),
            jax.ShapeDtypeStruct((8, N_TOK), F32),
            jax.ShapeDtypeStruct((2, 1, 128), F32),
        ],
        scratch_shapes=[pltpu.VMEM((1, 128), F32)],
        compiler_params=_cp("arbitrary"),
        name="moe_router",
    )(x_all, mod_l, n2g, w_hi, wr_lo, br)


EXP_TM = 128
N_ASSIGN = 2 * N_TOK
N_GROUPS = 2 * N_EXP
MAX_TILES = N_ASSIGN // EXP_TM + N_GROUPS
N_SORTED = MAX_TILES * EXP_TM
ORDER_BLK = 2048
CODE_PLANE = 2 * HALF_TOK
CODE_MASK = 8 * CODE_PLANE - 1
DUMMY8 = HALF_TOK * 8


def _order_kernel(pos1_ref, pos2_ref, pad_lo_ref, pad_hi_ref, src_ref):
    i = pl.program_id(0)
    local = (i % (HALF_TOK // ORDER_BLK)) * ORDER_BLK

    def body(t, carry):
        src_ref[pos1_ref[t]] = (local + t) * 8
        src_ref[pos2_ref[t]] = (local + t + CODE_PLANE) * 8
        return carry

    lax.fori_loop(0, ORDER_BLK, body, 0, unroll=16)

    @pl.when(i == 0)
    def _():
        def group(g, carry):
            def pad(p, c):
                src_ref[p] = DUMMY8
                return c
            return lax.fori_loop(pad_lo_ref[g], pad_hi_ref[g], pad, carry)

        lax.fori_loop(0, N_GROUPS, group, 0)


def _order(pos, pad_lo, pad_hi):
    return pl.pallas_call(
        _order_kernel,
        grid=(N_TOK // ORDER_BLK,),
        in_specs=[
            pl.BlockSpec((ORDER_BLK,), lambda i: (i,), memory_space=pltpu.SMEM),
            pl.BlockSpec((ORDER_BLK,), lambda i: (N_TOK // ORDER_BLK + i,), memory_space=pltpu.SMEM),
            pl.BlockSpec(memory_space=pltpu.SMEM),
            pl.BlockSpec(memory_space=pltpu.SMEM),
        ],
        out_specs=pl.BlockSpec(memory_space=pltpu.SMEM),
        out_shape=jax.ShapeDtypeStruct((N_SORTED,), jnp.int32),
        compiler_params=_cp("arbitrary"),
        name="moe_order",
    )(pos, pos, pad_lo, pad_hi)


GATE_BLK = CODE_PLANE + HALF_TOK
ACC_TOK = HALF_TOK + 64
GATHER_GROUP, ACC_GROUP = 16, 8


RES_TB = 256


def _experts_kernel(tile0_ref, ntile_ref, count_ref, src_ref, gs_ref, h_hbm, x_hbm, mod_ref, wg_ref, wu_ref, wd_ref,
                    *rest, final):
    if final:
        fg_ref, *dst_hbm = rest[:3]
        rest = rest[3:]
    else:
        dst_hbm, rest = rest[:1], rest[1:]
    h_res, acc, xbuf, ybuf, wgb, wub, wdb, xin, xout, sem, in_sem, out_sem = rest
    group = pl.program_id(0)
    expert = group % N_EXP
    half = group // N_EXP
    rows0 = pl.multiple_of(half * (HALF_TOK * 8), 8)

    @pl.when(expert == 0)
    def _():
        cp = pltpu.make_async_copy(h_hbm.at[pl.ds(rows0, HALF_TOK * 8), :], h_res.at[pl.ds(0, HALF_TOK * 8), :], sem)
        cp.start()
        h_res[pl.ds(DUMMY8, 8), :] = jnp.zeros((8, 128), F32)
        xbuf[...] = jnp.zeros_like(xbuf)

        def zero(i, carry):
            acc[pl.ds(pl.multiple_of(i * 512, 512), 512), :] = jnp.zeros((512, 128), F32)
            return carry

        lax.fori_loop(0, ACC_TOK * 8 // 512, zero, 0)
        cp.wait()

    n_tiles = ntile_ref[group]

    @pl.when(n_tiles > 0)
    def _():
        wgb[...] = wg_ref[...].astype(BF16)
        wub[...] = wu_ref[...].astype(BF16)
        wdb[...] = wd_ref[...].astype(BF16)

    row0 = tile0_ref[group] * EXP_TM
    row_end = row0 + count_ref[group]

    def process(base, rows):
        live = (jnp.clip(row_end - base, 0, rows) + GATHER_GROUP - 1) // GATHER_GROUP

        def gather(g, c):
            for i in range(GATHER_GROUP):
                r = g * GATHER_GROUP + i
                xbuf[pl.ds(pl.multiple_of(r * 8, 8), 8), :] = _tile_of(h_res, src_ref[base + r] & CODE_MASK)[...]
            return c

        lax.fori_loop(0, live, gather, 0)
        x = _tiles_to_rows(xbuf, rows).astype(BF16)
        hid = _silu(_dot(x, wgb[...])) * _dot(x, wub[...])
        _rows_to_tiles(ybuf, _dot(hid, wdb[...]))

        def accumulate(g, c):
            targets, values = [], []
            for i in range(ACC_GROUP):
                r = g * ACC_GROUP + i
                code = src_ref[base + r]
                target = _tile_of(acc, code & CODE_MASK)
                targets.append(target)
                values.append(target[...] + gs_ref[code >> 3] * ybuf[pl.ds(pl.multiple_of(r * 8, 8), 8), :])
            for target, value in zip(targets, values):
                target[...] = value
            return c

        lax.fori_loop(0, live * (GATHER_GROUP // ACC_GROUP), accumulate, 0)


    def pair_body(j, carry):
        process(row0 + j * (2 * EXP_TM), 2 * EXP_TM)
        return carry

    lax.fori_loop(0, n_tiles // 2, pair_body, 0)

    @pl.when(n_tiles % 2 == 1)
    def _():
        process(row0 + (n_tiles - 1) * EXP_TM, EXP_TM)

    def rows_of(first, blk):
        return pl.ds(pl.multiple_of(first + blk * RES_TB, RES_TB), RES_TB)

    def load_x(blk, slot):
        return pltpu.make_async_copy(x_hbm.at[rows_of(half * HALF_TOK, blk), :], xin.at[slot], in_sem.at[slot])

    def residual(blk, slot):
        cond = jnp.where(half == 0, 0, 1 + blk // (DEC_SEQ // RES_TB))
        gate = mod_ref[cond, 5:6, :]
        y = _tiles_to_rows(acc.at[pl.ds(pl.multiple_of(blk * (RES_TB * 8), RES_TB * 8), RES_TB * 8), :], RES_TB)
        x_new = xin[slot] + gate * y
        xout[slot] = _rms(x_new, fg_ref[...]) if final else x_new

    def epilogue(dst, first_row):
        def store_x(blk, slot):
            return pltpu.make_async_copy(xout.at[slot], dst.at[rows_of(first_row, blk), :], out_sem.at[slot])

        n_pairs = HALF_TOK // RES_TB // 2
        load_x(0, 0).start()

        def pair(p, carry):
            for slot in range(2):
                blk = 2 * p + slot
                if slot == 0:
                    load_x(blk + 1, 1).start()
                else:
                    @pl.when(p + 1 < n_pairs)
                    def _():
                        load_x(blk + 1, 0).start()
                load_x(blk, slot).wait()

                @pl.when(p > 0)
                def _():
                    store_x(blk - 2, slot).wait()

                residual(blk, slot)
                store_x(blk, slot).start()
            return carry

        lax.fori_loop(0, n_pairs, pair, 0)
        store_x(2 * n_pairs - 2, 0).wait()
        store_x(2 * n_pairs - 1, 1).wait()

    if final:
        for which in range(2):
            @pl.when((expert == N_EXP - 1) & (half == which))
            def _():
                epilogue(dst_hbm[which], 0)
    else:
        @pl.when(expert == N_EXP - 1)
        def _():
            epilogue(dst_hbm[0], half * HALF_TOK)


def _experts(tile0, n_tiles, counts, src, gs, h, x_all, mod_l, wg, wu, wd, layer, final_g=None):
    final = final_g is not None
    wmap = lambda g, t0, nt, cnt, src: (layer, g % N_EXP, 0, 0)
    any_spec = pl.BlockSpec(memory_space=pl.ANY)
    extra_specs = [pl.BlockSpec((1, D), lambda g, t0, nt, cnt, src: (0, 0))] if final else []
    extra_args = (final_g,) if final else ()
    return pl.pallas_call(
        functools.partial(_experts_kernel, final=final),
        grid_spec=pltpu.PrefetchScalarGridSpec(
            num_scalar_prefetch=4,
            grid=(N_GROUPS,),
            in_specs=[
                pl.BlockSpec((GATE_BLK,), lambda g, t0, nt, cnt, src: (g // N_EXP,), memory_space=pltpu.SMEM),
                pl.BlockSpec(memory_space=pl.ANY),
                pl.BlockSpec(memory_space=pl.ANY),
                pl.BlockSpec((8, 6, D), lambda g, t0, nt, cnt, src: (0, 0, 0)),
                pl.BlockSpec((None, None, D, D_EXP), wmap),
                pl.BlockSpec((None, None, D, D_EXP), wmap),
                pl.BlockSpec((None, None, D_EXP, D), wmap),
            ] + extra_specs,
            out_specs=[any_spec, any_spec] if final else any_spec,
            scratch_shapes=[
                pltpu.VMEM((ACC_TOK * 8, 128), F32),
                pltpu.VMEM((ACC_TOK * 8, 128), F32),
                pltpu.VMEM((2 * EXP_TM * 8, 128), F32),
                pltpu.VMEM((2 * EXP_TM * 8, 128), F32),
                pltpu.VMEM((D, D_EXP), BF16),
                pltpu.VMEM((D, D_EXP), BF16),
                pltpu.VMEM((D_EXP, D), BF16),
                pltpu.VMEM((2, RES_TB, D), F32),
                pltpu.VMEM((2, RES_TB, D), F32),
                pltpu.SemaphoreType.DMA,
                pltpu.SemaphoreType.DMA((2,)),
                pltpu.SemaphoreType.DMA((2,)),
            ],
        ),
        out_shape=([jax.ShapeDtypeStruct((HALF_TOK, D), F32)] * 2 if final
                   else jax.ShapeDtypeStruct((N_TOK, D), F32)),
        input_output_aliases={} if final else {6: 0},
        compiler_params=_cp("arbitrary"),
        name="moe_experts_final" if final else "moe_experts",
    )(tile0, n_tiles, counts, src, gs, h, x_all, mod_l, wg, wu, wd, *extra_args)


def _moe(x_all, mod_l, n2g, wr, br, wg, wu, wd, layer, final_g=None):
    h, metat, cnt = _router(x_all, mod_l, n2g, wr, br)
    counts = cnt[:, 0, :N_EXP].astype(jnp.int32).reshape(N_GROUPS)
    padded = (counts + EXP_TM - 1) // EXP_TM * EXP_TM
    ends = jnp.cumsum(padded)
    offs = ends - padded
    rec = metat.astype(jnp.int32)
    half = (jnp.arange(N_TOK, dtype=jnp.int32) // HALF_TOK)[None, :]
    group = rec[M_E1:M_E2 + 1] + N_EXP * half
    is_group = group[None] == jnp.arange(N_GROUPS, dtype=jnp.int32)[:, None, None]
    pos = jnp.sum(jnp.where(is_group, offs[:, None, None], 0), axis=0) + rec[M_R1:M_R2 + 1]
    live_end = offs + (counts + GATHER_GROUP - 1) // GATHER_GROUP * GATHER_GROUP
    src = _order(pos.reshape(N_ASSIGN), offs + counts, live_end)
    g12 = metat[M_G1:M_G2 + 1].reshape(2, 2, HALF_TOK)
    gates = jnp.concatenate([g12[0], jnp.zeros((2, CODE_PLANE - HALF_TOK), F32), g12[1]], axis=1)
    return _experts(offs // EXP_TM, padded // EXP_TM, counts, src, gates.reshape(2 * GATE_BLK), h, x_all, mod_l,
                    wg, wu, wd, layer, final_g)


def _rope_tables():
    pos = jnp.arange(DEC_SEQ)
    row = (pos // GRID_W).astype(F32)
    col = (pos % GRID_W).astype(F32)
    n_freq = HD // 4
    inv = ROPE_THETA ** (-jnp.arange(n_freq, dtype=F32) / n_freq)
    ang = jnp.concatenate([row[:, None] * inv, col[:, None] * inv], axis=-1)
    cos = jnp.repeat(jnp.cos(ang), 2, axis=-1)
    sin = jnp.repeat(jnp.sin(ang), 2, axis=-1) * jnp.tile(jnp.array([-1.0, 1.0], F32), HD // 2)
    nblk = DEC_SEQ // QKV_TB
    cos_tab = jnp.concatenate([jnp.ones((1, QKV_TB, HD), F32), cos.reshape(nblk, QKV_TB, HD)], axis=0)
    sin_tab = jnp.concatenate([jnp.zeros((1, QKV_TB, HD), F32), sin.reshape(nblk, QKV_TB, HD)], axis=0)
    return cos_tab, sin_tab


def kernel(x_prompt, x_sample, state_gla, cache_k, cache_v, c, c_ctx, w_mod, b_mod, norm1_g, norm2_g,
           w_in_even, w_gate_up, b_gate_up, gla_norm_g, w_spatial, b_spatial, w_out_even,
           w_in_odd, q_norm_g, k_norm_g, w_out_odd, w_router_group, b_router_group,
           w_router_expert, b_router_expert, w_exp_gate, w_exp_up, w_exp_down, final_norm_g):
    x_all = None
    cond8 = jnp.concatenate([c_ctx[None], c, jnp.zeros((3, D), F32)], axis=0)
    mod = _modulation(cond8, w_mod, b_mod)
    cos_tab, sin_tab = _rope_tables()
    zero_state = jnp.zeros((1, 2, QK_W, GLA_DV), F32)
    state_in = state_gla.reshape(DEC_BATCH, -1, 2, QK_W, GLA_DV)
    cache_k2 = cache_k.reshape(DEC_BATCH, -1, SEQ, KV_W)
    cache_v2 = cache_v.reshape(DEC_BATCH, -1, SEQ, KV_W)

    gla_states, caches = [], None
    for l in range(DEPTH):
        i = l // 2
        n1g = norm1_g[l][None]
        if l % 2 == 0:
            w = w_in_even[i]
            win = jnp.concatenate([w[:, :1536], w[:, 1568:], w[:, 1536:1568], jnp.zeros((D, 96), F32)],
                                  axis=1).astype(BF16)
            wgu = jnp.zeros((128, 2 * QK_W), F32)
            wgu = wgu.at[0:GLA_RANK, 0:QK_W].set(w_gate_up[i, 0])
            wgu = wgu.at[GLA_RANK:2 * GLA_RANK, QK_W:].set(w_gate_up[i, 1]).astype(BF16)
            bgu = b_gate_up[i].reshape(1, 2 * QK_W)
            args = (mod[l], n1g, win, wgu, bgu, gla_norm_g[i][None], w_spatial[i].astype(BF16),
                    b_spatial[i].T, w_out_even[i].astype(BF16))
            first = l == 0
            x_all, st = _even_mixer(x_all, *args, zero_state, latent=False,
                                    x_first=x_prompt.reshape(N_CTX, D) if first else None)
            gla_states.append(st)
            x_all, _ = _even_mixer(x_all, *args, state_in[:, i], latent=True,
                                   x_first=x_sample.reshape(N_LAT, D) if first else None)
        else:
            q, k, v, *caches = _qkv(x_all, mod[l], n1g, w_in_odd[i].astype(BF16), q_norm_g[i][None],
                                    k_norm_g[i][None], cos_tab, sin_tab, i, caches)
            wout = w_out_odd[i].astype(BF16)
            x_all = _attention(x_all, mod[l], q, k, v, wout)
            x_all = _attention(x_all, mod[l], q, k, v, wout, cache_k2, cache_v2, layer_i=i)
        wr = jnp.concatenate([w_router_expert[l], w_router_group[l],
                              jnp.zeros((D, 128 - N_EXP - MOE_GROUPS), F32)], axis=1)
        br = jnp.concatenate([b_router_expert[l], b_router_group[l],
                              jnp.zeros((128 - N_EXP - MOE_GROUPS,), F32)])[None]
        x_all = _moe(x_all, mod[l], norm2_g[l][None], wr, br, w_exp_gate, w_exp_up, w_exp_down, l,
                     final_norm_g[None] if l == DEPTH - 1 else None)

    y_prompt = x_all[0].reshape(BATCH, SEQ, D)
    y_sample = x_all[1].reshape(DEC_BATCH, DEC_SEQ, D)
    new_state = jnp.stack(gla_states, axis=1).reshape(BATCH, -1, 2, GLA_HEADS, GLA_DK, GLA_DV)
    new_k, new_v = (a.reshape(BATCH, DEPTH // 2, SEQ, ATT_KV, HD) for a in caches)
    return (y_prompt, y_sample, new_state, new_k, new_v)
```

```python
import functools

import jax
import jax.numpy as jnp
import numpy as np
from jax import lax
from jax.experimental import pallas as pl
from jax.experimental.pallas import tpu as pltpu

F32 = jnp.float32
BF16 = jnp.bfloat16

D = 1024
BATCH, SEQ = 16, 256
DEC_BATCH, DEC_SEQ = 4, 1024
N_CTX = BATCH * SEQ
N_LAT = DEC_BATCH * DEC_SEQ
N_TOK = N_CTX + N_LAT
DEPTH = 4
EPS = 1e-6
GRID_W = 64
ROPE_THETA = 10000.0

GLA_HEADS, GLA_DK, GLA_DV, GLA_RANK, GLA_CHUNK, GLA_TAU = 4, 64, 128, 16, 128, 16.0
QK_W = GLA_HEADS * GLA_DK
V_W = GLA_HEADS * GLA_DV
GMLP_GROUPS, GMLP_DIM, GMLP_CHUNK = 4, 128, 128
GMLP_W = GMLP_GROUPS * GMLP_DIM
C_Q, C_K, C_V, C_G, C_U, C_VG, C_A = 0, 256, 512, 1024, 1536, 2048, 2560
EVEN_PACK = 2688

ATT_HEADS, ATT_KV, HD = 8, 2, 128
ATT_G = ATT_HEADS // ATT_KV
Q_W = ATT_HEADS * HD
KV_W = ATT_KV * HD

MOE_GROUPS, MOE_PER_GROUP = 4, 8
N_EXP = MOE_GROUPS * MOE_PER_GROUP
D_EXP = D // 4
NEG = -1e30

VMEM_LIMIT = 56 * 1024 * 1024


def _cp(*sem):
    return pltpu.CompilerParams(dimension_semantics=sem, vmem_limit_bytes=VMEM_LIMIT)


def _dot(a, b):
    return jnp.dot(a.astype(BF16), b.astype(BF16), preferred_element_type=F32)


def _dot_nt(a, b):
    return lax.dot_general(a.astype(BF16), b.astype(BF16), (((1,), (1,)), ((), ())),
                           preferred_element_type=F32)


def _dot_tn(a, b):
    return lax.dot_general(a.astype(BF16), b.astype(BF16), (((0,), (0,)), ((), ())),
                           preferred_element_type=F32)


def _rms(x, g):
    return x * lax.rsqrt(jnp.mean(x * x, axis=-1, keepdims=True) + EPS) * g


def _silu(x):
    return x * jax.nn.sigmoid(x)


def _gelu(x):
    return 0.5 * x * (1.0 + jnp.tanh(np.sqrt(2.0 / np.pi).astype(np.float32) * (x + 0.044715 * (x * x * x))))


def _log_sigmoid(z):
    return jnp.minimum(z, 0.0) - jnp.log(1.0 + jnp.exp(-jnp.abs(z)))


def _rows_to_tiles(ref, x):
    rows = x.shape[0]
    for j in range(D // 128):
        ref[pl.ds(j, rows, stride=8), :] = x[:, j * 128:(j + 1) * 128]


def _tiles_to_rows(ref, rows):
    return jnp.concatenate([ref[pl.ds(j, rows, stride=8), :] for j in range(D // 128)], axis=1)


def _tile_of(ref, row8):
    return ref.at[pl.ds(pl.multiple_of(row8, 8), 8), :]


def _resident(shape, index_map):
    return pl.BlockSpec(shape, index_map, pipeline_mode=pl.Buffered(1))


def _split_bf16(x):
    hi = x.astype(BF16)
    return hi, (x - hi.astype(F32)).astype(BF16)


def _mod_kernel(cond_ref, w_ref, b_ref, o_ref):
    s_hi, s_lo = _split_bf16(_silu(cond_ref[...]))
    w_hi, w_lo = _split_bf16(w_ref[...])
    dot = functools.partial(jnp.dot, preferred_element_type=F32)
    both = dot(jnp.concatenate([s_hi, s_lo], axis=0), w_hi)
    o_ref[...] = both[0:8] + both[8:16] + dot(s_hi, w_lo) + b_ref[...]


def _modulation(cond8, w_mod, b_mod):
    tn = 1024
    out = pl.pallas_call(
        _mod_kernel,
        grid=(DEPTH, 6 * D // tn),
        in_specs=[
            pl.BlockSpec((8, D), lambda l, j: (0, 0)),
            pl.BlockSpec((None, D, tn), lambda l, j: (l, 0, j)),
            pl.BlockSpec((None, 1, tn), lambda l, j: (l, 0, j)),
        ],
        out_specs=pl.BlockSpec((None, 8, tn), lambda l, j: (l, 0, j)),
        out_shape=jax.ShapeDtypeStruct((DEPTH, 8, 6 * D), F32),
        compiler_params=_cp("arbitrary", "arbitrary"),
        name="adaln_mod",
    )(cond8, w_mod, b_mod.reshape(DEPTH, 1, 6 * D))
    return out.reshape(DEPTH, 8, 6, D)


def _even_kernel(x_ref, mod_ref, n1g_ref, win_ref, wgu_ref, bgu_ref, glag_ref, ws_ref, bs_ref,
                 wout_ref, s0_ref, xo_ref, st_ref, proj, la, o_f, o_b, st_scr, *, T):
    n_chunks = T // GLA_CHUNK
    shift, scale, gate = mod_ref[0:1, :], mod_ref[1:2, :], mod_ref[2:3, :]
    RB = 128
    PROJ_RB = 256

    def proj_body(r, carry):
        r0 = pl.multiple_of(r * PROJ_RB, PROJ_RB)
        h = _rms(x_ref[pl.ds(r0, PROJ_RB), :], n1g_ref[...]) * (1.0 + scale) + shift
        p = _dot(h, win_ref[...])
        proj[pl.ds(r0, PROJ_RB), :] = p
        z = _dot(p[:, C_A:C_A + 128], wgu_ref[...]) + bgu_ref[...]
        la[pl.ds(r0, PROJ_RB), :] = _log_sigmoid(z) * (1.0 / GLA_TAU)
        return carry

    lax.fori_loop(0, T // PROJ_RB, proj_body, 0)

    st_scr[0] = s0_ref[0].T
    st_scr[1] = s0_ref[1].T

    ci = lax.broadcasted_iota(jnp.int32, (GLA_CHUNK, GLA_CHUNK), 0)
    cj = lax.broadcasted_iota(jnp.int32, (GLA_CHUNK, GLA_CHUNK), 1)
    tri = (jnp.where(ci >= cj, 1.0, 0.0).astype(BF16), jnp.where(ci <= cj, 1.0, 0.0).astype(BF16))
    ai = lax.broadcasted_iota(jnp.int32, (GLA_HEADS * GLA_CHUNK, GLA_CHUNK), 0) % GLA_CHUNK
    aj = lax.broadcasted_iota(jnp.int32, (GLA_HEADS * GLA_CHUNK, GLA_CHUNK), 1)
    amask = (ai >= aj, ai <= aj)
    lane_head = lax.broadcasted_iota(jnp.int32, (1, QK_W), 1) // GLA_DK
    hmask = [jnp.where(lane_head == h, 1.0, 0.0) for h in range(GLA_HEADS)]

    def chunk_body(i, carry):
        for d in range(2):
            c = i if d == 0 else n_chunks - 1 - i
            r0 = pl.multiple_of(c * GLA_CHUNK, GLA_CHUNK)
            q = proj[pl.ds(r0, GLA_CHUNK), C_Q:C_Q + QK_W] * (GLA_DK ** -0.5)
            k = proj[pl.ds(r0, GLA_CHUNK), C_K:C_K + QK_W]
            v = proj[pl.ds(r0, GLA_CHUNK), C_V:C_V + V_W]
            lac = la[pl.ds(r0, GLA_CHUNK), d * QK_W:(d + 1) * QK_W]
            hi = lac.astype(BF16)
            lo = (lac - hi.astype(F32)).astype(BF16)
            b = (jnp.dot(tri[d], hi, preferred_element_type=F32)
                 + jnp.dot(tri[d], lo, preferred_element_type=F32))
            bend = b[GLA_CHUNK - 1:GLA_CHUNK, :] if d == 0 else b[0:1, :]
            qe = q * jnp.exp(b)
            ke = k * jnp.exp(-b)
            kd = k * jnp.exp(bend - b)
            st = st_scr[d]
            qstack = jnp.concatenate([qe * hmask[h] for h in range(GLA_HEADS)], axis=0).astype(BF16)
            att = jnp.where(amask[d], _dot_nt(qstack, ke), 0.0)
            inter = _dot_nt(qstack, st)
            outs = []
            for h in range(GLA_HEADS):
                rows = slice(h * GLA_CHUNK, (h + 1) * GLA_CHUNK)
                outs.append(_dot(att[rows], v[:, h * GLA_DV:(h + 1) * GLA_DV]) + inter[rows])
            o = jnp.concatenate(outs, axis=1)
            if d == 0:
                o_f[pl.ds(r0, GLA_CHUNK), :] = o
            else:
                o_b[pl.ds(r0, GLA_CHUNK), :] = o
            vstack = jnp.concatenate([v[:, h * GLA_DV:(h + 1) * GLA_DV] for h in range(GLA_HEADS)], axis=0)
            kstack = jnp.concatenate([kd * hmask[h] for h in range(GLA_HEADS)], axis=0)
            st_scr[d] = st * jnp.exp(bend) + _dot_tn(vstack, kstack)
        return carry

    lax.fori_loop(0, n_chunks, chunk_body, 0, unroll=2)
    st_ref[0] = st_scr[0].T
    st_ref[1] = st_scr[1].T

    def out_body(r, carry):
        r0 = pl.multiple_of(r * RB, RB)
        osum = o_f[pl.ds(r0, RB), :] + o_b[pl.ds(r0, RB), :]
        g = proj[pl.ds(r0, RB), C_G:C_G + V_W]
        u = proj[pl.ds(r0, RB), C_U:C_U + GMLP_W]
        vg = _gelu(proj[pl.ds(r0, RB), C_VG:C_VG + GMLP_W])
        parts = []
        for h in range(GLA_HEADS):
            oh = osum[:, h * GLA_DV:(h + 1) * GLA_DV]
            parts.append(_rms(oh, glag_ref[...]) * _silu(g[:, h * GLA_DV:(h + 1) * GLA_DV]))
        for gi in range(GMLP_GROUPS):
            vc = vg[:, gi * GMLP_DIM:(gi + 1) * GMLP_DIM]
            vc = vc - jnp.mean(vc, axis=-1, keepdims=True)
            vn = vc * lax.rsqrt(jnp.mean(vc * vc, axis=-1, keepdims=True) + EPS)
            sg = _dot(ws_ref[gi], vn) + bs_ref[:, gi:gi + 1]
            parts.append(_gelu(u[:, gi * GMLP_DIM:(gi + 1) * GMLP_DIM]) * sg)
        mix = jnp.concatenate(parts, axis=1)
        y = _dot(mix, wout_ref[...])
        xo_ref[pl.ds(r0, RB), :] = x_ref[pl.ds(r0, RB), :] + gate * y
        return carry

    lax.fori_loop(0, T // RB, out_body, 0)


def _even_mixer(x_all, mod_l, n1g, win, wgu, bgu, glag, ws, bs, wout, s0, *, latent, x_first=None):
    if latent:
        T, nseq, blk0 = DEC_SEQ, DEC_BATCH, N_CTX // DEC_SEQ
        cond = lambda i: 1 + i
        s0_spec = pl.BlockSpec((None, 2, QK_W, GLA_DV), lambda i: (i, 0, 0, 0))
    else:
        T, nseq, blk0 = SEQ, BATCH, 0
        cond = lambda i: 0
        s0_spec = pl.BlockSpec((None, 2, QK_W, GLA_DV), lambda i: (0, 0, 0, 0))
    const2 = lambda i: (0, 0)
    body = functools.partial(_even_kernel, T=T)
    x_spec = pl.BlockSpec((T, D), lambda i: (blk0 + i, 0))
    if x_first is None:
        lead_specs, lead_args, aliases = [x_spec], (x_all,), {0: 0}
    elif x_all is None:
        lead_specs, lead_args, aliases = [pl.BlockSpec((T, D), lambda i: (i, 0))], (x_first,), {}
    else:
        lead_specs = [pl.BlockSpec(memory_space=pl.ANY), pl.BlockSpec((T, D), lambda i: (i, 0))]
        lead_args, aliases = (x_all, x_first), {0: 0}
        body = lambda dst_ref, *refs: _even_kernel(*refs, T=T)
    x_new, states = pl.pallas_call(
        body,
        grid=(nseq,),
        in_specs=lead_specs + [
            pl.BlockSpec((None, 6, D), lambda i: (cond(i), 0, 0)),
            _resident((1, D), const2),
            _resident((D, EVEN_PACK), const2),
            _resident((128, 2 * QK_W), const2),
            _resident((1, 2 * QK_W), const2),
            _resident((1, GLA_DV), const2),
            _resident((GMLP_GROUPS, GMLP_CHUNK, GMLP_CHUNK), lambda i: (0, 0, 0)),
            _resident((GMLP_CHUNK, GMLP_GROUPS), const2),
            _resident((D, D), const2),
            s0_spec,
        ],
        out_specs=[
            x_spec,
            pl.BlockSpec((None, 2, QK_W, GLA_DV), lambda i: (i, 0, 0, 0)),
        ],
        out_shape=[
            jax.ShapeDtypeStruct((N_TOK, D), F32),
            jax.ShapeDtypeStruct((nseq, 2, QK_W, GLA_DV), F32),
        ],
        scratch_shapes=[
            pltpu.VMEM((T, EVEN_PACK), F32),
            pltpu.VMEM((T, 2 * QK_W), F32),
            pltpu.VMEM((T, V_W), F32),
            pltpu.VMEM((T, V_W), F32),
            pltpu.VMEM((2, GLA_DV, QK_W), F32),
        ],
        input_output_aliases=aliases,
        compiler_params=_cp("arbitrary"),
        name="even_mixer_latent" if latent else "even_mixer_context",
    )(*lead_args, mod_l, n1g, win, wgu, bgu, glag, ws, bs, wout, s0)
    return x_new, states


QKV_TB = 512


def _qkv_kernel(x_ref, mod_ref, n1g_ref, win_ref, gq_ref, gk_ref, cos_ref, sin_ref, q_ref, k_ref, v_ref,
                ck_ref, cv_ref):
    shift, scale = mod_ref[0:1, :], mod_ref[1:2, :]
    h = _rms(x_ref[...], n1g_ref[...]) * (1.0 + scale) + shift
    p = _dot(h, win_ref[...])
    cos, sin = cos_ref[...], sin_ref[...]
    even_lane = lax.broadcasted_iota(jnp.int32, (1, HD), 1) % 2 == 0

    def rope(xn):
        swapped = jnp.where(even_lane, pltpu.roll(xn, HD - 1, axis=1), pltpu.roll(xn, 1, axis=1))
        return xn * cos + swapped * sin

    def emit(rotate, to_cache):
        for hh in range(ATT_HEADS):
            qn = _rms(p[:, hh * HD:(hh + 1) * HD], gq_ref[...])
            q_ref[:, hh * HD:(hh + 1) * HD] = (rotate(qn) * (HD ** -0.5)).astype(BF16)
        for hh in range(ATT_KV):
            kn = rotate(_rms(p[:, Q_W + hh * HD:Q_W + (hh + 1) * HD], gk_ref[...]))
            k_ref[:, hh * HD:(hh + 1) * HD] = kn
            if to_cache:
                for s in range(QKV_TB // SEQ):
                    ck_ref[s, :, hh * HD:(hh + 1) * HD] = kn[s * SEQ:(s + 1) * SEQ]
        if to_cache:
            for s in range(QKV_TB // SEQ):
                cv_ref[s] = p[s * SEQ:(s + 1) * SEQ, Q_W + KV_W:]

    is_latent = pl.program_id(0) >= N_CTX // QKV_TB

    @pl.when(is_latent)
    def _():
        emit(rope, False)

    @pl.when(jnp.logical_not(is_latent))
    def _():
        emit(lambda xn: xn, True)

    v_ref[...] = p[:, Q_W + KV_W:]


def _qkv(x_all, mod_l, n1g, win, gq, gk, cos_tab, sin_tab, layer_i, caches=None):
    nb_ctx = N_CTX // QKV_TB
    per_seq = DEC_SEQ // QKV_TB
    cond = lambda i: jnp.where(i < nb_ctx, 0, 1 + (i - nb_ctx) // per_seq)
    tab = lambda i: jnp.where(i < nb_ctx, 0, 1 + (i - nb_ctx) % per_seq)
    const2 = lambda i: (0, 0)
    cache_spec = pl.BlockSpec((QKV_TB // SEQ, None, SEQ, KV_W), lambda i: (jnp.minimum(i, nb_ctx - 1), layer_i, 0, 0))
    cache_shape = jax.ShapeDtypeStruct((BATCH, DEPTH // 2, SEQ, KV_W), F32)
    if caches is None:
        body, lead_specs, lead_args, aliases = _qkv_kernel, [], (), {}
    else:
        body = lambda ck_in, cv_in, *refs: _qkv_kernel(*refs)
        lead_specs = [pl.BlockSpec(memory_space=pl.ANY)] * 2
        lead_args, aliases = tuple(caches), {0: 3, 1: 4}
    return pl.pallas_call(
        body,
        grid=(N_TOK // QKV_TB,),
        in_specs=lead_specs + [
            pl.BlockSpec((QKV_TB, D), lambda i: (i, 0)),
            pl.BlockSpec((None, 6, D), lambda i: (cond(i), 0, 0)),
            _resident((1, D), const2),
            _resident((D, Q_W + 2 * KV_W), const2),
            _resident((1, HD), const2),
            _resident((1, HD), const2),
            pl.BlockSpec((None, QKV_TB, HD), lambda i: (tab(i), 0, 0)),
            pl.BlockSpec((None, QKV_TB, HD), lambda i: (tab(i), 0, 0)),
        ],
        out_specs=[
            pl.BlockSpec((QKV_TB, Q_W), lambda i: (i, 0)),
            pl.BlockSpec((QKV_TB, KV_W), lambda i: (i, 0)),
            pl.BlockSpec((QKV_TB, KV_W), lambda i: (i, 0)),
            cache_spec,
            cache_spec,
        ],
        out_shape=[
            jax.ShapeDtypeStruct((N_TOK, Q_W), BF16),
            jax.ShapeDtypeStruct((N_TOK, KV_W), F32),
            jax.ShapeDtypeStruct((N_TOK, KV_W), F32),
            cache_shape,
            cache_shape,
        ],
        input_output_aliases=aliases,
        compiler_params=_cp("arbitrary"),
        name="odd_qkv",
    )(*lead_args, x_all, mod_l, n1g, win, gq, gk, cos_tab, sin_tab)


ATT_TQ_LATENT = 512


def _attn_kernel(*refs, n_kv):
    q_ref = refs[0]
    kv_refs = refs[1:1 + 2 * n_kv]
    x_ref, mod_ref, wout_ref, xo_ref, att_scr = refs[1 + 2 * n_kv:]
    gate = mod_ref[2:3, :]
    for kh in range(ATT_KV):
        ks = [kv_refs[2 * s][:, kh * HD:(kh + 1) * HD].astype(BF16) for s in range(n_kv)]
        vs = [jnp.concatenate([kv_refs[2 * s + 1][:, kh * HD:(kh + 1) * HD].astype(BF16),
                               jnp.ones((kv_refs[2 * s + 1].shape[0], HD), BF16)], axis=1) for s in range(n_kv)]
        for g in range(ATT_G):
            hh = kh * ATT_G + g
            qh = q_ref[:, hh * HD:(hh + 1) * HD]
            ss = [_dot_nt(qh, kk) for kk in ks]
            m = ss[0].max(axis=-1, keepdims=True)
            for s in ss[1:]:
                m = jnp.maximum(m, s.max(axis=-1, keepdims=True))
            o = _dot(jnp.exp(ss[0] - m), vs[0])
            for s, vv in zip(ss[1:], vs[1:]):
                o = o + _dot(jnp.exp(s - m), vv)
            att_scr[:, hh * HD:(hh + 1) * HD] = o[:, :HD] / o[:, HD:HD + 1]
    y = _dot(att_scr[...], wout_ref[...])
    xo_ref[...] = x_ref[...] + gate * y


def _attention(x_all, mod_l, q, k, v, wout, cache_k=None, cache_v=None, layer_i=0):
    latent = cache_k is not None
    const2 = lambda *a: (0, 0)
    tq = ATT_TQ_LATENT if latent else SEQ
    if latent:
        nq = DEC_SEQ // tq
        row_blk = lambda b, j: (N_CTX // tq + b * nq + j, 0)
        grid = (DEC_BATCH, nq)
        kv_specs = [
            pl.BlockSpec((None, None, SEQ, KV_W), lambda b, j: (b, layer_i, 0, 0)),
            pl.BlockSpec((None, None, SEQ, KV_W), lambda b, j: (b, layer_i, 0, 0)),
            pl.BlockSpec((DEC_SEQ, KV_W), lambda b, j: (N_CTX // DEC_SEQ + b, 0)),
            pl.BlockSpec((DEC_SEQ, KV_W), lambda b, j: (N_CTX // DEC_SEQ + b, 0)),
        ]
        kv_args = (cache_k, cache_v, k, v)
        mod_spec = pl.BlockSpec((None, 6, D), lambda b, j: (1 + b, 0, 0))
        sem = ("arbitrary", "arbitrary")
        n_kv = 2
    else:
        row_blk = lambda i: (i, 0)
        grid = (BATCH,)
        kv_specs = [pl.BlockSpec((SEQ, KV_W), row_blk), pl.BlockSpec((SEQ, KV_W), row_blk)]
        kv_args = (k, v)
        mod_spec = pl.BlockSpec((None, 6, D), lambda i: (0, 0, 0))
        sem = ("arbitrary",)
        n_kv = 1
    n_in = 1 + len(kv_args)
    return pl.pallas_call(
        functools.partial(_attn_kernel, n_kv=n_kv),
        grid=grid,
        in_specs=[pl.BlockSpec((tq, Q_W), row_blk)] + kv_specs + [
            pl.BlockSpec((tq, D), row_blk),
            mod_spec,
            _resident((D, D), const2),
        ],
        out_specs=pl.BlockSpec((tq, D), row_blk),
        out_shape=jax.ShapeDtypeStruct((N_TOK, D), F32),
        scratch_shapes=[pltpu.VMEM((tq, Q_W), F32)],
        input_output_aliases={n_in: 0},
        compiler_params=_cp(*sem),
        name="attention_latent" if latent else "attention_context",
    )(q, *kv_args, x_all, mod_l, wout)


ROUTE_TB = 512
HALF_TOK = N_TOK // 2
M_E1, M_E2, M_G1, M_G2, M_R1, M_R2 = 0, 1, 2, 3, 4, 5


def _router_kernel(x_ref, mod_ref, n2g_ref, w2_ref, br_ref, h_ref, metat_ref, cnt_ref, run):
    @pl.when(pl.program_id(0) % (HALF_TOK // ROUTE_TB) == 0)
    def _():
        run[...] = jnp.zeros_like(run)

    shift, scale = mod_ref[3:4, :], mod_ref[4:5, :]
    h = _rms(x_ref[...], n2g_ref[...]) * (1.0 + scale) + shift
    _rows_to_tiles(h_ref, h)
    h_hi, h_lo = _split_bf16(h)
    dot = functools.partial(jnp.dot, preferred_element_type=F32)
    wide = dot(h_hi, w2_ref[...])
    logits = wide[:, :128] + wide[:, 128:] + dot(h_lo, w2_ref[:, :128]) + br_ref[...]
    lane = lax.broadcasted_iota(jnp.int32, logits.shape, 1).astype(F32)
    big = 1e4

    def first_argmax(vals):
        m = vals.max(axis=-1, keepdims=True)
        return m, jnp.where(vals == m, lane, big).min(axis=-1, keepdims=True)

    gl = jnp.where((lane >= N_EXP) & (lane < N_EXP + MOE_GROUPS), logits, NEG)
    gmax, glane = first_argmax(gl)
    g_p = 1.0 / jnp.exp(gl - gmax).sum(axis=-1, keepdims=True)
    lo = (glane - N_EXP) * MOE_PER_GROUP
    el = jnp.where((lane >= lo) & (lane < lo + MOE_PER_GROUP), logits, NEG)
    m1, i1 = first_argmax(el)
    m2, i2 = first_argmax(jnp.where(lane == i1, NEG, el))
    t = jnp.exp(m2 - m1)
    w1 = 1.0 / (1.0 + t)
    sel1, sel2 = lane == i1, lane == i2
    onehot = jnp.where(sel1 | sel2, 1.0, 0.0)
    ri = lax.broadcasted_iota(jnp.int32, (ROUTE_TB, ROUTE_TB), 0)
    rj = lax.broadcasted_iota(jnp.int32, (ROUTE_TB, ROUTE_TB), 1)
    before = _dot(jnp.where(ri > rj, 1.0, 0.0), onehot) + run[...]
    r1 = jnp.where(sel1, before, 0.0).sum(axis=-1, keepdims=True)
    r2 = jnp.where(sel2, before, 0.0).sum(axis=-1, keepdims=True)
    run[...] += onehot.sum(axis=0, keepdims=True)
    cnt_ref[...] = run[...]
    meta = jnp.zeros_like(logits)
    for j, val in enumerate([i1, i2, w1 * g_p, (t * w1) * g_p, r1, r2]):
        meta = jnp.where(lane == j, val, meta)
    metat_ref[...] = meta.T[0:8, :]


def _router(x_all, mod_l, n2g, wr, br):
    w2 = jnp.concatenate(_split_bf16(wr), axis=1)
    nb_ctx = N_CTX // ROUTE_TB
    per_seq = DEC_SEQ // ROUTE_TB
    cond = lambda i: jnp.where(i < nb_ctx, 0, 1 + (i - nb_ctx) // per_seq)
    const2 = lambda i: (0, 0)
    return pl.pallas_call(
        _router_kernel,
        grid=(N_TOK // ROUTE_TB,),
        in_specs=[
            pl.BlockSpec((ROUTE_TB, D), lambda i: (i, 0)),
            pl.BlockSpec((None, 6, D), lambda i: (cond(i), 0, 0)),
            _resident((1, D), const2),
            _resident((D, 256), const2),
            _resident((1, 128), const2),
        ],
        out_specs=[
            pl.BlockSpec((ROUTE_TB * 8, 128), lambda i: (i, 0)),
            pl.BlockSpec((8, ROUTE_TB), lambda i: (0, i)),
            pl.BlockSpec((None, 1, 128), lambda i: (i // (HALF_TOK // ROUTE_TB), 0, 0)),
        ],
        out_shape=[
            jax.ShapeDtypeStruct((N_TOK * 8, 128), F32),
            jax.ShapeDtypeStruct((8, N_TOK), F32),
            jax.ShapeDtypeStruct((2, 1, 128), F32),
        ],
        scratch_shapes=[pltpu.VMEM((1, 128), F32)],
        compiler_params=_cp("arbitrary"),
        name="moe_router",
    )(x_all, mod_l, n2g, w2, br)


EXP_TM = 128
N_ASSIGN = 2 * N_TOK
N_GROUPS = 2 * N_EXP
MAX_TILES = N_ASSIGN // EXP_TM + N_GROUPS
N_SORTED = MAX_TILES * EXP_TM
ORDER_BLK = 2048
CODE_PLANE = 2 * HALF_TOK
CODE_MASK = 8 * CODE_PLANE - 1
DUMMY8 = HALF_TOK * 8


def _order_kernel(pos1_ref, pos2_ref, pad_lo_ref, pad_hi_ref, src_ref):
    i = pl.program_id(0)
    local = (i % (HALF_TOK // ORDER_BLK)) * ORDER_BLK

    def body(t, carry):
        src_ref[pos1_ref[t]] = (local + t) * 8
        src_ref[pos2_ref[t]] = (local + t + CODE_PLANE) * 8
        return carry

    lax.fori_loop(0, ORDER_BLK, body, 0, unroll=16)

    @pl.when(i == 0)
    def _():
        def group(g, carry):
            def pad(p, c):
                src_ref[p] = DUMMY8
                return c
            return lax.fori_loop(pad_lo_ref[g], pad_hi_ref[g], pad, carry)

        lax.fori_loop(0, N_GROUPS, group, 0)


def _order(pos, pad_lo, pad_hi):
    return pl.pallas_call(
        _order_kernel,
        grid=(N_TOK // ORDER_BLK,),
        in_specs=[
            pl.BlockSpec((ORDER_BLK,), lambda i: (i,), memory_space=pltpu.SMEM),
            pl.BlockSpec((ORDER_BLK,), lambda i: (N_TOK // ORDER_BLK + i,), memory_space=pltpu.SMEM),
            pl.BlockSpec(memory_space=pltpu.SMEM),
            pl.BlockSpec(memory_space=pltpu.SMEM),
        ],
        out_specs=pl.BlockSpec(memory_space=pltpu.SMEM),
        out_shape=jax.ShapeDtypeStruct((N_SORTED,), jnp.int32),
        compiler_params=_cp("arbitrary"),
        name="moe_order",
    )(pos, pos, pad_lo, pad_hi)


GATE_BLK = CODE_PLANE + HALF_TOK
ACC_TOK = HALF_TOK + 64
GATHER_GROUP, ACC_GROUP = 16, 8


RES_TB = 256


def _experts_kernel(tile0_ref, ntile_ref, count_ref, src_ref, gs_ref, h_hbm, x_hbm, mod_ref, wg_ref, wu_ref, wd_ref,
                    *rest, final):
    if final:
        fg_ref, *dst_hbm = rest[:3]
        rest = rest[3:]
    else:
        dst_hbm, rest = rest[:1], rest[1:]
    h_res, acc, xbuf, ybuf, wgb, wub, wdb, xin, xout, sem, in_sem, out_sem = rest
    group = pl.program_id(0)
    expert = group % N_EXP
    half = group // N_EXP
    rows0 = pl.multiple_of(half * (HALF_TOK * 8), 8)

    @pl.when(expert == 0)
    def _():
        cp = pltpu.make_async_copy(h_hbm.at[pl.ds(rows0, HALF_TOK * 8), :], h_res.at[pl.ds(0, HALF_TOK * 8), :], sem)
        cp.start()
        h_res[pl.ds(DUMMY8, 8), :] = jnp.zeros((8, 128), F32)
        xbuf[...] = jnp.zeros_like(xbuf)

        def zero(i, carry):
            acc[pl.ds(pl.multiple_of(i * 512, 512), 512), :] = jnp.zeros((512, 128), F32)
            return carry

        lax.fori_loop(0, ACC_TOK * 8 // 512, zero, 0)
        cp.wait()

    n_tiles = ntile_ref[group]

    @pl.when(n_tiles > 0)
    def _():
        wgb[...] = wg_ref[...].astype(BF16)
        wub[...] = wu_ref[...].astype(BF16)
        wdb[...] = wd_ref[...].astype(BF16)

    row0 = tile0_ref[group] * EXP_TM
    row_end = row0 + count_ref[group]

    def process(base, rows):
        live = (jnp.clip(row_end - base, 0, rows) + GATHER_GROUP - 1) // GATHER_GROUP

        def gather(g, c):
            for i in range(GATHER_GROUP):
                r = g * GATHER_GROUP + i
                xbuf[pl.ds(pl.multiple_of(r * 8, 8), 8), :] = _tile_of(h_res, src_ref[base + r] & CODE_MASK)[...]
            return c

        lax.fori_loop(0, live, gather, 0)
        x = _tiles_to_rows(xbuf, rows).astype(BF16)
        hid = _silu(_dot(x, wgb[...])) * _dot(x, wub[...])
        _rows_to_tiles(ybuf, _dot(hid, wdb[...]))

        def accumulate(g, c):
            targets, values = [], []
            for i in range(ACC_GROUP):
                r = g * ACC_GROUP + i
                code = src_ref[base + r]
                target = _tile_of(acc, code & CODE_MASK)
                targets.append(target)
                values.append(target[...] + gs_ref[code >> 3] * ybuf[pl.ds(pl.multiple_of(r * 8, 8), 8), :])
            for target, value in zip(targets, values):
                target[...] = value
            return c

        lax.fori_loop(0, live * (GATHER_GROUP // ACC_GROUP), accumulate, 0)


    def pair_body(j, carry):
        process(row0 + j * (2 * EXP_TM), 2 * EXP_TM)
        return carry

    lax.fori_loop(0, n_tiles // 2, pair_body, 0)

    @pl.when(n_tiles % 2 == 1)
    def _():
        process(row0 + (n_tiles - 1) * EXP_TM, EXP_TM)

    def rows_of(first, blk):
        return pl.ds(pl.multiple_of(first + blk * RES_TB, RES_TB), RES_TB)

    def load_x(blk, slot):
        return pltpu.make_async_copy(x_hbm.at[rows_of(half * HALF_TOK, blk), :], xin.at[slot], in_sem.at[slot])

    def residual(blk, slot):
        cond = jnp.where(half == 0, 0, 1 + blk // (DEC_SEQ // RES_TB))
        gate = mod_ref[cond, 5:6, :]
        y = _tiles_to_rows(acc.at[pl.ds(pl.multiple_of(blk * (RES_TB * 8), RES_TB * 8), RES_TB * 8), :], RES_TB)
        x_new = xin[slot] + gate * y
        xout[slot] = _rms(x_new, fg_ref[...]) if final else x_new

    def epilogue(dst, first_row):
        def store_x(blk, slot):
            return pltpu.make_async_copy(xout.at[slot], dst.at[rows_of(first_row, blk), :], out_sem.at[slot])

        n_pairs = HALF_TOK // RES_TB // 2
        load_x(0, 0).start()

        def pair(p, carry):
            for slot in range(2):
                blk = 2 * p + slot
                if slot == 0:
                    load_x(blk + 1, 1).start()
                else:
                    @pl.when(p + 1 < n_pairs)
                    def _():
                        load_x(blk + 1, 0).start()
                load_x(blk, slot).wait()

                @pl.when(p > 0)
                def _():
                    store_x(blk - 2, slot).wait()

                residual(blk, slot)
                store_x(blk, slot).start()
            return carry

        lax.fori_loop(0, n_pairs, pair, 0)
        store_x(2 * n_pairs - 2, 0).wait()
        store_x(2 * n_pairs - 1, 1).wait()

    if final:
        for which in range(2):
            @pl.when((expert == N_EXP - 1) & (half == which))
            def _():
                epilogue(dst_hbm[which], 0)
    else:
        @pl.when(expert == N_EXP - 1)
        def _():
            epilogue(dst_hbm[0], half * HALF_TOK)


def _experts(tile0, n_tiles, counts, src, gs, h, x_all, mod_l, wg, wu, wd, layer, final_g=None):
    final = final_g is not None
    wmap = lambda g, t0, nt, cnt, src: (layer, g % N_EXP, 0, 0)
    any_spec = pl.BlockSpec(memory_space=pl.ANY)
    extra_specs = [pl.BlockSpec((1, D), lambda g, t0, nt, cnt, src: (0, 0))] if final else []
    extra_args = (final_g,) if final else ()
    return pl.pallas_call(
        functools.partial(_experts_kernel, final=final),
        grid_spec=pltpu.PrefetchScalarGridSpec(
            num_scalar_prefetch=4,
            grid=(N_GROUPS,),
            in_specs=[
                pl.BlockSpec((GATE_BLK,), lambda g, t0, nt, cnt, src: (g // N_EXP,), memory_space=pltpu.SMEM),
                pl.BlockSpec(memory_space=pl.ANY),
                pl.BlockSpec(memory_space=pl.ANY),
                pl.BlockSpec((8, 6, D), lambda g, t0, nt, cnt, src: (0, 0, 0)),
                pl.BlockSpec((None, None, D, D_EXP), wmap),
                pl.BlockSpec((None, None, D, D_EXP), wmap),
                pl.BlockSpec((None, None, D_EXP, D), wmap),
            ] + extra_specs,
            out_specs=[any_spec, any_spec] if final else any_spec,
            scratch_shapes=[
                pltpu.VMEM((ACC_TOK * 8, 128), F32),
                pltpu.VMEM((ACC_TOK * 8, 128), F32),
                pltpu.VMEM((2 * EXP_TM * 8, 128), F32),
                pltpu.VMEM((2 * EXP_TM * 8, 128), F32),
                pltpu.VMEM((D, D_EXP), BF16),
                pltpu.VMEM((D, D_EXP), BF16),
                pltpu.VMEM((D_EXP, D), BF16),
                pltpu.VMEM((2, RES_TB, D), F32),
                pltpu.VMEM((2, RES_TB, D), F32),
                pltpu.SemaphoreType.DMA,
                pltpu.SemaphoreType.DMA((2,)),
                pltpu.SemaphoreType.DMA((2,)),
            ],
        ),
        out_shape=([jax.ShapeDtypeStruct((HALF_TOK, D), F32)] * 2 if final
                   else jax.ShapeDtypeStruct((N_TOK, D), F32)),
        input_output_aliases={} if final else {6: 0},
        compiler_params=_cp("arbitrary"),
        name="moe_experts_final" if final else "moe_experts",
    )(tile0, n_tiles, counts, src, gs, h, x_all, mod_l, wg, wu, wd, *extra_args)


def _moe(x_all, mod_l, n2g, wr, br, wg, wu, wd, layer, final_g=None):
    h, metat, cnt = _router(x_all, mod_l, n2g, wr, br)
    counts = cnt[:, 0, :N_EXP].astype(jnp.int32).reshape(N_GROUPS)
    padded = (counts + EXP_TM - 1) // EXP_TM * EXP_TM
    ends = jnp.cumsum(padded)
    offs = ends - padded
    rec = metat.astype(jnp.int32)
    half = (jnp.arange(N_TOK, dtype=jnp.int32) // HALF_TOK)[None, :]
    group = rec[M_E1:M_E2 + 1] + N_EXP * half
    is_group = group[None] == jnp.arange(N_GROUPS, dtype=jnp.int32)[:, None, None]
    pos = jnp.sum(jnp.where(is_group, offs[:, None, None], 0), axis=0) + rec[M_R1:M_R2 + 1]
    live_end = offs + (counts + GATHER_GROUP - 1) // GATHER_GROUP * GATHER_GROUP
    src = _order(pos.reshape(N_ASSIGN), offs + counts, live_end)
    g12 = metat[M_G1:M_G2 + 1].reshape(2, 2, HALF_TOK)
    gates = jnp.concatenate([g12[0], jnp.zeros((2, CODE_PLANE - HALF_TOK), F32), g12[1]], axis=1)
    return _experts(offs // EXP_TM, padded // EXP_TM, counts, src, gates.reshape(2 * GATE_BLK), h, x_all, mod_l,
                    wg, wu, wd, layer, final_g)


def _rope_tables():
    pos = jnp.arange(DEC_SEQ)
    row = (pos // GRID_W).astype(F32)
    col = (pos % GRID_W).astype(F32)
    n_freq = HD // 4
    inv = ROPE_THETA ** (-jnp.arange(n_freq, dtype=F32) / n_freq)
    ang = jnp.concatenate([row[:, None] * inv, col[:, None] * inv], axis=-1)
    cos = jnp.repeat(jnp.cos(ang), 2, axis=-1)
    sin = jnp.repeat(jnp.sin(ang), 2, axis=-1) * jnp.tile(jnp.array([-1.0, 1.0], F32), HD // 2)
    nblk = DEC_SEQ // QKV_TB
    cos_tab = jnp.concatenate([jnp.ones((1, QKV_TB, HD), F32), cos.reshape(nblk, QKV_TB, HD)], axis=0)
    sin_tab = jnp.concatenate([jnp.zeros((1, QKV_TB, HD), F32), sin.reshape(nblk, QKV_TB, HD)], axis=0)
    return cos_tab, sin_tab


def kernel(x_prompt, x_sample, state_gla, cache_k, cache_v, c, c_ctx, w_mod, b_mod, norm1_g, norm2_g,
           w_in_even, w_gate_up, b_gate_up, gla_norm_g, w_spatial, b_spatial, w_out_even,
           w_in_odd, q_norm_g, k_norm_g, w_out_odd, w_router_group, b_router_group,
           w_router_expert, b_router_expert, w_exp_gate, w_exp_up, w_exp_down, final_norm_g):
    x_all = None
    cond8 = jnp.concatenate([c_ctx[None], c, jnp.zeros((3, D), F32)], axis=0)
    mod = _modulation(cond8, w_mod, b_mod)
    cos_tab, sin_tab = _rope_tables()
    zero_state = jnp.zeros((1, 2, QK_W, GLA_DV), F32)
    state_in = state_gla.reshape(DEC_BATCH, -1, 2, QK_W, GLA_DV)
    cache_k2 = cache_k.reshape(DEC_BATCH, -1, SEQ, KV_W)
    cache_v2 = cache_v.reshape(DEC_BATCH, -1, SEQ, KV_W)

    gla_states, caches = [], None
    for l in range(DEPTH):
        i = l // 2
        n1g = norm1_g[l][None]
        if l % 2 == 0:
            w = w_in_even[i]
            win = jnp.concatenate([w[:, :1536], w[:, 1568:], w[:, 1536:1568], jnp.zeros((D, 96), F32)],
                                  axis=1).astype(BF16)
            wgu = jnp.zeros((128, 2 * QK_W), F32)
            wgu = wgu.at[0:GLA_RANK, 0:QK_W].set(w_gate_up[i, 0])
            wgu = wgu.at[GLA_RANK:2 * GLA_RANK, QK_W:].set(w_gate_up[i, 1]).astype(BF16)
            bgu = b_gate_up[i].reshape(1, 2 * QK_W)
            args = (mod[l], n1g, win, wgu, bgu, gla_norm_g[i][None], w_spatial[i].astype(BF16),
                    b_spatial[i].T, w_out_even[i].astype(BF16))
            first = l == 0
            x_all, st = _even_mixer(x_all, *args, zero_state, latent=False,
                                    x_first=x_prompt.reshape(N_CTX, D) if first else None)
            gla_states.append(st)
            x_all, _ = _even_mixer(x_all, *args, state_in[:, i], latent=True,
                                   x_first=x_sample.reshape(N_LAT, D) if first else None)
        else:
            q, k, v, *caches = _qkv(x_all, mod[l], n1g, w_in_odd[i].astype(BF16), q_norm_g[i][None],
                                    k_norm_g[i][None], cos_tab, sin_tab, i, caches)
            wout = w_out_odd[i].astype(BF16)
            x_all = _attention(x_all, mod[l], q, k, v, wout)
            x_all = _attention(x_all, mod[l], q, k, v, wout, cache_k2, cache_v2, layer_i=i)
        wr = jnp.concatenate([w_router_expert[l], w_router_group[l],
                              jnp.zeros((D, 128 - N_EXP - MOE_GROUPS), F32)], axis=1)
        br = jnp.concatenate([b_router_expert[l], b_router_group[l],
                              jnp.zeros((128 - N_EXP - MOE_GROUPS,), F32)])[None]
        x_all = _moe(x_all, mod[l], norm2_g[l][None], wr, br, w_exp_gate, w_exp_up, w_exp_down, l,
                     final_norm_g[None] if l == DEPTH - 1 else None)

    y_prompt = x_all[0].reshape(BATCH, SEQ, D)
    y_sample = x_all[1].reshape(DEC_BATCH, DEC_SEQ, D)
    new_state = jnp.stack(gla_states, axis=1).reshape(BATCH, -1, 2, GLA_HEADS, GLA_DK, GLA_DV)
    new_k, new_v = (a.reshape(BATCH, DEPTH // 2, SEQ, ATT_KV, HD) for a in caches)
    return (y_prompt, y_sample, new_state, new_k, new_v)
```

```python
import functools

import jax
import jax.numpy as jnp
import numpy as np
from jax import lax
from jax.experimental import pallas as pl
from jax.experimental.pallas import tpu as pltpu

F32 = jnp.float32
BF16 = jnp.bfloat16

D = 1024
BATCH, SEQ = 16, 256
DEC_BATCH, DEC_SEQ = 4, 1024
N_CTX = BATCH * SEQ
N_LAT = DEC_BATCH * DEC_SEQ
N_TOK = N_CTX + N_LAT
DEPTH = 4
EPS = 1e-6
GRID_W = 64
ROPE_THETA = 10000.0

GLA_HEADS, GLA_DK, GLA_DV, GLA_RANK, GLA_CHUNK, GLA_TAU = 4, 64, 128, 16, 128, 16.0
QK_W = GLA_HEADS * GLA_DK
V_W = GLA_HEADS * GLA_DV
GMLP_GROUPS, GMLP_DIM, GMLP_CHUNK = 4, 128, 128
GMLP_W = GMLP_GROUPS * GMLP_DIM
C_Q, C_K, C_V, C_G, C_U, C_VG, C_A = 0, 256, 512, 1024, 1536, 2048, 2560
EVEN_PACK = 2688

ATT_HEADS, ATT_KV, HD = 8, 2, 128
ATT_G = ATT_HEADS // ATT_KV
Q_W = ATT_HEADS * HD
KV_W = ATT_KV * HD

MOE_GROUPS, MOE_PER_GROUP = 4, 8
N_EXP = MOE_GROUPS * MOE_PER_GROUP
D_EXP = D // 4
NEG = -1e30

MIB = 1024 * 1024
V7X_VMEM_MIB = 64
LANES, SUBLANES = 128, 8


def _cp(*sem, vmem_mib=32):
    assert vmem_mib < V7X_VMEM_MIB
    return pltpu.CompilerParams(dimension_semantics=sem, vmem_limit_bytes=vmem_mib * MIB)


def _dot(a, b):
    return jnp.dot(a.astype(BF16), b.astype(BF16), preferred_element_type=F32)


def _dot_nt(a, b):
    return lax.dot_general(a.astype(BF16), b.astype(BF16), (((1,), (1,)), ((), ())),
                           preferred_element_type=F32)


def _dot_tn(a, b):
    return lax.dot_general(a.astype(BF16), b.astype(BF16), (((0,), (0,)), ((), ())),
                           preferred_element_type=F32)


def _rms(x, g):
    return x * lax.rsqrt(jnp.mean(x * x, axis=-1, keepdims=True) + EPS) * g


def _silu(x):
    return x * jax.nn.sigmoid(x)


def _gelu(x):
    return 0.5 * x * (1.0 + jnp.tanh(np.sqrt(2.0 / np.pi).astype(np.float32) * (x + 0.044715 * (x * x * x))))


def _log_sigmoid(z):
    return jnp.minimum(z, 0.0) - jnp.log(1.0 + jnp.exp(-jnp.abs(z)))


assert D == LANES * SUBLANES


def _rows_to_tiles(ref, x):
    rows = x.shape[0]
    for j in range(SUBLANES):
        ref[pl.ds(j, rows, stride=SUBLANES), :] = x[:, j * LANES:(j + 1) * LANES]


def _tiles_to_rows(ref, rows):
    return jnp.concatenate([ref[pl.ds(j, rows, stride=SUBLANES), :] for j in range(SUBLANES)], axis=1)


def _tile_of(ref, row8):
    return ref.at[pl.ds(pl.multiple_of(row8, SUBLANES), SUBLANES), :]


def _resident(shape, index_map):
    return pl.BlockSpec(shape, index_map, pipeline_mode=pl.Buffered(1))


def _split_bf16(x):
    hi = x.astype(BF16)
    return hi, (x - hi.astype(F32)).astype(BF16)


def _mod_kernel(cond_ref, w_ref, b_ref, o_ref):
    s_hi, s_lo = _split_bf16(_silu(cond_ref[...]))
    w_hi, w_lo = _split_bf16(w_ref[...])
    dot = functools.partial(jnp.dot, preferred_element_type=F32)
    both = dot(jnp.concatenate([s_hi, s_lo], axis=0), w_hi)
    o_ref[...] = both[0:8] + both[8:16] + dot(s_hi, w_lo) + b_ref[...]


def _modulation(cond8, w_mod, b_mod):
    tn = 2048
    out = pl.pallas_call(
        _mod_kernel,
        grid=(DEPTH, 6 * D // tn),
        in_specs=[
            pl.BlockSpec((8, D), lambda l, j: (0, 0)),
            pl.BlockSpec((None, D, tn), lambda l, j: (l, 0, j)),
            pl.BlockSpec((None, 1, tn), lambda l, j: (l, 0, j)),
        ],
        out_specs=pl.BlockSpec((None, 8, tn), lambda l, j: (l, 0, j)),
        out_shape=jax.ShapeDtypeStruct((DEPTH, 8, 6 * D), F32),
        compiler_params=_cp("arbitrary", "arbitrary"),
        name="adaln_mod",
    )(cond8, w_mod, b_mod.reshape(DEPTH, 1, 6 * D))
    return out.reshape(DEPTH, 8, 6, D)


def _even_kernel(x_ref, mod_ref, n1g_ref, win_ref, wgu_ref, bgu_ref, glag_ref, ws_ref, bs_ref,
                 wout_ref, s0_ref, xo_ref, st_ref, proj, la, o_f, o_b, st_scr, *, T):
    n_chunks = T // GLA_CHUNK
    shift, scale, gate = mod_ref[0:1, :], mod_ref[1:2, :], mod_ref[2:3, :]
    RB = 128
    PROJ_RB = 256

    def proj_body(r, carry):
        r0 = pl.multiple_of(r * PROJ_RB, PROJ_RB)
        h = _rms(x_ref[pl.ds(r0, PROJ_RB), :], n1g_ref[...]) * (1.0 + scale) + shift
        p = _dot(h, win_ref[...])
        proj[pl.ds(r0, PROJ_RB), :] = p
        z = _dot(p[:, C_A:C_A + 128], wgu_ref[...]) + bgu_ref[...]
        la[pl.ds(r0, PROJ_RB), :] = _log_sigmoid(z) * (1.0 / GLA_TAU)
        return carry

    lax.fori_loop(0, T // PROJ_RB, proj_body, 0)

    st_scr[0] = s0_ref[0].T
    st_scr[1] = s0_ref[1].T

    ci = lax.broadcasted_iota(jnp.int32, (GLA_CHUNK, GLA_CHUNK), 0)
    cj = lax.broadcasted_iota(jnp.int32, (GLA_CHUNK, GLA_CHUNK), 1)
    tri = (jnp.where(ci >= cj, 1.0, 0.0).astype(BF16), jnp.where(ci <= cj, 1.0, 0.0).astype(BF16))
    ai = lax.broadcasted_iota(jnp.int32, (GLA_HEADS * GLA_CHUNK, GLA_CHUNK), 0) % GLA_CHUNK
    aj = lax.broadcasted_iota(jnp.int32, (GLA_HEADS * GLA_CHUNK, GLA_CHUNK), 1)
    amask = (ai >= aj, ai <= aj)
    lane_head = lax.broadcasted_iota(jnp.int32, (1, QK_W), 1) // GLA_DK
    hmask = [jnp.where(lane_head == h, 1.0, 0.0) for h in range(GLA_HEADS)]

    def chunk_body(i, carry):
        for d in range(2):
            c = i if d == 0 else n_chunks - 1 - i
            r0 = pl.multiple_of(c * GLA_CHUNK, GLA_CHUNK)
            q = proj[pl.ds(r0, GLA_CHUNK), C_Q:C_Q + QK_W] * (GLA_DK ** -0.5)
            k = proj[pl.ds(r0, GLA_CHUNK), C_K:C_K + QK_W]
            v = proj[pl.ds(r0, GLA_CHUNK), C_V:C_V + V_W]
            lac = la[pl.ds(r0, GLA_CHUNK), d * QK_W:(d + 1) * QK_W]
            hi = lac.astype(BF16)
            lo = (lac - hi.astype(F32)).astype(BF16)
            b = (jnp.dot(tri[d], hi, preferred_element_type=F32)
                 + jnp.dot(tri[d], lo, preferred_element_type=F32))
            bend = b[GLA_CHUNK - 1:GLA_CHUNK, :] if d == 0 else b[0:1, :]
            qe = q * jnp.exp(b)
            ke = k * jnp.exp(-b)
            kd = k * jnp.exp(bend - b)
            st = st_scr[d]
            qstack = jnp.concatenate([qe * hmask[h] for h in range(GLA_HEADS)], axis=0).astype(BF16)
            att = jnp.where(amask[d], _dot_nt(qstack, ke), 0.0)
            inter = _dot_nt(qstack, st)
            outs = []
            for h in range(GLA_HEADS):
                rows = slice(h * GLA_CHUNK, (h + 1) * GLA_CHUNK)
                outs.append(_dot(att[rows], v[:, h * GLA_DV:(h + 1) * GLA_DV]) + inter[rows])
            o = jnp.concatenate(outs, axis=1)
            if d == 0:
                o_f[pl.ds(r0, GLA_CHUNK), :] = o
            else:
                o_b[pl.ds(r0, GLA_CHUNK), :] = o
            vstack = jnp.concatenate([v[:, h * GLA_DV:(h + 1) * GLA_DV] for h in range(GLA_HEADS)], axis=0)
            kstack = jnp.concatenate([kd * hmask[h] for h in range(GLA_HEADS)], axis=0)
            st_scr[d] = st * jnp.exp(bend) + _dot_tn(vstack, kstack)
        return carry

    lax.fori_loop(0, n_chunks, chunk_body, 0, unroll=2)
    st_ref[0] = st_scr[0].T
    st_ref[1] = st_scr[1].T

    def out_body(r, carry):
        r0 = pl.multiple_of(r * RB, RB)
        osum = o_f[pl.ds(r0, RB), :] + o_b[pl.ds(r0, RB), :]
        g = proj[pl.ds(r0, RB), C_G:C_G + V_W]
        u = proj[pl.ds(r0, RB), C_U:C_U + GMLP_W]
        vg = _gelu(proj[pl.ds(r0, RB), C_VG:C_VG + GMLP_W])
        parts = []
        for h in range(GLA_HEADS):
            oh = osum[:, h * GLA_DV:(h + 1) * GLA_DV]
            parts.append(_rms(oh, glag_ref[...]) * _silu(g[:, h * GLA_DV:(h + 1) * GLA_DV]))
        for gi in range(GMLP_GROUPS):
            vc = vg[:, gi * GMLP_DIM:(gi + 1) * GMLP_DIM]
            vc = vc - jnp.mean(vc, axis=-1, keepdims=True)
            vn = vc * lax.rsqrt(jnp.mean(vc * vc, axis=-1, keepdims=True) + EPS)
            sg = _dot(ws_ref[gi], vn) + bs_ref[:, gi:gi + 1]
            parts.append(_gelu(u[:, gi * GMLP_DIM:(gi + 1) * GMLP_DIM]) * sg)
        mix = jnp.concatenate(parts, axis=1)
        y = _dot(mix, wout_ref[...])
        xo_ref[pl.ds(r0, RB), :] = x_ref[pl.ds(r0, RB), :] + gate * y
        return carry

    lax.fori_loop(0, T // RB, out_body, 0)


def _even_mixer(x_all, mod_l, n1g, win, wgu, bgu, glag, ws, bs, wout, s0, *, latent, x_first=None):
    if latent:
        T, nseq, blk0 = DEC_SEQ, DEC_BATCH, N_CTX // DEC_SEQ
        cond = lambda i: 1 + i
        s0_spec = pl.BlockSpec((None, 2, QK_W, GLA_DV), lambda i: (i, 0, 0, 0))
    else:
        T, nseq, blk0 = SEQ, BATCH, 0
        cond = lambda i: 0
        s0_spec = pl.BlockSpec((None, 2, QK_W, GLA_DV), lambda i: (0, 0, 0, 0))
    const2 = lambda i: (0, 0)
    body = functools.partial(_even_kernel, T=T)
    x_spec = pl.BlockSpec((T, D), lambda i: (blk0 + i, 0))
    if x_first is None:
        lead_specs, lead_args, aliases = [x_spec], (x_all,), {0: 0}
    elif x_all is None:
        lead_specs, lead_args, aliases = [pl.BlockSpec((T, D), lambda i: (i, 0))], (x_first,), {}
    else:
        lead_specs = [pl.BlockSpec(memory_space=pl.ANY), pl.BlockSpec((T, D), lambda i: (i, 0))]
        lead_args, aliases = (x_all, x_first), {0: 0}
        body = lambda dst_ref, *refs: _even_kernel(*refs, T=T)
    x_new, states = pl.pallas_call(
        body,
        grid=(nseq,),
        in_specs=lead_specs + [
            pl.BlockSpec((None, 6, D), lambda i: (cond(i), 0, 0)),
            _resident((1, D), const2),
            _resident((D, EVEN_PACK), const2),
            _resident((128, 2 * QK_W), const2),
            _resident((1, 2 * QK_W), const2),
            _resident((1, GLA_DV), const2),
            _resident((GMLP_GROUPS, GMLP_CHUNK, GMLP_CHUNK), lambda i: (0, 0, 0)),
            _resident((GMLP_CHUNK, GMLP_GROUPS), const2),
            _resident((D, D), const2),
            s0_spec,
        ],
        out_specs=[
            x_spec,
            pl.BlockSpec((None, 2, QK_W, GLA_DV), lambda i: (i, 0, 0, 0)),
        ],
        out_shape=[
            jax.ShapeDtypeStruct((N_TOK, D), F32),
            jax.ShapeDtypeStruct((nseq, 2, QK_W, GLA_DV), F32),
        ],
        scratch_shapes=[
            pltpu.VMEM((T, EVEN_PACK), F32),
            pltpu.VMEM((T, 2 * QK_W), F32),
            pltpu.VMEM((T, V_W), F32),
            pltpu.VMEM((T, V_W), F32),
            pltpu.VMEM((2, GLA_DV, QK_W), F32),
        ],
        input_output_aliases=aliases,
        compiler_params=_cp("arbitrary", vmem_mib=48 if latent else 32),
        name="even_mixer_latent" if latent else "even_mixer_context",
    )(*lead_args, mod_l, n1g, win, wgu, bgu, glag, ws, bs, wout, s0)
    return x_new, states


QKV_TB = 512


def _qkv_kernel(x_ref, mod_ref, n1g_ref, win_ref, gq_ref, gk_ref, cos_ref, sin_ref, q_ref, k_ref, v_ref,
                ck_ref, cv_ref):
    shift, scale = mod_ref[0:1, :], mod_ref[1:2, :]
    h = _rms(x_ref[...], n1g_ref[...]) * (1.0 + scale) + shift
    p = _dot(h, win_ref[...])
    cos, sin = cos_ref[...], sin_ref[...]
    even_lane = lax.broadcasted_iota(jnp.int32, (1, HD), 1) % 2 == 0

    def rope(xn):
        swapped = jnp.where(even_lane, pltpu.roll(xn, HD - 1, axis=1), pltpu.roll(xn, 1, axis=1))
        return xn * cos + swapped * sin

    def emit(rotate, to_cache):
        for hh in range(ATT_HEADS):
            qn = _rms(p[:, hh * HD:(hh + 1) * HD], gq_ref[...])
            q_ref[:, hh * HD:(hh + 1) * HD] = (rotate(qn) * (HD ** -0.5)).astype(BF16)
        for hh in range(ATT_KV):
            kn = rotate(_rms(p[:, Q_W + hh * HD:Q_W + (hh + 1) * HD], gk_ref[...]))
            k_ref[:, hh * HD:(hh + 1) * HD] = kn
            if to_cache:
                for s in range(QKV_TB // SEQ):
                    ck_ref[s, :, hh * HD:(hh + 1) * HD] = kn[s * SEQ:(s + 1) * SEQ]
        if to_cache:
            for s in range(QKV_TB // SEQ):
                cv_ref[s] = p[s * SEQ:(s + 1) * SEQ, Q_W + KV_W:]

    is_latent = pl.program_id(0) >= N_CTX // QKV_TB

    @pl.when(is_latent)
    def _():
        emit(rope, False)

    @pl.when(jnp.logical_not(is_latent))
    def _():
        emit(lambda xn: xn, True)

    v_ref[...] = p[:, Q_W + KV_W:]


def _qkv(x_all, mod_l, n1g, win, gq, gk, cos_tab, sin_tab, layer_i, caches=None):
    nb_ctx = N_CTX // QKV_TB
    per_seq = DEC_SEQ // QKV_TB
    cond = lambda i: jnp.where(i < nb_ctx, 0, 1 + (i - nb_ctx) // per_seq)
    tab = lambda i: jnp.where(i < nb_ctx, 0, 1 + (i - nb_ctx) % per_seq)
    const2 = lambda i: (0, 0)
    cache_spec = pl.BlockSpec((QKV_TB // SEQ, None, SEQ, KV_W), lambda i: (jnp.minimum(i, nb_ctx - 1), layer_i, 0, 0))
    cache_shape = jax.ShapeDtypeStruct((BATCH, DEPTH // 2, SEQ, KV_W), F32)
    if caches is None:
        body, lead_specs, lead_args, aliases = _qkv_kernel, [], (), {}
    else:
        body = lambda ck_in, cv_in, *refs: _qkv_kernel(*refs)
        lead_specs = [pl.BlockSpec(memory_space=pl.ANY)] * 2
        lead_args, aliases = tuple(caches), {0: 3, 1: 4}
    return pl.pallas_call(
        body,
        grid=(N_TOK // QKV_TB,),
        in_specs=lead_specs + [
            pl.BlockSpec((QKV_TB, D), lambda i: (i, 0)),
            pl.BlockSpec((None, 6, D), lambda i: (cond(i), 0, 0)),
            _resident((1, D), const2),
            _resident((D, Q_W + 2 * KV_W), const2),
            _resident((1, HD), const2),
            _resident((1, HD), const2),
            pl.BlockSpec((None, QKV_TB, HD), lambda i: (tab(i), 0, 0)),
            pl.BlockSpec((None, QKV_TB, HD), lambda i: (tab(i), 0, 0)),
        ],
        out_specs=[
            pl.BlockSpec((QKV_TB, Q_W), lambda i: (i, 0)),
            pl.BlockSpec((QKV_TB, KV_W), lambda i: (i, 0)),
            pl.BlockSpec((QKV_TB, KV_W), lambda i: (i, 0)),
            cache_spec,
            cache_spec,
        ],
        out_shape=[
            jax.ShapeDtypeStruct((N_TOK, Q_W), BF16),
            jax.ShapeDtypeStruct((N_TOK, KV_W), F32),
            jax.ShapeDtypeStruct((N_TOK, KV_W), F32),
            cache_shape,
            cache_shape,
        ],
        input_output_aliases=aliases,
        compiler_params=_cp("arbitrary"),
        name="odd_qkv",
    )(*lead_args, x_all, mod_l, n1g, win, gq, gk, cos_tab, sin_tab)


ATT_TQ_LATENT = 512


def _attn_kernel(*refs, n_kv):
    q_ref = refs[0]
    kv_refs = refs[1:1 + 2 * n_kv]
    x_ref, mod_ref, wout_ref, xo_ref, att_scr = refs[1 + 2 * n_kv:]
    gate = mod_ref[2:3, :]
    for kh in range(ATT_KV):
        ks = [kv_refs[2 * s][:, kh * HD:(kh + 1) * HD].astype(BF16) for s in range(n_kv)]
        vs = [jnp.concatenate([kv_refs[2 * s + 1][:, kh * HD:(kh + 1) * HD].astype(BF16),
                               jnp.ones((kv_refs[2 * s + 1].shape[0], HD), BF16)], axis=1) for s in range(n_kv)]
        for g in range(ATT_G):
            hh = kh * ATT_G + g
            qh = q_ref[:, hh * HD:(hh + 1) * HD]
            ss = [_dot_nt(qh, kk) for kk in ks]
            m = ss[0].max(axis=-1, keepdims=True)
            for s in ss[1:]:
                m = jnp.maximum(m, s.max(axis=-1, keepdims=True))
            o = _dot(jnp.exp(ss[0] - m), vs[0])
            for s, vv in zip(ss[1:], vs[1:]):
                o = o + _dot(jnp.exp(s - m), vv)
            att_scr[:, hh * HD:(hh + 1) * HD] = o[:, :HD] / o[:, HD:HD + 1]
    y = _dot(att_scr[...], wout_ref[...])
    xo_ref[...] = x_ref[...] + gate * y


def _attention(x_all, mod_l, q, k, v, wout, cache_k=None, cache_v=None, layer_i=0):
    latent = cache_k is not None
    const2 = lambda *a: (0, 0)
    tq = ATT_TQ_LATENT if latent else SEQ
    if latent:
        nq = DEC_SEQ // tq
        row_blk = lambda b, j: (N_CTX // tq + b * nq + j, 0)
        grid = (DEC_BATCH, nq)
        kv_specs = [
            pl.BlockSpec((None, None, SEQ, KV_W), lambda b, j: (b, layer_i, 0, 0)),
            pl.BlockSpec((None, None, SEQ, KV_W), lambda b, j: (b, layer_i, 0, 0)),
            pl.BlockSpec((DEC_SEQ, KV_W), lambda b, j: (N_CTX // DEC_SEQ + b, 0)),
            pl.BlockSpec((DEC_SEQ, KV_W), lambda b, j: (N_CTX // DEC_SEQ + b, 0)),
        ]
        kv_args = (cache_k, cache_v, k, v)
        mod_spec = pl.BlockSpec((None, 6, D), lambda b, j: (1 + b, 0, 0))
        sem = ("arbitrary", "arbitrary")
        n_kv = 2
    else:
        row_blk = lambda i: (i, 0)
        grid = (BATCH,)
        kv_specs = [pl.BlockSpec((SEQ, KV_W), row_blk), pl.BlockSpec((SEQ, KV_W), row_blk)]
        kv_args = (k, v)
        mod_spec = pl.BlockSpec((None, 6, D), lambda i: (0, 0, 0))
        sem = ("arbitrary",)
        n_kv = 1
    n_in = 1 + len(kv_args)
    return pl.pallas_call(
        functools.partial(_attn_kernel, n_kv=n_kv),
        grid=grid,
        in_specs=[pl.BlockSpec((tq, Q_W), row_blk)] + kv_specs + [
            pl.BlockSpec((tq, D), row_blk),
            mod_spec,
            _resident((D, D), const2),
        ],
        out_specs=pl.BlockSpec((tq, D), row_blk),
        out_shape=jax.ShapeDtypeStruct((N_TOK, D), F32),
        scratch_shapes=[pltpu.VMEM((tq, Q_W), F32)],
        input_output_aliases={n_in: 0},
        compiler_params=_cp(*sem),
        name="attention_latent" if latent else "attention_context",
    )(q, *kv_args, x_all, mod_l, wout)


ROUTE_TB = 512
HALF_TOK = N_TOK // 2
M_E1, M_E2, M_G1, M_G2, M_R1, M_R2 = 0, 1, 2, 3, 4, 5


def _router_kernel(x_ref, mod_ref, n2g_ref, w2_ref, br_ref, h_ref, metat_ref, cnt_ref, run):
    @pl.when(pl.program_id(0) % (HALF_TOK // ROUTE_TB) == 0)
    def _():
        run[...] = jnp.zeros_like(run)

    shift, scale = mod_ref[3:4, :], mod_ref[4:5, :]
    h = _rms(x_ref[...], n2g_ref[...]) * (1.0 + scale) + shift
    _rows_to_tiles(h_ref, h)
    h_hi, h_lo = _split_bf16(h)
    dot = functools.partial(jnp.dot, preferred_element_type=F32)
    wide = dot(h_hi, w2_ref[...])
    logits = wide[:, :128] + wide[:, 128:] + dot(h_lo, w2_ref[:, :128]) + br_ref[...]
    lane = lax.broadcasted_iota(jnp.int32, logits.shape, 1).astype(F32)
    big = 1e4

    def first_argmax(vals):
        m = vals.max(axis=-1, keepdims=True)
        return m, jnp.where(vals == m, lane, big).min(axis=-1, keepdims=True)

    gl = jnp.where((lane >= N_EXP) & (lane < N_EXP + MOE_GROUPS), logits, NEG)
    gmax, glane = first_argmax(gl)
    g_p = 1.0 / jnp.exp(gl - gmax).sum(axis=-1, keepdims=True)
    lo = (glane - N_EXP) * MOE_PER_GROUP
    el = jnp.where((lane >= lo) & (lane < lo + MOE_PER_GROUP), logits, NEG)
    m1, i1 = first_argmax(el)
    m2, i2 = first_argmax(jnp.where(lane == i1, NEG, el))
    t = jnp.exp(m2 - m1)
    w1 = 1.0 / (1.0 + t)
    sel1, sel2 = lane == i1, lane == i2
    onehot = jnp.where(sel1 | sel2, 1.0, 0.0)
    ri = lax.broadcasted_iota(jnp.int32, (ROUTE_TB, ROUTE_TB), 0)
    rj = lax.broadcasted_iota(jnp.int32, (ROUTE_TB, ROUTE_TB), 1)
    before = _dot(jnp.where(ri > rj, 1.0, 0.0), onehot) + run[...]
    r1 = jnp.where(sel1, before, 0.0).sum(axis=-1, keepdims=True)
    r2 = jnp.where(sel2, before, 0.0).sum(axis=-1, keepdims=True)
    run[...] += onehot.sum(axis=0, keepdims=True)
    cnt_ref[...] = run[...]
    meta = jnp.zeros_like(logits)
    for j, val in enumerate([i1, i2, w1 * g_p, (t * w1) * g_p, r1, r2]):
        meta = jnp.where(lane == j, val, meta)
    metat_ref[...] = meta.T[0:8, :]


def _router(x_all, mod_l, n2g, wr, br):
    w2 = jnp.concatenate(_split_bf16(wr), axis=1)
    nb_ctx = N_CTX // ROUTE_TB
    per_seq = DEC_SEQ // ROUTE_TB
    cond = lambda i: jnp.where(i < nb_ctx, 0, 1 + (i - nb_ctx) // per_seq)
    const2 = lambda i: (0, 0)
    return pl.pallas_call(
        _router_kernel,
        grid=(N_TOK // ROUTE_TB,),
        in_specs=[
            pl.BlockSpec((ROUTE_TB, D), lambda i: (i, 0)),
            pl.BlockSpec((None, 6, D), lambda i: (cond(i), 0, 0)),
            _resident((1, D), const2),
            _resident((D, 256), const2),
            _resident((1, 128), const2),
        ],
        out_specs=[
            pl.BlockSpec((ROUTE_TB * 8, 128), lambda i: (i, 0)),
            pl.BlockSpec((8, ROUTE_TB), lambda i: (0, i)),
            pl.BlockSpec((None, 1, 128), lambda i: (i // (HALF_TOK // ROUTE_TB), 0, 0)),
        ],
        out_shape=[
            jax.ShapeDtypeStruct((N_TOK * 8, 128), F32),
            jax.ShapeDtypeStruct((8, N_TOK), F32),
            jax.ShapeDtypeStruct((2, 1, 128), F32),
        ],
        scratch_shapes=[pltpu.VMEM((1, 128), F32)],
        compiler_params=_cp("arbitrary"),
        name="moe_router",
    )(x_all, mod_l, n2g, w2, br)


EXP_TM = 128
N_ASSIGN = 2 * N_TOK
N_GROUPS = 2 * N_EXP
MAX_TILES = N_ASSIGN // EXP_TM + N_GROUPS
N_SORTED = MAX_TILES * EXP_TM
ORDER_BLK = 2048
CODE_PLANE = 2 * HALF_TOK
CODE_MASK = 8 * CODE_PLANE - 1
DUMMY8 = HALF_TOK * 8


def _order_kernel(pos1_ref, pos2_ref, pad_lo_ref, pad_hi_ref, src_ref):
    i = pl.program_id(0)
    local = (i % (HALF_TOK // ORDER_BLK)) * ORDER_BLK

    def body(t, carry):
        src_ref[pos1_ref[t]] = (local + t) * 8
        src_ref[pos2_ref[t]] = (local + t + CODE_PLANE) * 8
        return carry

    lax.fori_loop(0, ORDER_BLK, body, 0, unroll=16)

    @pl.when(i == 0)
    def _():
        def group(g, carry):
            def pad(p, c):
                src_ref[p] = DUMMY8
                return c
            return lax.fori_loop(pad_lo_ref[g], pad_hi_ref[g], pad, carry)

        lax.fori_loop(0, N_GROUPS, group, 0)


def _order(pos, pad_lo, pad_hi):
    return pl.pallas_call(
        _order_kernel,
        grid=(N_TOK // ORDER_BLK,),
        in_specs=[
            pl.BlockSpec((ORDER_BLK,), lambda i: (i,), memory_space=pltpu.SMEM),
            pl.BlockSpec((ORDER_BLK,), lambda i: (N_TOK // ORDER_BLK + i,), memory_space=pltpu.SMEM),
            pl.BlockSpec(memory_space=pltpu.SMEM),
            pl.BlockSpec(memory_space=pltpu.SMEM),
        ],
        out_specs=pl.BlockSpec(memory_space=pltpu.SMEM),
        out_shape=jax.ShapeDtypeStruct((N_SORTED,), jnp.int32),
        compiler_params=_cp("arbitrary"),
        name="moe_order",
    )(pos, pos, pad_lo, pad_hi)


GATE_BLK = CODE_PLANE + HALF_TOK
ACC_TOK = HALF_TOK + 64
GATHER_GROUP, ACC_GROUP = 16, 8


RES_TB = 256


def _experts_kernel(tile0_ref, ntile_ref, count_ref, src_ref, gs_ref, h_hbm, x_hbm, mod_ref, wg_ref, wu_ref, wd_ref,
                    *rest, final):
    if final:
        fg_ref, *dst_hbm = rest[:3]
        rest = rest[3:]
    else:
        dst_hbm, rest = rest[:1], rest[1:]
    h_res, acc, xbuf, ybuf, wgb, wub, wdb, xin, xout, sem, in_sem, out_sem = rest
    group = pl.program_id(0)
    expert = group % N_EXP
    half = group // N_EXP
    rows0 = pl.multiple_of(half * (HALF_TOK * 8), 8)

    @pl.when(expert == 0)
    def _():
        cp = pltpu.make_async_copy(h_hbm.at[pl.ds(rows0, HALF_TOK * 8), :], h_res.at[pl.ds(0, HALF_TOK * 8), :], sem)
        cp.start()
        h_res[pl.ds(DUMMY8, 8), :] = jnp.zeros((8, 128), F32)
        xbuf[...] = jnp.zeros_like(xbuf)

        def zero(i, carry):
            acc[pl.ds(pl.multiple_of(i * 512, 512), 512), :] = jnp.zeros((512, 128), F32)
            return carry

        lax.fori_loop(0, ACC_TOK * 8 // 512, zero, 0)
        cp.wait()

    n_tiles = ntile_ref[group]

    @pl.when(n_tiles > 0)
    def _():
        wgb[...] = wg_ref[...].astype(BF16)
        wub[...] = wu_ref[...].astype(BF16)
        wdb[...] = wd_ref[...].astype(BF16)

    row0 = tile0_ref[group] * EXP_TM
    row_end = row0 + count_ref[group]

    def process(base, rows):
        live = (jnp.clip(row_end - base, 0, rows) + GATHER_GROUP - 1) // GATHER_GROUP

        def gather(g, c):
            for i in range(GATHER_GROUP):
                r = g * GATHER_GROUP + i
                xbuf[pl.ds(pl.multiple_of(r * 8, 8), 8), :] = _tile_of(h_res, src_ref[base + r] & CODE_MASK)[...]
            return c

        lax.fori_loop(0, live, gather, 0)
        x = _tiles_to_rows(xbuf, rows).astype(BF16)
        hid = _silu(_dot(x, wgb[...])) * _dot(x, wub[...])
        _rows_to_tiles(ybuf, _dot(hid, wdb[...]))

        def accumulate(g, c):
            targets, values = [], []
            for i in range(ACC_GROUP):
                r = g * ACC_GROUP + i
                code = src_ref[base + r]
                target = _tile_of(acc, code & CODE_MASK)
                targets.append(target)
                values.append(target[...] + gs_ref[code >> 3] * ybuf[pl.ds(pl.multiple_of(r * 8, 8), 8), :])
            for target, value in zip(targets, values):
                target[...] = value
            return c

        lax.fori_loop(0, live * (GATHER_GROUP // ACC_GROUP), accumulate, 0)


    def pair_body(j, carry):
        process(row0 + j * (2 * EXP_TM), 2 * EXP_TM)
        return carry

    lax.fori_loop(0, n_tiles // 2, pair_body, 0)

    @pl.when(n_tiles % 2 == 1)
    def _():
        process(row0 + (n_tiles - 1) * EXP_TM, EXP_TM)

    def rows_of(first, blk):
        return pl.ds(pl.multiple_of(first + blk * RES_TB, RES_TB), RES_TB)

    def load_x(blk, slot):
        return pltpu.make_async_copy(x_hbm.at[rows_of(half * HALF_TOK, blk), :], xin.at[slot], in_sem.at[slot])

    def residual(blk, slot):
        cond = jnp.where(half == 0, 0, 1 + blk // (DEC_SEQ // RES_TB))
        gate = mod_ref[cond, 5:6, :]
        y = _tiles_to_rows(acc.at[pl.ds(pl.multiple_of(blk * (RES_TB * 8), RES_TB * 8), RES_TB * 8), :], RES_TB)
        x_new = xin[slot] + gate * y
        xout[slot] = _rms(x_new, fg_ref[...]) if final else x_new

    def epilogue(dst, first_row):
        def store_x(blk, slot):
            return pltpu.make_async_copy(xout.at[slot], dst.at[rows_of(first_row, blk), :], out_sem.at[slot])

        n_pairs = HALF_TOK // RES_TB // 2
        load_x(0, 0).start()

        def pair(p, carry):
            for slot in range(2):
                blk = 2 * p + slot
                if slot == 0:
                    load_x(blk + 1, 1).start()
                else:
                    @pl.when(p + 1 < n_pairs)
                    def _():
                        load_x(blk + 1, 0).start()
                load_x(blk, slot).wait()

                @pl.when(p > 0)
                def _():
                    store_x(blk - 2, slot).wait()

                residual(blk, slot)
                store_x(blk, slot).start()
            return carry

        lax.fori_loop(0, n_pairs, pair, 0)
        store_x(2 * n_pairs - 2, 0).wait()
        store_x(2 * n_pairs - 1, 1).wait()

    if final:
        for which in range(2):
            @pl.when((expert == N_EXP - 1) & (half == which))
            def _():
                epilogue(dst_hbm[which], 0)
    else:
        @pl.when(expert == N_EXP - 1)
        def _():
            epilogue(dst_hbm[0], half * HALF_TOK)


def _experts(tile0, n_tiles, counts, src, gs, h, x_all, mod_l, wg, wu, wd, layer, final_g=None):
    final = final_g is not None
    wmap = lambda g, t0, nt, cnt, src: (layer, g % N_EXP, 0, 0)
    any_spec = pl.BlockSpec(memory_space=pl.ANY)
    extra_specs = [pl.BlockSpec((1, D), lambda g, t0, nt, cnt, src: (0, 0))] if final else []
    extra_args = (final_g,) if final else ()
    return pl.pallas_call(
        functools.partial(_experts_kernel, final=final),
        grid_spec=pltpu.PrefetchScalarGridSpec(
            num_scalar_prefetch=4,
            grid=(N_GROUPS,),
            in_specs=[
                pl.BlockSpec((GATE_BLK,), lambda g, t0, nt, cnt, src: (g // N_EXP,), memory_space=pltpu.SMEM),
                pl.BlockSpec(memory_space=pl.ANY),
                pl.BlockSpec(memory_space=pl.ANY),
                pl.BlockSpec((8, 6, D), lambda g, t0, nt, cnt, src: (0, 0, 0)),
                pl.BlockSpec((None, None, D, D_EXP), wmap),
                pl.BlockSpec((None, None, D, D_EXP), wmap),
                pl.BlockSpec((None, None, D_EXP, D), wmap),
            ] + extra_specs,
            out_specs=[any_spec, any_spec] if final else any_spec,
            scratch_shapes=[
                pltpu.VMEM((ACC_TOK * 8, 128), F32),
                pltpu.VMEM((ACC_TOK * 8, 128), F32),
                pltpu.VMEM((2 * EXP_TM * 8, 128), F32),
                pltpu.VMEM((2 * EXP_TM * 8, 128), F32),
                pltpu.VMEM((D, D_EXP), BF16),
                pltpu.VMEM((D, D_EXP), BF16),
                pltpu.VMEM((D_EXP, D), BF16),
                pltpu.VMEM((2, RES_TB, D), F32),
                pltpu.VMEM((2, RES_TB, D), F32),
                pltpu.SemaphoreType.DMA,
                pltpu.SemaphoreType.DMA((2,)),
                pltpu.SemaphoreType.DMA((2,)),
            ],
        ),
        out_shape=([jax.ShapeDtypeStruct((HALF_TOK, D), F32)] * 2 if final
                   else jax.ShapeDtypeStruct((N_TOK, D), F32)),
        input_output_aliases={} if final else {6: 0},
        compiler_params=_cp("arbitrary", vmem_mib=56),
        name="moe_experts_final" if final else "moe_experts",
    )(tile0, n_tiles, counts, src, gs, h, x_all, mod_l, wg, wu, wd, *extra_args)


def _moe(x_all, mod_l, n2g, wr, br, wg, wu, wd, layer, final_g=None):
    h, metat, cnt = _router(x_all, mod_l, n2g, wr, br)
    counts = cnt[:, 0, :N_EXP].astype(jnp.int32).reshape(N_GROUPS)
    padded = (counts + EXP_TM - 1) // EXP_TM * EXP_TM
    ends = jnp.cumsum(padded)
    offs = ends - padded
    rec = metat.astype(jnp.int32)
    half = (jnp.arange(N_TOK, dtype=jnp.int32) // HALF_TOK)[None, :]
    group = rec[M_E1:M_E2 + 1] + N_EXP * half
    is_group = group[None] == jnp.arange(N_GROUPS, dtype=jnp.int32)[:, None, None]
    pos = jnp.sum(jnp.where(is_group, offs[:, None, None], 0), axis=0) + rec[M_R1:M_R2 + 1]
    live_end = offs + (counts + GATHER_GROUP - 1) // GATHER_GROUP * GATHER_GROUP
    src = _order(pos.reshape(N_ASSIGN), offs + counts, live_end)
    g12 = metat[M_G1:M_G2 + 1].reshape(2, 2, HALF_TOK)
    gates = jnp.concatenate([g12[0], jnp.zeros((2, CODE_PLANE - HALF_TOK), F32), g12[1]], axis=1)
    return _experts(offs // EXP_TM, padded // EXP_TM, counts, src, gates.reshape(2 * GATE_BLK), h, x_all, mod_l,
                    wg, wu, wd, layer, final_g)


def _rope_tables():
    pos = jnp.arange(DEC_SEQ)
    row = (pos // GRID_W).astype(F32)
    col = (pos % GRID_W).astype(F32)
    n_freq = HD // 4
    inv = ROPE_THETA ** (-jnp.arange(n_freq, dtype=F32) / n_freq)
    ang = jnp.concatenate([row[:, None] * inv, col[:, None] * inv], axis=-1)
    cos = jnp.repeat(jnp.cos(ang), 2, axis=-1)
    sin = jnp.repeat(jnp.sin(ang), 2, axis=-1) * jnp.tile(jnp.array([-1.0, 1.0], F32), HD // 2)
    nblk = DEC_SEQ // QKV_TB
    cos_tab = jnp.concatenate([jnp.ones((1, QKV_TB, HD), F32), cos.reshape(nblk, QKV_TB, HD)], axis=0)
    sin_tab = jnp.concatenate([jnp.zeros((1, QKV_TB, HD), F32), sin.reshape(nblk, QKV_TB, HD)], axis=0)
    return cos_tab, sin_tab


def kernel(x_prompt, x_sample, state_gla, cache_k, cache_v, c, c_ctx, w_mod, b_mod, norm1_g, norm2_g,
           w_in_even, w_gate_up, b_gate_up, gla_norm_g, w_spatial, b_spatial, w_out_even,
           w_in_odd, q_norm_g, k_norm_g, w_out_odd, w_router_group, b_router_group,
           w_router_expert, b_router_expert, w_exp_gate, w_exp_up, w_exp_down, final_norm_g):
    x_all = None
    cond8 = jnp.concatenate([c_ctx[None], c, jnp.zeros((3, D), F32)], axis=0)
    mod = _modulation(cond8, w_mod, b_mod)
    cos_tab, sin_tab = _rope_tables()
    zero_state = jnp.zeros((1, 2, QK_W, GLA_DV), F32)
    state_in = state_gla.reshape(DEC_BATCH, -1, 2, QK_W, GLA_DV)
    cache_k2 = cache_k.reshape(DEC_BATCH, -1, SEQ, KV_W)
    cache_v2 = cache_v.reshape(DEC_BATCH, -1, SEQ, KV_W)

    gla_states, caches = [], None
    for l in range(DEPTH):
        i = l // 2
        n1g = norm1_g[l][None]
        if l % 2 == 0:
            w = w_in_even[i]
            win = jnp.concatenate([w[:, :1536], w[:, 1568:], w[:, 1536:1568], jnp.zeros((D, 96), F32)],
                                  axis=1).astype(BF16)
            wgu = jnp.zeros((128, 2 * QK_W), F32)
            wgu = wgu.at[0:GLA_RANK, 0:QK_W].set(w_gate_up[i, 0])
            wgu = wgu.at[GLA_RANK:2 * GLA_RANK, QK_W:].set(w_gate_up[i, 1]).astype(BF16)
            bgu = b_gate_up[i].reshape(1, 2 * QK_W)
            args = (mod[l], n1g, win, wgu, bgu, gla_norm_g[i][None], w_spatial[i].astype(BF16),
                    b_spatial[i].T, w_out_even[i].astype(BF16))
            first = l == 0
            x_all, st = _even_mixer(x_all, *args, zero_state, latent=False,
                                    x_first=x_prompt.reshape(N_CTX, D) if first else None)
            gla_states.append(st)
            x_all, _ = _even_mixer(x_all, *args, state_in[:, i], latent=True,
                                   x_first=x_sample.reshape(N_LAT, D) if first else None)
        else:
            q, k, v, *caches = _qkv(x_all, mod[l], n1g, w_in_odd[i].astype(BF16), q_norm_g[i][None],
                                    k_norm_g[i][None], cos_tab, sin_tab, i, caches)
            wout = w_out_odd[i].astype(BF16)
            x_all = _attention(x_all, mod[l], q, k, v, wout)
            x_all = _attention(x_all, mod[l], q, k, v, wout, cache_k2, cache_v2, layer_i=i)
        wr = jnp.concatenate([w_router_expert[l], w_router_group[l],
                              jnp.zeros((D, 128 - N_EXP - MOE_GROUPS), F32)], axis=1)
        br = jnp.concatenate([b_router_expert[l], b_router_group[l],
                              jnp.zeros((128 - N_EXP - MOE_GROUPS,), F32)])[None]
        x_all = _moe(x_all, mod[l], norm2_g[l][None], wr, br, w_exp_gate, w_exp_up, w_exp_down, l,
                     final_norm_g[None] if l == DEPTH - 1 else None)

    y_prompt = x_all[0].reshape(BATCH, SEQ, D)
    y_sample = x_all[1].reshape(DEC_BATCH, DEC_SEQ, D)
    new_state = jnp.stack(gla_states, axis=1).reshape(BATCH, -1, 2, GLA_HEADS, GLA_DK, GLA_DV)
    new_k, new_v = (a.reshape(BATCH, DEPTH // 2, SEQ, ATT_KV, HD) for a in caches)
    return (y_prompt, y_sample, new_state, new_k, new_v)
```

```python
import functools

import jax
import jax.numpy as jnp
import numpy as np
from jax import lax
from jax.experimental import pallas as pl
from jax.experimental.pallas import tpu as pltpu

F32 = jnp.float32
BF16 = jnp.bfloat16

D = 1024
BATCH, SEQ = 16, 256
DEC_BATCH, DEC_SEQ = 4, 1024
N_CTX = BATCH * SEQ
N_LAT = DEC_BATCH * DEC_SEQ
N_TOK = N_CTX + N_LAT
DEPTH = 4
EPS = 1e-6
GRID_W = 64
ROPE_THETA = 10000.0

GLA_HEADS, GLA_DK, GLA_DV, GLA_RANK, GLA_CHUNK, GLA_TAU = 4, 64, 128, 16, 128, 16.0
QK_W = GLA_HEADS * GLA_DK
V_W = GLA_HEADS * GLA_DV
GMLP_GROUPS, GMLP_DIM, GMLP_CHUNK = 4, 128, 128
GMLP_W = GMLP_GROUPS * GMLP_DIM
C_Q, C_K, C_V, C_G, C_U, C_VG, C_A = 0, 256, 512, 1024, 1536, 2048, 2560
EVEN_PACK = 2688

ATT_HEADS, ATT_KV, HD = 8, 2, 128
ATT_G = ATT_HEADS // ATT_KV
Q_W = ATT_HEADS * HD
KV_W = ATT_KV * HD

MOE_GROUPS, MOE_PER_GROUP = 4, 8
N_EXP = MOE_GROUPS * MOE_PER_GROUP
D_EXP = D // 4
NEG = -1e30

MIB = 1024 * 1024
V7X_VMEM_MIB = 64
LANES, SUBLANES = 128, 8


def _cp(*sem, vmem_mib=32):
    assert vmem_mib < V7X_VMEM_MIB
    return pltpu.CompilerParams(dimension_semantics=sem, vmem_limit_bytes=vmem_mib * MIB)


def _dot(a, b):
    return jnp.dot(a.astype(BF16), b.astype(BF16), preferred_element_type=F32)


def _dot_nt(a, b):
    return lax.dot_general(a.astype(BF16), b.astype(BF16), (((1,), (1,)), ((), ())),
                           preferred_element_type=F32)


def _dot_tn(a, b):
    return lax.dot_general(a.astype(BF16), b.astype(BF16), (((0,), (0,)), ((), ())),
                           preferred_element_type=F32)


def _rms(x, g):
    return x * lax.rsqrt(jnp.mean(x * x, axis=-1, keepdims=True) + EPS) * g


def _silu(x):
    return x * jax.nn.sigmoid(x)


def _gelu(x):
    return 0.5 * x * (1.0 + jnp.tanh(np.sqrt(2.0 / np.pi).astype(np.float32) * (x + 0.044715 * (x * x * x))))


def _log_sigmoid(z):
    return jnp.minimum(z, 0.0) - jnp.log(1.0 + jnp.exp(-jnp.abs(z)))


assert D == LANES * SUBLANES


def _rows_to_tiles(ref, x):
    rows = x.shape[0]
    for j in range(SUBLANES):
        ref[pl.ds(j, rows, stride=SUBLANES), :] = x[:, j * LANES:(j + 1) * LANES]


def _tiles_to_rows(ref, rows):
    return jnp.concatenate([ref[pl.ds(j, rows, stride=SUBLANES), :] for j in range(SUBLANES)], axis=1)


def _tile_of(ref, row8):
    return ref.at[pl.ds(pl.multiple_of(row8, SUBLANES), SUBLANES), :]


def _resident(shape, index_map):
    return pl.BlockSpec(shape, index_map, pipeline_mode=pl.Buffered(1))


def _split_bf16(x):
    hi = x.astype(BF16)
    return hi, (x - hi.astype(F32)).astype(BF16)


def _mod_kernel(cond_ref, w_ref, b_ref, o_ref):
    s_hi, s_lo = _split_bf16(_silu(cond_ref[...]))
    w_hi, w_lo = _split_bf16(w_ref[...])
    dot = functools.partial(jnp.dot, preferred_element_type=F32)
    both = dot(jnp.concatenate([s_hi, s_lo], axis=0), w_hi)
    o_ref[...] = both[0:8] + both[8:16] + dot(s_hi, w_lo) + b_ref[...]


def _modulation(cond8, w_mod, b_mod):
    tn = 2048
    out = pl.pallas_call(
        _mod_kernel,
        grid=(DEPTH, 6 * D // tn),
        in_specs=[
            pl.BlockSpec((8, D), lambda l, j: (0, 0)),
            pl.BlockSpec((None, D, tn), lambda l, j: (l, 0, j)),
            pl.BlockSpec((None, 1, tn), lambda l, j: (l, 0, j)),
        ],
        out_specs=pl.BlockSpec((None, 8, tn), lambda l, j: (l, 0, j)),
        out_shape=jax.ShapeDtypeStruct((DEPTH, 8, 6 * D), F32),
        compiler_params=_cp("arbitrary", "arbitrary", vmem_mib=24),
        name="adaln_mod",
    )(cond8, w_mod, b_mod.reshape(DEPTH, 1, 6 * D))
    return out.reshape(DEPTH, 8, 6, D)


def _even_kernel(x_ref, mod_ref, n1g_ref, win_ref, wgu_ref, bgu_ref, glag_ref, ws_ref, bs_ref,
                 wout_ref, s0_ref, xo_ref, st_ref, proj, la, o_f, o_b, st_scr, *, T):
    n_chunks = T // GLA_CHUNK
    shift, scale, gate = mod_ref[0:1, :], mod_ref[1:2, :], mod_ref[2:3, :]
    RB = 128
    PROJ_RB = 256

    def proj_body(r, carry):
        r0 = pl.multiple_of(r * PROJ_RB, PROJ_RB)
        h = _rms(x_ref[pl.ds(r0, PROJ_RB), :], n1g_ref[...]) * (1.0 + scale) + shift
        p = _dot(h, win_ref[...])
        proj[pl.ds(r0, PROJ_RB), :] = p
        z = _dot(p[:, C_A:C_A + 128], wgu_ref[...]) + bgu_ref[...]
        la[pl.ds(r0, PROJ_RB), :] = _log_sigmoid(z) * (1.0 / GLA_TAU)
        return carry

    lax.fori_loop(0, T // PROJ_RB, proj_body, 0)

    st_scr[0] = s0_ref[0].T
    st_scr[1] = s0_ref[1].T

    ci = lax.broadcasted_iota(jnp.int32, (GLA_CHUNK, GLA_CHUNK), 0)
    cj = lax.broadcasted_iota(jnp.int32, (GLA_CHUNK, GLA_CHUNK), 1)
    tri = (jnp.where(ci >= cj, 1.0, 0.0).astype(BF16), jnp.where(ci <= cj, 1.0, 0.0).astype(BF16))
    ai = lax.broadcasted_iota(jnp.int32, (GLA_HEADS * GLA_CHUNK, GLA_CHUNK), 0) % GLA_CHUNK
    aj = lax.broadcasted_iota(jnp.int32, (GLA_HEADS * GLA_CHUNK, GLA_CHUNK), 1)
    amask = (ai >= aj, ai <= aj)
    lane_head = lax.broadcasted_iota(jnp.int32, (1, QK_W), 1) // GLA_DK
    hmask = [jnp.where(lane_head == h, 1.0, 0.0) for h in range(GLA_HEADS)]

    def chunk_body(i, carry):
        for d in range(2):
            c = i if d == 0 else n_chunks - 1 - i
            r0 = pl.multiple_of(c * GLA_CHUNK, GLA_CHUNK)
            q = proj[pl.ds(r0, GLA_CHUNK), C_Q:C_Q + QK_W] * (GLA_DK ** -0.5)
            k = proj[pl.ds(r0, GLA_CHUNK), C_K:C_K + QK_W]
            v = proj[pl.ds(r0, GLA_CHUNK), C_V:C_V + V_W]
            lac = la[pl.ds(r0, GLA_CHUNK), d * QK_W:(d + 1) * QK_W]
            hi = lac.astype(BF16)
            lo = (lac - hi.astype(F32)).astype(BF16)
            b = (jnp.dot(tri[d], hi, preferred_element_type=F32)
                 + jnp.dot(tri[d], lo, preferred_element_type=F32))
            bend = b[GLA_CHUNK - 1:GLA_CHUNK, :] if d == 0 else b[0:1, :]
            qe = q * jnp.exp(b)
            ke = k * jnp.exp(-b)
            kd = k * jnp.exp(bend - b)
            st = st_scr[d]
            qstack = jnp.concatenate([qe * hmask[h] for h in range(GLA_HEADS)], axis=0).astype(BF16)
            att = jnp.where(amask[d], _dot_nt(qstack, ke), 0.0)
            inter = _dot_nt(qstack, st)
            outs = []
            for h in range(GLA_HEADS):
                rows = slice(h * GLA_CHUNK, (h + 1) * GLA_CHUNK)
                outs.append(_dot(att[rows], v[:, h * GLA_DV:(h + 1) * GLA_DV]) + inter[rows])
            o = jnp.concatenate(outs, axis=1)
            if d == 0:
                o_f[pl.ds(r0, GLA_CHUNK), :] = o
            else:
                o_b[pl.ds(r0, GLA_CHUNK), :] = o
            vstack = jnp.concatenate([v[:, h * GLA_DV:(h + 1) * GLA_DV] for h in range(GLA_HEADS)], axis=0)
            kstack = jnp.concatenate([kd * hmask[h] for h in range(GLA_HEADS)], axis=0)
            st_scr[d] = st * jnp.exp(bend) + _dot_tn(vstack, kstack)
        return carry

    lax.fori_loop(0, n_chunks, chunk_body, 0, unroll=2)
    st_ref[0] = st_scr[0].T
    st_ref[1] = st_scr[1].T

    def out_body(r, carry):
        r0 = pl.multiple_of(r * RB, RB)
        osum = o_f[pl.ds(r0, RB), :] + o_b[pl.ds(r0, RB), :]
        g = proj[pl.ds(r0, RB), C_G:C_G + V_W]
        u = proj[pl.ds(r0, RB), C_U:C_U + GMLP_W]
        vg = _gelu(proj[pl.ds(r0, RB), C_VG:C_VG + GMLP_W])
        parts = []
        for h in range(GLA_HEADS):
            oh = osum[:, h * GLA_DV:(h + 1) * GLA_DV]
            parts.append(_rms(oh, glag_ref[...]) * _silu(g[:, h * GLA_DV:(h + 1) * GLA_DV]))
        for gi in range(GMLP_GROUPS):
            vc = vg[:, gi * GMLP_DIM:(gi + 1) * GMLP_DIM]
            vc = vc - jnp.mean(vc, axis=-1, keepdims=True)
            vn = vc * lax.rsqrt(jnp.mean(vc * vc, axis=-1, keepdims=True) + EPS)
            sg = _dot(ws_ref[gi], vn) + bs_ref[:, gi:gi + 1]
            parts.append(_gelu(u[:, gi * GMLP_DIM:(gi + 1) * GMLP_DIM]) * sg)
        mix = jnp.concatenate(parts, axis=1)
        y = _dot(mix, wout_ref[...])
        xo_ref[pl.ds(r0, RB), :] = x_ref[pl.ds(r0, RB), :] + gate * y
        return carry

    lax.fori_loop(0, T // RB, out_body, 0)


def _even_mixer(x_all, mod_l, n1g, win, wgu, bgu, glag, ws, bs, wout, s0, *, latent, x_first=None):
    if latent:
        T, nseq, blk0 = DEC_SEQ, DEC_BATCH, N_CTX // DEC_SEQ
        cond = lambda i: 1 + i
        s0_spec = pl.BlockSpec((None, 2, QK_W, GLA_DV), lambda i: (i, 0, 0, 0))
    else:
        T, nseq, blk0 = SEQ, BATCH, 0
        cond = lambda i: 0
        s0_spec = pl.BlockSpec((None, 2, QK_W, GLA_DV), lambda i: (0, 0, 0, 0))
    const2 = lambda i: (0, 0)
    body = functools.partial(_even_kernel, T=T)
    x_spec = pl.BlockSpec((T, D), lambda i: (blk0 + i, 0))
    if x_first is None:
        lead_specs, lead_args, aliases = [x_spec], (x_all,), {0: 0}
    elif x_all is None:
        lead_specs, lead_args, aliases = [pl.BlockSpec((T, D), lambda i: (i, 0))], (x_first,), {}
    else:
        lead_specs = [pl.BlockSpec(memory_space=pl.ANY), pl.BlockSpec((T, D), lambda i: (i, 0))]
        lead_args, aliases = (x_all, x_first), {0: 0}
        body = lambda dst_ref, *refs: _even_kernel(*refs, T=T)
    x_new, states = pl.pallas_call(
        body,
        grid=(nseq,),
        in_specs=lead_specs + [
            pl.BlockSpec((None, 6, D), lambda i: (cond(i), 0, 0)),
            _resident((1, D), const2),
            _resident((D, EVEN_PACK), const2),
            _resident((128, 2 * QK_W), const2),
            _resident((1, 2 * QK_W), const2),
            _resident((1, GLA_DV), const2),
            _resident((GMLP_GROUPS, GMLP_CHUNK, GMLP_CHUNK), lambda i: (0, 0, 0)),
            _resident((GMLP_CHUNK, GMLP_GROUPS), const2),
            _resident((D, D), const2),
            s0_spec,
        ],
        out_specs=[
            x_spec,
            pl.BlockSpec((None, 2, QK_W, GLA_DV), lambda i: (i, 0, 0, 0)),
        ],
        out_shape=[
            jax.ShapeDtypeStruct((N_TOK, D), F32),
            jax.ShapeDtypeStruct((nseq, 2, QK_W, GLA_DV), F32),
        ],
        scratch_shapes=[
            pltpu.VMEM((T, EVEN_PACK), F32),
            pltpu.VMEM((T, 2 * QK_W), F32),
            pltpu.VMEM((T, V_W), F32),
            pltpu.VMEM((T, V_W), F32),
            pltpu.VMEM((2, GLA_DV, QK_W), F32),
        ],
        input_output_aliases=aliases,
        compiler_params=_cp("arbitrary", vmem_mib=44 if latent else 14),
        name="even_mixer_latent" if latent else "even_mixer_context",
    )(*lead_args, mod_l, n1g, win, wgu, bgu, glag, ws, bs, wout, s0)
    return x_new, states


QKV_TB = 512


def _qkv_kernel(x_ref, mod_ref, n1g_ref, win_ref, gq_ref, gk_ref, cos_ref, sin_ref, q_ref, k_ref, v_ref,
                ck_ref, cv_ref):
    shift, scale = mod_ref[0:1, :], mod_ref[1:2, :]
    h = _rms(x_ref[...], n1g_ref[...]) * (1.0 + scale) + shift
    p = _dot(h, win_ref[...])
    cos, sin = cos_ref[...], sin_ref[...]
    even_lane = lax.broadcasted_iota(jnp.int32, (1, HD), 1) % 2 == 0

    def rope(xn):
        swapped = jnp.where(even_lane, pltpu.roll(xn, HD - 1, axis=1), pltpu.roll(xn, 1, axis=1))
        return xn * cos + swapped * sin

    def emit(rotate, to_cache):
        for hh in range(ATT_HEADS):
            qn = _rms(p[:, hh * HD:(hh + 1) * HD], gq_ref[...])
            q_ref[:, hh * HD:(hh + 1) * HD] = (rotate(qn) * (HD ** -0.5)).astype(BF16)
        for hh in range(ATT_KV):
            kn = rotate(_rms(p[:, Q_W + hh * HD:Q_W + (hh + 1) * HD], gk_ref[...]))
            k_ref[:, hh * HD:(hh + 1) * HD] = kn
            if to_cache:
                for s in range(QKV_TB // SEQ):
                    ck_ref[s, :, hh * HD:(hh + 1) * HD] = kn[s * SEQ:(s + 1) * SEQ]
        if to_cache:
            for s in range(QKV_TB // SEQ):
                cv_ref[s] = p[s * SEQ:(s + 1) * SEQ, Q_W + KV_W:]

    is_latent = pl.program_id(0) >= N_CTX // QKV_TB

    @pl.when(is_latent)
    def _():
        emit(rope, False)

    @pl.when(jnp.logical_not(is_latent))
    def _():
        emit(lambda xn: xn, True)

    v_ref[...] = p[:, Q_W + KV_W:]


def _qkv(x_all, mod_l, n1g, win, gq, gk, cos_tab, sin_tab, layer_i, caches=None):
    nb_ctx = N_CTX // QKV_TB
    per_seq = DEC_SEQ // QKV_TB
    cond = lambda i: jnp.where(i < nb_ctx, 0, 1 + (i - nb_ctx) // per_seq)
    tab = lambda i: jnp.where(i < nb_ctx, 0, 1 + (i - nb_ctx) % per_seq)
    const2 = lambda i: (0, 0)
    cache_spec = pl.BlockSpec((QKV_TB // SEQ, None, SEQ, KV_W), lambda i: (jnp.minimum(i, nb_ctx - 1), layer_i, 0, 0))
    cache_shape = jax.ShapeDtypeStruct((BATCH, DEPTH // 2, SEQ, KV_W), F32)
    if caches is None:
        body, lead_specs, lead_args, aliases = _qkv_kernel, [], (), {}
    else:
        body = lambda ck_in, cv_in, *refs: _qkv_kernel(*refs)
        lead_specs = [pl.BlockSpec(memory_space=pl.ANY)] * 2
        lead_args, aliases = tuple(caches), {0: 3, 1: 4}
    return pl.pallas_call(
        body,
        grid=(N_TOK // QKV_TB,),
        in_specs=lead_specs + [
            pl.BlockSpec((QKV_TB, D), lambda i: (i, 0)),
            pl.BlockSpec((None, 6, D), lambda i: (cond(i), 0, 0)),
            _resident((1, D), const2),
            _resident((D, Q_W + 2 * KV_W), const2),
            _resident((1, HD), const2),
            _resident((1, HD), const2),
            pl.BlockSpec((None, QKV_TB, HD), lambda i: (tab(i), 0, 0)),
            pl.BlockSpec((None, QKV_TB, HD), lambda i: (tab(i), 0, 0)),
        ],
        out_specs=[
            pl.BlockSpec((QKV_TB, Q_W), lambda i: (i, 0)),
            pl.BlockSpec((QKV_TB, KV_W), lambda i: (i, 0)),
            pl.BlockSpec((QKV_TB, KV_W), lambda i: (i, 0)),
            cache_spec,
            cache_spec,
        ],
        out_shape=[
            jax.ShapeDtypeStruct((N_TOK, Q_W), BF16),
            jax.ShapeDtypeStruct((N_TOK, KV_W), F32),
            jax.ShapeDtypeStruct((N_TOK, KV_W), F32),
            cache_shape,
            cache_shape,
        ],
        input_output_aliases=aliases,
        compiler_params=_cp("arbitrary", vmem_mib=16),
        name="odd_qkv",
    )(*lead_args, x_all, mod_l, n1g, win, gq, gk, cos_tab, sin_tab)


ATT_TQ_LATENT = 512


def _attn_kernel(*refs, n_kv):
    q_ref = refs[0]
    kv_refs = refs[1:1 + 2 * n_kv]
    x_ref, mod_ref, wout_ref, xo_ref, att_scr = refs[1 + 2 * n_kv:]
    gate = mod_ref[2:3, :]
    for kh in range(ATT_KV):
        ks = [kv_refs[2 * s][:, kh * HD:(kh + 1) * HD].astype(BF16) for s in range(n_kv)]
        vs = [jnp.concatenate([kv_refs[2 * s + 1][:, kh * HD:(kh + 1) * HD].astype(BF16),
                               jnp.ones((kv_refs[2 * s + 1].shape[0], HD), BF16)], axis=1) for s in range(n_kv)]
        for g in range(ATT_G):
            hh = kh * ATT_G + g
            qh = q_ref[:, hh * HD:(hh + 1) * HD]
            ss = [_dot_nt(qh, kk) for kk in ks]
            m = ss[0].max(axis=-1, keepdims=True)
            for s in ss[1:]:
                m = jnp.maximum(m, s.max(axis=-1, keepdims=True))
            o = _dot(jnp.exp(ss[0] - m), vs[0])
            for s, vv in zip(ss[1:], vs[1:]):
                o = o + _dot(jnp.exp(s - m), vv)
            att_scr[:, hh * HD:(hh + 1) * HD] = o[:, :HD] / o[:, HD:HD + 1]
    y = _dot(att_scr[...], wout_ref[...])
    xo_ref[...] = x_ref[...] + gate * y


def _attention(x_all, mod_l, q, k, v, wout, cache_k=None, cache_v=None, layer_i=0):
    latent = cache_k is not None
    const2 = lambda *a: (0, 0)
    tq = ATT_TQ_LATENT if latent else SEQ
    if latent:
        nq = DEC_SEQ // tq
        row_blk = lambda b, j: (N_CTX // tq + b * nq + j, 0)
        grid = (DEC_BATCH, nq)
        kv_specs = [
            pl.BlockSpec((None, None, SEQ, KV_W), lambda b, j: (b, layer_i, 0, 0)),
            pl.BlockSpec((None, None, SEQ, KV_W), lambda b, j: (b, layer_i, 0, 0)),
            pl.BlockSpec((DEC_SEQ, KV_W), lambda b, j: (N_CTX // DEC_SEQ + b, 0)),
            pl.BlockSpec((DEC_SEQ, KV_W), lambda b, j: (N_CTX // DEC_SEQ + b, 0)),
        ]
        kv_args = (cache_k, cache_v, k, v)
        mod_spec = pl.BlockSpec((None, 6, D), lambda b, j: (1 + b, 0, 0))
        sem = ("arbitrary", "arbitrary")
        n_kv = 2
    else:
        row_blk = lambda i: (i, 0)
        grid = (BATCH,)
        kv_specs = [pl.BlockSpec((SEQ, KV_W), row_blk), pl.BlockSpec((SEQ, KV_W), row_blk)]
        kv_args = (k, v)
        mod_spec = pl.BlockSpec((None, 6, D), lambda i: (0, 0, 0))
        sem = ("arbitrary",)
        n_kv = 1
    n_in = 1 + len(kv_args)
    return pl.pallas_call(
        functools.partial(_attn_kernel, n_kv=n_kv),
        grid=grid,
        in_specs=[pl.BlockSpec((tq, Q_W), row_blk)] + kv_specs + [
            pl.BlockSpec((tq, D), row_blk),
            mod_spec,
            _resident((D, D), const2),
        ],
        out_specs=pl.BlockSpec((tq, D), row_blk),
        out_shape=jax.ShapeDtypeStruct((N_TOK, D), F32),
        scratch_shapes=[pltpu.VMEM((tq, Q_W), F32)],
        input_output_aliases={n_in: 0},
        compiler_params=_cp(*sem, vmem_mib=24 if latent else 10),
        name="attention_latent" if latent else "attention_context",
    )(q, *kv_args, x_all, mod_l, wout)


ROUTE_TB = 512
HALF_TOK = N_TOK // 2
M_E1, M_E2, M_G1, M_G2, M_R1, M_R2 = 0, 1, 2, 3, 4, 5


def _router_kernel(x_ref, mod_ref, n2g_ref, w2_ref, br_ref, h_ref, metat_ref, cnt_ref, run):
    @pl.when(pl.program_id(0) % (HALF_TOK // ROUTE_TB) == 0)
    def _():
        run[...] = jnp.zeros_like(run)

    shift, scale = mod_ref[3:4, :], mod_ref[4:5, :]
    h = _rms(x_ref[...], n2g_ref[...]) * (1.0 + scale) + shift
    _rows_to_tiles(h_ref, h)
    h_hi, h_lo = _split_bf16(h)
    dot = functools.partial(jnp.dot, preferred_element_type=F32)
    wide = dot(h_hi, w2_ref[...])
    logits = wide[:, :128] + wide[:, 128:] + dot(h_lo, w2_ref[:, :128]) + br_ref[...]
    lane = lax.broadcasted_iota(jnp.int32, logits.shape, 1).astype(F32)
    big = 1e4

    def first_argmax(vals):
        m = vals.max(axis=-1, keepdims=True)
        return m, jnp.where(vals == m, lane, big).min(axis=-1, keepdims=True)

    gl = jnp.where((lane >= N_EXP) & (lane < N_EXP + MOE_GROUPS), logits, NEG)
    gmax, glane = first_argmax(gl)
    g_p = 1.0 / jnp.exp(gl - gmax).sum(axis=-1, keepdims=True)
    lo = (glane - N_EXP) * MOE_PER_GROUP
    el = jnp.where((lane >= lo) & (lane < lo + MOE_PER_GROUP), logits, NEG)
    m1, i1 = first_argmax(el)
    m2, i2 = first_argmax(jnp.where(lane == i1, NEG, el))
    t = jnp.exp(m2 - m1)
    w1 = 1.0 / (1.0 + t)
    sel1, sel2 = lane == i1, lane == i2
    onehot = jnp.where(sel1 | sel2, 1.0, 0.0)
    ri = lax.broadcasted_iota(jnp.int32, (ROUTE_TB, ROUTE_TB), 0)
    rj = lax.broadcasted_iota(jnp.int32, (ROUTE_TB, ROUTE_TB), 1)
    before = _dot(jnp.where(ri > rj, 1.0, 0.0), onehot) + run[...]
    r1 = jnp.where(sel1, before, 0.0).sum(axis=-1, keepdims=True)
    r2 = jnp.where(sel2, before, 0.0).sum(axis=-1, keepdims=True)
    run[...] += onehot.sum(axis=0, keepdims=True)
    cnt_ref[...] = run[...]
    meta = jnp.zeros_like(logits)
    for j, val in enumerate([i1, i2, w1 * g_p, (t * w1) * g_p, r1, r2]):
        meta = jnp.where(lane == j, val, meta)
    metat_ref[...] = meta.T[0:8, :]


def _router(x_all, mod_l, n2g, wr, br):
    w2 = jnp.concatenate(_split_bf16(wr), axis=1)
    nb_ctx = N_CTX // ROUTE_TB
    per_seq = DEC_SEQ // ROUTE_TB
    cond = lambda i: jnp.where(i < nb_ctx, 0, 1 + (i - nb_ctx) // per_seq)
    const2 = lambda i: (0, 0)
    return pl.pallas_call(
        _router_kernel,
        grid=(N_TOK // ROUTE_TB,),
        in_specs=[
            pl.BlockSpec((ROUTE_TB, D), lambda i: (i, 0)),
            pl.BlockSpec((None, 6, D), lambda i: (cond(i), 0, 0)),
            _resident((1, D), const2),
            _resident((D, 256), const2),
            _resident((1, 128), const2),
        ],
        out_specs=[
            pl.BlockSpec((ROUTE_TB * 8, 128), lambda i: (i, 0)),
            pl.BlockSpec((8, ROUTE_TB), lambda i: (0, i)),
            pl.BlockSpec((None, 1, 128), lambda i: (i // (HALF_TOK // ROUTE_TB), 0, 0)),
        ],
        out_shape=[
            jax.ShapeDtypeStruct((N_TOK * 8, 128), F32),
            jax.ShapeDtypeStruct((8, N_TOK), F32),
            jax.ShapeDtypeStruct((2, 1, 128), F32),
        ],
        scratch_shapes=[pltpu.VMEM((1, 128), F32)],
        compiler_params=_cp("arbitrary", vmem_mib=14),
        name="moe_router",
    )(x_all, mod_l, n2g, w2, br)


EXP_TM = 128
N_ASSIGN = 2 * N_TOK
N_GROUPS = 2 * N_EXP
MAX_TILES = N_ASSIGN // EXP_TM + N_GROUPS
N_SORTED = MAX_TILES * EXP_TM
ORDER_BLK = 2048
CODE_PLANE = 2 * HALF_TOK
CODE_MASK = 8 * CODE_PLANE - 1
DUMMY8 = HALF_TOK * 8


def _order_kernel(pos1_ref, pos2_ref, pad_lo_ref, pad_hi_ref, src_ref):
    i = pl.program_id(0)
    local = (i % (HALF_TOK // ORDER_BLK)) * ORDER_BLK

    def body(t, carry):
        src_ref[pos1_ref[t]] = (local + t) * 8
        src_ref[pos2_ref[t]] = (local + t + CODE_PLANE) * 8
        return carry

    lax.fori_loop(0, ORDER_BLK, body, 0, unroll=16)

    @pl.when(i == 0)
    def _():
        def group(g, carry):
            def pad(p, c):
                src_ref[p] = DUMMY8
                return c
            return lax.fori_loop(pad_lo_ref[g], pad_hi_ref[g], pad, carry)

        lax.fori_loop(0, N_GROUPS, group, 0)


def _order(pos, pad_lo, pad_hi):
    return pl.pallas_call(
        _order_kernel,
        grid=(N_TOK // ORDER_BLK,),
        in_specs=[
            pl.BlockSpec((ORDER_BLK,), lambda i: (i,), memory_space=pltpu.SMEM),
            pl.BlockSpec((ORDER_BLK,), lambda i: (N_TOK // ORDER_BLK + i,), memory_space=pltpu.SMEM),
            pl.BlockSpec(memory_space=pltpu.SMEM),
            pl.BlockSpec(memory_space=pltpu.SMEM),
        ],
        out_specs=pl.BlockSpec(memory_space=pltpu.SMEM),
        out_shape=jax.ShapeDtypeStruct((N_SORTED,), jnp.int32),
        compiler_params=_cp("arbitrary", vmem_mib=8),
        name="moe_order",
    )(pos, pos, pad_lo, pad_hi)


GATE_BLK = CODE_PLANE + HALF_TOK
ACC_TOK = HALF_TOK + 64
GATHER_GROUP, ACC_GROUP = 16, 8


RES_TB = 256


def _experts_kernel(tile0_ref, ntile_ref, count_ref, src_ref, gs_ref, h_hbm, x_hbm, mod_ref, wg_ref, wu_ref, wd_ref,
                    *rest, final):
    if final:
        fg_ref, *dst_hbm = rest[:3]
        rest = rest[3:]
    else:
        dst_hbm, rest = rest[:1], rest[1:]
    h_res, acc, xbuf, ybuf, wgb, wub, wdb, xin, xout, sem, in_sem, out_sem = rest
    group = pl.program_id(0)
    expert = group % N_EXP
    half = group // N_EXP
    rows0 = pl.multiple_of(half * (HALF_TOK * 8), 8)

    @pl.when(expert == 0)
    def _():
        cp = pltpu.make_async_copy(h_hbm.at[pl.ds(rows0, HALF_TOK * 8), :], h_res.at[pl.ds(0, HALF_TOK * 8), :], sem)
        cp.start()
        h_res[pl.ds(DUMMY8, 8), :] = jnp.zeros((8, 128), F32)
        xbuf[...] = jnp.zeros_like(xbuf)

        def zero(i, carry):
            acc[pl.ds(pl.multiple_of(i * 512, 512), 512), :] = jnp.zeros((512, 128), F32)
            return carry

        lax.fori_loop(0, ACC_TOK * 8 // 512, zero, 0)
        cp.wait()

    n_tiles = ntile_ref[group]

    @pl.when(n_tiles > 0)
    def _():
        wgb[...] = wg_ref[...].astype(BF16)
        wub[...] = wu_ref[...].astype(BF16)
        wdb[...] = wd_ref[...].astype(BF16)

    row0 = tile0_ref[group] * EXP_TM
    row_end = row0 + count_ref[group]

    def process(base, rows):
        live = (jnp.clip(row_end - base, 0, rows) + GATHER_GROUP - 1) // GATHER_GROUP

        def gather(g, c):
            for i in range(GATHER_GROUP):
                r = g * GATHER_GROUP + i
                xbuf[pl.ds(pl.multiple_of(r * 8, 8), 8), :] = _tile_of(h_res, src_ref[base + r] & CODE_MASK)[...]
            return c

        lax.fori_loop(0, live, gather, 0)
        x = _tiles_to_rows(xbuf, rows).astype(BF16)
        hid = _silu(_dot(x, wgb[...])) * _dot(x, wub[...])
        _rows_to_tiles(ybuf, _dot(hid, wdb[...]))

        def accumulate(g, c):
            targets, values = [], []
            for i in range(ACC_GROUP):
                r = g * ACC_GROUP + i
                code = src_ref[base + r]
                target = _tile_of(acc, code & CODE_MASK)
                targets.append(target)
                values.append(target[...] + gs_ref[code >> 3] * ybuf[pl.ds(pl.multiple_of(r * 8, 8), 8), :])
            for target, value in zip(targets, values):
                target[...] = value
            return c

        lax.fori_loop(0, live * (GATHER_GROUP // ACC_GROUP), accumulate, 0)


    def pair_body(j, carry):
        process(row0 + j * (2 * EXP_TM), 2 * EXP_TM)
        return carry

    lax.fori_loop(0, n_tiles // 2, pair_body, 0)

    @pl.when(n_tiles % 2 == 1)
    def _():
        process(row0 + (n_tiles - 1) * EXP_TM, EXP_TM)

    def rows_of(first, blk):
        return pl.ds(pl.multiple_of(first + blk * RES_TB, RES_TB), RES_TB)

    def load_x(blk, slot):
        return pltpu.make_async_copy(x_hbm.at[rows_of(half * HALF_TOK, blk), :], xin.at[slot], in_sem.at[slot])

    def residual(blk, slot):
        cond = jnp.where(half == 0, 0, 1 + blk // (DEC_SEQ // RES_TB))
        gate = mod_ref[cond, 5:6, :]
        y = _tiles_to_rows(acc.at[pl.ds(pl.multiple_of(blk * (RES_TB * 8), RES_TB * 8), RES_TB * 8), :], RES_TB)
        x_new = xin[slot] + gate * y
        xout[slot] = _rms(x_new, fg_ref[...]) if final else x_new

    def epilogue(dst, first_row):
        def store_x(blk, slot):
            return pltpu.make_async_copy(xout.at[slot], dst.at[rows_of(first_row, blk), :], out_sem.at[slot])

        n_pairs = HALF_TOK // RES_TB // 2
        load_x(0, 0).start()

        def pair(p, carry):
            for slot in range(2):
                blk = 2 * p + slot
                if slot == 0:
                    load_x(blk + 1, 1).start()
                else:
                    @pl.when(p + 1 < n_pairs)
                    def _():
                        load_x(blk + 1, 0).start()
                load_x(blk, slot).wait()

                @pl.when(p > 0)
                def _():
                    store_x(blk - 2, slot).wait()

                residual(blk, slot)
                store_x(blk, slot).start()
            return carry

        lax.fori_loop(0, n_pairs, pair, 0)
        store_x(2 * n_pairs - 2, 0).wait()
        store_x(2 * n_pairs - 1, 1).wait()

    if final:
        for which in range(2):
            @pl.when((expert == N_EXP - 1) & (half == which))
            def _():
                epilogue(dst_hbm[which], 0)
    else:
        @pl.when(expert == N_EXP - 1)
        def _():
            epilogue(dst_hbm[0], half * HALF_TOK)


def _experts(tile0, n_tiles, counts, src, gs, h, x_all, mod_l, wg, wu, wd, layer, final_g=None):
    final = final_g is not None
    wmap = lambda g, t0, nt, cnt, src: (layer, g % N_EXP, 0, 0)
    any_spec = pl.BlockSpec(memory_space=pl.ANY)
    extra_specs = [pl.BlockSpec((1, D), lambda g, t0, nt, cnt, src: (0, 0))] if final else []
    extra_args = (final_g,) if final else ()
    return pl.pallas_call(
        functools.partial(_experts_kernel, final=final),
        grid_spec=pltpu.PrefetchScalarGridSpec(
            num_scalar_prefetch=4,
            grid=(N_GROUPS,),
            in_specs=[
                pl.BlockSpec((GATE_BLK,), lambda g, t0, nt, cnt, src: (g // N_EXP,), memory_space=pltpu.SMEM),
                pl.BlockSpec(memory_space=pl.ANY),
                pl.BlockSpec(memory_space=pl.ANY),
                pl.BlockSpec((8, 6, D), lambda g, t0, nt, cnt, src: (0, 0, 0)),
                pl.BlockSpec((None, None, D, D_EXP), wmap),
                pl.BlockSpec((None, None, D, D_EXP), wmap),
                pl.BlockSpec((None, None, D_EXP, D), wmap),
            ] + extra_specs,
            out_specs=[any_spec, any_spec] if final else any_spec,
            scratch_shapes=[
                pltpu.VMEM((ACC_TOK * 8, 128), F32),
                pltpu.VMEM((ACC_TOK * 8, 128), F32),
                pltpu.VMEM((2 * EXP_TM * 8, 128), F32),
                pltpu.VMEM((2 * EXP_TM * 8, 128), F32),
                pltpu.VMEM((D, D_EXP), BF16),
                pltpu.VMEM((D, D_EXP), BF16),
                pltpu.VMEM((D_EXP, D), BF16),
                pltpu.VMEM((2, RES_TB, D), F32),
                pltpu.VMEM((2, RES_TB, D), F32),
                pltpu.SemaphoreType.DMA,
                pltpu.SemaphoreType.DMA((2,)),
                pltpu.SemaphoreType.DMA((2,)),
            ],
        ),
        out_shape=([jax.ShapeDtypeStruct((HALF_TOK, D), F32)] * 2 if final
                   else jax.ShapeDtypeStruct((N_TOK, D), F32)),
        input_output_aliases={} if final else {6: 0},
        compiler_params=_cp("arbitrary", vmem_mib=52),
        name="moe_experts_final" if final else "moe_experts",
    )(tile0, n_tiles, counts, src, gs, h, x_all, mod_l, wg, wu, wd, *extra_args)


def _moe(x_all, mod_l, n2g, wr, br, wg, wu, wd, layer, final_g=None):
    h, metat, cnt = _router(x_all, mod_l, n2g, wr, br)
    counts = cnt[:, 0, :N_EXP].astype(jnp.int32).reshape(N_GROUPS)
    padded = (counts + EXP_TM - 1) // EXP_TM * EXP_TM
    ends = jnp.cumsum(padded)
    offs = ends - padded
    rec = metat.astype(jnp.int32)
    half = (jnp.arange(N_TOK, dtype=jnp.int32) // HALF_TOK)[None, :]
    group = rec[M_E1:M_E2 + 1] + N_EXP * half
    is_group = group[None] == jnp.arange(N_GROUPS, dtype=jnp.int32)[:, None, None]
    pos = jnp.sum(jnp.where(is_group, offs[:, None, None], 0), axis=0) + rec[M_R1:M_R2 + 1]
    live_end = offs + (counts + GATHER_GROUP - 1) // GATHER_GROUP * GATHER_GROUP
    src = _order(pos.reshape(N_ASSIGN), offs + counts, live_end)
    g12 = metat[M_G1:M_G2 + 1].reshape(2, 2, HALF_TOK)
    gates = jnp.concatenate([g12[0], jnp.zeros((2, CODE_PLANE - HALF_TOK), F32), g12[1]], axis=1)
    return _experts(offs // EXP_TM, padded // EXP_TM, counts, src, gates.reshape(2 * GATE_BLK), h, x_all, mod_l,
                    wg, wu, wd, layer, final_g)


def _rope_tables():
    pos = jnp.arange(DEC_SEQ)
    row = (pos // GRID_W).astype(F32)
    col = (pos % GRID_W).astype(F32)
    n_freq = HD // 4
    inv = ROPE_THETA ** (-jnp.arange(n_freq, dtype=F32) / n_freq)
    ang = jnp.concatenate([row[:, None] * inv, col[:, None] * inv], axis=-1)
    cos = jnp.repeat(jnp.cos(ang), 2, axis=-1)
    sin = jnp.repeat(jnp.sin(ang), 2, axis=-1) * jnp.tile(jnp.array([-1.0, 1.0], F32), HD // 2)
    nblk = DEC_SEQ // QKV_TB
    cos_tab = jnp.concatenate([jnp.ones((1, QKV_TB, HD), F32), cos.reshape(nblk, QKV_TB, HD)], axis=0)
    sin_tab = jnp.concatenate([jnp.zeros((1, QKV_TB, HD), F32), sin.reshape(nblk, QKV_TB, HD)], axis=0)
    return cos_tab, sin_tab


def kernel(x_prompt, x_sample, state_gla, cache_k, cache_v, c, c_ctx, w_mod, b_mod, norm1_g, norm2_g,
           w_in_even, w_gate_up, b_gate_up, gla_norm_g, w_spatial, b_spatial, w_out_even,
           w_in_odd, q_norm_g, k_norm_g, w_out_odd, w_router_group, b_router_group,
           w_router_expert, b_router_expert, w_exp_gate, w_exp_up, w_exp_down, final_norm_g):
    x_all = None
    cond8 = jnp.concatenate([c_ctx[None], c, jnp.zeros((3, D), F32)], axis=0)
    mod = _modulation(cond8, w_mod, b_mod)
    cos_tab, sin_tab = _rope_tables()
    zero_state = jnp.zeros((1, 2, QK_W, GLA_DV), F32)
    state_in = state_gla.reshape(DEC_BATCH, -1, 2, QK_W, GLA_DV)
    cache_k2 = cache_k.reshape(DEC_BATCH, -1, SEQ, KV_W)
    cache_v2 = cache_v.reshape(DEC_BATCH, -1, SEQ, KV_W)

    gla_states, caches = [], None
    for l in range(DEPTH):
        i = l // 2
        n1g = norm1_g[l][None]
        if l % 2 == 0:
            w = w_in_even[i]
            win = jnp.concatenate([w[:, :1536], w[:, 1568:], w[:, 1536:1568], jnp.zeros((D, 96), F32)],
                                  axis=1).astype(BF16)
            wgu = jnp.zeros((128, 2 * QK_W), F32)
            wgu = wgu.at[0:GLA_RANK, 0:QK_W].set(w_gate_up[i, 0])
            wgu = wgu.at[GLA_RANK:2 * GLA_RANK, QK_W:].set(w_gate_up[i, 1]).astype(BF16)
            bgu = b_gate_up[i].reshape(1, 2 * QK_W)
            args = (mod[l], n1g, win, wgu, bgu, gla_norm_g[i][None], w_spatial[i].astype(BF16),
                    b_spatial[i].T, w_out_even[i].astype(BF16))
            first = l == 0
            x_all, st = _even_mixer(x_all, *args, zero_state, latent=False,
                                    x_first=x_prompt.reshape(N_CTX, D) if first else None)
            gla_states.append(st)
            x_all, _ = _even_mixer(x_all, *args, state_in[:, i], latent=True,
                                   x_first=x_sample.reshape(N_LAT, D) if first else None)
        else:
            q, k, v, *caches = _qkv(x_all, mod[l], n1g, w_in_odd[i].astype(BF16), q_norm_g[i][None],
                                    k_norm_g[i][None], cos_tab, sin_tab, i, caches)
            wout = w_out_odd[i].astype(BF16)
            x_all = _attention(x_all, mod[l], q, k, v, wout)
            x_all = _attention(x_all, mod[l], q, k, v, wout, cache_k2, cache_v2, layer_i=i)
        wr = jnp.concatenate([w_router_expert[l], w_router_group[l],
                              jnp.zeros((D, 128 - N_EXP - MOE_GROUPS), F32)], axis=1)
        br = jnp.concatenate([b_router_expert[l], b_router_group[l],
                              jnp.zeros((128 - N_EXP - MOE_GROUPS,), F32)])[None]
        x_all = _moe(x_all, mod[l], norm2_g[l][None], wr, br, w_exp_gate, w_exp_up, w_exp_down, l,
                     final_norm_g[None] if l == DEPTH - 1 else None)

    y_prompt = x_all[0].reshape(BATCH, SEQ, D)
    y_sample = x_all[1].reshape(DEC_BATCH, DEC_SEQ, D)
    new_state = jnp.stack(gla_states, axis=1).reshape(BATCH, -1, 2, GLA_HEADS, GLA_DK, GLA_DV)
    new_k, new_v = (a.reshape(BATCH, DEPTH // 2, SEQ, ATT_KV, HD) for a in caches)
    return (y_prompt, y_sample, new_state, new_k, new_v)
```

```python
import functools

import jax
import jax.numpy as jnp
import numpy as np
from jax import lax
from jax.experimental import pallas as pl
from jax.experimental.pallas import tpu as pltpu

F32 = jnp.float32
BF16 = jnp.bfloat16

D = 1024
BATCH, SEQ = 16, 256
DEC_BATCH, DEC_SEQ = 4, 1024
N_CTX = BATCH * SEQ
N_LAT = DEC_BATCH * DEC_SEQ
N_TOK = N_CTX + N_LAT
DEPTH = 4
EPS = 1e-6
GRID_W = 64
ROPE_THETA = 10000.0

GLA_HEADS, GLA_DK, GLA_DV, GLA_RANK, GLA_CHUNK, GLA_TAU = 4, 64, 128, 16, 128, 16.0
QK_W = GLA_HEADS * GLA_DK
V_W = GLA_HEADS * GLA_DV
GMLP_GROUPS, GMLP_DIM, GMLP_CHUNK = 4, 128, 128
GMLP_W = GMLP_GROUPS * GMLP_DIM
C_Q, C_K, C_V, C_G, C_U, C_VG, C_A = 0, 256, 512, 1024, 1536, 2048, 2560
EVEN_PACK = 2688

ATT_HEADS, ATT_KV, HD = 8, 2, 128
ATT_G = ATT_HEADS // ATT_KV
Q_W = ATT_HEADS * HD
KV_W = ATT_KV * HD

MOE_GROUPS, MOE_PER_GROUP = 4, 8
N_EXP = MOE_GROUPS * MOE_PER_GROUP
D_EXP = D // 4
NEG = -1e30

MIB = 1024 * 1024
V7X_VMEM_MIB = 64
LANES, SUBLANES = 128, 8


def _cp(*sem, vmem_mib=32):
    assert vmem_mib < V7X_VMEM_MIB
    return pltpu.CompilerParams(dimension_semantics=sem, vmem_limit_bytes=vmem_mib * MIB)


def _dot(a, b):
    return jnp.dot(a.astype(BF16), b.astype(BF16), preferred_element_type=F32)


def _dot_nt(a, b):
    return lax.dot_general(a.astype(BF16), b.astype(BF16), (((1,), (1,)), ((), ())),
                           preferred_element_type=F32)


def _dot_tn(a, b):
    return lax.dot_general(a.astype(BF16), b.astype(BF16), (((0,), (0,)), ((), ())),
                           preferred_element_type=F32)


def _rms(x, g):
    return x * lax.rsqrt(jnp.mean(x * x, axis=-1, keepdims=True) + EPS) * g


def _silu(x):
    return x * jax.nn.sigmoid(x)


def _gelu(x):
    return 0.5 * x * (1.0 + jnp.tanh(np.sqrt(2.0 / np.pi).astype(np.float32) * (x + 0.044715 * (x * x * x))))


def _log_sigmoid(z):
    return jnp.minimum(z, 0.0) - jnp.log(1.0 + jnp.exp(-jnp.abs(z)))


assert D == LANES * SUBLANES


def _rows_to_tiles(ref, x):
    rows = x.shape[0]
    for j in range(SUBLANES):
        ref[pl.ds(j, rows, stride=SUBLANES), :] = x[:, j * LANES:(j + 1) * LANES]


def _tiles_to_rows(ref, rows):
    return jnp.concatenate([ref[pl.ds(j, rows, stride=SUBLANES), :] for j in range(SUBLANES)], axis=1)


def _tile_of(ref, row8):
    return ref.at[pl.ds(pl.multiple_of(row8, SUBLANES), SUBLANES), :]


def _resident(shape, index_map):
    return pl.BlockSpec(shape, index_map, pipeline_mode=pl.Buffered(1))


def _split_bf16(x):
    hi = x.astype(BF16)
    return hi, (x - hi.astype(F32)).astype(BF16)


def _mod_kernel(cond_ref, w_ref, b_ref, o_ref):
    s_hi, s_lo = _split_bf16(_silu(cond_ref[...]))
    w_hi, w_lo = _split_bf16(w_ref[...])
    dot = functools.partial(jnp.dot, preferred_element_type=F32)
    both = dot(jnp.concatenate([s_hi, s_lo], axis=0), w_hi)
    o_ref[...] = both[0:8] + both[8:16] + dot(s_hi, w_lo) + b_ref[...]


def _modulation(cond8, w_mod, b_mod):
    tn = 2048
    out = pl.pallas_call(
        _mod_kernel,
        grid=(DEPTH, 6 * D // tn),
        in_specs=[
            pl.BlockSpec((8, D), lambda l, j: (0, 0)),
            pl.BlockSpec((None, D, tn), lambda l, j: (l, 0, j)),
            pl.BlockSpec((None, 1, tn), lambda l, j: (l, 0, j)),
        ],
        out_specs=pl.BlockSpec((None, 8, tn), lambda l, j: (l, 0, j)),
        out_shape=jax.ShapeDtypeStruct((DEPTH, 8, 6 * D), F32),
        compiler_params=_cp("arbitrary", "arbitrary"),
        name="adaln_mod",
    )(cond8, w_mod, b_mod.reshape(DEPTH, 1, 6 * D))
    return out.reshape(DEPTH, 8, 6, D)


def _even_kernel(x_ref, mod_ref, n1g_ref, win_ref, wgu_ref, bgu_ref, glag_ref, ws_ref, bs_ref,
                 wout_ref, s0_ref, xo_ref, st_ref, proj, la, o_f, o_b, st_scr, *, T):
    n_chunks = T // GLA_CHUNK
    shift, scale, gate = mod_ref[0:1, :], mod_ref[1:2, :], mod_ref[2:3, :]
    RB = 128
    PROJ_RB = 256

    def proj_body(r, carry):
        r0 = pl.multiple_of(r * PROJ_RB, PROJ_RB)
        h = _rms(x_ref[pl.ds(r0, PROJ_RB), :], n1g_ref[...]) * (1.0 + scale) + shift
        p = _dot(h, win_ref[...])
        proj[pl.ds(r0, PROJ_RB), :] = p
        z = _dot(p[:, C_A:C_A + 128], wgu_ref[...]) + bgu_ref[...]
        la[pl.ds(r0, PROJ_RB), :] = _log_sigmoid(z) * (1.0 / GLA_TAU)
        return carry

    lax.fori_loop(0, T // PROJ_RB, proj_body, 0)

    st_scr[0] = s0_ref[0].T
    st_scr[1] = s0_ref[1].T

    ci = lax.broadcasted_iota(jnp.int32, (GLA_CHUNK, GLA_CHUNK), 0)
    cj = lax.broadcasted_iota(jnp.int32, (GLA_CHUNK, GLA_CHUNK), 1)
    tri = (jnp.where(ci >= cj, 1.0, 0.0).astype(BF16), jnp.where(ci <= cj, 1.0, 0.0).astype(BF16))
    ai = lax.broadcasted_iota(jnp.int32, (GLA_HEADS * GLA_CHUNK, GLA_CHUNK), 0) % GLA_CHUNK
    aj = lax.broadcasted_iota(jnp.int32, (GLA_HEADS * GLA_CHUNK, GLA_CHUNK), 1)
    amask = (ai >= aj, ai <= aj)
    lane_head = lax.broadcasted_iota(jnp.int32, (1, QK_W), 1) // GLA_DK
    hmask = [jnp.where(lane_head == h, 1.0, 0.0) for h in range(GLA_HEADS)]

    def chunk_body(i, carry):
        for d in range(2):
            c = i if d == 0 else n_chunks - 1 - i
            r0 = pl.multiple_of(c * GLA_CHUNK, GLA_CHUNK)
            q = proj[pl.ds(r0, GLA_CHUNK), C_Q:C_Q + QK_W] * (GLA_DK ** -0.5)
            k = proj[pl.ds(r0, GLA_CHUNK), C_K:C_K + QK_W]
            v = proj[pl.ds(r0, GLA_CHUNK), C_V:C_V + V_W]
            lac = la[pl.ds(r0, GLA_CHUNK), d * QK_W:(d + 1) * QK_W]
            hi = lac.astype(BF16)
            lo = (lac - hi.astype(F32)).astype(BF16)
            b = (jnp.dot(tri[d], hi, preferred_element_type=F32)
                 + jnp.dot(tri[d], lo, preferred_element_type=F32))
            bend = b[GLA_CHUNK - 1:GLA_CHUNK, :] if d == 0 else b[0:1, :]
            qe = q * jnp.exp(b)
            ke = k * jnp.exp(-b)
            kd = k * jnp.exp(bend - b)
            st = st_scr[d]
            qstack = jnp.concatenate([qe * hmask[h] for h in range(GLA_HEADS)], axis=0).astype(BF16)
            att = jnp.where(amask[d], _dot_nt(qstack, ke), 0.0)
            inter = _dot_nt(qstack, st)
            outs = []
            for h in range(GLA_HEADS):
                rows = slice(h * GLA_CHUNK, (h + 1) * GLA_CHUNK)
                outs.append(_dot(att[rows], v[:, h * GLA_DV:(h + 1) * GLA_DV]) + inter[rows])
            o = jnp.concatenate(outs, axis=1)
            if d == 0:
                o_f[pl.ds(r0, GLA_CHUNK), :] = o
            else:
                o_b[pl.ds(r0, GLA_CHUNK), :] = o
            vstack = jnp.concatenate([v[:, h * GLA_DV:(h + 1) * GLA_DV] for h in range(GLA_HEADS)], axis=0)
            kstack = jnp.concatenate([kd * hmask[h] for h in range(GLA_HEADS)], axis=0)
            st_scr[d] = st * jnp.exp(bend) + _dot_tn(vstack, kstack)
        return carry

    lax.fori_loop(0, n_chunks, chunk_body, 0, unroll=2)
    st_ref[0] = st_scr[0].T
    st_ref[1] = st_scr[1].T

    def out_body(r, carry):
        r0 = pl.multiple_of(r * RB, RB)
        osum = o_f[pl.ds(r0, RB), :] + o_b[pl.ds(r0, RB), :]
        g = proj[pl.ds(r0, RB), C_G:C_G + V_W]
        u = proj[pl.ds(r0, RB), C_U:C_U + GMLP_W]
        vg = _gelu(proj[pl.ds(r0, RB), C_VG:C_VG + GMLP_W])
        parts = []
        for h in range(GLA_HEADS):
            oh = osum[:, h * GLA_DV:(h + 1) * GLA_DV]
            parts.append(_rms(oh, glag_ref[...]) * _silu(g[:, h * GLA_DV:(h + 1) * GLA_DV]))
        for gi in range(GMLP_GROUPS):
            vc = vg[:, gi * GMLP_DIM:(gi + 1) * GMLP_DIM]
            vc = vc - jnp.mean(vc, axis=-1, keepdims=True)
            vn = vc * lax.rsqrt(jnp.mean(vc * vc, axis=-1, keepdims=True) + EPS)
            sg = _dot(ws_ref[gi], vn) + bs_ref[:, gi:gi + 1]
            parts.append(_gelu(u[:, gi * GMLP_DIM:(gi + 1) * GMLP_DIM]) * sg)
        mix = jnp.concatenate(parts, axis=1)
        y = _dot(mix, wout_ref[...])
        xo_ref[pl.ds(r0, RB), :] = x_ref[pl.ds(r0, RB), :] + gate * y
        return carry

    lax.fori_loop(0, T // RB, out_body, 0)


def _even_mixer(x_all, mod_l, n1g, win, wgu, bgu, glag, ws, bs, wout, s0, *, latent, x_first=None):
    if latent:
        T, nseq, blk0 = DEC_SEQ, DEC_BATCH, N_CTX // DEC_SEQ
        cond = lambda i: 1 + i
        s0_spec = pl.BlockSpec((None, 2, QK_W, GLA_DV), lambda i: (i, 0, 0, 0))
    else:
        T, nseq, blk0 = SEQ, BATCH, 0
        cond = lambda i: 0
        s0_spec = pl.BlockSpec((None, 2, QK_W, GLA_DV), lambda i: (0, 0, 0, 0))
    const2 = lambda i: (0, 0)
    body = functools.partial(_even_kernel, T=T)
    x_spec = pl.BlockSpec((T, D), lambda i: (blk0 + i, 0))
    if x_first is None:
        lead_specs, lead_args, aliases = [x_spec], (x_all,), {0: 0}
    elif x_all is None:
        lead_specs, lead_args, aliases = [pl.BlockSpec((T, D), lambda i: (i, 0))], (x_first,), {}
    else:
        lead_specs = [pl.BlockSpec(memory_space=pl.ANY), pl.BlockSpec((T, D), lambda i: (i, 0))]
        lead_args, aliases = (x_all, x_first), {0: 0}
        body = lambda dst_ref, *refs: _even_kernel(*refs, T=T)
    x_new, states = pl.pallas_call(
        body,
        grid=(nseq,),
        in_specs=lead_specs + [
            pl.BlockSpec((None, 6, D), lambda i: (cond(i), 0, 0)),
            _resident((1, D), const2),
            _resident((D, EVEN_PACK), const2),
            _resident((128, 2 * QK_W), const2),
            _resident((1, 2 * QK_W), const2),
            _resident((1, GLA_DV), const2),
            _resident((GMLP_GROUPS, GMLP_CHUNK, GMLP_CHUNK), lambda i: (0, 0, 0)),
            _resident((GMLP_CHUNK, GMLP_GROUPS), const2),
            _resident((D, D), const2),
            s0_spec,
        ],
        out_specs=[
            x_spec,
            pl.BlockSpec((None, 2, QK_W, GLA_DV), lambda i: (i, 0, 0, 0)),
        ],
        out_shape=[
            jax.ShapeDtypeStruct((N_TOK, D), F32),
            jax.ShapeDtypeStruct((nseq, 2, QK_W, GLA_DV), F32),
        ],
        scratch_shapes=[
            pltpu.VMEM((T, EVEN_PACK), F32),
            pltpu.VMEM((T, 2 * QK_W), F32),
            pltpu.VMEM((T, V_W), F32),
            pltpu.VMEM((T, V_W), F32),
            pltpu.VMEM((2, GLA_DV, QK_W), F32),
        ],
        input_output_aliases=aliases,
        compiler_params=_cp("arbitrary", vmem_mib=48 if latent else 32),
        name="even_mixer_latent" if latent else "even_mixer_context",
    )(*lead_args, mod_l, n1g, win, wgu, bgu, glag, ws, bs, wout, s0)
    return x_new, states


QKV_TB = 512


def _qkv_kernel(x_ref, mod_ref, n1g_ref, win_ref, gq_ref, gk_ref, cos_ref, sin_ref, q_ref, k_ref, v_ref,
                ck_ref, cv_ref):
    shift, scale = mod_ref[0:1, :], mod_ref[1:2, :]
    h = _rms(x_ref[...], n1g_ref[...]) * (1.0 + scale) + shift
    p = _dot(h, win_ref[...])
    cos, sin = cos_ref[...], sin_ref[...]
    even_lane = lax.broadcasted_iota(jnp.int32, (1, HD), 1) % 2 == 0

    def rope(xn):
        swapped = jnp.where(even_lane, pltpu.roll(xn, HD - 1, axis=1), pltpu.roll(xn, 1, axis=1))
        return xn * cos + swapped * sin

    def emit(rotate, to_cache):
        for hh in range(ATT_HEADS):
            qn = _rms(p[:, hh * HD:(hh + 1) * HD], gq_ref[...])
            q_ref[:, hh * HD:(hh + 1) * HD] = (rotate(qn) * (HD ** -0.5)).astype(BF16)
        for hh in range(ATT_KV):
            kn = rotate(_rms(p[:, Q_W + hh * HD:Q_W + (hh + 1) * HD], gk_ref[...]))
            k_ref[:, hh * HD:(hh + 1) * HD] = kn
            if to_cache:
                for s in range(QKV_TB // SEQ):
                    ck_ref[s, pl.ds(hh, SEQ, stride=ATT_KV), :] = kn[s * SEQ:(s + 1) * SEQ]
                    cv_ref[s, pl.ds(hh, SEQ, stride=ATT_KV), :] = p[s * SEQ:(s + 1) * SEQ,
                                                                    Q_W + KV_W + hh * HD:Q_W + KV_W + (hh + 1) * HD]

    is_latent = pl.program_id(0) >= N_CTX // QKV_TB

    @pl.when(is_latent)
    def _():
        emit(rope, False)

    @pl.when(jnp.logical_not(is_latent))
    def _():
        emit(lambda xn: xn, True)

    v_ref[...] = p[:, Q_W + KV_W:]


def _qkv(x_all, mod_l, n1g, win, gq, gk, cos_tab, sin_tab, layer_i, caches=None):
    nb_ctx = N_CTX // QKV_TB
    per_seq = DEC_SEQ // QKV_TB
    cond = lambda i: jnp.where(i < nb_ctx, 0, 1 + (i - nb_ctx) // per_seq)
    tab = lambda i: jnp.where(i < nb_ctx, 0, 1 + (i - nb_ctx) % per_seq)
    const2 = lambda i: (0, 0)
    cache_spec = pl.BlockSpec((QKV_TB // SEQ, None, SEQ * ATT_KV, HD),
                              lambda i: (jnp.minimum(i, nb_ctx - 1), layer_i, 0, 0))
    cache_shape = jax.ShapeDtypeStruct((BATCH, DEPTH // 2, SEQ * ATT_KV, HD), F32)
    if caches is None:
        body, lead_specs, lead_args, aliases = _qkv_kernel, [], (), {}
    else:
        body = lambda ck_in, cv_in, *refs: _qkv_kernel(*refs)
        lead_specs = [pl.BlockSpec(memory_space=pl.ANY)] * 2
        lead_args, aliases = tuple(caches), {0: 3, 1: 4}
    return pl.pallas_call(
        body,
        grid=(N_TOK // QKV_TB,),
        in_specs=lead_specs + [
            pl.BlockSpec((QKV_TB, D), lambda i: (i, 0)),
            pl.BlockSpec((None, 6, D), lambda i: (cond(i), 0, 0)),
            _resident((1, D), const2),
            _resident((D, Q_W + 2 * KV_W), const2),
            _resident((1, HD), const2),
            _resident((1, HD), const2),
            pl.BlockSpec((None, QKV_TB, HD), lambda i: (tab(i), 0, 0)),
            pl.BlockSpec((None, QKV_TB, HD), lambda i: (tab(i), 0, 0)),
        ],
        out_specs=[
            pl.BlockSpec((QKV_TB, Q_W), lambda i: (i, 0)),
            pl.BlockSpec((QKV_TB, KV_W), lambda i: (i, 0)),
            pl.BlockSpec((QKV_TB, KV_W), lambda i: (i, 0)),
            cache_spec,
            cache_spec,
        ],
        out_shape=[
            jax.ShapeDtypeStruct((N_TOK, Q_W), BF16),
            jax.ShapeDtypeStruct((N_TOK, KV_W), F32),
            jax.ShapeDtypeStruct((N_TOK, KV_W), F32),
            cache_shape,
            cache_shape,
        ],
        input_output_aliases=aliases,
        compiler_params=_cp("arbitrary"),
        name="odd_qkv",
    )(*lead_args, x_all, mod_l, n1g, win, gq, gk, cos_tab, sin_tab)


ATT_TQ_LATENT = 512


def _attn_kernel(*refs, n_kv):
    q_ref = refs[0]
    kv_refs = refs[1:1 + 2 * n_kv]
    x_ref, mod_ref, wout_ref, xo_ref, att_scr = refs[1 + 2 * n_kv:]
    gate = mod_ref[2:3, :]
    for kh in range(ATT_KV):
        ks = [kv_refs[2 * s][:, kh * HD:(kh + 1) * HD].astype(BF16) for s in range(n_kv)]
        vs = [jnp.concatenate([kv_refs[2 * s + 1][:, kh * HD:(kh + 1) * HD].astype(BF16),
                               jnp.ones((kv_refs[2 * s + 1].shape[0], HD), BF16)], axis=1) for s in range(n_kv)]
        for g in range(ATT_G):
            hh = kh * ATT_G + g
            qh = q_ref[:, hh * HD:(hh + 1) * HD]
            ss = [_dot_nt(qh, kk) for kk in ks]
            m = ss[0].max(axis=-1, keepdims=True)
            for s in ss[1:]:
                m = jnp.maximum(m, s.max(axis=-1, keepdims=True))
            o = _dot(jnp.exp(ss[0] - m), vs[0])
            for s, vv in zip(ss[1:], vs[1:]):
                o = o + _dot(jnp.exp(s - m), vv)
            att_scr[:, hh * HD:(hh + 1) * HD] = o[:, :HD] / o[:, HD:HD + 1]
    y = _dot(att_scr[...], wout_ref[...])
    xo_ref[...] = x_ref[...] + gate * y


def _attention(x_all, mod_l, q, k, v, wout, cache_k=None, cache_v=None, layer_i=0):
    latent = cache_k is not None
    const2 = lambda *a: (0, 0)
    tq = ATT_TQ_LATENT if latent else SEQ
    if latent:
        nq = DEC_SEQ // tq
        row_blk = lambda b, j: (N_CTX // tq + b * nq + j, 0)
        grid = (DEC_BATCH, nq)
        kv_specs = [
            pl.BlockSpec((None, None, SEQ, KV_W), lambda b, j: (b, layer_i, 0, 0)),
            pl.BlockSpec((None, None, SEQ, KV_W), lambda b, j: (b, layer_i, 0, 0)),
            pl.BlockSpec((DEC_SEQ, KV_W), lambda b, j: (N_CTX // DEC_SEQ + b, 0)),
            pl.BlockSpec((DEC_SEQ, KV_W), lambda b, j: (N_CTX // DEC_SEQ + b, 0)),
        ]
        kv_args = (cache_k, cache_v, k, v)
        mod_spec = pl.BlockSpec((None, 6, D), lambda b, j: (1 + b, 0, 0))
        sem = ("arbitrary", "arbitrary")
        n_kv = 2
    else:
        row_blk = lambda i: (i, 0)
        grid = (BATCH,)
        kv_specs = [pl.BlockSpec((SEQ, KV_W), row_blk), pl.BlockSpec((SEQ, KV_W), row_blk)]
        kv_args = (k, v)
        mod_spec = pl.BlockSpec((None, 6, D), lambda i: (0, 0, 0))
        sem = ("arbitrary",)
        n_kv = 1
    n_in = 1 + len(kv_args)
    return pl.pallas_call(
        functools.partial(_attn_kernel, n_kv=n_kv),
        grid=grid,
        in_specs=[pl.BlockSpec((tq, Q_W), row_blk)] + kv_specs + [
            pl.BlockSpec((tq, D), row_blk),
            mod_spec,
            _resident((D, D), const2),
        ],
        out_specs=pl.BlockSpec((tq, D), row_blk),
        out_shape=jax.ShapeDtypeStruct((N_TOK, D), F32),
        scratch_shapes=[pltpu.VMEM((tq, Q_W), F32)],
        input_output_aliases={n_in: 0},
        compiler_params=_cp(*sem),
        name="attention_latent" if latent else "attention_context",
    )(q, *kv_args, x_all, mod_l, wout)


ROUTE_TB = 512
HALF_TOK = N_TOK // 2
M_E1, M_E2, M_G1, M_G2, M_R1, M_R2 = 0, 1, 2, 3, 4, 5


def _router_kernel(x_ref, mod_ref, n2g_ref, w2_ref, br_ref, h_ref, metat_ref, cnt_ref, run):
    @pl.when(pl.program_id(0) % (HALF_TOK // ROUTE_TB) == 0)
    def _():
        run[...] = jnp.zeros_like(run)

    shift, scale = mod_ref[3:4, :], mod_ref[4:5, :]
    h = _rms(x_ref[...], n2g_ref[...]) * (1.0 + scale) + shift
    _rows_to_tiles(h_ref, h)
    h_hi, h_lo = _split_bf16(h)
    dot = functools.partial(jnp.dot, preferred_element_type=F32)
    wide = dot(h_hi, w2_ref[...])
    logits = wide[:, :128] + wide[:, 128:] + dot(h_lo, w2_ref[:, :128]) + br_ref[...]
    lane = lax.broadcasted_iota(jnp.int32, logits.shape, 1).astype(F32)
    big = 1e4

    def first_argmax(vals):
        m = vals.max(axis=-1, keepdims=True)
        return m, jnp.where(vals == m, lane, big).min(axis=-1, keepdims=True)

    gl = jnp.where((lane >= N_EXP) & (lane < N_EXP + MOE_GROUPS), logits, NEG)
    gmax, glane = first_argmax(gl)
    g_p = 1.0 / jnp.exp(gl - gmax).sum(axis=-1, keepdims=True)
    lo = (glane - N_EXP) * MOE_PER_GROUP
    el = jnp.where((lane >= lo) & (lane < lo + MOE_PER_GROUP), logits, NEG)
    m1, i1 = first_argmax(el)
    m2, i2 = first_argmax(jnp.where(lane == i1, NEG, el))
    t = jnp.exp(m2 - m1)
    w1 = 1.0 / (1.0 + t)
    sel1, sel2 = lane == i1, lane == i2
    onehot = jnp.where(sel1 | sel2, 1.0, 0.0)
    ri = lax.broadcasted_iota(jnp.int32, (ROUTE_TB, ROUTE_TB), 0)
    rj = lax.broadcasted_iota(jnp.int32, (ROUTE_TB, ROUTE_TB), 1)
    before = _dot(jnp.where(ri > rj, 1.0, 0.0), onehot) + run[...]
    r1 = jnp.where(sel1, before, 0.0).sum(axis=-1, keepdims=True)
    r2 = jnp.where(sel2, before, 0.0).sum(axis=-1, keepdims=True)
    run[...] += onehot.sum(axis=0, keepdims=True)
    cnt_ref[...] = run[...]
    meta = jnp.zeros_like(logits)
    for j, val in enumerate([i1, i2, w1 * g_p, (t * w1) * g_p, r1, r2]):
        meta = jnp.where(lane == j, val, meta)
    metat_ref[...] = meta.T[0:8, :]


def _router(x_all, mod_l, n2g, wr, br):
    w2 = jnp.concatenate(_split_bf16(wr), axis=1)
    nb_ctx = N_CTX // ROUTE_TB
    per_seq = DEC_SEQ // ROUTE_TB
    cond = lambda i: jnp.where(i < nb_ctx, 0, 1 + (i - nb_ctx) // per_seq)
    const2 = lambda i: (0, 0)
    return pl.pallas_call(
        _router_kernel,
        grid=(N_TOK // ROUTE_TB,),
        in_specs=[
            pl.BlockSpec((ROUTE_TB, D), lambda i: (i, 0)),
            pl.BlockSpec((None, 6, D), lambda i: (cond(i), 0, 0)),
            _resident((1, D), const2),
            _resident((D, 256), const2),
            _resident((1, 128), const2),
        ],
        out_specs=[
            pl.BlockSpec((ROUTE_TB * 8, 128), lambda i: (i, 0)),
            pl.BlockSpec((8, ROUTE_TB), lambda i: (0, i)),
            pl.BlockSpec((None, 1, 128), lambda i: (i // (HALF_TOK // ROUTE_TB), 0, 0)),
        ],
        out_shape=[
            jax.ShapeDtypeStruct((N_TOK * 8, 128), F32),
            jax.ShapeDtypeStruct((8, N_TOK), F32),
            jax.ShapeDtypeStruct((2, 1, 128), F32),
        ],
        scratch_shapes=[pltpu.VMEM((1, 128), F32)],
        compiler_params=_cp("arbitrary"),
        name="moe_router",
    )(x_all, mod_l, n2g, w2, br)


EXP_TM = 128
N_ASSIGN = 2 * N_TOK
N_GROUPS = 2 * N_EXP
MAX_TILES = N_ASSIGN // EXP_TM + N_GROUPS
N_SORTED = MAX_TILES * EXP_TM
ORDER_BLK = 2048
CODE_PLANE = 2 * HALF_TOK
CODE_MASK = 8 * CODE_PLANE - 1
DUMMY8 = HALF_TOK * 8


def _order_kernel(pos1_ref, pos2_ref, pad_lo_ref, pad_hi_ref, src_ref):
    i = pl.program_id(0)
    local = (i % (HALF_TOK // ORDER_BLK)) * ORDER_BLK

    def body(t, carry):
        src_ref[pos1_ref[t]] = (local + t) * 8
        src_ref[pos2_ref[t]] = (local + t + CODE_PLANE) * 8
        return carry

    lax.fori_loop(0, ORDER_BLK, body, 0, unroll=16)

    @pl.when(i == 0)
    def _():
        def group(g, carry):
            def pad(p, c):
                src_ref[p] = DUMMY8
                return c
            return lax.fori_loop(pad_lo_ref[g], pad_hi_ref[g], pad, carry)

        lax.fori_loop(0, N_GROUPS, group, 0)


def _order(pos, pad_lo, pad_hi):
    return pl.pallas_call(
        _order_kernel,
        grid=(N_TOK // ORDER_BLK,),
        in_specs=[
            pl.BlockSpec((ORDER_BLK,), lambda i: (i,), memory_space=pltpu.SMEM),
            pl.BlockSpec((ORDER_BLK,), lambda i: (N_TOK // ORDER_BLK + i,), memory_space=pltpu.SMEM),
            pl.BlockSpec(memory_space=pltpu.SMEM),
            pl.BlockSpec(memory_space=pltpu.SMEM),
        ],
        out_specs=pl.BlockSpec(memory_space=pltpu.SMEM),
        out_shape=jax.ShapeDtypeStruct((N_SORTED,), jnp.int32),
        compiler_params=_cp("arbitrary"),
        name="moe_order",
    )(pos, pos, pad_lo, pad_hi)


GATE_BLK = CODE_PLANE + HALF_TOK
ACC_TOK = HALF_TOK + 64
GATHER_GROUP, ACC_GROUP = 16, 8


RES_TB = 256


def _experts_kernel(tile0_ref, ntile_ref, count_ref, src_ref, gs_ref, h_hbm, x_hbm, mod_ref, wg_ref, wu_ref, wd_ref,
                    *rest, final):
    if final:
        fg_ref, *dst_hbm = rest[:3]
        rest = rest[3:]
    else:
        dst_hbm, rest = rest[:1], rest[1:]
    h_res, acc, xbuf, ybuf, wgb, wub, wdb, xin, xout, sem, in_sem, out_sem = rest
    group = pl.program_id(0)
    expert = group % N_EXP
    half = group // N_EXP
    rows0 = pl.multiple_of(half * (HALF_TOK * 8), 8)

    @pl.when(expert == 0)
    def _():
        cp = pltpu.make_async_copy(h_hbm.at[pl.ds(rows0, HALF_TOK * 8), :], h_res.at[pl.ds(0, HALF_TOK * 8), :], sem)
        cp.start()
        h_res[pl.ds(DUMMY8, 8), :] = jnp.zeros((8, 128), F32)
        xbuf[...] = jnp.zeros_like(xbuf)

        def zero(i, carry):
            acc[pl.ds(pl.multiple_of(i * 512, 512), 512), :] = jnp.zeros((512, 128), F32)
            return carry

        lax.fori_loop(0, ACC_TOK * 8 // 512, zero, 0)
        cp.wait()

    n_tiles = ntile_ref[group]

    @pl.when(n_tiles > 0)
    def _():
        wgb[...] = wg_ref[...].astype(BF16)
        wub[...] = wu_ref[...].astype(BF16)
        wdb[...] = wd_ref[...].astype(BF16)

    row0 = tile0_ref[group] * EXP_TM
    row_end = row0 + count_ref[group]

    def process(base, rows):
        live = (jnp.clip(row_end - base, 0, rows) + GATHER_GROUP - 1) // GATHER_GROUP

        def gather(g, c):
            for i in range(GATHER_GROUP):
                r = g * GATHER_GROUP + i
                xbuf[pl.ds(pl.multiple_of(r * 8, 8), 8), :] = _tile_of(h_res, src_ref[base + r] & CODE_MASK)[...]
            return c

        lax.fori_loop(0, live, gather, 0)
        x = _tiles_to_rows(xbuf, rows).astype(BF16)
        hid = _silu(_dot(x, wgb[...])) * _dot(x, wub[...])
        _rows_to_tiles(ybuf, _dot(hid, wdb[...]))

        def accumulate(g, c):
            targets, values = [], []
            for i in range(ACC_GROUP):
                r = g * ACC_GROUP + i
                code = src_ref[base + r]
                target = _tile_of(acc, code & CODE_MASK)
                targets.append(target)
                values.append(target[...] + gs_ref[code >> 3] * ybuf[pl.ds(pl.multiple_of(r * 8, 8), 8), :])
            for target, value in zip(targets, values):
                target[...] = value
            return c

        lax.fori_loop(0, live * (GATHER_GROUP // ACC_GROUP), accumulate, 0)


    def pair_body(j, carry):
        process(row0 + j * (2 * EXP_TM), 2 * EXP_TM)
        return carry

    lax.fori_loop(0, n_tiles // 2, pair_body, 0)

    @pl.when(n_tiles % 2 == 1)
    def _():
        process(row0 + (n_tiles - 1) * EXP_TM, EXP_TM)

    def rows_of(first, blk):
        return pl.ds(pl.multiple_of(first + blk * RES_TB, RES_TB), RES_TB)

    def load_x(blk, slot):
        return pltpu.make_async_copy(x_hbm.at[rows_of(half * HALF_TOK, blk), :], xin.at[slot], in_sem.at[slot])

    def residual(blk, slot):
        cond = jnp.where(half == 0, 0, 1 + blk // (DEC_SEQ // RES_TB))
        gate = mod_ref[cond, 5:6, :]
        y = _tiles_to_rows(acc.at[pl.ds(pl.multiple_of(blk * (RES_TB * 8), RES_TB * 8), RES_TB * 8), :], RES_TB)
        x_new = xin[slot] + gate * y
        xout[slot] = _rms(x_new, fg_ref[...]) if final else x_new

    def epilogue(dst, first_row):
        def store_x(blk, slot):
            return pltpu.make_async_copy(xout.at[slot], dst.at[rows_of(first_row, blk), :], out_sem.at[slot])

        n_pairs = HALF_TOK // RES_TB // 2
        load_x(0, 0).start()

        def pair(p, carry):
            for slot in range(2):
                blk = 2 * p + slot
                if slot == 0:
                    load_x(blk + 1, 1).start()
                else:
                    @pl.when(p + 1 < n_pairs)
                    def _():
                        load_x(blk + 1, 0).start()
                load_x(blk, slot).wait()

                @pl.when(p > 0)
                def _():
                    store_x(blk - 2, slot).wait()

                residual(blk, slot)
                store_x(blk, slot).start()
            return carry

        lax.fori_loop(0, n_pairs, pair, 0)
        store_x(2 * n_pairs - 2, 0).wait()
        store_x(2 * n_pairs - 1, 1).wait()

    if final:
        for which in range(2):
            @pl.when((expert == N_EXP - 1) & (half == which))
            def _():
                epilogue(dst_hbm[which], 0)
    else:
        @pl.when(expert == N_EXP - 1)
        def _():
            epilogue(dst_hbm[0], half * HALF_TOK)


def _experts(tile0, n_tiles, counts, src, gs, h, x_all, mod_l, wg, wu, wd, layer, final_g=None):
    final = final_g is not None
    wmap = lambda g, t0, nt, cnt, src: (layer, g % N_EXP, 0, 0)
    any_spec = pl.BlockSpec(memory_space=pl.ANY)
    extra_specs = [pl.BlockSpec((1, D), lambda g, t0, nt, cnt, src: (0, 0))] if final else []
    extra_args = (final_g,) if final else ()
    return pl.pallas_call(
        functools.partial(_experts_kernel, final=final),
        grid_spec=pltpu.PrefetchScalarGridSpec(
            num_scalar_prefetch=4,
            grid=(N_GROUPS,),
            in_specs=[
                pl.BlockSpec((GATE_BLK,), lambda g, t0, nt, cnt, src: (g // N_EXP,), memory_space=pltpu.SMEM),
                pl.BlockSpec(memory_space=pl.ANY),
                pl.BlockSpec(memory_space=pl.ANY),
                pl.BlockSpec((8, 6, D), lambda g, t0, nt, cnt, src: (0, 0, 0)),
                pl.BlockSpec((None, None, D, D_EXP), wmap),
                pl.BlockSpec((None, None, D, D_EXP), wmap),
                pl.BlockSpec((None, None, D_EXP, D), wmap),
            ] + extra_specs,
            out_specs=[any_spec, any_spec] if final else any_spec,
            scratch_shapes=[
                pltpu.VMEM((ACC_TOK * 8, 128), F32),
                pltpu.VMEM((ACC_TOK * 8, 128), F32),
                pltpu.VMEM((2 * EXP_TM * 8, 128), F32),
                pltpu.VMEM((2 * EXP_TM * 8, 128), F32),
                pltpu.VMEM((D, D_EXP), BF16),
                pltpu.VMEM((D, D_EXP), BF16),
                pltpu.VMEM((D_EXP, D), BF16),
                pltpu.VMEM((2, RES_TB, D), F32),
                pltpu.VMEM((2, RES_TB, D), F32),
                pltpu.SemaphoreType.DMA,
                pltpu.SemaphoreType.DMA((2,)),
                pltpu.SemaphoreType.DMA((2,)),
            ],
        ),
        out_shape=([jax.ShapeDtypeStruct((HALF_TOK, D), F32)] * 2 if final
                   else jax.ShapeDtypeStruct((N_TOK, D), F32)),
        input_output_aliases={} if final else {6: 0},
        compiler_params=_cp("arbitrary", vmem_mib=56),
        name="moe_experts_final" if final else "moe_experts",
    )(tile0, n_tiles, counts, src, gs, h, x_all, mod_l, wg, wu, wd, *extra_args)


def _moe(x_all, mod_l, n2g, wr, br, wg, wu, wd, layer, final_g=None):
    h, metat, cnt = _router(x_all, mod_l, n2g, wr, br)
    counts = cnt[:, 0, :N_EXP].astype(jnp.int32).reshape(N_GROUPS)
    padded = (counts + EXP_TM - 1) // EXP_TM * EXP_TM
    ends = jnp.cumsum(padded)
    offs = ends - padded
    rec = metat.astype(jnp.int32)
    half = (jnp.arange(N_TOK, dtype=jnp.int32) // HALF_TOK)[None, :]
    group = rec[M_E1:M_E2 + 1] + N_EXP * half
    is_group = group[None] == jnp.arange(N_GROUPS, dtype=jnp.int32)[:, None, None]
    pos = jnp.sum(jnp.where(is_group, offs[:, None, None], 0), axis=0) + rec[M_R1:M_R2 + 1]
    live_end = offs + (counts + GATHER_GROUP - 1) // GATHER_GROUP * GATHER_GROUP
    src = _order(pos.reshape(N_ASSIGN), offs + counts, live_end)
    g12 = metat[M_G1:M_G2 + 1].reshape(2, 2, HALF_TOK)
    gates = jnp.concatenate([g12[0], jnp.zeros((2, CODE_PLANE - HALF_TOK), F32), g12[1]], axis=1)
    return _experts(offs // EXP_TM, padded // EXP_TM, counts, src, gates.reshape(2 * GATE_BLK), h, x_all, mod_l,
                    wg, wu, wd, layer, final_g)


def _rope_tables():
    pos = jnp.arange(DEC_SEQ)
    row = (pos // GRID_W).astype(F32)
    col = (pos % GRID_W).astype(F32)
    n_freq = HD // 4
    inv = ROPE_THETA ** (-jnp.arange(n_freq, dtype=F32) / n_freq)
    ang = jnp.concatenate([row[:, None] * inv, col[:, None] * inv], axis=-1)
    cos = jnp.repeat(jnp.cos(ang), 2, axis=-1)
    sin = jnp.repeat(jnp.sin(ang), 2, axis=-1) * jnp.tile(jnp.array([-1.0, 1.0], F32), HD // 2)
    nblk = DEC_SEQ // QKV_TB
    cos_tab = jnp.concatenate([jnp.ones((1, QKV_TB, HD), F32), cos.reshape(nblk, QKV_TB, HD)], axis=0)
    sin_tab = jnp.concatenate([jnp.zeros((1, QKV_TB, HD), F32), sin.reshape(nblk, QKV_TB, HD)], axis=0)
    return cos_tab, sin_tab


def kernel(x_prompt, x_sample, state_gla, cache_k, cache_v, c, c_ctx, w_mod, b_mod, norm1_g, norm2_g,
           w_in_even, w_gate_up, b_gate_up, gla_norm_g, w_spatial, b_spatial, w_out_even,
           w_in_odd, q_norm_g, k_norm_g, w_out_odd, w_router_group, b_router_group,
           w_router_expert, b_router_expert, w_exp_gate, w_exp_up, w_exp_down, final_norm_g):
    x_all = None
    cond8 = jnp.concatenate([c_ctx[None], c, jnp.zeros((3, D), F32)], axis=0)
    mod = _modulation(cond8, w_mod, b_mod)
    cos_tab, sin_tab = _rope_tables()
    zero_state = jnp.zeros((1, 2, QK_W, GLA_DV), F32)
    state_in = state_gla.reshape(DEC_BATCH, -1, 2, QK_W, GLA_DV)
    cache_k2 = cache_k.reshape(DEC_BATCH, -1, SEQ, KV_W)
    cache_v2 = cache_v.reshape(DEC_BATCH, -1, SEQ, KV_W)

    gla_states, caches = [], None
    for l in range(DEPTH):
        i = l // 2
        n1g = norm1_g[l][None]
        if l % 2 == 0:
            w = w_in_even[i]
            win = jnp.concatenate([w[:, :1536], w[:, 1568:], w[:, 1536:1568], jnp.zeros((D, 96), F32)],
                                  axis=1).astype(BF16)
            wgu = jnp.zeros((128, 2 * QK_W), F32)
            wgu = wgu.at[0:GLA_RANK, 0:QK_W].set(w_gate_up[i, 0])
            wgu = wgu.at[GLA_RANK:2 * GLA_RANK, QK_W:].set(w_gate_up[i, 1]).astype(BF16)
            bgu = b_gate_up[i].reshape(1, 2 * QK_W)
            args = (mod[l], n1g, win, wgu, bgu, gla_norm_g[i][None], w_spatial[i].astype(BF16),
                    b_spatial[i].T, w_out_even[i].astype(BF16))
            first = l == 0
            x_all, st = _even_mixer(x_all, *args, zero_state, latent=False,
                                    x_first=x_prompt.reshape(N_CTX, D) if first else None)
            gla_states.append(st)
            x_all, _ = _even_mixer(x_all, *args, state_in[:, i], latent=True,
                                   x_first=x_sample.reshape(N_LAT, D) if first else None)
        else:
            q, k, v, *caches = _qkv(x_all, mod[l], n1g, w_in_odd[i].astype(BF16), q_norm_g[i][None],
                                    k_norm_g[i][None], cos_tab, sin_tab, i, caches)
            wout = w_out_odd[i].astype(BF16)
            x_all = _attention(x_all, mod[l], q, k, v, wout)
            x_all = _attention(x_all, mod[l], q, k, v, wout, cache_k2, cache_v2, layer_i=i)
        wr = jnp.concatenate([w_router_expert[l], w_router_group[l],
                              jnp.zeros((D, 128 - N_EXP - MOE_GROUPS), F32)], axis=1)
        br = jnp.concatenate([b_router_expert[l], b_router_group[l],
                              jnp.zeros((128 - N_EXP - MOE_GROUPS,), F32)])[None]
        x_all = _moe(x_all, mod[l], norm2_g[l][None], wr, br, w_exp_gate, w_exp_up, w_exp_down, l,
                     final_norm_g[None] if l == DEPTH - 1 else None)

    y_prompt = x_all[0].reshape(BATCH, SEQ, D)
    y_sample = x_all[1].reshape(DEC_BATCH, DEC_SEQ, D)
    new_state = jnp.stack(gla_states, axis=1).reshape(BATCH, -1, 2, GLA_HEADS, GLA_DK, GLA_DV)
    new_k, new_v = (a.reshape(BATCH, DEPTH // 2, SEQ, ATT_KV, HD) for a in caches)
    return (y_prompt, y_sample, new_state, new_k, new_v)
```

```python
import functools

import jax
import jax.numpy as jnp
import numpy as np
from jax import lax
from jax.experimental import pallas as pl
from jax.experimental.pallas import tpu as pltpu

F32 = jnp.float32
BF16 = jnp.bfloat16

D = 1024
BATCH, SEQ = 16, 256
DEC_BATCH, DEC_SEQ = 4, 1024
N_CTX = BATCH * SEQ
N_LAT = DEC_BATCH * DEC_SEQ
N_TOK = N_CTX + N_LAT
DEPTH = 4
EPS = 1e-6
GRID_W = 64
ROPE_THETA = 10000.0

GLA_HEADS, GLA_DK, GLA_DV, GLA_RANK, GLA_CHUNK, GLA_TAU = 4, 64, 128, 16, 128, 16.0
QK_W = GLA_HEADS * GLA_DK
V_W = GLA_HEADS * GLA_DV
GMLP_GROUPS, GMLP_DIM, GMLP_CHUNK = 4, 128, 128
GMLP_W = GMLP_GROUPS * GMLP_DIM
C_Q, C_K, C_V, C_G, C_U, C_VG, C_A = 0, 256, 512, 1024, 1536, 2048, 2560
EVEN_PACK = 2688

ATT_HEADS, ATT_KV, HD = 8, 2, 128
ATT_G = ATT_HEADS // ATT_KV
Q_W = ATT_HEADS * HD
KV_W = ATT_KV * HD

MOE_GROUPS, MOE_PER_GROUP = 4, 8
N_EXP = MOE_GROUPS * MOE_PER_GROUP
D_EXP = D // 4
NEG = -1e30

MIB = 1024 * 1024
V7X_VMEM_MIB = 64
LANES, SUBLANES = 128, 8


def _cp(*sem, vmem_mib=32):
    assert vmem_mib < V7X_VMEM_MIB
    return pltpu.CompilerParams(dimension_semantics=sem, vmem_limit_bytes=vmem_mib * MIB)


def _dot(a, b):
    return jnp.dot(a.astype(BF16), b.astype(BF16), preferred_element_type=F32)


def _dot_nt(a, b):
    return lax.dot_general(a.astype(BF16), b.astype(BF16), (((1,), (1,)), ((), ())),
                           preferred_element_type=F32)


def _dot_tn(a, b):
    return lax.dot_general(a.astype(BF16), b.astype(BF16), (((0,), (0,)), ((), ())),
                           preferred_element_type=F32)


def _rms(x, g):
    return x * lax.rsqrt(jnp.mean(x * x, axis=-1, keepdims=True) + EPS) * g


def _silu(x):
    return x * jax.nn.sigmoid(x)


def _gelu(x):
    return 0.5 * x * (1.0 + jnp.tanh(np.sqrt(2.0 / np.pi).astype(np.float32) * (x + 0.044715 * (x * x * x))))


def _log_sigmoid(z):
    return jnp.minimum(z, 0.0) - jnp.log(1.0 + jnp.exp(-jnp.abs(z)))


assert D == LANES * SUBLANES


def _rows_to_tiles(ref, x):
    rows = x.shape[0]
    for j in range(SUBLANES):
        ref[pl.ds(j, rows, stride=SUBLANES), :] = x[:, j * LANES:(j + 1) * LANES]


def _tiles_to_rows(ref, rows):
    return jnp.concatenate([ref[pl.ds(j, rows, stride=SUBLANES), :] for j in range(SUBLANES)], axis=1)


def _tile_of(ref, row8):
    return ref.at[pl.ds(pl.multiple_of(row8, SUBLANES), SUBLANES), :]


def _resident(shape, index_map):
    return pl.BlockSpec(shape, index_map, pipeline_mode=pl.Buffered(1))


def _split_bf16(x):
    hi = x.astype(BF16)
    return hi, (x - hi.astype(F32)).astype(BF16)


def _mod_kernel(cond_ref, w_ref, b_ref, o_ref):
    s_hi, s_lo = _split_bf16(_silu(cond_ref[...]))
    w_hi, w_lo = _split_bf16(w_ref[...])
    dot = functools.partial(jnp.dot, preferred_element_type=F32)
    both = dot(jnp.concatenate([s_hi, s_lo], axis=0), w_hi)
    o_ref[...] = both[0:8] + both[8:16] + dot(s_hi, w_lo) + b_ref[...]


def _modulation(cond8, w_mod, b_mod):
    tn = 2048
    out = pl.pallas_call(
        _mod_kernel,
        grid=(DEPTH, 6 * D // tn),
        in_specs=[
            pl.BlockSpec((8, D), lambda l, j: (0, 0)),
            pl.BlockSpec((None, D, tn), lambda l, j: (l, 0, j)),
            pl.BlockSpec((None, 1, tn), lambda l, j: (l, 0, j)),
        ],
        out_specs=pl.BlockSpec((None, 8, tn), lambda l, j: (l, 0, j)),
        out_shape=jax.ShapeDtypeStruct((DEPTH, 8, 6 * D), F32),
        compiler_params=_cp("arbitrary", "arbitrary"),
        name="adaln_mod",
    )(cond8, w_mod, b_mod.reshape(DEPTH, 1, 6 * D))
    return out.reshape(DEPTH, 8, 6, D)


def _even_kernel(x_ref, mod_ref, n1g_ref, win_ref, wgu_ref, bgu_ref, glag_ref, ws_ref, bs_ref,
                 wout_ref, s0_ref, xo_ref, st_ref, proj, la, o_f, o_b, st_scr, *, T):
    n_chunks = T // GLA_CHUNK
    shift, scale, gate = mod_ref[0:1, :], mod_ref[1:2, :], mod_ref[2:3, :]
    RB = 128
    PROJ_RB = 256

    def proj_body(r, carry):
        r0 = pl.multiple_of(r * PROJ_RB, PROJ_RB)
        h = _rms(x_ref[pl.ds(r0, PROJ_RB), :], n1g_ref[...]) * (1.0 + scale) + shift
        p = _dot(h, win_ref[...])
        proj[pl.ds(r0, PROJ_RB), :] = p
        z = _dot(p[:, C_A:C_A + 128], wgu_ref[...]) + bgu_ref[...]
        la[pl.ds(r0, PROJ_RB), :] = _log_sigmoid(z) * (1.0 / GLA_TAU)
        return carry

    lax.fori_loop(0, T // PROJ_RB, proj_body, 0)

    st_scr[0] = s0_ref[0].T
    st_scr[1] = s0_ref[1].T

    ci = lax.broadcasted_iota(jnp.int32, (GLA_CHUNK, GLA_CHUNK), 0)
    cj = lax.broadcasted_iota(jnp.int32, (GLA_CHUNK, GLA_CHUNK), 1)
    tri = (jnp.where(ci >= cj, 1.0, 0.0).astype(BF16), jnp.where(ci <= cj, 1.0, 0.0).astype(BF16))
    ai = lax.broadcasted_iota(jnp.int32, (GLA_HEADS * GLA_CHUNK, GLA_CHUNK), 0) % GLA_CHUNK
    aj = lax.broadcasted_iota(jnp.int32, (GLA_HEADS * GLA_CHUNK, GLA_CHUNK), 1)
    amask = (ai >= aj, ai <= aj)
    lane_head = lax.broadcasted_iota(jnp.int32, (1, QK_W), 1) // GLA_DK
    hmask = [jnp.where(lane_head == h, 1.0, 0.0) for h in range(GLA_HEADS)]

    def chunk_body(i, carry):
        for d in range(2):
            c = i if d == 0 else n_chunks - 1 - i
            r0 = pl.multiple_of(c * GLA_CHUNK, GLA_CHUNK)
            q = proj[pl.ds(r0, GLA_CHUNK), C_Q:C_Q + QK_W] * (GLA_DK ** -0.5)
            k = proj[pl.ds(r0, GLA_CHUNK), C_K:C_K + QK_W]
            v = proj[pl.ds(r0, GLA_CHUNK), C_V:C_V + V_W]
            lac = la[pl.ds(r0, GLA_CHUNK), d * QK_W:(d + 1) * QK_W]
            hi = lac.astype(BF16)
            lo = (lac - hi.astype(F32)).astype(BF16)
            b = (jnp.dot(tri[d], hi, preferred_element_type=F32)
                 + jnp.dot(tri[d], lo, preferred_element_type=F32))
            bend = b[GLA_CHUNK - 1:GLA_CHUNK, :] if d == 0 else b[0:1, :]
            qe = q * jnp.exp(b)
            ke = k * jnp.exp(-b)
            kd = k * jnp.exp(bend - b)
            st = st_scr[d]
            qstack = jnp.concatenate([qe * hmask[h] for h in range(GLA_HEADS)], axis=0).astype(BF16)
            att = jnp.where(amask[d], _dot_nt(qstack, ke), 0.0)
            inter = _dot_nt(qstack, st)
            outs = []
            for h in range(GLA_HEADS):
                rows = slice(h * GLA_CHUNK, (h + 1) * GLA_CHUNK)
                outs.append(_dot(att[rows], v[:, h * GLA_DV:(h + 1) * GLA_DV]) + inter[rows])
            o = jnp.concatenate(outs, axis=1)
            if d == 0:
                o_f[pl.ds(r0, GLA_CHUNK), :] = o
            else:
                o_b[pl.ds(r0, GLA_CHUNK), :] = o
            vstack = jnp.concatenate([v[:, h * GLA_DV:(h + 1) * GLA_DV] for h in range(GLA_HEADS)], axis=0)
            kstack = jnp.concatenate([kd * hmask[h] for h in range(GLA_HEADS)], axis=0)
            st_scr[d] = st * jnp.exp(bend) + _dot_tn(vstack, kstack)
        return carry

    lax.fori_loop(0, n_chunks, chunk_body, 0, unroll=2)
    st_ref[0] = st_scr[0].T
    st_ref[1] = st_scr[1].T

    def out_body(r, carry):
        r0 = pl.multiple_of(r * RB, RB)
        osum = o_f[pl.ds(r0, RB), :] + o_b[pl.ds(r0, RB), :]
        g = proj[pl.ds(r0, RB), C_G:C_G + V_W]
        u = proj[pl.ds(r0, RB), C_U:C_U + GMLP_W]
        vg = _gelu(proj[pl.ds(r0, RB), C_VG:C_VG + GMLP_W])
        parts = []
        for h in range(GLA_HEADS):
            oh = osum[:, h * GLA_DV:(h + 1) * GLA_DV]
            parts.append(_rms(oh, glag_ref[...]) * _silu(g[:, h * GLA_DV:(h + 1) * GLA_DV]))
        for gi in range(GMLP_GROUPS):
            vc = vg[:, gi * GMLP_DIM:(gi + 1) * GMLP_DIM]
            vc = vc - jnp.mean(vc, axis=-1, keepdims=True)
            vn = vc * lax.rsqrt(jnp.mean(vc * vc, axis=-1, keepdims=True) + EPS)
            sg = _dot(ws_ref[gi], vn) + bs_ref[:, gi:gi + 1]
            parts.append(_gelu(u[:, gi * GMLP_DIM:(gi + 1) * GMLP_DIM]) * sg)
        mix = jnp.concatenate(parts, axis=1)
        y = _dot(mix, wout_ref[...])
        xo_ref[pl.ds(r0, RB), :] = x_ref[pl.ds(r0, RB), :] + gate * y
        return carry

    lax.fori_loop(0, T // RB, out_body, 0)


def _even_mixer(x_all, mod_l, n1g, win, wgu, bgu, glag, ws, bs, wout, s0, *, latent, x_first=None):
    if latent:
        T, nseq, blk0 = DEC_SEQ, DEC_BATCH, N_CTX // DEC_SEQ
        cond = lambda i: 1 + i
        s0_spec = pl.BlockSpec((None, 2, QK_W, GLA_DV), lambda i: (i, 0, 0, 0))
    else:
        T, nseq, blk0 = SEQ, BATCH, 0
        cond = lambda i: 0
        s0_spec = pl.BlockSpec((None, 2, QK_W, GLA_DV), lambda i: (0, 0, 0, 0))
    const2 = lambda i: (0, 0)
    body = functools.partial(_even_kernel, T=T)
    x_spec = pl.BlockSpec((T, D), lambda i: (blk0 + i, 0))
    if x_first is None:
        lead_specs, lead_args, aliases = [x_spec], (x_all,), {0: 0}
    elif x_all is None:
        lead_specs, lead_args, aliases = [pl.BlockSpec((T, D), lambda i: (i, 0))], (x_first,), {}
    else:
        lead_specs = [pl.BlockSpec(memory_space=pl.ANY), pl.BlockSpec((T, D), lambda i: (i, 0))]
        lead_args, aliases = (x_all, x_first), {0: 0}
        body = lambda dst_ref, *refs: _even_kernel(*refs, T=T)
    x_new, states = pl.pallas_call(
        body,
        grid=(nseq,),
        in_specs=lead_specs + [
            pl.BlockSpec((None, 6, D), lambda i: (cond(i), 0, 0)),
            _resident((1, D), const2),
            _resident((D, EVEN_PACK), const2),
            _resident((128, 2 * QK_W), const2),
            _resident((1, 2 * QK_W), const2),
            _resident((1, GLA_DV), const2),
            _resident((GMLP_GROUPS, GMLP_CHUNK, GMLP_CHUNK), lambda i: (0, 0, 0)),
            _resident((GMLP_CHUNK, GMLP_GROUPS), const2),
            _resident((D, D), const2),
            s0_spec,
        ],
        out_specs=[
            x_spec,
            pl.BlockSpec((None, 2, QK_W, GLA_DV), lambda i: (i, 0, 0, 0)),
        ],
        out_shape=[
            jax.ShapeDtypeStruct((N_TOK, D), F32),
            jax.ShapeDtypeStruct((nseq, 2, QK_W, GLA_DV), F32),
        ],
        scratch_shapes=[
            pltpu.VMEM((T, EVEN_PACK), F32),
            pltpu.VMEM((T, 2 * QK_W), F32),
            pltpu.VMEM((T, V_W), F32),
            pltpu.VMEM((T, V_W), F32),
            pltpu.VMEM((2, GLA_DV, QK_W), F32),
        ],
        input_output_aliases=aliases,
        compiler_params=_cp("arbitrary", vmem_mib=48 if latent else 32),
        name="even_mixer_latent" if latent else "even_mixer_context",
    )(*lead_args, mod_l, n1g, win, wgu, bgu, glag, ws, bs, wout, s0)
    return x_new, states


QKV_TB = 512


def _qkv_kernel(x_ref, mod_ref, n1g_ref, win_ref, gq_ref, gk_ref, cos_ref, sin_ref, q_ref, k_ref, v_ref,
                ck_ref, cv_ref):
    shift, scale = mod_ref[0:1, :], mod_ref[1:2, :]
    h = _rms(x_ref[...], n1g_ref[...]) * (1.0 + scale) + shift
    p = _dot(h, win_ref[...])
    cos, sin = cos_ref[...], sin_ref[...]
    even_lane = lax.broadcasted_iota(jnp.int32, (1, HD), 1) % 2 == 0

    def rope(xn):
        swapped = jnp.where(even_lane, pltpu.roll(xn, HD - 1, axis=1), pltpu.roll(xn, 1, axis=1))
        return xn * cos + swapped * sin

    def emit(rotate, to_cache):
        for hh in range(ATT_HEADS):
            qn = _rms(p[:, hh * HD:(hh + 1) * HD], gq_ref[...])
            q_ref[:, hh * HD:(hh + 1) * HD] = (rotate(qn) * (HD ** -0.5)).astype(BF16)
        for hh in range(ATT_KV):
            kn = rotate(_rms(p[:, Q_W + hh * HD:Q_W + (hh + 1) * HD], gk_ref[...]))
            k_ref[:, hh * HD:(hh + 1) * HD] = kn
            if to_cache:
                for s in range(QKV_TB // SEQ):
                    ck_ref[s, pl.ds(hh, SEQ, stride=ATT_KV), :] = kn[s * SEQ:(s + 1) * SEQ]
                    cv_ref[s, pl.ds(hh, SEQ, stride=ATT_KV), :] = p[s * SEQ:(s + 1) * SEQ,
                                                                    Q_W + KV_W + hh * HD:Q_W + KV_W + (hh + 1) * HD]

    is_latent = pl.program_id(0) >= N_CTX // QKV_TB

    @pl.when(is_latent)
    def _():
        emit(rope, False)

    @pl.when(jnp.logical_not(is_latent))
    def _():
        emit(lambda xn: xn, True)

    v_ref[...] = p[:, Q_W + KV_W:]


def _qkv(x_all, mod_l, n1g, win, gq, gk, cos_tab, sin_tab, layer_i, caches=None):
    nb_ctx = N_CTX // QKV_TB
    per_seq = DEC_SEQ // QKV_TB
    cond = lambda i: jnp.where(i < nb_ctx, 0, 1 + (i - nb_ctx) // per_seq)
    tab = lambda i: jnp.where(i < nb_ctx, 0, 1 + (i - nb_ctx) % per_seq)
    const2 = lambda i: (0, 0)
    cache_spec = pl.BlockSpec((QKV_TB // SEQ, None, SEQ * ATT_KV, HD),
                              lambda i: (jnp.minimum(i, nb_ctx - 1), layer_i, 0, 0))
    cache_shape = jax.ShapeDtypeStruct((BATCH, DEPTH // 2, SEQ * ATT_KV, HD), F32)
    if caches is None:
        body, lead_specs, lead_args, aliases = _qkv_kernel, [], (), {}
    else:
        body = lambda ck_in, cv_in, *refs: _qkv_kernel(*refs)
        lead_specs = [pl.BlockSpec(memory_space=pl.ANY)] * 2
        lead_args, aliases = tuple(caches), {0: 3, 1: 4}
    return pl.pallas_call(
        body,
        grid=(N_TOK // QKV_TB,),
        in_specs=lead_specs + [
            pl.BlockSpec((QKV_TB, D), lambda i: (i, 0)),
            pl.BlockSpec((None, 6, D), lambda i: (cond(i), 0, 0)),
            _resident((1, D), const2),
            _resident((D, Q_W + 2 * KV_W), const2),
            _resident((1, HD), const2),
            _resident((1, HD), const2),
            pl.BlockSpec((None, QKV_TB, HD), lambda i: (tab(i), 0, 0)),
            pl.BlockSpec((None, QKV_TB, HD), lambda i: (tab(i), 0, 0)),
        ],
        out_specs=[
            pl.BlockSpec((QKV_TB, Q_W), lambda i: (i, 0)),
            pl.BlockSpec((QKV_TB, KV_W), lambda i: (i, 0)),
            pl.BlockSpec((QKV_TB, KV_W), lambda i: (i, 0)),
            cache_spec,
            cache_spec,
        ],
        out_shape=[
            jax.ShapeDtypeStruct((N_TOK, Q_W), BF16),
            jax.ShapeDtypeStruct((N_TOK, KV_W), F32),
            jax.ShapeDtypeStruct((N_TOK, KV_W), F32),
            cache_shape,
            cache_shape,
        ],
        input_output_aliases=aliases,
        compiler_params=_cp("arbitrary"),
        name="odd_qkv",
    )(*lead_args, x_all, mod_l, n1g, win, gq, gk, cos_tab, sin_tab)


ATT_TQ_LATENT = 512


def _attn_kernel(*refs, n_kv):
    q_ref = refs[0]
    kv_refs = refs[1:1 + 2 * n_kv]
    x_ref, mod_ref, wout_ref, xo_ref, att_scr = refs[1 + 2 * n_kv:]
    gate = mod_ref[2:3, :]
    def head(ref, kh):
        if ref.shape[1] == HD:
            return ref[pl.ds(kh, ref.shape[0] // ATT_KV, stride=ATT_KV), :].astype(BF16)
        return ref[:, kh * HD:(kh + 1) * HD].astype(BF16)

    for kh in range(ATT_KV):
        ks = [head(kv_refs[2 * s], kh) for s in range(n_kv)]
        vs = [jnp.concatenate([vh, jnp.ones_like(vh)], axis=1)
              for vh in (head(kv_refs[2 * s + 1], kh) for s in range(n_kv))]
        for g in range(ATT_G):
            hh = kh * ATT_G + g
            qh = q_ref[:, hh * HD:(hh + 1) * HD]
            ss = [_dot_nt(qh, kk) for kk in ks]
            m = ss[0].max(axis=-1, keepdims=True)
            for s in ss[1:]:
                m = jnp.maximum(m, s.max(axis=-1, keepdims=True))
            o = _dot(jnp.exp(ss[0] - m), vs[0])
            for s, vv in zip(ss[1:], vs[1:]):
                o = o + _dot(jnp.exp(s - m), vv)
            att_scr[:, hh * HD:(hh + 1) * HD] = o[:, :HD] / o[:, HD:HD + 1]
    y = _dot(att_scr[...], wout_ref[...])
    xo_ref[...] = x_ref[...] + gate * y


def _attention(x_all, mod_l, q, k, v, wout, cache_k=None, cache_v=None, layer_i=0):
    latent = cache_k is not None
    const2 = lambda *a: (0, 0)
    tq = ATT_TQ_LATENT if latent else SEQ
    if latent:
        nq = DEC_SEQ // tq
        row_blk = lambda b, j: (N_CTX // tq + b * nq + j, 0)
        grid = (DEC_BATCH, nq)
        kv_specs = [
            pl.BlockSpec((None, None, SEQ * ATT_KV, HD), lambda b, j: (b, layer_i, 0, 0)),
            pl.BlockSpec((None, None, SEQ * ATT_KV, HD), lambda b, j: (b, layer_i, 0, 0)),
            pl.BlockSpec((DEC_SEQ, KV_W), lambda b, j: (N_CTX // DEC_SEQ + b, 0)),
            pl.BlockSpec((DEC_SEQ, KV_W), lambda b, j: (N_CTX // DEC_SEQ + b, 0)),
        ]
        kv_args = (cache_k, cache_v, k, v)
        mod_spec = pl.BlockSpec((None, 6, D), lambda b, j: (1 + b, 0, 0))
        sem = ("arbitrary", "arbitrary")
        n_kv = 2
    else:
        row_blk = lambda i: (i, 0)
        grid = (BATCH,)
        kv_specs = [pl.BlockSpec((SEQ, KV_W), row_blk), pl.BlockSpec((SEQ, KV_W), row_blk)]
        kv_args = (k, v)
        mod_spec = pl.BlockSpec((None, 6, D), lambda i: (0, 0, 0))
        sem = ("arbitrary",)
        n_kv = 1
    n_in = 1 + len(kv_args)
    return pl.pallas_call(
        functools.partial(_attn_kernel, n_kv=n_kv),
        grid=grid,
        in_specs=[pl.BlockSpec((tq, Q_W), row_blk)] + kv_specs + [
            pl.BlockSpec((tq, D), row_blk),
            mod_spec,
            _resident((D, D), const2),
        ],
        out_specs=pl.BlockSpec((tq, D), row_blk),
        out_shape=jax.ShapeDtypeStruct((N_TOK, D), F32),
        scratch_shapes=[pltpu.VMEM((tq, Q_W), F32)],
        input_output_aliases={n_in: 0},
        compiler_params=_cp(*sem),
        name="attention_latent" if latent else "attention_context",
    )(q, *kv_args, x_all, mod_l, wout)


ROUTE_TB = 512
HALF_TOK = N_TOK // 2
M_E1, M_E2, M_G1, M_G2, M_R1, M_R2 = 0, 1, 2, 3, 4, 5


def _router_kernel(x_ref, mod_ref, n2g_ref, w2_ref, br_ref, h_ref, metat_ref, cnt_ref, run):
    @pl.when(pl.program_id(0) % (HALF_TOK // ROUTE_TB) == 0)
    def _():
        run[...] = jnp.zeros_like(run)

    shift, scale = mod_ref[3:4, :], mod_ref[4:5, :]
    h = _rms(x_ref[...], n2g_ref[...]) * (1.0 + scale) + shift
    _rows_to_tiles(h_ref, h)
    h_hi, h_lo = _split_bf16(h)
    dot = functools.partial(jnp.dot, preferred_element_type=F32)
    wide = dot(h_hi, w2_ref[...])
    logits = wide[:, :128] + wide[:, 128:] + dot(h_lo, w2_ref[:, :128]) + br_ref[...]
    lane = lax.broadcasted_iota(jnp.int32, logits.shape, 1).astype(F32)
    big = 1e4

    def first_argmax(vals):
        m = vals.max(axis=-1, keepdims=True)
        return m, jnp.where(vals == m, lane, big).min(axis=-1, keepdims=True)

    gl = jnp.where((lane >= N_EXP) & (lane < N_EXP + MOE_GROUPS), logits, NEG)
    gmax, glane = first_argmax(gl)
    g_p = 1.0 / jnp.exp(gl - gmax).sum(axis=-1, keepdims=True)
    lo = (glane - N_EXP) * MOE_PER_GROUP
    el = jnp.where((lane >= lo) & (lane < lo + MOE_PER_GROUP), logits, NEG)
    m1, i1 = first_argmax(el)
    m2, i2 = first_argmax(jnp.where(lane == i1, NEG, el))
    t = jnp.exp(m2 - m1)
    w1 = 1.0 / (1.0 + t)
    sel1, sel2 = lane == i1, lane == i2
    onehot = jnp.where(sel1 | sel2, 1.0, 0.0)
    ri = lax.broadcasted_iota(jnp.int32, (ROUTE_TB, ROUTE_TB), 0)
    rj = lax.broadcasted_iota(jnp.int32, (ROUTE_TB, ROUTE_TB), 1)
    before = _dot(jnp.where(ri > rj, 1.0, 0.0), onehot) + run[...]
    r1 = jnp.where(sel1, before, 0.0).sum(axis=-1, keepdims=True)
    r2 = jnp.where(sel2, before, 0.0).sum(axis=-1, keepdims=True)
    run[...] += onehot.sum(axis=0, keepdims=True)
    cnt_ref[...] = run[...]
    meta = jnp.zeros_like(logits)
    for j, val in enumerate([i1, i2, w1 * g_p, (t * w1) * g_p, r1, r2]):
        meta = jnp.where(lane == j, val, meta)
    metat_ref[...] = meta.T[0:8, :]


def _router(x_all, mod_l, n2g, wr, br):
    w2 = jnp.concatenate(_split_bf16(wr), axis=1)
    nb_ctx = N_CTX // ROUTE_TB
    per_seq = DEC_SEQ // ROUTE_TB
    cond = lambda i: jnp.where(i < nb_ctx, 0, 1 + (i - nb_ctx) // per_seq)
    const2 = lambda i: (0, 0)
    return pl.pallas_call(
        _router_kernel,
        grid=(N_TOK // ROUTE_TB,),
        in_specs=[
            pl.BlockSpec((ROUTE_TB, D), lambda i: (i, 0)),
            pl.BlockSpec((None, 6, D), lambda i: (cond(i), 0, 0)),
            _resident((1, D), const2),
            _resident((D, 256), const2),
            _resident((1, 128), const2),
        ],
        out_specs=[
            pl.BlockSpec((ROUTE_TB * 8, 128), lambda i: (i, 0)),
            pl.BlockSpec((8, ROUTE_TB), lambda i: (0, i)),
            pl.BlockSpec((None, 1, 128), lambda i: (i // (HALF_TOK // ROUTE_TB), 0, 0)),
        ],
        out_shape=[
            jax.ShapeDtypeStruct((N_TOK * 8, 128), F32),
            jax.ShapeDtypeStruct((8, N_TOK), F32),
            jax.ShapeDtypeStruct((2, 1, 128), F32),
        ],
        scratch_shapes=[pltpu.VMEM((1, 128), F32)],
        compiler_params=_cp("arbitrary"),
        name="moe_router",
    )(x_all, mod_l, n2g, w2, br)


EXP_TM = 128
N_ASSIGN = 2 * N_TOK
N_GROUPS = 2 * N_EXP
MAX_TILES = N_ASSIGN // EXP_TM + N_GROUPS
N_SORTED = MAX_TILES * EXP_TM
ORDER_BLK = 2048
CODE_PLANE = 2 * HALF_TOK
CODE_MASK = 8 * CODE_PLANE - 1
DUMMY8 = HALF_TOK * 8


def _order_kernel(pos1_ref, pos2_ref, pad_lo_ref, pad_hi_ref, src_ref):
    i = pl.program_id(0)
    local = (i % (HALF_TOK // ORDER_BLK)) * ORDER_BLK

    def body(t, carry):
        src_ref[pos1_ref[t]] = (local + t) * 8
        src_ref[pos2_ref[t]] = (local + t + CODE_PLANE) * 8
        return carry

    lax.fori_loop(0, ORDER_BLK, body, 0, unroll=16)

    @pl.when(i == 0)
    def _():
        def group(g, carry):
            def pad(p, c):
                src_ref[p] = DUMMY8
                return c
            return lax.fori_loop(pad_lo_ref[g], pad_hi_ref[g], pad, carry)

        lax.fori_loop(0, N_GROUPS, group, 0)


def _order(pos, pad_lo, pad_hi):
    return pl.pallas_call(
        _order_kernel,
        grid=(N_TOK // ORDER_BLK,),
        in_specs=[
            pl.BlockSpec((ORDER_BLK,), lambda i: (i,), memory_space=pltpu.SMEM),
            pl.BlockSpec((ORDER_BLK,), lambda i: (N_TOK // ORDER_BLK + i,), memory_space=pltpu.SMEM),
            pl.BlockSpec(memory_space=pltpu.SMEM),
            pl.BlockSpec(memory_space=pltpu.SMEM),
        ],
        out_specs=pl.BlockSpec(memory_space=pltpu.SMEM),
        out_shape=jax.ShapeDtypeStruct((N_SORTED,), jnp.int32),
        compiler_params=_cp("arbitrary"),
        name="moe_order",
    )(pos, pos, pad_lo, pad_hi)


GATE_BLK = CODE_PLANE + HALF_TOK
ACC_TOK = HALF_TOK + 64
GATHER_GROUP, ACC_GROUP = 16, 8


RES_TB = 256


def _experts_kernel(tile0_ref, ntile_ref, count_ref, src_ref, gs_ref, h_hbm, x_hbm, mod_ref, wg_ref, wu_ref, wd_ref,
                    *rest, final):
    if final:
        fg_ref, *dst_hbm = rest[:3]
        rest = rest[3:]
    else:
        dst_hbm, rest = rest[:1], rest[1:]
    h_res, acc, xbuf, ybuf, wgb, wub, wdb, xin, xout, sem, in_sem, out_sem = rest
    group = pl.program_id(0)
    expert = group % N_EXP
    half = group // N_EXP
    rows0 = pl.multiple_of(half * (HALF_TOK * 8), 8)

    @pl.when(expert == 0)
    def _():
        cp = pltpu.make_async_copy(h_hbm.at[pl.ds(rows0, HALF_TOK * 8), :], h_res.at[pl.ds(0, HALF_TOK * 8), :], sem)
        cp.start()
        h_res[pl.ds(DUMMY8, 8), :] = jnp.zeros((8, 128), F32)
        xbuf[...] = jnp.zeros_like(xbuf)

        def zero(i, carry):
            acc[pl.ds(pl.multiple_of(i * 512, 512), 512), :] = jnp.zeros((512, 128), F32)
            return carry

        lax.fori_loop(0, ACC_TOK * 8 // 512, zero, 0)
        cp.wait()

    n_tiles = ntile_ref[group]

    @pl.when(n_tiles > 0)
    def _():
        wgb[...] = wg_ref[...].astype(BF16)
        wub[...] = wu_ref[...].astype(BF16)
        wdb[...] = wd_ref[...].astype(BF16)

    row0 = tile0_ref[group] * EXP_TM
    row_end = row0 + count_ref[group]

    def process(base, rows):
        live = (jnp.clip(row_end - base, 0, rows) + GATHER_GROUP - 1) // GATHER_GROUP

        def gather(g, c):
            for i in range(GATHER_GROUP):
                r = g * GATHER_GROUP + i
                xbuf[pl.ds(pl.multiple_of(r * 8, 8), 8), :] = _tile_of(h_res, src_ref[base + r] & CODE_MASK)[...]
            return c

        lax.fori_loop(0, live, gather, 0)
        x = _tiles_to_rows(xbuf, rows).astype(BF16)
        hid = _silu(_dot(x, wgb[...])) * _dot(x, wub[...])
        _rows_to_tiles(ybuf, _dot(hid, wdb[...]))

        def accumulate(g, c):
            targets, values = [], []
            for i in range(ACC_GROUP):
                r = g * ACC_GROUP + i
                code = src_ref[base + r]
                target = _tile_of(acc, code & CODE_MASK)
                targets.append(target)
                values.append(target[...] + gs_ref[code >> 3] * ybuf[pl.ds(pl.multiple_of(r * 8, 8), 8), :])
            for target, value in zip(targets, values):
                target[...] = value
            return c

        lax.fori_loop(0, live * (GATHER_GROUP // ACC_GROUP), accumulate, 0)


    def pair_body(j, carry):
        process(row0 + j * (2 * EXP_TM), 2 * EXP_TM)
        return carry

    lax.fori_loop(0, n_tiles // 2, pair_body, 0)

    @pl.when(n_tiles % 2 == 1)
    def _():
        process(row0 + (n_tiles - 1) * EXP_TM, EXP_TM)

    def rows_of(first, blk):
        return pl.ds(pl.multiple_of(first + blk * RES_TB, RES_TB), RES_TB)

    def load_x(blk, slot):
        return pltpu.make_async_copy(x_hbm.at[rows_of(half * HALF_TOK, blk), :], xin.at[slot], in_sem.at[slot])

    def residual(blk, slot):
        cond = jnp.where(half == 0, 0, 1 + blk // (DEC_SEQ // RES_TB))
        gate = mod_ref[cond, 5:6, :]
        y = _tiles_to_rows(acc.at[pl.ds(pl.multiple_of(blk * (RES_TB * 8), RES_TB * 8), RES_TB * 8), :], RES_TB)
        x_new = xin[slot] + gate * y
        xout[slot] = _rms(x_new, fg_ref[...]) if final else x_new

    def epilogue(dst, first_row):
        def store_x(blk, slot):
            return pltpu.make_async_copy(xout.at[slot], dst.at[rows_of(first_row, blk), :], out_sem.at[slot])

        n_pairs = HALF_TOK // RES_TB // 2
        load_x(0, 0).start()

        def pair(p, carry):
            for slot in range(2):
                blk = 2 * p + slot
                if slot == 0:
                    load_x(blk + 1, 1).start()
                else:
                    @pl.when(p + 1 < n_pairs)
                    def _():
                        load_x(blk + 1, 0).start()
                load_x(blk, slot).wait()

                @pl.when(p > 0)
                def _():
                    store_x(blk - 2, slot).wait()

                residual(blk, slot)
                store_x(blk, slot).start()
            return carry

        lax.fori_loop(0, n_pairs, pair, 0)
        store_x(2 * n_pairs - 2, 0).wait()
        store_x(2 * n_pairs - 1, 1).wait()

    if final:
        for which in range(2):
            @pl.when((expert == N_EXP - 1) & (half == which))
            def _():
                epilogue(dst_hbm[which], 0)
    else:
        @pl.when(expert == N_EXP - 1)
        def _():
            epilogue(dst_hbm[0], half * HALF_TOK)


def _experts(tile0, n_tiles, counts, src, gs, h, x_all, mod_l, wg, wu, wd, layer, final_g=None):
    final = final_g is not None
    wmap = lambda g, t0, nt, cnt, src: (layer, g % N_EXP, 0, 0)
    any_spec = pl.BlockSpec(memory_space=pl.ANY)
    extra_specs = [pl.BlockSpec((1, D), lambda g, t0, nt, cnt, src: (0, 0))] if final else []
    extra_args = (final_g,) if final else ()
    return pl.pallas_call(
        functools.partial(_experts_kernel, final=final),
        grid_spec=pltpu.PrefetchScalarGridSpec(
            num_scalar_prefetch=4,
            grid=(N_GROUPS,),
            in_specs=[
                pl.BlockSpec((GATE_BLK,), lambda g, t0, nt, cnt, src: (g // N_EXP,), memory_space=pltpu.SMEM),
                pl.BlockSpec(memory_space=pl.ANY),
                pl.BlockSpec(memory_space=pl.ANY),
                pl.BlockSpec((8, 6, D), lambda g, t0, nt, cnt, src: (0, 0, 0)),
                pl.BlockSpec((None, None, D, D_EXP), wmap),
                pl.BlockSpec((None, None, D, D_EXP), wmap),
                pl.BlockSpec((None, None, D_EXP, D), wmap),
            ] + extra_specs,
            out_specs=[any_spec, any_spec] if final else any_spec,
            scratch_shapes=[
                pltpu.VMEM((ACC_TOK * 8, 128), F32),
                pltpu.VMEM((ACC_TOK * 8, 128), F32),
                pltpu.VMEM((2 * EXP_TM * 8, 128), F32),
                pltpu.VMEM((2 * EXP_TM * 8, 128), F32),
                pltpu.VMEM((D, D_EXP), BF16),
                pltpu.VMEM((D, D_EXP), BF16),
                pltpu.VMEM((D_EXP, D), BF16),
                pltpu.VMEM((2, RES_TB, D), F32),
                pltpu.VMEM((2, RES_TB, D), F32),
                pltpu.SemaphoreType.DMA,
                pltpu.SemaphoreType.DMA((2,)),
                pltpu.SemaphoreType.DMA((2,)),
            ],
        ),
        out_shape=([jax.ShapeDtypeStruct((HALF_TOK, D), F32)] * 2 if final
                   else jax.ShapeDtypeStruct((N_TOK, D), F32)),
        input_output_aliases={} if final else {6: 0},
        compiler_params=_cp("arbitrary", vmem_mib=56),
        name="moe_experts_final" if final else "moe_experts",
    )(tile0, n_tiles, counts, src, gs, h, x_all, mod_l, wg, wu, wd, *extra_args)


def _moe(x_all, mod_l, n2g, wr, br, wg, wu, wd, layer, final_g=None):
    h, metat, cnt = _router(x_all, mod_l, n2g, wr, br)
    counts = cnt[:, 0, :N_EXP].astype(jnp.int32).reshape(N_GROUPS)
    padded = (counts + EXP_TM - 1) // EXP_TM * EXP_TM
    ends = jnp.cumsum(padded)
    offs = ends - padded
    rec = metat.astype(jnp.int32)
    half = (jnp.arange(N_TOK, dtype=jnp.int32) // HALF_TOK)[None, :]
    group = rec[M_E1:M_E2 + 1] + N_EXP * half
    is_group = group[None] == jnp.arange(N_GROUPS, dtype=jnp.int32)[:, None, None]
    pos = jnp.sum(jnp.where(is_group, offs[:, None, None], 0), axis=0) + rec[M_R1:M_R2 + 1]
    live_end = offs + (counts + GATHER_GROUP - 1) // GATHER_GROUP * GATHER_GROUP
    src = _order(pos.reshape(N_ASSIGN), offs + counts, live_end)
    g12 = metat[M_G1:M_G2 + 1].reshape(2, 2, HALF_TOK)
    gates = jnp.concatenate([g12[0], jnp.zeros((2, CODE_PLANE - HALF_TOK), F32), g12[1]], axis=1)
    return _experts(offs // EXP_TM, padded // EXP_TM, counts, src, gates.reshape(2 * GATE_BLK), h, x_all, mod_l,
                    wg, wu, wd, layer, final_g)


def _rope_tables():
    pos = jnp.arange(DEC_SEQ)
    row = (pos // GRID_W).astype(F32)
    col = (pos % GRID_W).astype(F32)
    n_freq = HD // 4
    inv = ROPE_THETA ** (-jnp.arange(n_freq, dtype=F32) / n_freq)
    ang = jnp.concatenate([row[:, None] * inv, col[:, None] * inv], axis=-1)
    cos = jnp.repeat(jnp.cos(ang), 2, axis=-1)
    sin = jnp.repeat(jnp.sin(ang), 2, axis=-1) * jnp.tile(jnp.array([-1.0, 1.0], F32), HD // 2)
    nblk = DEC_SEQ // QKV_TB
    cos_tab = jnp.concatenate([jnp.ones((1, QKV_TB, HD), F32), cos.reshape(nblk, QKV_TB, HD)], axis=0)
    sin_tab = jnp.concatenate([jnp.zeros((1, QKV_TB, HD), F32), sin.reshape(nblk, QKV_TB, HD)], axis=0)
    return cos_tab, sin_tab


def kernel(x_prompt, x_sample, state_gla, cache_k, cache_v, c, c_ctx, w_mod, b_mod, norm1_g, norm2_g,
           w_in_even, w_gate_up, b_gate_up, gla_norm_g, w_spatial, b_spatial, w_out_even,
           w_in_odd, q_norm_g, k_norm_g, w_out_odd, w_router_group, b_router_group,
           w_router_expert, b_router_expert, w_exp_gate, w_exp_up, w_exp_down, final_norm_g):
    x_all = None
    cond8 = jnp.concatenate([c_ctx[None], c, jnp.zeros((3, D), F32)], axis=0)
    mod = _modulation(cond8, w_mod, b_mod)
    cos_tab, sin_tab = _rope_tables()
    zero_state = jnp.zeros((1, 2, QK_W, GLA_DV), F32)
    state_in = state_gla.reshape(DEC_BATCH, -1, 2, QK_W, GLA_DV)
    cache_k2 = cache_k.reshape(DEC_BATCH, -1, SEQ * ATT_KV, HD)
    cache_v2 = cache_v.reshape(DEC_BATCH, -1, SEQ * ATT_KV, HD)

    gla_states, caches = [], None
    for l in range(DEPTH):
        i = l // 2
        n1g = norm1_g[l][None]
        if l % 2 == 0:
            w = w_in_even[i]
            win = jnp.concatenate([w[:, :1536], w[:, 1568:], w[:, 1536:1568], jnp.zeros((D, 96), F32)],
                                  axis=1).astype(BF16)
            wgu = jnp.zeros((128, 2 * QK_W), F32)
            wgu = wgu.at[0:GLA_RANK, 0:QK_W].set(w_gate_up[i, 0])
            wgu = wgu.at[GLA_RANK:2 * GLA_RANK, QK_W:].set(w_gate_up[i, 1]).astype(BF16)
            bgu = b_gate_up[i].reshape(1, 2 * QK_W)
            args = (mod[l], n1g, win, wgu, bgu, gla_norm_g[i][None], w_spatial[i].astype(BF16),
                    b_spatial[i].T, w_out_even[i].astype(BF16))
            first = l == 0
            x_all, st = _even_mixer(x_all, *args, zero_state, latent=False,
                                    x_first=x_prompt.reshape(N_CTX, D) if first else None)
            gla_states.append(st)
            x_all, _ = _even_mixer(x_all, *args, state_in[:, i], latent=True,
                                   x_first=x_sample.reshape(N_LAT, D) if first else None)
        else:
            q, k, v, *caches = _qkv(x_all, mod[l], n1g, w_in_odd[i].astype(BF16), q_norm_g[i][None],
                                    k_norm_g[i][None], cos_tab, sin_tab, i, caches)
            wout = w_out_odd[i].astype(BF16)
            x_all = _attention(x_all, mod[l], q, k, v, wout)
            x_all = _attention(x_all, mod[l], q, k, v, wout, cache_k2, cache_v2, layer_i=i)
        wr = jnp.concatenate([w_router_expert[l], w_router_group[l],
                              jnp.zeros((D, 128 - N_EXP - MOE_GROUPS), F32)], axis=1)
        br = jnp.concatenate([b_router_expert[l], b_router_group[l],
                              jnp.zeros((128 - N_EXP - MOE_GROUPS,), F32)])[None]
        x_all = _moe(x_all, mod[l], norm2_g[l][None], wr, br, w_exp_gate, w_exp_up, w_exp_down, l,
                     final_norm_g[None] if l == DEPTH - 1 else None)

    y_prompt = x_all[0].reshape(BATCH, SEQ, D)
    y_sample = x_all[1].reshape(DEC_BATCH, DEC_SEQ, D)
    new_state = jnp.stack(gla_states, axis=1).reshape(BATCH, -1, 2, GLA_HEADS, GLA_DK, GLA_DV)
    new_k, new_v = (a.reshape(BATCH, DEPTH // 2, SEQ, ATT_KV, HD) for a in caches)
    return (y_prompt, y_sample, new_state, new_k, new_v)
```

```python
import functools

import jax
import jax.numpy as jnp
import numpy as np
from jax import lax
from jax.experimental import pallas as pl
from jax.experimental.pallas import tpu as pltpu

F32 = jnp.float32
BF16 = jnp.bfloat16

D = 1024
BATCH, SEQ = 16, 256
DEC_BATCH, DEC_SEQ = 4, 1024
N_CTX = BATCH * SEQ
N_LAT = DEC_BATCH * DEC_SEQ
N_TOK = N_CTX + N_LAT
DEPTH = 4
EPS = 1e-6
GRID_W = 64
ROPE_THETA = 10000.0

GLA_HEADS, GLA_DK, GLA_DV, GLA_RANK, GLA_CHUNK, GLA_TAU = 4, 64, 128, 16, 128, 16.0
QK_W = GLA_HEADS * GLA_DK
V_W = GLA_HEADS * GLA_DV
GMLP_GROUPS, GMLP_DIM, GMLP_CHUNK = 4, 128, 128
GMLP_W = GMLP_GROUPS * GMLP_DIM
C_Q, C_K, C_V, C_G, C_A, C_U, C_VG = 0, 256, 512, 1024, 1536, 1664, 2176
EVEN_SPLIT = 1568
EVEN_W1 = C_U
EVEN_PACK = 2688

ATT_HEADS, ATT_KV, HD = 8, 2, 128
ATT_G = ATT_HEADS // ATT_KV
Q_W = ATT_HEADS * HD
KV_W = ATT_KV * HD

MOE_GROUPS, MOE_PER_GROUP = 4, 8
N_EXP = MOE_GROUPS * MOE_PER_GROUP
D_EXP = D // 4
NEG = -1e30

MIB = 1024 * 1024
V7X_VMEM_MIB = 64
LANES, SUBLANES = 128, 8


def _cp(*sem, vmem_mib=32):
    assert vmem_mib < V7X_VMEM_MIB
    return pltpu.CompilerParams(dimension_semantics=sem, vmem_limit_bytes=vmem_mib * MIB)


def _dot(a, b):
    return jnp.dot(a.astype(BF16), b.astype(BF16), preferred_element_type=F32)


def _dot_nt(a, b):
    return lax.dot_general(a.astype(BF16), b.astype(BF16), (((1,), (1,)), ((), ())),
                           preferred_element_type=F32)


def _dot_tn(a, b):
    return lax.dot_general(a.astype(BF16), b.astype(BF16), (((0,), (0,)), ((), ())),
                           preferred_element_type=F32)


def _rms(x, g):
    return x * lax.rsqrt(jnp.mean(x * x, axis=-1, keepdims=True) + EPS) * g


def _silu(x):
    return x * jax.nn.sigmoid(x)


def _gelu(x):
    return 0.5 * x * (1.0 + jnp.tanh(np.sqrt(2.0 / np.pi).astype(np.float32) * (x + 0.044715 * (x * x * x))))


def _log_sigmoid(z):
    return jnp.minimum(z, 0.0) - jnp.log(1.0 + jnp.exp(-jnp.abs(z)))


assert D == LANES * SUBLANES


def _rows_to_tiles(ref, x):
    rows = x.shape[0]
    for j in range(SUBLANES):
        ref[pl.ds(j, rows, stride=SUBLANES), :] = x[:, j * LANES:(j + 1) * LANES]


def _tiles_to_rows(ref, rows):
    return jnp.concatenate([ref[pl.ds(j, rows, stride=SUBLANES), :] for j in range(SUBLANES)], axis=1)


def _tile_of(ref, row8):
    return ref.at[pl.ds(pl.multiple_of(row8, SUBLANES), SUBLANES), :]


def _resident(shape, index_map):
    return pl.BlockSpec(shape, index_map, pipeline_mode=pl.Buffered(1))


def _split_bf16(x):
    hi = x.astype(BF16)
    return hi, (x - hi.astype(F32)).astype(BF16)


def _mod_kernel(cond_ref, w_ref, b_ref, o_ref):
    s_hi, s_lo = _split_bf16(_silu(cond_ref[...]))
    w_hi, w_lo = _split_bf16(w_ref[...])
    dot = functools.partial(jnp.dot, preferred_element_type=F32)
    both = dot(jnp.concatenate([s_hi, s_lo], axis=0), w_hi)
    o_ref[...] = both[0:8] + both[8:16] + dot(s_hi, w_lo) + b_ref[...]


def _modulation(cond8, w_mod, b_mod):
    tn = 2048
    out = pl.pallas_call(
        _mod_kernel,
        grid=(DEPTH, 6 * D // tn),
        in_specs=[
            pl.BlockSpec((8, D), lambda l, j: (0, 0)),
            pl.BlockSpec((None, D, tn), lambda l, j: (l, 0, j)),
            pl.BlockSpec((None, 1, tn), lambda l, j: (l, 0, j)),
        ],
        out_specs=pl.BlockSpec((None, 8, tn), lambda l, j: (l, 0, j)),
        out_shape=jax.ShapeDtypeStruct((DEPTH, 8, 6 * D), F32),
        compiler_params=_cp("arbitrary", "arbitrary"),
        name="adaln_mod",
    )(cond8, w_mod, b_mod.reshape(DEPTH, 1, 6 * D))
    return out.reshape(DEPTH, 8, 6, D)


def _even_kernel(x_ref, mod_ref, n1g_ref, win1_ref, win2_ref, wgu_ref, bgu_ref, glag_ref, ws_ref, bs_ref,
                 wout_ref, s0_ref, xo_ref, st_ref, proj, la, o_f, o_b, st_scr, *, T):
    n_chunks = T // GLA_CHUNK
    shift, scale, gate = mod_ref[0:1, :], mod_ref[1:2, :], mod_ref[2:3, :]
    RB = 128
    PROJ_RB = 256

    def proj_body(r, carry):
        r0 = pl.multiple_of(r * PROJ_RB, PROJ_RB)
        h = _rms(x_ref[pl.ds(r0, PROJ_RB), :], n1g_ref[...]) * (1.0 + scale) + shift
        hb = h.astype(BF16)
        p = _dot(hb, win1_ref[...])
        proj[pl.ds(r0, PROJ_RB), 0:EVEN_W1] = p
        proj[pl.ds(r0, PROJ_RB), EVEN_W1:] = _dot(hb, win2_ref[...])
        z = _dot(p[:, C_A:C_A + 128], wgu_ref[...]) + bgu_ref[...]
        la[pl.ds(r0, PROJ_RB), :] = _log_sigmoid(z) * (1.0 / GLA_TAU)
        return carry

    lax.fori_loop(0, T // PROJ_RB, proj_body, 0)

    st_scr[0] = s0_ref[0].T
    st_scr[1] = s0_ref[1].T

    ci = lax.broadcasted_iota(jnp.int32, (GLA_CHUNK, GLA_CHUNK), 0)
    cj = lax.broadcasted_iota(jnp.int32, (GLA_CHUNK, GLA_CHUNK), 1)
    tri = (jnp.where(ci >= cj, 1.0, 0.0).astype(BF16), jnp.where(ci <= cj, 1.0, 0.0).astype(BF16))
    ai = lax.broadcasted_iota(jnp.int32, (GLA_HEADS * GLA_CHUNK, GLA_CHUNK), 0) % GLA_CHUNK
    aj = lax.broadcasted_iota(jnp.int32, (GLA_HEADS * GLA_CHUNK, GLA_CHUNK), 1)
    amask = (ai >= aj, ai <= aj)
    lane_head = lax.broadcasted_iota(jnp.int32, (1, QK_W), 1) // GLA_DK
    hmask = [jnp.where(lane_head == h, 1.0, 0.0) for h in range(GLA_HEADS)]

    def chunk_body(i, carry):
        for d in range(2):
            c = i if d == 0 else n_chunks - 1 - i
            r0 = pl.multiple_of(c * GLA_CHUNK, GLA_CHUNK)
            q = proj[pl.ds(r0, GLA_CHUNK), C_Q:C_Q + QK_W] * (GLA_DK ** -0.5)
            k = proj[pl.ds(r0, GLA_CHUNK), C_K:C_K + QK_W]
            v = proj[pl.ds(r0, GLA_CHUNK), C_V:C_V + V_W]
            lac = la[pl.ds(r0, GLA_CHUNK), d * QK_W:(d + 1) * QK_W]
            hi = lac.astype(BF16)
            lo = (lac - hi.astype(F32)).astype(BF16)
            b = (jnp.dot(tri[d], hi, preferred_element_type=F32)
                 + jnp.dot(tri[d], lo, preferred_element_type=F32))
            bend = b[GLA_CHUNK - 1:GLA_CHUNK, :] if d == 0 else b[0:1, :]
            qe = q * jnp.exp(b)
            ke = k * jnp.exp(-b)
            kd = k * jnp.exp(bend - b)
            st = st_scr[d]
            qstack = jnp.concatenate([qe * hmask[h] for h in range(GLA_HEADS)], axis=0).astype(BF16)
            att = jnp.where(amask[d], _dot_nt(qstack, ke), 0.0)
            inter = _dot_nt(qstack, st)
            outs = []
            for h in range(GLA_HEADS):
                rows = slice(h * GLA_CHUNK, (h + 1) * GLA_CHUNK)
                outs.append(_dot(att[rows], v[:, h * GLA_DV:(h + 1) * GLA_DV]) + inter[rows])
            o = jnp.concatenate(outs, axis=1)
            if d == 0:
                o_f[pl.ds(r0, GLA_CHUNK), :] = o
            else:
                o_b[pl.ds(r0, GLA_CHUNK), :] = o
            vstack = jnp.concatenate([v[:, h * GLA_DV:(h + 1) * GLA_DV] for h in range(GLA_HEADS)], axis=0)
            kstack = jnp.concatenate([kd * hmask[h] for h in range(GLA_HEADS)], axis=0)
            st_scr[d] = st * jnp.exp(bend) + _dot_tn(vstack, kstack)
        return carry

    lax.fori_loop(0, n_chunks, chunk_body, 0, unroll=2)
    st_ref[0] = st_scr[0].T
    st_ref[1] = st_scr[1].T

    def out_body(r, carry):
        r0 = pl.multiple_of(r * RB, RB)
        osum = o_f[pl.ds(r0, RB), :] + o_b[pl.ds(r0, RB), :]
        g = proj[pl.ds(r0, RB), C_G:C_G + V_W]
        u = proj[pl.ds(r0, RB), C_U:C_U + GMLP_W]
        vg = _gelu(proj[pl.ds(r0, RB), C_VG:C_VG + GMLP_W])
        parts = []
        for h in range(GLA_HEADS):
            oh = osum[:, h * GLA_DV:(h + 1) * GLA_DV]
            parts.append(_rms(oh, glag_ref[...]) * _silu(g[:, h * GLA_DV:(h + 1) * GLA_DV]))
        for gi in range(GMLP_GROUPS):
            vc = vg[:, gi * GMLP_DIM:(gi + 1) * GMLP_DIM]
            vc = vc - jnp.mean(vc, axis=-1, keepdims=True)
            vn = vc * lax.rsqrt(jnp.mean(vc * vc, axis=-1, keepdims=True) + EPS)
            sg = _dot(ws_ref[gi], vn) + bs_ref[:, gi:gi + 1]
            parts.append(_gelu(u[:, gi * GMLP_DIM:(gi + 1) * GMLP_DIM]) * sg)
        mix = jnp.concatenate(parts, axis=1)
        y = _dot(mix, wout_ref[...])
        xo_ref[pl.ds(r0, RB), :] = x_ref[pl.ds(r0, RB), :] + gate * y
        return carry

    lax.fori_loop(0, T // RB, out_body, 0)


def _even_mixer(x_all, mod_l, n1g, win1, win2, wgu, bgu, glag, ws, bs, wout, s0, *, latent, x_first=None):
    if latent:
        T, nseq, blk0 = DEC_SEQ, DEC_BATCH, N_CTX // DEC_SEQ
        cond = lambda i: 1 + i
        s0_spec = pl.BlockSpec((None, 2, QK_W, GLA_DV), lambda i: (i, 0, 0, 0))
    else:
        T, nseq, blk0 = SEQ, BATCH, 0
        cond = lambda i: 0
        s0_spec = pl.BlockSpec((None, 2, QK_W, GLA_DV), lambda i: (0, 0, 0, 0))
    const2 = lambda i: (0, 0)
    body = functools.partial(_even_kernel, T=T)
    x_spec = pl.BlockSpec((T, D), lambda i: (blk0 + i, 0))
    if x_first is None:
        lead_specs, lead_args, aliases = [x_spec], (x_all,), {0: 0}
    elif x_all is None:
        lead_specs, lead_args, aliases = [pl.BlockSpec((T, D), lambda i: (i, 0))], (x_first,), {}
    else:
        lead_specs = [pl.BlockSpec(memory_space=pl.ANY), pl.BlockSpec((T, D), lambda i: (i, 0))]
        lead_args, aliases = (x_all, x_first), {0: 0}
        body = lambda dst_ref, *refs: _even_kernel(*refs, T=T)
    x_new, states = pl.pallas_call(
        body,
        grid=(nseq,),
        in_specs=lead_specs + [
            pl.BlockSpec((None, 6, D), lambda i: (cond(i), 0, 0)),
            _resident((1, D), const2),
            _resident((D, EVEN_W1), const2),
            _resident((D, EVEN_PACK - EVEN_W1), const2),
            _resident((128, 2 * QK_W), const2),
            _resident((1, 2 * QK_W), const2),
            _resident((1, GLA_DV), const2),
            _resident((GMLP_GROUPS, GMLP_CHUNK, GMLP_CHUNK), lambda i: (0, 0, 0)),
            _resident((GMLP_CHUNK, GMLP_GROUPS), const2),
            _resident((D, D), const2),
            s0_spec,
        ],
        out_specs=[
            x_spec,
            pl.BlockSpec((None, 2, QK_W, GLA_DV), lambda i: (i, 0, 0, 0)),
        ],
        out_shape=[
            jax.ShapeDtypeStruct((N_TOK, D), F32),
            jax.ShapeDtypeStruct((nseq, 2, QK_W, GLA_DV), F32),
        ],
        scratch_shapes=[
            pltpu.VMEM((T, EVEN_PACK), F32),
            pltpu.VMEM((T, 2 * QK_W), F32),
            pltpu.VMEM((T, V_W), F32),
            pltpu.VMEM((T, V_W), F32),
            pltpu.VMEM((2, GLA_DV, QK_W), F32),
        ],
        input_output_aliases=aliases,
        compiler_params=_cp("arbitrary", vmem_mib=48 if latent else 32),
        name="even_mixer_latent" if latent else "even_mixer_context",
    )(*lead_args, mod_l, n1g, win1, win2, wgu, bgu, glag, ws, bs, wout, s0)
    return x_new, states


QKV_TB = 512


def _qkv_kernel(x_ref, mod_ref, n1g_ref, win_ref, gq_ref, gk_ref, cos_ref, sin_ref, q_ref, k_ref, v_ref,
                ck_ref, cv_ref):
    shift, scale = mod_ref[0:1, :], mod_ref[1:2, :]
    h = _rms(x_ref[...], n1g_ref[...]) * (1.0 + scale) + shift
    p = _dot(h, win_ref[...])
    cos, sin = cos_ref[...], sin_ref[...]
    even_lane = lax.broadcasted_iota(jnp.int32, (1, HD), 1) % 2 == 0

    def rope(xn):
        swapped = jnp.where(even_lane, pltpu.roll(xn, HD - 1, axis=1), pltpu.roll(xn, 1, axis=1))
        return xn * cos + swapped * sin

    def emit(rotate, to_cache):
        for hh in range(ATT_HEADS):
            qn = _rms(p[:, hh * HD:(hh + 1) * HD], gq_ref[...])
            q_ref[:, hh * HD:(hh + 1) * HD] = (rotate(qn) * (HD ** -0.5)).astype(BF16)
        for hh in range(ATT_KV):
            kn = rotate(_rms(p[:, Q_W + hh * HD:Q_W + (hh + 1) * HD], gk_ref[...]))
            k_ref[:, hh * HD:(hh + 1) * HD] = kn
            if to_cache:
                for s in range(QKV_TB // SEQ):
                    ck_ref[s, pl.ds(hh, SEQ, stride=ATT_KV), :] = kn[s * SEQ:(s + 1) * SEQ]
                    cv_ref[s, pl.ds(hh, SEQ, stride=ATT_KV), :] = p[s * SEQ:(s + 1) * SEQ,
                                                                    Q_W + KV_W + hh * HD:Q_W + KV_W + (hh + 1) * HD]

    is_latent = pl.program_id(0) >= N_CTX // QKV_TB

    @pl.when(is_latent)
    def _():
        emit(rope, False)

    @pl.when(jnp.logical_not(is_latent))
    def _():
        emit(lambda xn: xn, True)

    v_ref[...] = p[:, Q_W + KV_W:]


def _qkv(x_all, mod_l, n1g, win, gq, gk, cos_tab, sin_tab, layer_i, caches=None):
    nb_ctx = N_CTX // QKV_TB
    per_seq = DEC_SEQ // QKV_TB
    cond = lambda i: jnp.where(i < nb_ctx, 0, 1 + (i - nb_ctx) // per_seq)
    tab = lambda i: jnp.where(i < nb_ctx, 0, 1 + (i - nb_ctx) % per_seq)
    const2 = lambda i: (0, 0)
    cache_spec = pl.BlockSpec((QKV_TB // SEQ, None, SEQ * ATT_KV, HD),
                              lambda i: (jnp.minimum(i, nb_ctx - 1), layer_i, 0, 0))
    cache_shape = jax.ShapeDtypeStruct((BATCH, DEPTH // 2, SEQ * ATT_KV, HD), F32)
    if caches is None:
        body, lead_specs, lead_args, aliases = _qkv_kernel, [], (), {}
    else:
        body = lambda ck_in, cv_in, *refs: _qkv_kernel(*refs)
        lead_specs = [pl.BlockSpec(memory_space=pl.ANY)] * 2
        lead_args, aliases = tuple(caches), {0: 3, 1: 4}
    return pl.pallas_call(
        body,
        grid=(N_TOK // QKV_TB,),
        in_specs=lead_specs + [
            pl.BlockSpec((QKV_TB, D), lambda i: (i, 0)),
            pl.BlockSpec((None, 6, D), lambda i: (cond(i), 0, 0)),
            _resident((1, D), const2),
            _resident((D, Q_W + 2 * KV_W), const2),
            _resident((1, HD), const2),
            _resident((1, HD), const2),
            pl.BlockSpec((None, QKV_TB, HD), lambda i: (tab(i), 0, 0)),
            pl.BlockSpec((None, QKV_TB, HD), lambda i: (tab(i), 0, 0)),
        ],
        out_specs=[
            pl.BlockSpec((QKV_TB, Q_W), lambda i: (i, 0)),
            pl.BlockSpec((QKV_TB, KV_W), lambda i: (i, 0)),
            pl.BlockSpec((QKV_TB, KV_W), lambda i: (i, 0)),
            cache_spec,
            cache_spec,
        ],
        out_shape=[
            jax.ShapeDtypeStruct((N_TOK, Q_W), BF16),
            jax.ShapeDtypeStruct((N_TOK, KV_W), F32),
            jax.ShapeDtypeStruct((N_TOK, KV_W), F32),
            cache_shape,
            cache_shape,
        ],
        input_output_aliases=aliases,
        compiler_params=_cp("arbitrary"),
        name="odd_qkv",
    )(*lead_args, x_all, mod_l, n1g, win, gq, gk, cos_tab, sin_tab)


ATT_TQ_LATENT = 512


def _attn_kernel(*refs, n_kv):
    q_ref = refs[0]
    kv_refs = refs[1:1 + 2 * n_kv]
    x_ref, mod_ref, wout_ref, xo_ref, att_scr = refs[1 + 2 * n_kv:]
    gate = mod_ref[2:3, :]
    def head(ref, kh):
        if ref.shape[1] == HD:
            return ref[pl.ds(kh, ref.shape[0] // ATT_KV, stride=ATT_KV), :].astype(BF16)
        return ref[:, kh * HD:(kh + 1) * HD].astype(BF16)

    for kh in range(ATT_KV):
        ks = [head(kv_refs[2 * s], kh) for s in range(n_kv)]
        vs = [jnp.concatenate([vh, jnp.ones_like(vh)], axis=1)
              for vh in (head(kv_refs[2 * s + 1], kh) for s in range(n_kv))]
        for g in range(ATT_G):
            hh = kh * ATT_G + g
            qh = q_ref[:, hh * HD:(hh + 1) * HD]
            ss = [_dot_nt(qh, kk) for kk in ks]
            m = ss[0].max(axis=-1, keepdims=True)
            for s in ss[1:]:
                m = jnp.maximum(m, s.max(axis=-1, keepdims=True))
            o = _dot(jnp.exp(ss[0] - m), vs[0])
            for s, vv in zip(ss[1:], vs[1:]):
                o = o + _dot(jnp.exp(s - m), vv)
            att_scr[:, hh * HD:(hh + 1) * HD] = o[:, :HD] / o[:, HD:HD + 1]
    y = _dot(att_scr[...], wout_ref[...])
    xo_ref[...] = x_ref[...] + gate * y


def _attention(x_all, mod_l, q, k, v, wout, cache_k=None, cache_v=None, layer_i=0):
    latent = cache_k is not None
    const2 = lambda *a: (0, 0)
    tq = ATT_TQ_LATENT if latent else SEQ
    if latent:
        nq = DEC_SEQ // tq
        row_blk = lambda b, j: (N_CTX // tq + b * nq + j, 0)
        grid = (DEC_BATCH, nq)
        kv_specs = [
            pl.BlockSpec((None, None, SEQ * ATT_KV, HD), lambda b, j: (b, layer_i, 0, 0)),
            pl.BlockSpec((None, None, SEQ * ATT_KV, HD), lambda b, j: (b, layer_i, 0, 0)),
            pl.BlockSpec((DEC_SEQ, KV_W), lambda b, j: (N_CTX // DEC_SEQ + b, 0)),
            pl.BlockSpec((DEC_SEQ, KV_W), lambda b, j: (N_CTX // DEC_SEQ + b, 0)),
        ]
        kv_args = (cache_k, cache_v, k, v)
        mod_spec = pl.BlockSpec((None, 6, D), lambda b, j: (1 + b, 0, 0))
        sem = ("arbitrary", "arbitrary")
        n_kv = 2
    else:
        row_blk = lambda i: (i, 0)
        grid = (BATCH,)
        kv_specs = [pl.BlockSpec((SEQ, KV_W), row_blk), pl.BlockSpec((SEQ, KV_W), row_blk)]
        kv_args = (k, v)
        mod_spec = pl.BlockSpec((None, 6, D), lambda i: (0, 0, 0))
        sem = ("arbitrary",)
        n_kv = 1
    n_in = 1 + len(kv_args)
    return pl.pallas_call(
        functools.partial(_attn_kernel, n_kv=n_kv),
        grid=grid,
        in_specs=[pl.BlockSpec((tq, Q_W), row_blk)] + kv_specs + [
            pl.BlockSpec((tq, D), row_blk),
            mod_spec,
            _resident((D, D), const2),
        ],
        out_specs=pl.BlockSpec((tq, D), row_blk),
        out_shape=jax.ShapeDtypeStruct((N_TOK, D), F32),
        scratch_shapes=[pltpu.VMEM((tq, Q_W), F32)],
        input_output_aliases={n_in: 0},
        compiler_params=_cp(*sem),
        name="attention_latent" if latent else "attention_context",
    )(q, *kv_args, x_all, mod_l, wout)


ROUTE_TB = 512
HALF_TOK = N_TOK // 2
M_E1, M_E2, M_G1, M_G2, M_R1, M_R2 = 0, 1, 2, 3, 4, 5


def _router_kernel(x_ref, mod_ref, n2g_ref, w2_ref, br_ref, h_ref, metat_ref, cnt_ref, run):
    @pl.when(pl.program_id(0) % (HALF_TOK // ROUTE_TB) == 0)
    def _():
        run[...] = jnp.zeros_like(run)

    shift, scale = mod_ref[3:4, :], mod_ref[4:5, :]
    h = _rms(x_ref[...], n2g_ref[...]) * (1.0 + scale) + shift
    _rows_to_tiles(h_ref, h)
    h_hi, h_lo = _split_bf16(h)
    dot = functools.partial(jnp.dot, preferred_element_type=F32)
    wide = dot(h_hi, w2_ref[...])
    logits = wide[:, :128] + wide[:, 128:] + dot(h_lo, w2_ref[:, :128]) + br_ref[...]
    lane = lax.broadcasted_iota(jnp.int32, logits.shape, 1).astype(F32)
    big = 1e4

    def first_argmax(vals):
        m = vals.max(axis=-1, keepdims=True)
        return m, jnp.where(vals == m, lane, big).min(axis=-1, keepdims=True)

    gl = jnp.where((lane >= N_EXP) & (lane < N_EXP + MOE_GROUPS), logits, NEG)
    gmax, glane = first_argmax(gl)
    g_p = 1.0 / jnp.exp(gl - gmax).sum(axis=-1, keepdims=True)
    lo = (glane - N_EXP) * MOE_PER_GROUP
    el = jnp.where((lane >= lo) & (lane < lo + MOE_PER_GROUP), logits, NEG)
    m1, i1 = first_argmax(el)
    m2, i2 = first_argmax(jnp.where(lane == i1, NEG, el))
    t = jnp.exp(m2 - m1)
    w1 = 1.0 / (1.0 + t)
    sel1, sel2 = lane == i1, lane == i2
    onehot = jnp.where(sel1 | sel2, 1.0, 0.0)
    ri = lax.broadcasted_iota(jnp.int32, (ROUTE_TB, ROUTE_TB), 0)
    rj = lax.broadcasted_iota(jnp.int32, (ROUTE_TB, ROUTE_TB), 1)
    before = _dot(jnp.where(ri > rj, 1.0, 0.0), onehot) + run[...]
    r1 = jnp.where(sel1, before, 0.0).sum(axis=-1, keepdims=True)
    r2 = jnp.where(sel2, before, 0.0).sum(axis=-1, keepdims=True)
    run[...] += onehot.sum(axis=0, keepdims=True)
    cnt_ref[...] = run[...]
    meta = jnp.zeros_like(logits)
    for j, val in enumerate([i1, i2, w1 * g_p, (t * w1) * g_p, r1, r2]):
        meta = jnp.where(lane == j, val, meta)
    metat_ref[...] = meta.T[0:8, :]


def _router(x_all, mod_l, n2g, wr, br):
    w2 = jnp.concatenate(_split_bf16(wr), axis=1)
    nb_ctx = N_CTX // ROUTE_TB
    per_seq = DEC_SEQ // ROUTE_TB
    cond = lambda i: jnp.where(i < nb_ctx, 0, 1 + (i - nb_ctx) // per_seq)
    const2 = lambda i: (0, 0)
    return pl.pallas_call(
        _router_kernel,
        grid=(N_TOK // ROUTE_TB,),
        in_specs=[
            pl.BlockSpec((ROUTE_TB, D), lambda i: (i, 0)),
            pl.BlockSpec((None, 6, D), lambda i: (cond(i), 0, 0)),
            _resident((1, D), const2),
            _resident((D, 256), const2),
            _resident((1, 128), const2),
        ],
        out_specs=[
            pl.BlockSpec((ROUTE_TB * 8, 128), lambda i: (i, 0)),
            pl.BlockSpec((8, ROUTE_TB), lambda i: (0, i)),
            pl.BlockSpec((None, 1, 128), lambda i: (i // (HALF_TOK // ROUTE_TB), 0, 0)),
        ],
        out_shape=[
            jax.ShapeDtypeStruct((N_TOK * 8, 128), F32),
            jax.ShapeDtypeStruct((8, N_TOK), F32),
            jax.ShapeDtypeStruct((2, 1, 128), F32),
        ],
        scratch_shapes=[pltpu.VMEM((1, 128), F32)],
        compiler_params=_cp("arbitrary"),
        name="moe_router",
    )(x_all, mod_l, n2g, w2, br)


EXP_TM = 128
N_ASSIGN = 2 * N_TOK
N_GROUPS = 2 * N_EXP
MAX_TILES = N_ASSIGN // EXP_TM + N_GROUPS
N_SORTED = MAX_TILES * EXP_TM
ORDER_BLK = 2048
CODE_PLANE = 2 * HALF_TOK
CODE_MASK = 8 * CODE_PLANE - 1
DUMMY8 = HALF_TOK * 8


def _order_kernel(pos1_ref, pos2_ref, pad_lo_ref, pad_hi_ref, src_ref):
    i = pl.program_id(0)
    local = (i % (HALF_TOK // ORDER_BLK)) * ORDER_BLK

    def body(t, carry):
        src_ref[pos1_ref[t]] = (local + t) * 8
        src_ref[pos2_ref[t]] = (local + t + CODE_PLANE) * 8
        return carry

    lax.fori_loop(0, ORDER_BLK, body, 0, unroll=16)

    @pl.when(i == 0)
    def _():
        def group(g, carry):
            def pad(p, c):
                src_ref[p] = DUMMY8
                return c
            return lax.fori_loop(pad_lo_ref[g], pad_hi_ref[g], pad, carry)

        lax.fori_loop(0, N_GROUPS, group, 0)


def _order(pos, pad_lo, pad_hi):
    return pl.pallas_call(
        _order_kernel,
        grid=(N_TOK // ORDER_BLK,),
        in_specs=[
            pl.BlockSpec((ORDER_BLK,), lambda i: (i,), memory_space=pltpu.SMEM),
            pl.BlockSpec((ORDER_BLK,), lambda i: (N_TOK // ORDER_BLK + i,), memory_space=pltpu.SMEM),
            pl.BlockSpec(memory_space=pltpu.SMEM),
            pl.BlockSpec(memory_space=pltpu.SMEM),
        ],
        out_specs=pl.BlockSpec(memory_space=pltpu.SMEM),
        out_shape=jax.ShapeDtypeStruct((N_SORTED,), jnp.int32),
        compiler_params=_cp("arbitrary"),
        name="moe_order",
    )(pos, pos, pad_lo, pad_hi)


GATE_BLK = CODE_PLANE + HALF_TOK
ACC_TOK = HALF_TOK + 64
GATHER_GROUP, ACC_GROUP = 16, 8


RES_TB = 256


def _experts_kernel(tile0_ref, ntile_ref, count_ref, src_ref, gs_ref, h_hbm, x_hbm, mod_ref, wg_ref, wu_ref, wd_ref,
                    *rest, final):
    if final:
        fg_ref, *dst_hbm = rest[:3]
        rest = rest[3:]
    else:
        dst_hbm, rest = rest[:1], rest[1:]
    h_res, acc, xbuf, ybuf, wgb, wub, wdb, xin, xout, sem, in_sem, out_sem = rest
    group = pl.program_id(0)
    expert = group % N_EXP
    half = group // N_EXP
    rows0 = pl.multiple_of(half * (HALF_TOK * 8), 8)

    @pl.when(expert == 0)
    def _():
        cp = pltpu.make_async_copy(h_hbm.at[pl.ds(rows0, HALF_TOK * 8), :], h_res.at[pl.ds(0, HALF_TOK * 8), :], sem)
        cp.start()
        h_res[pl.ds(DUMMY8, 8), :] = jnp.zeros((8, 128), F32)
        xbuf[...] = jnp.zeros_like(xbuf)

        def zero(i, carry):
            acc[pl.ds(pl.multiple_of(i * 512, 512), 512), :] = jnp.zeros((512, 128), F32)
            return carry

        lax.fori_loop(0, ACC_TOK * 8 // 512, zero, 0)
        cp.wait()

    n_tiles = ntile_ref[group]

    @pl.when(n_tiles > 0)
    def _():
        wgb[...] = wg_ref[...].astype(BF16)
        wub[...] = wu_ref[...].astype(BF16)
        wdb[...] = wd_ref[...].astype(BF16)

    row0 = tile0_ref[group] * EXP_TM
    row_end = row0 + count_ref[group]

    def process(base, rows):
        live = (jnp.clip(row_end - base, 0, rows) + GATHER_GROUP - 1) // GATHER_GROUP

        def gather(g, c):
            for i in range(GATHER_GROUP):
                r = g * GATHER_GROUP + i
                xbuf[pl.ds(pl.multiple_of(r * 8, 8), 8), :] = _tile_of(h_res, src_ref[base + r] & CODE_MASK)[...]
            return c

        lax.fori_loop(0, live, gather, 0)
        x = _tiles_to_rows(xbuf, rows).astype(BF16)
        hid = _silu(_dot(x, wgb[...])) * _dot(x, wub[...])
        _rows_to_tiles(ybuf, _dot(hid, wdb[...]))

        def accumulate(g, c):
            targets, values = [], []
            for i in range(ACC_GROUP):
                r = g * ACC_GROUP + i
                code = src_ref[base + r]
                target = _tile_of(acc, code & CODE_MASK)
                targets.append(target)
                values.append(target[...] + gs_ref[code >> 3] * ybuf[pl.ds(pl.multiple_of(r * 8, 8), 8), :])
            for target, value in zip(targets, values):
                target[...] = value
            return c

        lax.fori_loop(0, live * (GATHER_GROUP // ACC_GROUP), accumulate, 0)


    def pair_body(j, carry):
        process(row0 + j * (2 * EXP_TM), 2 * EXP_TM)
        return carry

    lax.fori_loop(0, n_tiles // 2, pair_body, 0)

    @pl.when(n_tiles % 2 == 1)
    def _():
        process(row0 + (n_tiles - 1) * EXP_TM, EXP_TM)

    def rows_of(first, blk):
        return pl.ds(pl.multiple_of(first + blk * RES_TB, RES_TB), RES_TB)

    def load_x(blk, slot):
        return pltpu.make_async_copy(x_hbm.at[rows_of(half * HALF_TOK, blk), :], xin.at[slot], in_sem.at[slot])

    def residual(blk, slot):
        cond = jnp.where(half == 0, 0, 1 + blk // (DEC_SEQ // RES_TB))
        gate = mod_ref[cond, 5:6, :]
        y = _tiles_to_rows(acc.at[pl.ds(pl.multiple_of(blk * (RES_TB * 8), RES_TB * 8), RES_TB * 8), :], RES_TB)
        x_new = xin[slot] + gate * y
        xout[slot] = _rms(x_new, fg_ref[...]) if final else x_new

    def epilogue(dst, first_row):
        def store_x(blk, slot):
            return pltpu.make_async_copy(xout.at[slot], dst.at[rows_of(first_row, blk), :], out_sem.at[slot])

        n_pairs = HALF_TOK // RES_TB // 2
        load_x(0, 0).start()

        def pair(p, carry):
            for slot in range(2):
                blk = 2 * p + slot
                if slot == 0:
                    load_x(blk + 1, 1).start()
                else:
                    @pl.when(p + 1 < n_pairs)
                    def _():
                        load_x(blk + 1, 0).start()
                load_x(blk, slot).wait()

                @pl.when(p > 0)
                def _():
                    store_x(blk - 2, slot).wait()

                residual(blk, slot)
                store_x(blk, slot).start()
            return carry

        lax.fori_loop(0, n_pairs, pair, 0)
        store_x(2 * n_pairs - 2, 0).wait()
        store_x(2 * n_pairs - 1, 1).wait()

    if final:
        for which in range(2):
            @pl.when((expert == N_EXP - 1) & (half == which))
            def _():
                epilogue(dst_hbm[which], 0)
    else:
        @pl.when(expert == N_EXP - 1)
        def _():
            epilogue(dst_hbm[0], half * HALF_TOK)


def _experts(tile0, n_tiles, counts, src, gs, h, x_all, mod_l, wg, wu, wd, layer, final_g=None):
    final = final_g is not None
    wmap = lambda g, t0, nt, cnt, src: (layer, g % N_EXP, 0, 0)
    any_spec = pl.BlockSpec(memory_space=pl.ANY)
    extra_specs = [pl.BlockSpec((1, D), lambda g, t0, nt, cnt, src: (0, 0))] if final else []
    extra_args = (final_g,) if final else ()
    return pl.pallas_call(
        functools.partial(_experts_kernel, final=final),
        grid_spec=pltpu.PrefetchScalarGridSpec(
            num_scalar_prefetch=4,
            grid=(N_GROUPS,),
            in_specs=[
                pl.BlockSpec((GATE_BLK,), lambda g, t0, nt, cnt, src: (g // N_EXP,), memory_space=pltpu.SMEM),
                pl.BlockSpec(memory_space=pl.ANY),
                pl.BlockSpec(memory_space=pl.ANY),
                pl.BlockSpec((8, 6, D), lambda g, t0, nt, cnt, src: (0, 0, 0)),
                pl.BlockSpec((None, None, D, D_EXP), wmap),
                pl.BlockSpec((None, None, D, D_EXP), wmap),
                pl.BlockSpec((None, None, D_EXP, D), wmap),
            ] + extra_specs,
            out_specs=[any_spec, any_spec] if final else any_spec,
            scratch_shapes=[
                pltpu.VMEM((ACC_TOK * 8, 128), F32),
                pltpu.VMEM((ACC_TOK * 8, 128), F32),
                pltpu.VMEM((2 * EXP_TM * 8, 128), F32),
                pltpu.VMEM((2 * EXP_TM * 8, 128), F32),
                pltpu.VMEM((D, D_EXP), BF16),
                pltpu.VMEM((D, D_EXP), BF16),
                pltpu.VMEM((D_EXP, D), BF16),
                pltpu.VMEM((2, RES_TB, D), F32),
                pltpu.VMEM((2, RES_TB, D), F32),
                pltpu.SemaphoreType.DMA,
                pltpu.SemaphoreType.DMA((2,)),
                pltpu.SemaphoreType.DMA((2,)),
            ],
        ),
        out_shape=([jax.ShapeDtypeStruct((HALF_TOK, D), F32)] * 2 if final
                   else jax.ShapeDtypeStruct((N_TOK, D), F32)),
        input_output_aliases={} if final else {6: 0},
        compiler_params=_cp("arbitrary", vmem_mib=56),
        name="moe_experts_final" if final else "moe_experts",
    )(tile0, n_tiles, counts, src, gs, h, x_all, mod_l, wg, wu, wd, *extra_args)


def _moe(x_all, mod_l, n2g, wr, br, wg, wu, wd, layer, final_g=None):
    h, metat, cnt = _router(x_all, mod_l, n2g, wr, br)
    counts = cnt[:, 0, :N_EXP].astype(jnp.int32).reshape(N_GROUPS)
    padded = (counts + EXP_TM - 1) // EXP_TM * EXP_TM
    ends = jnp.cumsum(padded)
    offs = ends - padded
    rec = metat.astype(jnp.int32)
    half = (jnp.arange(N_TOK, dtype=jnp.int32) // HALF_TOK)[None, :]
    group = rec[M_E1:M_E2 + 1] + N_EXP * half
    is_group = group[None] == jnp.arange(N_GROUPS, dtype=jnp.int32)[:, None, None]
    pos = jnp.sum(jnp.where(is_group, offs[:, None, None], 0), axis=0) + rec[M_R1:M_R2 + 1]
    live_end = offs + (counts + GATHER_GROUP - 1) // GATHER_GROUP * GATHER_GROUP
    src = _order(pos.reshape(N_ASSIGN), offs + counts, live_end)
    g12 = metat[M_G1:M_G2 + 1].reshape(2, 2, HALF_TOK)
    gates = jnp.concatenate([g12[0], jnp.zeros((2, CODE_PLANE - HALF_TOK), F32), g12[1]], axis=1)
    return _experts(offs // EXP_TM, padded // EXP_TM, counts, src, gates.reshape(2 * GATE_BLK), h, x_all, mod_l,
                    wg, wu, wd, layer, final_g)


def _rope_tables():
    pos = jnp.arange(DEC_SEQ)
    row = (pos // GRID_W).astype(F32)
    col = (pos % GRID_W).astype(F32)
    n_freq = HD // 4
    inv = ROPE_THETA ** (-jnp.arange(n_freq, dtype=F32) / n_freq)
    ang = jnp.concatenate([row[:, None] * inv, col[:, None] * inv], axis=-1)
    cos = jnp.repeat(jnp.cos(ang), 2, axis=-1)
    sin = jnp.repeat(jnp.sin(ang), 2, axis=-1) * jnp.tile(jnp.array([-1.0, 1.0], F32), HD // 2)
    nblk = DEC_SEQ // QKV_TB
    cos_tab = jnp.concatenate([jnp.ones((1, QKV_TB, HD), F32), cos.reshape(nblk, QKV_TB, HD)], axis=0)
    sin_tab = jnp.concatenate([jnp.zeros((1, QKV_TB, HD), F32), sin.reshape(nblk, QKV_TB, HD)], axis=0)
    return cos_tab, sin_tab


def kernel(x_prompt, x_sample, state_gla, cache_k, cache_v, c, c_ctx, w_mod, b_mod, norm1_g, norm2_g,
           w_in_even, w_gate_up, b_gate_up, gla_norm_g, w_spatial, b_spatial, w_out_even,
           w_in_odd, q_norm_g, k_norm_g, w_out_odd, w_router_group, b_router_group,
           w_router_expert, b_router_expert, w_exp_gate, w_exp_up, w_exp_down, final_norm_g):
    x_all = None
    cond8 = jnp.concatenate([c_ctx[None], c, jnp.zeros((3, D), F32)], axis=0)
    mod = _modulation(cond8, w_mod, b_mod)
    cos_tab, sin_tab = _rope_tables()
    zero_state = jnp.zeros((1, 2, QK_W, GLA_DV), F32)
    state_in = state_gla.reshape(DEC_BATCH, -1, 2, QK_W, GLA_DV)
    cache_k2 = cache_k.reshape(DEC_BATCH, -1, SEQ * ATT_KV, HD)
    cache_v2 = cache_v.reshape(DEC_BATCH, -1, SEQ * ATT_KV, HD)

    gla_states, caches = [], None
    for l in range(DEPTH):
        i = l // 2
        n1g = norm1_g[l][None]
        if l % 2 == 0:
            w = w_in_even[i]
            win1, win2 = w[:, :EVEN_W1].astype(BF16), w[:, EVEN_SPLIT:].astype(BF16)
            wgu = jnp.zeros((128, 2 * QK_W), F32)
            wgu = wgu.at[0:GLA_RANK, 0:QK_W].set(w_gate_up[i, 0])
            wgu = wgu.at[GLA_RANK:2 * GLA_RANK, QK_W:].set(w_gate_up[i, 1]).astype(BF16)
            bgu = b_gate_up[i].reshape(1, 2 * QK_W)
            args = (mod[l], n1g, win1, win2, wgu, bgu, gla_norm_g[i][None], w_spatial[i].astype(BF16),
                    b_spatial[i].T, w_out_even[i].astype(BF16))
            first = l == 0
            x_all, st = _even_mixer(x_all, *args, zero_state, latent=False,
                                    x_first=x_prompt.reshape(N_CTX, D) if first else None)
            gla_states.append(st)
            x_all, _ = _even_mixer(x_all, *args, state_in[:, i], latent=True,
                                   x_first=x_sample.reshape(N_LAT, D) if first else None)
        else:
            q, k, v, *caches = _qkv(x_all, mod[l], n1g, w_in_odd[i].astype(BF16), q_norm_g[i][None],
                                    k_norm_g[i][None], cos_tab, sin_tab, i, caches)
            wout = w_out_odd[i].astype(BF16)
            x_all = _attention(x_all, mod[l], q, k, v, wout)
            x_all = _attention(x_all, mod[l], q, k, v, wout, cache_k2, cache_v2, layer_i=i)
        wr = jnp.concatenate([w_router_expert[l], w_router_group[l],
                              jnp.zeros((D, 128 - N_EXP - MOE_GROUPS), F32)], axis=1)
        br = jnp.concatenate([b_router_expert[l], b_router_group[l],
                              jnp.zeros((128 - N_EXP - MOE_GROUPS,), F32)])[None]
        x_all = _moe(x_all, mod[l], norm2_g[l][None], wr, br, w_exp_gate, w_exp_up, w_exp_down, l,
                     final_norm_g[None] if l == DEPTH - 1 else None)

    y_prompt = x_all[0].reshape(BATCH, SEQ, D)
    y_sample = x_all[1].reshape(DEC_BATCH, DEC_SEQ, D)
    new_state = jnp.stack(gla_states, axis=1).reshape(BATCH, -1, 2, GLA_HEADS, GLA_DK, GLA_DV)
    new_k, new_v = (a.reshape(BATCH, DEPTH // 2, SEQ, ATT_KV, HD) for a in caches)
    return (y_prompt, y_sample, new_state, new_k, new_v)
```

```python
import functools

import jax
import jax.numpy as jnp
import numpy as np
from jax import lax
from jax.experimental import pallas as pl
from jax.experimental.pallas import tpu as pltpu

F32 = jnp.float32
BF16 = jnp.bfloat16

D = 1024
BATCH, SEQ = 16, 256
DEC_BATCH, DEC_SEQ = 4, 1024
N_CTX = BATCH * SEQ
N_LAT = DEC_BATCH * DEC_SEQ
N_TOK = N_CTX + N_LAT
DEPTH = 4
EPS = 1e-6
GRID_W = 64
ROPE_THETA = 10000.0

GLA_HEADS, GLA_DK, GLA_DV, GLA_RANK, GLA_CHUNK, GLA_TAU = 4, 64, 128, 16, 128, 16.0
QK_W = GLA_HEADS * GLA_DK
V_W = GLA_HEADS * GLA_DV
GMLP_GROUPS, GMLP_DIM, GMLP_CHUNK = 4, 128, 128
GMLP_W = GMLP_GROUPS * GMLP_DIM
C_Q, C_K, C_V, C_G, C_A, C_U, C_VG = 0, 256, 512, 1024, 1536, 1664, 2176
EVEN_SPLIT = 1568
EVEN_W1 = C_U
EVEN_PACK = 2688

ATT_HEADS, ATT_KV, HD = 8, 2, 128
ATT_G = ATT_HEADS // ATT_KV
Q_W = ATT_HEADS * HD
KV_W = ATT_KV * HD

MOE_GROUPS, MOE_PER_GROUP = 4, 8
N_EXP = MOE_GROUPS * MOE_PER_GROUP
D_EXP = D // 4
NEG = -1e30

MIB = 1024 * 1024
V7X_VMEM_MIB = 64
LANES, SUBLANES = 128, 8


def _cp(*sem, vmem_mib=32):
    assert vmem_mib < V7X_VMEM_MIB
    return pltpu.CompilerParams(dimension_semantics=sem, vmem_limit_bytes=vmem_mib * MIB)


def _dot(a, b):
    return jnp.dot(a.astype(BF16), b.astype(BF16), preferred_element_type=F32)


def _dot_nt(a, b):
    return lax.dot_general(a.astype(BF16), b.astype(BF16), (((1,), (1,)), ((), ())),
                           preferred_element_type=F32)


def _dot_tn(a, b):
    return lax.dot_general(a.astype(BF16), b.astype(BF16), (((0,), (0,)), ((), ())),
                           preferred_element_type=F32)


def _rms(x, g):
    return x * lax.rsqrt(jnp.mean(x * x, axis=-1, keepdims=True) + EPS) * g


def _silu(x):
    return x * jax.nn.sigmoid(x)


def _gelu(x):
    return 0.5 * x * (1.0 + jnp.tanh(np.sqrt(2.0 / np.pi).astype(np.float32) * (x + 0.044715 * (x * x * x))))


def _log_sigmoid(z):
    return jnp.minimum(z, 0.0) - jnp.log(1.0 + jnp.exp(-jnp.abs(z)))


assert D == LANES * SUBLANES


def _rows_to_tiles(ref, x):
    rows = x.shape[0]
    for j in range(SUBLANES):
        ref[pl.ds(j, rows, stride=SUBLANES), :] = x[:, j * LANES:(j + 1) * LANES]


def _tiles_to_rows(ref, rows):
    return jnp.concatenate([ref[pl.ds(j, rows, stride=SUBLANES), :] for j in range(SUBLANES)], axis=1)


def _tile_of(ref, row8):
    return ref.at[pl.ds(pl.multiple_of(row8, SUBLANES), SUBLANES), :]


def _resident(shape, index_map):
    return pl.BlockSpec(shape, index_map, pipeline_mode=pl.Buffered(1))


def _split_bf16(x):
    hi = x.astype(BF16)
    return hi, (x - hi.astype(F32)).astype(BF16)


def _mod_kernel(cond_ref, w_ref, b_ref, o_ref):
    s_hi, s_lo = _split_bf16(_silu(cond_ref[...]))
    w_hi, w_lo = _split_bf16(w_ref[...])
    dot = functools.partial(jnp.dot, preferred_element_type=F32)
    both = dot(jnp.concatenate([s_hi, s_lo], axis=0), w_hi)
    o_ref[...] = both[0:8] + both[8:16] + dot(s_hi, w_lo) + b_ref[...]


def _modulation(cond8, w_mod, b_mod):
    tn = 2048
    out = pl.pallas_call(
        _mod_kernel,
        grid=(DEPTH, 6 * D // tn),
        in_specs=[
            pl.BlockSpec((8, D), lambda l, j: (0, 0)),
            pl.BlockSpec((None, D, tn), lambda l, j: (l, 0, j)),
            pl.BlockSpec((None, 1, tn), lambda l, j: (l, 0, j)),
        ],
        out_specs=pl.BlockSpec((None, 8, tn), lambda l, j: (l, 0, j)),
        out_shape=jax.ShapeDtypeStruct((DEPTH, 8, 6 * D), F32),
        compiler_params=_cp("arbitrary", "arbitrary"),
        name="adaln_mod",
    )(cond8, w_mod, b_mod.reshape(DEPTH, 1, 6 * D))
    return out.reshape(DEPTH, 8, 6, D)


def _even_kernel(x_ref, mod_ref, n1g_ref, win1_ref, win2_ref, wgu_ref, bgu_ref, glag_ref, ws_ref, bs_ref,
                 wout_ref, s0_ref, xo_ref, st_ref, proj, la, o_f, o_b, st_scr, *, T):
    n_chunks = T // GLA_CHUNK
    shift, scale, gate = mod_ref[0:1, :], mod_ref[1:2, :], mod_ref[2:3, :]
    RB = 128
    PROJ_RB = 256

    def proj_body(r, carry):
        r0 = pl.multiple_of(r * PROJ_RB, PROJ_RB)
        h = _rms(x_ref[pl.ds(r0, PROJ_RB), :], n1g_ref[...]) * (1.0 + scale) + shift
        hb = h.astype(BF16)
        p = _dot(hb, win1_ref[...])
        proj[pl.ds(r0, PROJ_RB), 0:EVEN_W1] = p
        proj[pl.ds(r0, PROJ_RB), EVEN_W1:] = _dot(hb, win2_ref[...])
        z = _dot(p[:, C_A:C_A + 128], wgu_ref[...]) + bgu_ref[...]
        la[pl.ds(r0, PROJ_RB), :] = _log_sigmoid(z) * (1.0 / GLA_TAU)
        return carry

    lax.fori_loop(0, T // PROJ_RB, proj_body, 0)

    st_scr[0] = s0_ref[0].T
    st_scr[1] = s0_ref[1].T

    ci = lax.broadcasted_iota(jnp.int32, (GLA_CHUNK, GLA_CHUNK), 0)
    cj = lax.broadcasted_iota(jnp.int32, (GLA_CHUNK, GLA_CHUNK), 1)
    tri = (jnp.where(ci >= cj, 1.0, 0.0).astype(BF16), jnp.where(ci <= cj, 1.0, 0.0).astype(BF16))
    ai = lax.broadcasted_iota(jnp.int32, (GLA_HEADS * GLA_CHUNK, GLA_CHUNK), 0) % GLA_CHUNK
    aj = lax.broadcasted_iota(jnp.int32, (GLA_HEADS * GLA_CHUNK, GLA_CHUNK), 1)
    amask = (ai >= aj, ai <= aj)
    lane_head = lax.broadcasted_iota(jnp.int32, (1, QK_W), 1) // GLA_DK
    hmask = [jnp.where(lane_head == h, 1.0, 0.0) for h in range(GLA_HEADS)]

    def chunk_body(i, carry):
        for d in range(2):
            c = i if d == 0 else n_chunks - 1 - i
            r0 = pl.multiple_of(c * GLA_CHUNK, GLA_CHUNK)
            q = proj[pl.ds(r0, GLA_CHUNK), C_Q:C_Q + QK_W] * (GLA_DK ** -0.5)
            k = proj[pl.ds(r0, GLA_CHUNK), C_K:C_K + QK_W]
            v = proj[pl.ds(r0, GLA_CHUNK), C_V:C_V + V_W]
            lac = la[pl.ds(r0, GLA_CHUNK), d * QK_W:(d + 1) * QK_W]
            hi = lac.astype(BF16)
            lo = (lac - hi.astype(F32)).astype(BF16)
            b = (jnp.dot(tri[d], hi, preferred_element_type=F32)
                 + jnp.dot(tri[d], lo, preferred_element_type=F32))
            bend = b[GLA_CHUNK - 1:GLA_CHUNK, :] if d == 0 else b[0:1, :]
            qe = q * jnp.exp(b)
            ke = k * jnp.exp(-b)
            kd = k * jnp.exp(bend - b)
            st = st_scr[d]
            qstack = jnp.concatenate([qe * hmask[h] for h in range(GLA_HEADS)], axis=0).astype(BF16)
            att = jnp.where(amask[d], _dot_nt(qstack, ke), 0.0)
            inter = _dot_nt(qstack, st)
            outs = []
            for h in range(GLA_HEADS):
                rows = slice(h * GLA_CHUNK, (h + 1) * GLA_CHUNK)
                outs.append(_dot(att[rows], v[:, h * GLA_DV:(h + 1) * GLA_DV]) + inter[rows])
            o = jnp.concatenate(outs, axis=1)
            if d == 0:
                o_f[pl.ds(r0, GLA_CHUNK), :] = o
            else:
                o_b[pl.ds(r0, GLA_CHUNK), :] = o
            vstack = jnp.concatenate([v[:, h * GLA_DV:(h + 1) * GLA_DV] for h in range(GLA_HEADS)], axis=0)
            kstack = jnp.concatenate([kd * hmask[h] for h in range(GLA_HEADS)], axis=0)
            st_scr[d] = st * jnp.exp(bend) + _dot_tn(vstack, kstack)
        return carry

    lax.fori_loop(0, n_chunks, chunk_body, 0, unroll=2)
    st_ref[0] = st_scr[0].T
    st_ref[1] = st_scr[1].T

    def out_body(r, carry):
        r0 = pl.multiple_of(r * RB, RB)
        osum = o_f[pl.ds(r0, RB), :] + o_b[pl.ds(r0, RB), :]
        g = proj[pl.ds(r0, RB), C_G:C_G + V_W]
        u = proj[pl.ds(r0, RB), C_U:C_U + GMLP_W]
        vg = _gelu(proj[pl.ds(r0, RB), C_VG:C_VG + GMLP_W])
        parts = []
        for h in range(GLA_HEADS):
            oh = osum[:, h * GLA_DV:(h + 1) * GLA_DV]
            parts.append(_rms(oh, glag_ref[...]) * _silu(g[:, h * GLA_DV:(h + 1) * GLA_DV]))
        for gi in range(GMLP_GROUPS):
            vc = vg[:, gi * GMLP_DIM:(gi + 1) * GMLP_DIM]
            vc = vc - jnp.mean(vc, axis=-1, keepdims=True)
            vn = vc * lax.rsqrt(jnp.mean(vc * vc, axis=-1, keepdims=True) + EPS)
            sg = _dot(ws_ref[gi], vn) + bs_ref[:, gi:gi + 1]
            parts.append(_gelu(u[:, gi * GMLP_DIM:(gi + 1) * GMLP_DIM]) * sg)
        mix = jnp.concatenate(parts, axis=1)
        y = _dot(mix, wout_ref[...])
        xo_ref[pl.ds(r0, RB), :] = x_ref[pl.ds(r0, RB), :] + gate * y
        return carry

    lax.fori_loop(0, T // RB, out_body, 0)


def _even_mixer(x_all, mod_l, n1g, win1, win2, wgu, bgu, glag, ws, bs, wout, s0, *, latent, x_first=None):
    if latent:
        T, nseq, blk0 = DEC_SEQ, DEC_BATCH, N_CTX // DEC_SEQ
        cond = lambda i: 1 + i
        s0_spec = pl.BlockSpec((None, 2, QK_W, GLA_DV), lambda i: (i, 0, 0, 0))
    else:
        T, nseq, blk0 = SEQ, BATCH, 0
        cond = lambda i: 0
        s0_spec = pl.BlockSpec((None, 2, QK_W, GLA_DV), lambda i: (0, 0, 0, 0))
    const2 = lambda i: (0, 0)
    body = functools.partial(_even_kernel, T=T)
    x_spec = pl.BlockSpec((T, D), lambda i: (blk0 + i, 0))
    if x_first is None:
        lead_specs, lead_args, aliases = [x_spec], (x_all,), {0: 0}
    elif x_all is None:
        lead_specs, lead_args, aliases = [pl.BlockSpec((T, D), lambda i: (i, 0))], (x_first,), {}
    else:
        lead_specs = [pl.BlockSpec(memory_space=pl.ANY), pl.BlockSpec((T, D), lambda i: (i, 0))]
        lead_args, aliases = (x_all, x_first), {0: 0}
        body = lambda dst_ref, *refs: _even_kernel(*refs, T=T)
    x_new, states = pl.pallas_call(
        body,
        grid=(nseq,),
        in_specs=lead_specs + [
            pl.BlockSpec((None, 6, D), lambda i: (cond(i), 0, 0)),
            _resident((1, D), const2),
            _resident((D, EVEN_W1), const2),
            _resident((D, EVEN_PACK - EVEN_W1), const2),
            _resident((128, 2 * QK_W), const2),
            _resident((1, 2 * QK_W), const2),
            _resident((1, GLA_DV), const2),
            _resident((GMLP_GROUPS, GMLP_CHUNK, GMLP_CHUNK), lambda i: (0, 0, 0)),
            _resident((GMLP_CHUNK, GMLP_GROUPS), const2),
            _resident((D, D), const2),
            s0_spec,
        ],
        out_specs=[
            x_spec,
            pl.BlockSpec((None, 2, QK_W, GLA_DV), lambda i: (i, 0, 0, 0)),
        ],
        out_shape=[
            jax.ShapeDtypeStruct((N_TOK, D), F32),
            jax.ShapeDtypeStruct((nseq, 2, QK_W, GLA_DV), F32),
        ],
        scratch_shapes=[
            pltpu.VMEM((T, EVEN_PACK), F32),
            pltpu.VMEM((T, 2 * QK_W), F32),
            pltpu.VMEM((T, V_W), F32),
            pltpu.VMEM((T, V_W), F32),
            pltpu.VMEM((2, GLA_DV, QK_W), F32),
        ],
        input_output_aliases=aliases,
        compiler_params=_cp("arbitrary", vmem_mib=48 if latent else 32),
        name="even_mixer_latent" if latent else "even_mixer_context",
    )(*lead_args, mod_l, n1g, win1, win2, wgu, bgu, glag, ws, bs, wout, s0)
    return x_new, states


QKV_TB = 512


def _qkv_kernel(x_ref, mod_ref, n1g_ref, win_ref, gq_ref, gk_ref, cos_ref, sin_ref, q_ref, k_ref, v_ref,
                ck_ref, cv_ref):
    shift, scale = mod_ref[0:1, :], mod_ref[1:2, :]
    h = _rms(x_ref[...], n1g_ref[...]) * (1.0 + scale) + shift
    p = _dot(h, win_ref[...])
    cos, sin = cos_ref[...], sin_ref[...]
    even_lane = lax.broadcasted_iota(jnp.int32, (1, HD), 1) % 2 == 0

    def rope(xn):
        swapped = jnp.where(even_lane, pltpu.roll(xn, HD - 1, axis=1), pltpu.roll(xn, 1, axis=1))
        return xn * cos + swapped * sin

    def emit(rotate, to_cache):
        for hh in range(ATT_HEADS):
            qn = _rms(p[:, hh * HD:(hh + 1) * HD], gq_ref[...])
            q_ref[:, hh * HD:(hh + 1) * HD] = (rotate(qn) * (HD ** -0.5)).astype(BF16)
        for hh in range(ATT_KV):
            kn = rotate(_rms(p[:, Q_W + hh * HD:Q_W + (hh + 1) * HD], gk_ref[...]))
            k_ref[:, hh * HD:(hh + 1) * HD] = kn
            if to_cache:
                for s in range(QKV_TB // SEQ):
                    ck_ref[s, pl.ds(hh, SEQ, stride=ATT_KV), :] = kn[s * SEQ:(s + 1) * SEQ]
                    cv_ref[s, pl.ds(hh, SEQ, stride=ATT_KV), :] = p[s * SEQ:(s + 1) * SEQ,
                                                                    Q_W + KV_W + hh * HD:Q_W + KV_W + (hh + 1) * HD]

    is_latent = pl.program_id(0) >= N_CTX // QKV_TB

    @pl.when(is_latent)
    def _():
        emit(rope, False)

    @pl.when(jnp.logical_not(is_latent))
    def _():
        emit(lambda xn: xn, True)

    v_ref[...] = p[:, Q_W + KV_W:]


def _qkv(x_all, mod_l, n1g, win, gq, gk, cos_tab, sin_tab, layer_i, caches=None):
    nb_ctx = N_CTX // QKV_TB
    per_seq = DEC_SEQ // QKV_TB
    cond = lambda i: jnp.where(i < nb_ctx, 0, 1 + (i - nb_ctx) // per_seq)
    tab = lambda i: jnp.where(i < nb_ctx, 0, 1 + (i - nb_ctx) % per_seq)
    const2 = lambda i: (0, 0)
    cache_spec = pl.BlockSpec((QKV_TB // SEQ, None, SEQ * ATT_KV, HD),
                              lambda i: (jnp.minimum(i, nb_ctx - 1), layer_i, 0, 0))
    cache_shape = jax.ShapeDtypeStruct((BATCH, DEPTH // 2, SEQ * ATT_KV, HD), F32)
    if caches is None:
        body, lead_specs, lead_args, aliases = _qkv_kernel, [], (), {}
    else:
        body = lambda ck_in, cv_in, *refs: _qkv_kernel(*refs)
        lead_specs = [pl.BlockSpec(memory_space=pl.ANY)] * 2
        lead_args, aliases = tuple(caches), {0: 3, 1: 4}
    return pl.pallas_call(
        body,
        grid=(N_TOK // QKV_TB,),
        in_specs=lead_specs + [
            pl.BlockSpec((QKV_TB, D), lambda i: (i, 0)),
            pl.BlockSpec((None, 6, D), lambda i: (cond(i), 0, 0)),
            _resident((1, D), const2),
            _resident((D, Q_W + 2 * KV_W), const2),
            _resident((1, HD), const2),
            _resident((1, HD), const2),
            pl.BlockSpec((None, QKV_TB, HD), lambda i: (tab(i), 0, 0)),
            pl.BlockSpec((None, QKV_TB, HD), lambda i: (tab(i), 0, 0)),
        ],
        out_specs=[
            pl.BlockSpec((QKV_TB, Q_W), lambda i: (i, 0)),
            pl.BlockSpec((QKV_TB, KV_W), lambda i: (i, 0)),
            pl.BlockSpec((QKV_TB, KV_W), lambda i: (i, 0)),
            cache_spec,
            cache_spec,
        ],
        out_shape=[
            jax.ShapeDtypeStruct((N_TOK, Q_W), BF16),
            jax.ShapeDtypeStruct((N_TOK, KV_W), F32),
            jax.ShapeDtypeStruct((N_TOK, KV_W), F32),
            cache_shape,
            cache_shape,
        ],
        input_output_aliases=aliases,
        compiler_params=_cp("arbitrary"),
        name="odd_qkv",
    )(*lead_args, x_all, mod_l, n1g, win, gq, gk, cos_tab, sin_tab)


ATT_TQ_LATENT = 512


def _attn_kernel(*refs, n_kv):
    q_ref = refs[0]
    kv_refs = refs[1:1 + 2 * n_kv]
    x_ref, mod_ref, wout_ref, xo_ref, att_scr = refs[1 + 2 * n_kv:]
    gate = mod_ref[2:3, :]
    def head(ref, kh):
        if ref.shape[1] == HD:
            return ref[pl.ds(kh, ref.shape[0] // ATT_KV, stride=ATT_KV), :].astype(BF16)
        return ref[:, kh * HD:(kh + 1) * HD].astype(BF16)

    for kh in range(ATT_KV):
        ks = [head(kv_refs[2 * s], kh) for s in range(n_kv)]
        vs = [jnp.concatenate([vh, jnp.ones_like(vh)], axis=1)
              for vh in (head(kv_refs[2 * s + 1], kh) for s in range(n_kv))]
        for g in range(ATT_G):
            hh = kh * ATT_G + g
            qh = q_ref[:, hh * HD:(hh + 1) * HD]
            ss = [_dot_nt(qh, kk) for kk in ks]
            m = ss[0].max(axis=-1, keepdims=True)
            for s in ss[1:]:
                m = jnp.maximum(m, s.max(axis=-1, keepdims=True))
            o = _dot(jnp.exp(ss[0] - m), vs[0])
            for s, vv in zip(ss[1:], vs[1:]):
                o = o + _dot(jnp.exp(s - m), vv)
            att_scr[:, hh * HD:(hh + 1) * HD] = o[:, :HD] / o[:, HD:HD + 1]
    y = _dot(att_scr[...], wout_ref[...])
    xo_ref[...] = x_ref[...] + gate * y


def _attention(x_all, mod_l, q, k, v, wout, cache_k=None, cache_v=None, layer_i=0):
    latent = cache_k is not None
    const2 = lambda *a: (0, 0)
    tq = ATT_TQ_LATENT if latent else SEQ
    if latent:
        nq = DEC_SEQ // tq
        row_blk = lambda b, j: (N_CTX // tq + b * nq + j, 0)
        grid = (DEC_BATCH, nq)
        kv_specs = [
            pl.BlockSpec((None, None, SEQ * ATT_KV, HD), lambda b, j: (b, layer_i, 0, 0)),
            pl.BlockSpec((None, None, SEQ * ATT_KV, HD), lambda b, j: (b, layer_i, 0, 0)),
            pl.BlockSpec((DEC_SEQ, KV_W), lambda b, j: (N_CTX // DEC_SEQ + b, 0)),
            pl.BlockSpec((DEC_SEQ, KV_W), lambda b, j: (N_CTX // DEC_SEQ + b, 0)),
        ]
        kv_args = (cache_k, cache_v, k, v)
        mod_spec = pl.BlockSpec((None, 6, D), lambda b, j: (1 + b, 0, 0))
        sem = ("arbitrary", "arbitrary")
        n_kv = 2
    else:
        row_blk = lambda i: (i, 0)
        grid = (BATCH,)
        kv_specs = [pl.BlockSpec((SEQ, KV_W), row_blk), pl.BlockSpec((SEQ, KV_W), row_blk)]
        kv_args = (k, v)
        mod_spec = pl.BlockSpec((None, 6, D), lambda i: (0, 0, 0))
        sem = ("arbitrary",)
        n_kv = 1
    n_in = 1 + len(kv_args)
    return pl.pallas_call(
        functools.partial(_attn_kernel, n_kv=n_kv),
        grid=grid,
        in_specs=[pl.BlockSpec((tq, Q_W), row_blk)] + kv_specs + [
            pl.BlockSpec((tq, D), row_blk),
            mod_spec,
            _resident((D, D), const2),
        ],
        out_specs=pl.BlockSpec((tq, D), row_blk),
        out_shape=jax.ShapeDtypeStruct((N_TOK, D), F32),
        scratch_shapes=[pltpu.VMEM((tq, Q_W), F32)],
        input_output_aliases={n_in: 0},
        compiler_params=_cp(*sem),
        name="attention_latent" if latent else "attention_context",
    )(q, *kv_args, x_all, mod_l, wout)


ROUTE_TB = 512
HALF_TOK = N_TOK // 2
M_E1, M_E2, M_G1, M_G2, M_R1, M_R2 = 0, 1, 2, 3, 4, 5


def _router_kernel(x_ref, mod_ref, n2g_ref, w2_ref, br_ref, h_ref, metat_ref, cnt_ref, run):
    @pl.when(pl.program_id(0) % (HALF_TOK // ROUTE_TB) == 0)
    def _():
        run[...] = jnp.zeros_like(run)

    shift, scale = mod_ref[3:4, :], mod_ref[4:5, :]
    h = _rms(x_ref[...], n2g_ref[...]) * (1.0 + scale) + shift
    _rows_to_tiles(h_ref, h)
    h_hi, h_lo = _split_bf16(h)
    dot = functools.partial(jnp.dot, preferred_element_type=F32)
    wide = dot(h_hi, w2_ref[...])
    logits = wide[:, :128] + wide[:, 128:] + dot(h_lo, w2_ref[:, :128]) + br_ref[...]
    lane = lax.broadcasted_iota(jnp.int32, logits.shape, 1).astype(F32)
    big = 1e4

    def first_argmax(vals):
        m = vals.max(axis=-1, keepdims=True)
        return m, jnp.where(vals == m, lane, big).min(axis=-1, keepdims=True)

    gl = jnp.where((lane >= N_EXP) & (lane < N_EXP + MOE_GROUPS), logits, NEG)
    gmax, glane = first_argmax(gl)
    g_p = 1.0 / jnp.exp(gl - gmax).sum(axis=-1, keepdims=True)
    lo = (glane - N_EXP) * MOE_PER_GROUP
    el = jnp.where((lane >= lo) & (lane < lo + MOE_PER_GROUP), logits, NEG)
    m1, i1 = first_argmax(el)
    m2, i2 = first_argmax(jnp.where(lane == i1, NEG, el))
    t = jnp.exp(m2 - m1)
    w1 = 1.0 / (1.0 + t)
    sel1, sel2 = lane == i1, lane == i2
    onehot = jnp.where(sel1 | sel2, 1.0, 0.0)
    ri = lax.broadcasted_iota(jnp.int32, (ROUTE_TB, ROUTE_TB), 0)
    rj = lax.broadcasted_iota(jnp.int32, (ROUTE_TB, ROUTE_TB), 1)
    before = _dot(jnp.where(ri > rj, 1.0, 0.0), onehot) + run[...]
    r1 = jnp.where(sel1, before, 0.0).sum(axis=-1, keepdims=True)
    r2 = jnp.where(sel2, before, 0.0).sum(axis=-1, keepdims=True)
    run[...] += onehot.sum(axis=0, keepdims=True)
    cnt_ref[...] = run[...]
    meta = jnp.zeros_like(logits)
    for j, val in enumerate([i1, i2, w1 * g_p, (t * w1) * g_p, r1, r2]):
        meta = jnp.where(lane == j, val, meta)
    metat_ref[...] = meta.T[0:8, :]


def _router(x_all, mod_l, n2g, wr, br):
    w2 = jnp.concatenate(_split_bf16(wr), axis=1)
    nb_ctx = N_CTX // ROUTE_TB
    per_seq = DEC_SEQ // ROUTE_TB
    cond = lambda i: jnp.where(i < nb_ctx, 0, 1 + (i - nb_ctx) // per_seq)
    const2 = lambda i: (0, 0)
    return pl.pallas_call(
        _router_kernel,
        grid=(N_TOK // ROUTE_TB,),
        in_specs=[
            pl.BlockSpec((ROUTE_TB, D), lambda i: (i, 0)),
            pl.BlockSpec((None, 6, D), lambda i: (cond(i), 0, 0)),
            _resident((1, D), const2),
            _resident((D, 256), const2),
            _resident((1, 128), const2),
        ],
        out_specs=[
            pl.BlockSpec((ROUTE_TB * 8, 128), lambda i: (i, 0)),
            pl.BlockSpec((8, ROUTE_TB), lambda i: (0, i)),
            pl.BlockSpec((None, 1, 128), lambda i: (i // (HALF_TOK // ROUTE_TB), 0, 0)),
        ],
        out_shape=[
            jax.ShapeDtypeStruct((N_TOK * 8, 128), F32),
            jax.ShapeDtypeStruct((8, N_TOK), F32),
            jax.ShapeDtypeStruct((2, 1, 128), F32),
        ],
        scratch_shapes=[pltpu.VMEM((1, 128), F32)],
        compiler_params=_cp("arbitrary"),
        name="moe_router",
    )(x_all, mod_l, n2g, w2, br)


EXP_TM = 128
N_ASSIGN = 2 * N_TOK
N_GROUPS = 2 * N_EXP
MAX_TILES = N_ASSIGN // EXP_TM + N_GROUPS
N_SORTED = MAX_TILES * EXP_TM
ORDER_BLK = 2048
CODE_PLANE = 2 * HALF_TOK
CODE_MASK = 8 * CODE_PLANE - 1
DUMMY8 = HALF_TOK * 8


def _order_kernel(pos1_ref, pos2_ref, pad_lo_ref, pad_hi_ref, src_ref):
    i = pl.program_id(0)
    local = (i % (HALF_TOK // ORDER_BLK)) * ORDER_BLK

    def body(t, carry):
        src_ref[pos1_ref[t]] = (local + t) * 8
        src_ref[pos2_ref[t]] = (local + t + CODE_PLANE) * 8
        return carry

    lax.fori_loop(0, ORDER_BLK, body, 0, unroll=16)

    @pl.when(i == 0)
    def _():
        def group(g, carry):
            def pad(p, c):
                src_ref[p] = DUMMY8
                return c
            return lax.fori_loop(pad_lo_ref[g], pad_hi_ref[g], pad, carry)

        lax.fori_loop(0, N_GROUPS, group, 0)


def _order(pos, pad_lo, pad_hi):
    return pl.pallas_call(
        _order_kernel,
        grid=(N_TOK // ORDER_BLK,),
        in_specs=[
            pl.BlockSpec((ORDER_BLK,), lambda i: (i,), memory_space=pltpu.SMEM),
            pl.BlockSpec((ORDER_BLK,), lambda i: (N_TOK // ORDER_BLK + i,), memory_space=pltpu.SMEM),
            pl.BlockSpec(memory_space=pltpu.SMEM),
            pl.BlockSpec(memory_space=pltpu.SMEM),
        ],
        out_specs=pl.BlockSpec(memory_space=pltpu.SMEM),
        out_shape=jax.ShapeDtypeStruct((N_SORTED,), jnp.int32),
        compiler_params=_cp("arbitrary"),
        name="moe_order",
    )(pos, pos, pad_lo, pad_hi)


GATE_BLK = CODE_PLANE + HALF_TOK
ACC_TOK = HALF_TOK + 64
GATHER_GROUP, ACC_GROUP = 16, 8


RES_TB = 256


EXP_PER_STEP = 2


def _experts_kernel(tile0_ref, ntile_ref, count_ref, src_ref, gs_ref, h_hbm, x_hbm, mod_ref, *rest, final):
    weights, rest = rest[:3 * EXP_PER_STEP], rest[3 * EXP_PER_STEP:]
    if final:
        fg_ref, *dst_hbm = rest[:3]
        rest = rest[3:]
    else:
        dst_hbm, rest = rest[:1], rest[1:]
    h_res, acc, xbuf, ybuf, wgb, wub, wdb, xin, xout, sem, in_sem, out_sem = rest
    first_group = pl.program_id(0) * EXP_PER_STEP
    half = first_group // N_EXP
    rows0 = pl.multiple_of(half * (HALF_TOK * 8), 8)

    @pl.when(first_group % N_EXP == 0)
    def _():
        cp = pltpu.make_async_copy(h_hbm.at[pl.ds(rows0, HALF_TOK * 8), :], h_res.at[pl.ds(0, HALF_TOK * 8), :], sem)
        cp.start()
        h_res[pl.ds(DUMMY8, 8), :] = jnp.zeros((8, 128), F32)
        xbuf[...] = jnp.zeros_like(xbuf)

        def zero(i, carry):
            acc[pl.ds(pl.multiple_of(i * 512, 512), 512), :] = jnp.zeros((512, 128), F32)
            return carry

        lax.fori_loop(0, ACC_TOK * 8 // 512, zero, 0)
        cp.wait()

    def run_group(group, wg_ref, wu_ref, wd_ref):
        n_tiles = ntile_ref[group]

        @pl.when(n_tiles > 0)
        def _():
            wgb[...] = wg_ref[...].astype(BF16)
            wub[...] = wu_ref[...].astype(BF16)
            wdb[...] = wd_ref[...].astype(BF16)

        row0 = tile0_ref[group] * EXP_TM
        row_end = row0 + count_ref[group]

        def process(base, rows):
            live = (jnp.clip(row_end - base, 0, rows) + GATHER_GROUP - 1) // GATHER_GROUP

            def gather(g, c):
                for i in range(GATHER_GROUP):
                    r = g * GATHER_GROUP + i
                    xbuf[pl.ds(pl.multiple_of(r * 8, 8), 8), :] = _tile_of(h_res, src_ref[base + r] & CODE_MASK)[...]
                return c

            lax.fori_loop(0, live, gather, 0)
            x = _tiles_to_rows(xbuf, rows).astype(BF16)
            hid = _silu(_dot(x, wgb[...])) * _dot(x, wub[...])
            _rows_to_tiles(ybuf, _dot(hid, wdb[...]))

            def accumulate(g, c):
                targets, values = [], []
                for i in range(ACC_GROUP):
                    r = g * ACC_GROUP + i
                    code = src_ref[base + r]
                    target = _tile_of(acc, code & CODE_MASK)
                    targets.append(target)
                    values.append(target[...] + gs_ref[code >> 3] * ybuf[pl.ds(pl.multiple_of(r * 8, 8), 8), :])
                for target, value in zip(targets, values):
                    target[...] = value
                return c

            lax.fori_loop(0, live * (GATHER_GROUP // ACC_GROUP), accumulate, 0)

        def pair_body(j, carry):
            process(row0 + j * (2 * EXP_TM), 2 * EXP_TM)
            return carry

        lax.fori_loop(0, n_tiles // 2, pair_body, 0)

        @pl.when(n_tiles % 2 == 1)
        def _():
            process(row0 + (n_tiles - 1) * EXP_TM, EXP_TM)

    for sub in range(EXP_PER_STEP):
        run_group(first_group + sub, *weights[3 * sub:3 * sub + 3])
    expert = (first_group + EXP_PER_STEP - 1) % N_EXP

    def rows_of(first, blk):
        return pl.ds(pl.multiple_of(first + blk * RES_TB, RES_TB), RES_TB)

    def load_x(blk, slot):
        return pltpu.make_async_copy(x_hbm.at[rows_of(half * HALF_TOK, blk), :], xin.at[slot], in_sem.at[slot])

    def residual(blk, slot):
        cond = jnp.where(half == 0, 0, 1 + blk // (DEC_SEQ // RES_TB))
        gate = mod_ref[cond, 5:6, :]
        y = _tiles_to_rows(acc.at[pl.ds(pl.multiple_of(blk * (RES_TB * 8), RES_TB * 8), RES_TB * 8), :], RES_TB)
        x_new = xin[slot] + gate * y
        xout[slot] = _rms(x_new, fg_ref[...]) if final else x_new

    def epilogue(dst, first_row):
        def store_x(blk, slot):
            return pltpu.make_async_copy(xout.at[slot], dst.at[rows_of(first_row, blk), :], out_sem.at[slot])

        n_pairs = HALF_TOK // RES_TB // 2
        load_x(0, 0).start()

        def pair(p, carry):
            for slot in range(2):
                blk = 2 * p + slot
                if slot == 0:
                    load_x(blk + 1, 1).start()
                else:
                    @pl.when(p + 1 < n_pairs)
                    def _():
                        load_x(blk + 1, 0).start()
                load_x(blk, slot).wait()

                @pl.when(p > 0)
                def _():
                    store_x(blk - 2, slot).wait()

                residual(blk, slot)
                store_x(blk, slot).start()
            return carry

        lax.fori_loop(0, n_pairs, pair, 0)
        store_x(2 * n_pairs - 2, 0).wait()
        store_x(2 * n_pairs - 1, 1).wait()

    if final:
        for which in range(2):
            @pl.when((expert == N_EXP - 1) & (half == which))
            def _():
                epilogue(dst_hbm[which], 0)
    else:
        @pl.when(expert == N_EXP - 1)
        def _():
            epilogue(dst_hbm[0], half * HALF_TOK)


def _experts(tile0, n_tiles, counts, src, gs, h, x_all, mod_l, wg, wu, wd, layer, final_g=None):
    final = final_g is not None
    any_spec = pl.BlockSpec(memory_space=pl.ANY)
    extra_specs = [pl.BlockSpec((1, D), lambda g, t0, nt, cnt, src: (0, 0))] if final else []
    extra_args = (final_g,) if final else ()
    weight_specs, weight_args = [], []
    for sub in range(EXP_PER_STEP):
        wmap = lambda g, t0, nt, cnt, src, sub=sub: (layer, (g * EXP_PER_STEP + sub) % N_EXP, 0, 0)
        weight_specs += [pl.BlockSpec((None, None, D, D_EXP), wmap), pl.BlockSpec((None, None, D, D_EXP), wmap),
                         pl.BlockSpec((None, None, D_EXP, D), wmap)]
        weight_args += [wg, wu, wd]
    return pl.pallas_call(
        functools.partial(_experts_kernel, final=final),
        grid_spec=pltpu.PrefetchScalarGridSpec(
            num_scalar_prefetch=4,
            grid=(N_GROUPS // EXP_PER_STEP,),
            in_specs=[
                pl.BlockSpec((GATE_BLK,), lambda g, t0, nt, cnt, src: (g * EXP_PER_STEP // N_EXP,),
                             memory_space=pltpu.SMEM),
                pl.BlockSpec(memory_space=pl.ANY),
                pl.BlockSpec(memory_space=pl.ANY),
                pl.BlockSpec((8, 6, D), lambda g, t0, nt, cnt, src: (0, 0, 0)),
            ] + weight_specs + extra_specs,
            out_specs=[any_spec, any_spec] if final else any_spec,
            scratch_shapes=[
                pltpu.VMEM((ACC_TOK * 8, 128), F32),
                pltpu.VMEM((ACC_TOK * 8, 128), F32),
                pltpu.VMEM((2 * EXP_TM * 8, 128), F32),
                pltpu.VMEM((2 * EXP_TM * 8, 128), F32),
                pltpu.VMEM((D, D_EXP), BF16),
                pltpu.VMEM((D, D_EXP), BF16),
                pltpu.VMEM((D_EXP, D), BF16),
                pltpu.VMEM((2, RES_TB, D), F32),
                pltpu.VMEM((2, RES_TB, D), F32),
                pltpu.SemaphoreType.DMA,
                pltpu.SemaphoreType.DMA((2,)),
                pltpu.SemaphoreType.DMA((2,)),
            ],
        ),
        out_shape=([jax.ShapeDtypeStruct((HALF_TOK, D), F32)] * 2 if final
                   else jax.ShapeDtypeStruct((N_TOK, D), F32)),
        input_output_aliases={} if final else {6: 0},
        compiler_params=_cp("arbitrary", vmem_mib=56),
        name="moe_experts_final" if final else "moe_experts",
    )(tile0, n_tiles, counts, src, gs, h, x_all, mod_l, *weight_args, *extra_args)


def _moe(x_all, mod_l, n2g, wr, br, wg, wu, wd, layer, final_g=None):
    h, metat, cnt = _router(x_all, mod_l, n2g, wr, br)
    counts = cnt[:, 0, :N_EXP].astype(jnp.int32).reshape(N_GROUPS)
    padded = (counts + EXP_TM - 1) // EXP_TM * EXP_TM
    ends = jnp.cumsum(padded)
    offs = ends - padded
    rec = metat.astype(jnp.int32)
    half = (jnp.arange(N_TOK, dtype=jnp.int32) // HALF_TOK)[None, :]
    group = rec[M_E1:M_E2 + 1] + N_EXP * half
    is_group = group[None] == jnp.arange(N_GROUPS, dtype=jnp.int32)[:, None, None]
    pos = jnp.sum(jnp.where(is_group, offs[:, None, None], 0), axis=0) + rec[M_R1:M_R2 + 1]
    live_end = offs + (counts + GATHER_GROUP - 1) // GATHER_GROUP * GATHER_GROUP
    src = _order(pos.reshape(N_ASSIGN), offs + counts, live_end)
    g12 = metat[M_G1:M_G2 + 1].reshape(2, 2, HALF_TOK)
    gates = jnp.concatenate([g12[0], jnp.zeros((2, CODE_PLANE - HALF_TOK), F32), g12[1]], axis=1)
    return _experts(offs // EXP_TM, padded // EXP_TM, counts, src, gates.reshape(2 * GATE_BLK), h, x_all, mod_l,
                    wg, wu, wd, layer, final_g)


def _rope_tables():
    pos = jnp.arange(DEC_SEQ)
    row = (pos // GRID_W).astype(F32)
    col = (pos % GRID_W).astype(F32)
    n_freq = HD // 4
    inv = ROPE_THETA ** (-jnp.arange(n_freq, dtype=F32) / n_freq)
    ang = jnp.concatenate([row[:, None] * inv, col[:, None] * inv], axis=-1)
    cos = jnp.repeat(jnp.cos(ang), 2, axis=-1)
    sin = jnp.repeat(jnp.sin(ang), 2, axis=-1) * jnp.tile(jnp.array([-1.0, 1.0], F32), HD // 2)
    nblk = DEC_SEQ // QKV_TB
    cos_tab = jnp.concatenate([jnp.ones((1, QKV_TB, HD), F32), cos.reshape(nblk, QKV_TB, HD)], axis=0)
    sin_tab = jnp.concatenate([jnp.zeros((1, QKV_TB, HD), F32), sin.reshape(nblk, QKV_TB, HD)], axis=0)
    return cos_tab, sin_tab


def kernel(x_prompt, x_sample, state_gla, cache_k, cache_v, c, c_ctx, w_mod, b_mod, norm1_g, norm2_g,
           w_in_even, w_gate_up, b_gate_up, gla_norm_g, w_spatial, b_spatial, w_out_even,
           w_in_odd, q_norm_g, k_norm_g, w_out_odd, w_router_group, b_router_group,
           w_router_expert, b_router_expert, w_exp_gate, w_exp_up, w_exp_down, final_norm_g):
    x_all = None
    cond8 = jnp.concatenate([c_ctx[None], c, jnp.zeros((3, D), F32)], axis=0)
    mod = _modulation(cond8, w_mod, b_mod)
    cos_tab, sin_tab = _rope_tables()
    zero_state = jnp.zeros((1, 2, QK_W, GLA_DV), F32)
    state_in = state_gla.reshape(DEC_BATCH, -1, 2, QK_W, GLA_DV)
    cache_k2 = cache_k.reshape(DEC_BATCH, -1, SEQ * ATT_KV, HD)
    cache_v2 = cache_v.reshape(DEC_BATCH, -1, SEQ * ATT_KV, HD)

    gla_states, caches = [], None
    for l in range(DEPTH):
        i = l // 2
        n1g = norm1_g[l][None]
        if l % 2 == 0:
            w = w_in_even[i]
            win1, win2 = w[:, :EVEN_W1].astype(BF16), w[:, EVEN_SPLIT:].astype(BF16)
            wgu = jnp.zeros((128, 2 * QK_W), F32)
            wgu = wgu.at[0:GLA_RANK, 0:QK_W].set(w_gate_up[i, 0])
            wgu = wgu.at[GLA_RANK:2 * GLA_RANK, QK_W:].set(w_gate_up[i, 1]).astype(BF16)
            bgu = b_gate_up[i].reshape(1, 2 * QK_W)
            args = (mod[l], n1g, win1, win2, wgu, bgu, gla_norm_g[i][None], w_spatial[i].astype(BF16),
                    b_spatial[i].T, w_out_even[i].astype(BF16))
            first = l == 0
            x_all, st = _even_mixer(x_all, *args, zero_state, latent=False,
                                    x_first=x_prompt.reshape(N_CTX, D) if first else None)
            gla_states.append(st)
            x_all, _ = _even_mixer(x_all, *args, state_in[:, i], latent=True,
                                   x_first=x_sample.reshape(N_LAT, D) if first else None)
        else:
            q, k, v, *caches = _qkv(x_all, mod[l], n1g, w_in_odd[i].astype(BF16), q_norm_g[i][None],
                                    k_norm_g[i][None], cos_tab, sin_tab, i, caches)
            wout = w_out_odd[i].astype(BF16)
            x_all = _attention(x_all, mod[l], q, k, v, wout)
            x_all = _attention(x_all, mod[l], q, k, v, wout, cache_k2, cache_v2, layer_i=i)
        wr = jnp.concatenate([w_router_expert[l], w_router_group[l],
                              jnp.zeros((D, 128 - N_EXP - MOE_GROUPS), F32)], axis=1)
        br = jnp.concatenate([b_router_expert[l], b_router_group[l],
                              jnp.zeros((128 - N_EXP - MOE_GROUPS,), F32)])[None]
        x_all = _moe(x_all, mod[l], norm2_g[l][None], wr, br, w_exp_gate, w_exp_up, w_exp_down, l,
                     final_norm_g[None] if l == DEPTH - 1 else None)

    y_prompt = x_all[0].reshape(BATCH, SEQ, D)
    y_sample = x_all[1].reshape(DEC_BATCH, DEC_SEQ, D)
    new_state = jnp.stack(gla_states, axis=1).reshape(BATCH, -1, 2, GLA_HEADS, GLA_DK, GLA_DV)
    new_k, new_v = (a.reshape(BATCH, DEPTH // 2, SEQ, ATT_KV, HD) for a in caches)
    return (y_prompt, y_sample, new_state, new_k, new_v)
```

```python
import functools

import jax
import jax.numpy as jnp
import numpy as np
from jax import lax
from jax.experimental import pallas as pl
from jax.experimental.pallas import tpu as pltpu

F32 = jnp.float32
BF16 = jnp.bfloat16

D = 1024
BATCH, SEQ = 16, 256
DEC_BATCH, DEC_SEQ = 4, 1024
N_CTX = BATCH * SEQ
N_LAT = DEC_BATCH * DEC_SEQ
N_TOK = N_CTX + N_LAT
DEPTH = 4
EPS = 1e-6
GRID_W = 64
ROPE_THETA = 10000.0

GLA_HEADS, GLA_DK, GLA_DV, GLA_RANK, GLA_CHUNK, GLA_TAU = 4, 64, 128, 16, 128, 16.0
QK_W = GLA_HEADS * GLA_DK
V_W = GLA_HEADS * GLA_DV
GMLP_GROUPS, GMLP_DIM, GMLP_CHUNK = 4, 128, 128
GMLP_W = GMLP_GROUPS * GMLP_DIM
C_Q, C_K, C_V, C_G, C_A, C_U, C_VG = 0, 256, 512, 1024, 1536, 1664, 2176
EVEN_SPLIT = 1568
EVEN_W1 = C_U
EVEN_PACK = 2688

ATT_HEADS, ATT_KV, HD = 8, 2, 128
ATT_G = ATT_HEADS // ATT_KV
Q_W = ATT_HEADS * HD
KV_W = ATT_KV * HD

MOE_GROUPS, MOE_PER_GROUP = 4, 8
N_EXP = MOE_GROUPS * MOE_PER_GROUP
D_EXP = D // 4
NEG = -1e30

MIB = 1024 * 1024
V7X_VMEM_MIB = 64
LANES, SUBLANES = 128, 8


def _cp(*sem, vmem_mib=32):
    assert vmem_mib < V7X_VMEM_MIB
    return pltpu.CompilerParams(dimension_semantics=sem, vmem_limit_bytes=vmem_mib * MIB)


def _dot(a, b):
    return jnp.dot(a.astype(BF16), b.astype(BF16), preferred_element_type=F32)


def _dot_nt(a, b):
    return lax.dot_general(a.astype(BF16), b.astype(BF16), (((1,), (1,)), ((), ())),
                           preferred_element_type=F32)


def _dot_tn(a, b):
    return lax.dot_general(a.astype(BF16), b.astype(BF16), (((0,), (0,)), ((), ())),
                           preferred_element_type=F32)


def _rms(x, g):
    return x * lax.rsqrt(jnp.mean(x * x, axis=-1, keepdims=True) + EPS) * g


def _silu(x):
    return x * jax.nn.sigmoid(x)


def _gelu(x):
    return 0.5 * x * (1.0 + jnp.tanh(np.sqrt(2.0 / np.pi).astype(np.float32) * (x + 0.044715 * (x * x * x))))


def _log_sigmoid(z):
    return jnp.minimum(z, 0.0) - jnp.log(1.0 + jnp.exp(-jnp.abs(z)))


assert D == LANES * SUBLANES


def _rows_to_tiles(ref, x):
    rows = x.shape[0]
    for j in range(SUBLANES):
        ref[pl.ds(j, rows, stride=SUBLANES), :] = x[:, j * LANES:(j + 1) * LANES]


def _tiles_to_rows(ref, rows):
    return jnp.concatenate([ref[pl.ds(j, rows, stride=SUBLANES), :] for j in range(SUBLANES)], axis=1)


def _tile_of(ref, row8):
    return ref.at[pl.ds(pl.multiple_of(row8, SUBLANES), SUBLANES), :]


def _resident(shape, index_map):
    return pl.BlockSpec(shape, index_map, pipeline_mode=pl.Buffered(1))


def _split_bf16(x):
    hi = x.astype(BF16)
    return hi, (x - hi.astype(F32)).astype(BF16)


def _mod_kernel(cond_ref, w_ref, b_ref, o_ref):
    s_hi, s_lo = _split_bf16(_silu(cond_ref[...]))
    w_hi, w_lo = _split_bf16(w_ref[...])
    dot = functools.partial(jnp.dot, preferred_element_type=F32)
    both = dot(jnp.concatenate([s_hi, s_lo], axis=0), w_hi)
    o_ref[...] = both[0:8] + both[8:16] + dot(s_hi, w_lo) + b_ref[...]


def _modulation(cond8, w_mod, b_mod):
    tn = 2048
    out = pl.pallas_call(
        _mod_kernel,
        grid=(DEPTH, 6 * D // tn),
        in_specs=[
            pl.BlockSpec((8, D), lambda l, j: (0, 0)),
            pl.BlockSpec((None, D, tn), lambda l, j: (l, 0, j)),
            pl.BlockSpec((None, 1, tn), lambda l, j: (l, 0, j)),
        ],
        out_specs=pl.BlockSpec((None, 8, tn), lambda l, j: (l, 0, j)),
        out_shape=jax.ShapeDtypeStruct((DEPTH, 8, 6 * D), F32),
        compiler_params=_cp("arbitrary", "arbitrary"),
        name="adaln_mod",
    )(cond8, w_mod, b_mod.reshape(DEPTH, 1, 6 * D))
    return out.reshape(DEPTH, 8, 6, D)


def _even_kernel(x_ref, mod_ref, n1g_ref, win1_ref, win2_ref, wgu_ref, bgu_ref, glag_ref, ws_ref, bs_ref,
                 wout_ref, s0_ref, xo_ref, st_ref, proj, la, o_f, o_b, st_scr, *, T):
    n_chunks = T // GLA_CHUNK
    shift, scale, gate = mod_ref[0:1, :], mod_ref[1:2, :], mod_ref[2:3, :]
    RB = 128
    PROJ_RB = 256

    def proj_body(r, carry):
        r0 = pl.multiple_of(r * PROJ_RB, PROJ_RB)
        h = _rms(x_ref[pl.ds(r0, PROJ_RB), :], n1g_ref[...]) * (1.0 + scale) + shift
        hb = h.astype(BF16)
        p = _dot(hb, win1_ref[...])
        proj[pl.ds(r0, PROJ_RB), 0:EVEN_W1] = p
        proj[pl.ds(r0, PROJ_RB), EVEN_W1:] = _dot(hb, win2_ref[...])
        z = _dot(p[:, C_A:C_A + 128], wgu_ref[...]) + bgu_ref[...]
        la[pl.ds(r0, PROJ_RB), :] = _log_sigmoid(z) * (1.0 / GLA_TAU)
        return carry

    lax.fori_loop(0, T // PROJ_RB, proj_body, 0)

    st_scr[0] = s0_ref[0].T
    st_scr[1] = s0_ref[1].T

    ci = lax.broadcasted_iota(jnp.int32, (GLA_CHUNK, GLA_CHUNK), 0)
    cj = lax.broadcasted_iota(jnp.int32, (GLA_CHUNK, GLA_CHUNK), 1)
    tri = (jnp.where(ci >= cj, 1.0, 0.0).astype(BF16), jnp.where(ci <= cj, 1.0, 0.0).astype(BF16))
    ai = lax.broadcasted_iota(jnp.int32, (GLA_HEADS * GLA_CHUNK, GLA_CHUNK), 0) % GLA_CHUNK
    aj = lax.broadcasted_iota(jnp.int32, (GLA_HEADS * GLA_CHUNK, GLA_CHUNK), 1)
    amask = (ai >= aj, ai <= aj)
    lane_head = lax.broadcasted_iota(jnp.int32, (1, QK_W), 1) // GLA_DK
    hmask = [jnp.where(lane_head == h, 1.0, 0.0) for h in range(GLA_HEADS)]

    def chunk_body(i, carry):
        for d in range(2):
            c = i if d == 0 else n_chunks - 1 - i
            r0 = pl.multiple_of(c * GLA_CHUNK, GLA_CHUNK)
            q = proj[pl.ds(r0, GLA_CHUNK), C_Q:C_Q + QK_W] * (GLA_DK ** -0.5)
            k = proj[pl.ds(r0, GLA_CHUNK), C_K:C_K + QK_W]
            v = proj[pl.ds(r0, GLA_CHUNK), C_V:C_V + V_W]
            lac = la[pl.ds(r0, GLA_CHUNK), d * QK_W:(d + 1) * QK_W]
            hi = lac.astype(BF16)
            lo = (lac - hi.astype(F32)).astype(BF16)
            b = (jnp.dot(tri[d], hi, preferred_element_type=F32)
                 + jnp.dot(tri[d], lo, preferred_element_type=F32))
            bend = b[GLA_CHUNK - 1:GLA_CHUNK, :] if d == 0 else b[0:1, :]
            qe = q * jnp.exp(b)
            ke = k * jnp.exp(-b)
            kd = k * jnp.exp(bend - b)
            st = st_scr[d]
            qstack = jnp.concatenate([qe * hmask[h] for h in range(GLA_HEADS)], axis=0).astype(BF16)
            att = jnp.where(amask[d], _dot_nt(qstack, ke), 0.0)
            inter = _dot_nt(qstack, st)
            outs = []
            for h in range(GLA_HEADS):
                rows = slice(h * GLA_CHUNK, (h + 1) * GLA_CHUNK)
                outs.append(_dot(att[rows], v[:, h * GLA_DV:(h + 1) * GLA_DV]) + inter[rows])
            o = jnp.concatenate(outs, axis=1)
            if d == 0:
                o_f[pl.ds(r0, GLA_CHUNK), :] = o
            else:
                o_b[pl.ds(r0, GLA_CHUNK), :] = o
            vstack = jnp.concatenate([v[:, h * GLA_DV:(h + 1) * GLA_DV] for h in range(GLA_HEADS)], axis=0)
            kstack = jnp.concatenate([kd * hmask[h] for h in range(GLA_HEADS)], axis=0)
            st_scr[d] = st * jnp.exp(bend) + _dot_tn(vstack, kstack)
        return carry

    lax.fori_loop(0, n_chunks, chunk_body, 0, unroll=2)
    st_ref[0] = st_scr[0].T
    st_ref[1] = st_scr[1].T

    def out_body(r, carry):
        r0 = pl.multiple_of(r * RB, RB)
        osum = o_f[pl.ds(r0, RB), :] + o_b[pl.ds(r0, RB), :]
        g = proj[pl.ds(r0, RB), C_G:C_G + V_W]
        u = proj[pl.ds(r0, RB), C_U:C_U + GMLP_W]
        vg = _gelu(proj[pl.ds(r0, RB), C_VG:C_VG + GMLP_W])
        parts = []
        for h in range(GLA_HEADS):
            oh = osum[:, h * GLA_DV:(h + 1) * GLA_DV]
            parts.append(_rms(oh, glag_ref[...]) * _silu(g[:, h * GLA_DV:(h + 1) * GLA_DV]))
        for gi in range(GMLP_GROUPS):
            vc = vg[:, gi * GMLP_DIM:(gi + 1) * GMLP_DIM]
            vc = vc - jnp.mean(vc, axis=-1, keepdims=True)
            vn = vc * lax.rsqrt(jnp.mean(vc * vc, axis=-1, keepdims=True) + EPS)
            sg = _dot(ws_ref[gi], vn) + bs_ref[:, gi:gi + 1]
            parts.append(_gelu(u[:, gi * GMLP_DIM:(gi + 1) * GMLP_DIM]) * sg)
        mix = jnp.concatenate(parts, axis=1)
        y = _dot(mix, wout_ref[...])
        xo_ref[pl.ds(r0, RB), :] = x_ref[pl.ds(r0, RB), :] + gate * y
        return carry

    lax.fori_loop(0, T // RB, out_body, 0)


def _even_mixer(x_all, mod_l, n1g, win1, win2, wgu, bgu, glag, ws, bs, wout, s0, *, latent, x_first=None):
    if latent:
        T, nseq, blk0 = DEC_SEQ, DEC_BATCH, N_CTX // DEC_SEQ
        cond = lambda i: 1 + i
        s0_spec = pl.BlockSpec((None, 2, QK_W, GLA_DV), lambda i: (i, 0, 0, 0))
    else:
        T, nseq, blk0 = SEQ, BATCH, 0
        cond = lambda i: 0
        s0_spec = pl.BlockSpec((None, 2, QK_W, GLA_DV), lambda i: (0, 0, 0, 0))
    const2 = lambda i: (0, 0)
    body = functools.partial(_even_kernel, T=T)
    x_spec = pl.BlockSpec((T, D), lambda i: (blk0 + i, 0))
    if x_first is None:
        lead_specs, lead_args, aliases = [x_spec], (x_all,), {0: 0}
    elif x_all is None:
        lead_specs, lead_args, aliases = [pl.BlockSpec((T, D), lambda i: (i, 0))], (x_first,), {}
    else:
        lead_specs = [pl.BlockSpec(memory_space=pl.ANY), pl.BlockSpec((T, D), lambda i: (i, 0))]
        lead_args, aliases = (x_all, x_first), {0: 0}
        body = lambda dst_ref, *refs: _even_kernel(*refs, T=T)
    x_new, states = pl.pallas_call(
        body,
        grid=(nseq,),
        in_specs=lead_specs + [
            pl.BlockSpec((None, 6, D), lambda i: (cond(i), 0, 0)),
            _resident((1, D), const2),
            _resident((D, EVEN_W1), const2),
            _resident((D, EVEN_PACK - EVEN_W1), const2),
            _resident((128, 2 * QK_W), const2),
            _resident((1, 2 * QK_W), const2),
            _resident((1, GLA_DV), const2),
            _resident((GMLP_GROUPS, GMLP_CHUNK, GMLP_CHUNK), lambda i: (0, 0, 0)),
            _resident((GMLP_CHUNK, GMLP_GROUPS), const2),
            _resident((D, D), const2),
            s0_spec,
        ],
        out_specs=[
            x_spec,
            pl.BlockSpec((None, 2, QK_W, GLA_DV), lambda i: (i, 0, 0, 0)),
        ],
        out_shape=[
            jax.ShapeDtypeStruct((N_TOK, D), F32),
            jax.ShapeDtypeStruct((nseq, 2, QK_W, GLA_DV), F32),
        ],
        scratch_shapes=[
            pltpu.VMEM((T, EVEN_PACK), F32),
            pltpu.VMEM((T, 2 * QK_W), F32),
            pltpu.VMEM((T, V_W), F32),
            pltpu.VMEM((T, V_W), F32),
            pltpu.VMEM((2, GLA_DV, QK_W), F32),
        ],
        input_output_aliases=aliases,
        compiler_params=_cp("arbitrary", vmem_mib=48 if latent else 32),
        name="even_mixer_latent" if latent else "even_mixer_context",
    )(*lead_args, mod_l, n1g, win1, win2, wgu, bgu, glag, ws, bs, wout, s0)
    return x_new, states


QKV_TB = 512


def _qkv_kernel(x_ref, mod_ref, n1g_ref, win_ref, gq_ref, gk_ref, cos_ref, sin_ref, q_ref, k_ref, v_ref,
                ck_ref, cv_ref):
    shift, scale = mod_ref[0:1, :], mod_ref[1:2, :]
    h = _rms(x_ref[...], n1g_ref[...]) * (1.0 + scale) + shift
    p = _dot(h, win_ref[...])
    cos, sin = cos_ref[...], sin_ref[...]
    even_lane = lax.broadcasted_iota(jnp.int32, (1, HD), 1) % 2 == 0

    def rope(xn):
        swapped = jnp.where(even_lane, pltpu.roll(xn, HD - 1, axis=1), pltpu.roll(xn, 1, axis=1))
        return xn * cos + swapped * sin

    def emit(rotate, to_cache):
        for hh in range(ATT_HEADS):
            qn = _rms(p[:, hh * HD:(hh + 1) * HD], gq_ref[...])
            q_ref[:, hh * HD:(hh + 1) * HD] = (rotate(qn) * (HD ** -0.5)).astype(BF16)
        for hh in range(ATT_KV):
            kn = rotate(_rms(p[:, Q_W + hh * HD:Q_W + (hh + 1) * HD], gk_ref[...]))
            k_ref[:, hh * HD:(hh + 1) * HD] = kn
            if to_cache:
                for s in range(QKV_TB // SEQ):
                    ck_ref[s, pl.ds(hh, SEQ, stride=ATT_KV), :] = kn[s * SEQ:(s + 1) * SEQ]
                    cv_ref[s, pl.ds(hh, SEQ, stride=ATT_KV), :] = p[s * SEQ:(s + 1) * SEQ,
                                                                    Q_W + KV_W + hh * HD:Q_W + KV_W + (hh + 1) * HD]

    is_latent = pl.program_id(0) >= N_CTX // QKV_TB

    @pl.when(is_latent)
    def _():
        emit(rope, False)

    @pl.when(jnp.logical_not(is_latent))
    def _():
        emit(lambda xn: xn, True)

    v_ref[...] = p[:, Q_W + KV_W:]


def _qkv(x_all, mod_l, n1g, win, gq, gk, cos_tab, sin_tab, layer_i, caches=None):
    nb_ctx = N_CTX // QKV_TB
    per_seq = DEC_SEQ // QKV_TB
    cond = lambda i: jnp.where(i < nb_ctx, 0, 1 + (i - nb_ctx) // per_seq)
    tab = lambda i: jnp.where(i < nb_ctx, 0, 1 + (i - nb_ctx) % per_seq)
    const2 = lambda i: (0, 0)
    cache_spec = pl.BlockSpec((QKV_TB // SEQ, None, SEQ * ATT_KV, HD),
                              lambda i: (jnp.minimum(i, nb_ctx - 1), layer_i, 0, 0))
    cache_shape = jax.ShapeDtypeStruct((BATCH, DEPTH // 2, SEQ * ATT_KV, HD), F32)
    if caches is None:
        body, lead_specs, lead_args, aliases = _qkv_kernel, [], (), {}
    else:
        body = lambda ck_in, cv_in, *refs: _qkv_kernel(*refs)
        lead_specs = [pl.BlockSpec(memory_space=pl.ANY)] * 2
        lead_args, aliases = tuple(caches), {0: 3, 1: 4}
    return pl.pallas_call(
        body,
        grid=(N_TOK // QKV_TB,),
        in_specs=lead_specs + [
            pl.BlockSpec((QKV_TB, D), lambda i: (i, 0)),
            pl.BlockSpec((None, 6, D), lambda i: (cond(i), 0, 0)),
            _resident((1, D), const2),
            _resident((D, Q_W + 2 * KV_W), const2),
            _resident((1, HD), const2),
            _resident((1, HD), const2),
            pl.BlockSpec((None, QKV_TB, HD), lambda i: (tab(i), 0, 0)),
            pl.BlockSpec((None, QKV_TB, HD), lambda i: (tab(i), 0, 0)),
        ],
        out_specs=[
            pl.BlockSpec((QKV_TB, Q_W), lambda i: (i, 0)),
            pl.BlockSpec((QKV_TB, KV_W), lambda i: (i, 0)),
            pl.BlockSpec((QKV_TB, KV_W), lambda i: (i, 0)),
            cache_spec,
            cache_spec,
        ],
        out_shape=[
            jax.ShapeDtypeStruct((N_TOK, Q_W), BF16),
            jax.ShapeDtypeStruct((N_TOK, KV_W), F32),
            jax.ShapeDtypeStruct((N_TOK, KV_W), F32),
            cache_shape,
            cache_shape,
        ],
        input_output_aliases=aliases,
        compiler_params=_cp("arbitrary"),
        name="odd_qkv",
    )(*lead_args, x_all, mod_l, n1g, win, gq, gk, cos_tab, sin_tab)


ATT_TQ_LATENT = 512


def _attn_kernel(*refs, n_kv):
    q_ref = refs[0]
    kv_refs = refs[1:1 + 2 * n_kv]
    x_ref, mod_ref, wout_ref, xo_ref, att_scr = refs[1 + 2 * n_kv:]
    gate = mod_ref[2:3, :]
    def head(ref, kh):
        if ref.shape[1] == HD:
            return ref[pl.ds(kh, ref.shape[0] // ATT_KV, stride=ATT_KV), :].astype(BF16)
        return ref[:, kh * HD:(kh + 1) * HD].astype(BF16)

    for kh in range(ATT_KV):
        ks = [head(kv_refs[2 * s], kh) for s in range(n_kv)]
        vs = [jnp.concatenate([vh, jnp.ones_like(vh)], axis=1)
              for vh in (head(kv_refs[2 * s + 1], kh) for s in range(n_kv))]
        for g in range(ATT_G):
            hh = kh * ATT_G + g
            qh = q_ref[:, hh * HD:(hh + 1) * HD]
            ss = [_dot_nt(qh, kk) for kk in ks]
            m = ss[0].max(axis=-1, keepdims=True)
            for s in ss[1:]:
                m = jnp.maximum(m, s.max(axis=-1, keepdims=True))
            o = _dot(jnp.exp(ss[0] - m), vs[0])
            for s, vv in zip(ss[1:], vs[1:]):
                o = o + _dot(jnp.exp(s - m), vv)
            att_scr[:, hh * HD:(hh + 1) * HD] = o[:, :HD] / o[:, HD:HD + 1]
    y = _dot(att_scr[...], wout_ref[...])
    xo_ref[...] = x_ref[...] + gate * y


def _attention(x_all, mod_l, q, k, v, wout, cache_k=None, cache_v=None, layer_i=0):
    latent = cache_k is not None
    const2 = lambda *a: (0, 0)
    tq = ATT_TQ_LATENT if latent else SEQ
    if latent:
        nq = DEC_SEQ // tq
        row_blk = lambda b, j: (N_CTX // tq + b * nq + j, 0)
        grid = (DEC_BATCH, nq)
        kv_specs = [
            pl.BlockSpec((None, None, SEQ * ATT_KV, HD), lambda b, j: (b, layer_i, 0, 0)),
            pl.BlockSpec((None, None, SEQ * ATT_KV, HD), lambda b, j: (b, layer_i, 0, 0)),
            pl.BlockSpec((DEC_SEQ, KV_W), lambda b, j: (N_CTX // DEC_SEQ + b, 0)),
            pl.BlockSpec((DEC_SEQ, KV_W), lambda b, j: (N_CTX // DEC_SEQ + b, 0)),
        ]
        kv_args = (cache_k, cache_v, k, v)
        mod_spec = pl.BlockSpec((None, 6, D), lambda b, j: (1 + b, 0, 0))
        sem = ("arbitrary", "arbitrary")
        n_kv = 2
    else:
        row_blk = lambda i: (i, 0)
        grid = (BATCH,)
        kv_specs = [pl.BlockSpec((SEQ, KV_W), row_blk), pl.BlockSpec((SEQ, KV_W), row_blk)]
        kv_args = (k, v)
        mod_spec = pl.BlockSpec((None, 6, D), lambda i: (0, 0, 0))
        sem = ("arbitrary",)
        n_kv = 1
    n_in = 1 + len(kv_args)
    return pl.pallas_call(
        functools.partial(_attn_kernel, n_kv=n_kv),
        grid=grid,
        in_specs=[pl.BlockSpec((tq, Q_W), row_blk)] + kv_specs + [
            pl.BlockSpec((tq, D), row_blk),
            mod_spec,
            _resident((D, D), const2),
        ],
        out_specs=pl.BlockSpec((tq, D), row_blk),
        out_shape=jax.ShapeDtypeStruct((N_TOK, D), F32),
        scratch_shapes=[pltpu.VMEM((tq, Q_W), F32)],
        input_output_aliases={n_in: 0},
        compiler_params=_cp(*sem),
        name="attention_latent" if latent else "attention_context",
    )(q, *kv_args, x_all, mod_l, wout)


ROUTE_TB = 512
HALF_TOK = N_TOK // 2
M_E1, M_E2, M_G1, M_G2, M_R1, M_R2 = 0, 1, 2, 3, 4, 5


def _router_kernel(x_ref, mod_ref, n2g_ref, w2_ref, br_ref, h_ref, metat_ref, cnt_ref, run):
    @pl.when(pl.program_id(0) % (HALF_TOK // ROUTE_TB) == 0)
    def _():
        run[...] = jnp.zeros_like(run)

    shift, scale = mod_ref[3:4, :], mod_ref[4:5, :]
    h = _rms(x_ref[...], n2g_ref[...]) * (1.0 + scale) + shift
    _rows_to_tiles(h_ref, h)
    h_hi, h_lo = _split_bf16(h)
    dot = functools.partial(jnp.dot, preferred_element_type=F32)
    wide = dot(h_hi, w2_ref[...])
    logits = wide[:, :128] + wide[:, 128:] + dot(h_lo, w2_ref[:, :128]) + br_ref[...]
    lane = lax.broadcasted_iota(jnp.int32, logits.shape, 1).astype(F32)
    big = 1e4

    def first_argmax(vals):
        m = vals.max(axis=-1, keepdims=True)
        return m, jnp.where(vals == m, lane, big).min(axis=-1, keepdims=True)

    gl = jnp.where((lane >= N_EXP) & (lane < N_EXP + MOE_GROUPS), logits, NEG)
    gmax, glane = first_argmax(gl)
    g_p = 1.0 / jnp.exp(gl - gmax).sum(axis=-1, keepdims=True)
    lo = (glane - N_EXP) * MOE_PER_GROUP
    el = jnp.where((lane >= lo) & (lane < lo + MOE_PER_GROUP), logits, NEG)
    m1, i1 = first_argmax(el)
    m2, i2 = first_argmax(jnp.where(lane == i1, NEG, el))
    t = jnp.exp(m2 - m1)
    w1 = 1.0 / (1.0 + t)
    sel1, sel2 = lane == i1, lane == i2
    onehot = jnp.where(sel1 | sel2, 1.0, 0.0)
    ri = lax.broadcasted_iota(jnp.int32, (ROUTE_TB, ROUTE_TB), 0)
    rj = lax.broadcasted_iota(jnp.int32, (ROUTE_TB, ROUTE_TB), 1)
    before = _dot(jnp.where(ri > rj, 1.0, 0.0), onehot) + run[...]
    r1 = jnp.where(sel1, before, 0.0).sum(axis=-1, keepdims=True)
    r2 = jnp.where(sel2, before, 0.0).sum(axis=-1, keepdims=True)
    run[...] += onehot.sum(axis=0, keepdims=True)
    cnt_ref[...] = run[...]
    meta = jnp.zeros_like(logits)
    for j, val in enumerate([i1, i2, w1 * g_p, (t * w1) * g_p, r1, r2]):
        meta = jnp.where(lane == j, val, meta)
    metat_ref[...] = meta.T[0:8, :]


def _router(x_all, mod_l, n2g, wr, br):
    w2 = jnp.concatenate(_split_bf16(wr), axis=1)
    nb_ctx = N_CTX // ROUTE_TB
    per_seq = DEC_SEQ // ROUTE_TB
    cond = lambda i: jnp.where(i < nb_ctx, 0, 1 + (i - nb_ctx) // per_seq)
    const2 = lambda i: (0, 0)
    return pl.pallas_call(
        _router_kernel,
        grid=(N_TOK // ROUTE_TB,),
        in_specs=[
            pl.BlockSpec((ROUTE_TB, D), lambda i: (i, 0)),
            pl.BlockSpec((None, 6, D), lambda i: (cond(i), 0, 0)),
            _resident((1, D), const2),
            _resident((D, 256), const2),
            _resident((1, 128), const2),
        ],
        out_specs=[
            pl.BlockSpec((ROUTE_TB * 8, 128), lambda i: (i, 0)),
            pl.BlockSpec((8, ROUTE_TB), lambda i: (0, i)),
            pl.BlockSpec((None, 1, 128), lambda i: (i // (HALF_TOK // ROUTE_TB), 0, 0)),
        ],
        out_shape=[
            jax.ShapeDtypeStruct((N_TOK * 8, 128), F32),
            jax.ShapeDtypeStruct((8, N_TOK), F32),
            jax.ShapeDtypeStruct((2, 1, 128), F32),
        ],
        scratch_shapes=[pltpu.VMEM((1, 128), F32)],
        compiler_params=_cp("arbitrary"),
        name="moe_router",
    )(x_all, mod_l, n2g, w2, br)


EXP_TM = 128
N_ASSIGN = 2 * N_TOK
N_GROUPS = 2 * N_EXP
MAX_TILES = N_ASSIGN // EXP_TM + N_GROUPS
N_SORTED = MAX_TILES * EXP_TM
ORDER_BLK = 2048
CODE_PLANE = 2 * HALF_TOK
CODE_MASK = 8 * CODE_PLANE - 1
DUMMY8 = HALF_TOK * 8


def _order_kernel(pos1_ref, pos2_ref, pad_lo_ref, pad_hi_ref, src_ref):
    i = pl.program_id(0)
    local = (i % (HALF_TOK // ORDER_BLK)) * ORDER_BLK

    def body(t, carry):
        src_ref[pos1_ref[t]] = (local + t) * 8
        src_ref[pos2_ref[t]] = (local + t + CODE_PLANE) * 8
        return carry

    lax.fori_loop(0, ORDER_BLK, body, 0, unroll=16)

    @pl.when(i == 0)
    def _():
        def group(g, carry):
            def pad(p, c):
                src_ref[p] = DUMMY8
                return c
            return lax.fori_loop(pad_lo_ref[g], pad_hi_ref[g], pad, carry)

        lax.fori_loop(0, N_GROUPS, group, 0)


def _order(pos, pad_lo, pad_hi):
    return pl.pallas_call(
        _order_kernel,
        grid=(N_TOK // ORDER_BLK,),
        in_specs=[
            pl.BlockSpec((ORDER_BLK,), lambda i: (i,), memory_space=pltpu.SMEM),
            pl.BlockSpec((ORDER_BLK,), lambda i: (N_TOK // ORDER_BLK + i,), memory_space=pltpu.SMEM),
            pl.BlockSpec(memory_space=pltpu.SMEM),
            pl.BlockSpec(memory_space=pltpu.SMEM),
        ],
        out_specs=pl.BlockSpec(memory_space=pltpu.SMEM),
        out_shape=jax.ShapeDtypeStruct((N_SORTED,), jnp.int32),
        compiler_params=_cp("arbitrary"),
        name="moe_order",
    )(pos, pos, pad_lo, pad_hi)


GATE_BLK = CODE_PLANE + HALF_TOK
ACC_TOK = HALF_TOK + 64
GATHER_GROUP, ACC_GROUP = 16, 8


RES_TB = 256


EXP_PER_STEP = 2


def _experts_kernel(tile0_ref, ntile_ref, count_ref, src_ref, gs_ref, h_hbm, x_hbm, mod_ref, *rest, final):
    weights, rest = rest[:3 * EXP_PER_STEP], rest[3 * EXP_PER_STEP:]
    if final:
        fg_ref, *dst_hbm = rest[:3]
        rest = rest[3:]
    else:
        dst_hbm, rest = rest[:1], rest[1:]
    h_res, acc, xbuf, ybuf, wgb, wub, wdb, xin, xout, sem, in_sem, out_sem = rest
    first_group = pl.program_id(0) * EXP_PER_STEP
    half = first_group // N_EXP
    rows0 = pl.multiple_of(half * (HALF_TOK * 8), 8)

    @pl.when(first_group % N_EXP == 0)
    def _():
        cp = pltpu.make_async_copy(h_hbm.at[pl.ds(rows0, HALF_TOK * 8), :], h_res.at[pl.ds(0, HALF_TOK * 8), :], sem)
        cp.start()
        h_res[pl.ds(DUMMY8, 8), :] = jnp.zeros((8, 128), F32)
        xbuf[...] = jnp.zeros_like(xbuf)

        def zero(i, carry):
            acc[pl.ds(pl.multiple_of(i * 512, 512), 512), :] = jnp.zeros((512, 128), F32)
            return carry

        lax.fori_loop(0, ACC_TOK * 8 // 512, zero, 0)
        cp.wait()

    def run_group(group, wg_ref, wu_ref, wd_ref):
        n_tiles = ntile_ref[group]

        @pl.when(n_tiles > 0)
        def _():
            wgb[...] = wg_ref[...].astype(BF16)
            wub[...] = wu_ref[...].astype(BF16)
            wdb[...] = wd_ref[...].astype(BF16)

        row0 = tile0_ref[group] * EXP_TM
        row_end = row0 + count_ref[group]

        def process(base, rows):
            live = (jnp.clip(row_end - base, 0, rows) + GATHER_GROUP - 1) // GATHER_GROUP

            def gather(g, c):
                for i in range(GATHER_GROUP):
                    r = g * GATHER_GROUP + i
                    xbuf[pl.ds(pl.multiple_of(r * 8, 8), 8), :] = _tile_of(h_res, src_ref[base + r] & CODE_MASK)[...]
                return c

            lax.fori_loop(0, live, gather, 0)
            x = _tiles_to_rows(xbuf, rows).astype(BF16)
            hid = _silu(_dot(x, wgb[...])) * _dot(x, wub[...])
            _rows_to_tiles(ybuf, _dot(hid, wdb[...]))

            def accumulate(g, c):
                targets, values = [], []
                for i in range(ACC_GROUP):
                    r = g * ACC_GROUP + i
                    code = src_ref[base + r]
                    target = _tile_of(acc, code & CODE_MASK)
                    targets.append(target)
                    values.append(target[...] + gs_ref[code >> 3] * ybuf[pl.ds(pl.multiple_of(r * 8, 8), 8), :])
                for target, value in zip(targets, values):
                    target[...] = value
                return c

            lax.fori_loop(0, live * (GATHER_GROUP // ACC_GROUP), accumulate, 0)

        ends_in_triple = (n_tiles % 2 == 1) & (n_tiles >= 3)
        n_pairs = jnp.where(ends_in_triple, (n_tiles - 3) // 2, n_tiles // 2)

        def pair_body(j, carry):
            process(row0 + j * (2 * EXP_TM), 2 * EXP_TM)
            return carry

        lax.fori_loop(0, n_pairs, pair_body, 0)

        @pl.when(ends_in_triple)
        def _():
            process(row0 + (n_tiles - 3) * EXP_TM, 3 * EXP_TM)

        @pl.when(n_tiles == 1)
        def _():
            process(row0, EXP_TM)

    for sub in range(EXP_PER_STEP):
        run_group(first_group + sub, *weights[3 * sub:3 * sub + 3])
    expert = (first_group + EXP_PER_STEP - 1) % N_EXP

    def rows_of(first, blk):
        return pl.ds(pl.multiple_of(first + blk * RES_TB, RES_TB), RES_TB)

    def load_x(blk, slot):
        return pltpu.make_async_copy(x_hbm.at[rows_of(half * HALF_TOK, blk), :], xin.at[slot], in_sem.at[slot])

    def residual(blk, slot):
        cond = jnp.where(half == 0, 0, 1 + blk // (DEC_SEQ // RES_TB))
        gate = mod_ref[cond, 5:6, :]
        y = _tiles_to_rows(acc.at[pl.ds(pl.multiple_of(blk * (RES_TB * 8), RES_TB * 8), RES_TB * 8), :], RES_TB)
        x_new = xin[slot] + gate * y
        xout[slot] = _rms(x_new, fg_ref[...]) if final else x_new

    def epilogue(dst, first_row):
        def store_x(blk, slot):
            return pltpu.make_async_copy(xout.at[slot], dst.at[rows_of(first_row, blk), :], out_sem.at[slot])

        n_pairs = HALF_TOK // RES_TB // 2
        load_x(0, 0).start()

        def pair(p, carry):
            for slot in range(2):
                blk = 2 * p + slot
                if slot == 0:
                    load_x(blk + 1, 1).start()
                else:
                    @pl.when(p + 1 < n_pairs)
                    def _():
                        load_x(blk + 1, 0).start()
                load_x(blk, slot).wait()

                @pl.when(p > 0)
                def _():
                    store_x(blk - 2, slot).wait()

                residual(blk, slot)
                store_x(blk, slot).start()
            return carry

        lax.fori_loop(0, n_pairs, pair, 0)
        store_x(2 * n_pairs - 2, 0).wait()
        store_x(2 * n_pairs - 1, 1).wait()

    if final:
        for which in range(2):
            @pl.when((expert == N_EXP - 1) & (half == which))
            def _():
                epilogue(dst_hbm[which], 0)
    else:
        @pl.when(expert == N_EXP - 1)
        def _():
            epilogue(dst_hbm[0], half * HALF_TOK)


def _experts(tile0, n_tiles, counts, src, gs, h, x_all, mod_l, wg, wu, wd, layer, final_g=None):
    final = final_g is not None
    any_spec = pl.BlockSpec(memory_space=pl.ANY)
    extra_specs = [pl.BlockSpec((1, D), lambda g, t0, nt, cnt, src: (0, 0))] if final else []
    extra_args = (final_g,) if final else ()
    weight_specs, weight_args = [], []
    for sub in range(EXP_PER_STEP):
        wmap = lambda g, t0, nt, cnt, src, sub=sub: (layer, (g * EXP_PER_STEP + sub) % N_EXP, 0, 0)
        weight_specs += [pl.BlockSpec((None, None, D, D_EXP), wmap), pl.BlockSpec((None, None, D, D_EXP), wmap),
                         pl.BlockSpec((None, None, D_EXP, D), wmap)]
        weight_args += [wg, wu, wd]
    return pl.pallas_call(
        functools.partial(_experts_kernel, final=final),
        grid_spec=pltpu.PrefetchScalarGridSpec(
            num_scalar_prefetch=4,
            grid=(N_GROUPS // EXP_PER_STEP,),
            in_specs=[
                pl.BlockSpec((GATE_BLK,), lambda g, t0, nt, cnt, src: (g * EXP_PER_STEP // N_EXP,),
                             memory_space=pltpu.SMEM),
                pl.BlockSpec(memory_space=pl.ANY),
                pl.BlockSpec(memory_space=pl.ANY),
                pl.BlockSpec((8, 6, D), lambda g, t0, nt, cnt, src: (0, 0, 0)),
            ] + weight_specs + extra_specs,
            out_specs=[any_spec, any_spec] if final else any_spec,
            scratch_shapes=[
                pltpu.VMEM((ACC_TOK * 8, 128), F32),
                pltpu.VMEM((ACC_TOK * 8, 128), F32),
                pltpu.VMEM((3 * EXP_TM * 8, 128), F32),
                pltpu.VMEM((3 * EXP_TM * 8, 128), F32),
                pltpu.VMEM((D, D_EXP), BF16),
                pltpu.VMEM((D, D_EXP), BF16),
                pltpu.VMEM((D_EXP, D), BF16),
                pltpu.VMEM((2, RES_TB, D), F32),
                pltpu.VMEM((2, RES_TB, D), F32),
                pltpu.SemaphoreType.DMA,
                pltpu.SemaphoreType.DMA((2,)),
                pltpu.SemaphoreType.DMA((2,)),
            ],
        ),
        out_shape=([jax.ShapeDtypeStruct((HALF_TOK, D), F32)] * 2 if final
                   else jax.ShapeDtypeStruct((N_TOK, D), F32)),
        input_output_aliases={} if final else {6: 0},
        compiler_params=_cp("arbitrary", vmem_mib=56),
        name="moe_experts_final" if final else "moe_experts",
    )(tile0, n_tiles, counts, src, gs, h, x_all, mod_l, *weight_args, *extra_args)


def _moe(x_all, mod_l, n2g, wr, br, wg, wu, wd, layer, final_g=None):
    h, metat, cnt = _router(x_all, mod_l, n2g, wr, br)
    counts = cnt[:, 0, :N_EXP].astype(jnp.int32).reshape(N_GROUPS)
    padded = (counts + EXP_TM - 1) // EXP_TM * EXP_TM
    ends = jnp.cumsum(padded)
    offs = ends - padded
    rec = metat.astype(jnp.int32)
    half = (jnp.arange(N_TOK, dtype=jnp.int32) // HALF_TOK)[None, :]
    group = rec[M_E1:M_E2 + 1] + N_EXP * half
    is_group = group[None] == jnp.arange(N_GROUPS, dtype=jnp.int32)[:, None, None]
    pos = jnp.sum(jnp.where(is_group, offs[:, None, None], 0), axis=0) + rec[M_R1:M_R2 + 1]
    live_end = offs + (counts + GATHER_GROUP - 1) // GATHER_GROUP * GATHER_GROUP
    src = _order(pos.reshape(N_ASSIGN), offs + counts, live_end)
    g12 = metat[M_G1:M_G2 + 1].reshape(2, 2, HALF_TOK)
    gates = jnp.concatenate([g12[0], jnp.zeros((2, CODE_PLANE - HALF_TOK), F32), g12[1]], axis=1)
    return _experts(offs // EXP_TM, padded // EXP_TM, counts, src, gates.reshape(2 * GATE_BLK), h, x_all, mod_l,
                    wg, wu, wd, layer, final_g)


def _rope_tables():
    pos = jnp.arange(DEC_SEQ)
    row = (pos // GRID_W).astype(F32)
    col = (pos % GRID_W).astype(F32)
    n_freq = HD // 4
    inv = ROPE_THETA ** (-jnp.arange(n_freq, dtype=F32) / n_freq)
    ang = jnp.concatenate([row[:, None] * inv, col[:, None] * inv], axis=-1)
    cos = jnp.repeat(jnp.cos(ang), 2, axis=-1)
    sin = jnp.repeat(jnp.sin(ang), 2, axis=-1) * jnp.tile(jnp.array([-1.0, 1.0], F32), HD // 2)
    nblk = DEC_SEQ // QKV_TB
    cos_tab = jnp.concatenate([jnp.ones((1, QKV_TB, HD), F32), cos.reshape(nblk, QKV_TB, HD)], axis=0)
    sin_tab = jnp.concatenate([jnp.zeros((1, QKV_TB, HD), F32), sin.reshape(nblk, QKV_TB, HD)], axis=0)
    return cos_tab, sin_tab


def kernel(x_prompt, x_sample, state_gla, cache_k, cache_v, c, c_ctx, w_mod, b_mod, norm1_g, norm2_g,
           w_in_even, w_gate_up, b_gate_up, gla_norm_g, w_spatial, b_spatial, w_out_even,
           w_in_odd, q_norm_g, k_norm_g, w_out_odd, w_router_group, b_router_group,
           w_router_expert, b_router_expert, w_exp_gate, w_exp_up, w_exp_down, final_norm_g):
    x_all = None
    cond8 = jnp.concatenate([c_ctx[None], c, jnp.zeros((3, D), F32)], axis=0)
    mod = _modulation(cond8, w_mod, b_mod)
    cos_tab, sin_tab = _rope_tables()
    zero_state = jnp.zeros((1, 2, QK_W, GLA_DV), F32)
    state_in = state_gla.reshape(DEC_BATCH, -1, 2, QK_W, GLA_DV)
    cache_k2 = cache_k.reshape(DEC_BATCH, -1, SEQ * ATT_KV, HD)
    cache_v2 = cache_v.reshape(DEC_BATCH, -1, SEQ * ATT_KV, HD)

    gla_states, caches = [], None
    for l in range(DEPTH):
        i = l // 2
        n1g = norm1_g[l][None]
        if l % 2 == 0:
            w = w_in_even[i]
            win1, win2 = w[:, :EVEN_W1].astype(BF16), w[:, EVEN_SPLIT:].astype(BF16)
            wgu = jnp.zeros((128, 2 * QK_W), F32)
            wgu = wgu.at[0:GLA_RANK, 0:QK_W].set(w_gate_up[i, 0])
            wgu = wgu.at[GLA_RANK:2 * GLA_RANK, QK_W:].set(w_gate_up[i, 1]).astype(BF16)
            bgu = b_gate_up[i].reshape(1, 2 * QK_W)
            args = (mod[l], n1g, win1, win2, wgu, bgu, gla_norm_g[i][None], w_spatial[i].astype(BF16),
                    b_spatial[i].T, w_out_even[i].astype(BF16))
            first = l == 0
            x_all, st = _even_mixer(x_all, *args, zero_state, latent=False,
                                    x_first=x_prompt.reshape(N_CTX, D) if first else None)
            gla_states.append(st)
            x_all, _ = _even_mixer(x_all, *args, state_in[:, i], latent=True,
                                   x_first=x_sample.reshape(N_LAT, D) if first else None)
        else:
            q, k, v, *caches = _qkv(x_all, mod[l], n1g, w_in_odd[i].astype(BF16), q_norm_g[i][None],
                                    k_norm_g[i][None], cos_tab, sin_tab, i, caches)
            wout = w_out_odd[i].astype(BF16)
            x_all = _attention(x_all, mod[l], q, k, v, wout)
            x_all = _attention(x_all, mod[l], q, k, v, wout, cache_k2, cache_v2, layer_i=i)
        wr = jnp.concatenate([w_router_expert[l], w_router_group[l],
                              jnp.zeros((D, 128 - N_EXP - MOE_GROUPS), F32)], axis=1)
        br = jnp.concatenate([b_router_expert[l], b_router_group[l],
                              jnp.zeros((128 - N_EXP - MOE_GROUPS,), F32)])[None]
        x_all = _moe(x_all, mod[l], norm2_g[l][None], wr, br, w_exp_gate, w_exp_up, w_exp_down, l,
                     final_norm_g[None] if l == DEPTH - 1 else None)

    y_prompt = x_all[0].reshape(BATCH, SEQ, D)
    y_sample = x_all[1].reshape(DEC_BATCH, DEC_SEQ, D)
    new_state = jnp.stack(gla_states, axis=1).reshape(BATCH, -1, 2, GLA_HEADS, GLA_DK, GLA_DV)
    new_k, new_v = (a.reshape(BATCH, DEPTH // 2, SEQ, ATT_KV, HD) for a in caches)
    return (y_prompt, y_sample, new_state, new_k, new_v)
```

```python
import functools

import jax
import jax.numpy as jnp
import numpy as np
from jax import lax
from jax.experimental import pallas as pl
from jax.experimental.pallas import tpu as pltpu

F32 = jnp.float32
BF16 = jnp.bfloat16

D = 1024
BATCH, SEQ = 16, 256
DEC_BATCH, DEC_SEQ = 4, 1024
N_CTX = BATCH * SEQ
N_LAT = DEC_BATCH * DEC_SEQ
N_TOK = N_CTX + N_LAT
DEPTH = 4
EPS = 1e-6
GRID_W = 64
ROPE_THETA = 10000.0

GLA_HEADS, GLA_DK, GLA_DV, GLA_RANK, GLA_CHUNK, GLA_TAU = 4, 64, 128, 16, 128, 16.0
QK_W = GLA_HEADS * GLA_DK
V_W = GLA_HEADS * GLA_DV
GMLP_GROUPS, GMLP_DIM, GMLP_CHUNK = 4, 128, 128
GMLP_W = GMLP_GROUPS * GMLP_DIM
C_Q, C_K, C_V, C_G, C_A, C_U, C_VG = 0, 256, 512, 1024, 1536, 1664, 2176
EVEN_SPLIT = 1568
EVEN_W1 = C_U
EVEN_PACK = 2688

ATT_HEADS, ATT_KV, HD = 8, 2, 128
ATT_G = ATT_HEADS // ATT_KV
Q_W = ATT_HEADS * HD
KV_W = ATT_KV * HD

MOE_GROUPS, MOE_PER_GROUP = 4, 8
N_EXP = MOE_GROUPS * MOE_PER_GROUP
D_EXP = D // 4
NEG = -1e30

MIB = 1024 * 1024
V7X_VMEM_MIB = 64
LANES, SUBLANES = 128, 8


def _cp(*sem, vmem_mib=32):
    assert vmem_mib < V7X_VMEM_MIB
    return pltpu.CompilerParams(dimension_semantics=sem, vmem_limit_bytes=vmem_mib * MIB)


def _dot(a, b):
    return jnp.dot(a.astype(BF16), b.astype(BF16), preferred_element_type=F32)


def _dot_nt(a, b):
    return lax.dot_general(a.astype(BF16), b.astype(BF16), (((1,), (1,)), ((), ())),
                           preferred_element_type=F32)


def _dot_tn(a, b):
    return lax.dot_general(a.astype(BF16), b.astype(BF16), (((0,), (0,)), ((), ())),
                           preferred_element_type=F32)


def _rms(x, g):
    return x * lax.rsqrt(jnp.mean(x * x, axis=-1, keepdims=True) + EPS) * g


def _silu(x):
    return x * jax.nn.sigmoid(x)


def _gelu(x):
    return 0.5 * x * (1.0 + jnp.tanh(np.sqrt(2.0 / np.pi).astype(np.float32) * (x + 0.044715 * (x * x * x))))


def _log_sigmoid(z):
    return jnp.minimum(z, 0.0) - jnp.log(1.0 + jnp.exp(-jnp.abs(z)))


assert D == LANES * SUBLANES


def _rows_to_tiles(ref, x):
    rows = x.shape[0]
    for j in range(SUBLANES):
        ref[pl.ds(j, rows, stride=SUBLANES), :] = x[:, j * LANES:(j + 1) * LANES]


def _tiles_to_rows(ref, rows):
    return jnp.concatenate([ref[pl.ds(j, rows, stride=SUBLANES), :] for j in range(SUBLANES)], axis=1)


def _tile_of(ref, row8):
    return ref.at[pl.ds(pl.multiple_of(row8, SUBLANES), SUBLANES), :]


def _resident(shape, index_map):
    return pl.BlockSpec(shape, index_map, pipeline_mode=pl.Buffered(1))


def _split_bf16(x):
    hi = x.astype(BF16)
    return hi, (x - hi.astype(F32)).astype(BF16)


def _mod_kernel(cond_ref, w_ref, b_ref, o_ref):
    s_hi, s_lo = _split_bf16(_silu(cond_ref[...]))
    w_hi, w_lo = _split_bf16(w_ref[...])
    dot = functools.partial(jnp.dot, preferred_element_type=F32)
    both = dot(jnp.concatenate([s_hi, s_lo], axis=0), w_hi)
    o_ref[...] = both[0:8] + both[8:16] + dot(s_hi, w_lo) + b_ref[...]


def _modulation(cond8, w_mod, b_mod):
    tn = 2048
    out = pl.pallas_call(
        _mod_kernel,
        grid=(DEPTH, 6 * D // tn),
        in_specs=[
            pl.BlockSpec((8, D), lambda l, j: (0, 0)),
            pl.BlockSpec((None, D, tn), lambda l, j: (l, 0, j)),
            pl.BlockSpec((None, 1, tn), lambda l, j: (l, 0, j)),
        ],
        out_specs=pl.BlockSpec((None, 8, tn), lambda l, j: (l, 0, j)),
        out_shape=jax.ShapeDtypeStruct((DEPTH, 8, 6 * D), F32),
        compiler_params=_cp("arbitrary", "arbitrary"),
        name="adaln_mod",
    )(cond8, w_mod, b_mod.reshape(DEPTH, 1, 6 * D))
    return out.reshape(DEPTH, 8, 6, D)


def _even_kernel(x_ref, mod_ref, n1g_ref, win1_ref, win2_ref, wgu_ref, bgu_ref, glag_ref, ws_ref, bs_ref,
                 wout_ref, s0_ref, xo_ref, st_ref, proj, la, o_f, o_b, st_scr, *, T):
    n_chunks = T // GLA_CHUNK
    shift, scale, gate = mod_ref[0:1, :], mod_ref[1:2, :], mod_ref[2:3, :]
    RB = 128
    PROJ_RB = 256

    def proj_body(r, carry):
        r0 = pl.multiple_of(r * PROJ_RB, PROJ_RB)
        h = _rms(x_ref[pl.ds(r0, PROJ_RB), :], n1g_ref[...]) * (1.0 + scale) + shift
        hb = h.astype(BF16)
        p = _dot(hb, win1_ref[...])
        proj[pl.ds(r0, PROJ_RB), 0:EVEN_W1] = p
        proj[pl.ds(r0, PROJ_RB), EVEN_W1:] = _dot(hb, win2_ref[...])
        z = _dot(p[:, C_A:C_A + 128], wgu_ref[...]) + bgu_ref[...]
        la[pl.ds(r0, PROJ_RB), :] = _log_sigmoid(z) * (1.0 / GLA_TAU)
        return carry

    lax.fori_loop(0, T // PROJ_RB, proj_body, 0)

    st_scr[0] = s0_ref[0].T
    st_scr[1] = s0_ref[1].T

    ci = lax.broadcasted_iota(jnp.int32, (GLA_CHUNK, GLA_CHUNK), 0)
    cj = lax.broadcasted_iota(jnp.int32, (GLA_CHUNK, GLA_CHUNK), 1)
    tri = (jnp.where(ci >= cj, 1.0, 0.0).astype(BF16), jnp.where(ci <= cj, 1.0, 0.0).astype(BF16))
    ai = lax.broadcasted_iota(jnp.int32, (GLA_HEADS * GLA_CHUNK, GLA_CHUNK), 0) % GLA_CHUNK
    aj = lax.broadcasted_iota(jnp.int32, (GLA_HEADS * GLA_CHUNK, GLA_CHUNK), 1)
    amask = (ai >= aj, ai <= aj)
    lane_head = lax.broadcasted_iota(jnp.int32, (1, QK_W), 1) // GLA_DK
    hmask = [jnp.where(lane_head == h, 1.0, 0.0) for h in range(GLA_HEADS)]

    def chunk_body(i, carry):
        for d in range(2):
            c = i if d == 0 else n_chunks - 1 - i
            r0 = pl.multiple_of(c * GLA_CHUNK, GLA_CHUNK)
            q = proj[pl.ds(r0, GLA_CHUNK), C_Q:C_Q + QK_W] * (GLA_DK ** -0.5)
            k = proj[pl.ds(r0, GLA_CHUNK), C_K:C_K + QK_W]
            v = proj[pl.ds(r0, GLA_CHUNK), C_V:C_V + V_W]
            lac = la[pl.ds(r0, GLA_CHUNK), d * QK_W:(d + 1) * QK_W]
            hi = lac.astype(BF16)
            lo = (lac - hi.astype(F32)).astype(BF16)
            b = (jnp.dot(tri[d], hi, preferred_element_type=F32)
                 + jnp.dot(tri[d], lo, preferred_element_type=F32))
            bend = b[GLA_CHUNK - 1:GLA_CHUNK, :] if d == 0 else b[0:1, :]
            qe = q * jnp.exp(b)
            ke = k * jnp.exp(-b)
            kd = k * jnp.exp(bend - b)
            st = st_scr[d]
            qstack = jnp.concatenate([qe * hmask[h] for h in range(GLA_HEADS)], axis=0).astype(BF16)
            att = jnp.where(amask[d], _dot_nt(qstack, ke), 0.0)
            inter = _dot_nt(qstack, st)
            outs = []
            for h in range(GLA_HEADS):
                rows = slice(h * GLA_CHUNK, (h + 1) * GLA_CHUNK)
                outs.append(_dot(att[rows], v[:, h * GLA_DV:(h + 1) * GLA_DV]) + inter[rows])
            o = jnp.concatenate(outs, axis=1)
            if d == 0:
                o_f[pl.ds(r0, GLA_CHUNK), :] = o
            else:
                o_b[pl.ds(r0, GLA_CHUNK), :] = o
            vstack = jnp.concatenate([v[:, h * GLA_DV:(h + 1) * GLA_DV] for h in range(GLA_HEADS)], axis=0)
            kstack = jnp.concatenate([kd * hmask[h] for h in range(GLA_HEADS)], axis=0)
            st_scr[d] = st * jnp.exp(bend) + _dot_tn(vstack, kstack)
        return carry

    lax.fori_loop(0, n_chunks, chunk_body, 0, unroll=2)
    st_ref[0] = st_scr[0].T
    st_ref[1] = st_scr[1].T

    def out_body(r, carry):
        r0 = pl.multiple_of(r * RB, RB)
        osum = o_f[pl.ds(r0, RB), :] + o_b[pl.ds(r0, RB), :]
        g = proj[pl.ds(r0, RB), C_G:C_G + V_W]
        u = proj[pl.ds(r0, RB), C_U:C_U + GMLP_W]
        vg = _gelu(proj[pl.ds(r0, RB), C_VG:C_VG + GMLP_W])
        parts = []
        for h in range(GLA_HEADS):
            oh = osum[:, h * GLA_DV:(h + 1) * GLA_DV]
            parts.append(_rms(oh, glag_ref[...]) * _silu(g[:, h * GLA_DV:(h + 1) * GLA_DV]))
        for gi in range(GMLP_GROUPS):
            vc = vg[:, gi * GMLP_DIM:(gi + 1) * GMLP_DIM]
            vc = vc - jnp.mean(vc, axis=-1, keepdims=True)
            vn = vc * lax.rsqrt(jnp.mean(vc * vc, axis=-1, keepdims=True) + EPS)
            sg = _dot(ws_ref[gi], vn) + bs_ref[:, gi:gi + 1]
            parts.append(_gelu(u[:, gi * GMLP_DIM:(gi + 1) * GMLP_DIM]) * sg)
        mix = jnp.concatenate(parts, axis=1)
        y = _dot(mix, wout_ref[...])
        xo_ref[pl.ds(r0, RB), :] = x_ref[pl.ds(r0, RB), :] + gate * y
        return carry

    lax.fori_loop(0, T // RB, out_body, 0)


def _even_mixer(x_all, mod_l, n1g, win1, win2, wgu, bgu, glag, ws, bs, wout, s0, *, latent, x_first=None):
    if latent:
        T, nseq, blk0 = DEC_SEQ, DEC_BATCH, N_CTX // DEC_SEQ
        cond = lambda i: 1 + i
        s0_spec = pl.BlockSpec((None, 2, QK_W, GLA_DV), lambda i: (i, 0, 0, 0))
    else:
        T, nseq, blk0 = SEQ, BATCH, 0
        cond = lambda i: 0
        s0_spec = pl.BlockSpec((None, 2, QK_W, GLA_DV), lambda i: (0, 0, 0, 0))
    const2 = lambda i: (0, 0)
    body = functools.partial(_even_kernel, T=T)
    x_spec = pl.BlockSpec((T, D), lambda i: (blk0 + i, 0))
    if x_first is None:
        lead_specs, lead_args, aliases = [x_spec], (x_all,), {0: 0}
    elif x_all is None:
        lead_specs, lead_args, aliases = [pl.BlockSpec((T, D), lambda i: (i, 0))], (x_first,), {}
    else:
        lead_specs = [pl.BlockSpec(memory_space=pl.ANY), pl.BlockSpec((T, D), lambda i: (i, 0))]
        lead_args, aliases = (x_all, x_first), {0: 0}
        body = lambda dst_ref, *refs: _even_kernel(*refs, T=T)
    x_new, states = pl.pallas_call(
        body,
        grid=(nseq,),
        in_specs=lead_specs + [
            pl.BlockSpec((None, 6, D), lambda i: (cond(i), 0, 0)),
            _resident((1, D), const2),
            _resident((D, EVEN_W1), const2),
            _resident((D, EVEN_PACK - EVEN_W1), const2),
            _resident((128, 2 * QK_W), const2),
            _resident((1, 2 * QK_W), const2),
            _resident((1, GLA_DV), const2),
            _resident((GMLP_GROUPS, GMLP_CHUNK, GMLP_CHUNK), lambda i: (0, 0, 0)),
            _resident((GMLP_CHUNK, GMLP_GROUPS), const2),
            _resident((D, D), const2),
            s0_spec,
        ],
        out_specs=[
            x_spec,
            pl.BlockSpec((None, 2, QK_W, GLA_DV), lambda i: (i, 0, 0, 0)),
        ],
        out_shape=[
            jax.ShapeDtypeStruct((N_TOK, D), F32),
            jax.ShapeDtypeStruct((nseq, 2, QK_W, GLA_DV), F32),
        ],
        scratch_shapes=[
            pltpu.VMEM((T, EVEN_PACK), F32),
            pltpu.VMEM((T, 2 * QK_W), F32),
            pltpu.VMEM((T, V_W), F32),
            pltpu.VMEM((T, V_W), F32),
            pltpu.VMEM((2, GLA_DV, QK_W), F32),
        ],
        input_output_aliases=aliases,
        compiler_params=_cp("arbitrary", vmem_mib=48 if latent else 32),
        name="even_mixer_latent" if latent else "even_mixer_context",
    )(*lead_args, mod_l, n1g, win1, win2, wgu, bgu, glag, ws, bs, wout, s0)
    return x_new, states


QKV_TB = 512


def _qkv_kernel(x_ref, mod_ref, n1g_ref, win_ref, gq_ref, gk_ref, cos_ref, sin_ref, q_ref, k_ref, v_ref,
                ck_ref, cv_ref):
    shift, scale = mod_ref[0:1, :], mod_ref[1:2, :]
    h = _rms(x_ref[...], n1g_ref[...]) * (1.0 + scale) + shift
    p = _dot(h, win_ref[...])
    cos, sin = cos_ref[...], sin_ref[...]
    even_lane = lax.broadcasted_iota(jnp.int32, (1, HD), 1) % 2 == 0

    def rope(xn):
        swapped = jnp.where(even_lane, pltpu.roll(xn, HD - 1, axis=1), pltpu.roll(xn, 1, axis=1))
        return xn * cos + swapped * sin

    def emit(rotate, to_cache):
        for hh in range(ATT_HEADS):
            qn = _rms(p[:, hh * HD:(hh + 1) * HD], gq_ref[...])
            q_ref[:, hh * HD:(hh + 1) * HD] = (rotate(qn) * (HD ** -0.5)).astype(BF16)
        for hh in range(ATT_KV):
            kn = rotate(_rms(p[:, Q_W + hh * HD:Q_W + (hh + 1) * HD], gk_ref[...]))
            k_ref[:, hh * HD:(hh + 1) * HD] = kn
            if to_cache:
                for s in range(QKV_TB // SEQ):
                    ck_ref[s, pl.ds(hh, SEQ, stride=ATT_KV), :] = kn[s * SEQ:(s + 1) * SEQ]
                    cv_ref[s, pl.ds(hh, SEQ, stride=ATT_KV), :] = p[s * SEQ:(s + 1) * SEQ,
                                                                    Q_W + KV_W + hh * HD:Q_W + KV_W + (hh + 1) * HD]

    is_latent = pl.program_id(0) >= N_CTX // QKV_TB

    @pl.when(is_latent)
    def _():
        emit(rope, False)

    @pl.when(jnp.logical_not(is_latent))
    def _():
        emit(lambda xn: xn, True)

    v_ref[...] = p[:, Q_W + KV_W:]


def _qkv(x_all, mod_l, n1g, win, gq, gk, cos_tab, sin_tab, layer_i, caches=None):
    nb_ctx = N_CTX // QKV_TB
    per_seq = DEC_SEQ // QKV_TB
    cond = lambda i: jnp.where(i < nb_ctx, 0, 1 + (i - nb_ctx) // per_seq)
    tab = lambda i: jnp.where(i < nb_ctx, 0, 1 + (i - nb_ctx) % per_seq)
    const2 = lambda i: (0, 0)
    cache_spec = pl.BlockSpec((QKV_TB // SEQ, None, SEQ * ATT_KV, HD),
                              lambda i: (jnp.minimum(i, nb_ctx - 1), layer_i, 0, 0))
    cache_shape = jax.ShapeDtypeStruct((BATCH, DEPTH // 2, SEQ * ATT_KV, HD), F32)
    if caches is None:
        body, lead_specs, lead_args, aliases = _qkv_kernel, [], (), {}
    else:
        body = lambda ck_in, cv_in, *refs: _qkv_kernel(*refs)
        lead_specs = [pl.BlockSpec(memory_space=pl.ANY)] * 2
        lead_args, aliases = tuple(caches), {0: 3, 1: 4}
    return pl.pallas_call(
        body,
        grid=(N_TOK // QKV_TB,),
        in_specs=lead_specs + [
            pl.BlockSpec((QKV_TB, D), lambda i: (i, 0)),
            pl.BlockSpec((None, 6, D), lambda i: (cond(i), 0, 0)),
            _resident((1, D), const2),
            _resident((D, Q_W + 2 * KV_W), const2),
            _resident((1, HD), const2),
            _resident((1, HD), const2),
            pl.BlockSpec((None, QKV_TB, HD), lambda i: (tab(i), 0, 0)),
            pl.BlockSpec((None, QKV_TB, HD), lambda i: (tab(i), 0, 0)),
        ],
        out_specs=[
            pl.BlockSpec((QKV_TB, Q_W), lambda i: (i, 0)),
            pl.BlockSpec((QKV_TB, KV_W), lambda i: (i, 0)),
            pl.BlockSpec((QKV_TB, KV_W), lambda i: (i, 0)),
            cache_spec,
            cache_spec,
        ],
        out_shape=[
            jax.ShapeDtypeStruct((N_TOK, Q_W), BF16),
            jax.ShapeDtypeStruct((N_TOK, KV_W), F32),
            jax.ShapeDtypeStruct((N_TOK, KV_W), F32),
            cache_shape,
            cache_shape,
        ],
        input_output_aliases=aliases,
        compiler_params=_cp("arbitrary"),
        name="odd_qkv",
    )(*lead_args, x_all, mod_l, n1g, win, gq, gk, cos_tab, sin_tab)


ATT_TQ_LATENT = 512


def _attn_kernel(*refs, n_kv):
    q_ref = refs[0]
    kv_refs = refs[1:1 + 2 * n_kv]
    x_ref, mod_ref, wout_ref, xo_ref, att_scr = refs[1 + 2 * n_kv:]
    gate = mod_ref[2:3, :]
    def head(ref, kh):
        if ref.shape[1] == HD:
            return ref[pl.ds(kh, ref.shape[0] // ATT_KV, stride=ATT_KV), :].astype(BF16)
        return ref[:, kh * HD:(kh + 1) * HD].astype(BF16)

    for kh in range(ATT_KV):
        ks = [head(kv_refs[2 * s], kh) for s in range(n_kv)]
        vs = [jnp.concatenate([vh, jnp.ones_like(vh)], axis=1)
              for vh in (head(kv_refs[2 * s + 1], kh) for s in range(n_kv))]
        for g in range(ATT_G):
            hh = kh * ATT_G + g
            qh = q_ref[:, hh * HD:(hh + 1) * HD]
            ss = [_dot_nt(qh, kk) for kk in ks]
            m = ss[0].max(axis=-1, keepdims=True)
            for s in ss[1:]:
                m = jnp.maximum(m, s.max(axis=-1, keepdims=True))
            o = _dot(jnp.exp(ss[0] - m), vs[0])
            for s, vv in zip(ss[1:], vs[1:]):
                o = o + _dot(jnp.exp(s - m), vv)
            att_scr[:, hh * HD:(hh + 1) * HD] = o[:, :HD] / o[:, HD:HD + 1]
    y = _dot(att_scr[...], wout_ref[...])
    xo_ref[...] = x_ref[...] + gate * y


def _attention(x_all, mod_l, q, k, v, wout, cache_k=None, cache_v=None, layer_i=0):
    latent = cache_k is not None
    const2 = lambda *a: (0, 0)
    tq = ATT_TQ_LATENT if latent else SEQ
    if latent:
        nq = DEC_SEQ // tq
        row_blk = lambda b, j: (N_CTX // tq + b * nq + j, 0)
        grid = (DEC_BATCH, nq)
        kv_specs = [
            pl.BlockSpec((None, None, SEQ * ATT_KV, HD), lambda b, j: (b, layer_i, 0, 0)),
            pl.BlockSpec((None, None, SEQ * ATT_KV, HD), lambda b, j: (b, layer_i, 0, 0)),
            pl.BlockSpec((DEC_SEQ, KV_W), lambda b, j: (N_CTX // DEC_SEQ + b, 0)),
            pl.BlockSpec((DEC_SEQ, KV_W), lambda b, j: (N_CTX // DEC_SEQ + b, 0)),
        ]
        kv_args = (cache_k, cache_v, k, v)
        mod_spec = pl.BlockSpec((None, 6, D), lambda b, j: (1 + b, 0, 0))
        sem = ("arbitrary", "arbitrary")
        n_kv = 2
    else:
        row_blk = lambda i: (i, 0)
        grid = (BATCH,)
        kv_specs = [pl.BlockSpec((SEQ, KV_W), row_blk), pl.BlockSpec((SEQ, KV_W), row_blk)]
        kv_args = (k, v)
        mod_spec = pl.BlockSpec((None, 6, D), lambda i: (0, 0, 0))
        sem = ("arbitrary",)
        n_kv = 1
    n_in = 1 + len(kv_args)
    return pl.pallas_call(
        functools.partial(_attn_kernel, n_kv=n_kv),
        grid=grid,
        in_specs=[pl.BlockSpec((tq, Q_W), row_blk)] + kv_specs + [
            pl.BlockSpec((tq, D), row_blk),
            mod_spec,
            _resident((D, D), const2),
        ],
        out_specs=pl.BlockSpec((tq, D), row_blk),
        out_shape=jax.ShapeDtypeStruct((N_TOK, D), F32),
        scratch_shapes=[pltpu.VMEM((tq, Q_W), F32)],
        input_output_aliases={n_in: 0},
        compiler_params=_cp(*sem),
        name="attention_latent" if latent else "attention_context",
    )(q, *kv_args, x_all, mod_l, wout)


ROUTE_TB = 512
HALF_TOK = N_TOK // 2
M_E1, M_E2, M_G1, M_G2, M_R1, M_R2 = 0, 1, 2, 3, 4, 5


def _router_kernel(x_ref, mod_ref, n2g_ref, w2_ref, br_ref, h_ref, metat_ref, cnt_ref, run):
    @pl.when(pl.program_id(0) % (HALF_TOK // ROUTE_TB) == 0)
    def _():
        run[...] = jnp.zeros_like(run)

    shift, scale = mod_ref[3:4, :], mod_ref[4:5, :]
    h = _rms(x_ref[...], n2g_ref[...]) * (1.0 + scale) + shift
    _rows_to_tiles(h_ref, h)
    h_hi, h_lo = _split_bf16(h)
    dot = functools.partial(jnp.dot, preferred_element_type=F32)
    wide = dot(h_hi, w2_ref[...])
    logits = wide[:, :128] + wide[:, 128:] + dot(h_lo, w2_ref[:, :128]) + br_ref[...]
    lane = lax.broadcasted_iota(jnp.int32, logits.shape, 1).astype(F32)
    big = 1e4

    def first_argmax(vals):
        m = vals.max(axis=-1, keepdims=True)
        return m, jnp.where(vals == m, lane, big).min(axis=-1, keepdims=True)

    gl = jnp.where((lane >= N_EXP) & (lane < N_EXP + MOE_GROUPS), logits, NEG)
    gmax, glane = first_argmax(gl)
    g_p = 1.0 / jnp.exp(gl - gmax).sum(axis=-1, keepdims=True)
    lo = (glane - N_EXP) * MOE_PER_GROUP
    el = jnp.where((lane >= lo) & (lane < lo + MOE_PER_GROUP), logits, NEG)
    m1, i1 = first_argmax(el)
    m2, i2 = first_argmax(jnp.where(lane == i1, NEG, el))
    t = jnp.exp(m2 - m1)
    w1 = 1.0 / (1.0 + t)
    sel1, sel2 = lane == i1, lane == i2
    onehot = jnp.where(sel1 | sel2, 1.0, 0.0)
    ri = lax.broadcasted_iota(jnp.int32, (ROUTE_TB, ROUTE_TB), 0)
    rj = lax.broadcasted_iota(jnp.int32, (ROUTE_TB, ROUTE_TB), 1)
    before = _dot(jnp.where(ri > rj, 1.0, 0.0), onehot) + run[...]
    r1 = jnp.where(sel1, before, 0.0).sum(axis=-1, keepdims=True)
    r2 = jnp.where(sel2, before, 0.0).sum(axis=-1, keepdims=True)
    run[...] += onehot.sum(axis=0, keepdims=True)
    cnt_ref[...] = run[...]
    meta = jnp.zeros_like(logits)
    for j, val in enumerate([i1, i2, w1 * g_p, (t * w1) * g_p, r1, r2]):
        meta = jnp.where(lane == j, val, meta)
    metat_ref[...] = meta.T[0:8, :]


def _router(x_all, mod_l, n2g, wr, br):
    w2 = jnp.concatenate(_split_bf16(wr), axis=1)
    nb_ctx = N_CTX // ROUTE_TB
    per_seq = DEC_SEQ // ROUTE_TB
    cond = lambda i: jnp.where(i < nb_ctx, 0, 1 + (i - nb_ctx) // per_seq)
    const2 = lambda i: (0, 0)
    return pl.pallas_call(
        _router_kernel,
        grid=(N_TOK // ROUTE_TB,),
        in_specs=[
            pl.BlockSpec((ROUTE_TB, D), lambda i: (i, 0)),
            pl.BlockSpec((None, 6, D), lambda i: (cond(i), 0, 0)),
            _resident((1, D), const2),
            _resident((D, 256), const2),
            _resident((1, 128), const2),
        ],
        out_specs=[
            pl.BlockSpec((ROUTE_TB * 8, 128), lambda i: (i, 0)),
            pl.BlockSpec((8, ROUTE_TB), lambda i: (0, i)),
            pl.BlockSpec((None, 1, 128), lambda i: (i // (HALF_TOK // ROUTE_TB), 0, 0)),
        ],
        out_shape=[
            jax.ShapeDtypeStruct((N_TOK * 8, 128), F32),
            jax.ShapeDtypeStruct((8, N_TOK), F32),
            jax.ShapeDtypeStruct((2, 1, 128), F32),
        ],
        scratch_shapes=[pltpu.VMEM((1, 128), F32)],
        compiler_params=_cp("arbitrary"),
        name="moe_router",
    )(x_all, mod_l, n2g, w2, br)


EXP_TM = 128
N_ASSIGN = 2 * N_TOK
N_GROUPS = 2 * N_EXP
MAX_TILES = N_ASSIGN // EXP_TM + N_GROUPS
N_SORTED = MAX_TILES * EXP_TM
ORDER_BLK = 4096
CODE_PLANE = 2 * HALF_TOK
CODE_MASK = 8 * CODE_PLANE - 1
DUMMY8 = HALF_TOK * 8


def _order_kernel(pos1_ref, pos2_ref, pad_lo_ref, pad_hi_ref, src_ref):
    i = pl.program_id(0)
    local = (i % (HALF_TOK // ORDER_BLK)) * ORDER_BLK

    def body(t, carry):
        src_ref[pos1_ref[t]] = (local + t) * 8
        src_ref[pos2_ref[t]] = (local + t + CODE_PLANE) * 8
        return carry

    lax.fori_loop(0, ORDER_BLK, body, 0, unroll=16)

    @pl.when(i == 0)
    def _():
        def group(g, carry):
            def pad(p, c):
                src_ref[p] = DUMMY8
                return c
            return lax.fori_loop(pad_lo_ref[g], pad_hi_ref[g], pad, carry)

        lax.fori_loop(0, N_GROUPS, group, 0)


def _order(pos, pad_lo, pad_hi):
    return pl.pallas_call(
        _order_kernel,
        grid=(N_TOK // ORDER_BLK,),
        in_specs=[
            pl.BlockSpec((ORDER_BLK,), lambda i: (i,), memory_space=pltpu.SMEM),
            pl.BlockSpec((ORDER_BLK,), lambda i: (N_TOK // ORDER_BLK + i,), memory_space=pltpu.SMEM),
            pl.BlockSpec(memory_space=pltpu.SMEM),
            pl.BlockSpec(memory_space=pltpu.SMEM),
        ],
        out_specs=pl.BlockSpec(memory_space=pltpu.SMEM),
        out_shape=jax.ShapeDtypeStruct((N_SORTED,), jnp.int32),
        compiler_params=_cp("arbitrary"),
        name="moe_order",
    )(pos, pos, pad_lo, pad_hi)


GATE_BLK = CODE_PLANE + HALF_TOK
ACC_TOK = HALF_TOK + 64
GATHER_GROUP, ACC_GROUP = 16, 8
TAIL_ROWS = 2 * EXP_TM + EXP_TM // 2


RES_TB = 256


EXP_PER_STEP = 2


def _experts_kernel(tile0_ref, ntile_ref, count_ref, src_ref, gs_ref, h_hbm, x_hbm, mod_ref, *rest, final):
    weights, rest = rest[:3 * EXP_PER_STEP], rest[3 * EXP_PER_STEP:]
    if final:
        fg_ref, *dst_hbm = rest[:3]
        rest = rest[3:]
    else:
        dst_hbm, rest = rest[:1], rest[1:]
    h_res, acc, xbuf, ybuf, wgb, wub, wdb, xin, xout, sem, in_sem, out_sem = rest
    first_group = pl.program_id(0) * EXP_PER_STEP
    half = first_group // N_EXP
    rows0 = pl.multiple_of(half * (HALF_TOK * 8), 8)

    @pl.when(first_group % N_EXP == 0)
    def _():
        cp = pltpu.make_async_copy(h_hbm.at[pl.ds(rows0, HALF_TOK * 8), :], h_res.at[pl.ds(0, HALF_TOK * 8), :], sem)
        cp.start()
        h_res[pl.ds(DUMMY8, 8), :] = jnp.zeros((8, 128), F32)
        xbuf[...] = jnp.zeros_like(xbuf)

        def zero(i, carry):
            acc[pl.ds(pl.multiple_of(i * 512, 512), 512), :] = jnp.zeros((512, 128), F32)
            return carry

        lax.fori_loop(0, ACC_TOK * 8 // 512, zero, 0)
        cp.wait()

    def run_group(group, wg_ref, wu_ref, wd_ref):
        n_tiles = ntile_ref[group]

        @pl.when(n_tiles > 0)
        def _():
            wgb[...] = wg_ref[...].astype(BF16)
            wub[...] = wu_ref[...].astype(BF16)
            wdb[...] = wd_ref[...].astype(BF16)

        row0 = tile0_ref[group] * EXP_TM
        row_end = row0 + count_ref[group]

        def process(base, rows):
            live = (jnp.clip(row_end - base, 0, rows) + GATHER_GROUP - 1) // GATHER_GROUP

            def gather(g, c):
                for i in range(GATHER_GROUP):
                    r = g * GATHER_GROUP + i
                    xbuf[pl.ds(pl.multiple_of(r * 8, 8), 8), :] = _tile_of(h_res, src_ref[base + r] & CODE_MASK)[...]
                return c

            lax.fori_loop(0, live, gather, 0)
            x = _tiles_to_rows(xbuf, rows).astype(BF16)
            hid = _silu(_dot(x, wgb[...])) * _dot(x, wub[...])
            _rows_to_tiles(ybuf, _dot(hid, wdb[...]))

            def accumulate(g, c):
                targets, values = [], []
                for i in range(ACC_GROUP):
                    r = g * ACC_GROUP + i
                    code = src_ref[base + r]
                    target = _tile_of(acc, code & CODE_MASK)
                    targets.append(target)
                    values.append(target[...] + gs_ref[code >> 3] * ybuf[pl.ds(pl.multiple_of(r * 8, 8), 8), :])
                for target, value in zip(targets, values):
                    target[...] = value
                return c

            lax.fori_loop(0, live * (GATHER_GROUP // ACC_GROUP), accumulate, 0)

        ends_in_triple = (n_tiles % 2 == 1) & (n_tiles >= 3)
        n_pairs = jnp.where(ends_in_triple, (n_tiles - 3) // 2, n_tiles // 2)

        def pair_body(j, carry):
            process(row0 + j * (2 * EXP_TM), 2 * EXP_TM)
            return carry

        lax.fori_loop(0, n_pairs, pair_body, 0)

        @pl.when(ends_in_triple)
        def _():
            base = row0 + (n_tiles - 3) * EXP_TM
            short = row_end - base <= TAIL_ROWS

            @pl.when(short)
            def _():
                process(base, TAIL_ROWS)

            @pl.when(jnp.logical_not(short))
            def _():
                process(base, 3 * EXP_TM)

        @pl.when(n_tiles == 1)
        def _():
            process(row0, EXP_TM)

    for sub in range(EXP_PER_STEP):
        run_group(first_group + sub, *weights[3 * sub:3 * sub + 3])
    expert = (first_group + EXP_PER_STEP - 1) % N_EXP

    def rows_of(first, blk):
        return pl.ds(pl.multiple_of(first + blk * RES_TB, RES_TB), RES_TB)

    def load_x(blk, slot):
        return pltpu.make_async_copy(x_hbm.at[rows_of(half * HALF_TOK, blk), :], xin.at[slot], in_sem.at[slot])

    def residual(blk, slot):
        cond = jnp.where(half == 0, 0, 1 + blk // (DEC_SEQ // RES_TB))
        gate = mod_ref[cond, 5:6, :]
        y = _tiles_to_rows(acc.at[pl.ds(pl.multiple_of(blk * (RES_TB * 8), RES_TB * 8), RES_TB * 8), :], RES_TB)
        x_new = xin[slot] + gate * y
        xout[slot] = _rms(x_new, fg_ref[...]) if final else x_new

    def epilogue(dst, first_row):
        def store_x(blk, slot):
            return pltpu.make_async_copy(xout.at[slot], dst.at[rows_of(first_row, blk), :], out_sem.at[slot])

        n_pairs = HALF_TOK // RES_TB // 2
        load_x(0, 0).start()

        def pair(p, carry):
            for slot in range(2):
                blk = 2 * p + slot
                if slot == 0:
                    load_x(blk + 1, 1).start()
                else:
                    @pl.when(p + 1 < n_pairs)
                    def _():
                        load_x(blk + 1, 0).start()
                load_x(blk, slot).wait()

                @pl.when(p > 0)
                def _():
                    store_x(blk - 2, slot).wait()

                residual(blk, slot)
                store_x(blk, slot).start()
            return carry

        lax.fori_loop(0, n_pairs, pair, 0)
        store_x(2 * n_pairs - 2, 0).wait()
        store_x(2 * n_pairs - 1, 1).wait()

    if final:
        for which in range(2):
            @pl.when((expert == N_EXP - 1) & (half == which))
            def _():
                epilogue(dst_hbm[which], 0)
    else:
        @pl.when(expert == N_EXP - 1)
        def _():
            epilogue(dst_hbm[0], half * HALF_TOK)


def _experts(tile0, n_tiles, counts, src, gs, h, x_all, mod_l, wg, wu, wd, layer, final_g=None):
    final = final_g is not None
    any_spec = pl.BlockSpec(memory_space=pl.ANY)
    extra_specs = [pl.BlockSpec((1, D), lambda g, t0, nt, cnt, src: (0, 0))] if final else []
    extra_args = (final_g,) if final else ()
    weight_specs, weight_args = [], []
    for sub in range(EXP_PER_STEP):
        wmap = lambda g, t0, nt, cnt, src, sub=sub: (layer, (g * EXP_PER_STEP + sub) % N_EXP, 0, 0)
        weight_specs += [pl.BlockSpec((None, None, D, D_EXP), wmap), pl.BlockSpec((None, None, D, D_EXP), wmap),
                         pl.BlockSpec((None, None, D_EXP, D), wmap)]
        weight_args += [wg, wu, wd]
    return pl.pallas_call(
        functools.partial(_experts_kernel, final=final),
        grid_spec=pltpu.PrefetchScalarGridSpec(
            num_scalar_prefetch=4,
            grid=(N_GROUPS // EXP_PER_STEP,),
            in_specs=[
                pl.BlockSpec((GATE_BLK,), lambda g, t0, nt, cnt, src: (g * EXP_PER_STEP // N_EXP,),
                             memory_space=pltpu.SMEM),
                pl.BlockSpec(memory_space=pl.ANY),
                pl.BlockSpec(memory_space=pl.ANY),
                pl.BlockSpec((8, 6, D), lambda g, t0, nt, cnt, src: (0, 0, 0)),
            ] + weight_specs + extra_specs,
            out_specs=[any_spec, any_spec] if final else any_spec,
            scratch_shapes=[
                pltpu.VMEM((ACC_TOK * 8, 128), F32),
                pltpu.VMEM((ACC_TOK * 8, 128), F32),
                pltpu.VMEM((3 * EXP_TM * 8, 128), F32),
                pltpu.VMEM((3 * EXP_TM * 8, 128), F32),
                pltpu.VMEM((D, D_EXP), BF16),
                pltpu.VMEM((D, D_EXP), BF16),
                pltpu.VMEM((D_EXP, D), BF16),
                pltpu.VMEM((2, RES_TB, D), F32),
                pltpu.VMEM((2, RES_TB, D), F32),
                pltpu.SemaphoreType.DMA,
                pltpu.SemaphoreType.DMA((2,)),
                pltpu.SemaphoreType.DMA((2,)),
            ],
        ),
        out_shape=([jax.ShapeDtypeStruct((HALF_TOK, D), F32)] * 2 if final
                   else jax.ShapeDtypeStruct((N_TOK, D), F32)),
        input_output_aliases={} if final else {6: 0},
        compiler_params=_cp("arbitrary", vmem_mib=56),
        name="moe_experts_final" if final else "moe_experts",
    )(tile0, n_tiles, counts, src, gs, h, x_all, mod_l, *weight_args, *extra_args)


def _moe(x_all, mod_l, n2g, wr, br, wg, wu, wd, layer, final_g=None):
    h, metat, cnt = _router(x_all, mod_l, n2g, wr, br)
    counts = cnt[:, 0, :N_EXP].astype(jnp.int32).reshape(N_GROUPS)
    padded = (counts + EXP_TM - 1) // EXP_TM * EXP_TM
    ends = jnp.cumsum(padded)
    offs = ends - padded
    rec = metat.astype(jnp.int32)
    half = (jnp.arange(N_TOK, dtype=jnp.int32) // HALF_TOK)[None, :]
    group = rec[M_E1:M_E2 + 1] + N_EXP * half
    is_group = group[None] == jnp.arange(N_GROUPS, dtype=jnp.int32)[:, None, None]
    pos = jnp.sum(jnp.where(is_group, offs[:, None, None], 0), axis=0) + rec[M_R1:M_R2 + 1]
    live_end = offs + (counts + GATHER_GROUP - 1) // GATHER_GROUP * GATHER_GROUP
    src = _order(pos.reshape(N_ASSIGN), offs + counts, live_end)
    g12 = metat[M_G1:M_G2 + 1].reshape(2, 2, HALF_TOK)
    gates = jnp.concatenate([g12[0], jnp.zeros((2, CODE_PLANE - HALF_TOK), F32), g12[1]], axis=1)
    return _experts(offs // EXP_TM, padded // EXP_TM, counts, src, gates.reshape(2 * GATE_BLK), h, x_all, mod_l,
                    wg, wu, wd, layer, final_g)


def _rope_tables():
    pos = jnp.arange(DEC_SEQ)
    row = (pos // GRID_W).astype(F32)
    col = (pos % GRID_W).astype(F32)
    n_freq = HD // 4
    inv = ROPE_THETA ** (-jnp.arange(n_freq, dtype=F32) / n_freq)
    ang = jnp.concatenate([row[:, None] * inv, col[:, None] * inv], axis=-1)
    cos = jnp.repeat(jnp.cos(ang), 2, axis=-1)
    sin = jnp.repeat(jnp.sin(ang), 2, axis=-1) * jnp.tile(jnp.array([-1.0, 1.0], F32), HD // 2)
    nblk = DEC_SEQ // QKV_TB
    cos_tab = jnp.concatenate([jnp.ones((1, QKV_TB, HD), F32), cos.reshape(nblk, QKV_TB, HD)], axis=0)
    sin_tab = jnp.concatenate([jnp.zeros((1, QKV_TB, HD), F32), sin.reshape(nblk, QKV_TB, HD)], axis=0)
    return cos_tab, sin_tab


def kernel(x_prompt, x_sample, state_gla, cache_k, cache_v, c, c_ctx, w_mod, b_mod, norm1_g, norm2_g,
           w_in_even, w_gate_up, b_gate_up, gla_norm_g, w_spatial, b_spatial, w_out_even,
           w_in_odd, q_norm_g, k_norm_g, w_out_odd, w_router_group, b_router_group,
           w_router_expert, b_router_expert, w_exp_gate, w_exp_up, w_exp_down, final_norm_g):
    x_all = None
    cond8 = jnp.concatenate([c_ctx[None], c, jnp.zeros((3, D), F32)], axis=0)
    mod = _modulation(cond8, w_mod, b_mod)
    cos_tab, sin_tab = _rope_tables()
    zero_state = jnp.zeros((1, 2, QK_W, GLA_DV), F32)
    state_in = state_gla.reshape(DEC_BATCH, -1, 2, QK_W, GLA_DV)
    cache_k2 = cache_k.reshape(DEC_BATCH, -1, SEQ * ATT_KV, HD)
    cache_v2 = cache_v.reshape(DEC_BATCH, -1, SEQ * ATT_KV, HD)

    gla_states, caches = [], None
    for l in range(DEPTH):
        i = l // 2
        n1g = norm1_g[l][None]
        if l % 2 == 0:
            w = w_in_even[i]
            win1, win2 = w[:, :EVEN_W1].astype(BF16), w[:, EVEN_SPLIT:].astype(BF16)
            wgu = jnp.zeros((128, 2 * QK_W), F32)
            wgu = wgu.at[0:GLA_RANK, 0:QK_W].set(w_gate_up[i, 0])
            wgu = wgu.at[GLA_RANK:2 * GLA_RANK, QK_W:].set(w_gate_up[i, 1]).astype(BF16)
            bgu = b_gate_up[i].reshape(1, 2 * QK_W)
            args = (mod[l], n1g, win1, win2, wgu, bgu, gla_norm_g[i][None], w_spatial[i].astype(BF16),
                    b_spatial[i].T, w_out_even[i].astype(BF16))
            first = l == 0
            x_all, st = _even_mixer(x_all, *args, zero_state, latent=False,
                                    x_first=x_prompt.reshape(N_CTX, D) if first else None)
            gla_states.append(st)
            x_all, _ = _even_mixer(x_all, *args, state_in[:, i], latent=True,
                                   x_first=x_sample.reshape(N_LAT, D) if first else None)
        else:
            q, k, v, *caches = _qkv(x_all, mod[l], n1g, w_in_odd[i].astype(BF16), q_norm_g[i][None],
                                    k_norm_g[i][None], cos_tab, sin_tab, i, caches)
            wout = w_out_odd[i].astype(BF16)
            x_all = _attention(x_all, mod[l], q, k, v, wout)
            x_all = _attention(x_all, mod[l], q, k, v, wout, cache_k2, cache_v2, layer_i=i)
        wr = jnp.concatenate([w_router_expert[l], w_router_group[l],
                              jnp.zeros((D, 128 - N_EXP - MOE_GROUPS), F32)], axis=1)
        br = jnp.concatenate([b_router_expert[l], b_router_group[l],
                              jnp.zeros((128 - N_EXP - MOE_GROUPS,), F32)])[None]
        x_all = _moe(x_all, mod[l], norm2_g[l][None], wr, br, w_exp_gate, w_exp_up, w_exp_down, l,
                     final_norm_g[None] if l == DEPTH - 1 else None)

    y_prompt = x_all[0].reshape(BATCH, SEQ, D)
    y_sample = x_all[1].reshape(DEC_BATCH, DEC_SEQ, D)
    new_state = jnp.stack(gla_states, axis=1).reshape(BATCH, -1, 2, GLA_HEADS, GLA_DK, GLA_DV)
    new_k, new_v = (a.reshape(BATCH, DEPTH // 2, SEQ, ATT_KV, HD) for a in caches)
    return (y_prompt, y_sample, new_state, new_k, new_v)
```

```python
import functools

import jax
import jax.numpy as jnp
import numpy as np
from jax import lax
from jax.experimental import pallas as pl
from jax.experimental.pallas import tpu as pltpu

F32 = jnp.float32
BF16 = jnp.bfloat16

D = 1024
BATCH, SEQ = 16, 256
DEC_BATCH, DEC_SEQ = 4, 1024
N_CTX = BATCH * SEQ
N_LAT = DEC_BATCH * DEC_SEQ
N_TOK = N_CTX + N_LAT
DEPTH = 4
EPS = 1e-6
GRID_W = 64
ROPE_THETA = 10000.0

GLA_HEADS, GLA_DK, GLA_DV, GLA_RANK, GLA_CHUNK, GLA_TAU = 4, 64, 128, 16, 128, 16.0
QK_W = GLA_HEADS * GLA_DK
V_W = GLA_HEADS * GLA_DV
GMLP_GROUPS, GMLP_DIM, GMLP_CHUNK = 4, 128, 128
GMLP_W = GMLP_GROUPS * GMLP_DIM
C_Q, C_K, C_V, C_G, C_A, C_U, C_VG = 0, 256, 512, 1024, 1536, 1664, 2176
EVEN_SPLIT = 1568
EVEN_W1 = C_U
EVEN_PACK = 2688

ATT_HEADS, ATT_KV, HD = 8, 2, 128
ATT_G = ATT_HEADS // ATT_KV
Q_W = ATT_HEADS * HD
KV_W = ATT_KV * HD

MOE_GROUPS, MOE_PER_GROUP = 4, 8
N_EXP = MOE_GROUPS * MOE_PER_GROUP
D_EXP = D // 4
NEG = -1e30

MIB = 1024 * 1024
V7X_VMEM_MIB = 64
LANES, SUBLANES = 128, 8


def _cp(*sem, vmem_mib=24):
    assert vmem_mib < V7X_VMEM_MIB
    return pltpu.CompilerParams(dimension_semantics=sem, vmem_limit_bytes=vmem_mib * MIB)


def _dot(a, b):
    return jnp.dot(a.astype(BF16), b.astype(BF16), preferred_element_type=F32)


def _dot_nt(a, b):
    return lax.dot_general(a.astype(BF16), b.astype(BF16), (((1,), (1,)), ((), ())),
                           preferred_element_type=F32)


def _dot_tn(a, b):
    return lax.dot_general(a.astype(BF16), b.astype(BF16), (((0,), (0,)), ((), ())),
                           preferred_element_type=F32)


def _rms(x, g):
    return x * lax.rsqrt(jnp.mean(x * x, axis=-1, keepdims=True) + EPS) * g


def _silu(x):
    return x * jax.nn.sigmoid(x)


def _gelu(x):
    return 0.5 * x * (1.0 + jnp.tanh(np.sqrt(2.0 / np.pi).astype(np.float32) * (x + 0.044715 * (x * x * x))))


def _log_sigmoid(z):
    return jnp.minimum(z, 0.0) - jnp.log(1.0 + jnp.exp(-jnp.abs(z)))


assert D == LANES * SUBLANES


def _rows_to_tiles(ref, x):
    rows = x.shape[0]
    for j in range(SUBLANES):
        ref[pl.ds(j, rows, stride=SUBLANES), :] = x[:, j * LANES:(j + 1) * LANES]


def _tiles_to_rows(ref, rows):
    return jnp.concatenate([ref[pl.ds(j, rows, stride=SUBLANES), :] for j in range(SUBLANES)], axis=1)


def _tile_of(ref, row8):
    return ref.at[pl.ds(pl.multiple_of(row8, SUBLANES), SUBLANES), :]


def _resident(shape, index_map):
    return pl.BlockSpec(shape, index_map, pipeline_mode=pl.Buffered(1))


def _split_bf16(x):
    hi = x.astype(BF16)
    return hi, (x - hi.astype(F32)).astype(BF16)


def _mod_kernel(cond_ref, w_ref, b_ref, o_ref):
    s_hi, s_lo = _split_bf16(_silu(cond_ref[...]))
    w_hi, w_lo = _split_bf16(w_ref[...])
    dot = functools.partial(jnp.dot, preferred_element_type=F32)
    both = dot(jnp.concatenate([s_hi, s_lo], axis=0), w_hi)
    o_ref[...] = both[0:8] + both[8:16] + dot(s_hi, w_lo) + b_ref[...]


def _modulation(cond8, w_mod, b_mod):
    tn = 2048
    out = pl.pallas_call(
        _mod_kernel,
        grid=(DEPTH, 6 * D // tn),
        in_specs=[
            pl.BlockSpec((8, D), lambda l, j: (0, 0)),
            pl.BlockSpec((None, D, tn), lambda l, j: (l, 0, j)),
            pl.BlockSpec((None, 1, tn), lambda l, j: (l, 0, j)),
        ],
        out_specs=pl.BlockSpec((None, 8, tn), lambda l, j: (l, 0, j)),
        out_shape=jax.ShapeDtypeStruct((DEPTH, 8, 6 * D), F32),
        compiler_params=_cp("arbitrary", "arbitrary"),
        name="adaln_mod",
    )(cond8, w_mod, b_mod.reshape(DEPTH, 1, 6 * D))
    return out.reshape(DEPTH, 8, 6, D)


def _even_kernel(x_ref, mod_ref, n1g_ref, win1_ref, win2_ref, wgu_ref, bgu_ref, glag_ref, ws_ref, bs_ref,
                 wout_ref, s0_ref, xo_ref, st_ref, proj, la, o_f, o_b, st_scr, *, T):
    n_chunks = T // GLA_CHUNK
    shift, scale, gate = mod_ref[0:1, :], mod_ref[1:2, :], mod_ref[2:3, :]
    RB = 128
    PROJ_RB = 256

    def proj_body(r, carry):
        r0 = pl.multiple_of(r * PROJ_RB, PROJ_RB)
        h = _rms(x_ref[pl.ds(r0, PROJ_RB), :], n1g_ref[...]) * (1.0 + scale) + shift
        hb = h.astype(BF16)
        p = _dot(hb, win1_ref[...])
        proj[pl.ds(r0, PROJ_RB), 0:EVEN_W1] = p
        proj[pl.ds(r0, PROJ_RB), EVEN_W1:] = _dot(hb, win2_ref[...])
        z = _dot(p[:, C_A:C_A + 128], wgu_ref[...]) + bgu_ref[...]
        la[pl.ds(r0, PROJ_RB), :] = _log_sigmoid(z) * (1.0 / GLA_TAU)
        return carry

    lax.fori_loop(0, T // PROJ_RB, proj_body, 0)

    st_scr[0] = s0_ref[0].T
    st_scr[1] = s0_ref[1].T

    ci = lax.broadcasted_iota(jnp.int32, (GLA_CHUNK, GLA_CHUNK), 0)
    cj = lax.broadcasted_iota(jnp.int32, (GLA_CHUNK, GLA_CHUNK), 1)
    tri = (jnp.where(ci >= cj, 1.0, 0.0).astype(BF16), jnp.where(ci <= cj, 1.0, 0.0).astype(BF16))
    ai = lax.broadcasted_iota(jnp.int32, (GLA_HEADS * GLA_CHUNK, GLA_CHUNK), 0) % GLA_CHUNK
    aj = lax.broadcasted_iota(jnp.int32, (GLA_HEADS * GLA_CHUNK, GLA_CHUNK), 1)
    amask = (ai >= aj, ai <= aj)
    lane_head = lax.broadcasted_iota(jnp.int32, (1, QK_W), 1) // GLA_DK
    hmask = [jnp.where(lane_head == h, 1.0, 0.0) for h in range(GLA_HEADS)]

    def chunk_body(i, carry):
        for d in range(2):
            c = i if d == 0 else n_chunks - 1 - i
            r0 = pl.multiple_of(c * GLA_CHUNK, GLA_CHUNK)
            q = proj[pl.ds(r0, GLA_CHUNK), C_Q:C_Q + QK_W] * (GLA_DK ** -0.5)
            k = proj[pl.ds(r0, GLA_CHUNK), C_K:C_K + QK_W]
            v = proj[pl.ds(r0, GLA_CHUNK), C_V:C_V + V_W]
            lac = la[pl.ds(r0, GLA_CHUNK), d * QK_W:(d + 1) * QK_W]
            hi = lac.astype(BF16)
            lo = (lac - hi.astype(F32)).astype(BF16)
            b = (jnp.dot(tri[d], hi, preferred_element_type=F32)
                 + jnp.dot(tri[d], lo, preferred_element_type=F32))
            bend = b[GLA_CHUNK - 1:GLA_CHUNK, :] if d == 0 else b[0:1, :]
            qe = q * jnp.exp(b)
            ke = k * jnp.exp(-b)
            kd = k * jnp.exp(bend - b)
            st = st_scr[d]
            qstack = jnp.concatenate([qe * hmask[h] for h in range(GLA_HEADS)], axis=0).astype(BF16)
            att = jnp.where(amask[d], _dot_nt(qstack, ke), 0.0)
            inter = _dot_nt(qstack, st)
            outs = []
            for h in range(GLA_HEADS):
                rows = slice(h * GLA_CHUNK, (h + 1) * GLA_CHUNK)
                outs.append(_dot(att[rows], v[:, h * GLA_DV:(h + 1) * GLA_DV]) + inter[rows])
            o = jnp.concatenate(outs, axis=1)
            if d == 0:
                o_f[pl.ds(r0, GLA_CHUNK), :] = o
            else:
                o_b[pl.ds(r0, GLA_CHUNK), :] = o
            vstack = jnp.concatenate([v[:, h * GLA_DV:(h + 1) * GLA_DV] for h in range(GLA_HEADS)], axis=0)
            kstack = jnp.concatenate([kd * hmask[h] for h in range(GLA_HEADS)], axis=0)
            st_scr[d] = st * jnp.exp(bend) + _dot_tn(vstack, kstack)
        return carry

    lax.fori_loop(0, n_chunks, chunk_body, 0, unroll=2)
    st_ref[0] = st_scr[0].T
    st_ref[1] = st_scr[1].T

    def out_body(r, carry):
        r0 = pl.multiple_of(r * RB, RB)
        osum = o_f[pl.ds(r0, RB), :] + o_b[pl.ds(r0, RB), :]
        g = proj[pl.ds(r0, RB), C_G:C_G + V_W]
        u = proj[pl.ds(r0, RB), C_U:C_U + GMLP_W]
        vg = _gelu(proj[pl.ds(r0, RB), C_VG:C_VG + GMLP_W])
        parts = []
        for h in range(GLA_HEADS):
            oh = osum[:, h * GLA_DV:(h + 1) * GLA_DV]
            parts.append(_rms(oh, glag_ref[...]) * _silu(g[:, h * GLA_DV:(h + 1) * GLA_DV]))
        for gi in range(GMLP_GROUPS):
            vc = vg[:, gi * GMLP_DIM:(gi + 1) * GMLP_DIM]
            vc = vc - jnp.mean(vc, axis=-1, keepdims=True)
            vn = vc * lax.rsqrt(jnp.mean(vc * vc, axis=-1, keepdims=True) + EPS)
            sg = _dot(ws_ref[gi], vn) + bs_ref[:, gi:gi + 1]
            parts.append(_gelu(u[:, gi * GMLP_DIM:(gi + 1) * GMLP_DIM]) * sg)
        mix = jnp.concatenate(parts, axis=1)
        y = _dot(mix, wout_ref[...])
        xo_ref[pl.ds(r0, RB), :] = x_ref[pl.ds(r0, RB), :] + gate * y
        return carry

    lax.fori_loop(0, T // RB, out_body, 0)


def _even_mixer(x_all, mod_l, n1g, win1, win2, wgu, bgu, glag, ws, bs, wout, s0, *, latent, x_first=None):
    if latent:
        T, nseq, blk0 = DEC_SEQ, DEC_BATCH, N_CTX // DEC_SEQ
        cond = lambda i: 1 + i
        s0_spec = pl.BlockSpec((None, 2, QK_W, GLA_DV), lambda i: (i, 0, 0, 0))
    else:
        T, nseq, blk0 = SEQ, BATCH, 0
        cond = lambda i: 0
        s0_spec = pl.BlockSpec((None, 2, QK_W, GLA_DV), lambda i: (0, 0, 0, 0))
    const2 = lambda i: (0, 0)
    body = functools.partial(_even_kernel, T=T)
    x_spec = pl.BlockSpec((T, D), lambda i: (blk0 + i, 0))
    if x_first is None:
        lead_specs, lead_args, aliases = [x_spec], (x_all,), {0: 0}
    elif x_all is None:
        lead_specs, lead_args, aliases = [pl.BlockSpec((T, D), lambda i: (i, 0))], (x_first,), {}
    else:
        lead_specs = [pl.BlockSpec(memory_space=pl.ANY), pl.BlockSpec((T, D), lambda i: (i, 0))]
        lead_args, aliases = (x_all, x_first), {0: 0}
        body = lambda dst_ref, *refs: _even_kernel(*refs, T=T)
    x_new, states = pl.pallas_call(
        body,
        grid=(nseq,),
        in_specs=lead_specs + [
            pl.BlockSpec((None, 6, D), lambda i: (cond(i), 0, 0)),
            _resident((1, D), const2),
            _resident((D, EVEN_W1), const2),
            _resident((D, EVEN_PACK - EVEN_W1), const2),
            _resident((128, 2 * QK_W), const2),
            _resident((1, 2 * QK_W), const2),
            _resident((1, GLA_DV), const2),
            _resident((GMLP_GROUPS, GMLP_CHUNK, GMLP_CHUNK), lambda i: (0, 0, 0)),
            _resident((GMLP_CHUNK, GMLP_GROUPS), const2),
            _resident((D, D), const2),
            s0_spec,
        ],
        out_specs=[
            x_spec,
            pl.BlockSpec((None, 2, QK_W, GLA_DV), lambda i: (i, 0, 0, 0)),
        ],
        out_shape=[
            jax.ShapeDtypeStruct((N_TOK, D), F32),
            jax.ShapeDtypeStruct((nseq, 2, QK_W, GLA_DV), F32),
        ],
        scratch_shapes=[
            pltpu.VMEM((T, EVEN_PACK), F32),
            pltpu.VMEM((T, 2 * QK_W), F32),
            pltpu.VMEM((T, V_W), F32),
            pltpu.VMEM((T, V_W), F32),
            pltpu.VMEM((2, GLA_DV, QK_W), F32),
        ],
        input_output_aliases=aliases,
        compiler_params=_cp("arbitrary", vmem_mib=48 if latent else 24),
        name="even_mixer_latent" if latent else "even_mixer_context",
    )(*lead_args, mod_l, n1g, win1, win2, wgu, bgu, glag, ws, bs, wout, s0)
    return x_new, states


QKV_TB = 512


def _qkv_kernel(x_ref, mod_ref, n1g_ref, win_ref, gq_ref, gk_ref, cos_ref, sin_ref, q_ref, k_ref, v_ref,
                ck_ref, cv_ref):
    shift, scale = mod_ref[0:1, :], mod_ref[1:2, :]
    h = _rms(x_ref[...], n1g_ref[...]) * (1.0 + scale) + shift
    p = _dot(h, win_ref[...])
    cos, sin = cos_ref[...], sin_ref[...]
    even_lane = lax.broadcasted_iota(jnp.int32, (1, HD), 1) % 2 == 0

    def rope(xn):
        swapped = jnp.where(even_lane, pltpu.roll(xn, HD - 1, axis=1), pltpu.roll(xn, 1, axis=1))
        return xn * cos + swapped * sin

    def emit(rotate, to_cache):
        for hh in range(ATT_HEADS):
            qn = _rms(p[:, hh * HD:(hh + 1) * HD], gq_ref[...])
            q_ref[:, hh * HD:(hh + 1) * HD] = (rotate(qn) * (HD ** -0.5)).astype(BF16)
        for hh in range(ATT_KV):
            kn = rotate(_rms(p[:, Q_W + hh * HD:Q_W + (hh + 1) * HD], gk_ref[...]))
            k_ref[:, hh * HD:(hh + 1) * HD] = kn
            if to_cache:
                for s in range(QKV_TB // SEQ):
                    ck_ref[s, pl.ds(hh, SEQ, stride=ATT_KV), :] = kn[s * SEQ:(s + 1) * SEQ]
                    cv_ref[s, pl.ds(hh, SEQ, stride=ATT_KV), :] = p[s * SEQ:(s + 1) * SEQ,
                                                                    Q_W + KV_W + hh * HD:Q_W + KV_W + (hh + 1) * HD]

    is_latent = pl.program_id(0) >= N_CTX // QKV_TB

    @pl.when(is_latent)
    def _():
        emit(rope, False)

    @pl.when(jnp.logical_not(is_latent))
    def _():
        emit(lambda xn: xn, True)

    v_ref[...] = p[:, Q_W + KV_W:]


def _qkv(x_all, mod_l, n1g, win, gq, gk, cos_tab, sin_tab, layer_i, caches=None):
    nb_ctx = N_CTX // QKV_TB
    per_seq = DEC_SEQ // QKV_TB
    cond = lambda i: jnp.where(i < nb_ctx, 0, 1 + (i - nb_ctx) // per_seq)
    tab = lambda i: jnp.where(i < nb_ctx, 0, 1 + (i - nb_ctx) % per_seq)
    const2 = lambda i: (0, 0)
    cache_spec = pl.BlockSpec((QKV_TB // SEQ, None, SEQ * ATT_KV, HD),
                              lambda i: (jnp.minimum(i, nb_ctx - 1), layer_i, 0, 0))
    cache_shape = jax.ShapeDtypeStruct((BATCH, DEPTH // 2, SEQ * ATT_KV, HD), F32)
    if caches is None:
        body, lead_specs, lead_args, aliases = _qkv_kernel, [], (), {}
    else:
        body = lambda ck_in, cv_in, *refs: _qkv_kernel(*refs)
        lead_specs = [pl.BlockSpec(memory_space=pl.ANY)] * 2
        lead_args, aliases = tuple(caches), {0: 3, 1: 4}
    return pl.pallas_call(
        body,
        grid=(N_TOK // QKV_TB,),
        in_specs=lead_specs + [
            pl.BlockSpec((QKV_TB, D), lambda i: (i, 0)),
            pl.BlockSpec((None, 6, D), lambda i: (cond(i), 0, 0)),
            _resident((1, D), const2),
            _resident((D, Q_W + 2 * KV_W), const2),
            _resident((1, HD), const2),
            _resident((1, HD), const2),
            pl.BlockSpec((None, QKV_TB, HD), lambda i: (tab(i), 0, 0)),
            pl.BlockSpec((None, QKV_TB, HD), lambda i: (tab(i), 0, 0)),
        ],
        out_specs=[
            pl.BlockSpec((QKV_TB, Q_W), lambda i: (i, 0)),
            pl.BlockSpec((QKV_TB, KV_W), lambda i: (i, 0)),
            pl.BlockSpec((QKV_TB, KV_W), lambda i: (i, 0)),
            cache_spec,
            cache_spec,
        ],
        out_shape=[
            jax.ShapeDtypeStruct((N_TOK, Q_W), BF16),
            jax.ShapeDtypeStruct((N_TOK, KV_W), F32),
            jax.ShapeDtypeStruct((N_TOK, KV_W), F32),
            cache_shape,
            cache_shape,
        ],
        input_output_aliases=aliases,
        compiler_params=_cp("arbitrary"),
        name="odd_qkv",
    )(*lead_args, x_all, mod_l, n1g, win, gq, gk, cos_tab, sin_tab)


ATT_TQ_LATENT = 512


def _attn_kernel(*refs, n_kv):
    q_ref = refs[0]
    kv_refs = refs[1:1 + 2 * n_kv]
    x_ref, mod_ref, wout_ref, xo_ref, att_scr = refs[1 + 2 * n_kv:]
    gate = mod_ref[2:3, :]
    def head(ref, kh):
        if ref.shape[1] == HD:
            return ref[pl.ds(kh, ref.shape[0] // ATT_KV, stride=ATT_KV), :].astype(BF16)
        return ref[:, kh * HD:(kh + 1) * HD].astype(BF16)

    for kh in range(ATT_KV):
        ks = [head(kv_refs[2 * s], kh) for s in range(n_kv)]
        vs = [jnp.concatenate([vh, jnp.ones_like(vh)], axis=1)
              for vh in (head(kv_refs[2 * s + 1], kh) for s in range(n_kv))]
        for g in range(ATT_G):
            hh = kh * ATT_G + g
            qh = q_ref[:, hh * HD:(hh + 1) * HD]
            ss = [_dot_nt(qh, kk) for kk in ks]
            m = ss[0].max(axis=-1, keepdims=True)
            for s in ss[1:]:
                m = jnp.maximum(m, s.max(axis=-1, keepdims=True))
            o = _dot(jnp.exp(ss[0] - m), vs[0])
            for s, vv in zip(ss[1:], vs[1:]):
                o = o + _dot(jnp.exp(s - m), vv)
            att_scr[:, hh * HD:(hh + 1) * HD] = o[:, :HD] / o[:, HD:HD + 1]
    y = _dot(att_scr[...], wout_ref[...])
    xo_ref[...] = x_ref[...] + gate * y


def _attention(x_all, mod_l, q, k, v, wout, cache_k=None, cache_v=None, layer_i=0):
    latent = cache_k is not None
    const2 = lambda *a: (0, 0)
    tq = ATT_TQ_LATENT if latent else SEQ
    if latent:
        nq = DEC_SEQ // tq
        row_blk = lambda b, j: (N_CTX // tq + b * nq + j, 0)
        grid = (DEC_BATCH, nq)
        kv_specs = [
            pl.BlockSpec((None, None, SEQ * ATT_KV, HD), lambda b, j: (b, layer_i, 0, 0)),
            pl.BlockSpec((None, None, SEQ * ATT_KV, HD), lambda b, j: (b, layer_i, 0, 0)),
            pl.BlockSpec((DEC_SEQ, KV_W), lambda b, j: (N_CTX // DEC_SEQ + b, 0)),
            pl.BlockSpec((DEC_SEQ, KV_W), lambda b, j: (N_CTX // DEC_SEQ + b, 0)),
        ]
        kv_args = (cache_k, cache_v, k, v)
        mod_spec = pl.BlockSpec((None, 6, D), lambda b, j: (1 + b, 0, 0))
        sem = ("arbitrary", "arbitrary")
        n_kv = 2
    else:
        row_blk = lambda i: (i, 0)
        grid = (BATCH,)
        kv_specs = [pl.BlockSpec((SEQ, KV_W), row_blk), pl.BlockSpec((SEQ, KV_W), row_blk)]
        kv_args = (k, v)
        mod_spec = pl.BlockSpec((None, 6, D), lambda i: (0, 0, 0))
        sem = ("arbitrary",)
        n_kv = 1
    n_in = 1 + len(kv_args)
    return pl.pallas_call(
        functools.partial(_attn_kernel, n_kv=n_kv),
        grid=grid,
        in_specs=[pl.BlockSpec((tq, Q_W), row_blk)] + kv_specs + [
            pl.BlockSpec((tq, D), row_blk),
            mod_spec,
            _resident((D, D), const2),
        ],
        out_specs=pl.BlockSpec((tq, D), row_blk),
        out_shape=jax.ShapeDtypeStruct((N_TOK, D), F32),
        scratch_shapes=[pltpu.VMEM((tq, Q_W), F32)],
        input_output_aliases={n_in: 0},
        compiler_params=_cp(*sem),
        name="attention_latent" if latent else "attention_context",
    )(q, *kv_args, x_all, mod_l, wout)


ROUTE_TB = 512
HALF_TOK = N_TOK // 2
M_E1, M_E2, M_G1, M_G2, M_R1, M_R2 = 0, 1, 2, 3, 4, 5


def _router_kernel(x_ref, mod_ref, n2g_ref, w2_ref, br_ref, h_ref, metat_ref, cnt_ref, run):
    @pl.when(pl.program_id(0) % (HALF_TOK // ROUTE_TB) == 0)
    def _():
        run[...] = jnp.zeros_like(run)

    shift, scale = mod_ref[3:4, :], mod_ref[4:5, :]
    h = _rms(x_ref[...], n2g_ref[...]) * (1.0 + scale) + shift
    _rows_to_tiles(h_ref, h)
    h_hi, h_lo = _split_bf16(h)
    dot = functools.partial(jnp.dot, preferred_element_type=F32)
    wide = dot(h_hi, w2_ref[...])
    logits = wide[:, :128] + wide[:, 128:] + dot(h_lo, w2_ref[:, :128]) + br_ref[...]
    lane = lax.broadcasted_iota(jnp.int32, logits.shape, 1).astype(F32)
    big = 1e4

    def first_argmax(vals):
        m = vals.max(axis=-1, keepdims=True)
        return m, jnp.where(vals == m, lane, big).min(axis=-1, keepdims=True)

    gl = jnp.where((lane >= N_EXP) & (lane < N_EXP + MOE_GROUPS), logits, NEG)
    gmax, glane = first_argmax(gl)
    g_p = 1.0 / jnp.exp(gl - gmax).sum(axis=-1, keepdims=True)
    lo = (glane - N_EXP) * MOE_PER_GROUP
    el = jnp.where((lane >= lo) & (lane < lo + MOE_PER_GROUP), logits, NEG)
    m1, i1 = first_argmax(el)
    m2, i2 = first_argmax(jnp.where(lane == i1, NEG, el))
    t = jnp.exp(m2 - m1)
    w1 = 1.0 / (1.0 + t)
    sel1, sel2 = lane == i1, lane == i2
    onehot = jnp.where(sel1 | sel2, 1.0, 0.0)
    ri = lax.broadcasted_iota(jnp.int32, (ROUTE_TB, ROUTE_TB), 0)
    rj = lax.broadcasted_iota(jnp.int32, (ROUTE_TB, ROUTE_TB), 1)
    before = _dot(jnp.where(ri > rj, 1.0, 0.0), onehot) + run[...]
    r1 = jnp.where(sel1, before, 0.0).sum(axis=-1, keepdims=True)
    r2 = jnp.where(sel2, before, 0.0).sum(axis=-1, keepdims=True)
    run[...] += onehot.sum(axis=0, keepdims=True)
    cnt_ref[...] = run[...]
    meta = jnp.zeros_like(logits)
    for j, val in enumerate([i1, i2, w1 * g_p, (t * w1) * g_p, r1, r2]):
        meta = jnp.where(lane == j, val, meta)
    metat_ref[...] = meta.T[0:8, :]


def _router(x_all, mod_l, n2g, wr, br):
    w2 = jnp.concatenate(_split_bf16(wr), axis=1)
    nb_ctx = N_CTX // ROUTE_TB
    per_seq = DEC_SEQ // ROUTE_TB
    cond = lambda i: jnp.where(i < nb_ctx, 0, 1 + (i - nb_ctx) // per_seq)
    const2 = lambda i: (0, 0)
    return pl.pallas_call(
        _router_kernel,
        grid=(N_TOK // ROUTE_TB,),
        in_specs=[
            pl.BlockSpec((ROUTE_TB, D), lambda i: (i, 0)),
            pl.BlockSpec((None, 6, D), lambda i: (cond(i), 0, 0)),
            _resident((1, D), const2),
            _resident((D, 256), const2),
            _resident((1, 128), const2),
        ],
        out_specs=[
            pl.BlockSpec((ROUTE_TB * 8, 128), lambda i: (i, 0)),
            pl.BlockSpec((8, ROUTE_TB), lambda i: (0, i)),
            pl.BlockSpec((None, 1, 128), lambda i: (i // (HALF_TOK // ROUTE_TB), 0, 0)),
        ],
        out_shape=[
            jax.ShapeDtypeStruct((N_TOK * 8, 128), F32),
            jax.ShapeDtypeStruct((8, N_TOK), F32),
            jax.ShapeDtypeStruct((2, 1, 128), F32),
        ],
        scratch_shapes=[pltpu.VMEM((1, 128), F32)],
        compiler_params=_cp("arbitrary"),
        name="moe_router",
    )(x_all, mod_l, n2g, w2, br)


EXP_TM = 128
N_ASSIGN = 2 * N_TOK
N_GROUPS = 2 * N_EXP
MAX_TILES = N_ASSIGN // EXP_TM + N_GROUPS
N_SORTED = MAX_TILES * EXP_TM
ORDER_BLK = 4096
CODE_PLANE = 2 * HALF_TOK
CODE_MASK = 8 * CODE_PLANE - 1
DUMMY8 = HALF_TOK * 8


def _order_kernel(pos1_ref, pos2_ref, pad_lo_ref, pad_hi_ref, src_ref):
    i = pl.program_id(0)
    local = (i % (HALF_TOK // ORDER_BLK)) * ORDER_BLK

    def body(t, carry):
        src_ref[pos1_ref[t]] = (local + t) * 8
        src_ref[pos2_ref[t]] = (local + t + CODE_PLANE) * 8
        return carry

    lax.fori_loop(0, ORDER_BLK, body, 0, unroll=16)

    @pl.when(i == 0)
    def _():
        def group(g, carry):
            def pad(p, c):
                src_ref[p] = DUMMY8
                return c
            return lax.fori_loop(pad_lo_ref[g], pad_hi_ref[g], pad, carry)

        lax.fori_loop(0, N_GROUPS, group, 0)


def _order(pos, pad_lo, pad_hi):
    return pl.pallas_call(
        _order_kernel,
        grid=(N_TOK // ORDER_BLK,),
        in_specs=[
            pl.BlockSpec((ORDER_BLK,), lambda i: (i,), memory_space=pltpu.SMEM),
            pl.BlockSpec((ORDER_BLK,), lambda i: (N_TOK // ORDER_BLK + i,), memory_space=pltpu.SMEM),
            pl.BlockSpec(memory_space=pltpu.SMEM),
            pl.BlockSpec(memory_space=pltpu.SMEM),
        ],
        out_specs=pl.BlockSpec(memory_space=pltpu.SMEM),
        out_shape=jax.ShapeDtypeStruct((N_SORTED,), jnp.int32),
        compiler_params=_cp("arbitrary"),
        name="moe_order",
    )(pos, pos, pad_lo, pad_hi)


GATE_BLK = CODE_PLANE + HALF_TOK
ACC_TOK = HALF_TOK + 64
GATHER_GROUP, ACC_GROUP = 16, 8
TAIL_ROWS = 2 * EXP_TM + EXP_TM // 2


RES_TB = 256


EXP_PER_STEP = 2


def _experts_kernel(tile0_ref, ntile_ref, count_ref, src_ref, gs_ref, h_hbm, x_hbm, mod_ref, *rest, final):
    weights, rest = rest[:3 * EXP_PER_STEP], rest[3 * EXP_PER_STEP:]
    if final:
        fg_ref, *dst_hbm = rest[:3]
        rest = rest[3:]
    else:
        dst_hbm, rest = rest[:1], rest[1:]
    h_res, acc, xbuf, ybuf, wgb, wub, wdb, xin, xout, sem, in_sem, out_sem = rest
    first_group = pl.program_id(0) * EXP_PER_STEP
    half = first_group // N_EXP
    rows0 = pl.multiple_of(half * (HALF_TOK * 8), 8)

    @pl.when(first_group % N_EXP == 0)
    def _():
        cp = pltpu.make_async_copy(h_hbm.at[pl.ds(rows0, HALF_TOK * 8), :], h_res.at[pl.ds(0, HALF_TOK * 8), :], sem)
        cp.start()
        h_res[pl.ds(DUMMY8, 8), :] = jnp.zeros((8, 128), F32)
        xbuf[...] = jnp.zeros_like(xbuf)

        def zero(i, carry):
            acc[pl.ds(pl.multiple_of(i * 512, 512), 512), :] = jnp.zeros((512, 128), F32)
            return carry

        lax.fori_loop(0, ACC_TOK * 8 // 512, zero, 0)
        cp.wait()

    def run_group(group, wg_ref, wu_ref, wd_ref):
        n_tiles = ntile_ref[group]

        @pl.when(n_tiles > 0)
        def _():
            wgb[...] = wg_ref[...].astype(BF16)
            wub[...] = wu_ref[...].astype(BF16)
            wdb[...] = wd_ref[...].astype(BF16)

        row0 = tile0_ref[group] * EXP_TM
        row_end = row0 + count_ref[group]

        def process(base, rows):
            live = (jnp.clip(row_end - base, 0, rows) + GATHER_GROUP - 1) // GATHER_GROUP

            def gather(g, c):
                for i in range(GATHER_GROUP):
                    r = g * GATHER_GROUP + i
                    xbuf[pl.ds(pl.multiple_of(r * 8, 8), 8), :] = _tile_of(h_res, src_ref[base + r] & CODE_MASK)[...]
                return c

            lax.fori_loop(0, live, gather, 0)
            x = _tiles_to_rows(xbuf, rows).astype(BF16)
            hid = _silu(_dot(x, wgb[...])) * _dot(x, wub[...])
            _rows_to_tiles(ybuf, _dot(hid, wdb[...]))

            def accumulate(g, c):
                targets, values = [], []
                for i in range(ACC_GROUP):
                    r = g * ACC_GROUP + i
                    code = src_ref[base + r]
                    target = _tile_of(acc, code & CODE_MASK)
                    targets.append(target)
                    values.append(target[...] + gs_ref[code >> 3] * ybuf[pl.ds(pl.multiple_of(r * 8, 8), 8), :])
                for target, value in zip(targets, values):
                    target[...] = value
                return c

            lax.fori_loop(0, live * (GATHER_GROUP // ACC_GROUP), accumulate, 0)

        ends_in_triple = (n_tiles % 2 == 1) & (n_tiles >= 3)
        n_pairs = jnp.where(ends_in_triple, (n_tiles - 3) // 2, n_tiles // 2)

        def pair_body(j, carry):
            process(row0 + j * (2 * EXP_TM), 2 * EXP_TM)
            return carry

        lax.fori_loop(0, n_pairs, pair_body, 0)

        @pl.when(ends_in_triple)
        def _():
            base = row0 + (n_tiles - 3) * EXP_TM
            short = row_end - base <= TAIL_ROWS

            @pl.when(short)
            def _():
                process(base, TAIL_ROWS)

            @pl.when(jnp.logical_not(short))
            def _():
                process(base, 3 * EXP_TM)

        @pl.when(n_tiles == 1)
        def _():
            process(row0, EXP_TM)

    for sub in range(EXP_PER_STEP):
        run_group(first_group + sub, *weights[3 * sub:3 * sub + 3])
    expert = (first_group + EXP_PER_STEP - 1) % N_EXP

    def rows_of(first, blk):
        return pl.ds(pl.multiple_of(first + blk * RES_TB, RES_TB), RES_TB)

    def load_x(blk, slot):
        return pltpu.make_async_copy(x_hbm.at[rows_of(half * HALF_TOK, blk), :], xin.at[slot], in_sem.at[slot])

    def residual(blk, slot):
        cond = jnp.where(half == 0, 0, 1 + blk // (DEC_SEQ // RES_TB))
        gate = mod_ref[cond, 5:6, :]
        y = _tiles_to_rows(acc.at[pl.ds(pl.multiple_of(blk * (RES_TB * 8), RES_TB * 8), RES_TB * 8), :], RES_TB)
        x_new = xin[slot] + gate * y
        xout[slot] = _rms(x_new, fg_ref[...]) if final else x_new

    def epilogue(dst, first_row):
        def store_x(blk, slot):
            return pltpu.make_async_copy(xout.at[slot], dst.at[rows_of(first_row, blk), :], out_sem.at[slot])

        n_pairs = HALF_TOK // RES_TB // 2
        load_x(0, 0).start()

        def pair(p, carry):
            for slot in range(2):
                blk = 2 * p + slot
                if slot == 0:
                    load_x(blk + 1, 1).start()
                else:
                    @pl.when(p + 1 < n_pairs)
                    def _():
                        load_x(blk + 1, 0).start()
                load_x(blk, slot).wait()

                @pl.when(p > 0)
                def _():
                    store_x(blk - 2, slot).wait()

                residual(blk, slot)
                store_x(blk, slot).start()
            return carry

        lax.fori_loop(0, n_pairs, pair, 0)
        store_x(2 * n_pairs - 2, 0).wait()
        store_x(2 * n_pairs - 1, 1).wait()

    if final:
        for which in range(2):
            @pl.when((expert == N_EXP - 1) & (half == which))
            def _():
                epilogue(dst_hbm[which], 0)
    else:
        @pl.when(expert == N_EXP - 1)
        def _():
            epilogue(dst_hbm[0], half * HALF_TOK)


def _experts(tile0, n_tiles, counts, src, gs, h, x_all, mod_l, wg, wu, wd, layer, final_g=None):
    final = final_g is not None
    any_spec = pl.BlockSpec(memory_space=pl.ANY)
    extra_specs = [pl.BlockSpec((1, D), lambda g, t0, nt, cnt, src: (0, 0))] if final else []
    extra_args = (final_g,) if final else ()
    weight_specs, weight_args = [], []
    for sub in range(EXP_PER_STEP):
        wmap = lambda g, t0, nt, cnt, src, sub=sub: (layer, (g * EXP_PER_STEP + sub) % N_EXP, 0, 0)
        weight_specs += [pl.BlockSpec((None, None, D, D_EXP), wmap), pl.BlockSpec((None, None, D, D_EXP), wmap),
                         pl.BlockSpec((None, None, D_EXP, D), wmap)]
        weight_args += [wg, wu, wd]
    return pl.pallas_call(
        functools.partial(_experts_kernel, final=final),
        grid_spec=pltpu.PrefetchScalarGridSpec(
            num_scalar_prefetch=4,
            grid=(N_GROUPS // EXP_PER_STEP,),
            in_specs=[
                pl.BlockSpec((GATE_BLK,), lambda g, t0, nt, cnt, src: (g * EXP_PER_STEP // N_EXP,),
                             memory_space=pltpu.SMEM),
                pl.BlockSpec(memory_space=pl.ANY),
                pl.BlockSpec(memory_space=pl.ANY),
                pl.BlockSpec((8, 6, D), lambda g, t0, nt, cnt, src: (0, 0, 0)),
            ] + weight_specs + extra_specs,
            out_specs=[any_spec, any_spec] if final else any_spec,
            scratch_shapes=[
                pltpu.VMEM((ACC_TOK * 8, 128), F32),
                pltpu.VMEM((ACC_TOK * 8, 128), F32),
                pltpu.VMEM((3 * EXP_TM * 8, 128), F32),
                pltpu.VMEM((3 * EXP_TM * 8, 128), F32),
                pltpu.VMEM((D, D_EXP), BF16),
                pltpu.VMEM((D, D_EXP), BF16),
                pltpu.VMEM((D_EXP, D), BF16),
                pltpu.VMEM((2, RES_TB, D), F32),
                pltpu.VMEM((2, RES_TB, D), F32),
                pltpu.SemaphoreType.DMA,
                pltpu.SemaphoreType.DMA((2,)),
                pltpu.SemaphoreType.DMA((2,)),
            ],
        ),
        out_shape=([jax.ShapeDtypeStruct((HALF_TOK, D), F32)] * 2 if final
                   else jax.ShapeDtypeStruct((N_TOK, D), F32)),
        input_output_aliases={} if final else {6: 0},
        compiler_params=_cp("arbitrary", vmem_mib=56),
        name="moe_experts_final" if final else "moe_experts",
    )(tile0, n_tiles, counts, src, gs, h, x_all, mod_l, *weight_args, *extra_args)


def _moe(x_all, mod_l, n2g, wr, br, wg, wu, wd, layer, final_g=None):
    h, metat, cnt = _router(x_all, mod_l, n2g, wr, br)
    counts = cnt[:, 0, :N_EXP].astype(jnp.int32).reshape(N_GROUPS)
    padded = (counts + EXP_TM - 1) // EXP_TM * EXP_TM
    ends = jnp.cumsum(padded)
    offs = ends - padded
    rec = metat.astype(jnp.int32)
    half = (jnp.arange(N_TOK, dtype=jnp.int32) // HALF_TOK)[None, :]
    group = rec[M_E1:M_E2 + 1] + N_EXP * half
    is_group = group[None] == jnp.arange(N_GROUPS, dtype=jnp.int32)[:, None, None]
    pos = jnp.sum(jnp.where(is_group, offs[:, None, None], 0), axis=0) + rec[M_R1:M_R2 + 1]
    live_end = offs + (counts + GATHER_GROUP - 1) // GATHER_GROUP * GATHER_GROUP
    src = _order(pos.reshape(N_ASSIGN), offs + counts, live_end)
    g12 = metat[M_G1:M_G2 + 1].reshape(2, 2, HALF_TOK)
    gates = jnp.concatenate([g12[0], jnp.zeros((2, CODE_PLANE - HALF_TOK), F32), g12[1]], axis=1)
    return _experts(offs // EXP_TM, padded // EXP_TM, counts, src, gates.reshape(2 * GATE_BLK), h, x_all, mod_l,
                    wg, wu, wd, layer, final_g)


def _rope_tables():
    pos = jnp.arange(DEC_SEQ)
    row = (pos // GRID_W).astype(F32)
    col = (pos % GRID_W).astype(F32)
    n_freq = HD // 4
    inv = ROPE_THETA ** (-jnp.arange(n_freq, dtype=F32) / n_freq)
    ang = jnp.concatenate([row[:, None] * inv, col[:, None] * inv], axis=-1)
    cos = jnp.repeat(jnp.cos(ang), 2, axis=-1)
    sin = jnp.repeat(jnp.sin(ang), 2, axis=-1) * jnp.tile(jnp.array([-1.0, 1.0], F32), HD // 2)
    nblk = DEC_SEQ // QKV_TB
    cos_tab = jnp.concatenate([jnp.ones((1, QKV_TB, HD), F32), cos.reshape(nblk, QKV_TB, HD)], axis=0)
    sin_tab = jnp.concatenate([jnp.zeros((1, QKV_TB, HD), F32), sin.reshape(nblk, QKV_TB, HD)], axis=0)
    return cos_tab, sin_tab


def kernel(x_prompt, x_sample, state_gla, cache_k, cache_v, c, c_ctx, w_mod, b_mod, norm1_g, norm2_g,
           w_in_even, w_gate_up, b_gate_up, gla_norm_g, w_spatial, b_spatial, w_out_even,
           w_in_odd, q_norm_g, k_norm_g, w_out_odd, w_router_group, b_router_group,
           w_router_expert, b_router_expert, w_exp_gate, w_exp_up, w_exp_down, final_norm_g):
    x_all = None
    cond8 = jnp.concatenate([c_ctx[None], c, jnp.zeros((3, D), F32)], axis=0)
    mod = _modulation(cond8, w_mod, b_mod)
    cos_tab, sin_tab = _rope_tables()
    zero_state = jnp.zeros((1, 2, QK_W, GLA_DV), F32)
    state_in = state_gla.reshape(DEC_BATCH, -1, 2, QK_W, GLA_DV)
    cache_k2 = cache_k.reshape(DEC_BATCH, -1, SEQ * ATT_KV, HD)
    cache_v2 = cache_v.reshape(DEC_BATCH, -1, SEQ * ATT_KV, HD)

    gla_states, caches = [], None
    for l in range(DEPTH):
        i = l // 2
        n1g = norm1_g[l][None]
        if l % 2 == 0:
            w = w_in_even[i]
            win1, win2 = w[:, :EVEN_W1].astype(BF16), w[:, EVEN_SPLIT:].astype(BF16)
            wgu = jnp.zeros((128, 2 * QK_W), F32)
            wgu = wgu.at[0:GLA_RANK, 0:QK_W].set(w_gate_up[i, 0])
            wgu = wgu.at[GLA_RANK:2 * GLA_RANK, QK_W:].set(w_gate_up[i, 1]).astype(BF16)
            bgu = b_gate_up[i].reshape(1, 2 * QK_W)
            args = (mod[l], n1g, win1, win2, wgu, bgu, gla_norm_g[i][None], w_spatial[i].astype(BF16),
                    b_spatial[i].T, w_out_even[i].astype(BF16))
            first = l == 0
            x_all, st = _even_mixer(x_all, *args, zero_state, latent=False,
                                    x_first=x_prompt.reshape(N_CTX, D) if first else None)
            gla_states.append(st)
            x_all, _ = _even_mixer(x_all, *args, state_in[:, i], latent=True,
                                   x_first=x_sample.reshape(N_LAT, D) if first else None)
        else:
            q, k, v, *caches = _qkv(x_all, mod[l], n1g, w_in_odd[i].astype(BF16), q_norm_g[i][None],
                                    k_norm_g[i][None], cos_tab, sin_tab, i, caches)
            wout = w_out_odd[i].astype(BF16)
            x_all = _attention(x_all, mod[l], q, k, v, wout)
            x_all = _attention(x_all, mod[l], q, k, v, wout, cache_k2, cache_v2, layer_i=i)
        wr = jnp.concatenate([w_router_expert[l], w_router_group[l],
                              jnp.zeros((D, 128 - N_EXP - MOE_GROUPS), F32)], axis=1)
        br = jnp.concatenate([b_router_expert[l], b_router_group[l],
                              jnp.zeros((128 - N_EXP - MOE_GROUPS,), F32)])[None]
        x_all = _moe(x_all, mod[l], norm2_g[l][None], wr, br, w_exp_gate, w_exp_up, w_exp_down, l,
                     final_norm_g[None] if l == DEPTH - 1 else None)

    y_prompt = x_all[0].reshape(BATCH, SEQ, D)
    y_sample = x_all[1].reshape(DEC_BATCH, DEC_SEQ, D)
    new_state = jnp.stack(gla_states, axis=1).reshape(BATCH, -1, 2, GLA_HEADS, GLA_DK, GLA_DV)
    new_k, new_v = (a.reshape(BATCH, DEPTH // 2, SEQ, ATT_KV, HD) for a in caches)
    return (y_prompt, y_sample, new_state, new_k, new_v)
```

```python
import functools

import jax
import jax.numpy as jnp
import numpy as np
from jax import lax
from jax.experimental import pallas as pl
from jax.experimental.pallas import tpu as pltpu

F32 = jnp.float32
BF16 = jnp.bfloat16

D = 1024
BATCH, SEQ = 16, 256
DEC_BATCH, DEC_SEQ = 4, 1024
N_CTX = BATCH * SEQ
N_LAT = DEC_BATCH * DEC_SEQ
N_TOK = N_CTX + N_LAT
DEPTH = 4
EPS = 1e-6
GRID_W = 64
ROPE_THETA = 10000.0

GLA_HEADS, GLA_DK, GLA_DV, GLA_RANK, GLA_CHUNK, GLA_TAU = 4, 64, 128, 16, 128, 16.0
QK_W = GLA_HEADS * GLA_DK
V_W = GLA_HEADS * GLA_DV
GMLP_GROUPS, GMLP_DIM, GMLP_CHUNK = 4, 128, 128
GMLP_W = GMLP_GROUPS * GMLP_DIM
C_Q, C_K, C_V, C_G, C_A, C_U, C_VG = 0, 256, 512, 1024, 1536, 1664, 2176
EVEN_SPLIT = 1568
EVEN_W1 = C_U
EVEN_PACK = 2688

ATT_HEADS, ATT_KV, HD = 8, 2, 128
ATT_G = ATT_HEADS // ATT_KV
Q_W = ATT_HEADS * HD
KV_W = ATT_KV * HD

MOE_GROUPS, MOE_PER_GROUP = 4, 8
N_EXP = MOE_GROUPS * MOE_PER_GROUP
D_EXP = D // 4
NEG = -1e30

MIB = 1024 * 1024
V7X_VMEM_MIB = 64
LANES, SUBLANES = 128, 8


def _cp(*sem, vmem_mib=32):
    assert vmem_mib < V7X_VMEM_MIB
    return pltpu.CompilerParams(dimension_semantics=sem, vmem_limit_bytes=vmem_mib * MIB)


def _dot(a, b):
    return jnp.dot(a.astype(BF16), b.astype(BF16), preferred_element_type=F32)


def _dot_nt(a, b):
    return lax.dot_general(a.astype(BF16), b.astype(BF16), (((1,), (1,)), ((), ())),
                           preferred_element_type=F32)


def _dot_tn(a, b):
    return lax.dot_general(a.astype(BF16), b.astype(BF16), (((0,), (0,)), ((), ())),
                           preferred_element_type=F32)


def _rms(x, g):
    return x * lax.rsqrt(jnp.mean(x * x, axis=-1, keepdims=True) + EPS) * g


def _silu(x):
    return x * jax.nn.sigmoid(x)


def _gelu(x):
    return 0.5 * x * (1.0 + jnp.tanh(np.sqrt(2.0 / np.pi).astype(np.float32) * (x + 0.044715 * (x * x * x))))


def _log_sigmoid(z):
    return jnp.minimum(z, 0.0) - jnp.log(1.0 + jnp.exp(-jnp.abs(z)))


assert D == LANES * SUBLANES


def _rows_to_tiles(ref, x):
    rows = x.shape[0]
    for j in range(SUBLANES):
        ref[pl.ds(j, rows, stride=SUBLANES), :] = x[:, j * LANES:(j + 1) * LANES]


def _tiles_to_rows(ref, rows):
    return jnp.concatenate([ref[pl.ds(j, rows, stride=SUBLANES), :] for j in range(SUBLANES)], axis=1)


def _tile_of(ref, row8):
    return ref.at[pl.ds(pl.multiple_of(row8, SUBLANES), SUBLANES), :]


def _resident(shape, index_map):
    return pl.BlockSpec(shape, index_map, pipeline_mode=pl.Buffered(1))


def _split_bf16(x):
    hi = x.astype(BF16)
    return hi, (x - hi.astype(F32)).astype(BF16)


def _mod_kernel(cond_ref, w_ref, b_ref, o_ref):
    s_hi, s_lo = _split_bf16(_silu(cond_ref[...]))
    w_hi, w_lo = _split_bf16(w_ref[...])
    dot = functools.partial(jnp.dot, preferred_element_type=F32)
    both = dot(jnp.concatenate([s_hi, s_lo], axis=0), w_hi)
    o_ref[...] = both[0:8] + both[8:16] + dot(s_hi, w_lo) + b_ref[...]


def _modulation(cond8, w_mod, b_mod):
    tn = 2048
    out = pl.pallas_call(
        _mod_kernel,
        grid=(DEPTH, 6 * D // tn),
        in_specs=[
            pl.BlockSpec((8, D), lambda l, j: (0, 0)),
            pl.BlockSpec((None, D, tn), lambda l, j: (l, 0, j)),
            pl.BlockSpec((None, 1, tn), lambda l, j: (l, 0, j)),
        ],
        out_specs=pl.BlockSpec((None, 8, tn), lambda l, j: (l, 0, j)),
        out_shape=jax.ShapeDtypeStruct((DEPTH, 8, 6 * D), F32),
        compiler_params=_cp("arbitrary", "arbitrary"),
        name="adaln_mod",
    )(cond8, w_mod, b_mod.reshape(DEPTH, 1, 6 * D))
    return out.reshape(DEPTH, 8, 6, D)


def _even_kernel(x_ref, mod_ref, n1g_ref, win1_ref, win2_ref, wgu_ref, bgu_ref, glag_ref, ws_ref, bs_ref,
                 wout_ref, s0_ref, xo_ref, st_ref, proj, la, o_f, o_b, st_scr, *, T):
    n_chunks = T // GLA_CHUNK
    shift, scale, gate = mod_ref[0:1, :], mod_ref[1:2, :], mod_ref[2:3, :]
    RB = 128
    PROJ_RB = 256

    def proj_body(r, carry):
        r0 = pl.multiple_of(r * PROJ_RB, PROJ_RB)
        h = _rms(x_ref[pl.ds(r0, PROJ_RB), :], n1g_ref[...]) * (1.0 + scale) + shift
        hb = h.astype(BF16)
        p = _dot(hb, win1_ref[...])
        proj[pl.ds(r0, PROJ_RB), 0:EVEN_W1] = p
        proj[pl.ds(r0, PROJ_RB), EVEN_W1:] = _dot(hb, win2_ref[...])
        z = _dot(p[:, C_A:C_A + 128], wgu_ref[...]) + bgu_ref[...]
        la[pl.ds(r0, PROJ_RB), :] = _log_sigmoid(z) * (1.0 / GLA_TAU)
        return carry

    lax.fori_loop(0, T // PROJ_RB, proj_body, 0)

    st_scr[0] = s0_ref[0].T
    st_scr[1] = s0_ref[1].T

    ci = lax.broadcasted_iota(jnp.int32, (GLA_CHUNK, GLA_CHUNK), 0)
    cj = lax.broadcasted_iota(jnp.int32, (GLA_CHUNK, GLA_CHUNK), 1)
    tri = (jnp.where(ci >= cj, 1.0, 0.0).astype(BF16), jnp.where(ci <= cj, 1.0, 0.0).astype(BF16))
    ai = lax.broadcasted_iota(jnp.int32, (GLA_HEADS * GLA_CHUNK, GLA_CHUNK), 0) % GLA_CHUNK
    aj = lax.broadcasted_iota(jnp.int32, (GLA_HEADS * GLA_CHUNK, GLA_CHUNK), 1)
    amask = (ai >= aj, ai <= aj)
    lane_head = lax.broadcasted_iota(jnp.int32, (1, QK_W), 1) // GLA_DK
    hmask = [jnp.where(lane_head == h, 1.0, 0.0) for h in range(GLA_HEADS)]

    def chunk_body(i, carry):
        for d in range(2):
            c = i if d == 0 else n_chunks - 1 - i
            r0 = pl.multiple_of(c * GLA_CHUNK, GLA_CHUNK)
            q = proj[pl.ds(r0, GLA_CHUNK), C_Q:C_Q + QK_W] * (GLA_DK ** -0.5)
            k = proj[pl.ds(r0, GLA_CHUNK), C_K:C_K + QK_W]
            v = proj[pl.ds(r0, GLA_CHUNK), C_V:C_V + V_W]
            lac = la[pl.ds(r0, GLA_CHUNK), d * QK_W:(d + 1) * QK_W]
            hi = lac.astype(BF16)
            lo = (lac - hi.astype(F32)).astype(BF16)
            b = (jnp.dot(tri[d], hi, preferred_element_type=F32)
                 + jnp.dot(tri[d], lo, preferred_element_type=F32))
            bend = b[GLA_CHUNK - 1:GLA_CHUNK, :] if d == 0 else b[0:1, :]
            qe = q * jnp.exp(b)
            ke = k * jnp.exp(-b)
            kd = k * jnp.exp(bend - b)
            st = st_scr[d]
            qstack = jnp.concatenate([qe * hmask[h] for h in range(GLA_HEADS)], axis=0).astype(BF16)
            att = jnp.where(amask[d], _dot_nt(qstack, ke), 0.0)
            inter = _dot_nt(qstack, st)
            outs = []
            for h in range(GLA_HEADS):
                rows = slice(h * GLA_CHUNK, (h + 1) * GLA_CHUNK)
                outs.append(_dot(att[rows], v[:, h * GLA_DV:(h + 1) * GLA_DV]) + inter[rows])
            o = jnp.concatenate(outs, axis=1)
            if d == 0:
                o_f[pl.ds(r0, GLA_CHUNK), :] = o
            else:
                o_b[pl.ds(r0, GLA_CHUNK), :] = o
            vstack = jnp.concatenate([v[:, h * GLA_DV:(h + 1) * GLA_DV] for h in range(GLA_HEADS)], axis=0)
            kstack = jnp.concatenate([kd * hmask[h] for h in range(GLA_HEADS)], axis=0)
            st_scr[d] = st * jnp.exp(bend) + _dot_tn(vstack, kstack)
        return carry

    lax.fori_loop(0, n_chunks, chunk_body, 0, unroll=2)
    st_ref[0] = st_scr[0].T
    st_ref[1] = st_scr[1].T

    def out_body(r, carry):
        r0 = pl.multiple_of(r * RB, RB)
        osum = o_f[pl.ds(r0, RB), :] + o_b[pl.ds(r0, RB), :]
        g = proj[pl.ds(r0, RB), C_G:C_G + V_W]
        u = proj[pl.ds(r0, RB), C_U:C_U + GMLP_W]
        vg = _gelu(proj[pl.ds(r0, RB), C_VG:C_VG + GMLP_W])
        parts = []
        for h in range(GLA_HEADS):
            oh = osum[:, h * GLA_DV:(h + 1) * GLA_DV]
            parts.append(_rms(oh, glag_ref[...]) * _silu(g[:, h * GLA_DV:(h + 1) * GLA_DV]))
        for gi in range(GMLP_GROUPS):
            vc = vg[:, gi * GMLP_DIM:(gi + 1) * GMLP_DIM]
            vc = vc - jnp.mean(vc, axis=-1, keepdims=True)
            vn = vc * lax.rsqrt(jnp.mean(vc * vc, axis=-1, keepdims=True) + EPS)
            sg = _dot(ws_ref[gi], vn) + bs_ref[:, gi:gi + 1]
            parts.append(_gelu(u[:, gi * GMLP_DIM:(gi + 1) * GMLP_DIM]) * sg)
        mix = jnp.concatenate(parts, axis=1)
        y = _dot(mix, wout_ref[...])
        xo_ref[pl.ds(r0, RB), :] = x_ref[pl.ds(r0, RB), :] + gate * y
        return carry

    lax.fori_loop(0, T // RB, out_body, 0)


def _even_mixer(x_all, mod_l, n1g, win1, win2, wgu, bgu, glag, ws, bs, wout, s0, *, latent, x_first=None):
    if latent:
        T, nseq, blk0 = DEC_SEQ, DEC_BATCH, N_CTX // DEC_SEQ
        cond = lambda i: 1 + i
        s0_spec = pl.BlockSpec((None, 2, QK_W, GLA_DV), lambda i: (i, 0, 0, 0))
    else:
        T, nseq, blk0 = SEQ, BATCH, 0
        cond = lambda i: 0
        s0_spec = pl.BlockSpec((None, 2, QK_W, GLA_DV), lambda i: (0, 0, 0, 0))
    const2 = lambda i: (0, 0)
    body = functools.partial(_even_kernel, T=T)
    x_spec = pl.BlockSpec((T, D), lambda i: (blk0 + i, 0))
    if x_first is None:
        lead_specs, lead_args, aliases = [x_spec], (x_all,), {0: 0}
    elif x_all is None:
        lead_specs, lead_args, aliases = [pl.BlockSpec((T, D), lambda i: (i, 0))], (x_first,), {}
    else:
        lead_specs = [pl.BlockSpec(memory_space=pl.ANY), pl.BlockSpec((T, D), lambda i: (i, 0))]
        lead_args, aliases = (x_all, x_first), {0: 0}
        body = lambda dst_ref, *refs: _even_kernel(*refs, T=T)
    x_new, states = pl.pallas_call(
        body,
        grid=(nseq,),
        in_specs=lead_specs + [
            pl.BlockSpec((None, 6, D), lambda i: (cond(i), 0, 0)),
            _resident((1, D), const2),
            _resident((D, EVEN_W1), const2),
            _resident((D, EVEN_PACK - EVEN_W1), const2),
            _resident((128, 2 * QK_W), const2),
            _resident((1, 2 * QK_W), const2),
            _resident((1, GLA_DV), const2),
            _resident((GMLP_GROUPS, GMLP_CHUNK, GMLP_CHUNK), lambda i: (0, 0, 0)),
            _resident((GMLP_CHUNK, GMLP_GROUPS), const2),
            _resident((D, D), const2),
            s0_spec,
        ],
        out_specs=[
            x_spec,
            pl.BlockSpec((None, 2, QK_W, GLA_DV), lambda i: (i, 0, 0, 0)),
        ],
        out_shape=[
            jax.ShapeDtypeStruct((N_TOK, D), F32),
            jax.ShapeDtypeStruct((nseq, 2, QK_W, GLA_DV), F32),
        ],
        scratch_shapes=[
            pltpu.VMEM((T, EVEN_PACK), F32),
            pltpu.VMEM((T, 2 * QK_W), F32),
            pltpu.VMEM((T, V_W), F32),
            pltpu.VMEM((T, V_W), F32),
            pltpu.VMEM((2, GLA_DV, QK_W), F32),
        ],
        input_output_aliases=aliases,
        compiler_params=_cp("arbitrary", vmem_mib=48 if latent else 32),
        name="even_mixer_latent" if latent else "even_mixer_context",
    )(*lead_args, mod_l, n1g, win1, win2, wgu, bgu, glag, ws, bs, wout, s0)
    return x_new, states


QKV_TB = 512


def _qkv_kernel(x_ref, mod_ref, n1g_ref, win_ref, gq_ref, gk_ref, cos_ref, sin_ref, q_ref, k_ref, v_ref,
                ck_ref, cv_ref):
    shift, scale = mod_ref[0:1, :], mod_ref[1:2, :]
    h = _rms(x_ref[...], n1g_ref[...]) * (1.0 + scale) + shift
    p = _dot(h, win_ref[...])
    cos, sin = cos_ref[...], sin_ref[...]
    even_lane = lax.broadcasted_iota(jnp.int32, (1, HD), 1) % 2 == 0

    def rope(xn):
        swapped = jnp.where(even_lane, pltpu.roll(xn, HD - 1, axis=1), pltpu.roll(xn, 1, axis=1))
        return xn * cos + swapped * sin

    def emit(rotate, to_cache):
        for hh in range(ATT_HEADS):
            qn = _rms(p[:, hh * HD:(hh + 1) * HD], gq_ref[...])
            q_ref[:, hh * HD:(hh + 1) * HD] = (rotate(qn) * (HD ** -0.5)).astype(BF16)
        for hh in range(ATT_KV):
            kn = rotate(_rms(p[:, Q_W + hh * HD:Q_W + (hh + 1) * HD], gk_ref[...]))
            k_ref[:, hh * HD:(hh + 1) * HD] = kn
            if to_cache:
                for s in range(QKV_TB // SEQ):
                    ck_ref[s, pl.ds(hh, SEQ, stride=ATT_KV), :] = kn[s * SEQ:(s + 1) * SEQ]
                    cv_ref[s, pl.ds(hh, SEQ, stride=ATT_KV), :] = p[s * SEQ:(s + 1) * SEQ,
                                                                    Q_W + KV_W + hh * HD:Q_W + KV_W + (hh + 1) * HD]

    is_latent = pl.program_id(0) >= N_CTX // QKV_TB

    @pl.when(is_latent)
    def _():
        emit(rope, False)

    @pl.when(jnp.logical_not(is_latent))
    def _():
        emit(lambda xn: xn, True)

    v_ref[...] = p[:, Q_W + KV_W:]


def _qkv(x_all, mod_l, n1g, win, gq, gk, cos_tab, sin_tab, layer_i, caches=None):
    nb_ctx = N_CTX // QKV_TB
    per_seq = DEC_SEQ // QKV_TB
    cond = lambda i: jnp.where(i < nb_ctx, 0, 1 + (i - nb_ctx) // per_seq)
    tab = lambda i: jnp.where(i < nb_ctx, 0, 1 + (i - nb_ctx) % per_seq)
    const2 = lambda i: (0, 0)
    cache_spec = pl.BlockSpec((QKV_TB // SEQ, None, SEQ * ATT_KV, HD),
                              lambda i: (jnp.minimum(i, nb_ctx - 1), layer_i, 0, 0))
    cache_shape = jax.ShapeDtypeStruct((BATCH, DEPTH // 2, SEQ * ATT_KV, HD), F32)
    if caches is None:
        body, lead_specs, lead_args, aliases = _qkv_kernel, [], (), {}
    else:
        body = lambda ck_in, cv_in, *refs: _qkv_kernel(*refs)
        lead_specs = [pl.BlockSpec(memory_space=pl.ANY)] * 2
        lead_args, aliases = tuple(caches), {0: 3, 1: 4}
    return pl.pallas_call(
        body,
        grid=(N_TOK // QKV_TB,),
        in_specs=lead_specs + [
            pl.BlockSpec((QKV_TB, D), lambda i: (i, 0)),
            pl.BlockSpec((None, 6, D), lambda i: (cond(i), 0, 0)),
            _resident((1, D), const2),
            _resident((D, Q_W + 2 * KV_W), const2),
            _resident((1, HD), const2),
            _resident((1, HD), const2),
            pl.BlockSpec((None, QKV_TB, HD), lambda i: (tab(i), 0, 0)),
            pl.BlockSpec((None, QKV_TB, HD), lambda i: (tab(i), 0, 0)),
        ],
        out_specs=[
            pl.BlockSpec((QKV_TB, Q_W), lambda i: (i, 0)),
            pl.BlockSpec((QKV_TB, KV_W), lambda i: (i, 0)),
            pl.BlockSpec((QKV_TB, KV_W), lambda i: (i, 0)),
            cache_spec,
            cache_spec,
        ],
        out_shape=[
            jax.ShapeDtypeStruct((N_TOK, Q_W), BF16),
            jax.ShapeDtypeStruct((N_TOK, KV_W), F32),
            jax.ShapeDtypeStruct((N_TOK, KV_W), F32),
            cache_shape,
            cache_shape,
        ],
        input_output_aliases=aliases,
        compiler_params=_cp("arbitrary"),
        name="odd_qkv",
    )(*lead_args, x_all, mod_l, n1g, win, gq, gk, cos_tab, sin_tab)


ATT_TQ_LATENT = 512


def _attn_kernel(*refs, n_kv):
    q_ref = refs[0]
    kv_refs = refs[1:1 + 2 * n_kv]
    x_ref, mod_ref, wout_ref, xo_ref, att_scr = refs[1 + 2 * n_kv:]
    gate = mod_ref[2:3, :]
    def head(ref, kh):
        if ref.shape[1] == HD:
            return ref[pl.ds(kh, ref.shape[0] // ATT_KV, stride=ATT_KV), :].astype(BF16)
        return ref[:, kh * HD:(kh + 1) * HD].astype(BF16)

    for kh in range(ATT_KV):
        ks = [head(kv_refs[2 * s], kh) for s in range(n_kv)]
        vs = [jnp.concatenate([vh, jnp.ones_like(vh)], axis=1)
              for vh in (head(kv_refs[2 * s + 1], kh) for s in range(n_kv))]
        for g in range(ATT_G):
            hh = kh * ATT_G + g
            qh = q_ref[:, hh * HD:(hh + 1) * HD]
            ss = [_dot_nt(qh, kk) for kk in ks]
            m = ss[0].max(axis=-1, keepdims=True)
            for s in ss[1:]:
                m = jnp.maximum(m, s.max(axis=-1, keepdims=True))
            o = _dot(jnp.exp(ss[0] - m), vs[0])
            for s, vv in zip(ss[1:], vs[1:]):
                o = o + _dot(jnp.exp(s - m), vv)
            att_scr[:, hh * HD:(hh + 1) * HD] = o[:, :HD] / o[:, HD:HD + 1]
    y = _dot(att_scr[...], wout_ref[...])
    xo_ref[...] = x_ref[...] + gate * y


def _attention(x_all, mod_l, q, k, v, wout, cache_k=None, cache_v=None, layer_i=0):
    latent = cache_k is not None
    const2 = lambda *a: (0, 0)
    tq = ATT_TQ_LATENT if latent else SEQ
    if latent:
        nq = DEC_SEQ // tq
        row_blk = lambda b, j: (N_CTX // tq + b * nq + j, 0)
        grid = (DEC_BATCH, nq)
        kv_specs = [
            pl.BlockSpec((None, None, SEQ * ATT_KV, HD), lambda b, j: (b, layer_i, 0, 0)),
            pl.BlockSpec((None, None, SEQ * ATT_KV, HD), lambda b, j: (b, layer_i, 0, 0)),
            pl.BlockSpec((DEC_SEQ, KV_W), lambda b, j: (N_CTX // DEC_SEQ + b, 0)),
            pl.BlockSpec((DEC_SEQ, KV_W), lambda b, j: (N_CTX // DEC_SEQ + b, 0)),
        ]
        kv_args = (cache_k, cache_v, k, v)
        mod_spec = pl.BlockSpec((None, 6, D), lambda b, j: (1 + b, 0, 0))
        sem = ("arbitrary", "arbitrary")
        n_kv = 2
    else:
        row_blk = lambda i: (i, 0)
        grid = (BATCH,)
        kv_specs = [pl.BlockSpec((SEQ, KV_W), row_blk), pl.BlockSpec((SEQ, KV_W), row_blk)]
        kv_args = (k, v)
        mod_spec = pl.BlockSpec((None, 6, D), lambda i: (0, 0, 0))
        sem = ("arbitrary",)
        n_kv = 1
    n_in = 1 + len(kv_args)
    return pl.pallas_call(
        functools.partial(_attn_kernel, n_kv=n_kv),
        grid=grid,
        in_specs=[pl.BlockSpec((tq, Q_W), row_blk)] + kv_specs + [
            pl.BlockSpec((tq, D), row_blk),
            mod_spec,
            _resident((D, D), const2),
        ],
        out_specs=pl.BlockSpec((tq, D), row_blk),
        out_shape=jax.ShapeDtypeStruct((N_TOK, D), F32),
        scratch_shapes=[pltpu.VMEM((tq, Q_W), F32)],
        input_output_aliases={n_in: 0},
        compiler_params=_cp(*sem),
        name="attention_latent" if latent else "attention_context",
    )(q, *kv_args, x_all, mod_l, wout)


ROUTE_TB = 512
HALF_TOK = N_TOK // 2
M_E1, M_E2, M_G1, M_G2, M_R1, M_R2 = 0, 1, 2, 3, 4, 5


def _router_kernel(x_ref, mod_ref, n2g_ref, w2_ref, br_ref, h_ref, metat_ref, cnt_ref, run):
    @pl.when(pl.program_id(0) % (HALF_TOK // ROUTE_TB) == 0)
    def _():
        run[...] = jnp.zeros_like(run)

    shift, scale = mod_ref[3:4, :], mod_ref[4:5, :]
    h = _rms(x_ref[...], n2g_ref[...]) * (1.0 + scale) + shift
    _rows_to_tiles(h_ref, h)
    h_hi, h_lo = _split_bf16(h)
    dot = functools.partial(jnp.dot, preferred_element_type=F32)
    both = dot(jnp.concatenate([h_hi, h_lo], axis=0), w2_ref[...])
    logits = both[:ROUTE_TB, :128] + both[:ROUTE_TB, 128:] + both[ROUTE_TB:, :128] + br_ref[...]
    lane = lax.broadcasted_iota(jnp.int32, logits.shape, 1).astype(F32)
    big = 1e4

    def first_argmax(vals):
        m = vals.max(axis=-1, keepdims=True)
        return m, jnp.where(vals == m, lane, big).min(axis=-1, keepdims=True)

    gl = jnp.where((lane >= N_EXP) & (lane < N_EXP + MOE_GROUPS), logits, NEG)
    gmax, glane = first_argmax(gl)
    g_p = 1.0 / jnp.exp(gl - gmax).sum(axis=-1, keepdims=True)
    lo = (glane - N_EXP) * MOE_PER_GROUP
    el = jnp.where((lane >= lo) & (lane < lo + MOE_PER_GROUP), logits, NEG)
    m1, i1 = first_argmax(el)
    m2, i2 = first_argmax(jnp.where(lane == i1, NEG, el))
    t = jnp.exp(m2 - m1)
    w1 = 1.0 / (1.0 + t)
    sel1, sel2 = lane == i1, lane == i2
    onehot = jnp.where(sel1 | sel2, 1.0, 0.0)
    ri = lax.broadcasted_iota(jnp.int32, (ROUTE_TB, ROUTE_TB), 0)
    rj = lax.broadcasted_iota(jnp.int32, (ROUTE_TB, ROUTE_TB), 1)
    before = _dot(jnp.where(ri > rj, 1.0, 0.0), onehot) + run[...]
    r1 = jnp.where(sel1, before, 0.0).sum(axis=-1, keepdims=True)
    r2 = jnp.where(sel2, before, 0.0).sum(axis=-1, keepdims=True)
    run[...] += onehot.sum(axis=0, keepdims=True)
    cnt_ref[...] = run[...]
    meta = jnp.zeros_like(logits)
    for j, val in enumerate([i1, i2, w1 * g_p, (t * w1) * g_p, r1, r2]):
        meta = jnp.where(lane == j, val, meta)
    metat_ref[...] = meta.T[0:8, :]


def _router(x_all, mod_l, n2g, w2_all, br_all, layer):
    nb_ctx = N_CTX // ROUTE_TB
    per_seq = DEC_SEQ // ROUTE_TB
    cond = lambda i: jnp.where(i < nb_ctx, 0, 1 + (i - nb_ctx) // per_seq)
    const2 = lambda i: (0, 0)
    return pl.pallas_call(
        _router_kernel,
        grid=(N_TOK // ROUTE_TB,),
        in_specs=[
            pl.BlockSpec((ROUTE_TB, D), lambda i: (i, 0)),
            pl.BlockSpec((None, 6, D), lambda i: (cond(i), 0, 0)),
            _resident((1, D), const2),
            _resident((None, D, 256), lambda i: (layer, 0, 0)),
            _resident((None, 1, 128), lambda i: (layer, 0, 0)),
        ],
        out_specs=[
            pl.BlockSpec((ROUTE_TB * 8, 128), lambda i: (i, 0)),
            pl.BlockSpec((8, ROUTE_TB), lambda i: (0, i)),
            pl.BlockSpec((None, 1, 128), lambda i: (i // (HALF_TOK // ROUTE_TB), 0, 0)),
        ],
        out_shape=[
            jax.ShapeDtypeStruct((N_TOK * 8, 128), F32),
            jax.ShapeDtypeStruct((8, N_TOK), F32),
            jax.ShapeDtypeStruct((2, 1, 128), F32),
        ],
        scratch_shapes=[pltpu.VMEM((1, 128), F32)],
        compiler_params=_cp("arbitrary"),
        name="moe_router",
    )(x_all, mod_l, n2g, w2_all, br_all)


EXP_TM = 128
N_ASSIGN = 2 * N_TOK
N_GROUPS = 2 * N_EXP
MAX_TILES = N_ASSIGN // EXP_TM + N_GROUPS
N_SORTED = MAX_TILES * EXP_TM
ORDER_BLK = 4096
CODE_PLANE = 2 * HALF_TOK
CODE_MASK = 8 * CODE_PLANE - 1
DUMMY8 = HALF_TOK * 8


def _order_kernel(pos1_ref, pos2_ref, pad_lo_ref, pad_hi_ref, src_ref):
    i = pl.program_id(0)
    local = (i % (HALF_TOK // ORDER_BLK)) * ORDER_BLK

    def body(t, carry):
        src_ref[pos1_ref[t]] = (local + t) * 8
        src_ref[pos2_ref[t]] = (local + t + CODE_PLANE) * 8
        return carry

    lax.fori_loop(0, ORDER_BLK, body, 0, unroll=16)

    @pl.when(i == 0)
    def _():
        def group(g, carry):
            def pad(p, c):
                src_ref[p] = DUMMY8
                return c
            return lax.fori_loop(pad_lo_ref[g], pad_hi_ref[g], pad, carry)

        lax.fori_loop(0, N_GROUPS, group, 0)


def _order(pos, pad_lo, pad_hi):
    return pl.pallas_call(
        _order_kernel,
        grid=(N_TOK // ORDER_BLK,),
        in_specs=[
            pl.BlockSpec((ORDER_BLK,), lambda i: (i,), memory_space=pltpu.SMEM),
            pl.BlockSpec((ORDER_BLK,), lambda i: (N_TOK // ORDER_BLK + i,), memory_space=pltpu.SMEM),
            pl.BlockSpec(memory_space=pltpu.SMEM),
            pl.BlockSpec(memory_space=pltpu.SMEM),
        ],
        out_specs=pl.BlockSpec(memory_space=pltpu.SMEM),
        out_shape=jax.ShapeDtypeStruct((N_SORTED,), jnp.int32),
        compiler_params=_cp("arbitrary"),
        name="moe_order",
    )(pos, pos, pad_lo, pad_hi)


GATE_BLK = CODE_PLANE + HALF_TOK
ACC_TOK = HALF_TOK + 64
GATHER_GROUP, ACC_GROUP = 16, 8
TAIL_ROWS = 2 * EXP_TM + EXP_TM // 2


RES_TB = 256


EXP_PER_STEP = 2


def _experts_kernel(tile0_ref, ntile_ref, count_ref, src_ref, gs_ref, h_hbm, x_hbm, mod_ref, *rest, final):
    weights, rest = rest[:3 * EXP_PER_STEP], rest[3 * EXP_PER_STEP:]
    if final:
        fg_ref, *dst_hbm = rest[:3]
        rest = rest[3:]
    else:
        dst_hbm, rest = rest[:1], rest[1:]
    h_res, acc, xbuf, ybuf, wgb, wub, wdb, xin, xout, sem, in_sem, out_sem = rest
    first_group = pl.program_id(0) * EXP_PER_STEP
    half = first_group // N_EXP
    rows0 = pl.multiple_of(half * (HALF_TOK * 8), 8)

    @pl.when(first_group % N_EXP == 0)
    def _():
        cp = pltpu.make_async_copy(h_hbm.at[pl.ds(rows0, HALF_TOK * 8), :], h_res.at[pl.ds(0, HALF_TOK * 8), :], sem)
        cp.start()
        h_res[pl.ds(DUMMY8, 8), :] = jnp.zeros((8, 128), F32)
        xbuf[...] = jnp.zeros_like(xbuf)

        def zero(i, carry):
            acc[pl.ds(pl.multiple_of(i * 512, 512), 512), :] = jnp.zeros((512, 128), F32)
            return carry

        lax.fori_loop(0, ACC_TOK * 8 // 512, zero, 0)
        cp.wait()

    def run_group(group, wg_ref, wu_ref, wd_ref):
        n_tiles = ntile_ref[group]

        @pl.when(n_tiles > 0)
        def _():
            wgb[...] = wg_ref[...].astype(BF16)
            wub[...] = wu_ref[...].astype(BF16)
            wdb[...] = wd_ref[...].astype(BF16)

        row0 = tile0_ref[group] * EXP_TM
        row_end = row0 + count_ref[group]

        def process(base, rows):
            live = (jnp.clip(row_end - base, 0, rows) + GATHER_GROUP - 1) // GATHER_GROUP

            def gather(g, c):
                for i in range(GATHER_GROUP):
                    r = g * GATHER_GROUP + i
                    xbuf[pl.ds(pl.multiple_of(r * 8, 8), 8), :] = _tile_of(h_res, src_ref[base + r] & CODE_MASK)[...]
                return c

            lax.fori_loop(0, live, gather, 0)
            x = _tiles_to_rows(xbuf, rows).astype(BF16)
            hid = _silu(_dot(x, wgb[...])) * _dot(x, wub[...])
            _rows_to_tiles(ybuf, _dot(hid, wdb[...]))

            def accumulate(g, c):
                targets, values = [], []
                for i in range(ACC_GROUP):
                    r = g * ACC_GROUP + i
                    code = src_ref[base + r]
                    target = _tile_of(acc, code & CODE_MASK)
                    targets.append(target)
                    values.append(target[...] + gs_ref[code >> 3] * ybuf[pl.ds(pl.multiple_of(r * 8, 8), 8), :])
                for target, value in zip(targets, values):
                    target[...] = value
                return c

            lax.fori_loop(0, live * (GATHER_GROUP // ACC_GROUP), accumulate, 0)

        ends_in_triple = (n_tiles % 2 == 1) & (n_tiles >= 3)
        n_pairs = jnp.where(ends_in_triple, (n_tiles - 3) // 2, n_tiles // 2)

        def pair_body(j, carry):
            process(row0 + j * (2 * EXP_TM), 2 * EXP_TM)
            return carry

        lax.fori_loop(0, n_pairs, pair_body, 0)

        @pl.when(ends_in_triple)
        def _():
            base = row0 + (n_tiles - 3) * EXP_TM
            short = row_end - base <= TAIL_ROWS

            @pl.when(short)
            def _():
                process(base, TAIL_ROWS)

            @pl.when(jnp.logical_not(short))
            def _():
                process(base, 3 * EXP_TM)

        @pl.when(n_tiles == 1)
        def _():
            process(row0, EXP_TM)

    for sub in range(EXP_PER_STEP):
        run_group(first_group + sub, *weights[3 * sub:3 * sub + 3])
    expert = (first_group + EXP_PER_STEP - 1) % N_EXP

    def rows_of(first, blk):
        return pl.ds(pl.multiple_of(first + blk * RES_TB, RES_TB), RES_TB)

    def load_x(blk, slot):
        return pltpu.make_async_copy(x_hbm.at[rows_of(half * HALF_TOK, blk), :], xin.at[slot], in_sem.at[slot])

    def residual(blk, slot):
        cond = jnp.where(half == 0, 0, 1 + blk // (DEC_SEQ // RES_TB))
        gate = mod_ref[cond, 5:6, :]
        y = _tiles_to_rows(acc.at[pl.ds(pl.multiple_of(blk * (RES_TB * 8), RES_TB * 8), RES_TB * 8), :], RES_TB)
        x_new = xin[slot] + gate * y
        xout[slot] = _rms(x_new, fg_ref[...]) if final else x_new

    def epilogue(dst, first_row):
        def store_x(blk, slot):
            return pltpu.make_async_copy(xout.at[slot], dst.at[rows_of(first_row, blk), :], out_sem.at[slot])

        n_pairs = HALF_TOK // RES_TB // 2
        load_x(0, 0).start()

        def pair(p, carry):
            for slot in range(2):
                blk = 2 * p + slot
                if slot == 0:
                    load_x(blk + 1, 1).start()
                else:
                    @pl.when(p + 1 < n_pairs)
                    def _():
                        load_x(blk + 1, 0).start()
                load_x(blk, slot).wait()

                @pl.when(p > 0)
                def _():
                    store_x(blk - 2, slot).wait()

                residual(blk, slot)
                store_x(blk, slot).start()
            return carry

        lax.fori_loop(0, n_pairs, pair, 0)
        store_x(2 * n_pairs - 2, 0).wait()
        store_x(2 * n_pairs - 1, 1).wait()

    if final:
        for which in range(2):
            @pl.when((expert == N_EXP - 1) & (half == which))
            def _():
                epilogue(dst_hbm[which], 0)
    else:
        @pl.when(expert == N_EXP - 1)
        def _():
            epilogue(dst_hbm[0], half * HALF_TOK)


def _experts(tile0, n_tiles, counts, src, gs, h, x_all, mod_l, wg, wu, wd, layer, final_g=None):
    final = final_g is not None
    any_spec = pl.BlockSpec(memory_space=pl.ANY)
    extra_specs = [pl.BlockSpec((1, D), lambda g, t0, nt, cnt, src: (0, 0))] if final else []
    extra_args = (final_g,) if final else ()
    weight_specs, weight_args = [], []
    for sub in range(EXP_PER_STEP):
        wmap = lambda g, t0, nt, cnt, src, sub=sub: (layer, (g * EXP_PER_STEP + sub) % N_EXP, 0, 0)
        weight_specs += [pl.BlockSpec((None, None, D, D_EXP), wmap), pl.BlockSpec((None, None, D, D_EXP), wmap),
                         pl.BlockSpec((None, None, D_EXP, D), wmap)]
        weight_args += [wg, wu, wd]
    return pl.pallas_call(
        functools.partial(_experts_kernel, final=final),
        grid_spec=pltpu.PrefetchScalarGridSpec(
            num_scalar_prefetch=4,
            grid=(N_GROUPS // EXP_PER_STEP,),
            in_specs=[
                pl.BlockSpec((GATE_BLK,), lambda g, t0, nt, cnt, src: (g * EXP_PER_STEP // N_EXP,),
                             memory_space=pltpu.SMEM),
                pl.BlockSpec(memory_space=pl.ANY),
                pl.BlockSpec(memory_space=pl.ANY),
                pl.BlockSpec((8, 6, D), lambda g, t0, nt, cnt, src: (0, 0, 0)),
            ] + weight_specs + extra_specs,
            out_specs=[any_spec, any_spec] if final else any_spec,
            scratch_shapes=[
                pltpu.VMEM((ACC_TOK * 8, 128), F32),
                pltpu.VMEM((ACC_TOK * 8, 128), F32),
                pltpu.VMEM((3 * EXP_TM * 8, 128), F32),
                pltpu.VMEM((3 * EXP_TM * 8, 128), F32),
                pltpu.VMEM((D, D_EXP), BF16),
                pltpu.VMEM((D, D_EXP), BF16),
                pltpu.VMEM((D_EXP, D), BF16),
                pltpu.VMEM((2, RES_TB, D), F32),
                pltpu.VMEM((2, RES_TB, D), F32),
                pltpu.SemaphoreType.DMA,
                pltpu.SemaphoreType.DMA((2,)),
                pltpu.SemaphoreType.DMA((2,)),
            ],
        ),
        out_shape=([jax.ShapeDtypeStruct((HALF_TOK, D), F32)] * 2 if final
                   else jax.ShapeDtypeStruct((N_TOK, D), F32)),
        input_output_aliases={} if final else {6: 0},
        compiler_params=_cp("arbitrary", vmem_mib=56),
        name="moe_experts_final" if final else "moe_experts",
    )(tile0, n_tiles, counts, src, gs, h, x_all, mod_l, *weight_args, *extra_args)


def _moe(x_all, mod_l, n2g, w2_all, br_all, wg, wu, wd, layer, final_g=None):
    h, metat, cnt = _router(x_all, mod_l, n2g, w2_all, br_all, layer)
    counts = cnt[:, 0, :N_EXP].astype(jnp.int32).reshape(N_GROUPS)
    padded = (counts + EXP_TM - 1) // EXP_TM * EXP_TM
    ends = jnp.cumsum(padded)
    offs = ends - padded
    rec = metat.astype(jnp.int32)
    half = (jnp.arange(N_TOK, dtype=jnp.int32) // HALF_TOK)[None, :]
    group = rec[M_E1:M_E2 + 1] + N_EXP * half
    is_group = group[None] == jnp.arange(N_GROUPS, dtype=jnp.int32)[:, None, None]
    pos = jnp.sum(jnp.where(is_group, offs[:, None, None], 0), axis=0) + rec[M_R1:M_R2 + 1]
    live_end = offs + (counts + GATHER_GROUP - 1) // GATHER_GROUP * GATHER_GROUP
    src = _order(pos.reshape(N_ASSIGN), offs + counts, live_end)
    g12 = metat[M_G1:M_G2 + 1].reshape(2, 2, HALF_TOK)
    gates = jnp.concatenate([g12[0], jnp.zeros((2, CODE_PLANE - HALF_TOK), F32), g12[1]], axis=1)
    return _experts(offs // EXP_TM, padded // EXP_TM, counts, src, gates.reshape(2 * GATE_BLK), h, x_all, mod_l,
                    wg, wu, wd, layer, final_g)


def _rope_tables():
    pos = jnp.arange(DEC_SEQ)
    row = (pos // GRID_W).astype(F32)
    col = (pos % GRID_W).astype(F32)
    n_freq = HD // 4
    inv = ROPE_THETA ** (-jnp.arange(n_freq, dtype=F32) / n_freq)
    ang = jnp.concatenate([row[:, None] * inv, col[:, None] * inv], axis=-1)
    cos = jnp.repeat(jnp.cos(ang), 2, axis=-1)
    sin = jnp.repeat(jnp.sin(ang), 2, axis=-1) * jnp.tile(jnp.array([-1.0, 1.0], F32), HD // 2)
    nblk = DEC_SEQ // QKV_TB
    cos_tab = jnp.concatenate([jnp.ones((1, QKV_TB, HD), F32), cos.reshape(nblk, QKV_TB, HD)], axis=0)
    sin_tab = jnp.concatenate([jnp.zeros((1, QKV_TB, HD), F32), sin.reshape(nblk, QKV_TB, HD)], axis=0)
    return cos_tab, sin_tab


def kernel(x_prompt, x_sample, state_gla, cache_k, cache_v, c, c_ctx, w_mod, b_mod, norm1_g, norm2_g,
           w_in_even, w_gate_up, b_gate_up, gla_norm_g, w_spatial, b_spatial, w_out_even,
           w_in_odd, q_norm_g, k_norm_g, w_out_odd, w_router_group, b_router_group,
           w_router_expert, b_router_expert, w_exp_gate, w_exp_up, w_exp_down, final_norm_g):
    x_all = None
    cond8 = jnp.concatenate([c_ctx[None], c, jnp.zeros((3, D), F32)], axis=0)
    mod = _modulation(cond8, w_mod, b_mod)
    cos_tab, sin_tab = _rope_tables()
    zero_state = jnp.zeros((1, 2, QK_W, GLA_DV), F32)
    state_in = state_gla.reshape(DEC_BATCH, -1, 2, QK_W, GLA_DV)
    cache_k2 = cache_k.reshape(DEC_BATCH, -1, SEQ * ATT_KV, HD)
    cache_v2 = cache_v.reshape(DEC_BATCH, -1, SEQ * ATT_KV, HD)

    lane_pad = 128 - N_EXP - MOE_GROUPS
    wr_all = jnp.concatenate([w_router_expert, w_router_group, jnp.zeros((DEPTH, D, lane_pad), F32)], axis=2)
    w2_all = jnp.concatenate(_split_bf16(wr_all), axis=2)
    br_all = jnp.concatenate([b_router_expert, b_router_group, jnp.zeros((DEPTH, lane_pad), F32)], axis=1)[:, None]
    gla_states, caches = [], None
    for l in range(DEPTH):
        i = l // 2
        n1g = norm1_g[l][None]
        if l % 2 == 0:
            w = w_in_even[i]
            win1, win2 = w[:, :EVEN_W1].astype(BF16), w[:, EVEN_SPLIT:].astype(BF16)
            wgu = jnp.zeros((128, 2 * QK_W), F32)
            wgu = wgu.at[0:GLA_RANK, 0:QK_W].set(w_gate_up[i, 0])
            wgu = wgu.at[GLA_RANK:2 * GLA_RANK, QK_W:].set(w_gate_up[i, 1]).astype(BF16)
            bgu = b_gate_up[i].reshape(1, 2 * QK_W)
            args = (mod[l], n1g, win1, win2, wgu, bgu, gla_norm_g[i][None], w_spatial[i].astype(BF16),
                    b_spatial[i].T, w_out_even[i].astype(BF16))
            first = l == 0
            x_all, st = _even_mixer(x_all, *args, zero_state, latent=False,
                                    x_first=x_prompt.reshape(N_CTX, D) if first else None)
            gla_states.append(st)
            x_all, _ = _even_mixer(x_all, *args, state_in[:, i], latent=True,
                                   x_first=x_sample.reshape(N_LAT, D) if first else None)
        else:
            q, k, v, *caches = _qkv(x_all, mod[l], n1g, w_in_odd[i].astype(BF16), q_norm_g[i][None],
                                    k_norm_g[i][None], cos_tab, sin_tab, i, caches)
            wout = w_out_odd[i].astype(BF16)
            x_all = _attention(x_all, mod[l], q, k, v, wout)
            x_all = _attention(x_all, mod[l], q, k, v, wout, cache_k2, cache_v2, layer_i=i)
        x_all = _moe(x_all, mod[l], norm2_g[l][None], w2_all, br_all, w_exp_gate, w_exp_up, w_exp_down, l,
                     final_norm_g[None] if l == DEPTH - 1 else None)

    y_prompt = x_all[0].reshape(BATCH, SEQ, D)
    y_sample = x_all[1].reshape(DEC_BATCH, DEC_SEQ, D)
    new_state = jnp.stack(gla_states, axis=1).reshape(BATCH, -1, 2, GLA_HEADS, GLA_DK, GLA_DV)
    new_k, new_v = (a.reshape(BATCH, DEPTH // 2, SEQ, ATT_KV, HD) for a in caches)
    return (y_prompt, y_sample, new_state, new_k, new_v)
```
